```python
import math
import jax, jax.numpy as jnp
from jax import lax
import numpy as np

D_MODEL = 1024
BATCH = 2
SEQ = 8192
DEPTH = 2

BLOCK = 128
A_WIDTH = D_MODEL // 2
A_HEAD_DIM = 64
A_HEADS = A_WIDTH // A_HEAD_DIM
A_KV_HEADS = A_HEADS // 4
WINDOW = 128
B_WIDTH = D_MODEL - A_WIDTH
SSM_GROUP = 16
SSM_GROUPS = B_WIDTH // SSM_GROUP
SSM_STATE = 64
DT_MIN = 0.001
DT_MAX = 0.1
C_WIDTH = D_MODEL
C_HEADS = A_HEADS
C_HEAD_DIM = C_WIDTH // C_HEADS
C_KV_HEADS = C_HEADS // 4
IDX_HEADS = 8
IDX_DIM = 64
TOPK_MAX = 256
NUM_BUCKETS = 32
REL_MAX_DIST = 1024
EPS = 1e-6
NEG_INF = -1e30

EVEN_SPLITS = (A_WIDTH, A_KV_HEADS * A_HEAD_DIM, A_KV_HEADS * A_HEAD_DIM, A_WIDTH, B_WIDTH, B_WIDTH)
ODD_SPLITS = (C_WIDTH, C_KV_HEADS * C_HEAD_DIM, C_KV_HEADS * C_HEAD_DIM, C_WIDTH,
              IDX_HEADS * IDX_DIM, IDX_DIM, IDX_HEADS)
N_EVEN = (DEPTH + 1) // 2
N_ODD = DEPTH // 2

kernel_name = 'hybrid_swa_s5_dsa_block'


def rms_norm(x, g):
    xf = x.astype(jnp.float32)
    y = xf * lax.rsqrt(jnp.mean(xf * xf, axis=-1, keepdims=True) + EPS)
    return (y * g.astype(jnp.float32)).astype(x.dtype)


def split_cols(z, sizes):
    idx = np.cumsum(sizes)[:-1].tolist()
    return jnp.split(z, idx, axis=-1)


def t5_bucket(dist):
    n = jnp.maximum(dist, 0)
    max_exact = NUM_BUCKETS // 2
    nf = jnp.maximum(n, 1).astype(jnp.float32)
    large = max_exact + (jnp.log(nf / max_exact) / math.log(REL_MAX_DIST / max_exact)
                         * (NUM_BUCKETS - max_exact)).astype(jnp.int32)
    large = jnp.minimum(large, NUM_BUCKETS - 1)
    return jnp.where(n < max_exact, n, large)


def swa_sink_attention(q, k, v, sinks, rel_bias):
    bsz, L, H, Dh = q.shape
    hkv = k.shape[2]
    G = H // hkv
    nb = L // BLOCK
    qb = q.reshape(bsz, nb, BLOCK, hkv, G, Dh)
    kb = k.reshape(bsz, nb, BLOCK, hkv, Dh)
    vb = v.reshape(bsz, nb, BLOCK, hkv, Dh)
    k_band = jnp.concatenate([jnp.concatenate([jnp.zeros_like(kb[:, :1]), kb[:, :-1]], axis=1), kb], axis=2)
    v_band = jnp.concatenate([jnp.concatenate([jnp.zeros_like(vb[:, :1]), vb[:, :-1]], axis=1), vb], axis=2)
    logits = jnp.einsum('bnqhgd,bnkhd->bnhgqk', qb, k_band).astype(jnp.float32) * (Dh ** -0.5)
    i = jnp.arange(BLOCK, dtype=jnp.int32)[:, None]
    j = jnp.arange(2 * BLOCK, dtype=jnp.int32)[None, :]
    d = i + BLOCK - j
    in_window = (d >= 0) & (d < WINDOW)
    first = (jnp.arange(nb) == 0)[:, None, None] & (j < BLOCK)[None]
    mask = in_window[None] & ~first
    bias = rel_bias[t5_bucket(d)].astype(jnp.float32)
    bias = bias.transpose(2, 0, 1).reshape(hkv, G, BLOCK, 2 * BLOCK)
    logits = jnp.where(mask[None, :, None, None], logits + bias, NEG_INF)
    sink = jnp.broadcast_to(sinks.astype(jnp.float32).reshape(1, 1, hkv, G, 1, 1), logits.shape[:-1] + (1,))
    p = jax.nn.softmax(jnp.concatenate([logits, sink], axis=-1), axis=-1)[..., :-1]
    out = jnp.einsum('bnhgqk,bnkhd->bnqhgd', p.astype(v.dtype), v_band)
    return out.reshape(bsz, L, H * Dh)


def ssm_combine(e1, e2):
    a1r, a1i, b1r, b1i = e1
    a2r, a2i, b2r, b2i = e2
    ar = a1r * a2r - a1i * a2i
    ai = a1r * a2i + a1i * a2r
    br = a2r * b1r - a2i * b1i + b2r
    bi = a2r * b1i + a2i * b1r + b2i
    return (ar, ai, br, bi)


def s5_mixer(u, log_dt, a_re, a_im, b_re, b_im, c_re, c_im, d_skip, glu_w, glu_b):
    bsz, L, W = u.shape
    G, P = a_re.shape
    ug = u.reshape(bsz, L, G, W // G)
    dt = jnp.exp(log_dt)[:, None]
    mag = jnp.exp(a_re * dt)
    ang = a_im * dt
    ab_re = mag * jnp.cos(ang)
    ab_im = mag * jnp.sin(ang)
    den = a_re * a_re + a_im * a_im
    n_re = ab_re - 1.0
    n_im = ab_im
    f_re = (n_re * a_re + n_im * a_im) / den
    f_im = (n_im * a_re - n_re * a_im) / den
    bb_re = f_re[..., None] * b_re - f_im[..., None] * b_im
    bb_im = f_re[..., None] * b_im + f_im[..., None] * b_re
    bu_re = jnp.einsum('blgh,gph->blgp', ug, bb_re)
    bu_im = jnp.einsum('blgh,gph->blgp', ug, bb_im)
    at_re = jnp.broadcast_to(ab_re[None, None], (1, L, G, P))
    at_im = jnp.broadcast_to(ab_im[None, None], (1, L, G, P))
    _, _, x_re, x_im = lax.associative_scan(ssm_combine, (at_re, at_im, bu_re, bu_im), axis=1)
    y = (jnp.einsum('blgp,ghp->blgh', x_re, c_re) - jnp.einsum('blgp,ghp->blgh', x_im, c_im)
         + d_skip * ug)
    y = jax.nn.gelu(y.reshape(bsz, L, W))
    h = y @ glu_w + glu_b
    return h[..., :W] * jax.nn.sigmoid(h[..., W:])


def dsa_attention(q, k, v, qi, ki, wi, rel_bias):
    bsz, L, H, Dh = q.shape
    hkv = k.shape[2]
    G = H // hkv
    nb = L // BLOCK
    topk = min(TOPK_MAX, L // 4)
    qb = q.reshape(bsz, nb, BLOCK, hkv, G, Dh).transpose(1, 0, 2, 3, 4, 5)
    qib = qi.reshape(bsz, nb, BLOCK, IDX_HEADS, IDX_DIM).transpose(1, 0, 2, 3, 4)
    wib = wi.reshape(bsz, nb, BLOCK, IDX_HEADS).transpose(1, 0, 2, 3)
    starts = jnp.arange(nb, dtype=jnp.int32) * BLOCK
    key_pos = jnp.arange(L, dtype=jnp.int32)

    def one_block(args):
        q_blk, qi_blk, w_blk, start = args
        t = start + jnp.arange(BLOCK, dtype=jnp.int32)
        rel = jax.nn.relu(jnp.einsum('bqhd,bsd->bqhs', qi_blk, ki).astype(jnp.float32) * (IDX_DIM ** -0.5))
        score = jnp.einsum('bqhs,bqh->bqs', rel, w_blk.astype(jnp.float32))
        score = jnp.where(key_pos[None, None, :] <= t[None, :, None], score, NEG_INF)
        _, idx = lax.top_k(score, topk)
        valid = idx <= t[None, :, None]
        k_sel = jax.vmap(lambda kk, ii: kk[ii])(k, idx)
        v_sel = jax.vmap(lambda vv, ii: vv[ii])(v, idx)
        logits = jnp.einsum('bqhgd,bqkhd->bqhgk', q_blk, k_sel).astype(jnp.float32) * (Dh ** -0.5)
        bias = rel_bias[t5_bucket(t[None, :, None] - idx)].astype(jnp.float32)
        bias = bias.reshape(bsz, BLOCK, topk, hkv, G).transpose(0, 1, 3, 4, 2)
        logits = jnp.where(valid[:, :, None, None, :], logits + bias, NEG_INF)
        p = jax.nn.softmax(logits, axis=-1).astype(v.dtype)
        return jnp.einsum('bqhgk,bqkhd->bqhgd', p, v_sel)

    out = lax.map(one_block, (qb, qib, wib, starts))
    return out.transpose(1, 0, 2, 3, 4, 5).reshape(bsz, L, H * Dh)


def even_layer(hn, rel_bias, w_in, w_out, q_g, k_g, sinks, log_dt, a_re, a_im, b_re, b_im,
               c_re, c_im, d_skip, glu_w, glu_b):
    bsz, L, _ = hn.shape
    q, k, v, gate_a, u, gate_b = split_cols(hn @ w_in, EVEN_SPLITS)
    q = rms_norm(q.reshape(bsz, L, A_HEADS, A_HEAD_DIM), q_g)
    k = rms_norm(k.reshape(bsz, L, A_KV_HEADS, A_HEAD_DIM), k_g)
    v = v.reshape(bsz, L, A_KV_HEADS, A_HEAD_DIM)
    att = swa_sink_attention(q, k, v, sinks, rel_bias) * jax.nn.silu(gate_a)
    ssm = s5_mixer(u, log_dt, a_re, a_im, b_re, b_im, c_re, c_im, d_skip, glu_w, glu_b) * jax.nn.silu(gate_b)
    return jnp.concatenate([att, ssm], axis=-1) @ w_out


def odd_layer(hn, rel_bias, w_in, w_out, q_g, k_g):
    bsz, L, _ = hn.shape
    q, k, v, gate, qi, ki, wi = split_cols(hn @ w_in, ODD_SPLITS)
    q = rms_norm(q.reshape(bsz, L, C_HEADS, C_HEAD_DIM), q_g)
    k = rms_norm(k.reshape(bsz, L, C_KV_HEADS, C_HEAD_DIM), k_g)
    v = v.reshape(bsz, L, C_KV_HEADS, C_HEAD_DIM)
    qi = qi.reshape(bsz, L, IDX_HEADS, IDX_DIM)
    wi = wi * (IDX_HEADS ** -0.5)
    att = dsa_attention(q, k, v, qi, ki, wi, rel_bias)
    return (att * jax.nn.silu(gate)) @ w_out


def setup_inputs(seed: int = 0) -> dict:
    key = jax.random.key(seed)
    ks = jax.random.split(key, 24)
    f32 = jnp.float32

    def nrm(k, shape, s):
        return jax.random.normal(k, shape, f32) * s

    ev_in = sum(EVEN_SPLITS)
    od_in = sum(ODD_SPLITS)
    ssm_n = jnp.arange(SSM_STATE, dtype=f32)
    return {
        'x': nrm(ks[0], (BATCH, SEQ, D_MODEL), 1.0),
        'rel_bias': nrm(ks[1], (NUM_BUCKETS, A_HEADS), 0.5),
        'norm_g': 1.0 + nrm(ks[2], (DEPTH, D_MODEL), 0.02),
        'ev_w_in': nrm(ks[3], (N_EVEN, D_MODEL, ev_in), D_MODEL ** -0.5),
        'ev_w_out': nrm(ks[4], (N_EVEN, A_WIDTH + B_WIDTH, D_MODEL), (A_WIDTH + B_WIDTH) ** -0.5),
        'ev_q_norm_g': 1.0 + nrm(ks[5], (N_EVEN, A_HEAD_DIM), 0.02),
        'ev_k_norm_g': 1.0 + nrm(ks[6], (N_EVEN, A_HEAD_DIM), 0.02),
        'ev_sinks': nrm(ks[7], (N_EVEN, A_HEADS), 1.0),
        'ev_ssm_log_dt': jax.random.uniform(ks[8], (N_EVEN, SSM_GROUPS), f32, math.log(DT_MIN), math.log(DT_MAX)),
        'ev_ssm_a_re': -0.5 + nrm(ks[9], (N_EVEN, SSM_GROUPS, SSM_STATE), 0.01),
        'ev_ssm_a_im': jnp.pi * ssm_n + nrm(ks[10], (N_EVEN, SSM_GROUPS, SSM_STATE), 0.01),
        'ev_ssm_b_re': nrm(ks[11], (N_EVEN, SSM_GROUPS, SSM_STATE, SSM_GROUP), (2 * SSM_GROUP) ** -0.5),
        'ev_ssm_b_im': nrm(ks[12], (N_EVEN, SSM_GROUPS, SSM_STATE, SSM_GROUP), (2 * SSM_GROUP) ** -0.5),
        'ev_ssm_c_re': nrm(ks[13], (N_EVEN, SSM_GROUPS, SSM_GROUP, SSM_STATE), (2 * SSM_STATE) ** -0.5),
        'ev_ssm_c_im': nrm(ks[14], (N_EVEN, SSM_GROUPS, SSM_GROUP, SSM_STATE), (2 * SSM_STATE) ** -0.5),
        'ev_ssm_d': nrm(ks[15], (N_EVEN, SSM_GROUPS, SSM_GROUP), 1.0),
        'ev_glu_w': nrm(ks[16], (N_EVEN, B_WIDTH, 2 * B_WIDTH), B_WIDTH ** -0.5),
        'ev_glu_b': nrm(ks[17], (N_EVEN, 2 * B_WIDTH), 0.01),
        'od_w_in': nrm(ks[18], (N_ODD, D_MODEL, od_in), D_MODEL ** -0.5),
        'od_w_out': nrm(ks[19], (N_ODD, C_WIDTH, D_MODEL), C_WIDTH ** -0.5),
        'od_q_norm_g': 1.0 + nrm(ks[20], (N_ODD, C_HEAD_DIM), 0.02),
        'od_k_norm_g': 1.0 + nrm(ks[21], (N_ODD, C_HEAD_DIM), 0.02),
    }


def reference(x, rel_bias, norm_g, ev_w_in, ev_w_out, ev_q_norm_g, ev_k_norm_g, ev_sinks,
              ev_ssm_log_dt, ev_ssm_a_re, ev_ssm_a_im, ev_ssm_b_re, ev_ssm_b_im, ev_ssm_c_re,
              ev_ssm_c_im, ev_ssm_d, ev_glu_w, ev_glu_b, od_w_in, od_w_out, od_q_norm_g, od_k_norm_g):
    h = x
    for layer in range(DEPTH):
        hn = rms_norm(h, norm_g[layer])
        j = layer // 2
        if layer % 2 == 0:
            out = even_layer(hn, rel_bias, ev_w_in[j], ev_w_out[j], ev_q_norm_g[j], ev_k_norm_g[j],
                             ev_sinks[j], ev_ssm_log_dt[j], ev_ssm_a_re[j], ev_ssm_a_im[j],
                             ev_ssm_b_re[j], ev_ssm_b_im[j], ev_ssm_c_re[j], ev_ssm_c_im[j],
                             ev_ssm_d[j], ev_glu_w[j], ev_glu_b[j])
        else:
            out = odd_layer(hn, rel_bias, od_w_in[j], od_w_out[j], od_q_norm_g[j], od_k_norm_g[j])
        h = h + out
    return h
```

```python
import functools
import math

import jax
import jax.numpy as jnp
import numpy as np
from jax import lax
from jax.experimental import pallas as pl
from jax.experimental.pallas import tpu as pltpu

F32 = jnp.float32
BF16 = jnp.bfloat16
I32 = jnp.int32

LANES = 128
SUBLANES = 8
VMEM_LIMIT = 56 * 1024 * 1024

BLOCK = 128
WINDOW = 128
A_HEADS = 8
A_HEAD_DIM = 64
A_KV_HEADS = 2
A_WIDTH = A_HEADS * A_HEAD_DIM
SSM_GROUP = 16
SSM_STATE = 64
C_HEADS = 8
C_HEAD_DIM = 128
C_KV_HEADS = 2
IDX_HEADS = 8
IDX_DIM = 64
TOPK_MAX = 256
NUM_BUCKETS = 32
REL_MAX_DIST = 1024
EPS = 1e-6
NEG_INF = -1e30
INT_MIN = -(2 ** 31)
INT_MAX = 2 ** 31 - 1

ROW_TILE = 256
KEY_CHUNK = 512
NEAR_BLOCKS = 8
NT_DIMS = (((1,), (1,)), ((), ()))


def _t5_bucket(dist):
    n = jnp.maximum(dist, 0)
    max_exact = NUM_BUCKETS // 2
    nf = jnp.maximum(n, 1).astype(F32)
    large = max_exact + (jnp.log(nf / max_exact) / math.log(REL_MAX_DIST / max_exact)
                         * (NUM_BUCKETS - max_exact)).astype(I32)
    large = jnp.minimum(large, NUM_BUCKETS - 1)
    return jnp.where(n < max_exact, n, large)


def _silu(x):
    return x * jax.nn.sigmoid(x)


def _rms(x, g):
    ms = jnp.mean(x * x, axis=-1, keepdims=True)
    return x * lax.rsqrt(ms + EPS) * g


def _mm(a, b):
    return jnp.dot(a, b, preferred_element_type=F32)


def _mm_nt(a, b):
    return lax.dot_general(a, b, NT_DIMS, preferred_element_type=F32)


def _params(*sem):
    return pltpu.CompilerParams(dimension_semantics=sem, vmem_limit_bytes=VMEM_LIMIT)


def _const_spec(shape):
    zeros = (0,) * len(shape)
    return pl.BlockSpec(shape, lambda *_: zeros)


def _proj0_kernel(x_ref, g_ref, w_ref, q_ref, k_ref, v_ref, sga_ref, u_ref, sgb_ref):
    hn = _rms(x_ref[...], g_ref[...]).astype(BF16)

    def mm(lo, hi):
        return _mm(hn, w_ref[:, lo:hi])

    q_ref[...] = mm(0, 512)
    k_ref[...] = mm(512, 640)
    v_ref[...] = mm(640, 768)
    sga_ref[...] = _silu(mm(768, 1280)).astype(BF16)
    u_ref[...] = mm(1280, 1792)
    sgb_ref[...] = _silu(mm(1792, 2304)).astype(BF16)


def _proj0(x2, g, w):
    rows, d = x2.shape
    t = ROW_TILE

    def row(n):
        return pl.BlockSpec((t, n), lambda i: (i, 0))

    return pl.pallas_call(
        _proj0_kernel,
        grid=(rows // t,),
        in_specs=[row(d), _const_spec((1, d)), _const_spec(w.shape)],
        out_specs=[row(512), row(128), row(128), row(512), row(512), row(512)],
        out_shape=[jax.ShapeDtypeStruct((rows, 512), F32), jax.ShapeDtypeStruct((rows, 128), F32),
                   jax.ShapeDtypeStruct((rows, 128), F32), jax.ShapeDtypeStruct((rows, 512), BF16),
                   jax.ShapeDtypeStruct((rows, 512), F32), jax.ShapeDtypeStruct((rows, 512), BF16)],
        compiler_params=_params("arbitrary"),
        name="proj0",
    )(x2, g, w)


def _attn0_kernel(q_ref, kc_ref, kp_ref, vc_ref, vp_ref, sga_ref, bias_ref, sink_ref, qg_ref, kg_ref, o_ref):
    i = pl.program_id(1)
    lane = lax.broadcasted_iota(I32, (1, LANES), 1)
    lo = lane < A_HEAD_DIM

    def segnorm(x, g2):
        sq = x * x
        s_lo = jnp.sum(jnp.where(lo, sq, 0.0), axis=-1, keepdims=True)
        s_hi = jnp.sum(jnp.where(lo, 0.0, sq), axis=-1, keepdims=True)
        inv = jnp.where(lo, lax.rsqrt(s_lo / A_HEAD_DIM + EPS), lax.rsqrt(s_hi / A_HEAD_DIM + EPS))
        return x * inv * g2

    kn = segnorm(jnp.concatenate([kp_ref[0], kc_ref[0]], axis=0), kg_ref[...])
    vb = jnp.concatenate([vp_ref[0], vc_ref[0]], axis=0)
    kr = pltpu.roll(kn, A_HEAD_DIM, axis=1)
    vr = pltpu.roll(vb, A_HEAD_DIM, axis=1)

    def variants(x, xr):
        return {(0, 0): jnp.where(lo, x, 0.0).astype(BF16), (0, 1): jnp.where(lo, 0.0, xr).astype(BF16),
                (1, 0): jnp.where(lo, xr, 0.0).astype(BF16), (1, 1): jnp.where(lo, 0.0, x).astype(BF16)}

    kvar = variants(kn, kr)
    vvar = variants(vb, vr)

    row = lax.broadcasted_iota(I32, (BLOCK, 2 * BLOCK), 0)
    col = lax.broadcasted_iota(I32, (BLOCK, 2 * BLOCK), 1)
    d = row + BLOCK - col
    mask = (d >= 0) & (d < WINDOW) & ((i > 0) | (col >= BLOCK))

    for p in range(A_HEADS // 2):
        sl = slice(p * LANES, (p + 1) * LANES)
        qp = (segnorm(q_ref[0, :, sl], qg_ref[...]) * (A_HEAD_DIM ** -0.5)).astype(BF16)
        g = p // 2
        acc = jnp.zeros((BLOCK, LANES), F32)
        for a in range(2):
            h = 2 * p + a
            lg = _mm_nt(qp, kvar[(g, a)]) + bias_ref[h]
            lg = jnp.where(mask, lg, NEG_INF)
            sink = sink_ref[h:h + 1, 0:1]
            m = jnp.maximum(jnp.max(lg, axis=-1, keepdims=True), sink)
            e = jnp.exp(lg - m)
            den = jnp.sum(e, axis=-1, keepdims=True) + jnp.exp(sink - m)
            acc = acc + _mm((e / den).astype(BF16), vvar[(g, a)])
        o_ref[0, :, sl] = (acc * sga_ref[0, :, sl].astype(F32)).astype(BF16)


def _attn0(q, k, v, sga, bias0, sinks, qg2, kg2):
    b, l, _ = q.shape
    nb = l // BLOCK

    def cur(n):
        return pl.BlockSpec((1, BLOCK, n), lambda bb, i: (bb, i, 0))

    def prev(n):
        return pl.BlockSpec((1, BLOCK, n), lambda bb, i: (bb, jnp.maximum(i - 1, 0), 0))

    return pl.pallas_call(
        _attn0_kernel,
        grid=(b, nb),
        in_specs=[cur(512), cur(128), prev(128), cur(128), prev(128), cur(512),
                  _const_spec(bias0.shape), _const_spec(sinks.shape), _const_spec(qg2.shape), _const_spec(kg2.shape)],
        out_specs=cur(512),
        out_shape=jax.ShapeDtypeStruct((b, l, 512), BF16),
        compiler_params=_params("arbitrary", "arbitrary"),
        name="attn0",
    )(q, k, k, v, v, sga, bias0, sinks, qg2, kg2)


def _ssm_kernel(u_ref, sgb_ref, bmat_ref, cre_ref, cim_ref, sc_ref, d_ref, gw_ref, gb_ref, o_ref, xre_ref, xim_ref):
    t = u_ref.shape[1]
    nq = bmat_ref.shape[0]
    half = bmat_ref.shape[2] // 2

    @pl.when(pl.program_id(1) == 0)
    def _():
        xre_ref[0:SUBLANES, :] = jnp.zeros((SUBLANES, xre_ref.shape[1]), F32)
        xim_ref[0:SUBLANES, :] = jnp.zeros((SUBLANES, xim_ref.shape[1]), F32)

    u = u_ref[0]
    ub = u.astype(BF16)
    for q in range(nq):
        bu = _mm(ub[:, q * LANES:(q + 1) * LANES], bmat_ref[q])
        xre_ref[SUBLANES:, q * half:(q + 1) * half] = bu[:, :half]
        xim_ref[SUBLANES:, q * half:(q + 1) * half] = bu[:, half:]

    def scan(r, _):
        base = pl.multiple_of(SUBLANES + r * SUBLANES, SUBLANES)
        xr = xre_ref[pl.ds(base, SUBLANES), :]
        xi = xim_ref[pl.ds(base, SUBLANES), :]
        for s, k in enumerate((1, 2, 4)):
            ar = sc_ref[2 * s]
            ai = sc_ref[2 * s + 1]
            sr = pltpu.roll(xr, k, axis=0)
            si = pltpu.roll(xi, k, axis=0)
            xr, xi = xr + ar * sr - ai * si, xi + ar * si + ai * sr
        cr = xre_ref[pl.ds(base - 1, 1), :]
        ci = xim_ref[pl.ds(base - 1, 1), :]
        pr = sc_ref[6]
        pi = sc_ref[7]
        xre_ref[pl.ds(base, SUBLANES), :] = xr + pr * cr - pi * ci
        xim_ref[pl.ds(base, SUBLANES), :] = xi + pr * ci + pi * cr
        return 0

    lax.fori_loop(0, t // SUBLANES, scan, 0, unroll=2)
    xre_ref[0:SUBLANES, :] = xre_ref[t:t + SUBLANES, :]
    xim_ref[0:SUBLANES, :] = xim_ref[t:t + SUBLANES, :]

    ys = []
    for q in range(nq):
        xr = xre_ref[SUBLANES:, q * half:(q + 1) * half].astype(BF16)
        xi = xim_ref[SUBLANES:, q * half:(q + 1) * half].astype(BF16)
        ys.append(_mm(xr, cre_ref[q]) + _mm(xi, cim_ref[q]))
    y = jnp.concatenate(ys, axis=1) + d_ref[...] * u
    y = jax.nn.gelu(y).astype(BF16)
    hh = _mm(y, gw_ref[...]) + gb_ref[...]
    w = hh.shape[1] // 2
    o_ref[0] = (hh[:, :w] * jax.nn.sigmoid(hh[:, w:]) * sgb_ref[0].astype(F32)).astype(BF16)


def _ssm(u, sgb, bmat, cre, cim, sc, dskip, gw, gb):
    b, l, w = u.shape
    t = ROW_TILE
    ns = sc.shape[-1]

    def row(n):
        return pl.BlockSpec((1, t, n), lambda bb, i: (bb, i, 0))

    return pl.pallas_call(
        _ssm_kernel,
        grid=(b, l // t),
        in_specs=[row(w), row(w), _const_spec(bmat.shape), _const_spec(cre.shape), _const_spec(cim.shape),
                  _const_spec(sc.shape), _const_spec(dskip.shape), _const_spec(gw.shape), _const_spec(gb.shape)],
        out_specs=row(w),
        out_shape=jax.ShapeDtypeStruct((b, l, w), BF16),
        scratch_shapes=[pltpu.VMEM((SUBLANES + t, ns), F32), pltpu.VMEM((SUBLANES + t, ns), F32)],
        compiler_params=_params("arbitrary", "arbitrary"),
        name="ssm",
    )(u, sgb, bmat, cre, cim, sc, dskip, gw, gb)


def _s5_prep(log_dt, a_re, a_im, b_re, b_im, c_re, c_im):
    g, p = a_re.shape
    h = b_re.shape[-1]
    gl = LANES // h
    nq = g // gl
    dt = jnp.exp(log_dt)[:, None]
    mag = jnp.exp(a_re * dt)
    ang = a_im * dt
    ab_re = mag * jnp.cos(ang)
    ab_im = mag * jnp.sin(ang)
    den = a_re * a_re + a_im * a_im
    n_re = ab_re - 1.0
    n_im = ab_im
    f_re = (n_re * a_re + n_im * a_im) / den
    f_im = (n_im * a_re - n_re * a_im) / den
    bb_re = f_re[..., None] * b_re - f_im[..., None] * b_im
    bb_im = f_re[..., None] * b_im + f_im[..., None] * b_re
    eye = jnp.eye(gl, dtype=F32)

    def bdiag_in(m):
        m = m.reshape(nq, gl, p, h)
        return jnp.einsum('qgph,gk->qghkp', m, eye).reshape(nq, gl * h, gl * p)

    def bdiag_out(m):
        m = m.reshape(nq, gl, h, p)
        return jnp.einsum('qghp,gk->qgpkh', m, eye).reshape(nq, gl * p, gl * h)

    bmat = jnp.concatenate([bdiag_in(bb_re), bdiag_in(bb_im)], axis=2).astype(BF16)
    cre = bdiag_out(c_re).astype(BF16)
    cim = bdiag_out(-c_im).astype(BF16)

    pw = [(ab_re.reshape(-1), ab_im.reshape(-1))]
    for _ in range(SUBLANES - 1):
        pr, pi = pw[-1]
        pw.append((pr * pw[0][0] - pi * pw[0][1], pr * pw[0][1] + pi * pw[0][0]))
    rows = jnp.arange(SUBLANES)[:, None]
    sc = []
    for k in (1, 2, 4):
        sc.append(jnp.where(rows >= k, pw[k - 1][0][None, :], 0.0))
        sc.append(jnp.where(rows >= k, pw[k - 1][1][None, :], 0.0))
    sc.append(jnp.stack([pw[r][0] for r in range(SUBLANES)]))
    sc.append(jnp.stack([pw[r][1] for r in range(SUBLANES)]))
    return bmat, cre, cim, jnp.stack(sc).astype(F32)


def _mid_kernel(x_ref, a_ref, s_ref, wo_ref, g_ref, wq_ref, wk_ref, wvt_ref, wg_ref, wqi_ref, wki_ref, wwt_ref,
                qg_ref, kg_ref, h_ref, q_ref, k_ref, vt_ref, sg_ref, qi_ref, ki_ref, wt_ref):
    aw = a_ref.shape[2]
    h = x_ref[0] + _mm(a_ref[0], wo_ref[0:aw, :]) + _mm(s_ref[0], wo_ref[aw:, :])
    h_ref[0] = h
    hn = _rms(h, g_ref[...]).astype(BF16)
    qf = _mm(hn, wq_ref[...])
    for hd in range(C_HEADS):
        sl = slice(hd * C_HEAD_DIM, (hd + 1) * C_HEAD_DIM)
        q_ref[0, :, sl] = (_rms(qf[:, sl], qg_ref[...]) * (C_HEAD_DIM ** -0.5)).astype(BF16)
    kf = _mm(hn, wk_ref[...])
    for hd in range(C_KV_HEADS):
        sl = slice(hd * C_HEAD_DIM, (hd + 1) * C_HEAD_DIM)
        k_ref[0, :, sl] = _rms(kf[:, sl], kg_ref[...]).astype(BF16)
    vt_ref[0] = _mm_nt(wvt_ref[...], hn).astype(BF16)
    sg_ref[0] = _silu(_mm(hn, wg_ref[...])).astype(BF16)
    qi_ref[0] = _mm(hn, wqi_ref[...]).astype(BF16)
    ki_ref[0] = _mm(hn, wki_ref[...]).astype(BF16)
    wt_ref[0] = _mm_nt(wwt_ref[...], hn) * ((IDX_HEADS ** -0.5) * (IDX_DIM ** -0.5))


def _mid(x, att0, ssm0, wo, g, wq, wk, wvt, wg, wqi, wki2, wwt, qg, kg):
    b, l, d = x.shape
    t = ROW_TILE

    def row(n):
        return pl.BlockSpec((1, t, n), lambda bb, i: (bb, i, 0))

    def col(n):
        return pl.BlockSpec((1, n, t), lambda bb, i: (bb, 0, i))

    weights = [wo, g, wq, wk, wvt, wg, wqi, wki2, wwt, qg, kg]
    cw = C_HEADS * C_HEAD_DIM
    ckv = C_KV_HEADS * C_HEAD_DIM
    return pl.pallas_call(
        _mid_kernel,
        grid=(b, l // t),
        in_specs=[row(d), row(att0.shape[2]), row(ssm0.shape[2])] + [_const_spec(w.shape) for w in weights],
        out_specs=[row(d), row(cw), row(ckv), col(ckv), row(cw), row(IDX_HEADS * IDX_DIM), row(2 * LANES),
                   col(IDX_HEADS)],
        out_shape=[jax.ShapeDtypeStruct((b, l, d), F32), jax.ShapeDtypeStruct((b, l, cw), BF16),
                   jax.ShapeDtypeStruct((b, l, ckv), BF16), jax.ShapeDtypeStruct((b, ckv, l), BF16),
                   jax.ShapeDtypeStruct((b, l, cw), BF16), jax.ShapeDtypeStruct((b, l, IDX_HEADS * IDX_DIM), BF16),
                   jax.ShapeDtypeStruct((b, l, 2 * LANES), BF16), jax.ShapeDtypeStruct((b, IDX_HEADS, l), F32)],
        compiler_params=_params("arbitrary", "arbitrary"),
        name="mid",
    )(x, att0, ssm0, *weights)


def _dsa_kernel(q_ref, qi_ref, wt_ref, sg_ref, k_ref, vt_ref, ki_ref, tab_ref, o_ref,
                keys_ref, m_ref, l_ref, acc_ref, *, seq_len, topk):
    i = pl.program_id(1)
    ck = KEY_CHUNK
    per = ck // BLOCK
    nch = (i + per) // per
    t_row = i * BLOCK + lax.broadcasted_iota(I32, (1, LANES), 1)
    kiota = lax.broadcasted_iota(I32, (ck, LANES), 0)
    npair = C_HEADS // 2

    def chunk_off(c):
        return pl.multiple_of(c * ck, ck)

    qi = qi_ref[0]
    qi_stack = [jnp.concatenate([qi[:, (2 * s) * LANES:(2 * s + 1) * LANES],
                                 qi[:, (2 * s + 1) * LANES:(2 * s + 2) * LANES]], axis=0) for s in range(2)]
    wt = wt_ref[0]

    def score_chunk(c, _):
        off = chunk_off(c)
        sc = jnp.zeros((ck, LANES), F32)
        for a in range(2):
            kk = ki_ref[0, pl.ds(off, ck), a * LANES:(a + 1) * LANES]
            for s in range(2):
                r = _mm_nt(kk, qi_stack[s])
                for j in range(2):
                    hd = 2 * (2 * s + j) + a
                    sc = sc + jnp.maximum(r[:, j * LANES:(j + 1) * LANES], 0.0) * wt[hd:hd + 1, :]
        sc = jnp.where(off + kiota <= t_row, sc, NEG_INF)
        bits = pltpu.bitcast(sc, I32)
        keys_ref[pl.ds(off, ck), :] = jnp.where(bits < 0, INT_MIN - bits, bits)
        return 0

    lax.fori_loop(0, nch, score_chunk, 0)

    def count(pred):
        def body(c, acc):
            off = chunk_off(c)
            ind = pred(keys_ref[pl.ds(off, ck), :], off).astype(I32)
            return acc + jnp.sum(ind.reshape(ck // SUBLANES, SUBLANES, LANES), axis=0)

        acc = lax.fori_loop(0, nch, body, jnp.zeros((SUBLANES, LANES), I32))
        return jnp.sum(acc, axis=0, keepdims=True)

    def bisect(_, st):
        lo, hi, clo, chi = st
        mid = (lo >> 1) + (hi >> 1) + (lo & hi & 1)
        c = count(lambda k, off: k >= mid)
        ge = c >= topk
        return jnp.where(ge, mid, lo), jnp.where(ge, hi, mid), jnp.where(ge, c, clo), jnp.where(ge, chi, c)

    def full(v):
        return jnp.full((1, LANES), v, I32)

    searching = (i + 1) * BLOCK > topk
    nbis = jnp.where(searching, 32, 0)
    vkey, _, clo, chi = lax.fori_loop(0, nbis, bisect, (full(INT_MIN), full(INT_MAX), full(0) + nch * ck, full(0)))
    need = topk - chi
    ties = clo - chi

    def tie_search():
        def body(_, st):
            lj, hj = st
            mid = (lj + hj) >> 1
            c = count(lambda k, off: (k == vkey) & (off + kiota <= mid))
            ok = c >= need
            return jnp.where(ok, lj, mid), jnp.where(ok, mid, hj)

        _, hj = lax.fori_loop(0, 14, body, (full(-1), full(0) + (nch * ck - 1)))
        return hj

    any_tie = searching & (jnp.max(ties - need) > 0)
    jmax = lax.cond(any_tie, tie_search, lambda: full(seq_len))

    m_ref[...] = jnp.full(m_ref.shape, NEG_INF, F32)
    l_ref[...] = jnp.zeros(l_ref.shape, F32)
    acc_ref[...] = jnp.zeros(acc_ref.shape, F32)
    q = q_ref[0]
    q_stack = [jnp.concatenate([q[:, (2 * j) * LANES:(2 * j + 1) * LANES],
                                q[:, (2 * j + 1) * LANES:(2 * j + 2) * LANES]], axis=0) for j in range(npair)]

    def attend(c, _):
        off = chunk_off(c)
        key = keys_ref[pl.ds(off, ck), :]
        s_idx = off + kiota
        sel = ((key > vkey) | ((key == vkey) & (s_idx <= jmax))) & (s_idx <= t_row)
        madd = jnp.where(sel, 0.0, NEG_INF)
        tidx = [jnp.clip(i - (c * per + jj), 0, NEAR_BLOCKS) for jj in range(per)]
        for j in range(npair):
            g = j // (npair // C_KV_HEADS)
            kc = k_ref[0, pl.ds(off, ck), g * LANES:(g + 1) * LANES]
            lg = _mm_nt(kc, q_stack[j])
            ps, alphas = [], []
            for a in range(2):
                hd = 2 * j + a
                bias = jnp.concatenate([tab_ref[hd, tidx[jj]] for jj in range(per)], axis=0)
                x = lg[:, a * LANES:(a + 1) * LANES] + madd + bias
                m_old = m_ref[hd:hd + 1, :]
                m_new = jnp.maximum(m_old, jnp.max(x, axis=0, keepdims=True))
                alpha = jnp.exp(m_old - m_new)
                p = jnp.exp(x - m_new)
                l_ref[hd:hd + 1, :] = alpha * l_ref[hd:hd + 1, :] + jnp.sum(p, axis=0, keepdims=True)
                m_ref[hd:hd + 1, :] = m_new
                ps.append(p.astype(BF16))
                alphas.append(alpha)
            vt = vt_ref[0, g * LANES:(g + 1) * LANES, pl.ds(off, ck)]
            acc_ref[j] = acc_ref[j] * jnp.concatenate(alphas, axis=1) + _mm(vt, jnp.concatenate(ps, axis=1))
        return 0

    lax.fori_loop(0, nch, attend, 0)

    for j in range(npair):
        for a in range(2):
            hd = 2 * j + a
            sl = slice(hd * LANES, (hd + 1) * LANES)
            ot = acc_ref[j, :, a * LANES:(a + 1) * LANES] / l_ref[hd:hd + 1, :]
            o_ref[0, :, sl] = (ot.T * sg_ref[0, :, sl].astype(F32)).astype(BF16)


def _dsa(q, qi, wt, sg, k, vt, ki2, tab):
    b, l, cw = q.shape
    nb = l // BLOCK
    topk = min(TOPK_MAX, l // 4)

    def blk(n):
        return pl.BlockSpec((1, BLOCK, n), lambda bb, i: (bb, i, 0))

    def whole(s1, s2):
        return pl.BlockSpec((1, s1, s2), lambda bb, i: (bb, 0, 0))

    return pl.pallas_call(
        functools.partial(_dsa_kernel, seq_len=l, topk=topk),
        grid=(b, nb),
        in_specs=[blk(cw), blk(qi.shape[2]), pl.BlockSpec((1, IDX_HEADS, BLOCK), lambda bb, i: (bb, 0, i)), blk(cw),
                  whole(l, k.shape[2]), whole(vt.shape[1], l), whole(l, ki2.shape[2]), _const_spec(tab.shape)],
        out_specs=blk(cw),
        out_shape=jax.ShapeDtypeStruct((b, l, cw), BF16),
        scratch_shapes=[pltpu.VMEM((l, LANES), I32), pltpu.VMEM((C_HEADS, LANES), F32),
                        pltpu.VMEM((C_HEADS, LANES), F32), pltpu.VMEM((C_HEADS // 2, C_HEAD_DIM, 2 * LANES), F32)],
        compiler_params=_params("arbitrary", "arbitrary"),
        name="dsa",
    )(q, qi, wt, sg, k, vt, ki2, tab)


def _out_kernel(h_ref, a_ref, w_ref, o_ref):
    o_ref[...] = h_ref[...] + _mm(a_ref[...], w_ref[...])


def _outproj(h2, a2, w):
    rows, d = h2.shape
    t = ROW_TILE
    return pl.pallas_call(
        _out_kernel,
        grid=(rows // t,),
        in_specs=[pl.BlockSpec((t, d), lambda i: (i, 0)), pl.BlockSpec((t, a2.shape[1]), lambda i: (i, 0)),
                  _const_spec(w.shape)],
        out_specs=pl.BlockSpec((t, d), lambda i: (i, 0)),
        out_shape=jax.ShapeDtypeStruct((rows, d), F32),
        compiler_params=_params("arbitrary"),
        name="outproj1",
    )(h2, a2, w)


def _bias_tables(rel_bias, seq_len):
    vec = rel_bias[_t5_bucket(jnp.arange(seq_len, dtype=I32))].astype(F32)
    i = jnp.arange(BLOCK, dtype=I32)[:, None]
    j = jnp.arange(2 * BLOCK, dtype=I32)[None, :]
    bias0 = jnp.transpose(vec[jnp.clip(i + BLOCK - j, 0, seq_len - 1)], (2, 0, 1))
    kl = jnp.arange(BLOCK, dtype=I32)[:, None]
    ql = jnp.arange(BLOCK, dtype=I32)[None, :]
    delta = jnp.arange(NEAR_BLOCKS, dtype=I32)[:, None, None]
    dist = jnp.clip(delta * BLOCK + ql - kl, 0, seq_len - 1)
    far = rel_bias[NUM_BUCKETS - 1].astype(F32)
    tab = jnp.transpose(vec[dist], (3, 0, 1, 2)) - far[:, None, None, None]
    tab = jnp.concatenate([tab, jnp.zeros((tab.shape[0], 1, BLOCK, BLOCK), F32)], axis=1)
    return bias0, tab


def kernel(x, rel_bias, norm_g, ev_w_in, ev_w_out, ev_q_norm_g, ev_k_norm_g, ev_sinks, ev_ssm_log_dt, ev_ssm_a_re,
           ev_ssm_a_im, ev_ssm_b_re, ev_ssm_b_im, ev_ssm_c_re, ev_ssm_c_im, ev_ssm_d, ev_glu_w, ev_glu_b, od_w_in,
           od_w_out, od_q_norm_g, od_k_norm_g):
    b, l, d = x.shape
    assert l % KEY_CHUNK == 0 and l % ROW_TILE == 0
    assert (NEAR_BLOCKS - 1) * BLOCK + 1 >= 16 * 64 ** (15 / 16) + 1
    bias0, tab = _bias_tables(rel_bias, l)

    q0, k0, v0, sga, u, sgb = _proj0(x.reshape(b * l, d), norm_g[0][None, :], ev_w_in[0].astype(BF16))
    shp = lambda a: a.reshape(b, l, a.shape[-1])
    qg2 = jnp.tile(ev_q_norm_g[0], 2)[None, :]
    kg2 = jnp.tile(ev_k_norm_g[0], 2)[None, :]
    sinks = jnp.broadcast_to(ev_sinks[0][:, None], (A_HEADS, LANES)).astype(F32)
    att0 = _attn0(shp(q0), shp(k0), shp(v0), shp(sga), bias0, sinks, qg2, kg2)
    bmat, cre, cim, sc = _s5_prep(ev_ssm_log_dt[0], ev_ssm_a_re[0], ev_ssm_a_im[0], ev_ssm_b_re[0], ev_ssm_b_im[0],
                                  ev_ssm_c_re[0], ev_ssm_c_im[0])
    ssm0 = _ssm(shp(u), shp(sgb), bmat, cre, cim, sc, ev_ssm_d[0].reshape(1, -1), ev_glu_w[0].astype(BF16),
                ev_glu_b[0][None, :])

    w1 = od_w_in[0]
    cw = C_HEADS * C_HEAD_DIM
    ckv = C_KV_HEADS * C_HEAD_DIM
    o = np.cumsum([0, cw, ckv, ckv, cw, IDX_HEADS * IDX_DIM, IDX_DIM, IDX_HEADS])
    wq, wk, wv, wg, wqi, wki, ww = (w1[:, o[n]:o[n + 1]] for n in range(7))
    zki = jnp.zeros((d, LANES - IDX_DIM), w1.dtype)
    wki2 = jnp.concatenate([wki, zki, zki, wki], axis=1)
    bf = lambda a: a.astype(BF16)
    h1, q1, k1, vt1, sg1, qi1, ki2, wt1 = _mid(
        x, att0, ssm0, bf(ev_w_out[0]), norm_g[1][None, :], bf(wq), bf(wk), bf(wv.T), bf(wg), bf(wqi), bf(wki2),
        bf(ww.T), od_q_norm_g[0][None, :], od_k_norm_g[0][None, :])
    att1 = _dsa(q1, qi1, wt1, sg1, k1, vt1, ki2, tab)
    out = _outproj(h1.reshape(b * l, d), att1.reshape(b * l, cw), bf(od_w_out[0]))
    return out.reshape(b, l, d)
```

```python
import functools
import math

import jax
import jax.numpy as jnp
import numpy as np
from jax import lax
from jax.experimental import pallas as pl
from jax.experimental.pallas import tpu as pltpu

F32 = jnp.float32
BF16 = jnp.bfloat16
I32 = jnp.int32

LANES = 128
SUBLANES = 8
VMEM_LIMIT = 56 * 1024 * 1024

BLOCK = 128
WINDOW = 128
A_HEADS = 8
A_HEAD_DIM = 64
A_KV_HEADS = 2
A_WIDTH = A_HEADS * A_HEAD_DIM
SSM_GROUP = 16
SSM_STATE = 64
C_HEADS = 8
C_HEAD_DIM = 128
C_KV_HEADS = 2
IDX_HEADS = 8
IDX_DIM = 64
TOPK_MAX = 256
NUM_BUCKETS = 32
REL_MAX_DIST = 1024
EPS = 1e-6
NEG_INF = -1e30
INT_MIN = -(2 ** 31)
INT_MAX = 2 ** 31 - 1

ROW_TILE = 256
KEY_CHUNK = 512
NEAR_BLOCKS = 8
NT_DIMS = (((1,), (1,)), ((), ()))


def _t5_bucket(dist):
    n = jnp.maximum(dist, 0)
    max_exact = NUM_BUCKETS // 2
    nf = jnp.maximum(n, 1).astype(F32)
    large = max_exact + (jnp.log(nf / max_exact) / math.log(REL_MAX_DIST / max_exact)
                         * (NUM_BUCKETS - max_exact)).astype(I32)
    large = jnp.minimum(large, NUM_BUCKETS - 1)
    return jnp.where(n < max_exact, n, large)


def _silu(x):
    return x * jax.nn.sigmoid(x)


def _rms(x, g):
    ms = jnp.mean(x * x, axis=-1, keepdims=True)
    return x * lax.rsqrt(ms + EPS) * g


def _mm(a, b):
    return jnp.dot(a, b, preferred_element_type=F32)


def _mm_nt(a, b):
    return lax.dot_general(a, b, NT_DIMS, preferred_element_type=F32)


def _params(*sem):
    return pltpu.CompilerParams(dimension_semantics=sem, vmem_limit_bytes=VMEM_LIMIT)


def _const_spec(shape):
    zeros = (0,) * len(shape)
    return pl.BlockSpec(shape, lambda *_: zeros)


def _proj0_kernel(x_ref, g_ref, w_ref, q_ref, k_ref, v_ref, sga_ref, u_ref, sgb_ref):
    hn = _rms(x_ref[...], g_ref[...]).astype(BF16)

    def mm(lo, hi):
        return _mm(hn, w_ref[:, lo:hi])

    q_ref[...] = mm(0, 512)
    k_ref[...] = mm(512, 640)
    v_ref[...] = mm(640, 768)
    sga_ref[...] = _silu(mm(768, 1280)).astype(BF16)
    u_ref[...] = mm(1280, 1792)
    sgb_ref[...] = _silu(mm(1792, 2304)).astype(BF16)


def _proj0(x2, g, w):
    rows, d = x2.shape
    t = ROW_TILE

    def row(n):
        return pl.BlockSpec((t, n), lambda i: (i, 0))

    return pl.pallas_call(
        _proj0_kernel,
        grid=(rows // t,),
        in_specs=[row(d), _const_spec((1, d)), _const_spec(w.shape)],
        out_specs=[row(512), row(128), row(128), row(512), row(512), row(512)],
        out_shape=[jax.ShapeDtypeStruct((rows, 512), F32), jax.ShapeDtypeStruct((rows, 128), F32),
                   jax.ShapeDtypeStruct((rows, 128), F32), jax.ShapeDtypeStruct((rows, 512), BF16),
                   jax.ShapeDtypeStruct((rows, 512), F32), jax.ShapeDtypeStruct((rows, 512), BF16)],
        compiler_params=_params("arbitrary"),
        name="proj0",
    )(x2, g, w)


def _attn0_kernel(q_ref, kc_ref, kp_ref, vc_ref, vp_ref, sga_ref, bias_ref, sink_ref, qg_ref, kg_ref, o_ref):
    i = pl.program_id(1)
    lane = lax.broadcasted_iota(I32, (1, LANES), 1)
    lo = lane < A_HEAD_DIM

    def segnorm(x, g2):
        sq = x * x
        s_lo = jnp.sum(jnp.where(lo, sq, 0.0), axis=-1, keepdims=True)
        s_hi = jnp.sum(jnp.where(lo, 0.0, sq), axis=-1, keepdims=True)
        inv = jnp.where(lo, lax.rsqrt(s_lo / A_HEAD_DIM + EPS), lax.rsqrt(s_hi / A_HEAD_DIM + EPS))
        return x * inv * g2

    kn = segnorm(jnp.concatenate([kp_ref[0], kc_ref[0]], axis=0), kg_ref[...])
    vb = jnp.concatenate([vp_ref[0], vc_ref[0]], axis=0)
    kr = pltpu.roll(kn, A_HEAD_DIM, axis=1)
    vr = pltpu.roll(vb, A_HEAD_DIM, axis=1)

    def variants(x, xr):
        return {(0, 0): jnp.where(lo, x, 0.0).astype(BF16), (0, 1): jnp.where(lo, 0.0, xr).astype(BF16),
                (1, 0): jnp.where(lo, xr, 0.0).astype(BF16), (1, 1): jnp.where(lo, 0.0, x).astype(BF16)}

    kvar = variants(kn, kr)
    vvar = variants(vb, vr)

    row = lax.broadcasted_iota(I32, (BLOCK, 2 * BLOCK), 0)
    col = lax.broadcasted_iota(I32, (BLOCK, 2 * BLOCK), 1)
    d = row + BLOCK - col
    mask = (d >= 0) & (d < WINDOW) & ((i > 0) | (col >= BLOCK))

    for p in range(A_HEADS // 2):
        sl = slice(p * LANES, (p + 1) * LANES)
        qp = (segnorm(q_ref[0, :, sl], qg_ref[...]) * (A_HEAD_DIM ** -0.5)).astype(BF16)
        g = p // 2
        acc = jnp.zeros((BLOCK, LANES), F32)
        for a in range(2):
            h = 2 * p + a
            lg = _mm_nt(qp, kvar[(g, a)]) + bias_ref[h]
            lg = jnp.where(mask, lg, NEG_INF)
            sink = sink_ref[h:h + 1, 0:1]
            m = jnp.maximum(jnp.max(lg, axis=-1, keepdims=True), sink)
            e = jnp.exp(lg - m)
            den = jnp.sum(e, axis=-1, keepdims=True) + jnp.exp(sink - m)
            acc = acc + _mm((e / den).astype(BF16), vvar[(g, a)])
        o_ref[0, :, sl] = (acc * sga_ref[0, :, sl].astype(F32)).astype(BF16)


def _attn0(q, k, v, sga, bias0, sinks, qg2, kg2):
    b, l, _ = q.shape
    nb = l // BLOCK

    def cur(n):
        return pl.BlockSpec((1, BLOCK, n), lambda bb, i: (bb, i, 0))

    def prev(n):
        return pl.BlockSpec((1, BLOCK, n), lambda bb, i: (bb, jnp.maximum(i - 1, 0), 0))

    return pl.pallas_call(
        _attn0_kernel,
        grid=(b, nb),
        in_specs=[cur(512), cur(128), prev(128), cur(128), prev(128), cur(512),
                  _const_spec(bias0.shape), _const_spec(sinks.shape), _const_spec(qg2.shape), _const_spec(kg2.shape)],
        out_specs=cur(512),
        out_shape=jax.ShapeDtypeStruct((b, l, 512), BF16),
        compiler_params=_params("arbitrary", "arbitrary"),
        name="attn0",
    )(q, k, k, v, v, sga, bias0, sinks, qg2, kg2)


def _ssm_kernel(u_ref, sgb_ref, bmat_ref, cre_ref, cim_ref, sc_ref, d_ref, gw_ref, gb_ref, o_ref, xre_ref, xim_ref):
    t = u_ref.shape[1]
    nq = bmat_ref.shape[0]
    half = bmat_ref.shape[2] // 2

    @pl.when(pl.program_id(1) == 0)
    def _():
        xre_ref[0:SUBLANES, :] = jnp.zeros((SUBLANES, xre_ref.shape[1]), F32)
        xim_ref[0:SUBLANES, :] = jnp.zeros((SUBLANES, xim_ref.shape[1]), F32)

    u = u_ref[0]
    ub = u.astype(BF16)
    for q in range(nq):
        bu = _mm(ub[:, q * LANES:(q + 1) * LANES], bmat_ref[q])
        xre_ref[SUBLANES:, q * half:(q + 1) * half] = bu[:, :half]
        xim_ref[SUBLANES:, q * half:(q + 1) * half] = bu[:, half:]

    def scan(r, _):
        base = pl.multiple_of(SUBLANES + r * SUBLANES, SUBLANES)
        xr = xre_ref[pl.ds(base, SUBLANES), :]
        xi = xim_ref[pl.ds(base, SUBLANES), :]
        for s, k in enumerate((1, 2, 4)):
            ar = sc_ref[2 * s]
            ai = sc_ref[2 * s + 1]
            sr = pltpu.roll(xr, k, axis=0)
            si = pltpu.roll(xi, k, axis=0)
            xr, xi = xr + ar * sr - ai * si, xi + ar * si + ai * sr
        cr = xre_ref[pl.ds(base - 1, 1), :]
        ci = xim_ref[pl.ds(base - 1, 1), :]
        pr = sc_ref[6]
        pi = sc_ref[7]
        xre_ref[pl.ds(base, SUBLANES), :] = xr + pr * cr - pi * ci
        xim_ref[pl.ds(base, SUBLANES), :] = xi + pr * ci + pi * cr
        return 0

    lax.fori_loop(0, t // SUBLANES, scan, 0, unroll=2)
    xre_ref[0:SUBLANES, :] = xre_ref[t:t + SUBLANES, :]
    xim_ref[0:SUBLANES, :] = xim_ref[t:t + SUBLANES, :]

    ys = []
    for q in range(nq):
        xr = xre_ref[SUBLANES:, q * half:(q + 1) * half].astype(BF16)
        xi = xim_ref[SUBLANES:, q * half:(q + 1) * half].astype(BF16)
        ys.append(_mm(xr, cre_ref[q]) + _mm(xi, cim_ref[q]))
    y = jnp.concatenate(ys, axis=1) + d_ref[...] * u
    y = jax.nn.gelu(y).astype(BF16)
    hh = _mm(y, gw_ref[...]) + gb_ref[...]
    w = hh.shape[1] // 2
    o_ref[0] = (hh[:, :w] * jax.nn.sigmoid(hh[:, w:]) * sgb_ref[0].astype(F32)).astype(BF16)


def _ssm(u, sgb, bmat, cre, cim, sc, dskip, gw, gb):
    b, l, w = u.shape
    t = ROW_TILE
    ns = sc.shape[-1]

    def row(n):
        return pl.BlockSpec((1, t, n), lambda bb, i: (bb, i, 0))

    return pl.pallas_call(
        _ssm_kernel,
        grid=(b, l // t),
        in_specs=[row(w), row(w), _const_spec(bmat.shape), _const_spec(cre.shape), _const_spec(cim.shape),
                  _const_spec(sc.shape), _const_spec(dskip.shape), _const_spec(gw.shape), _const_spec(gb.shape)],
        out_specs=row(w),
        out_shape=jax.ShapeDtypeStruct((b, l, w), BF16),
        scratch_shapes=[pltpu.VMEM((SUBLANES + t, ns), F32), pltpu.VMEM((SUBLANES + t, ns), F32)],
        compiler_params=_params("arbitrary", "arbitrary"),
        name="ssm",
    )(u, sgb, bmat, cre, cim, sc, dskip, gw, gb)


def _s5_prep(log_dt, a_re, a_im, b_re, b_im, c_re, c_im):
    g, p = a_re.shape
    h = b_re.shape[-1]
    gl = LANES // h
    nq = g // gl
    dt = jnp.exp(log_dt)[:, None]
    mag = jnp.exp(a_re * dt)
    ang = a_im * dt
    ab_re = mag * jnp.cos(ang)
    ab_im = mag * jnp.sin(ang)
    den = a_re * a_re + a_im * a_im
    n_re = ab_re - 1.0
    n_im = ab_im
    f_re = (n_re * a_re + n_im * a_im) / den
    f_im = (n_im * a_re - n_re * a_im) / den
    bb_re = f_re[..., None] * b_re - f_im[..., None] * b_im
    bb_im = f_re[..., None] * b_im + f_im[..., None] * b_re
    eye = jnp.eye(gl, dtype=F32)

    def bdiag_in(m):
        m = m.reshape(nq, gl, p, h)
        return jnp.einsum('qgph,gk->qghkp', m, eye).reshape(nq, gl * h, gl * p)

    def bdiag_out(m):
        m = m.reshape(nq, gl, h, p)
        return jnp.einsum('qghp,gk->qgpkh', m, eye).reshape(nq, gl * p, gl * h)

    bmat = jnp.concatenate([bdiag_in(bb_re), bdiag_in(bb_im)], axis=2).astype(BF16)
    cre = bdiag_out(c_re).astype(BF16)
    cim = bdiag_out(-c_im).astype(BF16)

    pw = [(ab_re.reshape(-1), ab_im.reshape(-1))]
    for _ in range(SUBLANES - 1):
        pr, pi = pw[-1]
        pw.append((pr * pw[0][0] - pi * pw[0][1], pr * pw[0][1] + pi * pw[0][0]))
    rows = jnp.arange(SUBLANES)[:, None]
    sc = []
    for k in (1, 2, 4):
        sc.append(jnp.where(rows >= k, pw[k - 1][0][None, :], 0.0))
        sc.append(jnp.where(rows >= k, pw[k - 1][1][None, :], 0.0))
    sc.append(jnp.stack([pw[r][0] for r in range(SUBLANES)]))
    sc.append(jnp.stack([pw[r][1] for r in range(SUBLANES)]))
    return bmat, cre, cim, jnp.stack(sc).astype(F32)


def _mid_kernel(x_ref, a_ref, s_ref, wo_ref, g_ref, wq_ref, wk_ref, wvt_ref, wg_ref, wqi_ref, wki_ref, wwt_ref,
                qg_ref, kg_ref, h_ref, q_ref, k_ref, vt_ref, sg_ref, qi_ref, ki_ref, wt_ref):
    aw = a_ref.shape[2]
    h = x_ref[0] + _mm(a_ref[0], wo_ref[0:aw, :]) + _mm(s_ref[0], wo_ref[aw:, :])
    h_ref[0] = h
    hn = _rms(h, g_ref[...]).astype(BF16)
    qf = _mm(hn, wq_ref[...])
    for hd in range(C_HEADS):
        sl = slice(hd * C_HEAD_DIM, (hd + 1) * C_HEAD_DIM)
        q_ref[0, :, sl] = (_rms(qf[:, sl], qg_ref[...]) * (C_HEAD_DIM ** -0.5)).astype(BF16)
    kf = _mm(hn, wk_ref[...])
    for hd in range(C_KV_HEADS):
        sl = slice(hd * C_HEAD_DIM, (hd + 1) * C_HEAD_DIM)
        k_ref[0, :, sl] = _rms(kf[:, sl], kg_ref[...]).astype(BF16)
    vt_ref[0] = _mm_nt(wvt_ref[...], hn).astype(BF16)
    sg_ref[0] = _silu(_mm(hn, wg_ref[...])).astype(BF16)
    qi_ref[0] = _mm(hn, wqi_ref[...]).astype(BF16)
    ki_ref[0] = _mm(hn, wki_ref[...]).astype(BF16)
    wt_ref[0] = _mm_nt(wwt_ref[...], hn) * ((IDX_HEADS ** -0.5) * (IDX_DIM ** -0.5))


def _mid(x, att0, ssm0, wo, g, wq, wk, wvt, wg, wqi, wki2, wwt, qg, kg):
    b, l, d = x.shape
    t = ROW_TILE

    def row(n):
        return pl.BlockSpec((1, t, n), lambda bb, i: (bb, i, 0))

    def col(n):
        return pl.BlockSpec((1, n, t), lambda bb, i: (bb, 0, i))

    weights = [wo, g, wq, wk, wvt, wg, wqi, wki2, wwt, qg, kg]
    cw = C_HEADS * C_HEAD_DIM
    ckv = C_KV_HEADS * C_HEAD_DIM
    return pl.pallas_call(
        _mid_kernel,
        grid=(b, l // t),
        in_specs=[row(d), row(att0.shape[2]), row(ssm0.shape[2])] + [_const_spec(w.shape) for w in weights],
        out_specs=[row(d), row(cw), row(ckv), col(ckv), row(cw), row(IDX_HEADS * IDX_DIM), row(2 * LANES),
                   col(IDX_HEADS)],
        out_shape=[jax.ShapeDtypeStruct((b, l, d), F32), jax.ShapeDtypeStruct((b, l, cw), BF16),
                   jax.ShapeDtypeStruct((b, l, ckv), BF16), jax.ShapeDtypeStruct((b, ckv, l), BF16),
                   jax.ShapeDtypeStruct((b, l, cw), BF16), jax.ShapeDtypeStruct((b, l, IDX_HEADS * IDX_DIM), BF16),
                   jax.ShapeDtypeStruct((b, l, 2 * LANES), BF16), jax.ShapeDtypeStruct((b, IDX_HEADS, l), F32)],
        compiler_params=_params("arbitrary", "arbitrary"),
        name="mid",
    )(x, att0, ssm0, *weights)


def _dsa_kernel(q_ref, qi_ref, wt_ref, sg_ref, k_ref, vt_ref, ki_ref, tab_ref, o_ref,
                keys_ref, acc_ref, *, seq_len, topk):
    i = pl.program_id(1)
    ck = KEY_CHUNK
    per = ck // BLOCK
    nch = (i + per) // per
    t_row = i * BLOCK + lax.broadcasted_iota(I32, (1, LANES), 1)
    kiota = lax.broadcasted_iota(I32, (ck, LANES), 0)
    npair = C_HEADS // 2

    def chunk_off(c):
        return pl.multiple_of(c * ck, ck)

    qi = qi_ref[0]
    qi_stack = [jnp.concatenate([qi[:, (2 * s) * LANES:(2 * s + 1) * LANES],
                                 qi[:, (2 * s + 1) * LANES:(2 * s + 2) * LANES]], axis=0) for s in range(2)]
    wt = wt_ref[0]

    def score_chunk(c, _):
        off = chunk_off(c)
        sc = jnp.zeros((ck, LANES), F32)
        for a in range(2):
            kk = ki_ref[0, pl.ds(off, ck), a * LANES:(a + 1) * LANES]
            for s in range(2):
                r = _mm_nt(kk, qi_stack[s])
                for j in range(2):
                    hd = 2 * (2 * s + j) + a
                    sc = sc + jnp.maximum(r[:, j * LANES:(j + 1) * LANES], 0.0) * wt[hd:hd + 1, :]
        sc = jnp.where(off + kiota <= t_row, sc, NEG_INF)
        bits = pltpu.bitcast(sc, I32)
        keys_ref[pl.ds(off, ck), :] = jnp.where(bits < 0, INT_MIN - bits, bits)
        return 0

    lax.fori_loop(0, nch, score_chunk, 0)

    def count(pred):
        def body(c, acc):
            off = chunk_off(c)
            ind = pred(keys_ref[pl.ds(off, ck), :], off).astype(I32)
            return acc + jnp.sum(ind.reshape(ck // SUBLANES, SUBLANES, LANES), axis=0)

        acc = lax.fori_loop(0, nch, body, jnp.zeros((SUBLANES, LANES), I32))
        return jnp.sum(acc, axis=0, keepdims=True)

    def bisect(_, st):
        lo, hi, clo, chi = st
        mid = (lo >> 1) + (hi >> 1) + (lo & hi & 1)
        c = count(lambda k, off: k >= mid)
        ge = c >= topk
        return jnp.where(ge, mid, lo), jnp.where(ge, hi, mid), jnp.where(ge, c, clo), jnp.where(ge, chi, c)

    def full(v):
        return jnp.full((1, LANES), v, I32)

    searching = (i + 1) * BLOCK > topk
    nbis = jnp.where(searching, 32, 0)
    vkey, _, clo, chi = lax.fori_loop(0, nbis, bisect, (full(INT_MIN), full(INT_MAX), full(0) + nch * ck, full(0)))
    need = topk - chi
    ties = clo - chi

    def tie_search():
        def body(_, st):
            lj, hj = st
            mid = (lj + hj) >> 1
            c = count(lambda k, off: (k == vkey) & (off + kiota <= mid))
            ok = c >= need
            return jnp.where(ok, lj, mid), jnp.where(ok, mid, hj)

        _, hj = lax.fori_loop(0, 14, body, (full(-1), full(0) + (nch * ck - 1)))
        return hj

    any_tie = searching & (jnp.max(ties - need) > 0)
    jmax = lax.cond(any_tie, tie_search, lambda: full(seq_len))

    q = q_ref[0]
    q_stack = [jnp.concatenate([q[:, (2 * j) * LANES:(2 * j + 1) * LANES],
                                q[:, (2 * j + 1) * LANES:(2 * j + 2) * LANES]], axis=0) for j in range(npair)]
    n_far = jnp.maximum((i - NEAR_BLOCKS + 1) // per, 0)

    def fold(x):
        return x.reshape(ck // SUBLANES, SUBLANES, LANES)

    def head_logits(c, off, madd, near):
        tidx = [jnp.clip(i - (c * per + jj), 0, NEAR_BLOCKS) for jj in range(per)]
        xs = []
        for j in range(npair):
            g = j // (npair // C_KV_HEADS)
            lg = _mm_nt(k_ref[0, pl.ds(off, ck), g * LANES:(g + 1) * LANES], q_stack[j])
            for a in range(2):
                x = lg[:, a * LANES:(a + 1) * LANES] + madd
                if near:
                    x = x + jnp.concatenate([tab_ref[2 * j + a, tidx[jj]] for jj in range(per)], axis=0)
                xs.append(x)
        return xs

    def max_body(near):
        def body(c, mx):
            off = chunk_off(c)
            key = keys_ref[pl.ds(off, ck), :]
            s_idx = off + kiota
            sel = ((key > vkey) | ((key == vkey) & (s_idx <= jmax))) & (s_idx <= t_row)
            madd = jnp.where(sel, 0.0, NEG_INF)
            keys_ref[pl.ds(off, ck), :] = pltpu.bitcast(madd, I32)
            xs = head_logits(c, off, madd, near)
            return tuple(jnp.maximum(mx[hd], jnp.max(fold(xs[hd]), axis=0)) for hd in range(C_HEADS))

        return body

    mx = tuple(jnp.full((SUBLANES, LANES), NEG_INF, F32) for _ in range(C_HEADS))
    mx = lax.fori_loop(0, n_far, max_body(False), mx)
    mx = lax.fori_loop(n_far, nch, max_body(True), mx)
    m = [jnp.max(v, axis=0, keepdims=True) for v in mx]

    acc_ref[...] = jnp.zeros(acc_ref.shape, F32)

    def att_body(near):
        def body(c, ls):
            off = chunk_off(c)
            madd = pltpu.bitcast(keys_ref[pl.ds(off, ck), :], F32)
            xs = head_logits(c, off, madd, near)
            out = []
            for j in range(npair):
                g = j // (npair // C_KV_HEADS)
                ps = []
                for a in range(2):
                    hd = 2 * j + a
                    p = jnp.exp(xs[hd] - m[hd])
                    out.append(ls[hd] + jnp.sum(fold(p), axis=0))
                    ps.append(p.astype(BF16))
                vt = vt_ref[0, g * LANES:(g + 1) * LANES, pl.ds(off, ck)]
                acc_ref[j] += _mm(vt, jnp.concatenate(ps, axis=1))
            return tuple(out)

        return body

    ls = tuple(jnp.zeros((SUBLANES, LANES), F32) for _ in range(C_HEADS))
    ls = lax.fori_loop(0, n_far, att_body(False), ls)
    ls = lax.fori_loop(n_far, nch, att_body(True), ls)

    for j in range(npair):
        for a in range(2):
            hd = 2 * j + a
            sl = slice(hd * LANES, (hd + 1) * LANES)
            ot = acc_ref[j, :, a * LANES:(a + 1) * LANES] / jnp.sum(ls[hd], axis=0, keepdims=True)
            o_ref[0, :, sl] = (ot.T * sg_ref[0, :, sl].astype(F32)).astype(BF16)


def _dsa(q, qi, wt, sg, k, vt, ki2, tab):
    b, l, cw = q.shape
    nb = l // BLOCK
    topk = min(TOPK_MAX, l // 4)

    def blk(n):
        return pl.BlockSpec((1, BLOCK, n), lambda bb, i: (bb, i, 0))

    def whole(s1, s2):
        return pl.BlockSpec((1, s1, s2), lambda bb, i: (bb, 0, 0))

    return pl.pallas_call(
        functools.partial(_dsa_kernel, seq_len=l, topk=topk),
        grid=(b, nb),
        in_specs=[blk(cw), blk(qi.shape[2]), pl.BlockSpec((1, IDX_HEADS, BLOCK), lambda bb, i: (bb, 0, i)), blk(cw),
                  whole(l, k.shape[2]), whole(vt.shape[1], l), whole(l, ki2.shape[2]), _const_spec(tab.shape)],
        out_specs=blk(cw),
        out_shape=jax.ShapeDtypeStruct((b, l, cw), BF16),
        scratch_shapes=[pltpu.VMEM((l, LANES), I32), pltpu.VMEM((C_HEADS // 2, C_HEAD_DIM, 2 * LANES), F32)],
        compiler_params=_params("arbitrary", "arbitrary"),
        name="dsa",
    )(q, qi, wt, sg, k, vt, ki2, tab)


def _out_kernel(h_ref, a_ref, w_ref, o_ref):
    o_ref[...] = h_ref[...] + _mm(a_ref[...], w_ref[...])


def _outproj(h2, a2, w):
    rows, d = h2.shape
    t = ROW_TILE
    return pl.pallas_call(
        _out_kernel,
        grid=(rows // t,),
        in_specs=[pl.BlockSpec((t, d), lambda i: (i, 0)), pl.BlockSpec((t, a2.shape[1]), lambda i: (i, 0)),
                  _const_spec(w.shape)],
        out_specs=pl.BlockSpec((t, d), lambda i: (i, 0)),
        out_shape=jax.ShapeDtypeStruct((rows, d), F32),
        compiler_params=_params("arbitrary"),
        name="outproj1",
    )(h2, a2, w)


def _bias_tables(rel_bias, seq_len):
    del seq_len
    rel_bias = rel_bias.astype(F32)

    def lookup(dist):
        bucket = _t5_bucket(dist)[None]
        out = jnp.zeros((rel_bias.shape[1],) + dist.shape, F32)
        for b in range(NUM_BUCKETS):
            out = jnp.where(bucket == b, rel_bias[b].reshape((-1,) + (1,) * dist.ndim), out)
        return out

    i = jnp.arange(BLOCK, dtype=I32)[:, None]
    j = jnp.arange(2 * BLOCK, dtype=I32)[None, :]
    bias0 = lookup(i + BLOCK - j)
    kl = jnp.arange(BLOCK, dtype=I32)[:, None]
    ql = jnp.arange(BLOCK, dtype=I32)[None, :]
    delta = jnp.arange(NEAR_BLOCKS, dtype=I32)[:, None, None]
    tab = lookup(delta * BLOCK + ql - kl) - rel_bias[NUM_BUCKETS - 1][:, None, None, None]
    tab = jnp.concatenate([tab, jnp.zeros((tab.shape[0], 1, BLOCK, BLOCK), F32)], axis=1)
    return bias0, tab


def kernel(x, rel_bias, norm_g, ev_w_in, ev_w_out, ev_q_norm_g, ev_k_norm_g, ev_sinks, ev_ssm_log_dt, ev_ssm_a_re,
           ev_ssm_a_im, ev_ssm_b_re, ev_ssm_b_im, ev_ssm_c_re, ev_ssm_c_im, ev_ssm_d, ev_glu_w, ev_glu_b, od_w_in,
           od_w_out, od_q_norm_g, od_k_norm_g):
    b, l, d = x.shape
    assert l % KEY_CHUNK == 0 and l % ROW_TILE == 0
    assert (NEAR_BLOCKS - 1) * BLOCK + 1 >= 16 * 64 ** (15 / 16) + 1
    bias0, tab = _bias_tables(rel_bias, l)

    q0, k0, v0, sga, u, sgb = _proj0(x.reshape(b * l, d), norm_g[0][None, :], ev_w_in[0].astype(BF16))
    shp = lambda a: a.reshape(b, l, a.shape[-1])
    qg2 = jnp.tile(ev_q_norm_g[0], 2)[None, :]
    kg2 = jnp.tile(ev_k_norm_g[0], 2)[None, :]
    sinks = jnp.broadcast_to(ev_sinks[0][:, None], (A_HEADS, LANES)).astype(F32)
    att0 = _attn0(shp(q0), shp(k0), shp(v0), shp(sga), bias0, sinks, qg2, kg2)
    bmat, cre, cim, sc = _s5_prep(ev_ssm_log_dt[0], ev_ssm_a_re[0], ev_ssm_a_im[0], ev_ssm_b_re[0], ev_ssm_b_im[0],
                                  ev_ssm_c_re[0], ev_ssm_c_im[0])
    ssm0 = _ssm(shp(u), shp(sgb), bmat, cre, cim, sc, ev_ssm_d[0].reshape(1, -1), ev_glu_w[0].astype(BF16),
                ev_glu_b[0][None, :])

    w1 = od_w_in[0]
    cw = C_HEADS * C_HEAD_DIM
    ckv = C_KV_HEADS * C_HEAD_DIM
    o = np.cumsum([0, cw, ckv, ckv, cw, IDX_HEADS * IDX_DIM, IDX_DIM, IDX_HEADS])
    wq, wk, wv, wg, wqi, wki, ww = (w1[:, o[n]:o[n + 1]] for n in range(7))
    zki = jnp.zeros((d, LANES - IDX_DIM), w1.dtype)
    wki2 = jnp.concatenate([wki, zki, zki, wki], axis=1)
    bf = lambda a: a.astype(BF16)
    h1, q1, k1, vt1, sg1, qi1, ki2, wt1 = _mid(
        x, att0, ssm0, bf(ev_w_out[0]), norm_g[1][None, :], bf(wq), bf(wk), bf(wv.T), bf(wg), bf(wqi), bf(wki2),
        bf(ww.T), od_q_norm_g[0][None, :], od_k_norm_g[0][None, :])
    att1 = _dsa(q1, qi1, wt1, sg1, k1, vt1, ki2, tab)
    out = _outproj(h1.reshape(b * l, d), att1.reshape(b * l, cw), bf(od_w_out[0]))
    return out.reshape(b, l, d)
```

```python
import functools
import math

import jax
import jax.numpy as jnp
import numpy as np
from jax import lax
from jax.experimental import pallas as pl
from jax.experimental.pallas import tpu as pltpu

F32 = jnp.float32
BF16 = jnp.bfloat16
I32 = jnp.int32
I16 = jnp.int16

LANES = 128
SUBLANES = 8
PACK16 = 16
HALF16 = 1 << 15
VMEM_LIMIT = 56 * 1024 * 1024

BLOCK = 128
WINDOW = 128
A_HEADS = 8
A_HEAD_DIM = 64
A_KV_HEADS = 2
A_WIDTH = A_HEADS * A_HEAD_DIM
SSM_GROUP = 16
SSM_STATE = 64
C_HEADS = 8
C_HEAD_DIM = 128
C_KV_HEADS = 2
IDX_HEADS = 8
IDX_DIM = 64
TOPK_MAX = 256
NUM_BUCKETS = 32
REL_MAX_DIST = 1024
EPS = 1e-6
NEG_INF = -1e30
INT_MIN = -(2 ** 31)
KEY_MIN_NORMAL = 0x00800000
KEY_POS_INF = 0x7F800000
KEY_NEG_INF = INT_MIN + 0x00800000

ROW_TILE = 256
KEY_CHUNK = 512
NEAR_BLOCKS = 8
NT_DIMS = (((1,), (1,)), ((), ()))


def _t5_bucket(dist):
    n = jnp.maximum(dist, 0)
    max_exact = NUM_BUCKETS // 2
    nf = jnp.maximum(n, 1).astype(F32)
    large = max_exact + (jnp.log(nf / max_exact) / math.log(REL_MAX_DIST / max_exact)
                         * (NUM_BUCKETS - max_exact)).astype(I32)
    large = jnp.minimum(large, NUM_BUCKETS - 1)
    return jnp.where(n < max_exact, n, large)


def _silu(x):
    return x * jax.nn.sigmoid(x)


def _rms(x, g):
    ms = jnp.mean(x * x, axis=-1, keepdims=True)
    return x * lax.rsqrt(ms + EPS) * g


def _mm(a, b):
    return jnp.dot(a, b, preferred_element_type=F32)


def _mm_nt(a, b):
    return lax.dot_general(a, b, NT_DIMS, preferred_element_type=F32)


def _params(*sem):
    return pltpu.CompilerParams(dimension_semantics=sem, vmem_limit_bytes=VMEM_LIMIT)


def _const_spec(shape):
    zeros = (0,) * len(shape)
    return pl.BlockSpec(shape, lambda *_: zeros)


def _proj0_kernel(x_ref, g_ref, w_ref, q_ref, k_ref, v_ref, sga_ref, u_ref, sgb_ref):
    hn = _rms(x_ref[...], g_ref[...]).astype(BF16)

    def mm(lo, hi):
        return _mm(hn, w_ref[:, lo:hi])

    q_ref[...] = mm(0, 512)
    k_ref[...] = mm(512, 640)
    v_ref[...] = mm(640, 768)
    sga_ref[...] = _silu(mm(768, 1280)).astype(BF16)
    u_ref[...] = mm(1280, 1792)
    sgb_ref[...] = _silu(mm(1792, 2304)).astype(BF16)


def _proj0(x2, g, w):
    rows, d = x2.shape
    t = ROW_TILE

    def row(n):
        return pl.BlockSpec((t, n), lambda i: (i, 0))

    return pl.pallas_call(
        _proj0_kernel,
        grid=(rows // t,),
        in_specs=[row(d), _const_spec((1, d)), _const_spec(w.shape)],
        out_specs=[row(512), row(128), row(128), row(512), row(512), row(512)],
        out_shape=[jax.ShapeDtypeStruct((rows, 512), F32), jax.ShapeDtypeStruct((rows, 128), F32),
                   jax.ShapeDtypeStruct((rows, 128), F32), jax.ShapeDtypeStruct((rows, 512), BF16),
                   jax.ShapeDtypeStruct((rows, 512), F32), jax.ShapeDtypeStruct((rows, 512), BF16)],
        compiler_params=_params("arbitrary"),
        name="proj0",
    )(x2, g, w)


def _attn0_kernel(q_ref, kc_ref, kp_ref, vc_ref, vp_ref, sga_ref, bias_ref, sink_ref, qg_ref, kg_ref, o_ref):
    i = pl.program_id(1)
    lane = lax.broadcasted_iota(I32, (1, LANES), 1)
    lo = lane < A_HEAD_DIM

    def segnorm(x, g2):
        sq = x * x
        s_lo = jnp.sum(jnp.where(lo, sq, 0.0), axis=-1, keepdims=True)
        s_hi = jnp.sum(jnp.where(lo, 0.0, sq), axis=-1, keepdims=True)
        inv = jnp.where(lo, lax.rsqrt(s_lo / A_HEAD_DIM + EPS), lax.rsqrt(s_hi / A_HEAD_DIM + EPS))
        return x * inv * g2

    kn = segnorm(jnp.concatenate([kp_ref[0], kc_ref[0]], axis=0), kg_ref[...])
    vb = jnp.concatenate([vp_ref[0], vc_ref[0]], axis=0)
    kr = pltpu.roll(kn, A_HEAD_DIM, axis=1)
    vr = pltpu.roll(vb, A_HEAD_DIM, axis=1)

    def variants(x, xr):
        return {(0, 0): jnp.where(lo, x, 0.0).astype(BF16), (0, 1): jnp.where(lo, 0.0, xr).astype(BF16),
                (1, 0): jnp.where(lo, xr, 0.0).astype(BF16), (1, 1): jnp.where(lo, 0.0, x).astype(BF16)}

    kvar = variants(kn, kr)
    vvar = variants(vb, vr)

    row = lax.broadcasted_iota(I32, (BLOCK, 2 * BLOCK), 0)
    col = lax.broadcasted_iota(I32, (BLOCK, 2 * BLOCK), 1)
    d = row + BLOCK - col
    mask = (d >= 0) & (d < WINDOW) & ((i > 0) | (col >= BLOCK))

    lgs, sinks = [], []
    for p in range(A_HEADS // 2):
        qp = (segnorm(q_ref[0, :, p * LANES:(p + 1) * LANES], qg_ref[...]) * (A_HEAD_DIM ** -0.5)).astype(BF16)
        for a in range(2):
            h = 2 * p + a
            lgs.append(jnp.where(mask, _mm_nt(qp, kvar[(p // 2, a)]) + bias_ref[h], NEG_INF))
            sinks.append(jnp.broadcast_to(sink_ref[h:h + 1, 0:1], (BLOCK, 1)))
    lg = jnp.concatenate(lgs, axis=0)
    sink = jnp.concatenate(sinks, axis=0)
    m = jnp.maximum(jnp.max(lg, axis=-1, keepdims=True), sink)
    e = jnp.exp(lg - m)
    den = jnp.sum(e, axis=-1, keepdims=True) + jnp.exp(sink - m)
    pr = (e / den).astype(BF16)
    for p in range(A_HEADS // 2):
        sl = slice(p * LANES, (p + 1) * LANES)
        acc = jnp.zeros((BLOCK, LANES), F32)
        for a in range(2):
            h = 2 * p + a
            acc = acc + _mm(pr[h * BLOCK:(h + 1) * BLOCK], vvar[(p // 2, a)])
        o_ref[0, :, sl] = (acc * sga_ref[0, :, sl].astype(F32)).astype(BF16)


def _attn0(q, k, v, sga, bias0, sinks, qg2, kg2):
    b, l, _ = q.shape
    nb = l // BLOCK

    def cur(n):
        return pl.BlockSpec((1, BLOCK, n), lambda bb, i: (bb, i, 0))

    def prev(n):
        return pl.BlockSpec((1, BLOCK, n), lambda bb, i: (bb, jnp.maximum(i - 1, 0), 0))

    return pl.pallas_call(
        _attn0_kernel,
        grid=(b, nb),
        in_specs=[cur(512), cur(128), prev(128), cur(128), prev(128), cur(512),
                  _const_spec(bias0.shape), _const_spec(sinks.shape), _const_spec(qg2.shape), _const_spec(kg2.shape)],
        out_specs=cur(512),
        out_shape=jax.ShapeDtypeStruct((b, l, 512), BF16),
        compiler_params=_params("arbitrary", "arbitrary"),
        name="attn0",
    )(q, k, k, v, v, sga, bias0, sinks, qg2, kg2)


def _ssm_kernel(u_ref, sgb_ref, bmat_ref, cre_ref, cim_ref, sc_ref, d_ref, gw_ref, gb_ref, o_ref, xre_ref, xim_ref):
    t = u_ref.shape[1]
    nq = bmat_ref.shape[0]
    half = bmat_ref.shape[2] // 2

    @pl.when(pl.program_id(1) == 0)
    def _():
        xre_ref[0:SUBLANES, :] = jnp.zeros((SUBLANES, xre_ref.shape[1]), F32)
        xim_ref[0:SUBLANES, :] = jnp.zeros((SUBLANES, xim_ref.shape[1]), F32)

    u = u_ref[0]
    ub = u.astype(BF16)
    for q in range(nq):
        bu = _mm(ub[:, q * LANES:(q + 1) * LANES], bmat_ref[q])
        xre_ref[SUBLANES:, q * half:(q + 1) * half] = bu[:, :half]
        xim_ref[SUBLANES:, q * half:(q + 1) * half] = bu[:, half:]

    def scan(r, _):
        base = pl.multiple_of(SUBLANES + r * SUBLANES, SUBLANES)
        xr = xre_ref[pl.ds(base, SUBLANES), :]
        xi = xim_ref[pl.ds(base, SUBLANES), :]
        for s, k in enumerate((1, 2, 4)):
            ar = sc_ref[2 * s]
            ai = sc_ref[2 * s + 1]
            sr = pltpu.roll(xr, k, axis=0)
            si = pltpu.roll(xi, k, axis=0)
            xr, xi = xr + ar * sr - ai * si, xi + ar * si + ai * sr
        cr = xre_ref[pl.ds(base - 1, 1), :]
        ci = xim_ref[pl.ds(base - 1, 1), :]
        pr = sc_ref[6]
        pi = sc_ref[7]
        xre_ref[pl.ds(base, SUBLANES), :] = xr + pr * cr - pi * ci
        xim_ref[pl.ds(base, SUBLANES), :] = xi + pr * ci + pi * cr
        return 0

    lax.fori_loop(0, t // SUBLANES, scan, 0, unroll=2)
    xre_ref[0:SUBLANES, :] = xre_ref[t:t + SUBLANES, :]
    xim_ref[0:SUBLANES, :] = xim_ref[t:t + SUBLANES, :]

    ys = []
    for q in range(nq):
        xr = xre_ref[SUBLANES:, q * half:(q + 1) * half].astype(BF16)
        xi = xim_ref[SUBLANES:, q * half:(q + 1) * half].astype(BF16)
        ys.append(_mm(xr, cre_ref[q]) + _mm(xi, cim_ref[q]))
    y = jnp.concatenate(ys, axis=1) + d_ref[...] * u
    y = jax.nn.gelu(y).astype(BF16)
    hh = _mm(y, gw_ref[...]) + gb_ref[...]
    w = hh.shape[1] // 2
    o_ref[0] = (hh[:, :w] * jax.nn.sigmoid(hh[:, w:]) * sgb_ref[0].astype(F32)).astype(BF16)


def _ssm(u, sgb, bmat, cre, cim, sc, dskip, gw, gb):
    b, l, w = u.shape
    t = ROW_TILE
    ns = sc.shape[-1]

    def row(n):
        return pl.BlockSpec((1, t, n), lambda bb, i: (bb, i, 0))

    return pl.pallas_call(
        _ssm_kernel,
        grid=(b, l // t),
        in_specs=[row(w), row(w), _const_spec(bmat.shape), _const_spec(cre.shape), _const_spec(cim.shape),
                  _const_spec(sc.shape), _const_spec(dskip.shape), _const_spec(gw.shape), _const_spec(gb.shape)],
        out_specs=row(w),
        out_shape=jax.ShapeDtypeStruct((b, l, w), BF16),
        scratch_shapes=[pltpu.VMEM((SUBLANES + t, ns), F32), pltpu.VMEM((SUBLANES + t, ns), F32)],
        compiler_params=_params("arbitrary", "arbitrary"),
        name="ssm",
    )(u, sgb, bmat, cre, cim, sc, dskip, gw, gb)


def _s5_prep(log_dt, a_re, a_im, b_re, b_im, c_re, c_im):
    g, p = a_re.shape
    h = b_re.shape[-1]
    gl = LANES // h
    nq = g // gl
    dt = jnp.exp(log_dt)[:, None]
    mag = jnp.exp(a_re * dt)
    ang = a_im * dt
    ab_re = mag * jnp.cos(ang)
    ab_im = mag * jnp.sin(ang)
    den = a_re * a_re + a_im * a_im
    n_re = ab_re - 1.0
    n_im = ab_im
    f_re = (n_re * a_re + n_im * a_im) / den
    f_im = (n_im * a_re - n_re * a_im) / den
    bb_re = f_re[..., None] * b_re - f_im[..., None] * b_im
    bb_im = f_re[..., None] * b_im + f_im[..., None] * b_re
    eye = jnp.eye(gl, dtype=F32)

    def bdiag_in(m):
        m = m.reshape(nq, gl, p, h)
        return jnp.einsum('qgph,gk->qghkp', m, eye).reshape(nq, gl * h, gl * p)

    def bdiag_out(m):
        m = m.reshape(nq, gl, h, p)
        return jnp.einsum('qghp,gk->qgpkh', m, eye).reshape(nq, gl * p, gl * h)

    bmat = jnp.concatenate([bdiag_in(bb_re), bdiag_in(bb_im)], axis=2).astype(BF16)
    cre = bdiag_out(c_re).astype(BF16)
    cim = bdiag_out(-c_im).astype(BF16)

    pw = [(ab_re.reshape(-1), ab_im.reshape(-1))]
    for _ in range(SUBLANES - 1):
        pr, pi = pw[-1]
        pw.append((pr * pw[0][0] - pi * pw[0][1], pr * pw[0][1] + pi * pw[0][0]))
    rows = jnp.arange(SUBLANES)[:, None]
    sc = []
    for k in (1, 2, 4):
        sc.append(jnp.where(rows >= k, pw[k - 1][0][None, :], 0.0))
        sc.append(jnp.where(rows >= k, pw[k - 1][1][None, :], 0.0))
    sc.append(jnp.stack([pw[r][0] for r in range(SUBLANES)]))
    sc.append(jnp.stack([pw[r][1] for r in range(SUBLANES)]))
    return bmat, cre, cim, jnp.stack(sc).astype(F32)


def _mid_kernel(x_ref, a_ref, s_ref, wo_ref, g_ref, wq_ref, wk_ref, wvt_ref, wg_ref, wqi_ref, wki_ref, wwt_ref,
                qg_ref, kg_ref, h_ref, q_ref, k_ref, vt_ref, sg_ref, qi_ref, ki_ref, wt_ref):
    aw = a_ref.shape[2]
    h = x_ref[0] + _mm(a_ref[0], wo_ref[0:aw, :]) + _mm(s_ref[0], wo_ref[aw:, :])
    h_ref[0] = h
    hn = _rms(h, g_ref[...]).astype(BF16)
    qf = _mm(hn, wq_ref[...])
    for hd in range(C_HEADS):
        sl = slice(hd * C_HEAD_DIM, (hd + 1) * C_HEAD_DIM)
        q_ref[0, :, sl] = (_rms(qf[:, sl], qg_ref[...]) * (C_HEAD_DIM ** -0.5)).astype(BF16)
    kf = _mm(hn, wk_ref[...])
    for hd in range(C_KV_HEADS):
        sl = slice(hd * C_HEAD_DIM, (hd + 1) * C_HEAD_DIM)
        k_ref[0, :, sl] = _rms(kf[:, sl], kg_ref[...]).astype(BF16)
    vt_ref[0] = _mm_nt(wvt_ref[...], hn).astype(BF16)
    sg_ref[0] = _silu(_mm(hn, wg_ref[...])).astype(BF16)
    qi_ref[0] = _mm(hn, wqi_ref[...]).astype(BF16)
    ki_ref[0] = _mm(hn, wki_ref[...]).astype(BF16)
    wt_ref[0] = _mm_nt(wwt_ref[...], hn) * ((IDX_HEADS ** -0.5) * (IDX_DIM ** -0.5))


def _mid(x, att0, ssm0, wo, g, wq, wk, wvt, wg, wqi, wki2, wwt, qg, kg):
    b, l, d = x.shape
    t = ROW_TILE

    def row(n):
        return pl.BlockSpec((1, t, n), lambda bb, i: (bb, i, 0))

    def col(n):
        return pl.BlockSpec((1, n, t), lambda bb, i: (bb, 0, i))

    weights = [wo, g, wq, wk, wvt, wg, wqi, wki2, wwt, qg, kg]
    cw = C_HEADS * C_HEAD_DIM
    ckv = C_KV_HEADS * C_HEAD_DIM
    return pl.pallas_call(
        _mid_kernel,
        grid=(b, l // t),
        in_specs=[row(d), row(att0.shape[2]), row(ssm0.shape[2])] + [_const_spec(w.shape) for w in weights],
        out_specs=[row(d), row(cw), row(ckv), col(ckv), row(cw), row(IDX_HEADS * IDX_DIM), row(2 * LANES),
                   col(IDX_HEADS)],
        out_shape=[jax.ShapeDtypeStruct((b, l, d), F32), jax.ShapeDtypeStruct((b, l, cw), BF16),
                   jax.ShapeDtypeStruct((b, l, ckv), BF16), jax.ShapeDtypeStruct((b, ckv, l), BF16),
                   jax.ShapeDtypeStruct((b, l, cw), BF16), jax.ShapeDtypeStruct((b, l, IDX_HEADS * IDX_DIM), BF16),
                   jax.ShapeDtypeStruct((b, l, 2 * LANES), BF16), jax.ShapeDtypeStruct((b, IDX_HEADS, l), F32)],
        compiler_params=_params("arbitrary", "arbitrary"),
        name="mid",
    )(x, att0, ssm0, *weights)


def _dsa_kernel(q_ref, qi_ref, wt_ref, sg_ref, k_ref, vt_ref, ki_ref, tab_ref, o_ref,
                sc_ref, hi16_ref, lo16_ref, acc_ref, *, seq_len, topk):
    i = pl.program_id(1)
    ck = KEY_CHUNK
    per = ck // BLOCK
    nch = (i + per) // per
    t_row = i * BLOCK + lax.broadcasted_iota(I32, (1, LANES), 1)
    kiota = lax.broadcasted_iota(I32, (ck, LANES), 0)
    npair = C_HEADS // 2

    def chunk_off(c):
        return pl.multiple_of(c * ck, ck)

    qi = qi_ref[0]
    qi_stack = [jnp.concatenate([qi[:, (2 * s) * LANES:(2 * s + 1) * LANES],
                                 qi[:, (2 * s + 1) * LANES:(2 * s + 2) * LANES]], axis=0) for s in range(2)]
    wt = wt_ref[0]

    def score_chunk(c, masked):
        off = chunk_off(c)
        sc = jnp.zeros((ck, LANES), F32)
        for a in range(2):
            kk = ki_ref[0, pl.ds(off, ck), a * LANES:(a + 1) * LANES]
            for s in range(2):
                r = _mm_nt(kk, qi_stack[s])
                for j in range(2):
                    hd = 2 * (2 * s + j) + a
                    sc = sc + jnp.maximum(r[:, j * LANES:(j + 1) * LANES], 0.0) * wt[hd:hd + 1, :]
        if masked:
            sc = jnp.where(off + kiota <= t_row, sc, NEG_INF)
        sc_ref[pl.ds(off, ck), :] = sc
        bits = pltpu.bitcast(sc, I32)
        key = jnp.where(bits < 0, INT_MIN - bits, bits)
        hi16_ref[pl.ds(off, ck), :] = (key >> 16).astype(I16)
        lo16_ref[pl.ds(off, ck), :] = ((key & 0xFFFF) - HALF16).astype(I16)

    def score_body(c, _):
        score_chunk(c, False)
        return 0

    lax.fori_loop(0, nch - 1, score_body, 0)
    score_chunk(nch - 1, True)

    @pl.when(nch % 2 == 1)
    def _():
        off = chunk_off(nch)
        hi16_ref[pl.ds(off, ck), :] = jnp.full((ck, LANES), -HALF16, I16)
        lo16_ref[pl.ds(off, ck), :] = jnp.full((ck, LANES), -HALF16, I16)

    nch2 = (nch + 1) // 2

    def count(pred):
        def body(c, acc):
            off = chunk_off(c)
            ind = pred(sc_ref[pl.ds(off, ck), :], off).astype(I32)
            return acc + jnp.sum(ind.reshape(ck // SUBLANES, SUBLANES, LANES), axis=0)

        acc = lax.fori_loop(0, nch, body, jnp.zeros((SUBLANES, LANES), I32))
        return jnp.sum(acc, axis=0, keepdims=True)

    def key_value(k):
        return pltpu.bitcast(jnp.where(k < 0, INT_MIN - k, k), F32)

    def count_ge(k):
        thr = key_value(k)
        return count(lambda s, off: s >= thr)

    def count16(ref, thr):
        thr16 = thr.astype(I16)

        def body(c, acc):
            k = ref[pl.ds(pl.multiple_of(c * (2 * ck), 2 * ck), 2 * ck), :]
            ind = jnp.where(k >= thr16, jnp.int16(1), jnp.int16(0))
            parts = [ind[r * PACK16:(r + 1) * PACK16] for r in range(2 * ck // PACK16)]
            while len(parts) > 1:
                parts = [a + b for a, b in zip(parts[::2], parts[1::2])]
            return acc + parts[0]

        acc = lax.fori_loop(0, nch2, body, jnp.zeros((PACK16, LANES), I16))
        return jnp.sum(acc.astype(I32), axis=0, keepdims=True)

    def full(v):
        return jnp.full((1, LANES), v, I32)

    def bisect16(ref, want):
        def step(_, st):
            lo, hi, clo, chi = st
            mid = (lo + hi) >> 1
            c = count16(ref, mid)
            ge = c >= want
            return jnp.where(ge, mid, lo), jnp.where(ge, hi, mid), jnp.where(ge, c, clo), jnp.where(ge, chi, c)

        lo, _, clo, chi = lax.fori_loop(0, nbis, step, (full(-HALF16), full(HALF16), full(0) + nch * ck, full(0)))
        return lo, clo, chi

    searching = (i + 1) * BLOCK > topk
    nbis = jnp.where(searching, 16, 0)
    vhi, _, above_hi = bisect16(hi16_ref, full(topk))
    vhi16 = vhi.astype(I16)

    def mask_low(c, _):
        off = chunk_off(c)
        lo16_ref[pl.ds(off, ck), :] = jnp.where(hi16_ref[pl.ds(off, ck), :] == vhi16, lo16_ref[pl.ds(off, ck), :],
                                                jnp.int16(-HALF16))
        return 0

    lax.fori_loop(0, jnp.where(searching, nch, 0), mask_low, 0)
    vlo, _, _ = bisect16(lo16_ref, topk - above_hi)
    guess = (vhi << 16) + (vlo + HALF16)

    def verify():
        above = guess + 1
        above = jnp.where((above > 0) & (above < KEY_MIN_NORMAL), KEY_MIN_NORMAL, above)
        c_lo, c_hi = count_ge(guess), count_ge(above)
        ok = (c_lo >= topk) & (c_hi < topk)
        st = (jnp.where(ok, guess, KEY_NEG_INF), jnp.where(ok, above, KEY_POS_INF + 1),
              jnp.where(ok, c_lo, nch * ck), jnp.where(ok, c_hi, 0))

        def wide(lo, hi):
            open_ = (lo + 1 < hi) & ~((lo >= 0) & (hi <= KEY_MIN_NORMAL))
            return jnp.max(jnp.where(open_, 1, 0)) > 0

        def step(carry):
            lo, hi, c_lo, c_hi = carry[:4]
            mid = (lo >> 1) + (hi >> 1) + (lo & hi & 1)
            c = count_ge(mid)
            ge = c >= topk
            lo, hi = jnp.where(ge, mid, lo), jnp.where(ge, hi, mid)
            return lo, hi, jnp.where(ge, c, c_lo), jnp.where(ge, c_hi, c), wide(lo, hi)

        out = lax.while_loop(lambda carry: carry[4], step, st + (wide(st[0], st[1]),))
        return out[0], out[2], out[3]

    vkey, c_lo, c_hi = lax.cond(searching, verify, lambda: (full(KEY_NEG_INF), full(topk), full(0)))
    vthr = key_value(vkey)
    need = topk - c_hi
    ties = c_lo - c_hi

    def tie_search():
        nxt = vkey + 1
        nxt = jnp.where((nxt > 0) & (nxt < KEY_MIN_NORMAL), KEY_MIN_NORMAL, nxt)
        step = key_value(nxt) - vthr

        def split(_, st):
            fl, fh = st
            fm = 0.5 * (fl + fh)
            thr = vthr + fm * step
            ge = count(lambda s, off: s >= thr) >= topk
            return jnp.where(ge, fm, fl), jnp.where(ge, fh, fm)

        fl, _ = lax.fori_loop(0, 26, split, (jnp.zeros((1, LANES), F32), jnp.ones((1, LANES), F32)))
        thr = vthr + fl * step
        want = topk - count(lambda s, off: s > thr)

        def body(_, st):
            lj, hj = st
            mid = (lj + hj) >> 1
            c = count(lambda s, off: (s == thr) & (off + kiota <= mid))
            ok = c >= want
            return jnp.where(ok, lj, mid), jnp.where(ok, mid, hj)

        _, hj = lax.fori_loop(0, 14, body, (full(-1), full(0) + (nch * ck - 1)))
        return thr, hj

    any_tie = searching & (jnp.max(ties - need) > 0)
    vthr, jmax = lax.cond(any_tie, tie_search, lambda: (vthr, full(seq_len)))

    q = q_ref[0]
    q_stack = [jnp.concatenate([q[:, (2 * j) * LANES:(2 * j + 1) * LANES],
                                q[:, (2 * j + 1) * LANES:(2 * j + 2) * LANES]], axis=0) for j in range(npair)]
    n_far = jnp.maximum((i - NEAR_BLOCKS + 1) // per, 0)

    def fold(x):
        return x.reshape(ck // SUBLANES, SUBLANES, LANES)

    def head_logits(c, off, madd, near):
        tidx = [jnp.clip(i - (c * per + jj), 0, NEAR_BLOCKS) for jj in range(per)]
        xs = []
        for j in range(npair):
            g = j // (npair // C_KV_HEADS)
            lg = _mm_nt(k_ref[0, pl.ds(off, ck), g * LANES:(g + 1) * LANES], q_stack[j])
            for a in range(2):
                x = lg[:, a * LANES:(a + 1) * LANES] + madd
                if near:
                    x = x + jnp.concatenate([tab_ref[2 * j + a, tidx[jj]] for jj in range(per)], axis=0)
                xs.append(x)
        return xs

    def max_body(near):
        def body(c, mx):
            off = chunk_off(c)
            s = sc_ref[pl.ds(off, ck), :]
            s_idx = off + kiota
            sel = ((s > vthr) | ((s == vthr) & (s_idx <= jmax))) & (s_idx <= t_row)
            madd = jnp.where(sel, 0.0, NEG_INF)
            sc_ref[pl.ds(off, ck), :] = madd
            xs = head_logits(c, off, madd, near)
            return tuple(jnp.maximum(mx[hd], jnp.max(fold(xs[hd]), axis=0)) for hd in range(C_HEADS))

        return body

    mx = tuple(jnp.full((SUBLANES, LANES), NEG_INF, F32) for _ in range(C_HEADS))
    mx = lax.fori_loop(0, n_far, max_body(False), mx)
    mx = lax.fori_loop(n_far, nch, max_body(True), mx)
    m = [jnp.max(v, axis=0, keepdims=True) for v in mx]

    acc_ref[...] = jnp.zeros(acc_ref.shape, F32)

    def att_body(near):
        def body(c, ls):
            off = chunk_off(c)
            madd = sc_ref[pl.ds(off, ck), :]
            xs = head_logits(c, off, madd, near)
            out = []
            for j in range(npair):
                g = j // (npair // C_KV_HEADS)
                ps = []
                for a in range(2):
                    hd = 2 * j + a
                    p = jnp.exp(xs[hd] - m[hd])
                    out.append(ls[hd] + jnp.sum(fold(p), axis=0))
                    ps.append(p.astype(BF16))
                vt = vt_ref[0, g * LANES:(g + 1) * LANES, pl.ds(off, ck)]
                acc_ref[j] += _mm(vt, jnp.concatenate(ps, axis=1))
            return tuple(out)

        return body

    ls = tuple(jnp.zeros((SUBLANES, LANES), F32) for _ in range(C_HEADS))
    ls = lax.fori_loop(0, n_far, att_body(False), ls)
    ls = lax.fori_loop(n_far, nch, att_body(True), ls)

    for j in range(npair):
        for a in range(2):
            hd = 2 * j + a
            sl = slice(hd * LANES, (hd + 1) * LANES)
            ot = acc_ref[j, :, a * LANES:(a + 1) * LANES] / jnp.sum(ls[hd], axis=0, keepdims=True)
            o_ref[0, :, sl] = (ot.T * sg_ref[0, :, sl].astype(F32)).astype(BF16)


def _dsa(q, qi, wt, sg, k, vt, ki2, tab):
    b, l, cw = q.shape
    nb = l // BLOCK
    topk = min(TOPK_MAX, l // 4)

    def blk(n):
        return pl.BlockSpec((1, BLOCK, n), lambda bb, i: (bb, i, 0))

    def whole(s1, s2):
        return pl.BlockSpec((1, s1, s2), lambda bb, i: (bb, 0, 0))

    return pl.pallas_call(
        functools.partial(_dsa_kernel, seq_len=l, topk=topk),
        grid=(b, nb),
        in_specs=[blk(cw), blk(qi.shape[2]), pl.BlockSpec((1, IDX_HEADS, BLOCK), lambda bb, i: (bb, 0, i)), blk(cw),
                  whole(l, k.shape[2]), whole(vt.shape[1], l), whole(l, ki2.shape[2]), _const_spec(tab.shape)],
        out_specs=blk(cw),
        out_shape=jax.ShapeDtypeStruct((b, l, cw), BF16),
        scratch_shapes=[pltpu.VMEM((l, LANES), F32), pltpu.VMEM((l, LANES), I16), pltpu.VMEM((l, LANES), I16),
                        pltpu.VMEM((C_HEADS // 2, C_HEAD_DIM, 2 * LANES), F32)],
        compiler_params=_params("arbitrary", "arbitrary"),
        name="dsa",
    )(q, qi, wt, sg, k, vt, ki2, tab)


def _out_kernel(h_ref, a_ref, w_ref, o_ref):
    o_ref[...] = h_ref[...] + _mm(a_ref[...], w_ref[...])


def _outproj(h2, a2, w):
    rows, d = h2.shape
    t = ROW_TILE
    return pl.pallas_call(
        _out_kernel,
        grid=(rows // t,),
        in_specs=[pl.BlockSpec((t, d), lambda i: (i, 0)), pl.BlockSpec((t, a2.shape[1]), lambda i: (i, 0)),
                  _const_spec(w.shape)],
        out_specs=pl.BlockSpec((t, d), lambda i: (i, 0)),
        out_shape=jax.ShapeDtypeStruct((rows, d), F32),
        compiler_params=_params("arbitrary"),
        name="outproj1",
    )(h2, a2, w)


def _bias_tables(rel_bias, seq_len):
    del seq_len
    nv = (NEAR_BLOCKS + 1) * BLOCK
    vec = rel_bias[_t5_bucket(jnp.arange(nv, dtype=I32))].astype(F32).T

    def window(lo, n):
        pad = max(0, -lo)
        body = vec[:, max(lo, 0):lo + n]
        return jnp.concatenate([jnp.broadcast_to(vec[:, :1], (vec.shape[0], pad)), body], axis=1)

    def toeplitz(g, rows, cols):
        w = rows + cols
        g2 = jnp.concatenate([g[:, rows - 1:rows - 1 + cols], g[:, :1], g[:, :rows - 1]], axis=1)
        flat = jnp.tile(g2, (1, rows))[:, :rows * (w - 1)]
        return flat.reshape(-1, rows, w - 1)[:, :, :cols]

    bias0 = jnp.transpose(toeplitz(window(BLOCK - (2 * BLOCK - 1), 3 * BLOCK - 1), 2 * BLOCK, BLOCK), (0, 2, 1))
    tiles = [toeplitz(window(dl * BLOCK - (BLOCK - 1), 2 * BLOCK - 1), BLOCK, BLOCK) for dl in range(NEAR_BLOCKS)]
    tab = jnp.stack(tiles, axis=1) - rel_bias[NUM_BUCKETS - 1].astype(F32)[:, None, None, None]
    tab = jnp.concatenate([tab, jnp.zeros((tab.shape[0], 1, BLOCK, BLOCK), F32)], axis=1)
    return bias0, tab


def kernel(x, rel_bias, norm_g, ev_w_in, ev_w_out, ev_q_norm_g, ev_k_norm_g, ev_sinks, ev_ssm_log_dt, ev_ssm_a_re,
           ev_ssm_a_im, ev_ssm_b_re, ev_ssm_b_im, ev_ssm_c_re, ev_ssm_c_im, ev_ssm_d, ev_glu_w, ev_glu_b, od_w_in,
           od_w_out, od_q_norm_g, od_k_norm_g):
    b, l, d = x.shape
    assert l % (2 * KEY_CHUNK) == 0 and l % ROW_TILE == 0
    assert (NEAR_BLOCKS - 1) * BLOCK + 1 >= 16 * 64 ** (15 / 16) + 1
    bias0, tab = _bias_tables(rel_bias, l)

    q0, k0, v0, sga, u, sgb = _proj0(x.reshape(b * l, d), norm_g[0][None, :], ev_w_in[0].astype(BF16))
    shp = lambda a: a.reshape(b, l, a.shape[-1])
    qg2 = jnp.tile(ev_q_norm_g[0], 2)[None, :]
    kg2 = jnp.tile(ev_k_norm_g[0], 2)[None, :]
    sinks = jnp.broadcast_to(ev_sinks[0][:, None], (A_HEADS, LANES)).astype(F32)
    att0 = _attn0(shp(q0), shp(k0), shp(v0), shp(sga), bias0, sinks, qg2, kg2)
    bmat, cre, cim, sc = _s5_prep(ev_ssm_log_dt[0], ev_ssm_a_re[0], ev_ssm_a_im[0], ev_ssm_b_re[0], ev_ssm_b_im[0],
                                  ev_ssm_c_re[0], ev_ssm_c_im[0])
    ssm0 = _ssm(shp(u), shp(sgb), bmat, cre, cim, sc, ev_ssm_d[0].reshape(1, -1), ev_glu_w[0].astype(BF16),
                ev_glu_b[0][None, :])

    w1 = od_w_in[0]
    cw = C_HEADS * C_HEAD_DIM
    ckv = C_KV_HEADS * C_HEAD_DIM
    o = np.cumsum([0, cw, ckv, ckv, cw, IDX_HEADS * IDX_DIM, IDX_DIM, IDX_HEADS])
    wq, wk, wv, wg, wqi, wki, ww = (w1[:, o[n]:o[n + 1]] for n in range(7))
    zki = jnp.zeros((d, LANES - IDX_DIM), w1.dtype)
    wki2 = jnp.concatenate([wki, zki, zki, wki], axis=1)
    bf = lambda a: a.astype(BF16)
    h1, q1, k1, vt1, sg1, qi1, ki2, wt1 = _mid(
        x, att0, ssm0, bf(ev_w_out[0]), norm_g[1][None, :], bf(wq), bf(wk), bf(wv.T), bf(wg), bf(wqi), bf(wki2),
        bf(ww.T), od_q_norm_g[0][None, :], od_k_norm_g[0][None, :])
    att1 = _dsa(q1, qi1, wt1, sg1, k1, vt1, ki2, tab)
    out = _outproj(h1.reshape(b * l, d), att1.reshape(b * l, cw), bf(od_w_out[0]))
    return out.reshape(b, l, d)
```

```python
import functools
import math

import jax
import jax.numpy as jnp
import numpy as np
from jax import lax
from jax.experimental import pallas as pl
from jax.experimental.pallas import tpu as pltpu

F32 = jnp.float32
BF16 = jnp.bfloat16
I32 = jnp.int32

LANES = 128
SUBLANES = 8
VMEM_LIMIT = 56 * 1024 * 1024

BLOCK = 128
WINDOW = 128
A_HEADS = 8
A_HEAD_DIM = 64
A_KV_HEADS = 2
A_WIDTH = A_HEADS * A_HEAD_DIM
SSM_GROUP = 16
SSM_STATE = 64
C_HEADS = 8
C_HEAD_DIM = 128
C_KV_HEADS = 2
IDX_HEADS = 8
IDX_DIM = 64
TOPK_MAX = 256
NUM_BUCKETS = 32
REL_MAX_DIST = 1024
EPS = 1e-6
NEG_INF = -1e30
INT_MIN = -(2 ** 31)
KEY_MIN_NORMAL = 0x00800000
KEY_POS_INF = 0x7F800000
KEY_NEG_INF = INT_MIN + 0x00800000

ROW_TILE = 256
KEY_CHUNK = 1024
NEAR_BLOCKS = 8
FOLD_CHAINS = 8
COUNT_ROWS = 512
NT_DIMS = (((1,), (1,)), ((), ()))


def _t5_bucket(dist):
    n = jnp.maximum(dist, 0)
    max_exact = NUM_BUCKETS // 2
    nf = jnp.maximum(n, 1).astype(F32)
    large = max_exact + (jnp.log(nf / max_exact) / math.log(REL_MAX_DIST / max_exact)
                         * (NUM_BUCKETS - max_exact)).astype(I32)
    large = jnp.minimum(large, NUM_BUCKETS - 1)
    return jnp.where(n < max_exact, n, large)


def _silu(x):
    return x * jax.nn.sigmoid(x)


def _rms(x, g):
    ms = jnp.mean(x * x, axis=-1, keepdims=True)
    return x * lax.rsqrt(ms + EPS) * g


def _mm(a, b):
    return jnp.dot(a, b, preferred_element_type=F32)


def _mm_nt(a, b):
    return lax.dot_general(a, b, NT_DIMS, preferred_element_type=F32)


def _fold(x, op):
    n = x.shape[0] // SUBLANES
    chains = min(FOLD_CHAINS, n)
    accs = [x[r * SUBLANES:(r + 1) * SUBLANES] for r in range(chains)]
    for r in range(chains, n):
        accs[r % chains] = op(accs[r % chains], x[r * SUBLANES:(r + 1) * SUBLANES])
    while len(accs) > 1:
        accs = [op(a, b) for a, b in zip(accs[::2], accs[1::2])] + accs[len(accs) & ~1:]
    return accs[0]


def _params(*sem):
    return pltpu.CompilerParams(dimension_semantics=sem, vmem_limit_bytes=VMEM_LIMIT)


def _const_spec(shape):
    zeros = (0,) * len(shape)
    return pl.BlockSpec(shape, lambda *_: zeros)


def _proj0_kernel(x_ref, g_ref, w_ref, q_ref, k_ref, v_ref, sga_ref, u_ref, sgb_ref):
    hn = _rms(x_ref[...], g_ref[...]).astype(BF16)

    def mm(lo, hi):
        return _mm(hn, w_ref[:, lo:hi])

    q_ref[...] = mm(0, 512)
    k_ref[...] = mm(512, 640)
    v_ref[...] = mm(640, 768)
    sga_ref[...] = _silu(mm(768, 1280)).astype(BF16)
    u_ref[...] = mm(1280, 1792)
    sgb_ref[...] = _silu(mm(1792, 2304)).astype(BF16)


def _proj0(x2, g, w):
    rows, d = x2.shape
    t = ROW_TILE

    def row(n):
        return pl.BlockSpec((t, n), lambda i: (i, 0))

    return pl.pallas_call(
        _proj0_kernel,
        grid=(rows // t,),
        in_specs=[row(d), _const_spec((1, d)), _const_spec(w.shape)],
        out_specs=[row(512), row(128), row(128), row(512), row(512), row(512)],
        out_shape=[jax.ShapeDtypeStruct((rows, 512), F32), jax.ShapeDtypeStruct((rows, 128), F32),
                   jax.ShapeDtypeStruct((rows, 128), F32), jax.ShapeDtypeStruct((rows, 512), BF16),
                   jax.ShapeDtypeStruct((rows, 512), F32), jax.ShapeDtypeStruct((rows, 512), BF16)],
        compiler_params=_params("arbitrary"),
        name="proj0",
    )(x2, g, w)


def _attn0_kernel(q_ref, kc_ref, kp_ref, vc_ref, vp_ref, sga_ref, bias_ref, sink_ref, qg_ref, kg_ref, o_ref):
    i = pl.program_id(1)
    lane = lax.broadcasted_iota(I32, (1, LANES), 1)
    lo = lane < A_HEAD_DIM

    def segnorm(x, g2):
        sq = x * x
        s_lo = jnp.sum(jnp.where(lo, sq, 0.0), axis=-1, keepdims=True)
        s_hi = jnp.sum(jnp.where(lo, 0.0, sq), axis=-1, keepdims=True)
        inv = jnp.where(lo, lax.rsqrt(s_lo / A_HEAD_DIM + EPS), lax.rsqrt(s_hi / A_HEAD_DIM + EPS))
        return x * inv * g2

    kn = segnorm(jnp.concatenate([kp_ref[0], kc_ref[0]], axis=0), kg_ref[...])
    vb = jnp.concatenate([vp_ref[0], vc_ref[0]], axis=0)
    kr = pltpu.roll(kn, A_HEAD_DIM, axis=1)
    vr = pltpu.roll(vb, A_HEAD_DIM, axis=1)

    def variants(x, xr):
        return {(0, 0): jnp.where(lo, x, 0.0).astype(BF16), (0, 1): jnp.where(lo, 0.0, xr).astype(BF16),
                (1, 0): jnp.where(lo, xr, 0.0).astype(BF16), (1, 1): jnp.where(lo, 0.0, x).astype(BF16)}

    kvar = variants(kn, kr)
    vvar = variants(vb, vr)

    row = lax.broadcasted_iota(I32, (BLOCK, 2 * BLOCK), 0)
    col = lax.broadcasted_iota(I32, (BLOCK, 2 * BLOCK), 1)
    d = row + BLOCK - col
    mask = (d >= 0) & (d < WINDOW) & ((i > 0) | (col >= BLOCK))

    lgs, sinks = [], []
    for p in range(A_HEADS // 2):
        qp = (segnorm(q_ref[0, :, p * LANES:(p + 1) * LANES], qg_ref[...]) * (A_HEAD_DIM ** -0.5)).astype(BF16)
        for a in range(2):
            h = 2 * p + a
            lgs.append(jnp.where(mask, _mm_nt(qp, kvar[(p // 2, a)]) + bias_ref[h], NEG_INF))
            sinks.append(jnp.broadcast_to(sink_ref[h:h + 1, 0:1], (BLOCK, 1)))
    lg = jnp.concatenate(lgs, axis=0)
    sink = jnp.concatenate(sinks, axis=0)
    m = jnp.maximum(jnp.max(lg, axis=-1, keepdims=True), sink)
    e = jnp.exp(lg - m)
    den = jnp.sum(e, axis=-1, keepdims=True) + jnp.exp(sink - m)
    pr = (e / den).astype(BF16)
    for p in range(A_HEADS // 2):
        sl = slice(p * LANES, (p + 1) * LANES)
        acc = jnp.zeros((BLOCK, LANES), F32)
        for a in range(2):
            h = 2 * p + a
            acc = acc + _mm(pr[h * BLOCK:(h + 1) * BLOCK], vvar[(p // 2, a)])
        o_ref[0, :, sl] = (acc * sga_ref[0, :, sl].astype(F32)).astype(BF16)


def _attn0(q, k, v, sga, bias0, sinks, qg2, kg2):
    b, l, _ = q.shape
    nb = l // BLOCK

    def cur(n):
        return pl.BlockSpec((1, BLOCK, n), lambda bb, i: (bb, i, 0))

    def prev(n):
        return pl.BlockSpec((1, BLOCK, n), lambda bb, i: (bb, jnp.maximum(i - 1, 0), 0))

    return pl.pallas_call(
        _attn0_kernel,
        grid=(b, nb),
        in_specs=[cur(512), cur(128), prev(128), cur(128), prev(128), cur(512),
                  _const_spec(bias0.shape), _const_spec(sinks.shape), _const_spec(qg2.shape), _const_spec(kg2.shape)],
        out_specs=cur(512),
        out_shape=jax.ShapeDtypeStruct((b, l, 512), BF16),
        compiler_params=_params("arbitrary", "arbitrary"),
        name="attn0",
    )(q, k, k, v, v, sga, bias0, sinks, qg2, kg2)


def _ssm_kernel(u_ref, sgb_ref, bmat_ref, cre_ref, cim_ref, sc_ref, d_ref, gw_ref, gb_ref, o_ref, xre_ref, xim_ref):
    t = u_ref.shape[1]
    nq = bmat_ref.shape[0]
    half = bmat_ref.shape[2] // 2

    @pl.when(pl.program_id(1) == 0)
    def _():
        xre_ref[0:SUBLANES, :] = jnp.zeros((SUBLANES, xre_ref.shape[1]), F32)
        xim_ref[0:SUBLANES, :] = jnp.zeros((SUBLANES, xim_ref.shape[1]), F32)

    u = u_ref[0]
    ub = u.astype(BF16)
    for q in range(nq):
        bu = _mm(ub[:, q * LANES:(q + 1) * LANES], bmat_ref[q])
        xre_ref[SUBLANES:, q * half:(q + 1) * half] = bu[:, :half]
        xim_ref[SUBLANES:, q * half:(q + 1) * half] = bu[:, half:]

    def scan(r, _):
        base = pl.multiple_of(SUBLANES + r * SUBLANES, SUBLANES)
        xr = xre_ref[pl.ds(base, SUBLANES), :]
        xi = xim_ref[pl.ds(base, SUBLANES), :]
        for s, k in enumerate((1, 2, 4)):
            ar = sc_ref[2 * s]
            ai = sc_ref[2 * s + 1]
            sr = pltpu.roll(xr, k, axis=0)
            si = pltpu.roll(xi, k, axis=0)
            xr, xi = xr + ar * sr - ai * si, xi + ar * si + ai * sr
        cr = xre_ref[pl.ds(base - 1, 1), :]
        ci = xim_ref[pl.ds(base - 1, 1), :]
        pr = sc_ref[6]
        pi = sc_ref[7]
        xre_ref[pl.ds(base, SUBLANES), :] = xr + pr * cr - pi * ci
        xim_ref[pl.ds(base, SUBLANES), :] = xi + pr * ci + pi * cr
        return 0

    lax.fori_loop(0, t // SUBLANES, scan, 0, unroll=2)
    xre_ref[0:SUBLANES, :] = xre_ref[t:t + SUBLANES, :]
    xim_ref[0:SUBLANES, :] = xim_ref[t:t + SUBLANES, :]

    ys = []
    for q in range(nq):
        xr = xre_ref[SUBLANES:, q * half:(q + 1) * half].astype(BF16)
        xi = xim_ref[SUBLANES:, q * half:(q + 1) * half].astype(BF16)
        ys.append(_mm(xr, cre_ref[q]) + _mm(xi, cim_ref[q]))
    y = jnp.concatenate(ys, axis=1) + d_ref[...] * u
    y = jax.nn.gelu(y).astype(BF16)
    hh = _mm(y, gw_ref[...]) + gb_ref[...]
    w = hh.shape[1] // 2
    o_ref[0] = (hh[:, :w] * jax.nn.sigmoid(hh[:, w:]) * sgb_ref[0].astype(F32)).astype(BF16)


def _ssm(u, sgb, bmat, cre, cim, sc, dskip, gw, gb):
    b, l, w = u.shape
    t = ROW_TILE
    ns = sc.shape[-1]

    def row(n):
        return pl.BlockSpec((1, t, n), lambda bb, i: (bb, i, 0))

    return pl.pallas_call(
        _ssm_kernel,
        grid=(b, l // t),
        in_specs=[row(w), row(w), _const_spec(bmat.shape), _const_spec(cre.shape), _const_spec(cim.shape),
                  _const_spec(sc.shape), _const_spec(dskip.shape), _const_spec(gw.shape), _const_spec(gb.shape)],
        out_specs=row(w),
        out_shape=jax.ShapeDtypeStruct((b, l, w), BF16),
        scratch_shapes=[pltpu.VMEM((SUBLANES + t, ns), F32), pltpu.VMEM((SUBLANES + t, ns), F32)],
        compiler_params=_params("arbitrary", "arbitrary"),
        name="ssm",
    )(u, sgb, bmat, cre, cim, sc, dskip, gw, gb)


def _s5_prep(log_dt, a_re, a_im, b_re, b_im, c_re, c_im):
    g, p = a_re.shape
    h = b_re.shape[-1]
    gl = LANES // h
    nq = g // gl
    dt = jnp.exp(log_dt)[:, None]
    mag = jnp.exp(a_re * dt)
    ang = a_im * dt
    ab_re = mag * jnp.cos(ang)
    ab_im = mag * jnp.sin(ang)
    den = a_re * a_re + a_im * a_im
    n_re = ab_re - 1.0
    n_im = ab_im
    f_re = (n_re * a_re + n_im * a_im) / den
    f_im = (n_im * a_re - n_re * a_im) / den
    bb_re = f_re[..., None] * b_re - f_im[..., None] * b_im
    bb_im = f_re[..., None] * b_im + f_im[..., None] * b_re
    eye = jnp.eye(gl, dtype=F32)

    def bdiag_in(m):
        m = m.reshape(nq, gl, p, h)
        return jnp.einsum('qgph,gk->qghkp', m, eye).reshape(nq, gl * h, gl * p)

    def bdiag_out(m):
        m = m.reshape(nq, gl, h, p)
        return jnp.einsum('qghp,gk->qgpkh', m, eye).reshape(nq, gl * p, gl * h)

    bmat = jnp.concatenate([bdiag_in(bb_re), bdiag_in(bb_im)], axis=2).astype(BF16)
    cre = bdiag_out(c_re).astype(BF16)
    cim = bdiag_out(-c_im).astype(BF16)

    pw = [(ab_re.reshape(-1), ab_im.reshape(-1))]
    for _ in range(SUBLANES - 1):
        pr, pi = pw[-1]
        pw.append((pr * pw[0][0] - pi * pw[0][1], pr * pw[0][1] + pi * pw[0][0]))
    rows = jnp.arange(SUBLANES)[:, None]
    sc = []
    for k in (1, 2, 4):
        sc.append(jnp.where(rows >= k, pw[k - 1][0][None, :], 0.0))
        sc.append(jnp.where(rows >= k, pw[k - 1][1][None, :], 0.0))
    sc.append(jnp.stack([pw[r][0] for r in range(SUBLANES)]))
    sc.append(jnp.stack([pw[r][1] for r in range(SUBLANES)]))
    return bmat, cre, cim, jnp.stack(sc).astype(F32)


def _mid_kernel(x_ref, a_ref, s_ref, wo_ref, g_ref, wq_ref, wk_ref, wvt_ref, wg_ref, wqi_ref, wki_ref, wwt_ref,
                qg_ref, kg_ref, h_ref, q_ref, k_ref, vt_ref, sg_ref, qi_ref, ki_ref, wt_ref):
    aw = a_ref.shape[2]
    h = x_ref[0] + _mm(a_ref[0], wo_ref[0:aw, :]) + _mm(s_ref[0], wo_ref[aw:, :])
    h_ref[0] = h
    hn = _rms(h, g_ref[...]).astype(BF16)
    qf = _mm(hn, wq_ref[...])
    for hd in range(C_HEADS):
        sl = slice(hd * C_HEAD_DIM, (hd + 1) * C_HEAD_DIM)
        q_ref[0, :, sl] = (_rms(qf[:, sl], qg_ref[...]) * (C_HEAD_DIM ** -0.5)).astype(BF16)
    kf = _mm(hn, wk_ref[...])
    for hd in range(C_KV_HEADS):
        sl = slice(hd * C_HEAD_DIM, (hd + 1) * C_HEAD_DIM)
        k_ref[0, :, sl] = _rms(kf[:, sl], kg_ref[...]).astype(BF16)
    vt_ref[0] = _mm_nt(wvt_ref[...], hn).astype(BF16)
    sg_ref[0] = _silu(_mm(hn, wg_ref[...])).astype(BF16)
    qi_ref[0] = _mm(hn, wqi_ref[...]).astype(BF16)
    ki_ref[0] = _mm(hn, wki_ref[...]).astype(BF16)
    wt_ref[0] = _mm_nt(wwt_ref[...], hn) * ((IDX_HEADS ** -0.5) * (IDX_DIM ** -0.5))


def _mid(x, att0, ssm0, wo, g, wq, wk, wvt, wg, wqi, wki2, wwt, qg, kg):
    b, l, d = x.shape
    t = ROW_TILE

    def row(n):
        return pl.BlockSpec((1, t, n), lambda bb, i: (bb, i, 0))

    def col(n):
        return pl.BlockSpec((1, n, t), lambda bb, i: (bb, 0, i))

    weights = [wo, g, wq, wk, wvt, wg, wqi, wki2, wwt, qg, kg]
    cw = C_HEADS * C_HEAD_DIM
    ckv = C_KV_HEADS * C_HEAD_DIM
    return pl.pallas_call(
        _mid_kernel,
        grid=(b, l // t),
        in_specs=[row(d), row(att0.shape[2]), row(ssm0.shape[2])] + [_const_spec(w.shape) for w in weights],
        out_specs=[row(d), row(cw), row(ckv), col(ckv), row(cw), row(IDX_HEADS * IDX_DIM), row(2 * LANES),
                   col(IDX_HEADS)],
        out_shape=[jax.ShapeDtypeStruct((b, l, d), F32), jax.ShapeDtypeStruct((b, l, cw), BF16),
                   jax.ShapeDtypeStruct((b, l, ckv), BF16), jax.ShapeDtypeStruct((b, ckv, l), BF16),
                   jax.ShapeDtypeStruct((b, l, cw), BF16), jax.ShapeDtypeStruct((b, l, IDX_HEADS * IDX_DIM), BF16),
                   jax.ShapeDtypeStruct((b, l, 2 * LANES), BF16), jax.ShapeDtypeStruct((b, IDX_HEADS, l), F32)],
        compiler_params=_params("arbitrary", "arbitrary"),
        name="mid",
    )(x, att0, ssm0, *weights)


def _dsa_kernel(q_ref, qi_ref, wt_ref, sg_ref, k_ref, vt_ref, ki_ref, tab_ref, o_ref,
                sc_ref, x_ref, acc_ref, *, seq_len, topk):
    i = pl.program_id(1)
    ck = KEY_CHUNK
    per = ck // BLOCK
    nch = (i + per) // per
    t_row = i * BLOCK + lax.broadcasted_iota(I32, (1, LANES), 1)
    kiota = lax.broadcasted_iota(I32, (ck, LANES), 0)

    def chunk_off(c):
        return pl.multiple_of(c * ck, ck)

    qi = qi_ref[0]
    qi_stack = [jnp.concatenate([qi[:, (2 * s) * LANES:(2 * s + 1) * LANES],
                                 qi[:, (2 * s + 1) * LANES:(2 * s + 2) * LANES]], axis=0) for s in range(2)]
    wt = wt_ref[0]

    def score_chunk(c, masked):
        off = chunk_off(c)
        sc = jnp.zeros((ck, LANES), F32)
        for a in range(2):
            kk = ki_ref[0, pl.ds(off, ck), a * LANES:(a + 1) * LANES]
            for s in range(2):
                r = _mm_nt(kk, qi_stack[s])
                for j in range(2):
                    hd = 2 * (2 * s + j) + a
                    sc = sc + jnp.maximum(r[:, j * LANES:(j + 1) * LANES], 0.0) * wt[hd:hd + 1, :]
        if masked:
            sc = jnp.where(off + kiota <= t_row, sc, NEG_INF)
        sc_ref[pl.ds(off, ck), :] = sc
        return _fold(sc, jnp.maximum)

    def score_body(c, mx):
        return jnp.maximum(mx, score_chunk(c, False))

    smax = lax.fori_loop(0, nch - 1, score_body, jnp.full((SUBLANES, LANES), NEG_INF, F32))
    smax = jnp.max(jnp.maximum(smax, score_chunk(nch - 1, True)), axis=0, keepdims=True)

    def count(pred):
        rows = COUNT_ROWS

        def body(c, acc):
            off = pl.multiple_of(c * rows, rows)
            ind = pred(sc_ref[pl.ds(off, rows), :], off).astype(I32)
            return acc + jnp.sum(ind.reshape(rows // SUBLANES, SUBLANES, LANES), axis=0)

        acc = lax.fori_loop(0, nch * (ck // rows), body, jnp.zeros((SUBLANES, LANES), I32))
        return jnp.sum(acc, axis=0, keepdims=True)

    def key_value(k):
        return pltpu.bitcast(jnp.where(k < 0, INT_MIN - k, k), F32)

    def count_ge(k):
        thr = key_value(k)
        return count(lambda s, off: s >= thr)

    def full(v):
        return jnp.full((1, LANES), v, I32)

    def bisect(_, st):
        lo, hi, c_lo, c_hi = st
        mid = (lo >> 1) + (hi >> 1) + (lo & hi & 1)
        c = count_ge(mid)
        ge = c >= topk
        return jnp.where(ge, mid, lo), jnp.where(ge, hi, mid), jnp.where(ge, c, c_lo), jnp.where(ge, c_hi, c)

    searching = (i + 1) * BLOCK > topk

    def float_key(x):
        bits = pltpu.bitcast(x, I32)
        return jnp.where(bits < 0, INT_MIN - bits, bits)

    def search():
        k_lo = float_key(smax * 0.125)
        c = count_ge(k_lo)
        ok = (smax > 0.0) & (c >= topk)
        trips = jnp.where(jnp.min(jnp.where(ok, 1, 0)) > 0, 25, 32)
        st = (jnp.where(ok, k_lo, KEY_NEG_INF), float_key(smax) + 1, jnp.where(ok, c, nch * ck), full(0))
        out = lax.fori_loop(0, trips, bisect, st)
        return out[0], out[2], out[3]

    vkey, c_lo, c_hi = lax.cond(searching, search, lambda: (full(KEY_NEG_INF), full(topk), full(0)))
    vthr = key_value(vkey)
    need = topk - c_hi
    ties = c_lo - c_hi

    def tie_search():
        nxt = vkey + 1
        nxt = jnp.where((nxt > 0) & (nxt < KEY_MIN_NORMAL), KEY_MIN_NORMAL, nxt)
        step = key_value(nxt) - vthr

        def split(_, st):
            fl, fh = st
            fm = 0.5 * (fl + fh)
            thr = vthr + fm * step
            ge = count(lambda s, off: s >= thr) >= topk
            return jnp.where(ge, fm, fl), jnp.where(ge, fh, fm)

        fl, _ = lax.fori_loop(0, 26, split, (jnp.zeros((1, LANES), F32), jnp.ones((1, LANES), F32)))
        thr = vthr + fl * step
        want = topk - count(lambda s, off: s > thr)

        def body(_, st):
            lj, hj = st
            mid = (lj + hj) >> 1
            c = count(lambda s, off: (s == thr) & (off + kiota[:COUNT_ROWS] <= mid))
            ok = c >= want
            return jnp.where(ok, lj, mid), jnp.where(ok, mid, hj)

        _, hj = lax.fori_loop(0, 14, body, (full(-1), full(0) + (nch * ck - 1)))
        return thr, hj

    any_tie = searching & (jnp.max(ties - need) > 0)
    vthr, jmax = lax.cond(any_tie, tie_search, lambda: (vthr, full(seq_len)))

    def mask_body(c, _):
        off = chunk_off(c)
        s = sc_ref[pl.ds(off, ck), :]
        s_idx = off + kiota
        sel = ((s > vthr) | ((s == vthr) & (s_idx <= jmax))) & (s_idx <= t_row)
        sc_ref[pl.ds(off, ck), :] = jnp.where(sel, 0.0, NEG_INF)
        return 0

    lax.fori_loop(0, nch, mask_body, 0)

    q = q_ref[0]
    n_far = jnp.maximum((i - NEAR_BLOCKS + 1) // per, 0)
    hpg = C_HEADS // C_KV_HEADS

    for g in range(C_KV_HEADS):
        q_stack = [jnp.concatenate([q[:, (hpg * g + 2 * jj) * LANES:(hpg * g + 2 * jj + 1) * LANES],
                                    q[:, (hpg * g + 2 * jj + 1) * LANES:(hpg * g + 2 * jj + 2) * LANES]], axis=0)
                   for jj in range(hpg // 2)]

        def stage_body(near, g=g, q_stack=q_stack):
            def body(c, mx):
                off = chunk_off(c)
                madd = sc_ref[pl.ds(off, ck), :]
                kc = k_ref[0, pl.ds(off, ck), g * LANES:(g + 1) * LANES]
                tidx = [jnp.clip(i - (c * per + r), 0, NEAR_BLOCKS) for r in range(per)]
                out = []
                for jj in range(hpg // 2):
                    lg = _mm_nt(kc, q_stack[jj])
                    for a in range(2):
                        hl = 2 * jj + a
                        x = lg[:, a * LANES:(a + 1) * LANES] + madd
                        if near:
                            x = x + jnp.concatenate([tab_ref[hpg * g + hl, tidx[r]] for r in range(per)], axis=0)
                        x_ref[pl.ds(off, ck), hl * LANES:(hl + 1) * LANES] = x
                        out.append(jnp.maximum(mx[hl], _fold(x, jnp.maximum)))
                return tuple(out)

            return body

        mx = tuple(jnp.full((SUBLANES, LANES), NEG_INF, F32) for _ in range(hpg))
        mx = lax.fori_loop(0, n_far, stage_body(False), mx)
        mx = lax.fori_loop(n_far, nch, stage_body(True), mx)
        m = [jnp.max(v, axis=0, keepdims=True) for v in mx]
        acc_ref[...] = jnp.zeros(acc_ref.shape, F32)

        def att_body(c, ls, g=g, m=m):
            off = chunk_off(c)
            vt = vt_ref[0, g * LANES:(g + 1) * LANES, pl.ds(off, ck)]
            out = []
            for jj in range(hpg // 2):
                ps = []
                for a in range(2):
                    hl = 2 * jj + a
                    p = jnp.exp(x_ref[pl.ds(off, ck), hl * LANES:(hl + 1) * LANES] - m[hl])
                    out.append(ls[hl] + _fold(p, jnp.add))
                    ps.append(p.astype(BF16))
                acc_ref[jj] += _mm(vt, jnp.concatenate(ps, axis=1))
            return tuple(out)

        ls = lax.fori_loop(0, nch, att_body, tuple(jnp.zeros((SUBLANES, LANES), F32) for _ in range(hpg)))
        for hl in range(hpg):
            sl = slice((hpg * g + hl) * LANES, (hpg * g + hl + 1) * LANES)
            ot = acc_ref[hl // 2, :, (hl % 2) * LANES:(hl % 2 + 1) * LANES] / jnp.sum(ls[hl], axis=0, keepdims=True)
            o_ref[0, :, sl] = (ot.T * sg_ref[0, :, sl].astype(F32)).astype(BF16)


def _dsa(q, qi, wt, sg, k, vt, ki2, tab):
    b, l, cw = q.shape
    nb = l // BLOCK
    topk = min(TOPK_MAX, l // 4)

    def blk(n):
        return pl.BlockSpec((1, BLOCK, n), lambda bb, i: (bb, i, 0))

    def whole(s1, s2):
        return pl.BlockSpec((1, s1, s2), lambda bb, i: (bb, 0, 0), pipeline_mode=pl.Buffered(1))

    hpg = C_HEADS // C_KV_HEADS
    return pl.pallas_call(
        functools.partial(_dsa_kernel, seq_len=l, topk=topk),
        grid=(b, nb),
        in_specs=[blk(cw), blk(qi.shape[2]), pl.BlockSpec((1, IDX_HEADS, BLOCK), lambda bb, i: (bb, 0, i)), blk(cw),
                  whole(l, k.shape[2]), whole(vt.shape[1], l), whole(l, ki2.shape[2]),
                  pl.BlockSpec(tab.shape, lambda bb, i: (0, 0, 0, 0), pipeline_mode=pl.Buffered(1))],
        out_specs=blk(cw),
        out_shape=jax.ShapeDtypeStruct((b, l, cw), BF16),
        scratch_shapes=[pltpu.VMEM((l, LANES), F32), pltpu.VMEM((l, hpg * LANES), F32),
                        pltpu.VMEM((hpg // 2, C_HEAD_DIM, 2 * LANES), F32)],
        compiler_params=_params("arbitrary", "arbitrary"),
        name="dsa",
    )(q, qi, wt, sg, k, vt, ki2, tab)


def _out_kernel(h_ref, a_ref, w_ref, o_ref):
    o_ref[...] = h_ref[...] + _mm(a_ref[...], w_ref[...])


def _outproj(h2, a2, w):
    rows, d = h2.shape
    t = ROW_TILE
    return pl.pallas_call(
        _out_kernel,
        grid=(rows // t,),
        in_specs=[pl.BlockSpec((t, d), lambda i: (i, 0)), pl.BlockSpec((t, a2.shape[1]), lambda i: (i, 0)),
                  _const_spec(w.shape)],
        out_specs=pl.BlockSpec((t, d), lambda i: (i, 0)),
        out_shape=jax.ShapeDtypeStruct((rows, d), F32),
        compiler_params=_params("arbitrary"),
        name="outproj1",
    )(h2, a2, w)


def _bias_tables(rel_bias, seq_len):
    del seq_len
    nv = (NEAR_BLOCKS + 1) * BLOCK
    vec = rel_bias[_t5_bucket(jnp.arange(nv, dtype=I32))].astype(F32).T

    def window(lo, n):
        pad = max(0, -lo)
        body = vec[:, max(lo, 0):lo + n]
        return jnp.concatenate([jnp.broadcast_to(vec[:, :1], (vec.shape[0], pad)), body], axis=1)

    def toeplitz(g, rows, cols):
        w = rows + cols
        g2 = jnp.concatenate([g[:, rows - 1:rows - 1 + cols], g[:, :1], g[:, :rows - 1]], axis=1)
        flat = jnp.tile(g2, (1, rows))[:, :rows * (w - 1)]
        return flat.reshape(-1, rows, w - 1)[:, :, :cols]

    bias0 = jnp.transpose(toeplitz(window(BLOCK - (2 * BLOCK - 1), 3 * BLOCK - 1), 2 * BLOCK, BLOCK), (0, 2, 1))
    tiles = [toeplitz(window(dl * BLOCK - (BLOCK - 1), 2 * BLOCK - 1), BLOCK, BLOCK) for dl in range(NEAR_BLOCKS)]
    tab = jnp.stack(tiles, axis=1) - rel_bias[NUM_BUCKETS - 1].astype(F32)[:, None, None, None]
    tab = jnp.concatenate([tab, jnp.zeros((tab.shape[0], 1, BLOCK, BLOCK), F32)], axis=1)
    return bias0, tab


def kernel(x, rel_bias, norm_g, ev_w_in, ev_w_out, ev_q_norm_g, ev_k_norm_g, ev_sinks, ev_ssm_log_dt, ev_ssm_a_re,
           ev_ssm_a_im, ev_ssm_b_re, ev_ssm_b_im, ev_ssm_c_re, ev_ssm_c_im, ev_ssm_d, ev_glu_w, ev_glu_b, od_w_in,
           od_w_out, od_q_norm_g, od_k_norm_g):
    b, l, d = x.shape
    assert l % KEY_CHUNK == 0 and l % ROW_TILE == 0
    assert (NEAR_BLOCKS - 1) * BLOCK + 1 >= 16 * 64 ** (15 / 16) + 1
    bias0, tab = _bias_tables(rel_bias, l)

    q0, k0, v0, sga, u, sgb = _proj0(x.reshape(b * l, d), norm_g[0][None, :], ev_w_in[0].astype(BF16))
    shp = lambda a: a.reshape(b, l, a.shape[-1])
    qg2 = jnp.tile(ev_q_norm_g[0], 2)[None, :]
    kg2 = jnp.tile(ev_k_norm_g[0], 2)[None, :]
    sinks = jnp.broadcast_to(ev_sinks[0][:, None], (A_HEADS, LANES)).astype(F32)
    att0 = _attn0(shp(q0), shp(k0), shp(v0), shp(sga), bias0, sinks, qg2, kg2)
    bmat, cre, cim, sc = _s5_prep(ev_ssm_log_dt[0], ev_ssm_a_re[0], ev_ssm_a_im[0], ev_ssm_b_re[0], ev_ssm_b_im[0],
                                  ev_ssm_c_re[0], ev_ssm_c_im[0])
    ssm0 = _ssm(shp(u), shp(sgb), bmat, cre, cim, sc, ev_ssm_d[0].reshape(1, -1), ev_glu_w[0].astype(BF16),
                ev_glu_b[0][None, :])

    w1 = od_w_in[0]
    cw = C_HEADS * C_HEAD_DIM
    ckv = C_KV_HEADS * C_HEAD_DIM
    o = np.cumsum([0, cw, ckv, ckv, cw, IDX_HEADS * IDX_DIM, IDX_DIM, IDX_HEADS])
    wq, wk, wv, wg, wqi, wki, ww = (w1[:, o[n]:o[n + 1]] for n in range(7))
    zki = jnp.zeros((d, LANES - IDX_DIM), w1.dtype)
    wki2 = jnp.concatenate([wki, zki, zki, wki], axis=1)
    bf = lambda a: a.astype(BF16)
    h1, q1, k1, vt1, sg1, qi1, ki2, wt1 = _mid(
        x, att0, ssm0, bf(ev_w_out[0]), norm_g[1][None, :], bf(wq), bf(wk), bf(wv.T), bf(wg), bf(wqi), bf(wki2),
        bf(ww.T), od_q_norm_g[0][None, :], od_k_norm_g[0][None, :])
    att1 = _dsa(q1, qi1, wt1, sg1, k1, vt1, ki2, tab)
    out = _outproj(h1.reshape(b * l, d), att1.reshape(b * l, cw), bf(od_w_out[0]))
    return out.reshape(b, l, d)
```

```python
import functools
import math

import jax
import jax.numpy as jnp
import numpy as np
from jax import lax
from jax.experimental import pallas as pl
from jax.experimental.pallas import tpu as pltpu

F32 = jnp.float32
BF16 = jnp.bfloat16
I32 = jnp.int32

LANES = 128
SUBLANES = 8
VMEM_LIMIT = 56 * 1024 * 1024

BLOCK = 128
WINDOW = 128
A_HEADS = 8
A_HEAD_DIM = 64
A_KV_HEADS = 2
A_WIDTH = A_HEADS * A_HEAD_DIM
SSM_GROUP = 16
SSM_STATE = 64
C_HEADS = 8
C_HEAD_DIM = 128
C_KV_HEADS = 2
IDX_HEADS = 8
IDX_DIM = 64
TOPK_MAX = 256
NUM_BUCKETS = 32
REL_MAX_DIST = 1024
EPS = 1e-6
NEG_INF = -1e30
INT_MIN = -(2 ** 31)
KEY_MIN_NORMAL = 0x00800000
KEY_POS_INF = 0x7F800000
KEY_NEG_INF = INT_MIN + 0x00800000

ROW_TILE = 256
KEY_CHUNK = 512
NEAR_BLOCKS = 8
FOLD_CHAINS = 8
COUNT_ROWS = 512
NT_DIMS = (((1,), (1,)), ((), ()))


def _t5_bucket(dist):
    n = jnp.maximum(dist, 0)
    max_exact = NUM_BUCKETS // 2
    nf = jnp.maximum(n, 1).astype(F32)
    large = max_exact + (jnp.log(nf / max_exact) / math.log(REL_MAX_DIST / max_exact)
                         * (NUM_BUCKETS - max_exact)).astype(I32)
    large = jnp.minimum(large, NUM_BUCKETS - 1)
    return jnp.where(n < max_exact, n, large)


def _silu(x):
    return x * jax.nn.sigmoid(x)


def _rms(x, g):
    ms = jnp.mean(x * x, axis=-1, keepdims=True)
    return x * lax.rsqrt(ms + EPS) * g


def _mm(a, b):
    return jnp.dot(a, b, preferred_element_type=F32)


def _mm_nt(a, b):
    return lax.dot_general(a, b, NT_DIMS, preferred_element_type=F32)


def _fold(x, op):
    n = x.shape[0] // SUBLANES
    chains = min(FOLD_CHAINS, n)
    accs = [x[r * SUBLANES:(r + 1) * SUBLANES] for r in range(chains)]
    for r in range(chains, n):
        accs[r % chains] = op(accs[r % chains], x[r * SUBLANES:(r + 1) * SUBLANES])
    while len(accs) > 1:
        accs = [op(a, b) for a, b in zip(accs[::2], accs[1::2])] + accs[len(accs) & ~1:]
    return accs[0]


def _params(*sem):
    return pltpu.CompilerParams(dimension_semantics=sem, vmem_limit_bytes=VMEM_LIMIT)


def _const_spec(shape):
    zeros = (0,) * len(shape)
    return pl.BlockSpec(shape, lambda *_: zeros)


def _proj0_kernel(x_ref, g_ref, w_ref, q_ref, k_ref, v_ref, sga_ref, u_ref, sgb_ref):
    hn = _rms(x_ref[...], g_ref[...]).astype(BF16)

    def mm(lo, hi):
        return _mm(hn, w_ref[:, lo:hi])

    q_ref[...] = mm(0, 512)
    k_ref[...] = mm(512, 640)
    v_ref[...] = mm(640, 768)
    sga_ref[...] = _silu(mm(768, 1280)).astype(BF16)
    u_ref[...] = mm(1280, 1792)
    sgb_ref[...] = _silu(mm(1792, 2304)).astype(BF16)


def _proj0(x2, g, w):
    rows, d = x2.shape
    t = ROW_TILE

    def row(n):
        return pl.BlockSpec((t, n), lambda i: (i, 0))

    return pl.pallas_call(
        _proj0_kernel,
        grid=(rows // t,),
        in_specs=[row(d), _const_spec((1, d)), _const_spec(w.shape)],
        out_specs=[row(512), row(128), row(128), row(512), row(512), row(512)],
        out_shape=[jax.ShapeDtypeStruct((rows, 512), F32), jax.ShapeDtypeStruct((rows, 128), F32),
                   jax.ShapeDtypeStruct((rows, 128), F32), jax.ShapeDtypeStruct((rows, 512), BF16),
                   jax.ShapeDtypeStruct((rows, 512), F32), jax.ShapeDtypeStruct((rows, 512), BF16)],
        compiler_params=_params("arbitrary"),
        name="proj0",
    )(x2, g, w)


def _attn0_kernel(q_ref, kc_ref, kp_ref, vc_ref, vp_ref, sga_ref, bias_ref, sink_ref, qg_ref, kg_ref, o_ref):
    i = pl.program_id(1)
    lane = lax.broadcasted_iota(I32, (1, LANES), 1)
    lo = lane < A_HEAD_DIM

    def segnorm(x, g2):
        sq = x * x
        s_lo = jnp.sum(jnp.where(lo, sq, 0.0), axis=-1, keepdims=True)
        s_hi = jnp.sum(jnp.where(lo, 0.0, sq), axis=-1, keepdims=True)
        inv = jnp.where(lo, lax.rsqrt(s_lo / A_HEAD_DIM + EPS), lax.rsqrt(s_hi / A_HEAD_DIM + EPS))
        return x * inv * g2

    kn = segnorm(jnp.concatenate([kp_ref[0], kc_ref[0]], axis=0), kg_ref[...])
    vb = jnp.concatenate([vp_ref[0], vc_ref[0]], axis=0)
    kr = pltpu.roll(kn, A_HEAD_DIM, axis=1)
    vr = pltpu.roll(vb, A_HEAD_DIM, axis=1)

    def variants(x, xr):
        return {(0, 0): jnp.where(lo, x, 0.0).astype(BF16), (0, 1): jnp.where(lo, 0.0, xr).astype(BF16),
                (1, 0): jnp.where(lo, xr, 0.0).astype(BF16), (1, 1): jnp.where(lo, 0.0, x).astype(BF16)}

    kvar = variants(kn, kr)
    vvar = variants(vb, vr)

    row = lax.broadcasted_iota(I32, (BLOCK, 2 * BLOCK), 0)
    col = lax.broadcasted_iota(I32, (BLOCK, 2 * BLOCK), 1)
    d = row + BLOCK - col
    mask = (d >= 0) & (d < WINDOW) & ((i > 0) | (col >= BLOCK))

    lgs, sinks = [], []
    for p in range(A_HEADS // 2):
        qp = (segnorm(q_ref[0, :, p * LANES:(p + 1) * LANES], qg_ref[...]) * (A_HEAD_DIM ** -0.5)).astype(BF16)
        for a in range(2):
            h = 2 * p + a
            lgs.append(jnp.where(mask, _mm_nt(qp, kvar[(p // 2, a)]) + bias_ref[h], NEG_INF))
            sinks.append(jnp.broadcast_to(sink_ref[h:h + 1, 0:1], (BLOCK, 1)))
    lg = jnp.concatenate(lgs, axis=0)
    sink = jnp.concatenate(sinks, axis=0)
    m = jnp.maximum(jnp.max(lg, axis=-1, keepdims=True), sink)
    e = jnp.exp(lg - m)
    den = jnp.sum(e, axis=-1, keepdims=True) + jnp.exp(sink - m)
    pr = (e / den).astype(BF16)
    for p in range(A_HEADS // 2):
        sl = slice(p * LANES, (p + 1) * LANES)
        acc = jnp.zeros((BLOCK, LANES), F32)
        for a in range(2):
            h = 2 * p + a
            acc = acc + _mm(pr[h * BLOCK:(h + 1) * BLOCK], vvar[(p // 2, a)])
        o_ref[0, :, sl] = (acc * sga_ref[0, :, sl].astype(F32)).astype(BF16)


def _attn0(q, k, v, sga, bias0, sinks, qg2, kg2):
    b, l, _ = q.shape
    nb = l // BLOCK

    def cur(n):
        return pl.BlockSpec((1, BLOCK, n), lambda bb, i: (bb, i, 0))

    def prev(n):
        return pl.BlockSpec((1, BLOCK, n), lambda bb, i: (bb, jnp.maximum(i - 1, 0), 0))

    return pl.pallas_call(
        _attn0_kernel,
        grid=(b, nb),
        in_specs=[cur(512), cur(128), prev(128), cur(128), prev(128), cur(512),
                  _const_spec(bias0.shape), _const_spec(sinks.shape), _const_spec(qg2.shape), _const_spec(kg2.shape)],
        out_specs=cur(512),
        out_shape=jax.ShapeDtypeStruct((b, l, 512), BF16),
        compiler_params=_params("arbitrary", "arbitrary"),
        name="attn0",
    )(q, k, k, v, v, sga, bias0, sinks, qg2, kg2)


def _ssm_kernel(u_ref, sgb_ref, bmat_ref, cre_ref, cim_ref, sc_ref, d_ref, gw_ref, gb_ref, o_ref, xre_ref, xim_ref):
    t = u_ref.shape[1]
    nq = bmat_ref.shape[0]
    half = bmat_ref.shape[2] // 2

    @pl.when(pl.program_id(1) == 0)
    def _():
        xre_ref[0:SUBLANES, :] = jnp.zeros((SUBLANES, xre_ref.shape[1]), F32)
        xim_ref[0:SUBLANES, :] = jnp.zeros((SUBLANES, xim_ref.shape[1]), F32)

    u = u_ref[0]
    ub = u.astype(BF16)
    for q in range(nq):
        bu = _mm(ub[:, q * LANES:(q + 1) * LANES], bmat_ref[q])
        xre_ref[SUBLANES:, q * half:(q + 1) * half] = bu[:, :half]
        xim_ref[SUBLANES:, q * half:(q + 1) * half] = bu[:, half:]

    def scan(r, _):
        base = pl.multiple_of(SUBLANES + r * SUBLANES, SUBLANES)
        xr = xre_ref[pl.ds(base, SUBLANES), :]
        xi = xim_ref[pl.ds(base, SUBLANES), :]
        for s, k in enumerate((1, 2, 4)):
            ar = sc_ref[2 * s]
            ai = sc_ref[2 * s + 1]
            sr = pltpu.roll(xr, k, axis=0)
            si = pltpu.roll(xi, k, axis=0)
            xr, xi = xr + ar * sr - ai * si, xi + ar * si + ai * sr
        cr = xre_ref[pl.ds(base - 1, 1), :]
        ci = xim_ref[pl.ds(base - 1, 1), :]
        pr = sc_ref[6]
        pi = sc_ref[7]
        xre_ref[pl.ds(base, SUBLANES), :] = xr + pr * cr - pi * ci
        xim_ref[pl.ds(base, SUBLANES), :] = xi + pr * ci + pi * cr
        return 0

    lax.fori_loop(0, t // SUBLANES, scan, 0, unroll=2)
    xre_ref[0:SUBLANES, :] = xre_ref[t:t + SUBLANES, :]
    xim_ref[0:SUBLANES, :] = xim_ref[t:t + SUBLANES, :]

    ys = []
    for q in range(nq):
        xr = xre_ref[SUBLANES:, q * half:(q + 1) * half].astype(BF16)
        xi = xim_ref[SUBLANES:, q * half:(q + 1) * half].astype(BF16)
        ys.append(_mm(xr, cre_ref[q]) + _mm(xi, cim_ref[q]))
    y = jnp.concatenate(ys, axis=1) + d_ref[...] * u
    y = jax.nn.gelu(y).astype(BF16)
    hh = _mm(y, gw_ref[...]) + gb_ref[...]
    w = hh.shape[1] // 2
    o_ref[0] = (hh[:, :w] * jax.nn.sigmoid(hh[:, w:]) * sgb_ref[0].astype(F32)).astype(BF16)


def _ssm(u, sgb, bmat, cre, cim, sc, dskip, gw, gb):
    b, l, w = u.shape
    t = ROW_TILE
    ns = sc.shape[-1]

    def row(n):
        return pl.BlockSpec((1, t, n), lambda bb, i: (bb, i, 0))

    return pl.pallas_call(
        _ssm_kernel,
        grid=(b, l // t),
        in_specs=[row(w), row(w), _const_spec(bmat.shape), _const_spec(cre.shape), _const_spec(cim.shape),
                  _const_spec(sc.shape), _const_spec(dskip.shape), _const_spec(gw.shape), _const_spec(gb.shape)],
        out_specs=row(w),
        out_shape=jax.ShapeDtypeStruct((b, l, w), BF16),
        scratch_shapes=[pltpu.VMEM((SUBLANES + t, ns), F32), pltpu.VMEM((SUBLANES + t, ns), F32)],
        compiler_params=_params("arbitrary", "arbitrary"),
        name="ssm",
    )(u, sgb, bmat, cre, cim, sc, dskip, gw, gb)


def _s5_prep(log_dt, a_re, a_im, b_re, b_im, c_re, c_im):
    g, p = a_re.shape
    h = b_re.shape[-1]
    gl = LANES // h
    nq = g // gl
    dt = jnp.exp(log_dt)[:, None]
    mag = jnp.exp(a_re * dt)
    ang = a_im * dt
    ab_re = mag * jnp.cos(ang)
    ab_im = mag * jnp.sin(ang)
    den = a_re * a_re + a_im * a_im
    n_re = ab_re - 1.0
    n_im = ab_im
    f_re = (n_re * a_re + n_im * a_im) / den
    f_im = (n_im * a_re - n_re * a_im) / den
    bb_re = f_re[..., None] * b_re - f_im[..., None] * b_im
    bb_im = f_re[..., None] * b_im + f_im[..., None] * b_re
    eye = jnp.eye(gl, dtype=F32)

    def bdiag_in(m):
        m = m.reshape(nq, gl, p, h)
        return jnp.einsum('qgph,gk->qghkp', m, eye).reshape(nq, gl * h, gl * p)

    def bdiag_out(m):
        m = m.reshape(nq, gl, h, p)
        return jnp.einsum('qghp,gk->qgpkh', m, eye).reshape(nq, gl * p, gl * h)

    bmat = jnp.concatenate([bdiag_in(bb_re), bdiag_in(bb_im)], axis=2).astype(BF16)
    cre = bdiag_out(c_re).astype(BF16)
    cim = bdiag_out(-c_im).astype(BF16)

    pw = [(ab_re.reshape(-1), ab_im.reshape(-1))]
    for _ in range(SUBLANES - 1):
        pr, pi = pw[-1]
        pw.append((pr * pw[0][0] - pi * pw[0][1], pr * pw[0][1] + pi * pw[0][0]))
    rows = jnp.arange(SUBLANES)[:, None]
    sc = []
    for k in (1, 2, 4):
        sc.append(jnp.where(rows >= k, pw[k - 1][0][None, :], 0.0))
        sc.append(jnp.where(rows >= k, pw[k - 1][1][None, :], 0.0))
    sc.append(jnp.stack([pw[r][0] for r in range(SUBLANES)]))
    sc.append(jnp.stack([pw[r][1] for r in range(SUBLANES)]))
    return bmat, cre, cim, jnp.stack(sc).astype(F32)


def _mid_kernel(x_ref, a_ref, s_ref, wo_ref, g_ref, wq_ref, wk_ref, wvt_ref, wg_ref, wqi_ref, wki_ref, wwt_ref,
                qg_ref, kg_ref, h_ref, q_ref, k_ref, vt_ref, sg_ref, qi_ref, ki_ref, wt_ref):
    aw = a_ref.shape[2]
    h = x_ref[0] + _mm(a_ref[0], wo_ref[0:aw, :]) + _mm(s_ref[0], wo_ref[aw:, :])
    h_ref[0] = h
    hn = _rms(h, g_ref[...]).astype(BF16)
    qf = _mm(hn, wq_ref[...])
    for hd in range(C_HEADS):
        sl = slice(hd * C_HEAD_DIM, (hd + 1) * C_HEAD_DIM)
        q_ref[0, :, sl] = (_rms(qf[:, sl], qg_ref[...]) * (C_HEAD_DIM ** -0.5)).astype(BF16)
    kf = _mm(hn, wk_ref[...])
    for hd in range(C_KV_HEADS):
        sl = slice(hd * C_HEAD_DIM, (hd + 1) * C_HEAD_DIM)
        k_ref[0, :, sl] = _rms(kf[:, sl], kg_ref[...]).astype(BF16)
    vt_ref[0] = _mm_nt(wvt_ref[...], hn).astype(BF16)
    sg_ref[0] = _silu(_mm(hn, wg_ref[...])).astype(BF16)
    qi_ref[0] = _mm(hn, wqi_ref[...]).astype(BF16)
    ki_ref[0] = _mm(hn, wki_ref[...]).astype(BF16)
    wt_ref[0] = _mm_nt(wwt_ref[...], hn) * ((IDX_HEADS ** -0.5) * (IDX_DIM ** -0.5))


def _mid(x, att0, ssm0, wo, g, wq, wk, wvt, wg, wqi, wki2, wwt, qg, kg):
    b, l, d = x.shape
    t = ROW_TILE

    def row(n):
        return pl.BlockSpec((1, t, n), lambda bb, i: (bb, i, 0))

    def col(n):
        return pl.BlockSpec((1, n, t), lambda bb, i: (bb, 0, i))

    weights = [wo, g, wq, wk, wvt, wg, wqi, wki2, wwt, qg, kg]
    cw = C_HEADS * C_HEAD_DIM
    ckv = C_KV_HEADS * C_HEAD_DIM
    return pl.pallas_call(
        _mid_kernel,
        grid=(b, l // t),
        in_specs=[row(d), row(att0.shape[2]), row(ssm0.shape[2])] + [_const_spec(w.shape) for w in weights],
        out_specs=[row(d), row(cw), row(ckv), col(ckv), row(cw), row(IDX_HEADS * IDX_DIM), row(2 * LANES),
                   col(IDX_HEADS)],
        out_shape=[jax.ShapeDtypeStruct((b, l, d), F32), jax.ShapeDtypeStruct((b, l, cw), BF16),
                   jax.ShapeDtypeStruct((b, l, ckv), BF16), jax.ShapeDtypeStruct((b, ckv, l), BF16),
                   jax.ShapeDtypeStruct((b, l, cw), BF16), jax.ShapeDtypeStruct((b, l, IDX_HEADS * IDX_DIM), BF16),
                   jax.ShapeDtypeStruct((b, l, 2 * LANES), BF16), jax.ShapeDtypeStruct((b, IDX_HEADS, l), F32)],
        compiler_params=_params("arbitrary", "arbitrary"),
        name="mid",
    )(x, att0, ssm0, *weights)


def _dsa_kernel(q_ref, qi_ref, wt_ref, sg_ref, k_ref, vt_ref, ki_ref, tab_ref, o_ref,
                sc_ref, x_ref, acc_ref, *, seq_len, topk):
    i = pl.program_id(1)
    ck = KEY_CHUNK
    per = ck // BLOCK
    nch = (i + per) // per
    t_row = i * BLOCK + lax.broadcasted_iota(I32, (1, LANES), 1)
    kiota = lax.broadcasted_iota(I32, (ck, LANES), 0)

    def chunk_off(c):
        return pl.multiple_of(c * ck, ck)

    qi = qi_ref[0]
    qi_stack = [jnp.concatenate([qi[:, (2 * s) * LANES:(2 * s + 1) * LANES],
                                 qi[:, (2 * s + 1) * LANES:(2 * s + 2) * LANES]], axis=0) for s in range(2)]
    wt = wt_ref[0]

    def score_chunk(c, masked):
        off = chunk_off(c)
        sc = jnp.zeros((ck, LANES), F32)
        for a in range(2):
            kk = ki_ref[0, pl.ds(off, ck), a * LANES:(a + 1) * LANES]
            for s in range(2):
                r = _mm_nt(kk, qi_stack[s])
                for j in range(2):
                    hd = 2 * (2 * s + j) + a
                    sc = sc + jnp.maximum(r[:, j * LANES:(j + 1) * LANES], 0.0) * wt[hd:hd + 1, :]
        if masked:
            sc = jnp.where(off + kiota <= t_row, sc, NEG_INF)
        sc_ref[pl.ds(off, ck), :] = sc
        return _fold(sc, jnp.maximum)

    def score_body(c, mx):
        return jnp.maximum(mx, score_chunk(c, False))

    smax = lax.fori_loop(0, nch - 1, score_body, jnp.full((SUBLANES, LANES), NEG_INF, F32))
    smax = jnp.max(jnp.maximum(smax, score_chunk(nch - 1, True)), axis=0, keepdims=True)

    def count(pred):
        rows = COUNT_ROWS

        def body(c, acc):
            off = pl.multiple_of(c * rows, rows)
            ind = pred(sc_ref[pl.ds(off, rows), :], off).astype(I32)
            return acc + jnp.sum(ind.reshape(rows // SUBLANES, SUBLANES, LANES), axis=0)

        acc = lax.fori_loop(0, nch * (ck // rows), body, jnp.zeros((SUBLANES, LANES), I32))
        return jnp.sum(acc, axis=0, keepdims=True)

    def key_value(k):
        return pltpu.bitcast(jnp.where(k < 0, INT_MIN - k, k), F32)

    def count_ge(k):
        thr = key_value(k)
        return count(lambda s, off: s >= thr)

    def full(v):
        return jnp.full((1, LANES), v, I32)

    def bisect(_, st):
        lo, hi, c_lo, c_hi = st
        mid = (lo >> 1) + (hi >> 1) + (lo & hi & 1)
        c = count_ge(mid)
        ge = c >= topk
        return jnp.where(ge, mid, lo), jnp.where(ge, hi, mid), jnp.where(ge, c, c_lo), jnp.where(ge, c_hi, c)

    searching = (i + 1) * BLOCK > topk

    def float_key(x):
        bits = pltpu.bitcast(x, I32)
        return jnp.where(bits < 0, INT_MIN - bits, bits)

    def search():
        k_lo = float_key(smax * 0.125)
        c = count_ge(k_lo)
        ok = (smax > 0.0) & (c >= topk)
        trips = jnp.where(jnp.min(jnp.where(ok, 1, 0)) > 0, 25, 32)
        st = (jnp.where(ok, k_lo, KEY_NEG_INF), float_key(smax) + 1, jnp.where(ok, c, nch * ck), full(0))
        out = lax.fori_loop(0, trips, bisect, st)
        return out[0], out[2], out[3]

    vkey, c_lo, c_hi = lax.cond(searching, search, lambda: (full(KEY_NEG_INF), full(topk), full(0)))
    vthr = key_value(vkey)
    need = topk - c_hi
    ties = c_lo - c_hi

    def tie_search():
        nxt = vkey + 1
        nxt = jnp.where((nxt > 0) & (nxt < KEY_MIN_NORMAL), KEY_MIN_NORMAL, nxt)
        step = key_value(nxt) - vthr

        def split(_, st):
            fl, fh = st
            fm = 0.5 * (fl + fh)
            thr = vthr + fm * step
            ge = count(lambda s, off: s >= thr) >= topk
            return jnp.where(ge, fm, fl), jnp.where(ge, fh, fm)

        fl, _ = lax.fori_loop(0, 26, split, (jnp.zeros((1, LANES), F32), jnp.ones((1, LANES), F32)))
        thr = vthr + fl * step
        want = topk - count(lambda s, off: s > thr)

        def body(_, st):
            lj, hj = st
            mid = (lj + hj) >> 1
            c = count(lambda s, off: (s == thr) & (off + kiota[:COUNT_ROWS] <= mid))
            ok = c >= want
            return jnp.where(ok, lj, mid), jnp.where(ok, mid, hj)

        _, hj = lax.fori_loop(0, 14, body, (full(-1), full(0) + (nch * ck - 1)))
        return thr, hj

    any_tie = searching & (jnp.max(ties - need) > 0)
    vthr, jmax = lax.cond(any_tie, tie_search, lambda: (vthr, full(seq_len)))

    def mask_body(c, _):
        off = chunk_off(c)
        s = sc_ref[pl.ds(off, ck), :]
        s_idx = off + kiota
        sel = ((s > vthr) | ((s == vthr) & (s_idx <= jmax))) & (s_idx <= t_row)
        sc_ref[pl.ds(off, ck), :] = jnp.where(sel, 0.0, NEG_INF)
        return 0

    lax.fori_loop(0, nch, mask_body, 0)

    q = q_ref[0]
    n_far = jnp.maximum((i - NEAR_BLOCKS + 1) // per, 0)
    hpg = C_HEADS // C_KV_HEADS

    for g in range(C_KV_HEADS):
        q_stack = [jnp.concatenate([q[:, (hpg * g + 2 * jj) * LANES:(hpg * g + 2 * jj + 1) * LANES],
                                    q[:, (hpg * g + 2 * jj + 1) * LANES:(hpg * g + 2 * jj + 2) * LANES]], axis=0)
                   for jj in range(hpg // 2)]

        def stage_body(near, g=g, q_stack=q_stack):
            def body(c, mx):
                off = chunk_off(c)
                madd = sc_ref[pl.ds(off, ck), :]
                kc = k_ref[0, pl.ds(off, ck), g * LANES:(g + 1) * LANES]
                tidx = [jnp.clip(i - (c * per + r), 0, NEAR_BLOCKS) for r in range(per)]
                out = []
                for jj in range(hpg // 2):
                    lg = _mm_nt(kc, q_stack[jj])
                    for a in range(2):
                        hl = 2 * jj + a
                        x = lg[:, a * LANES:(a + 1) * LANES] + madd
                        if near:
                            x = x + jnp.concatenate([tab_ref[hpg * g + hl, tidx[r]] for r in range(per)], axis=0)
                        x_ref[pl.ds(off, ck), hl * LANES:(hl + 1) * LANES] = x
                        out.append(jnp.maximum(mx[hl], _fold(x, jnp.maximum)))
                return tuple(out)

            return body

        mx = tuple(jnp.full((SUBLANES, LANES), NEG_INF, F32) for _ in range(hpg))
        mx = lax.fori_loop(0, n_far, stage_body(False), mx)
        mx = lax.fori_loop(n_far, nch, stage_body(True), mx)
        m = [jnp.max(v, axis=0, keepdims=True) for v in mx]
        acc_ref[...] = jnp.zeros(acc_ref.shape, F32)

        def att_body(c, ls, g=g, m=m):
            off = chunk_off(c)
            vt = vt_ref[0, g * LANES:(g + 1) * LANES, pl.ds(off, ck)]
            out = []
            for jj in range(hpg // 2):
                ps = []
                for a in range(2):
                    hl = 2 * jj + a
                    p = jnp.exp(x_ref[pl.ds(off, ck), hl * LANES:(hl + 1) * LANES] - m[hl])
                    out.append(ls[hl] + _fold(p, jnp.add))
                    ps.append(p.astype(BF16))
                acc_ref[jj] += _mm(vt, jnp.concatenate(ps, axis=1))
            return tuple(out)

        ls = lax.fori_loop(0, nch, att_body, tuple(jnp.zeros((SUBLANES, LANES), F32) for _ in range(hpg)))
        for hl in range(hpg):
            sl = slice((hpg * g + hl) * LANES, (hpg * g + hl + 1) * LANES)
            ot = acc_ref[hl // 2, :, (hl % 2) * LANES:(hl % 2 + 1) * LANES] / jnp.sum(ls[hl], axis=0, keepdims=True)
            o_ref[0, :, sl] = (ot.T * sg_ref[0, :, sl].astype(F32)).astype(BF16)


def _dsa(q, qi, wt, sg, k, vt, ki2, tab):
    b, l, cw = q.shape
    nb = l // BLOCK
    topk = min(TOPK_MAX, l // 4)

    def blk(n):
        return pl.BlockSpec((1, BLOCK, n), lambda bb, i: (bb, i, 0))

    def whole(s1, s2):
        return pl.BlockSpec((1, s1, s2), lambda bb, i: (bb, 0, 0), pipeline_mode=pl.Buffered(1))

    hpg = C_HEADS // C_KV_HEADS
    return pl.pallas_call(
        functools.partial(_dsa_kernel, seq_len=l, topk=topk),
        grid=(b, nb),
        in_specs=[blk(cw), blk(qi.shape[2]), pl.BlockSpec((1, IDX_HEADS, BLOCK), lambda bb, i: (bb, 0, i)), blk(cw),
                  whole(l, k.shape[2]), whole(vt.shape[1], l), whole(l, ki2.shape[2]),
                  pl.BlockSpec(tab.shape, lambda bb, i: (0, 0, 0, 0), pipeline_mode=pl.Buffered(1))],
        out_specs=blk(cw),
        out_shape=jax.ShapeDtypeStruct((b, l, cw), BF16),
        scratch_shapes=[pltpu.VMEM((l, LANES), F32), pltpu.VMEM((l, hpg * LANES), F32),
                        pltpu.VMEM((hpg // 2, C_HEAD_DIM, 2 * LANES), F32)],
        compiler_params=_params("arbitrary", "arbitrary"),
        name="dsa",
    )(q, qi, wt, sg, k, vt, ki2, tab)


def _out_kernel(h_ref, a_ref, w_ref, o_ref):
    o_ref[...] = h_ref[...] + _mm(a_ref[...], w_ref[...])


def _outproj(h2, a2, w):
    rows, d = h2.shape
    t = ROW_TILE
    return pl.pallas_call(
        _out_kernel,
        grid=(rows // t,),
        in_specs=[pl.BlockSpec((t, d), lambda i: (i, 0)), pl.BlockSpec((t, a2.shape[1]), lambda i: (i, 0)),
                  _const_spec(w.shape)],
        out_specs=pl.BlockSpec((t, d), lambda i: (i, 0)),
        out_shape=jax.ShapeDtypeStruct((rows, d), F32),
        compiler_params=_params("arbitrary"),
        name="outproj1",
    )(h2, a2, w)


def _bias_tables(rel_bias, seq_len):
    del seq_len
    nv = (NEAR_BLOCKS + 1) * BLOCK
    vec = rel_bias[_t5_bucket(jnp.arange(nv, dtype=I32))].astype(F32).T

    def window(lo, n):
        pad = max(0, -lo)
        body = vec[:, max(lo, 0):lo + n]
        return jnp.concatenate([jnp.broadcast_to(vec[:, :1], (vec.shape[0], pad)), body], axis=1)

    def toeplitz(g, rows, cols):
        w = rows + cols
        g2 = jnp.concatenate([g[:, rows - 1:rows - 1 + cols], g[:, :1], g[:, :rows - 1]], axis=1)
        flat = jnp.tile(g2, (1, rows))[:, :rows * (w - 1)]
        return flat.reshape(-1, rows, w - 1)[:, :, :cols]

    bias0 = jnp.transpose(toeplitz(window(BLOCK - (2 * BLOCK - 1), 3 * BLOCK - 1), 2 * BLOCK, BLOCK), (0, 2, 1))
    tiles = [toeplitz(window(dl * BLOCK - (BLOCK - 1), 2 * BLOCK - 1), BLOCK, BLOCK) for dl in range(NEAR_BLOCKS)]
    tab = jnp.stack(tiles, axis=1) - rel_bias[NUM_BUCKETS - 1].astype(F32)[:, None, None, None]
    tab = jnp.concatenate([tab, jnp.zeros((tab.shape[0], 1, BLOCK, BLOCK), F32)], axis=1)
    return bias0, tab


def kernel(x, rel_bias, norm_g, ev_w_in, ev_w_out, ev_q_norm_g, ev_k_norm_g, ev_sinks, ev_ssm_log_dt, ev_ssm_a_re,
           ev_ssm_a_im, ev_ssm_b_re, ev_ssm_b_im, ev_ssm_c_re, ev_ssm_c_im, ev_ssm_d, ev_glu_w, ev_glu_b, od_w_in,
           od_w_out, od_q_norm_g, od_k_norm_g):
    b, l, d = x.shape
    assert l % KEY_CHUNK == 0 and l % ROW_TILE == 0
    assert (NEAR_BLOCKS - 1) * BLOCK + 1 >= 16 * 64 ** (15 / 16) + 1
    bias0, tab = _bias_tables(rel_bias, l)

    q0, k0, v0, sga, u, sgb = _proj0(x.reshape(b * l, d), norm_g[0][None, :], ev_w_in[0].astype(BF16))
    shp = lambda a: a.reshape(b, l, a.shape[-1])
    qg2 = jnp.tile(ev_q_norm_g[0], 2)[None, :]
    kg2 = jnp.tile(ev_k_norm_g[0], 2)[None, :]
    sinks = jnp.broadcast_to(ev_sinks[0][:, None], (A_HEADS, LANES)).astype(F32)
    att0 = _attn0(shp(q0), shp(k0), shp(v0), shp(sga), bias0, sinks, qg2, kg2)
    bmat, cre, cim, sc = _s5_prep(ev_ssm_log_dt[0], ev_ssm_a_re[0], ev_ssm_a_im[0], ev_ssm_b_re[0], ev_ssm_b_im[0],
                                  ev_ssm_c_re[0], ev_ssm_c_im[0])
    ssm0 = _ssm(shp(u), shp(sgb), bmat, cre, cim, sc, ev_ssm_d[0].reshape(1, -1), ev_glu_w[0].astype(BF16),
                ev_glu_b[0][None, :])

    w1 = od_w_in[0]
    cw = C_HEADS * C_HEAD_DIM
    ckv = C_KV_HEADS * C_HEAD_DIM
    o = np.cumsum([0, cw, ckv, ckv, cw, IDX_HEADS * IDX_DIM, IDX_DIM, IDX_HEADS])
    wq, wk, wv, wg, wqi, wki, ww = (w1[:, o[n]:o[n + 1]] for n in range(7))
    zki = jnp.zeros((d, LANES - IDX_DIM), w1.dtype)
    wki2 = jnp.concatenate([wki, zki, zki, wki], axis=1)
    bf = lambda a: a.astype(BF16)
    h1, q1, k1, vt1, sg1, qi1, ki2, wt1 = _mid(
        x, att0, ssm0, bf(ev_w_out[0]), norm_g[1][None, :], bf(wq), bf(wk), bf(wv.T), bf(wg), bf(wqi), bf(wki2),
        bf(ww.T), od_q_norm_g[0][None, :], od_k_norm_g[0][None, :])
    att1 = _dsa(q1, qi1, wt1, sg1, k1, vt1, ki2, tab)
    out = _outproj(h1.reshape(b * l, d), att1.reshape(b * l, cw), bf(od_w_out[0]))
    return out.reshape(b, l, d)
```

```python
import functools
import math

import jax
import jax.numpy as jnp
import numpy as np
from jax import lax
from jax.experimental import pallas as pl
from jax.experimental.pallas import tpu as pltpu

F32 = jnp.float32
BF16 = jnp.bfloat16
I32 = jnp.int32

LANES = 128
SUBLANES = 8
VMEM_LIMIT = 56 * 1024 * 1024

BLOCK = 128
WINDOW = 128
A_HEADS = 8
A_HEAD_DIM = 64
A_KV_HEADS = 2
A_WIDTH = A_HEADS * A_HEAD_DIM
SSM_GROUP = 16
SSM_STATE = 64
C_HEADS = 8
C_HEAD_DIM = 128
C_KV_HEADS = 2
IDX_HEADS = 8
IDX_DIM = 64
TOPK_MAX = 256
NUM_BUCKETS = 32
REL_MAX_DIST = 1024
EPS = 1e-6
NEG_INF = -1e30
INT_MIN = -(2 ** 31)
KEY_MIN_NORMAL = 0x00800000
KEY_POS_INF = 0x7F800000
KEY_NEG_INF = INT_MIN + 0x00800000

ROW_TILE = 256
KEY_CHUNK = 1024
NEAR_BLOCKS = 8
FOLD_CHAINS = 8
COUNT_ROWS = 512
NT_DIMS = (((1,), (1,)), ((), ()))


def _t5_bucket(dist):
    n = jnp.maximum(dist, 0)
    max_exact = NUM_BUCKETS // 2
    nf = jnp.maximum(n, 1).astype(F32)
    large = max_exact + (jnp.log(nf / max_exact) / math.log(REL_MAX_DIST / max_exact)
                         * (NUM_BUCKETS - max_exact)).astype(I32)
    large = jnp.minimum(large, NUM_BUCKETS - 1)
    return jnp.where(n < max_exact, n, large)


def _silu(x):
    return x * jax.nn.sigmoid(x)


def _rms(x, g):
    ms = jnp.mean(x * x, axis=-1, keepdims=True)
    return x * lax.rsqrt(ms + EPS) * g


def _mm(a, b):
    return jnp.dot(a, b, preferred_element_type=F32)


def _mm_nt(a, b):
    return lax.dot_general(a, b, NT_DIMS, preferred_element_type=F32)


def _fold(x, op):
    n = x.shape[0] // SUBLANES
    chains = min(FOLD_CHAINS, n)
    accs = [x[r * SUBLANES:(r + 1) * SUBLANES] for r in range(chains)]
    for r in range(chains, n):
        accs[r % chains] = op(accs[r % chains], x[r * SUBLANES:(r + 1) * SUBLANES])
    while len(accs) > 1:
        accs = [op(a, b) for a, b in zip(accs[::2], accs[1::2])] + accs[len(accs) & ~1:]
    return accs[0]


def _params(*sem):
    return pltpu.CompilerParams(dimension_semantics=sem, vmem_limit_bytes=VMEM_LIMIT)


def _const_spec(shape):
    zeros = (0,) * len(shape)
    return pl.BlockSpec(shape, lambda *_: zeros)


def _proj0_kernel(x_ref, g_ref, w_ref, q_ref, k_ref, v_ref, sga_ref, u_ref, sgb_ref):
    hn = _rms(x_ref[...], g_ref[...]).astype(BF16)

    def mm(lo, hi):
        return _mm(hn, w_ref[:, lo:hi])

    q_ref[...] = mm(0, 512)
    k_ref[...] = mm(512, 640)
    v_ref[...] = mm(640, 768)
    sga_ref[...] = _silu(mm(768, 1280)).astype(BF16)
    u_ref[...] = mm(1280, 1792)
    sgb_ref[...] = _silu(mm(1792, 2304)).astype(BF16)


def _proj0(x2, g, w):
    rows, d = x2.shape
    t = ROW_TILE

    def row(n):
        return pl.BlockSpec((t, n), lambda i: (i, 0))

    return pl.pallas_call(
        _proj0_kernel,
        grid=(rows // t,),
        in_specs=[row(d), _const_spec((1, d)), _const_spec(w.shape)],
        out_specs=[row(512), row(128), row(128), row(512), row(512), row(512)],
        out_shape=[jax.ShapeDtypeStruct((rows, 512), F32), jax.ShapeDtypeStruct((rows, 128), F32),
                   jax.ShapeDtypeStruct((rows, 128), F32), jax.ShapeDtypeStruct((rows, 512), BF16),
                   jax.ShapeDtypeStruct((rows, 512), F32), jax.ShapeDtypeStruct((rows, 512), BF16)],
        compiler_params=_params("arbitrary"),
        name="proj0",
    )(x2, g, w)


def _attn0_kernel(q_ref, kc_ref, kp_ref, vc_ref, vp_ref, sga_ref, bias_ref, sink_ref, qg_ref, kg_ref, o_ref):
    i = pl.program_id(1)
    lane = lax.broadcasted_iota(I32, (1, LANES), 1)
    lo = lane < A_HEAD_DIM

    def segnorm(x, g2):
        sq = x * x
        s_lo = jnp.sum(jnp.where(lo, sq, 0.0), axis=-1, keepdims=True)
        s_hi = jnp.sum(jnp.where(lo, 0.0, sq), axis=-1, keepdims=True)
        inv = jnp.where(lo, lax.rsqrt(s_lo / A_HEAD_DIM + EPS), lax.rsqrt(s_hi / A_HEAD_DIM + EPS))
        return x * inv * g2

    kn = segnorm(jnp.concatenate([kp_ref[0], kc_ref[0]], axis=0), kg_ref[...])
    vb = jnp.concatenate([vp_ref[0], vc_ref[0]], axis=0)
    kr = pltpu.roll(kn, A_HEAD_DIM, axis=1)
    vr = pltpu.roll(vb, A_HEAD_DIM, axis=1)

    def variants(x, xr):
        return {(0, 0): jnp.where(lo, x, 0.0).astype(BF16), (0, 1): jnp.where(lo, 0.0, xr).astype(BF16),
                (1, 0): jnp.where(lo, xr, 0.0).astype(BF16), (1, 1): jnp.where(lo, 0.0, x).astype(BF16)}

    kvar = variants(kn, kr)
    vvar = variants(vb, vr)

    row = lax.broadcasted_iota(I32, (BLOCK, 2 * BLOCK), 0)
    col = lax.broadcasted_iota(I32, (BLOCK, 2 * BLOCK), 1)
    d = row + BLOCK - col
    mask = (d >= 0) & (d < WINDOW) & ((i > 0) | (col >= BLOCK))

    lgs, sinks = [], []
    for p in range(A_HEADS // 2):
        qp = (segnorm(q_ref[0, :, p * LANES:(p + 1) * LANES], qg_ref[...]) * (A_HEAD_DIM ** -0.5)).astype(BF16)
        for a in range(2):
            h = 2 * p + a
            lgs.append(jnp.where(mask, _mm_nt(qp, kvar[(p // 2, a)]) + bias_ref[h], NEG_INF))
            sinks.append(jnp.broadcast_to(sink_ref[h:h + 1, 0:1], (BLOCK, 1)))
    lg = jnp.concatenate(lgs, axis=0)
    sink = jnp.concatenate(sinks, axis=0)
    m = jnp.maximum(jnp.max(lg, axis=-1, keepdims=True), sink)
    e = jnp.exp(lg - m)
    den = jnp.sum(e, axis=-1, keepdims=True) + jnp.exp(sink - m)
    pr = (e / den).astype(BF16)
    for p in range(A_HEADS // 2):
        sl = slice(p * LANES, (p + 1) * LANES)
        acc = jnp.zeros((BLOCK, LANES), F32)
        for a in range(2):
            h = 2 * p + a
            acc = acc + _mm(pr[h * BLOCK:(h + 1) * BLOCK], vvar[(p // 2, a)])
        o_ref[0, :, sl] = (acc * sga_ref[0, :, sl].astype(F32)).astype(BF16)


def _attn0(q, k, v, sga, bias0, sinks, qg2, kg2):
    b, l, _ = q.shape
    nb = l // BLOCK

    def cur(n):
        return pl.BlockSpec((1, BLOCK, n), lambda bb, i: (bb, i, 0))

    def prev(n):
        return pl.BlockSpec((1, BLOCK, n), lambda bb, i: (bb, jnp.maximum(i - 1, 0), 0))

    return pl.pallas_call(
        _attn0_kernel,
        grid=(b, nb),
        in_specs=[cur(512), cur(128), prev(128), cur(128), prev(128), cur(512),
                  _const_spec(bias0.shape), _const_spec(sinks.shape), _const_spec(qg2.shape), _const_spec(kg2.shape)],
        out_specs=cur(512),
        out_shape=jax.ShapeDtypeStruct((b, l, 512), BF16),
        compiler_params=_params("arbitrary", "arbitrary"),
        name="attn0",
    )(q, k, k, v, v, sga, bias0, sinks, qg2, kg2)


def _ssm_kernel(u_ref, sgb_ref, bmat_ref, cre_ref, cim_ref, sc_ref, d_ref, gw_ref, gb_ref, o_ref, xre_ref, xim_ref):
    t = u_ref.shape[1]
    nq = bmat_ref.shape[0]
    half = bmat_ref.shape[2] // 2

    @pl.when(pl.program_id(1) == 0)
    def _():
        xre_ref[0:SUBLANES, :] = jnp.zeros((SUBLANES, xre_ref.shape[1]), F32)
        xim_ref[0:SUBLANES, :] = jnp.zeros((SUBLANES, xim_ref.shape[1]), F32)

    u = u_ref[0]
    ub = u.astype(BF16)
    for q in range(nq):
        bu = _mm(ub[:, q * LANES:(q + 1) * LANES], bmat_ref[q])
        xre_ref[SUBLANES:, q * half:(q + 1) * half] = bu[:, :half]
        xim_ref[SUBLANES:, q * half:(q + 1) * half] = bu[:, half:]

    def scan(r, _):
        base = pl.multiple_of(SUBLANES + r * SUBLANES, SUBLANES)
        xr = xre_ref[pl.ds(base, SUBLANES), :]
        xi = xim_ref[pl.ds(base, SUBLANES), :]
        for s, k in enumerate((1, 2, 4)):
            ar = sc_ref[2 * s]
            ai = sc_ref[2 * s + 1]
            sr = pltpu.roll(xr, k, axis=0)
            si = pltpu.roll(xi, k, axis=0)
            xr, xi = xr + ar * sr - ai * si, xi + ar * si + ai * sr
        cr = xre_ref[pl.ds(base - 1, 1), :]
        ci = xim_ref[pl.ds(base - 1, 1), :]
        pr = sc_ref[6]
        pi = sc_ref[7]
        xre_ref[pl.ds(base, SUBLANES), :] = xr + pr * cr - pi * ci
        xim_ref[pl.ds(base, SUBLANES), :] = xi + pr * ci + pi * cr
        return 0

    lax.fori_loop(0, t // SUBLANES, scan, 0, unroll=2)
    xre_ref[0:SUBLANES, :] = xre_ref[t:t + SUBLANES, :]
    xim_ref[0:SUBLANES, :] = xim_ref[t:t + SUBLANES, :]

    ys = []
    for q in range(nq):
        xr = xre_ref[SUBLANES:, q * half:(q + 1) * half].astype(BF16)
        xi = xim_ref[SUBLANES:, q * half:(q + 1) * half].astype(BF16)
        ys.append(_mm(xr, cre_ref[q]) + _mm(xi, cim_ref[q]))
    y = jnp.concatenate(ys, axis=1) + d_ref[...] * u
    y = jax.nn.gelu(y).astype(BF16)
    hh = _mm(y, gw_ref[...]) + gb_ref[...]
    w = hh.shape[1] // 2
    o_ref[0] = (hh[:, :w] * jax.nn.sigmoid(hh[:, w:]) * sgb_ref[0].astype(F32)).astype(BF16)


def _ssm(u, sgb, bmat, cre, cim, sc, dskip, gw, gb):
    b, l, w = u.shape
    t = ROW_TILE
    ns = sc.shape[-1]

    def row(n):
        return pl.BlockSpec((1, t, n), lambda bb, i: (bb, i, 0))

    return pl.pallas_call(
        _ssm_kernel,
        grid=(b, l // t),
        in_specs=[row(w), row(w), _const_spec(bmat.shape), _const_spec(cre.shape), _const_spec(cim.shape),
                  _const_spec(sc.shape), _const_spec(dskip.shape), _const_spec(gw.shape), _const_spec(gb.shape)],
        out_specs=row(w),
        out_shape=jax.ShapeDtypeStruct((b, l, w), BF16),
        scratch_shapes=[pltpu.VMEM((SUBLANES + t, ns), F32), pltpu.VMEM((SUBLANES + t, ns), F32)],
        compiler_params=_params("arbitrary", "arbitrary"),
        name="ssm",
    )(u, sgb, bmat, cre, cim, sc, dskip, gw, gb)


def _s5_prep(log_dt, a_re, a_im, b_re, b_im, c_re, c_im):
    g, p = a_re.shape
    h = b_re.shape[-1]
    gl = LANES // h
    nq = g // gl
    dt = jnp.exp(log_dt)[:, None]
    mag = jnp.exp(a_re * dt)
    ang = a_im * dt
    ab_re = mag * jnp.cos(ang)
    ab_im = mag * jnp.sin(ang)
    den = a_re * a_re + a_im * a_im
    n_re = ab_re - 1.0
    n_im = ab_im
    f_re = (n_re * a_re + n_im * a_im) / den
    f_im = (n_im * a_re - n_re * a_im) / den
    bb_re = f_re[..., None] * b_re - f_im[..., None] * b_im
    bb_im = f_re[..., None] * b_im + f_im[..., None] * b_re
    eye = jnp.eye(gl, dtype=F32)

    def bdiag_in(m):
        m = m.reshape(nq, gl, p, h)
        return jnp.einsum('qgph,gk->qghkp', m, eye).reshape(nq, gl * h, gl * p)

    def bdiag_out(m):
        m = m.reshape(nq, gl, h, p)
        return jnp.einsum('qghp,gk->qgpkh', m, eye).reshape(nq, gl * p, gl * h)

    bmat = jnp.concatenate([bdiag_in(bb_re), bdiag_in(bb_im)], axis=2).astype(BF16)
    cre = bdiag_out(c_re).astype(BF16)
    cim = bdiag_out(-c_im).astype(BF16)

    pw = [(ab_re.reshape(-1), ab_im.reshape(-1))]
    for _ in range(SUBLANES - 1):
        pr, pi = pw[-1]
        pw.append((pr * pw[0][0] - pi * pw[0][1], pr * pw[0][1] + pi * pw[0][0]))
    rows = jnp.arange(SUBLANES)[:, None]
    sc = []
    for k in (1, 2, 4):
        sc.append(jnp.where(rows >= k, pw[k - 1][0][None, :], 0.0))
        sc.append(jnp.where(rows >= k, pw[k - 1][1][None, :], 0.0))
    sc.append(jnp.stack([pw[r][0] for r in range(SUBLANES)]))
    sc.append(jnp.stack([pw[r][1] for r in range(SUBLANES)]))
    return bmat, cre, cim, jnp.stack(sc).astype(F32)


def _mid_kernel(x_ref, a_ref, s_ref, wo_ref, g_ref, wq_ref, wk_ref, wvt_ref, wg_ref, wqi_ref, wki_ref, wwt_ref,
                qg_ref, kg_ref, h_ref, q_ref, k_ref, vt_ref, sg_ref, qi_ref, ki_ref, wt_ref):
    aw = a_ref.shape[2]
    h = x_ref[0] + _mm(a_ref[0], wo_ref[0:aw, :]) + _mm(s_ref[0], wo_ref[aw:, :])
    h_ref[0] = h
    hn = _rms(h, g_ref[...]).astype(BF16)
    qf = _mm(hn, wq_ref[...])
    for hd in range(C_HEADS):
        sl = slice(hd * C_HEAD_DIM, (hd + 1) * C_HEAD_DIM)
        q_ref[0, :, sl] = (_rms(qf[:, sl], qg_ref[...]) * (C_HEAD_DIM ** -0.5)).astype(BF16)
    kf = _mm(hn, wk_ref[...])
    for hd in range(C_KV_HEADS):
        sl = slice(hd * C_HEAD_DIM, (hd + 1) * C_HEAD_DIM)
        k_ref[0, :, sl] = _rms(kf[:, sl], kg_ref[...]).astype(BF16)
    vt_ref[0] = _mm_nt(wvt_ref[...], hn).astype(BF16)
    sg_ref[0] = _silu(_mm(hn, wg_ref[...])).astype(BF16)
    qi_ref[0] = _mm(hn, wqi_ref[...]).astype(BF16)
    ki_ref[0] = _mm(hn, wki_ref[...]).astype(BF16)
    wt_ref[0] = _mm_nt(wwt_ref[...], hn) * ((IDX_HEADS ** -0.5) * (IDX_DIM ** -0.5))


def _mid(x, att0, ssm0, wo, g, wq, wk, wvt, wg, wqi, wki2, wwt, qg, kg):
    b, l, d = x.shape
    t = ROW_TILE

    def row(n):
        return pl.BlockSpec((1, t, n), lambda bb, i: (bb, i, 0))

    def col(n):
        return pl.BlockSpec((1, n, t), lambda bb, i: (bb, 0, i))

    weights = [wo, g, wq, wk, wvt, wg, wqi, wki2, wwt, qg, kg]
    cw = C_HEADS * C_HEAD_DIM
    ckv = C_KV_HEADS * C_HEAD_DIM
    return pl.pallas_call(
        _mid_kernel,
        grid=(b, l // t),
        in_specs=[row(d), row(att0.shape[2]), row(ssm0.shape[2])] + [_const_spec(w.shape) for w in weights],
        out_specs=[row(d), row(cw), row(ckv), col(ckv), row(cw), row(IDX_HEADS * IDX_DIM), row(2 * LANES),
                   col(IDX_HEADS)],
        out_shape=[jax.ShapeDtypeStruct((b, l, d), F32), jax.ShapeDtypeStruct((b, l, cw), BF16),
                   jax.ShapeDtypeStruct((b, l, ckv), BF16), jax.ShapeDtypeStruct((b, ckv, l), BF16),
                   jax.ShapeDtypeStruct((b, l, cw), BF16), jax.ShapeDtypeStruct((b, l, IDX_HEADS * IDX_DIM), BF16),
                   jax.ShapeDtypeStruct((b, l, 2 * LANES), BF16), jax.ShapeDtypeStruct((b, IDX_HEADS, l), F32)],
        compiler_params=_params("arbitrary", "arbitrary"),
        name="mid",
    )(x, att0, ssm0, *weights)


def _dsa_kernel(q_ref, qi_ref, wt_ref, sg_ref, k_ref, vt_ref, ki_ref, tab_ref, o_ref,
                sc_ref, x_ref, acc_ref, *, seq_len, topk):
    i = pl.program_id(1)
    ck = KEY_CHUNK
    per = ck // BLOCK
    nch = (i + per) // per
    t_row = i * BLOCK + lax.broadcasted_iota(I32, (1, LANES), 1)
    kiota = lax.broadcasted_iota(I32, (ck, LANES), 0)

    def chunk_off(c):
        return pl.multiple_of(c * ck, ck)

    qi = qi_ref[0]
    qi_stack = [jnp.concatenate([qi[:, (2 * s) * LANES:(2 * s + 1) * LANES],
                                 qi[:, (2 * s + 1) * LANES:(2 * s + 2) * LANES]], axis=0) for s in range(2)]
    wt = wt_ref[0]

    def score_chunk(c, masked):
        off = chunk_off(c)
        sc = jnp.zeros((ck, LANES), F32)
        for a in range(2):
            kk = ki_ref[0, pl.ds(off, ck), a * LANES:(a + 1) * LANES]
            for s in range(2):
                r = _mm_nt(kk, qi_stack[s])
                for j in range(2):
                    hd = 2 * (2 * s + j) + a
                    sc = sc + jnp.maximum(r[:, j * LANES:(j + 1) * LANES], 0.0) * wt[hd:hd + 1, :]
        if masked:
            sc = jnp.where(off + kiota <= t_row, sc, NEG_INF)
        sc_ref[pl.ds(off, ck), :] = sc
        return _fold(sc, jnp.maximum)

    def score_body(c, mx):
        return jnp.maximum(mx, score_chunk(c, False))

    smax = lax.fori_loop(0, nch - 1, score_body, jnp.full((SUBLANES, LANES), NEG_INF, F32))
    smax = jnp.max(jnp.maximum(smax, score_chunk(nch - 1, True)), axis=0, keepdims=True)

    def count(pred):
        rows = COUNT_ROWS

        def body(c, acc):
            off = pl.multiple_of(c * rows, rows)
            ind = pred(sc_ref[pl.ds(off, rows), :], off).astype(I32)
            return acc + jnp.sum(ind.reshape(rows // SUBLANES, SUBLANES, LANES), axis=0)

        acc = lax.fori_loop(0, nch * (ck // rows), body, jnp.zeros((SUBLANES, LANES), I32))
        return jnp.sum(acc, axis=0, keepdims=True)

    def key_value(k):
        return pltpu.bitcast(jnp.where(k < 0, INT_MIN - k, k), F32)

    def count_ge(k):
        thr = key_value(k)
        return count(lambda s, off: s >= thr)

    def full(v):
        return jnp.full((1, LANES), v, I32)

    def bisect(_, st):
        lo, hi, c_lo, c_hi = st
        mid = (lo >> 1) + (hi >> 1) + (lo & hi & 1)
        c = count_ge(mid)
        ge = c >= topk
        return jnp.where(ge, mid, lo), jnp.where(ge, hi, mid), jnp.where(ge, c, c_lo), jnp.where(ge, c_hi, c)

    searching = (i + 1) * BLOCK > topk

    def float_key(x):
        bits = pltpu.bitcast(x, I32)
        return jnp.where(bits < 0, INT_MIN - bits, bits)

    def search():
        k_lo = float_key(smax * 0.125)
        c = count_ge(k_lo)
        ok = (smax > 0.0) & (c >= topk)
        trips = jnp.where(jnp.min(jnp.where(ok, 1, 0)) > 0, 25, 32)
        st = (jnp.where(ok, k_lo, KEY_NEG_INF), float_key(smax) + 1, jnp.where(ok, c, nch * ck), full(0))
        out = lax.fori_loop(0, trips, bisect, st)
        return out[0], out[2], out[3]

    vkey, c_lo, c_hi = lax.cond(searching, search, lambda: (full(KEY_NEG_INF), full(topk), full(0)))
    vthr = key_value(vkey)
    need = topk - c_hi
    ties = c_lo - c_hi

    def tie_search():
        nxt = vkey + 1
        nxt = jnp.where((nxt > 0) & (nxt < KEY_MIN_NORMAL), KEY_MIN_NORMAL, nxt)
        step = key_value(nxt) - vthr

        def split(_, st):
            fl, fh = st
            fm = 0.5 * (fl + fh)
            thr = vthr + fm * step
            ge = count(lambda s, off: s >= thr) >= topk
            return jnp.where(ge, fm, fl), jnp.where(ge, fh, fm)

        fl, _ = lax.fori_loop(0, 26, split, (jnp.zeros((1, LANES), F32), jnp.ones((1, LANES), F32)))
        thr = vthr + fl * step
        want = topk - count(lambda s, off: s > thr)

        def body(_, st):
            lj, hj = st
            mid = (lj + hj) >> 1
            c = count(lambda s, off: (s == thr) & (off + kiota[:COUNT_ROWS] <= mid))
            ok = c >= want
            return jnp.where(ok, lj, mid), jnp.where(ok, mid, hj)

        _, hj = lax.fori_loop(0, 14, body, (full(-1), full(0) + (nch * ck - 1)))
        return thr, hj

    any_tie = searching & (jnp.max(ties - need) > 0)
    vthr, jmax = lax.cond(any_tie, tie_search, lambda: (vthr, full(seq_len)))

    def mask_body(c, _):
        off = chunk_off(c)
        s = sc_ref[pl.ds(off, ck), :]
        s_idx = off + kiota
        sel = ((s > vthr) | ((s == vthr) & (s_idx <= jmax))) & (s_idx <= t_row)
        sc_ref[pl.ds(off, ck), :] = jnp.where(sel, 0.0, NEG_INF)
        return 0

    lax.fori_loop(0, nch, mask_body, 0)

    q = q_ref[0]
    n_far = jnp.maximum((i - NEAR_BLOCKS + 1) // per, 0)
    hpg = C_HEADS // C_KV_HEADS

    for g in range(C_KV_HEADS):
        q_stack = [jnp.concatenate([q[:, (hpg * g + 2 * jj) * LANES:(hpg * g + 2 * jj + 1) * LANES],
                                    q[:, (hpg * g + 2 * jj + 1) * LANES:(hpg * g + 2 * jj + 2) * LANES]], axis=0)
                   for jj in range(hpg // 2)]

        def stage_body(near, g=g, q_stack=q_stack):
            def body(c, mx):
                off = chunk_off(c)
                madd = sc_ref[pl.ds(off, ck), :]
                kc = k_ref[0, pl.ds(off, ck), g * LANES:(g + 1) * LANES]
                tidx = [jnp.clip(i - (c * per + r), 0, NEAR_BLOCKS) for r in range(per)]
                out = []
                for jj in range(hpg // 2):
                    lg = _mm_nt(kc, q_stack[jj])
                    for a in range(2):
                        hl = 2 * jj + a
                        x = lg[:, a * LANES:(a + 1) * LANES] + madd
                        if near:
                            x = x + jnp.concatenate([tab_ref[hpg * g + hl, tidx[r]] for r in range(per)], axis=0)
                        x_ref[hl, pl.ds(off, ck), :] = x
                        out.append(jnp.maximum(mx[hl], _fold(x, jnp.maximum)))
                return tuple(out)

            return body

        mx = tuple(jnp.full((SUBLANES, LANES), NEG_INF, F32) for _ in range(hpg))
        mx = lax.fori_loop(0, n_far, stage_body(False), mx)
        mx = lax.fori_loop(n_far, nch, stage_body(True), mx)
        m = [jnp.max(v, axis=0, keepdims=True) for v in mx]
        acc_ref[...] = jnp.zeros(acc_ref.shape, F32)

        def att_body(c, ls, g=g, m=m):
            off = chunk_off(c)
            vt = vt_ref[0, g * LANES:(g + 1) * LANES, pl.ds(off, ck)]
            out = []
            for jj in range(hpg // 2):
                ps = []
                for a in range(2):
                    hl = 2 * jj + a
                    p = jnp.exp(x_ref[hl, pl.ds(off, ck), :] - m[hl])
                    out.append(ls[hl] + _fold(p, jnp.add))
                    ps.append(p.astype(BF16))
                acc_ref[jj] += _mm(vt, jnp.concatenate(ps, axis=1))
            return tuple(out)

        ls = lax.fori_loop(0, nch, att_body, tuple(jnp.zeros((SUBLANES, LANES), F32) for _ in range(hpg)))
        for hl in range(hpg):
            sl = slice((hpg * g + hl) * LANES, (hpg * g + hl + 1) * LANES)
            ot = acc_ref[hl // 2, :, (hl % 2) * LANES:(hl % 2 + 1) * LANES] / jnp.sum(ls[hl], axis=0, keepdims=True)
            o_ref[0, :, sl] = (ot.T * sg_ref[0, :, sl].astype(F32)).astype(BF16)


def _dsa(q, qi, wt, sg, k, vt, ki2, tab):
    b, l, cw = q.shape
    nb = l // BLOCK
    topk = min(TOPK_MAX, l // 4)

    def blk(n):
        return pl.BlockSpec((1, BLOCK, n), lambda bb, i: (bb, i, 0))

    def whole(s1, s2):
        return pl.BlockSpec((1, s1, s2), lambda bb, i: (bb, 0, 0), pipeline_mode=pl.Buffered(1))

    hpg = C_HEADS // C_KV_HEADS
    return pl.pallas_call(
        functools.partial(_dsa_kernel, seq_len=l, topk=topk),
        grid=(b, nb),
        in_specs=[blk(cw), blk(qi.shape[2]), pl.BlockSpec((1, IDX_HEADS, BLOCK), lambda bb, i: (bb, 0, i)), blk(cw),
                  whole(l, k.shape[2]), whole(vt.shape[1], l), whole(l, ki2.shape[2]),
                  pl.BlockSpec(tab.shape, lambda bb, i: (0, 0, 0, 0), pipeline_mode=pl.Buffered(1))],
        out_specs=blk(cw),
        out_shape=jax.ShapeDtypeStruct((b, l, cw), BF16),
        scratch_shapes=[pltpu.VMEM((l, LANES), F32), pltpu.VMEM((hpg, l, LANES), F32),
                        pltpu.VMEM((hpg // 2, C_HEAD_DIM, 2 * LANES), F32)],
        compiler_params=_params("arbitrary", "arbitrary"),
        name="dsa",
    )(q, qi, wt, sg, k, vt, ki2, tab)


def _out_kernel(h_ref, a_ref, w_ref, o_ref):
    o_ref[...] = h_ref[...] + _mm(a_ref[...], w_ref[...])


def _outproj(h2, a2, w):
    rows, d = h2.shape
    t = ROW_TILE
    return pl.pallas_call(
        _out_kernel,
        grid=(rows // t,),
        in_specs=[pl.BlockSpec((t, d), lambda i: (i, 0)), pl.BlockSpec((t, a2.shape[1]), lambda i: (i, 0)),
                  _const_spec(w.shape)],
        out_specs=pl.BlockSpec((t, d), lambda i: (i, 0)),
        out_shape=jax.ShapeDtypeStruct((rows, d), F32),
        compiler_params=_params("arbitrary"),
        name="outproj1",
    )(h2, a2, w)


def _bias_tables(rel_bias, seq_len):
    del seq_len
    nv = (NEAR_BLOCKS + 1) * BLOCK
    vec = rel_bias[_t5_bucket(jnp.arange(nv, dtype=I32))].astype(F32).T

    def window(lo, n):
        pad = max(0, -lo)
        body = vec[:, max(lo, 0):lo + n]
        return jnp.concatenate([jnp.broadcast_to(vec[:, :1], (vec.shape[0], pad)), body], axis=1)

    def toeplitz(g, rows, cols):
        w = rows + cols
        g2 = jnp.concatenate([g[:, rows - 1:rows - 1 + cols], g[:, :1], g[:, :rows - 1]], axis=1)
        flat = jnp.tile(g2, (1, rows))[:, :rows * (w - 1)]
        return flat.reshape(-1, rows, w - 1)[:, :, :cols]

    bias0 = jnp.transpose(toeplitz(window(BLOCK - (2 * BLOCK - 1), 3 * BLOCK - 1), 2 * BLOCK, BLOCK), (0, 2, 1))
    tiles = [toeplitz(window(dl * BLOCK - (BLOCK - 1), 2 * BLOCK - 1), BLOCK, BLOCK) for dl in range(NEAR_BLOCKS)]
    tab = jnp.stack(tiles, axis=1) - rel_bias[NUM_BUCKETS - 1].astype(F32)[:, None, None, None]
    tab = jnp.concatenate([tab, jnp.zeros((tab.shape[0], 1, BLOCK, BLOCK), F32)], axis=1)
    return bias0, tab


def kernel(x, rel_bias, norm_g, ev_w_in, ev_w_out, ev_q_norm_g, ev_k_norm_g, ev_sinks, ev_ssm_log_dt, ev_ssm_a_re,
           ev_ssm_a_im, ev_ssm_b_re, ev_ssm_b_im, ev_ssm_c_re, ev_ssm_c_im, ev_ssm_d, ev_glu_w, ev_glu_b, od_w_in,
           od_w_out, od_q_norm_g, od_k_norm_g):
    b, l, d = x.shape
    assert l % KEY_CHUNK == 0 and l % ROW_TILE == 0
    assert (NEAR_BLOCKS - 1) * BLOCK + 1 >= 16 * 64 ** (15 / 16) + 1
    bias0, tab = _bias_tables(rel_bias, l)

    q0, k0, v0, sga, u, sgb = _proj0(x.reshape(b * l, d), norm_g[0][None, :], ev_w_in[0].astype(BF16))
    shp = lambda a: a.reshape(b, l, a.shape[-1])
    qg2 = jnp.tile(ev_q_norm_g[0], 2)[None, :]
    kg2 = jnp.tile(ev_k_norm_g[0], 2)[None, :]
    sinks = jnp.broadcast_to(ev_sinks[0][:, None], (A_HEADS, LANES)).astype(F32)
    att0 = _attn0(shp(q0), shp(k0), shp(v0), shp(sga), bias0, sinks, qg2, kg2)
    bmat, cre, cim, sc = _s5_prep(ev_ssm_log_dt[0], ev_ssm_a_re[0], ev_ssm_a_im[0], ev_ssm_b_re[0], ev_ssm_b_im[0],
                                  ev_ssm_c_re[0], ev_ssm_c_im[0])
    ssm0 = _ssm(shp(u), shp(sgb), bmat, cre, cim, sc, ev_ssm_d[0].reshape(1, -1), ev_glu_w[0].astype(BF16),
                ev_glu_b[0][None, :])

    w1 = od_w_in[0]
    cw = C_HEADS * C_HEAD_DIM
    ckv = C_KV_HEADS * C_HEAD_DIM
    o = np.cumsum([0, cw, ckv, ckv, cw, IDX_HEADS * IDX_DIM, IDX_DIM, IDX_HEADS])
    wq, wk, wv, wg, wqi, wki, ww = (w1[:, o[n]:o[n + 1]] for n in range(7))
    zki = jnp.zeros((d, LANES - IDX_DIM), w1.dtype)
    wki2 = jnp.concatenate([wki, zki, zki, wki], axis=1)
    bf = lambda a: a.astype(BF16)
    h1, q1, k1, vt1, sg1, qi1, ki2, wt1 = _mid(
        x, att0, ssm0, bf(ev_w_out[0]), norm_g[1][None, :], bf(wq), bf(wk), bf(wv.T), bf(wg), bf(wqi), bf(wki2),
        bf(ww.T), od_q_norm_g[0][None, :], od_k_norm_g[0][None, :])
    att1 = _dsa(q1, qi1, wt1, sg1, k1, vt1, ki2, tab)
    out = _outproj(h1.reshape(b * l, d), att1.reshape(b * l, cw), bf(od_w_out[0]))
    return out.reshape(b, l, d)
```

```python
import functools
import math

import jax
import jax.numpy as jnp
import numpy as np
from jax import lax
from jax.experimental import pallas as pl
from jax.experimental.pallas import tpu as pltpu

F32 = jnp.float32
BF16 = jnp.bfloat16
I32 = jnp.int32

LANES = 128
SUBLANES = 8
VMEM_LIMIT = 56 * 1024 * 1024

BLOCK = 128
WINDOW = 128
A_HEADS = 8
A_HEAD_DIM = 64
A_KV_HEADS = 2
A_WIDTH = A_HEADS * A_HEAD_DIM
SSM_GROUP = 16
SSM_STATE = 64
C_HEADS = 8
C_HEAD_DIM = 128
C_KV_HEADS = 2
IDX_HEADS = 8
IDX_DIM = 64
TOPK_MAX = 256
NUM_BUCKETS = 32
REL_MAX_DIST = 1024
EPS = 1e-6
NEG_INF = -1e30
INT_MIN = -(2 ** 31)
KEY_MIN_NORMAL = 0x00800000
KEY_POS_INF = 0x7F800000
KEY_NEG_INF = INT_MIN + 0x00800000

ROW_TILE = 256
KEY_CHUNK = 1024
NEAR_BLOCKS = 8
FOLD_CHAINS = 8
COUNT_ROWS = 512
NT_DIMS = (((1,), (1,)), ((), ()))


def _t5_bucket(dist):
    n = jnp.maximum(dist, 0)
    max_exact = NUM_BUCKETS // 2
    nf = jnp.maximum(n, 1).astype(F32)
    large = max_exact + (jnp.log(nf / max_exact) / math.log(REL_MAX_DIST / max_exact)
                         * (NUM_BUCKETS - max_exact)).astype(I32)
    large = jnp.minimum(large, NUM_BUCKETS - 1)
    return jnp.where(n < max_exact, n, large)


def _silu(x):
    return x * jax.nn.sigmoid(x)


def _rms(x, g):
    ms = jnp.mean(x * x, axis=-1, keepdims=True)
    return x * lax.rsqrt(ms + EPS) * g


def _mm(a, b):
    return jnp.dot(a, b, preferred_element_type=F32)


def _mm_nt(a, b):
    return lax.dot_general(a, b, NT_DIMS, preferred_element_type=F32)


def _fold(x, op):
    n = x.shape[0] // SUBLANES
    chains = min(FOLD_CHAINS, n)
    accs = [x[r * SUBLANES:(r + 1) * SUBLANES] for r in range(chains)]
    for r in range(chains, n):
        accs[r % chains] = op(accs[r % chains], x[r * SUBLANES:(r + 1) * SUBLANES])
    while len(accs) > 1:
        accs = [op(a, b) for a, b in zip(accs[::2], accs[1::2])] + accs[len(accs) & ~1:]
    return accs[0]


def _params(*sem):
    return pltpu.CompilerParams(dimension_semantics=sem, vmem_limit_bytes=VMEM_LIMIT)


def _const_spec(shape):
    zeros = (0,) * len(shape)
    return pl.BlockSpec(shape, lambda *_: zeros)


def _proj0_kernel(x_ref, g_ref, w_ref, q_ref, k_ref, v_ref, sga_ref, u_ref, sgb_ref):
    hn = _rms(x_ref[...], g_ref[...]).astype(BF16)

    def mm(lo, hi):
        return _mm(hn, w_ref[:, lo:hi])

    q_ref[...] = mm(0, 512)
    k_ref[...] = mm(512, 640)
    v_ref[...] = mm(640, 768)
    sga_ref[...] = _silu(mm(768, 1280)).astype(BF16)
    u_ref[...] = mm(1280, 1792)
    sgb_ref[...] = _silu(mm(1792, 2304)).astype(BF16)


def _proj0(x2, g, w):
    rows, d = x2.shape
    t = ROW_TILE

    def row(n):
        return pl.BlockSpec((t, n), lambda i: (i, 0))

    return pl.pallas_call(
        _proj0_kernel,
        grid=(rows // t,),
        in_specs=[row(d), _const_spec((1, d)), _const_spec(w.shape)],
        out_specs=[row(512), row(128), row(128), row(512), row(512), row(512)],
        out_shape=[jax.ShapeDtypeStruct((rows, 512), F32), jax.ShapeDtypeStruct((rows, 128), F32),
                   jax.ShapeDtypeStruct((rows, 128), F32), jax.ShapeDtypeStruct((rows, 512), BF16),
                   jax.ShapeDtypeStruct((rows, 512), F32), jax.ShapeDtypeStruct((rows, 512), BF16)],
        compiler_params=_params("arbitrary"),
        name="proj0",
    )(x2, g, w)


def _attn0_kernel(q_ref, kc_ref, kp_ref, vc_ref, vp_ref, sga_ref, bias_ref, sink_ref, qg_ref, kg_ref, o_ref):
    i = pl.program_id(1)
    lane = lax.broadcasted_iota(I32, (1, LANES), 1)
    lo = lane < A_HEAD_DIM

    def segnorm(x, g2):
        sq = x * x
        s_lo = jnp.sum(jnp.where(lo, sq, 0.0), axis=-1, keepdims=True)
        s_hi = jnp.sum(jnp.where(lo, 0.0, sq), axis=-1, keepdims=True)
        inv = jnp.where(lo, lax.rsqrt(s_lo / A_HEAD_DIM + EPS), lax.rsqrt(s_hi / A_HEAD_DIM + EPS))
        return x * inv * g2

    kn = segnorm(jnp.concatenate([kp_ref[0], kc_ref[0]], axis=0), kg_ref[...])
    vb = jnp.concatenate([vp_ref[0], vc_ref[0]], axis=0)
    kr = pltpu.roll(kn, A_HEAD_DIM, axis=1)
    vr = pltpu.roll(vb, A_HEAD_DIM, axis=1)

    def variants(x, xr):
        return {(0, 0): jnp.where(lo, x, 0.0).astype(BF16), (0, 1): jnp.where(lo, 0.0, xr).astype(BF16),
                (1, 0): jnp.where(lo, xr, 0.0).astype(BF16), (1, 1): jnp.where(lo, 0.0, x).astype(BF16)}

    kvar = variants(kn, kr)
    vvar = variants(vb, vr)

    row = lax.broadcasted_iota(I32, (BLOCK, 2 * BLOCK), 0)
    col = lax.broadcasted_iota(I32, (BLOCK, 2 * BLOCK), 1)
    d = row + BLOCK - col
    mask = (d >= 0) & (d < WINDOW) & ((i > 0) | (col >= BLOCK))

    lgs, sinks = [], []
    for p in range(A_HEADS // 2):
        qp = (segnorm(q_ref[0, :, p * LANES:(p + 1) * LANES], qg_ref[...]) * (A_HEAD_DIM ** -0.5)).astype(BF16)
        for a in range(2):
            h = 2 * p + a
            lgs.append(jnp.where(mask, _mm_nt(qp, kvar[(p // 2, a)]) + bias_ref[h], NEG_INF))
            sinks.append(jnp.broadcast_to(sink_ref[h:h + 1, 0:1], (BLOCK, 1)))
    lg = jnp.concatenate(lgs, axis=0)
    sink = jnp.concatenate(sinks, axis=0)
    m = jnp.maximum(jnp.max(lg, axis=-1, keepdims=True), sink)
    e = jnp.exp(lg - m)
    den = jnp.sum(e, axis=-1, keepdims=True) + jnp.exp(sink - m)
    pr = (e / den).astype(BF16)
    for p in range(A_HEADS // 2):
        sl = slice(p * LANES, (p + 1) * LANES)
        acc = jnp.zeros((BLOCK, LANES), F32)
        for a in range(2):
            h = 2 * p + a
            acc = acc + _mm(pr[h * BLOCK:(h + 1) * BLOCK], vvar[(p // 2, a)])
        o_ref[0, :, sl] = (acc * sga_ref[0, :, sl].astype(F32)).astype(BF16)


def _attn0(q, k, v, sga, bias0, sinks, qg2, kg2):
    b, l, _ = q.shape
    nb = l // BLOCK

    def cur(n):
        return pl.BlockSpec((1, BLOCK, n), lambda bb, i: (bb, i, 0))

    def prev(n):
        return pl.BlockSpec((1, BLOCK, n), lambda bb, i: (bb, jnp.maximum(i - 1, 0), 0))

    return pl.pallas_call(
        _attn0_kernel,
        grid=(b, nb),
        in_specs=[cur(512), cur(128), prev(128), cur(128), prev(128), cur(512),
                  _const_spec(bias0.shape), _const_spec(sinks.shape), _const_spec(qg2.shape), _const_spec(kg2.shape)],
        out_specs=cur(512),
        out_shape=jax.ShapeDtypeStruct((b, l, 512), BF16),
        compiler_params=_params("arbitrary", "arbitrary"),
        name="attn0",
    )(q, k, k, v, v, sga, bias0, sinks, qg2, kg2)


def _ssm_kernel(u_ref, sgb_ref, bmat_ref, cre_ref, cim_ref, sc_ref, d_ref, gw_ref, gb_ref, o_ref, xre_ref, xim_ref):
    t = u_ref.shape[1]
    nq = bmat_ref.shape[0]
    half = bmat_ref.shape[2] // 2

    @pl.when(pl.program_id(1) == 0)
    def _():
        xre_ref[0:SUBLANES, :] = jnp.zeros((SUBLANES, xre_ref.shape[1]), F32)
        xim_ref[0:SUBLANES, :] = jnp.zeros((SUBLANES, xim_ref.shape[1]), F32)

    u = u_ref[0]
    ub = u.astype(BF16)
    for q in range(nq):
        bu = _mm(ub[:, q * LANES:(q + 1) * LANES], bmat_ref[q])
        xre_ref[SUBLANES:, q * half:(q + 1) * half] = bu[:, :half]
        xim_ref[SUBLANES:, q * half:(q + 1) * half] = bu[:, half:]

    def scan(r, _):
        base = pl.multiple_of(SUBLANES + r * SUBLANES, SUBLANES)
        xr = xre_ref[pl.ds(base, SUBLANES), :]
        xi = xim_ref[pl.ds(base, SUBLANES), :]
        for s, k in enumerate((1, 2, 4)):
            ar = sc_ref[2 * s]
            ai = sc_ref[2 * s + 1]
            sr = pltpu.roll(xr, k, axis=0)
            si = pltpu.roll(xi, k, axis=0)
            xr, xi = xr + ar * sr - ai * si, xi + ar * si + ai * sr
        cr = xre_ref[pl.ds(base - 1, 1), :]
        ci = xim_ref[pl.ds(base - 1, 1), :]
        pr = sc_ref[6]
        pi = sc_ref[7]
        xre_ref[pl.ds(base, SUBLANES), :] = xr + pr * cr - pi * ci
        xim_ref[pl.ds(base, SUBLANES), :] = xi + pr * ci + pi * cr
        return 0

    lax.fori_loop(0, t // SUBLANES, scan, 0, unroll=2)
    xre_ref[0:SUBLANES, :] = xre_ref[t:t + SUBLANES, :]
    xim_ref[0:SUBLANES, :] = xim_ref[t:t + SUBLANES, :]

    ys = []
    for q in range(nq):
        xr = xre_ref[SUBLANES:, q * half:(q + 1) * half].astype(BF16)
        xi = xim_ref[SUBLANES:, q * half:(q + 1) * half].astype(BF16)
        ys.append(_mm(xr, cre_ref[q]) + _mm(xi, cim_ref[q]))
    y = jnp.concatenate(ys, axis=1) + d_ref[...] * u
    y = jax.nn.gelu(y).astype(BF16)
    hh = _mm(y, gw_ref[...]) + gb_ref[...]
    w = hh.shape[1] // 2
    o_ref[0] = (hh[:, :w] * jax.nn.sigmoid(hh[:, w:]) * sgb_ref[0].astype(F32)).astype(BF16)


def _ssm(u, sgb, bmat, cre, cim, sc, dskip, gw, gb):
    b, l, w = u.shape
    t = ROW_TILE
    ns = sc.shape[-1]

    def row(n):
        return pl.BlockSpec((1, t, n), lambda bb, i: (bb, i, 0))

    return pl.pallas_call(
        _ssm_kernel,
        grid=(b, l // t),
        in_specs=[row(w), row(w), _const_spec(bmat.shape), _const_spec(cre.shape), _const_spec(cim.shape),
                  _const_spec(sc.shape), _const_spec(dskip.shape), _const_spec(gw.shape), _const_spec(gb.shape)],
        out_specs=row(w),
        out_shape=jax.ShapeDtypeStruct((b, l, w), BF16),
        scratch_shapes=[pltpu.VMEM((SUBLANES + t, ns), F32), pltpu.VMEM((SUBLANES + t, ns), F32)],
        compiler_params=_params("arbitrary", "arbitrary"),
        name="ssm",
    )(u, sgb, bmat, cre, cim, sc, dskip, gw, gb)


def _s5_prep(log_dt, a_re, a_im, b_re, b_im, c_re, c_im):
    g, p = a_re.shape
    h = b_re.shape[-1]
    gl = LANES // h
    nq = g // gl
    dt = jnp.exp(log_dt)[:, None]
    mag = jnp.exp(a_re * dt)
    ang = a_im * dt
    ab_re = mag * jnp.cos(ang)
    ab_im = mag * jnp.sin(ang)
    den = a_re * a_re + a_im * a_im
    n_re = ab_re - 1.0
    n_im = ab_im
    f_re = (n_re * a_re + n_im * a_im) / den
    f_im = (n_im * a_re - n_re * a_im) / den
    bb_re = f_re[..., None] * b_re - f_im[..., None] * b_im
    bb_im = f_re[..., None] * b_im + f_im[..., None] * b_re
    eye = jnp.eye(gl, dtype=F32)

    def bdiag_in(m):
        m = m.reshape(nq, gl, p, h)
        return jnp.einsum('qgph,gk->qghkp', m, eye).reshape(nq, gl * h, gl * p)

    def bdiag_out(m):
        m = m.reshape(nq, gl, h, p)
        return jnp.einsum('qghp,gk->qgpkh', m, eye).reshape(nq, gl * p, gl * h)

    bmat = jnp.concatenate([bdiag_in(bb_re), bdiag_in(bb_im)], axis=2).astype(BF16)
    cre = bdiag_out(c_re).astype(BF16)
    cim = bdiag_out(-c_im).astype(BF16)

    pw = [(ab_re.reshape(-1), ab_im.reshape(-1))]
    for _ in range(SUBLANES - 1):
        pr, pi = pw[-1]
        pw.append((pr * pw[0][0] - pi * pw[0][1], pr * pw[0][1] + pi * pw[0][0]))
    rows = jnp.arange(SUBLANES)[:, None]
    sc = []
    for k in (1, 2, 4):
        sc.append(jnp.where(rows >= k, pw[k - 1][0][None, :], 0.0))
        sc.append(jnp.where(rows >= k, pw[k - 1][1][None, :], 0.0))
    sc.append(jnp.stack([pw[r][0] for r in range(SUBLANES)]))
    sc.append(jnp.stack([pw[r][1] for r in range(SUBLANES)]))
    return bmat, cre, cim, jnp.stack(sc).astype(F32)


def _mid_kernel(x_ref, a_ref, s_ref, wo_ref, g_ref, wq_ref, wk_ref, wvt_ref, wg_ref, wqi_ref, wki_ref, wwt_ref,
                qg_ref, kg_ref, h_ref, q_ref, k_ref, vt_ref, sg_ref, qi_ref, ki_ref, wt_ref):
    aw = a_ref.shape[2]
    h = x_ref[0] + _mm(a_ref[0], wo_ref[0:aw, :]) + _mm(s_ref[0], wo_ref[aw:, :])
    h_ref[0] = h
    hn = _rms(h, g_ref[...]).astype(BF16)
    qf = _mm(hn, wq_ref[...])
    for hd in range(C_HEADS):
        sl = slice(hd * C_HEAD_DIM, (hd + 1) * C_HEAD_DIM)
        q_ref[0, :, sl] = (_rms(qf[:, sl], qg_ref[...]) * (C_HEAD_DIM ** -0.5)).astype(BF16)
    kf = _mm(hn, wk_ref[...])
    for hd in range(C_KV_HEADS):
        sl = slice(hd * C_HEAD_DIM, (hd + 1) * C_HEAD_DIM)
        k_ref[0, :, sl] = _rms(kf[:, sl], kg_ref[...]).astype(BF16)
    vt_ref[0] = _mm_nt(wvt_ref[...], hn).astype(BF16)
    sg_ref[0] = _silu(_mm(hn, wg_ref[...])).astype(BF16)
    qi_ref[0] = _mm(hn, wqi_ref[...]).astype(BF16)
    ki_ref[0] = _mm(hn, wki_ref[...]).astype(BF16)
    wt_ref[0] = _mm_nt(wwt_ref[...], hn) * ((IDX_HEADS ** -0.5) * (IDX_DIM ** -0.5))


def _mid(x, att0, ssm0, wo, g, wq, wk, wvt, wg, wqi, wki2, wwt, qg, kg):
    b, l, d = x.shape
    t = ROW_TILE

    def row(n):
        return pl.BlockSpec((1, t, n), lambda bb, i: (bb, i, 0))

    def col(n):
        return pl.BlockSpec((1, n, t), lambda bb, i: (bb, 0, i))

    weights = [wo, g, wq, wk, wvt, wg, wqi, wki2, wwt, qg, kg]
    cw = C_HEADS * C_HEAD_DIM
    ckv = C_KV_HEADS * C_HEAD_DIM
    return pl.pallas_call(
        _mid_kernel,
        grid=(b, l // t),
        in_specs=[row(d), row(att0.shape[2]), row(ssm0.shape[2])] + [_const_spec(w.shape) for w in weights],
        out_specs=[row(d), row(cw), row(ckv), col(ckv), row(cw), row(IDX_HEADS * IDX_DIM), row(2 * LANES),
                   col(IDX_HEADS)],
        out_shape=[jax.ShapeDtypeStruct((b, l, d), F32), jax.ShapeDtypeStruct((b, l, cw), BF16),
                   jax.ShapeDtypeStruct((b, l, ckv), BF16), jax.ShapeDtypeStruct((b, ckv, l), BF16),
                   jax.ShapeDtypeStruct((b, l, cw), BF16), jax.ShapeDtypeStruct((b, l, IDX_HEADS * IDX_DIM), BF16),
                   jax.ShapeDtypeStruct((b, l, 2 * LANES), BF16), jax.ShapeDtypeStruct((b, IDX_HEADS, l), F32)],
        compiler_params=_params("arbitrary", "arbitrary"),
        name="mid",
    )(x, att0, ssm0, *weights)


def _dsa_kernel(q_ref, qi_ref, wt_ref, sg_ref, k_ref, vt_ref, ki_ref, tab_ref, o_ref,
                sc_ref, x_ref, acc_ref, *, seq_len, topk):
    i = pl.program_id(1)
    ck = KEY_CHUNK
    per = ck // BLOCK
    nch = (i + per) // per
    t_row = i * BLOCK + lax.broadcasted_iota(I32, (1, LANES), 1)
    kiota = lax.broadcasted_iota(I32, (ck, LANES), 0)

    def chunk_off(c):
        return pl.multiple_of(c * ck, ck)

    qi = qi_ref[0]
    qi_stack = [jnp.concatenate([qi[:, (2 * s) * LANES:(2 * s + 1) * LANES],
                                 qi[:, (2 * s + 1) * LANES:(2 * s + 2) * LANES]], axis=0) for s in range(2)]
    wt = wt_ref[0]

    def score_chunk(c, masked):
        off = chunk_off(c)
        sc = jnp.zeros((ck, LANES), F32)
        for a in range(2):
            kk = ki_ref[0, pl.ds(off, ck), a * LANES:(a + 1) * LANES]
            for s in range(2):
                r = _mm_nt(kk, qi_stack[s])
                for j in range(2):
                    hd = 2 * (2 * s + j) + a
                    sc = sc + jnp.maximum(r[:, j * LANES:(j + 1) * LANES], 0.0) * wt[hd:hd + 1, :]
        if masked:
            sc = jnp.where(off + kiota <= t_row, sc, NEG_INF)
        sc_ref[pl.ds(off, ck), :] = sc
        return _fold(sc, jnp.maximum)

    def score_body(c, mx):
        return jnp.maximum(mx, score_chunk(c, False))

    smax = lax.fori_loop(0, nch - 1, score_body, jnp.full((SUBLANES, LANES), NEG_INF, F32))
    smax = jnp.max(jnp.maximum(smax, score_chunk(nch - 1, True)), axis=0, keepdims=True)

    def count(pred):
        rows = COUNT_ROWS

        def body(c, accs):
            out = []
            for u, acc in enumerate(accs):
                off = pl.multiple_of(c * ck + u * rows, rows)
                ind = pred(sc_ref[pl.ds(off, rows), :], off).astype(I32)
                out.append(acc + jnp.sum(ind.reshape(rows // SUBLANES, SUBLANES, LANES), axis=0))
            return tuple(out)

        accs = lax.fori_loop(0, nch, body, tuple(jnp.zeros((SUBLANES, LANES), I32) for _ in range(ck // rows)))
        return jnp.sum(sum(accs), axis=0, keepdims=True)

    def key_value(k):
        return pltpu.bitcast(jnp.where(k < 0, INT_MIN - k, k), F32)

    def count_ge(k):
        thr = key_value(k)
        return count(lambda s, off: s >= thr)

    def full(v):
        return jnp.full((1, LANES), v, I32)

    def bisect(_, st):
        lo, hi, c_lo, c_hi = st
        mid = (lo >> 1) + (hi >> 1) + (lo & hi & 1)
        c = count_ge(mid)
        ge = c >= topk
        return jnp.where(ge, mid, lo), jnp.where(ge, hi, mid), jnp.where(ge, c, c_lo), jnp.where(ge, c_hi, c)

    searching = (i + 1) * BLOCK > topk

    def float_key(x):
        bits = pltpu.bitcast(x, I32)
        return jnp.where(bits < 0, INT_MIN - bits, bits)

    def search():
        k_lo = float_key(smax * 0.125)
        c = count_ge(k_lo)
        ok = (smax > 0.0) & (c >= topk)
        trips = jnp.where(jnp.min(jnp.where(ok, 1, 0)) > 0, 25, 32)
        st = (jnp.where(ok, k_lo, KEY_NEG_INF), float_key(smax) + 1, jnp.where(ok, c, nch * ck), full(0))
        out = lax.fori_loop(0, trips, bisect, st)
        return out[0], out[2], out[3]

    vkey, c_lo, c_hi = lax.cond(searching, search, lambda: (full(KEY_NEG_INF), full(topk), full(0)))
    vthr = key_value(vkey)
    need = topk - c_hi
    ties = c_lo - c_hi

    def tie_search():
        nxt = vkey + 1
        nxt = jnp.where((nxt > 0) & (nxt < KEY_MIN_NORMAL), KEY_MIN_NORMAL, nxt)
        step = key_value(nxt) - vthr

        def split(_, st):
            fl, fh = st
            fm = 0.5 * (fl + fh)
            thr = vthr + fm * step
            ge = count(lambda s, off: s >= thr) >= topk
            return jnp.where(ge, fm, fl), jnp.where(ge, fh, fm)

        fl, _ = lax.fori_loop(0, 26, split, (jnp.zeros((1, LANES), F32), jnp.ones((1, LANES), F32)))
        thr = vthr + fl * step
        want = topk - count(lambda s, off: s > thr)

        def body(_, st):
            lj, hj = st
            mid = (lj + hj) >> 1
            c = count(lambda s, off: (s == thr) & (off + kiota[:COUNT_ROWS] <= mid))
            ok = c >= want
            return jnp.where(ok, lj, mid), jnp.where(ok, mid, hj)

        _, hj = lax.fori_loop(0, 14, body, (full(-1), full(0) + (nch * ck - 1)))
        return thr, hj

    any_tie = searching & (jnp.max(ties - need) > 0)
    vthr, jmax = lax.cond(any_tie, tie_search, lambda: (vthr, full(seq_len)))

    def mask_body(c, _):
        off = chunk_off(c)
        s = sc_ref[pl.ds(off, ck), :]
        s_idx = off + kiota
        sel = ((s > vthr) | ((s == vthr) & (s_idx <= jmax))) & (s_idx <= t_row)
        sc_ref[pl.ds(off, ck), :] = jnp.where(sel, 0.0, NEG_INF)
        return 0

    lax.fori_loop(0, nch, mask_body, 0)

    q = q_ref[0]
    n_far = jnp.maximum((i - NEAR_BLOCKS + 1) // per, 0)
    hpg = C_HEADS // C_KV_HEADS

    for g in range(C_KV_HEADS):
        q_stack = [jnp.concatenate([q[:, (hpg * g + 2 * jj) * LANES:(hpg * g + 2 * jj + 1) * LANES],
                                    q[:, (hpg * g + 2 * jj + 1) * LANES:(hpg * g + 2 * jj + 2) * LANES]], axis=0)
                   for jj in range(hpg // 2)]

        def stage_body(near, g=g, q_stack=q_stack):
            def body(c, mx):
                off = chunk_off(c)
                madd = sc_ref[pl.ds(off, ck), :]
                kc = k_ref[0, pl.ds(off, ck), g * LANES:(g + 1) * LANES]
                tidx = [jnp.clip(i - (c * per + r), 0, NEAR_BLOCKS) for r in range(per)]
                out = []
                for jj in range(hpg // 2):
                    lg = _mm_nt(kc, q_stack[jj])
                    for a in range(2):
                        hl = 2 * jj + a
                        x = lg[:, a * LANES:(a + 1) * LANES] + madd
                        if near:
                            x = x + jnp.concatenate([tab_ref[hpg * g + hl, tidx[r]] for r in range(per)], axis=0)
                        x_ref[hl, pl.ds(off, ck), :] = x
                        out.append(jnp.maximum(mx[hl], _fold(x, jnp.maximum)))
                return tuple(out)

            return body

        mx = tuple(jnp.full((SUBLANES, LANES), NEG_INF, F32) for _ in range(hpg))
        mx = lax.fori_loop(0, n_far, stage_body(False), mx)
        mx = lax.fori_loop(n_far, nch, stage_body(True), mx)
        m = [jnp.max(v, axis=0, keepdims=True) for v in mx]
        acc_ref[...] = jnp.zeros(acc_ref.shape, F32)

        def att_body(c, ls, g=g, m=m):
            off = chunk_off(c)
            vt = vt_ref[0, g * LANES:(g + 1) * LANES, pl.ds(off, ck)]
            out = []
            for jj in range(hpg // 2):
                ps = []
                for a in range(2):
                    hl = 2 * jj + a
                    p = jnp.exp(x_ref[hl, pl.ds(off, ck), :] - m[hl])
                    out.append(ls[hl] + _fold(p, jnp.add))
                    ps.append(p.astype(BF16))
                acc_ref[jj] += _mm(vt, jnp.concatenate(ps, axis=1))
            return tuple(out)

        ls = lax.fori_loop(0, nch, att_body, tuple(jnp.zeros((SUBLANES, LANES), F32) for _ in range(hpg)))
        for hl in range(hpg):
            sl = slice((hpg * g + hl) * LANES, (hpg * g + hl + 1) * LANES)
            ot = acc_ref[hl // 2, :, (hl % 2) * LANES:(hl % 2 + 1) * LANES] / jnp.sum(ls[hl], axis=0, keepdims=True)
            o_ref[0, :, sl] = (ot.T * sg_ref[0, :, sl].astype(F32)).astype(BF16)


def _dsa(q, qi, wt, sg, k, vt, ki2, tab):
    b, l, cw = q.shape
    nb = l // BLOCK
    topk = min(TOPK_MAX, l // 4)

    def blk(n):
        return pl.BlockSpec((1, BLOCK, n), lambda bb, i: (bb, i, 0))

    def whole(s1, s2):
        return pl.BlockSpec((1, s1, s2), lambda bb, i: (bb, 0, 0), pipeline_mode=pl.Buffered(1))

    hpg = C_HEADS // C_KV_HEADS
    return pl.pallas_call(
        functools.partial(_dsa_kernel, seq_len=l, topk=topk),
        grid=(b, nb),
        in_specs=[blk(cw), blk(qi.shape[2]), pl.BlockSpec((1, IDX_HEADS, BLOCK), lambda bb, i: (bb, 0, i)), blk(cw),
                  whole(l, k.shape[2]), whole(vt.shape[1], l), whole(l, ki2.shape[2]),
                  pl.BlockSpec(tab.shape, lambda bb, i: (0, 0, 0, 0), pipeline_mode=pl.Buffered(1))],
        out_specs=blk(cw),
        out_shape=jax.ShapeDtypeStruct((b, l, cw), BF16),
        scratch_shapes=[pltpu.VMEM((l, LANES), F32), pltpu.VMEM((hpg, l, LANES), F32),
                        pltpu.VMEM((hpg // 2, C_HEAD_DIM, 2 * LANES), F32)],
        compiler_params=_params("arbitrary", "arbitrary"),
        name="dsa",
    )(q, qi, wt, sg, k, vt, ki2, tab)


def _out_kernel(h_ref, a_ref, w_ref, o_ref):
    o_ref[...] = h_ref[...] + _mm(a_ref[...], w_ref[...])


def _outproj(h2, a2, w):
    rows, d = h2.shape
    t = ROW_TILE
    return pl.pallas_call(
        _out_kernel,
        grid=(rows // t,),
        in_specs=[pl.BlockSpec((t, d), lambda i: (i, 0)), pl.BlockSpec((t, a2.shape[1]), lambda i: (i, 0)),
                  _const_spec(w.shape)],
        out_specs=pl.BlockSpec((t, d), lambda i: (i, 0)),
        out_shape=jax.ShapeDtypeStruct((rows, d), F32),
        compiler_params=_params("arbitrary"),
        name="outproj1",
    )(h2, a2, w)


def _bias_tables(rel_bias, seq_len):
    del seq_len
    nv = (NEAR_BLOCKS + 1) * BLOCK
    vec = rel_bias[_t5_bucket(jnp.arange(nv, dtype=I32))].astype(F32).T

    def window(lo, n):
        pad = max(0, -lo)
        body = vec[:, max(lo, 0):lo + n]
        return jnp.concatenate([jnp.broadcast_to(vec[:, :1], (vec.shape[0], pad)), body], axis=1)

    def toeplitz(g, rows, cols):
        w = rows + cols
        g2 = jnp.concatenate([g[:, rows - 1:rows - 1 + cols], g[:, :1], g[:, :rows - 1]], axis=1)
        flat = jnp.tile(g2, (1, rows))[:, :rows * (w - 1)]
        return flat.reshape(-1, rows, w - 1)[:, :, :cols]

    bias0 = jnp.transpose(toeplitz(window(BLOCK - (2 * BLOCK - 1), 3 * BLOCK - 1), 2 * BLOCK, BLOCK), (0, 2, 1))
    tiles = [toeplitz(window(dl * BLOCK - (BLOCK - 1), 2 * BLOCK - 1), BLOCK, BLOCK) for dl in range(NEAR_BLOCKS)]
    tab = jnp.stack(tiles, axis=1) - rel_bias[NUM_BUCKETS - 1].astype(F32)[:, None, None, None]
    tab = jnp.concatenate([tab, jnp.zeros((tab.shape[0], 1, BLOCK, BLOCK), F32)], axis=1)
    return bias0, tab


def kernel(x, rel_bias, norm_g, ev_w_in, ev_w_out, ev_q_norm_g, ev_k_norm_g, ev_sinks, ev_ssm_log_dt, ev_ssm_a_re,
           ev_ssm_a_im, ev_ssm_b_re, ev_ssm_b_im, ev_ssm_c_re, ev_ssm_c_im, ev_ssm_d, ev_glu_w, ev_glu_b, od_w_in,
           od_w_out, od_q_norm_g, od_k_norm_g):
    b, l, d = x.shape
    assert l % KEY_CHUNK == 0 and l % ROW_TILE == 0
    assert (NEAR_BLOCKS - 1) * BLOCK + 1 >= 16 * 64 ** (15 / 16) + 1
    bias0, tab = _bias_tables(rel_bias, l)

    q0, k0, v0, sga, u, sgb = _proj0(x.reshape(b * l, d), norm_g[0][None, :], ev_w_in[0].astype(BF16))
    shp = lambda a: a.reshape(b, l, a.shape[-1])
    qg2 = jnp.tile(ev_q_norm_g[0], 2)[None, :]
    kg2 = jnp.tile(ev_k_norm_g[0], 2)[None, :]
    sinks = jnp.broadcast_to(ev_sinks[0][:, None], (A_HEADS, LANES)).astype(F32)
    att0 = _attn0(shp(q0), shp(k0), shp(v0), shp(sga), bias0, sinks, qg2, kg2)
    bmat, cre, cim, sc = _s5_prep(ev_ssm_log_dt[0], ev_ssm_a_re[0], ev_ssm_a_im[0], ev_ssm_b_re[0], ev_ssm_b_im[0],
                                  ev_ssm_c_re[0], ev_ssm_c_im[0])
    ssm0 = _ssm(shp(u), shp(sgb), bmat, cre, cim, sc, ev_ssm_d[0].reshape(1, -1), ev_glu_w[0].astype(BF16),
                ev_glu_b[0][None, :])

    w1 = od_w_in[0]
    cw = C_HEADS * C_HEAD_DIM
    ckv = C_KV_HEADS * C_HEAD_DIM
    o = np.cumsum([0, cw, ckv, ckv, cw, IDX_HEADS * IDX_DIM, IDX_DIM, IDX_HEADS])
    wq, wk, wv, wg, wqi, wki, ww = (w1[:, o[n]:o[n + 1]] for n in range(7))
    zki = jnp.zeros((d, LANES - IDX_DIM), w1.dtype)
    wki2 = jnp.concatenate([wki, zki, zki, wki], axis=1)
    bf = lambda a: a.astype(BF16)
    h1, q1, k1, vt1, sg1, qi1, ki2, wt1 = _mid(
        x, att0, ssm0, bf(ev_w_out[0]), norm_g[1][None, :], bf(wq), bf(wk), bf(wv.T), bf(wg), bf(wqi), bf(wki2),
        bf(ww.T), od_q_norm_g[0][None, :], od_k_norm_g[0][None, :])
    att1 = _dsa(q1, qi1, wt1, sg1, k1, vt1, ki2, tab)
    out = _outproj(h1.reshape(b * l, d), att1.reshape(b * l, cw), bf(od_w_out[0]))
    return out.reshape(b, l, d)
```

```python
import functools
import math

import jax
import jax.numpy as jnp
import numpy as np
from jax import lax
from jax.experimental import pallas as pl
from jax.experimental.pallas import tpu as pltpu

F32 = jnp.float32
BF16 = jnp.bfloat16
I32 = jnp.int32

LANES = 128
SUBLANES = 8
VMEM_LIMIT = 56 * 1024 * 1024

BLOCK = 128
WINDOW = 128
A_HEADS = 8
A_HEAD_DIM = 64
A_KV_HEADS = 2
A_WIDTH = A_HEADS * A_HEAD_DIM
SSM_GROUP = 16
SSM_STATE = 64
C_HEADS = 8
C_HEAD_DIM = 128
C_KV_HEADS = 2
IDX_HEADS = 8
IDX_DIM = 64
TOPK_MAX = 256
NUM_BUCKETS = 32
REL_MAX_DIST = 1024
EPS = 1e-6
NEG_INF = -1e30
INT_MIN = -(2 ** 31)
KEY_MIN_NORMAL = 0x00800000
KEY_POS_INF = 0x7F800000
KEY_NEG_INF = INT_MIN + 0x00800000

ROW_TILE = 256
KEY_CHUNK = 1024
NEAR_BLOCKS = 8
FOLD_CHAINS = 8
COUNT_ROWS = 512
CAND = 32
NT_DIMS = (((1,), (1,)), ((), ()))


def _t5_bucket(dist):
    n = jnp.maximum(dist, 0)
    max_exact = NUM_BUCKETS // 2
    nf = jnp.maximum(n, 1).astype(F32)
    large = max_exact + (jnp.log(nf / max_exact) / math.log(REL_MAX_DIST / max_exact)
                         * (NUM_BUCKETS - max_exact)).astype(I32)
    large = jnp.minimum(large, NUM_BUCKETS - 1)
    return jnp.where(n < max_exact, n, large)


def _silu(x):
    return x * jax.nn.sigmoid(x)


def _rms(x, g):
    ms = jnp.mean(x * x, axis=-1, keepdims=True)
    return x * lax.rsqrt(ms + EPS) * g


def _mm(a, b):
    return jnp.dot(a, b, preferred_element_type=F32)


def _mm_nt(a, b):
    return lax.dot_general(a, b, NT_DIMS, preferred_element_type=F32)


def _fold(x, op):
    n = x.shape[0] // SUBLANES
    chains = min(FOLD_CHAINS, n)
    accs = [x[r * SUBLANES:(r + 1) * SUBLANES] for r in range(chains)]
    for r in range(chains, n):
        accs[r % chains] = op(accs[r % chains], x[r * SUBLANES:(r + 1) * SUBLANES])
    while len(accs) > 1:
        accs = [op(a, b) for a, b in zip(accs[::2], accs[1::2])] + accs[len(accs) & ~1:]
    return accs[0]


def _params(*sem):
    return pltpu.CompilerParams(dimension_semantics=sem, vmem_limit_bytes=VMEM_LIMIT)


def _const_spec(shape):
    zeros = (0,) * len(shape)
    return pl.BlockSpec(shape, lambda *_: zeros)


def _proj0_kernel(x_ref, g_ref, w_ref, q_ref, k_ref, v_ref, sga_ref, u_ref, sgb_ref):
    hn = _rms(x_ref[...], g_ref[...]).astype(BF16)

    def mm(lo, hi):
        return _mm(hn, w_ref[:, lo:hi])

    q_ref[...] = mm(0, 512)
    k_ref[...] = mm(512, 640)
    v_ref[...] = mm(640, 768)
    sga_ref[...] = _silu(mm(768, 1280)).astype(BF16)
    u_ref[...] = mm(1280, 1792)
    sgb_ref[...] = _silu(mm(1792, 2304)).astype(BF16)


def _proj0(x2, g, w):
    rows, d = x2.shape
    t = ROW_TILE

    def row(n):
        return pl.BlockSpec((t, n), lambda i: (i, 0))

    return pl.pallas_call(
        _proj0_kernel,
        grid=(rows // t,),
        in_specs=[row(d), _const_spec((1, d)), _const_spec(w.shape)],
        out_specs=[row(512), row(128), row(128), row(512), row(512), row(512)],
        out_shape=[jax.ShapeDtypeStruct((rows, 512), F32), jax.ShapeDtypeStruct((rows, 128), F32),
                   jax.ShapeDtypeStruct((rows, 128), F32), jax.ShapeDtypeStruct((rows, 512), BF16),
                   jax.ShapeDtypeStruct((rows, 512), F32), jax.ShapeDtypeStruct((rows, 512), BF16)],
        compiler_params=_params("arbitrary"),
        name="proj0",
    )(x2, g, w)


def _attn0_kernel(q_ref, kc_ref, kp_ref, vc_ref, vp_ref, sga_ref, bias_ref, sink_ref, qg_ref, kg_ref, o_ref):
    i = pl.program_id(1)
    lane = lax.broadcasted_iota(I32, (1, LANES), 1)
    lo = lane < A_HEAD_DIM

    def segnorm(x, g2):
        sq = x * x
        s_lo = jnp.sum(jnp.where(lo, sq, 0.0), axis=-1, keepdims=True)
        s_hi = jnp.sum(jnp.where(lo, 0.0, sq), axis=-1, keepdims=True)
        inv = jnp.where(lo, lax.rsqrt(s_lo / A_HEAD_DIM + EPS), lax.rsqrt(s_hi / A_HEAD_DIM + EPS))
        return x * inv * g2

    kn = segnorm(jnp.concatenate([kp_ref[0], kc_ref[0]], axis=0), kg_ref[...])
    vb = jnp.concatenate([vp_ref[0], vc_ref[0]], axis=0)
    kr = pltpu.roll(kn, A_HEAD_DIM, axis=1)
    vr = pltpu.roll(vb, A_HEAD_DIM, axis=1)

    def variants(x, xr):
        return {(0, 0): jnp.where(lo, x, 0.0).astype(BF16), (0, 1): jnp.where(lo, 0.0, xr).astype(BF16),
                (1, 0): jnp.where(lo, xr, 0.0).astype(BF16), (1, 1): jnp.where(lo, 0.0, x).astype(BF16)}

    kvar = variants(kn, kr)
    vvar = variants(vb, vr)

    row = lax.broadcasted_iota(I32, (BLOCK, 2 * BLOCK), 0)
    col = lax.broadcasted_iota(I32, (BLOCK, 2 * BLOCK), 1)
    d = row + BLOCK - col
    mask = (d >= 0) & (d < WINDOW) & ((i > 0) | (col >= BLOCK))

    lgs, sinks = [], []
    for p in range(A_HEADS // 2):
        qp = (segnorm(q_ref[0, :, p * LANES:(p + 1) * LANES], qg_ref[...]) * (A_HEAD_DIM ** -0.5)).astype(BF16)
        for a in range(2):
            h = 2 * p + a
            lgs.append(jnp.where(mask, _mm_nt(qp, kvar[(p // 2, a)]) + bias_ref[h], NEG_INF))
            sinks.append(jnp.broadcast_to(sink_ref[h:h + 1, 0:1], (BLOCK, 1)))
    lg = jnp.concatenate(lgs, axis=0)
    sink = jnp.concatenate(sinks, axis=0)
    m = jnp.maximum(jnp.max(lg, axis=-1, keepdims=True), sink)
    e = jnp.exp(lg - m)
    den = jnp.sum(e, axis=-1, keepdims=True) + jnp.exp(sink - m)
    pr = (e / den).astype(BF16)
    for p in range(A_HEADS // 2):
        sl = slice(p * LANES, (p + 1) * LANES)
        acc = jnp.zeros((BLOCK, LANES), F32)
        for a in range(2):
            h = 2 * p + a
            acc = acc + _mm(pr[h * BLOCK:(h + 1) * BLOCK], vvar[(p // 2, a)])
        o_ref[0, :, sl] = (acc * sga_ref[0, :, sl].astype(F32)).astype(BF16)


def _attn0(q, k, v, sga, bias0, sinks, qg2, kg2):
    b, l, _ = q.shape
    nb = l // BLOCK

    def cur(n):
        return pl.BlockSpec((1, BLOCK, n), lambda bb, i: (bb, i, 0))

    def prev(n):
        return pl.BlockSpec((1, BLOCK, n), lambda bb, i: (bb, jnp.maximum(i - 1, 0), 0))

    return pl.pallas_call(
        _attn0_kernel,
        grid=(b, nb),
        in_specs=[cur(512), cur(128), prev(128), cur(128), prev(128), cur(512),
                  _const_spec(bias0.shape), _const_spec(sinks.shape), _const_spec(qg2.shape), _const_spec(kg2.shape)],
        out_specs=cur(512),
        out_shape=jax.ShapeDtypeStruct((b, l, 512), BF16),
        compiler_params=_params("arbitrary", "arbitrary"),
        name="attn0",
    )(q, k, k, v, v, sga, bias0, sinks, qg2, kg2)


def _ssm_kernel(u_ref, sgb_ref, bmat_ref, cre_ref, cim_ref, sc_ref, d_ref, gw_ref, gb_ref, o_ref, xre_ref, xim_ref):
    t = u_ref.shape[1]
    nq = bmat_ref.shape[0]
    half = bmat_ref.shape[2] // 2

    @pl.when(pl.program_id(1) == 0)
    def _():
        xre_ref[0:SUBLANES, :] = jnp.zeros((SUBLANES, xre_ref.shape[1]), F32)
        xim_ref[0:SUBLANES, :] = jnp.zeros((SUBLANES, xim_ref.shape[1]), F32)

    u = u_ref[0]
    ub = u.astype(BF16)
    for q in range(nq):
        bu = _mm(ub[:, q * LANES:(q + 1) * LANES], bmat_ref[q])
        xre_ref[SUBLANES:, q * half:(q + 1) * half] = bu[:, :half]
        xim_ref[SUBLANES:, q * half:(q + 1) * half] = bu[:, half:]

    def scan(r, _):
        base = pl.multiple_of(SUBLANES + r * SUBLANES, SUBLANES)
        xr = xre_ref[pl.ds(base, SUBLANES), :]
        xi = xim_ref[pl.ds(base, SUBLANES), :]
        for s, k in enumerate((1, 2, 4)):
            ar = sc_ref[2 * s]
            ai = sc_ref[2 * s + 1]
            sr = pltpu.roll(xr, k, axis=0)
            si = pltpu.roll(xi, k, axis=0)
            xr, xi = xr + ar * sr - ai * si, xi + ar * si + ai * sr
        cr = xre_ref[pl.ds(base - 1, 1), :]
        ci = xim_ref[pl.ds(base - 1, 1), :]
        pr = sc_ref[6]
        pi = sc_ref[7]
        xre_ref[pl.ds(base, SUBLANES), :] = xr + pr * cr - pi * ci
        xim_ref[pl.ds(base, SUBLANES), :] = xi + pr * ci + pi * cr
        return 0

    lax.fori_loop(0, t // SUBLANES, scan, 0, unroll=2)
    xre_ref[0:SUBLANES, :] = xre_ref[t:t + SUBLANES, :]
    xim_ref[0:SUBLANES, :] = xim_ref[t:t + SUBLANES, :]

    ys = []
    for q in range(nq):
        xr = xre_ref[SUBLANES:, q * half:(q + 1) * half].astype(BF16)
        xi = xim_ref[SUBLANES:, q * half:(q + 1) * half].astype(BF16)
        ys.append(_mm(xr, cre_ref[q]) + _mm(xi, cim_ref[q]))
    y = jnp.concatenate(ys, axis=1) + d_ref[...] * u
    y = jax.nn.gelu(y).astype(BF16)
    hh = _mm(y, gw_ref[...]) + gb_ref[...]
    w = hh.shape[1] // 2
    o_ref[0] = (hh[:, :w] * jax.nn.sigmoid(hh[:, w:]) * sgb_ref[0].astype(F32)).astype(BF16)


def _ssm(u, sgb, bmat, cre, cim, sc, dskip, gw, gb):
    b, l, w = u.shape
    t = ROW_TILE
    ns = sc.shape[-1]

    def row(n):
        return pl.BlockSpec((1, t, n), lambda bb, i: (bb, i, 0))

    return pl.pallas_call(
        _ssm_kernel,
        grid=(b, l // t),
        in_specs=[row(w), row(w), _const_spec(bmat.shape), _const_spec(cre.shape), _const_spec(cim.shape),
                  _const_spec(sc.shape), _const_spec(dskip.shape), _const_spec(gw.shape), _const_spec(gb.shape)],
        out_specs=row(w),
        out_shape=jax.ShapeDtypeStruct((b, l, w), BF16),
        scratch_shapes=[pltpu.VMEM((SUBLANES + t, ns), F32), pltpu.VMEM((SUBLANES + t, ns), F32)],
        compiler_params=_params("arbitrary", "arbitrary"),
        name="ssm",
    )(u, sgb, bmat, cre, cim, sc, dskip, gw, gb)


def _s5_prep(log_dt, a_re, a_im, b_re, b_im, c_re, c_im):
    g, p = a_re.shape
    h = b_re.shape[-1]
    gl = LANES // h
    nq = g // gl
    dt = jnp.exp(log_dt)[:, None]
    mag = jnp.exp(a_re * dt)
    ang = a_im * dt
    ab_re = mag * jnp.cos(ang)
    ab_im = mag * jnp.sin(ang)
    den = a_re * a_re + a_im * a_im
    n_re = ab_re - 1.0
    n_im = ab_im
    f_re = (n_re * a_re + n_im * a_im) / den
    f_im = (n_im * a_re - n_re * a_im) / den
    bb_re = f_re[..., None] * b_re - f_im[..., None] * b_im
    bb_im = f_re[..., None] * b_im + f_im[..., None] * b_re
    eye = jnp.eye(gl, dtype=F32)

    def bdiag_in(m):
        m = m.reshape(nq, gl, p, h)
        return jnp.einsum('qgph,gk->qghkp', m, eye).reshape(nq, gl * h, gl * p)

    def bdiag_out(m):
        m = m.reshape(nq, gl, h, p)
        return jnp.einsum('qghp,gk->qgpkh', m, eye).reshape(nq, gl * p, gl * h)

    bmat = jnp.concatenate([bdiag_in(bb_re), bdiag_in(bb_im)], axis=2).astype(BF16)
    cre = bdiag_out(c_re).astype(BF16)
    cim = bdiag_out(-c_im).astype(BF16)

    pw = [(ab_re.reshape(-1), ab_im.reshape(-1))]
    for _ in range(SUBLANES - 1):
        pr, pi = pw[-1]
        pw.append((pr * pw[0][0] - pi * pw[0][1], pr * pw[0][1] + pi * pw[0][0]))
    rows = jnp.arange(SUBLANES)[:, None]
    sc = []
    for k in (1, 2, 4):
        sc.append(jnp.where(rows >= k, pw[k - 1][0][None, :], 0.0))
        sc.append(jnp.where(rows >= k, pw[k - 1][1][None, :], 0.0))
    sc.append(jnp.stack([pw[r][0] for r in range(SUBLANES)]))
    sc.append(jnp.stack([pw[r][1] for r in range(SUBLANES)]))
    return bmat, cre, cim, jnp.stack(sc).astype(F32)


def _mid_kernel(x_ref, a_ref, s_ref, wo_ref, g_ref, wq_ref, wk_ref, wvt_ref, wg_ref, wqi_ref, wki_ref, wwt_ref,
                qg_ref, kg_ref, h_ref, q_ref, k_ref, vt_ref, sg_ref, qi_ref, ki_ref, wt_ref):
    aw = a_ref.shape[2]
    h = x_ref[0] + _mm(a_ref[0], wo_ref[0:aw, :]) + _mm(s_ref[0], wo_ref[aw:, :])
    h_ref[0] = h
    hn = _rms(h, g_ref[...]).astype(BF16)
    qf = _mm(hn, wq_ref[...])
    for hd in range(C_HEADS):
        sl = slice(hd * C_HEAD_DIM, (hd + 1) * C_HEAD_DIM)
        q_ref[0, :, sl] = (_rms(qf[:, sl], qg_ref[...]) * (C_HEAD_DIM ** -0.5)).astype(BF16)
    kf = _mm(hn, wk_ref[...])
    for hd in range(C_KV_HEADS):
        sl = slice(hd * C_HEAD_DIM, (hd + 1) * C_HEAD_DIM)
        k_ref[0, :, sl] = _rms(kf[:, sl], kg_ref[...]).astype(BF16)
    vt_ref[0] = _mm_nt(wvt_ref[...], hn).astype(BF16)
    sg_ref[0] = _silu(_mm(hn, wg_ref[...])).astype(BF16)
    qi_ref[0] = _mm(hn, wqi_ref[...]).astype(BF16)
    ki_ref[0] = _mm(hn, wki_ref[...]).astype(BF16)
    wt_ref[0] = _mm_nt(wwt_ref[...], hn) * ((IDX_HEADS ** -0.5) * (IDX_DIM ** -0.5))


def _mid(x, att0, ssm0, wo, g, wq, wk, wvt, wg, wqi, wki2, wwt, qg, kg):
    b, l, d = x.shape
    t = ROW_TILE

    def row(n):
        return pl.BlockSpec((1, t, n), lambda bb, i: (bb, i, 0))

    def col(n):
        return pl.BlockSpec((1, n, t), lambda bb, i: (bb, 0, i))

    weights = [wo, g, wq, wk, wvt, wg, wqi, wki2, wwt, qg, kg]
    cw = C_HEADS * C_HEAD_DIM
    ckv = C_KV_HEADS * C_HEAD_DIM
    return pl.pallas_call(
        _mid_kernel,
        grid=(b, l // t),
        in_specs=[row(d), row(att0.shape[2]), row(ssm0.shape[2])] + [_const_spec(w.shape) for w in weights],
        out_specs=[row(d), row(cw), row(ckv), col(ckv), row(cw), row(IDX_HEADS * IDX_DIM), row(2 * LANES),
                   col(IDX_HEADS)],
        out_shape=[jax.ShapeDtypeStruct((b, l, d), F32), jax.ShapeDtypeStruct((b, l, cw), BF16),
                   jax.ShapeDtypeStruct((b, l, ckv), BF16), jax.ShapeDtypeStruct((b, ckv, l), BF16),
                   jax.ShapeDtypeStruct((b, l, cw), BF16), jax.ShapeDtypeStruct((b, l, IDX_HEADS * IDX_DIM), BF16),
                   jax.ShapeDtypeStruct((b, l, 2 * LANES), BF16), jax.ShapeDtypeStruct((b, IDX_HEADS, l), F32)],
        compiler_params=_params("arbitrary", "arbitrary"),
        name="mid",
    )(x, att0, ssm0, *weights)


def _dsa_kernel(q_ref, qi_ref, wt_ref, sg_ref, k_ref, vt_ref, ki_ref, tab_ref, o_ref,
                sc_ref, best_ref, x_ref, acc_ref, *, seq_len, topk):
    i = pl.program_id(1)
    ck = KEY_CHUNK
    per = ck // BLOCK
    nch = (i + per) // per
    t_row = i * BLOCK + lax.broadcasted_iota(I32, (1, LANES), 1)
    kiota = lax.broadcasted_iota(I32, (ck, LANES), 0)

    def chunk_off(c):
        return pl.multiple_of(c * ck, ck)

    qi = qi_ref[0]
    qi_stack = [jnp.concatenate([qi[:, (2 * s) * LANES:(2 * s + 1) * LANES],
                                 qi[:, (2 * s + 1) * LANES:(2 * s + 2) * LANES]], axis=0) for s in range(2)]
    wt = wt_ref[0]

    def score_chunk(c, masked):
        off = chunk_off(c)
        sc = jnp.zeros((ck, LANES), F32)
        for a in range(2):
            kk = ki_ref[0, pl.ds(off, ck), a * LANES:(a + 1) * LANES]
            for s in range(2):
                r = _mm_nt(kk, qi_stack[s])
                for j in range(2):
                    hd = 2 * (2 * s + j) + a
                    sc = sc + jnp.maximum(r[:, j * LANES:(j + 1) * LANES], 0.0) * wt[hd:hd + 1, :]
        if masked:
            sc = jnp.where(off + kiota <= t_row, sc, NEG_INF)
        sc_ref[pl.ds(off, ck), :] = sc
        return _fold(sc, jnp.maximum)

    def score_body(c, mx):
        return jnp.maximum(mx, score_chunk(c, False))

    smax = lax.fori_loop(0, nch - 1, score_body, jnp.full((SUBLANES, LANES), NEG_INF, F32))
    smax = jnp.max(jnp.maximum(smax, score_chunk(nch - 1, True)), axis=0, keepdims=True)

    def count(pred):
        rows = COUNT_ROWS

        def body(c, accs):
            out = []
            for u, acc in enumerate(accs):
                off = pl.multiple_of(c * ck + u * rows, rows)
                ind = pred(sc_ref[pl.ds(off, rows), :], off).astype(I32)
                out.append(acc + jnp.sum(ind.reshape(rows // SUBLANES, SUBLANES, LANES), axis=0))
            return tuple(out)

        accs = lax.fori_loop(0, nch, body, tuple(jnp.zeros((SUBLANES, LANES), I32) for _ in range(ck // rows)))
        return jnp.sum(sum(accs), axis=0, keepdims=True)

    def key_value(k):
        return pltpu.bitcast(jnp.where(k < 0, INT_MIN - k, k), F32)

    def count_ge(k):
        thr = key_value(k)
        return count(lambda s, off: s >= thr)

    def full(v):
        return jnp.full((1, LANES), v, I32)

    def bisect(_, st):
        lo, hi, c_lo, c_hi = st
        mid = (lo >> 1) + (hi >> 1) + (lo & hi & 1)
        c = count_ge(mid)
        ge = c >= topk
        return jnp.where(ge, mid, lo), jnp.where(ge, hi, mid), jnp.where(ge, c, c_lo), jnp.where(ge, c_hi, c)

    searching = (i + 1) * BLOCK > topk

    def float_key(x):
        bits = pltpu.bitcast(x, I32)
        return jnp.where(bits < 0, INT_MIN - bits, bits)

    def search():
        k_lo = float_key(smax * 0.125)
        c = count_ge(k_lo)
        ok = (smax > 0.0) & (c >= topk)
        trips = jnp.where(jnp.min(jnp.where(ok, 1, 0)) > 0, 25, 32)
        st = (jnp.where(ok, k_lo, KEY_NEG_INF), float_key(smax) + 1, jnp.where(ok, c, nch * ck), full(0))
        out = lax.fori_loop(0, trips, bisect, st)
        return out[0], out[2], out[3]

    def exchange(v, a, b):
        v[a], v[b] = jnp.maximum(v[a], v[b]), jnp.minimum(v[a], v[b])

    def sort_desc(v):
        n, k = len(v), 2
        while k <= n:
            j = k // 2
            while j >= 1:
                for a in range(n):
                    b = a ^ j
                    if b > a:
                        exchange(v, *((a, b) if (a & k) == 0 else (b, a)))
                j //= 2
            k *= 2

    def merge_top(best, blk):
        n = len(best)
        v = [jnp.maximum(best[r], blk[n - 1 - r]) for r in range(n)]
        j = n // 2
        while j >= 1:
            for a in range(n):
                if a ^ j > a:
                    exchange(v, a, a ^ j)
            j //= 2
        return v

    def cand_body(c, _):
        rows = 2 * CAND * SUBLANES
        blk_all = sc_ref[pl.ds(pl.multiple_of(c * rows, rows), rows), :]
        for st in range(2):
            blk = [blk_all[(2 * r + st) * SUBLANES:(2 * r + st + 1) * SUBLANES] for r in range(CAND)]
            sort_desc(blk)
            base = st * CAND * SUBLANES
            best = [best_ref[base + r * SUBLANES:base + (r + 1) * SUBLANES, :] for r in range(CAND)]
            for r, x in enumerate(merge_top(best, blk)):
                best_ref[base + r * SUBLANES:base + (r + 1) * SUBLANES, :] = x
        return 0

    def cand_search():
        best_ref[...] = jnp.full(best_ref.shape, -jnp.inf, F32)
        lax.fori_loop(0, nch * (ck // (2 * CAND * SUBLANES)), cand_body, 0)
        cand = best_ref[...]

        def step(_, st):
            lo, hi, c_lo, c_hi = st
            mid = (lo >> 1) + (hi >> 1) + (lo & hi & 1)
            c = jnp.sum((cand >= key_value(mid)).astype(I32), axis=0, keepdims=True)
            ge = c >= topk
            return jnp.where(ge, mid, lo), jnp.where(ge, hi, mid), jnp.where(ge, c, c_lo), jnp.where(ge, c_hi, c)

        vk, _, _, _ = lax.fori_loop(0, 32, step, (full(KEY_NEG_INF), float_key(smax) + 1, full(cand.shape[0]), full(0)))
        thr = key_value(vk)
        above_cand = jnp.sum((cand > thr).astype(I32), axis=0, keepdims=True)
        c_ge, c_gt = count(lambda s, off: s >= thr), count(lambda s, off: s > thr)
        complete = jnp.min(jnp.where(c_gt == above_cand, 1, 0)) > 0
        return lax.cond(complete, lambda: (vk, c_ge, c_gt), search)

    vkey, c_lo, c_hi = lax.cond(searching, cand_search, lambda: (full(KEY_NEG_INF), full(topk), full(0)))
    vthr = key_value(vkey)
    need = topk - c_hi
    ties = c_lo - c_hi

    def tie_search():
        nxt = vkey + 1
        nxt = jnp.where((nxt > 0) & (nxt < KEY_MIN_NORMAL), KEY_MIN_NORMAL, nxt)
        step = key_value(nxt) - vthr

        def split(_, st):
            fl, fh = st
            fm = 0.5 * (fl + fh)
            thr = vthr + fm * step
            ge = count(lambda s, off: s >= thr) >= topk
            return jnp.where(ge, fm, fl), jnp.where(ge, fh, fm)

        fl, _ = lax.fori_loop(0, 26, split, (jnp.zeros((1, LANES), F32), jnp.ones((1, LANES), F32)))
        thr = vthr + fl * step
        want = topk - count(lambda s, off: s > thr)

        def body(_, st):
            lj, hj = st
            mid = (lj + hj) >> 1
            c = count(lambda s, off: (s == thr) & (off + kiota[:COUNT_ROWS] <= mid))
            ok = c >= want
            return jnp.where(ok, lj, mid), jnp.where(ok, mid, hj)

        _, hj = lax.fori_loop(0, 14, body, (full(-1), full(0) + (nch * ck - 1)))
        return thr, hj

    any_tie = searching & (jnp.max(ties - need) > 0)
    vthr, jmax = lax.cond(any_tie, tie_search, lambda: (vthr, full(seq_len)))

    def mask_body(c, _):
        off = chunk_off(c)
        s = sc_ref[pl.ds(off, ck), :]
        s_idx = off + kiota
        sel = ((s > vthr) | ((s == vthr) & (s_idx <= jmax))) & (s_idx <= t_row)
        sc_ref[pl.ds(off, ck), :] = jnp.where(sel, 0.0, NEG_INF)
        return 0

    lax.fori_loop(0, nch, mask_body, 0)

    q = q_ref[0]
    n_far = jnp.maximum((i - NEAR_BLOCKS + 1) // per, 0)
    hpg = C_HEADS // C_KV_HEADS

    for g in range(C_KV_HEADS):
        q_stack = [jnp.concatenate([q[:, (hpg * g + 2 * jj) * LANES:(hpg * g + 2 * jj + 1) * LANES],
                                    q[:, (hpg * g + 2 * jj + 1) * LANES:(hpg * g + 2 * jj + 2) * LANES]], axis=0)
                   for jj in range(hpg // 2)]

        def stage_body(near, g=g, q_stack=q_stack):
            def body(c, mx):
                off = chunk_off(c)
                madd = sc_ref[pl.ds(off, ck), :]
                kc = k_ref[0, pl.ds(off, ck), g * LANES:(g + 1) * LANES]
                tidx = [jnp.clip(i - (c * per + r), 0, NEAR_BLOCKS) for r in range(per)]
                out = []
                for jj in range(hpg // 2):
                    lg = _mm_nt(kc, q_stack[jj])
                    for a in range(2):
                        hl = 2 * jj + a
                        x = lg[:, a * LANES:(a + 1) * LANES] + madd
                        if near:
                            x = x + jnp.concatenate([tab_ref[hpg * g + hl, tidx[r]] for r in range(per)], axis=0)
                        x_ref[hl, pl.ds(off, ck), :] = x
                        out.append(jnp.maximum(mx[hl], _fold(x, jnp.maximum)))
                return tuple(out)

            return body

        mx = tuple(jnp.full((SUBLANES, LANES), NEG_INF, F32) for _ in range(hpg))
        mx = lax.fori_loop(0, n_far, stage_body(False), mx)
        mx = lax.fori_loop(n_far, nch, stage_body(True), mx)
        m = [jnp.max(v, axis=0, keepdims=True) for v in mx]
        acc_ref[...] = jnp.zeros(acc_ref.shape, F32)

        def att_body(c, ls, g=g, m=m):
            off = chunk_off(c)
            vt = vt_ref[0, g * LANES:(g + 1) * LANES, pl.ds(off, ck)]
            out = []
            for jj in range(hpg // 2):
                ps = []
                for a in range(2):
                    hl = 2 * jj + a
                    p = jnp.exp(x_ref[hl, pl.ds(off, ck), :] - m[hl])
                    out.append(ls[hl] + _fold(p, jnp.add))
                    ps.append(p.astype(BF16))
                acc_ref[jj] += _mm(vt, jnp.concatenate(ps, axis=1))
            return tuple(out)

        ls = lax.fori_loop(0, nch, att_body, tuple(jnp.zeros((SUBLANES, LANES), F32) for _ in range(hpg)))
        for hl in range(hpg):
            sl = slice((hpg * g + hl) * LANES, (hpg * g + hl + 1) * LANES)
            ot = acc_ref[hl // 2, :, (hl % 2) * LANES:(hl % 2 + 1) * LANES] / jnp.sum(ls[hl], axis=0, keepdims=True)
            o_ref[0, :, sl] = (ot.T * sg_ref[0, :, sl].astype(F32)).astype(BF16)


def _dsa(q, qi, wt, sg, k, vt, ki2, tab):
    b, l, cw = q.shape
    nb = l // BLOCK
    topk = min(TOPK_MAX, l // 4)

    def blk(n):
        return pl.BlockSpec((1, BLOCK, n), lambda bb, i: (bb, i, 0))

    def whole(s1, s2):
        return pl.BlockSpec((1, s1, s2), lambda bb, i: (bb, 0, 0), pipeline_mode=pl.Buffered(1))

    hpg = C_HEADS // C_KV_HEADS
    return pl.pallas_call(
        functools.partial(_dsa_kernel, seq_len=l, topk=topk),
        grid=(b, nb),
        in_specs=[blk(cw), blk(qi.shape[2]), pl.BlockSpec((1, IDX_HEADS, BLOCK), lambda bb, i: (bb, 0, i)), blk(cw),
                  whole(l, k.shape[2]), whole(vt.shape[1], l), whole(l, ki2.shape[2]),
                  pl.BlockSpec(tab.shape, lambda bb, i: (0, 0, 0, 0), pipeline_mode=pl.Buffered(1))],
        out_specs=blk(cw),
        out_shape=jax.ShapeDtypeStruct((b, l, cw), BF16),
        scratch_shapes=[pltpu.VMEM((l, LANES), F32), pltpu.VMEM((2 * CAND * SUBLANES, LANES), F32),
                        pltpu.VMEM((hpg, l, LANES), F32),
                        pltpu.VMEM((hpg // 2, C_HEAD_DIM, 2 * LANES), F32)],
        compiler_params=_params("arbitrary", "arbitrary"),
        name="dsa",
    )(q, qi, wt, sg, k, vt, ki2, tab)


def _out_kernel(h_ref, a_ref, w_ref, o_ref):
    o_ref[...] = h_ref[...] + _mm(a_ref[...], w_ref[...])


def _outproj(h2, a2, w):
    rows, d = h2.shape
    t = ROW_TILE
    return pl.pallas_call(
        _out_kernel,
        grid=(rows // t,),
        in_specs=[pl.BlockSpec((t, d), lambda i: (i, 0)), pl.BlockSpec((t, a2.shape[1]), lambda i: (i, 0)),
                  _const_spec(w.shape)],
        out_specs=pl.BlockSpec((t, d), lambda i: (i, 0)),
        out_shape=jax.ShapeDtypeStruct((rows, d), F32),
        compiler_params=_params("arbitrary"),
        name="outproj1",
    )(h2, a2, w)


def _bias_tables(rel_bias, seq_len):
    del seq_len
    nv = (NEAR_BLOCKS + 1) * BLOCK
    vec = rel_bias[_t5_bucket(jnp.arange(nv, dtype=I32))].astype(F32).T

    def window(lo, n):
        pad = max(0, -lo)
        body = vec[:, max(lo, 0):lo + n]
        return jnp.concatenate([jnp.broadcast_to(vec[:, :1], (vec.shape[0], pad)), body], axis=1)

    def toeplitz(g, rows, cols):
        w = rows + cols
        g2 = jnp.concatenate([g[:, rows - 1:rows - 1 + cols], g[:, :1], g[:, :rows - 1]], axis=1)
        flat = jnp.tile(g2, (1, rows))[:, :rows * (w - 1)]
        return flat.reshape(-1, rows, w - 1)[:, :, :cols]

    bias0 = jnp.transpose(toeplitz(window(BLOCK - (2 * BLOCK - 1), 3 * BLOCK - 1), 2 * BLOCK, BLOCK), (0, 2, 1))
    tiles = [toeplitz(window(dl * BLOCK - (BLOCK - 1), 2 * BLOCK - 1), BLOCK, BLOCK) for dl in range(NEAR_BLOCKS)]
    tab = jnp.stack(tiles, axis=1) - rel_bias[NUM_BUCKETS - 1].astype(F32)[:, None, None, None]
    tab = jnp.concatenate([tab, jnp.zeros((tab.shape[0], 1, BLOCK, BLOCK), F32)], axis=1)
    return bias0, tab


def kernel(x, rel_bias, norm_g, ev_w_in, ev_w_out, ev_q_norm_g, ev_k_norm_g, ev_sinks, ev_ssm_log_dt, ev_ssm_a_re,
           ev_ssm_a_im, ev_ssm_b_re, ev_ssm_b_im, ev_ssm_c_re, ev_ssm_c_im, ev_ssm_d, ev_glu_w, ev_glu_b, od_w_in,
           od_w_out, od_q_norm_g, od_k_norm_g):
    b, l, d = x.shape
    assert l % KEY_CHUNK == 0 and l % ROW_TILE == 0
    assert (NEAR_BLOCKS - 1) * BLOCK + 1 >= 16 * 64 ** (15 / 16) + 1
    bias0, tab = _bias_tables(rel_bias, l)

    q0, k0, v0, sga, u, sgb = _proj0(x.reshape(b * l, d), norm_g[0][None, :], ev_w_in[0].astype(BF16))
    shp = lambda a: a.reshape(b, l, a.shape[-1])
    qg2 = jnp.tile(ev_q_norm_g[0], 2)[None, :]
    kg2 = jnp.tile(ev_k_norm_g[0], 2)[None, :]
    sinks = jnp.broadcast_to(ev_sinks[0][:, None], (A_HEADS, LANES)).astype(F32)
    att0 = _attn0(shp(q0), shp(k0), shp(v0), shp(sga), bias0, sinks, qg2, kg2)
    bmat, cre, cim, sc = _s5_prep(ev_ssm_log_dt[0], ev_ssm_a_re[0], ev_ssm_a_im[0], ev_ssm_b_re[0], ev_ssm_b_im[0],
                                  ev_ssm_c_re[0], ev_ssm_c_im[0])
    ssm0 = _ssm(shp(u), shp(sgb), bmat, cre, cim, sc, ev_ssm_d[0].reshape(1, -1), ev_glu_w[0].astype(BF16),
                ev_glu_b[0][None, :])

    w1 = od_w_in[0]
    cw = C_HEADS * C_HEAD_DIM
    ckv = C_KV_HEADS * C_HEAD_DIM
    o = np.cumsum([0, cw, ckv, ckv, cw, IDX_HEADS * IDX_DIM, IDX_DIM, IDX_HEADS])
    wq, wk, wv, wg, wqi, wki, ww = (w1[:, o[n]:o[n + 1]] for n in range(7))
    zki = jnp.zeros((d, LANES - IDX_DIM), w1.dtype)
    wki2 = jnp.concatenate([wki, zki, zki, wki], axis=1)
    bf = lambda a: a.astype(BF16)
    h1, q1, k1, vt1, sg1, qi1, ki2, wt1 = _mid(
        x, att0, ssm0, bf(ev_w_out[0]), norm_g[1][None, :], bf(wq), bf(wk), bf(wv.T), bf(wg), bf(wqi), bf(wki2),
        bf(ww.T), od_q_norm_g[0][None, :], od_k_norm_g[0][None, :])
    att1 = _dsa(q1, qi1, wt1, sg1, k1, vt1, ki2, tab)
    out = _outproj(h1.reshape(b * l, d), att1.reshape(b * l, cw), bf(od_w_out[0]))
    return out.reshape(b, l, d)
```

```python
import functools
import math

import jax
import jax.numpy as jnp
import numpy as np
from jax import lax
from jax.experimental import pallas as pl
from jax.experimental.pallas import tpu as pltpu

F32 = jnp.float32
BF16 = jnp.bfloat16
I32 = jnp.int32

LANES = 128
SUBLANES = 8
VMEM_LIMIT = 56 * 1024 * 1024

BLOCK = 128
WINDOW = 128
A_HEADS = 8
A_HEAD_DIM = 64
A_KV_HEADS = 2
A_WIDTH = A_HEADS * A_HEAD_DIM
SSM_GROUP = 16
SSM_STATE = 64
C_HEADS = 8
C_HEAD_DIM = 128
C_KV_HEADS = 2
IDX_HEADS = 8
IDX_DIM = 64
TOPK_MAX = 256
NUM_BUCKETS = 32
REL_MAX_DIST = 1024
EPS = 1e-6
NEG_INF = -1e30
INT_MIN = -(2 ** 31)
KEY_MIN_NORMAL = 0x00800000
KEY_POS_INF = 0x7F800000
KEY_NEG_INF = INT_MIN + 0x00800000

ROW_TILE = 256
KEY_CHUNK = 1024
NEAR_BLOCKS = 8
FOLD_CHAINS = 8
COUNT_ROWS = 512
CAND = 32
STREAMS = 4
NT_DIMS = (((1,), (1,)), ((), ()))


def _t5_bucket(dist):
    n = jnp.maximum(dist, 0)
    max_exact = NUM_BUCKETS // 2
    nf = jnp.maximum(n, 1).astype(F32)
    large = max_exact + (jnp.log(nf / max_exact) / math.log(REL_MAX_DIST / max_exact)
                         * (NUM_BUCKETS - max_exact)).astype(I32)
    large = jnp.minimum(large, NUM_BUCKETS - 1)
    return jnp.where(n < max_exact, n, large)


def _silu(x):
    return x * jax.nn.sigmoid(x)


def _rms(x, g):
    ms = jnp.mean(x * x, axis=-1, keepdims=True)
    return x * lax.rsqrt(ms + EPS) * g


def _mm(a, b):
    return jnp.dot(a, b, preferred_element_type=F32)


def _mm_nt(a, b):
    return lax.dot_general(a, b, NT_DIMS, preferred_element_type=F32)


def _fold(x, op):
    n = x.shape[0] // SUBLANES
    chains = min(FOLD_CHAINS, n)
    accs = [x[r * SUBLANES:(r + 1) * SUBLANES] for r in range(chains)]
    for r in range(chains, n):
        accs[r % chains] = op(accs[r % chains], x[r * SUBLANES:(r + 1) * SUBLANES])
    while len(accs) > 1:
        accs = [op(a, b) for a, b in zip(accs[::2], accs[1::2])] + accs[len(accs) & ~1:]
    return accs[0]


def _params(*sem):
    return pltpu.CompilerParams(dimension_semantics=sem, vmem_limit_bytes=VMEM_LIMIT)


def _const_spec(shape):
    zeros = (0,) * len(shape)
    return pl.BlockSpec(shape, lambda *_: zeros)


def _proj0_kernel(x_ref, g_ref, w_ref, q_ref, k_ref, v_ref, sga_ref, u_ref, sgb_ref):
    hn = _rms(x_ref[...], g_ref[...]).astype(BF16)

    def mm(lo, hi):
        return _mm(hn, w_ref[:, lo:hi])

    q_ref[...] = mm(0, 512)
    k_ref[...] = mm(512, 640)
    v_ref[...] = mm(640, 768)
    sga_ref[...] = _silu(mm(768, 1280)).astype(BF16)
    u_ref[...] = mm(1280, 1792)
    sgb_ref[...] = _silu(mm(1792, 2304)).astype(BF16)


def _proj0(x2, g, w):
    rows, d = x2.shape
    t = ROW_TILE

    def row(n):
        return pl.BlockSpec((t, n), lambda i: (i, 0))

    return pl.pallas_call(
        _proj0_kernel,
        grid=(rows // t,),
        in_specs=[row(d), _const_spec((1, d)), _const_spec(w.shape)],
        out_specs=[row(512), row(128), row(128), row(512), row(512), row(512)],
        out_shape=[jax.ShapeDtypeStruct((rows, 512), F32), jax.ShapeDtypeStruct((rows, 128), F32),
                   jax.ShapeDtypeStruct((rows, 128), F32), jax.ShapeDtypeStruct((rows, 512), BF16),
                   jax.ShapeDtypeStruct((rows, 512), F32), jax.ShapeDtypeStruct((rows, 512), BF16)],
        compiler_params=_params("arbitrary"),
        name="proj0",
    )(x2, g, w)


def _attn0_kernel(q_ref, kc_ref, kp_ref, vc_ref, vp_ref, sga_ref, bias_ref, sink_ref, qg_ref, kg_ref, o_ref):
    i = pl.program_id(1)
    lane = lax.broadcasted_iota(I32, (1, LANES), 1)
    lo = lane < A_HEAD_DIM

    def segnorm(x, g2):
        sq = x * x
        s_lo = jnp.sum(jnp.where(lo, sq, 0.0), axis=-1, keepdims=True)
        s_hi = jnp.sum(jnp.where(lo, 0.0, sq), axis=-1, keepdims=True)
        inv = jnp.where(lo, lax.rsqrt(s_lo / A_HEAD_DIM + EPS), lax.rsqrt(s_hi / A_HEAD_DIM + EPS))
        return x * inv * g2

    kn = segnorm(jnp.concatenate([kp_ref[0], kc_ref[0]], axis=0), kg_ref[...])
    vb = jnp.concatenate([vp_ref[0], vc_ref[0]], axis=0)
    kr = pltpu.roll(kn, A_HEAD_DIM, axis=1)
    vr = pltpu.roll(vb, A_HEAD_DIM, axis=1)

    def variants(x, xr):
        return {(0, 0): jnp.where(lo, x, 0.0).astype(BF16), (0, 1): jnp.where(lo, 0.0, xr).astype(BF16),
                (1, 0): jnp.where(lo, xr, 0.0).astype(BF16), (1, 1): jnp.where(lo, 0.0, x).astype(BF16)}

    kvar = variants(kn, kr)
    vvar = variants(vb, vr)

    row = lax.broadcasted_iota(I32, (BLOCK, 2 * BLOCK), 0)
    col = lax.broadcasted_iota(I32, (BLOCK, 2 * BLOCK), 1)
    d = row + BLOCK - col
    mask = (d >= 0) & (d < WINDOW) & ((i > 0) | (col >= BLOCK))

    lgs, sinks = [], []
    for p in range(A_HEADS // 2):
        qp = (segnorm(q_ref[0, :, p * LANES:(p + 1) * LANES], qg_ref[...]) * (A_HEAD_DIM ** -0.5)).astype(BF16)
        for a in range(2):
            h = 2 * p + a
            lgs.append(jnp.where(mask, _mm_nt(qp, kvar[(p // 2, a)]) + bias_ref[h], NEG_INF))
            sinks.append(jnp.broadcast_to(sink_ref[h:h + 1, 0:1], (BLOCK, 1)))
    lg = jnp.concatenate(lgs, axis=0)
    sink = jnp.concatenate(sinks, axis=0)
    m = jnp.maximum(jnp.max(lg, axis=-1, keepdims=True), sink)
    e = jnp.exp(lg - m)
    den = jnp.sum(e, axis=-1, keepdims=True) + jnp.exp(sink - m)
    pr = (e / den).astype(BF16)
    for p in range(A_HEADS // 2):
        sl = slice(p * LANES, (p + 1) * LANES)
        acc = jnp.zeros((BLOCK, LANES), F32)
        for a in range(2):
            h = 2 * p + a
            acc = acc + _mm(pr[h * BLOCK:(h + 1) * BLOCK], vvar[(p // 2, a)])
        o_ref[0, :, sl] = (acc * sga_ref[0, :, sl].astype(F32)).astype(BF16)


def _attn0(q, k, v, sga, bias0, sinks, qg2, kg2):
    b, l, _ = q.shape
    nb = l // BLOCK

    def cur(n):
        return pl.BlockSpec((1, BLOCK, n), lambda bb, i: (bb, i, 0))

    def prev(n):
        return pl.BlockSpec((1, BLOCK, n), lambda bb, i: (bb, jnp.maximum(i - 1, 0), 0))

    return pl.pallas_call(
        _attn0_kernel,
        grid=(b, nb),
        in_specs=[cur(512), cur(128), prev(128), cur(128), prev(128), cur(512),
                  _const_spec(bias0.shape), _const_spec(sinks.shape), _const_spec(qg2.shape), _const_spec(kg2.shape)],
        out_specs=cur(512),
        out_shape=jax.ShapeDtypeStruct((b, l, 512), BF16),
        compiler_params=_params("arbitrary", "arbitrary"),
        name="attn0",
    )(q, k, k, v, v, sga, bias0, sinks, qg2, kg2)


def _ssm_kernel(u_ref, sgb_ref, bmat_ref, cre_ref, cim_ref, sc_ref, d_ref, gw_ref, gb_ref, o_ref, xre_ref, xim_ref):
    t = u_ref.shape[1]
    nq = bmat_ref.shape[0]
    half = bmat_ref.shape[2] // 2

    @pl.when(pl.program_id(1) == 0)
    def _():
        xre_ref[0:SUBLANES, :] = jnp.zeros((SUBLANES, xre_ref.shape[1]), F32)
        xim_ref[0:SUBLANES, :] = jnp.zeros((SUBLANES, xim_ref.shape[1]), F32)

    u = u_ref[0]
    ub = u.astype(BF16)
    for q in range(nq):
        bu = _mm(ub[:, q * LANES:(q + 1) * LANES], bmat_ref[q])
        xre_ref[SUBLANES:, q * half:(q + 1) * half] = bu[:, :half]
        xim_ref[SUBLANES:, q * half:(q + 1) * half] = bu[:, half:]

    def scan(r, _):
        base = pl.multiple_of(SUBLANES + r * SUBLANES, SUBLANES)
        xr = xre_ref[pl.ds(base, SUBLANES), :]
        xi = xim_ref[pl.ds(base, SUBLANES), :]
        for s, k in enumerate((1, 2, 4)):
            ar = sc_ref[2 * s]
            ai = sc_ref[2 * s + 1]
            sr = pltpu.roll(xr, k, axis=0)
            si = pltpu.roll(xi, k, axis=0)
            xr, xi = xr + ar * sr - ai * si, xi + ar * si + ai * sr
        cr = xre_ref[pl.ds(base - 1, 1), :]
        ci = xim_ref[pl.ds(base - 1, 1), :]
        pr = sc_ref[6]
        pi = sc_ref[7]
        xre_ref[pl.ds(base, SUBLANES), :] = xr + pr * cr - pi * ci
        xim_ref[pl.ds(base, SUBLANES), :] = xi + pr * ci + pi * cr
        return 0

    lax.fori_loop(0, t // SUBLANES, scan, 0, unroll=2)
    xre_ref[0:SUBLANES, :] = xre_ref[t:t + SUBLANES, :]
    xim_ref[0:SUBLANES, :] = xim_ref[t:t + SUBLANES, :]

    ys = []
    for q in range(nq):
        xr = xre_ref[SUBLANES:, q * half:(q + 1) * half].astype(BF16)
        xi = xim_ref[SUBLANES:, q * half:(q + 1) * half].astype(BF16)
        ys.append(_mm(xr, cre_ref[q]) + _mm(xi, cim_ref[q]))
    y = jnp.concatenate(ys, axis=1) + d_ref[...] * u
    y = jax.nn.gelu(y).astype(BF16)
    hh = _mm(y, gw_ref[...]) + gb_ref[...]
    w = hh.shape[1] // 2
    o_ref[0] = (hh[:, :w] * jax.nn.sigmoid(hh[:, w:]) * sgb_ref[0].astype(F32)).astype(BF16)


def _ssm(u, sgb, bmat, cre, cim, sc, dskip, gw, gb):
    b, l, w = u.shape
    t = ROW_TILE
    ns = sc.shape[-1]

    def row(n):
        return pl.BlockSpec((1, t, n), lambda bb, i: (bb, i, 0))

    return pl.pallas_call(
        _ssm_kernel,
        grid=(b, l // t),
        in_specs=[row(w), row(w), _const_spec(bmat.shape), _const_spec(cre.shape), _const_spec(cim.shape),
                  _const_spec(sc.shape), _const_spec(dskip.shape), _const_spec(gw.shape), _const_spec(gb.shape)],
        out_specs=row(w),
        out_shape=jax.ShapeDtypeStruct((b, l, w), BF16),
        scratch_shapes=[pltpu.VMEM((SUBLANES + t, ns), F32), pltpu.VMEM((SUBLANES + t, ns), F32)],
        compiler_params=_params("arbitrary", "arbitrary"),
        name="ssm",
    )(u, sgb, bmat, cre, cim, sc, dskip, gw, gb)


def _s5_prep(log_dt, a_re, a_im, b_re, b_im, c_re, c_im):
    g, p = a_re.shape
    h = b_re.shape[-1]
    gl = LANES // h
    nq = g // gl
    dt = jnp.exp(log_dt)[:, None]
    mag = jnp.exp(a_re * dt)
    ang = a_im * dt
    ab_re = mag * jnp.cos(ang)
    ab_im = mag * jnp.sin(ang)
    den = a_re * a_re + a_im * a_im
    n_re = ab_re - 1.0
    n_im = ab_im
    f_re = (n_re * a_re + n_im * a_im) / den
    f_im = (n_im * a_re - n_re * a_im) / den
    bb_re = f_re[..., None] * b_re - f_im[..., None] * b_im
    bb_im = f_re[..., None] * b_im + f_im[..., None] * b_re
    eye = jnp.eye(gl, dtype=F32)

    def bdiag_in(m):
        m = m.reshape(nq, gl, p, h)
        return jnp.einsum('qgph,gk->qghkp', m, eye).reshape(nq, gl * h, gl * p)

    def bdiag_out(m):
        m = m.reshape(nq, gl, h, p)
        return jnp.einsum('qghp,gk->qgpkh', m, eye).reshape(nq, gl * p, gl * h)

    bmat = jnp.concatenate([bdiag_in(bb_re), bdiag_in(bb_im)], axis=2).astype(BF16)
    cre = bdiag_out(c_re).astype(BF16)
    cim = bdiag_out(-c_im).astype(BF16)

    pw = [(ab_re.reshape(-1), ab_im.reshape(-1))]
    for _ in range(SUBLANES - 1):
        pr, pi = pw[-1]
        pw.append((pr * pw[0][0] - pi * pw[0][1], pr * pw[0][1] + pi * pw[0][0]))
    rows = jnp.arange(SUBLANES)[:, None]
    sc = []
    for k in (1, 2, 4):
        sc.append(jnp.where(rows >= k, pw[k - 1][0][None, :], 0.0))
        sc.append(jnp.where(rows >= k, pw[k - 1][1][None, :], 0.0))
    sc.append(jnp.stack([pw[r][0] for r in range(SUBLANES)]))
    sc.append(jnp.stack([pw[r][1] for r in range(SUBLANES)]))
    return bmat, cre, cim, jnp.stack(sc).astype(F32)


def _mid_kernel(x_ref, a_ref, s_ref, wo_ref, g_ref, wq_ref, wk_ref, wvt_ref, wg_ref, wqi_ref, wki_ref, wwt_ref,
                qg_ref, kg_ref, h_ref, q_ref, k_ref, vt_ref, sg_ref, qi_ref, ki_ref, wt_ref):
    aw = a_ref.shape[2]
    h = x_ref[0] + _mm(a_ref[0], wo_ref[0:aw, :]) + _mm(s_ref[0], wo_ref[aw:, :])
    h_ref[0] = h
    hn = _rms(h, g_ref[...]).astype(BF16)
    qf = _mm(hn, wq_ref[...])
    for hd in range(C_HEADS):
        sl = slice(hd * C_HEAD_DIM, (hd + 1) * C_HEAD_DIM)
        q_ref[0, :, sl] = (_rms(qf[:, sl], qg_ref[...]) * (C_HEAD_DIM ** -0.5)).astype(BF16)
    kf = _mm(hn, wk_ref[...])
    for hd in range(C_KV_HEADS):
        sl = slice(hd * C_HEAD_DIM, (hd + 1) * C_HEAD_DIM)
        k_ref[0, :, sl] = _rms(kf[:, sl], kg_ref[...]).astype(BF16)
    vt_ref[0] = _mm_nt(wvt_ref[...], hn).astype(BF16)
    sg_ref[0] = _silu(_mm(hn, wg_ref[...])).astype(BF16)
    qi_ref[0] = _mm(hn, wqi_ref[...]).astype(BF16)
    ki_ref[0] = _mm(hn, wki_ref[...]).astype(BF16)
    wt_ref[0] = _mm_nt(wwt_ref[...], hn) * ((IDX_HEADS ** -0.5) * (IDX_DIM ** -0.5))


def _mid(x, att0, ssm0, wo, g, wq, wk, wvt, wg, wqi, wki2, wwt, qg, kg):
    b, l, d = x.shape
    t = ROW_TILE

    def row(n):
        return pl.BlockSpec((1, t, n), lambda bb, i: (bb, i, 0))

    def col(n):
        return pl.BlockSpec((1, n, t), lambda bb, i: (bb, 0, i))

    weights = [wo, g, wq, wk, wvt, wg, wqi, wki2, wwt, qg, kg]
    cw = C_HEADS * C_HEAD_DIM
    ckv = C_KV_HEADS * C_HEAD_DIM
    return pl.pallas_call(
        _mid_kernel,
        grid=(b, l // t),
        in_specs=[row(d), row(att0.shape[2]), row(ssm0.shape[2])] + [_const_spec(w.shape) for w in weights],
        out_specs=[row(d), row(cw), row(ckv), col(ckv), row(cw), row(IDX_HEADS * IDX_DIM), row(2 * LANES),
                   col(IDX_HEADS)],
        out_shape=[jax.ShapeDtypeStruct((b, l, d), F32), jax.ShapeDtypeStruct((b, l, cw), BF16),
                   jax.ShapeDtypeStruct((b, l, ckv), BF16), jax.ShapeDtypeStruct((b, ckv, l), BF16),
                   jax.ShapeDtypeStruct((b, l, cw), BF16), jax.ShapeDtypeStruct((b, l, IDX_HEADS * IDX_DIM), BF16),
                   jax.ShapeDtypeStruct((b, l, 2 * LANES), BF16), jax.ShapeDtypeStruct((b, IDX_HEADS, l), F32)],
        compiler_params=_params("arbitrary", "arbitrary"),
        name="mid",
    )(x, att0, ssm0, *weights)


def _dsa_kernel(q_ref, qi_ref, wt_ref, sg_ref, k_ref, vt_ref, ki_ref, tab_ref, o_ref,
                sc_ref, best_ref, x_ref, acc_ref, *, seq_len, topk):
    i = pl.program_id(1)
    ck = KEY_CHUNK
    per = ck // BLOCK
    nch = (i + per) // per
    t_row = i * BLOCK + lax.broadcasted_iota(I32, (1, LANES), 1)
    kiota = lax.broadcasted_iota(I32, (ck, LANES), 0)

    def chunk_off(c):
        return pl.multiple_of(c * ck, ck)

    qi = qi_ref[0]
    qi_stack = [jnp.concatenate([qi[:, (2 * s) * LANES:(2 * s + 1) * LANES],
                                 qi[:, (2 * s + 1) * LANES:(2 * s + 2) * LANES]], axis=0) for s in range(2)]
    wt = wt_ref[0]

    def score_chunk(c, masked):
        off = chunk_off(c)
        sc = jnp.zeros((ck, LANES), F32)
        for a in range(2):
            kk = ki_ref[0, pl.ds(off, ck), a * LANES:(a + 1) * LANES]
            for s in range(2):
                r = _mm_nt(kk, qi_stack[s])
                for j in range(2):
                    hd = 2 * (2 * s + j) + a
                    sc = sc + jnp.maximum(r[:, j * LANES:(j + 1) * LANES], 0.0) * wt[hd:hd + 1, :]
        if masked:
            sc = jnp.where(off + kiota <= t_row, sc, NEG_INF)
        sc_ref[pl.ds(off, ck), :] = sc
        return _fold(sc, jnp.maximum)

    def score_body(c, mx):
        return jnp.maximum(mx, score_chunk(c, False))

    smax = lax.fori_loop(0, nch - 1, score_body, jnp.full((SUBLANES, LANES), NEG_INF, F32))
    smax = jnp.max(jnp.maximum(smax, score_chunk(nch - 1, True)), axis=0, keepdims=True)

    def count(pred):
        rows = COUNT_ROWS

        def body(c, accs):
            out = []
            for u, acc in enumerate(accs):
                off = pl.multiple_of(c * ck + u * rows, rows)
                ind = pred(sc_ref[pl.ds(off, rows), :], off).astype(I32)
                out.append(acc + jnp.sum(ind.reshape(rows // SUBLANES, SUBLANES, LANES), axis=0))
            return tuple(out)

        accs = lax.fori_loop(0, nch, body, tuple(jnp.zeros((SUBLANES, LANES), I32) for _ in range(ck // rows)))
        return jnp.sum(sum(accs), axis=0, keepdims=True)

    def key_value(k):
        return pltpu.bitcast(jnp.where(k < 0, INT_MIN - k, k), F32)

    def count_ge(k):
        thr = key_value(k)
        return count(lambda s, off: s >= thr)

    def full(v):
        return jnp.full((1, LANES), v, I32)

    def bisect(_, st):
        lo, hi, c_lo, c_hi = st
        mid = (lo >> 1) + (hi >> 1) + (lo & hi & 1)
        c = count_ge(mid)
        ge = c >= topk
        return jnp.where(ge, mid, lo), jnp.where(ge, hi, mid), jnp.where(ge, c, c_lo), jnp.where(ge, c_hi, c)

    searching = (i + 1) * BLOCK > topk

    def float_key(x):
        bits = pltpu.bitcast(x, I32)
        return jnp.where(bits < 0, INT_MIN - bits, bits)

    def search():
        k_lo = float_key(smax * 0.125)
        c = count_ge(k_lo)
        ok = (smax > 0.0) & (c >= topk)
        trips = jnp.where(jnp.min(jnp.where(ok, 1, 0)) > 0, 25, 32)
        st = (jnp.where(ok, k_lo, KEY_NEG_INF), float_key(smax) + 1, jnp.where(ok, c, nch * ck), full(0))
        out = lax.fori_loop(0, trips, bisect, st)
        return out[0], out[2], out[3]

    def exchange(v, a, b):
        v[a], v[b] = jnp.maximum(v[a], v[b]), jnp.minimum(v[a], v[b])

    def sort_desc(v):
        n, k = len(v), 2
        while k <= n:
            j = k // 2
            while j >= 1:
                for a in range(n):
                    b = a ^ j
                    if b > a:
                        exchange(v, *((a, b) if (a & k) == 0 else (b, a)))
                j //= 2
            k *= 2

    def merge_top(best, blk):
        n = len(best)
        v = [jnp.maximum(best[r], blk[n - 1 - r]) for r in range(n)]
        j = n // 2
        while j >= 1:
            for a in range(n):
                if a ^ j > a:
                    exchange(v, a, a ^ j)
            j //= 2
        return v

    crow = STREAMS * CAND * SUBLANES

    def cand_body(c, _):
        blk_all = sc_ref[pl.ds(pl.multiple_of(c * crow, crow), crow), :]
        for st in range(STREAMS):
            blk = [blk_all[(STREAMS * r + st) * SUBLANES:(STREAMS * r + st + 1) * SUBLANES] for r in range(CAND)]
            sort_desc(blk)
            base = st * CAND * SUBLANES
            best = [best_ref[base + r * SUBLANES:base + (r + 1) * SUBLANES, :] for r in range(CAND)]
            for r, x in enumerate(merge_top(best, blk)):
                best_ref[base + r * SUBLANES:base + (r + 1) * SUBLANES, :] = x
        return 0

    def cand_search():
        best_ref[...] = jnp.full(best_ref.shape, -jnp.inf, F32)
        lax.fori_loop(0, nch * (ck // crow), cand_body, 0)

        def count_cand(k):
            thr = key_value(k)
            parts = [jnp.sum((best_ref[r:r + COUNT_ROWS, :] >= thr).astype(I32)
                             .reshape(COUNT_ROWS // SUBLANES, SUBLANES, LANES), axis=0)
                     for r in range(0, crow, COUNT_ROWS)]
            return jnp.sum(sum(parts), axis=0, keepdims=True)

        def step(_, st):
            lo, hi = st
            mid = (lo >> 1) + (hi >> 1) + (lo & hi & 1)
            ge = count_cand(mid) >= topk
            return jnp.where(ge, mid, lo), jnp.where(ge, hi, mid)

        k_lo = float_key(smax * 0.125)
        ok = (smax > 0.0) & (count_cand(k_lo) >= topk)
        trips = jnp.where(jnp.min(jnp.where(ok, 1, 0)) > 0, 25, 32)
        vk, _ = lax.fori_loop(0, trips, step, (jnp.where(ok, k_lo, KEY_NEG_INF), float_key(smax) + 1))
        thr = key_value(vk)
        above_cand = jnp.sum((best_ref[...] > thr).astype(I32), axis=0, keepdims=True)
        c_ge, c_gt = count(lambda s, off: s >= thr), count(lambda s, off: s > thr)
        complete = jnp.min(jnp.where(c_gt == above_cand, 1, 0)) > 0
        return lax.cond(complete, lambda: (vk, c_ge, c_gt), search)

    vkey, c_lo, c_hi = lax.cond(searching, cand_search, lambda: (full(KEY_NEG_INF), full(topk), full(0)))
    vthr = key_value(vkey)
    need = topk - c_hi
    ties = c_lo - c_hi

    def tie_search():
        nxt = vkey + 1
        nxt = jnp.where((nxt > 0) & (nxt < KEY_MIN_NORMAL), KEY_MIN_NORMAL, nxt)
        step = key_value(nxt) - vthr

        def split(_, st):
            fl, fh = st
            fm = 0.5 * (fl + fh)
            thr = vthr + fm * step
            ge = count(lambda s, off: s >= thr) >= topk
            return jnp.where(ge, fm, fl), jnp.where(ge, fh, fm)

        fl, _ = lax.fori_loop(0, 26, split, (jnp.zeros((1, LANES), F32), jnp.ones((1, LANES), F32)))
        thr = vthr + fl * step
        want = topk - count(lambda s, off: s > thr)

        def body(_, st):
            lj, hj = st
            mid = (lj + hj) >> 1
            c = count(lambda s, off: (s == thr) & (off + kiota[:COUNT_ROWS] <= mid))
            ok = c >= want
            return jnp.where(ok, lj, mid), jnp.where(ok, mid, hj)

        _, hj = lax.fori_loop(0, 14, body, (full(-1), full(0) + (nch * ck - 1)))
        return thr, hj

    any_tie = searching & (jnp.max(ties - need) > 0)
    vthr, jmax = lax.cond(any_tie, tie_search, lambda: (vthr, full(seq_len)))

    def mask_body(c, _):
        off = chunk_off(c)
        s = sc_ref[pl.ds(off, ck), :]
        s_idx = off + kiota
        sel = ((s > vthr) | ((s == vthr) & (s_idx <= jmax))) & (s_idx <= t_row)
        sc_ref[pl.ds(off, ck), :] = jnp.where(sel, 0.0, NEG_INF)
        return 0

    lax.fori_loop(0, nch, mask_body, 0)

    q = q_ref[0]
    n_far = jnp.maximum((i - NEAR_BLOCKS + 1) // per, 0)
    hpg = C_HEADS // C_KV_HEADS

    for g in range(C_KV_HEADS):
        q_stack = [jnp.concatenate([q[:, (hpg * g + 2 * jj) * LANES:(hpg * g + 2 * jj + 1) * LANES],
                                    q[:, (hpg * g + 2 * jj + 1) * LANES:(hpg * g + 2 * jj + 2) * LANES]], axis=0)
                   for jj in range(hpg // 2)]

        def stage_body(near, g=g, q_stack=q_stack):
            def body(c, mx):
                off = chunk_off(c)
                madd = sc_ref[pl.ds(off, ck), :]
                kc = k_ref[0, pl.ds(off, ck), g * LANES:(g + 1) * LANES]
                tidx = [jnp.clip(i - (c * per + r), 0, NEAR_BLOCKS) for r in range(per)]
                out = []
                for jj in range(hpg // 2):
                    lg = _mm_nt(kc, q_stack[jj])
                    for a in range(2):
                        hl = 2 * jj + a
                        x = lg[:, a * LANES:(a + 1) * LANES] + madd
                        if near:
                            x = x + jnp.concatenate([tab_ref[hpg * g + hl, tidx[r]] for r in range(per)], axis=0)
                        x_ref[hl, pl.ds(off, ck), :] = x
                        out.append(jnp.maximum(mx[hl], _fold(x, jnp.maximum)))
                return tuple(out)

            return body

        mx = tuple(jnp.full((SUBLANES, LANES), NEG_INF, F32) for _ in range(hpg))
        mx = lax.fori_loop(0, n_far, stage_body(False), mx)
        mx = lax.fori_loop(n_far, nch, stage_body(True), mx)
        m = [jnp.max(v, axis=0, keepdims=True) for v in mx]
        acc_ref[...] = jnp.zeros(acc_ref.shape, F32)

        def att_body(c, ls, g=g, m=m):
            off = chunk_off(c)
            vt = vt_ref[0, g * LANES:(g + 1) * LANES, pl.ds(off, ck)]
            out = []
            for jj in range(hpg // 2):
                ps = []
                for a in range(2):
                    hl = 2 * jj + a
                    p = jnp.exp(x_ref[hl, pl.ds(off, ck), :] - m[hl])
                    out.append(ls[hl] + _fold(p, jnp.add))
                    ps.append(p.astype(BF16))
                acc_ref[jj] += _mm(vt, jnp.concatenate(ps, axis=1))
            return tuple(out)

        ls = lax.fori_loop(0, nch, att_body, tuple(jnp.zeros((SUBLANES, LANES), F32) for _ in range(hpg)))
        for hl in range(hpg):
            sl = slice((hpg * g + hl) * LANES, (hpg * g + hl + 1) * LANES)
            ot = acc_ref[hl // 2, :, (hl % 2) * LANES:(hl % 2 + 1) * LANES] / jnp.sum(ls[hl], axis=0, keepdims=True)
            o_ref[0, :, sl] = (ot.T * sg_ref[0, :, sl].astype(F32)).astype(BF16)


def _dsa(q, qi, wt, sg, k, vt, ki2, tab):
    b, l, cw = q.shape
    nb = l // BLOCK
    topk = min(TOPK_MAX, l // 4)

    def blk(n):
        return pl.BlockSpec((1, BLOCK, n), lambda bb, i: (bb, i, 0))

    def whole(s1, s2):
        return pl.BlockSpec((1, s1, s2), lambda bb, i: (bb, 0, 0), pipeline_mode=pl.Buffered(1))

    hpg = C_HEADS // C_KV_HEADS
    return pl.pallas_call(
        functools.partial(_dsa_kernel, seq_len=l, topk=topk),
        grid=(b, nb),
        in_specs=[blk(cw), blk(qi.shape[2]), pl.BlockSpec((1, IDX_HEADS, BLOCK), lambda bb, i: (bb, 0, i)), blk(cw),
                  whole(l, k.shape[2]), whole(vt.shape[1], l), whole(l, ki2.shape[2]),
                  pl.BlockSpec(tab.shape, lambda bb, i: (0, 0, 0, 0), pipeline_mode=pl.Buffered(1))],
        out_specs=blk(cw),
        out_shape=jax.ShapeDtypeStruct((b, l, cw), BF16),
        scratch_shapes=[pltpu.VMEM((l, LANES), F32), pltpu.VMEM((STREAMS * CAND * SUBLANES, LANES), F32),
                        pltpu.VMEM((hpg, l, LANES), F32),
                        pltpu.VMEM((hpg // 2, C_HEAD_DIM, 2 * LANES), F32)],
        compiler_params=_params("arbitrary", "arbitrary"),
        name="dsa",
    )(q, qi, wt, sg, k, vt, ki2, tab)


def _out_kernel(h_ref, a_ref, w_ref, o_ref):
    o_ref[...] = h_ref[...] + _mm(a_ref[...], w_ref[...])


def _outproj(h2, a2, w):
    rows, d = h2.shape
    t = ROW_TILE
    return pl.pallas_call(
        _out_kernel,
        grid=(rows // t,),
        in_specs=[pl.BlockSpec((t, d), lambda i: (i, 0)), pl.BlockSpec((t, a2.shape[1]), lambda i: (i, 0)),
                  _const_spec(w.shape)],
        out_specs=pl.BlockSpec((t, d), lambda i: (i, 0)),
        out_shape=jax.ShapeDtypeStruct((rows, d), F32),
        compiler_params=_params("arbitrary"),
        name="outproj1",
    )(h2, a2, w)


def _bias_tables(rel_bias, seq_len):
    del seq_len
    nv = (NEAR_BLOCKS + 1) * BLOCK
    vec = rel_bias[_t5_bucket(jnp.arange(nv, dtype=I32))].astype(F32).T

    def window(lo, n):
        pad = max(0, -lo)
        body = vec[:, max(lo, 0):lo + n]
        return jnp.concatenate([jnp.broadcast_to(vec[:, :1], (vec.shape[0], pad)), body], axis=1)

    def toeplitz(g, rows, cols):
        w = rows + cols
        g2 = jnp.concatenate([g[:, rows - 1:rows - 1 + cols], g[:, :1], g[:, :rows - 1]], axis=1)
        flat = jnp.tile(g2, (1, rows))[:, :rows * (w - 1)]
        return flat.reshape(-1, rows, w - 1)[:, :, :cols]

    bias0 = jnp.transpose(toeplitz(window(BLOCK - (2 * BLOCK - 1), 3 * BLOCK - 1), 2 * BLOCK, BLOCK), (0, 2, 1))
    tiles = [toeplitz(window(dl * BLOCK - (BLOCK - 1), 2 * BLOCK - 1), BLOCK, BLOCK) for dl in range(NEAR_BLOCKS)]
    tab = jnp.stack(tiles, axis=1) - rel_bias[NUM_BUCKETS - 1].astype(F32)[:, None, None, None]
    tab = jnp.concatenate([tab, jnp.zeros((tab.shape[0], 1, BLOCK, BLOCK), F32)], axis=1)
    return bias0, tab


def kernel(x, rel_bias, norm_g, ev_w_in, ev_w_out, ev_q_norm_g, ev_k_norm_g, ev_sinks, ev_ssm_log_dt, ev_ssm_a_re,
           ev_ssm_a_im, ev_ssm_b_re, ev_ssm_b_im, ev_ssm_c_re, ev_ssm_c_im, ev_ssm_d, ev_glu_w, ev_glu_b, od_w_in,
           od_w_out, od_q_norm_g, od_k_norm_g):
    b, l, d = x.shape
    assert l % KEY_CHUNK == 0 and l % ROW_TILE == 0
    assert (NEAR_BLOCKS - 1) * BLOCK + 1 >= 16 * 64 ** (15 / 16) + 1
    bias0, tab = _bias_tables(rel_bias, l)

    q0, k0, v0, sga, u, sgb = _proj0(x.reshape(b * l, d), norm_g[0][None, :], ev_w_in[0].astype(BF16))
    shp = lambda a: a.reshape(b, l, a.shape[-1])
    qg2 = jnp.tile(ev_q_norm_g[0], 2)[None, :]
    kg2 = jnp.tile(ev_k_norm_g[0], 2)[None, :]
    sinks = jnp.broadcast_to(ev_sinks[0][:, None], (A_HEADS, LANES)).astype(F32)
    att0 = _attn0(shp(q0), shp(k0), shp(v0), shp(sga), bias0, sinks, qg2, kg2)
    bmat, cre, cim, sc = _s5_prep(ev_ssm_log_dt[0], ev_ssm_a_re[0], ev_ssm_a_im[0], ev_ssm_b_re[0], ev_ssm_b_im[0],
                                  ev_ssm_c_re[0], ev_ssm_c_im[0])
    ssm0 = _ssm(shp(u), shp(sgb), bmat, cre, cim, sc, ev_ssm_d[0].reshape(1, -1), ev_glu_w[0].astype(BF16),
                ev_glu_b[0][None, :])

    w1 = od_w_in[0]
    cw = C_HEADS * C_HEAD_DIM
    ckv = C_KV_HEADS * C_HEAD_DIM
    o = np.cumsum([0, cw, ckv, ckv, cw, IDX_HEADS * IDX_DIM, IDX_DIM, IDX_HEADS])
    wq, wk, wv, wg, wqi, wki, ww = (w1[:, o[n]:o[n + 1]] for n in range(7))
    zki = jnp.zeros((d, LANES - IDX_DIM), w1.dtype)
    wki2 = jnp.concatenate([wki, zki, zki, wki], axis=1)
    bf = lambda a: a.astype(BF16)
    h1, q1, k1, vt1, sg1, qi1, ki2, wt1 = _mid(
        x, att0, ssm0, bf(ev_w_out[0]), norm_g[1][None, :], bf(wq), bf(wk), bf(wv.T), bf(wg), bf(wqi), bf(wki2),
        bf(ww.T), od_q_norm_g[0][None, :], od_k_norm_g[0][None, :])
    att1 = _dsa(q1, qi1, wt1, sg1, k1, vt1, ki2, tab)
    out = _outproj(h1.reshape(b * l, d), att1.reshape(b * l, cw), bf(od_w_out[0]))
    return out.reshape(b, l, d)
```

```python
import functools
import math

import jax
import jax.numpy as jnp
import numpy as np
from jax import lax
from jax.experimental import pallas as pl
from jax.experimental.pallas import tpu as pltpu

F32 = jnp.float32
BF16 = jnp.bfloat16
I32 = jnp.int32

LANES = 128
SUBLANES = 8
VMEM_LIMIT = 56 * 1024 * 1024

BLOCK = 128
WINDOW = 128
A_HEADS = 8
A_HEAD_DIM = 64
A_KV_HEADS = 2
A_WIDTH = A_HEADS * A_HEAD_DIM
SSM_GROUP = 16
SSM_STATE = 64
C_HEADS = 8
C_HEAD_DIM = 128
C_KV_HEADS = 2
IDX_HEADS = 8
IDX_DIM = 64
TOPK_MAX = 256
NUM_BUCKETS = 32
REL_MAX_DIST = 1024
EPS = 1e-6
NEG_INF = -1e30
INT_MIN = -(2 ** 31)
KEY_MIN_NORMAL = 0x00800000
KEY_POS_INF = 0x7F800000
KEY_NEG_INF = INT_MIN + 0x00800000

ROW_TILE = 256
KEY_CHUNK = 1024
NEAR_BLOCKS = 8
FOLD_CHAINS = 8
COUNT_ROWS = 512
CAND = 32
STREAMS = 2
NT_DIMS = (((1,), (1,)), ((), ()))


def _t5_bucket(dist):
    n = jnp.maximum(dist, 0)
    max_exact = NUM_BUCKETS // 2
    nf = jnp.maximum(n, 1).astype(F32)
    large = max_exact + (jnp.log(nf / max_exact) / math.log(REL_MAX_DIST / max_exact)
                         * (NUM_BUCKETS - max_exact)).astype(I32)
    large = jnp.minimum(large, NUM_BUCKETS - 1)
    return jnp.where(n < max_exact, n, large)


def _silu(x):
    return x * jax.nn.sigmoid(x)


def _rms(x, g):
    ms = jnp.mean(x * x, axis=-1, keepdims=True)
    return x * lax.rsqrt(ms + EPS) * g


def _mm(a, b):
    return jnp.dot(a, b, preferred_element_type=F32)


def _mm_nt(a, b):
    return lax.dot_general(a, b, NT_DIMS, preferred_element_type=F32)


def _fold(x, op):
    n = x.shape[0] // SUBLANES
    chains = min(FOLD_CHAINS, n)
    accs = [x[r * SUBLANES:(r + 1) * SUBLANES] for r in range(chains)]
    for r in range(chains, n):
        accs[r % chains] = op(accs[r % chains], x[r * SUBLANES:(r + 1) * SUBLANES])
    while len(accs) > 1:
        accs = [op(a, b) for a, b in zip(accs[::2], accs[1::2])] + accs[len(accs) & ~1:]
    return accs[0]


def _params(*sem):
    return pltpu.CompilerParams(dimension_semantics=sem, vmem_limit_bytes=VMEM_LIMIT)


def _const_spec(shape):
    zeros = (0,) * len(shape)
    return pl.BlockSpec(shape, lambda *_: zeros)


def _proj0_kernel(x_ref, g_ref, w_ref, q_ref, k_ref, v_ref, sga_ref, u_ref, sgb_ref):
    hn = _rms(x_ref[...], g_ref[...]).astype(BF16)

    def mm(lo, hi):
        return _mm(hn, w_ref[:, lo:hi])

    q_ref[...] = mm(0, 512)
    k_ref[...] = mm(512, 640)
    v_ref[...] = mm(640, 768)
    sga_ref[...] = _silu(mm(768, 1280)).astype(BF16)
    u_ref[...] = mm(1280, 1792)
    sgb_ref[...] = _silu(mm(1792, 2304)).astype(BF16)


def _proj0(x2, g, w):
    rows, d = x2.shape
    t = ROW_TILE

    def row(n):
        return pl.BlockSpec((t, n), lambda i: (i, 0))

    return pl.pallas_call(
        _proj0_kernel,
        grid=(rows // t,),
        in_specs=[row(d), _const_spec((1, d)), _const_spec(w.shape)],
        out_specs=[row(512), row(128), row(128), row(512), row(512), row(512)],
        out_shape=[jax.ShapeDtypeStruct((rows, 512), F32), jax.ShapeDtypeStruct((rows, 128), F32),
                   jax.ShapeDtypeStruct((rows, 128), F32), jax.ShapeDtypeStruct((rows, 512), BF16),
                   jax.ShapeDtypeStruct((rows, 512), F32), jax.ShapeDtypeStruct((rows, 512), BF16)],
        compiler_params=_params("arbitrary"),
        name="proj0",
    )(x2, g, w)


def _attn0_kernel(q_ref, kc_ref, kp_ref, vc_ref, vp_ref, sga_ref, bias_ref, sink_ref, qg_ref, kg_ref, o_ref):
    i = pl.program_id(1)
    lane = lax.broadcasted_iota(I32, (1, LANES), 1)
    lo = lane < A_HEAD_DIM

    def segnorm(x, g2):
        sq = x * x
        s_lo = jnp.sum(jnp.where(lo, sq, 0.0), axis=-1, keepdims=True)
        s_hi = jnp.sum(jnp.where(lo, 0.0, sq), axis=-1, keepdims=True)
        inv = jnp.where(lo, lax.rsqrt(s_lo / A_HEAD_DIM + EPS), lax.rsqrt(s_hi / A_HEAD_DIM + EPS))
        return x * inv * g2

    kn = segnorm(jnp.concatenate([kp_ref[0], kc_ref[0]], axis=0), kg_ref[...])
    vb = jnp.concatenate([vp_ref[0], vc_ref[0]], axis=0)
    kr = pltpu.roll(kn, A_HEAD_DIM, axis=1)
    vr = pltpu.roll(vb, A_HEAD_DIM, axis=1)

    def variants(x, xr):
        return {(0, 0): jnp.where(lo, x, 0.0).astype(BF16), (0, 1): jnp.where(lo, 0.0, xr).astype(BF16),
                (1, 0): jnp.where(lo, xr, 0.0).astype(BF16), (1, 1): jnp.where(lo, 0.0, x).astype(BF16)}

    kvar = variants(kn, kr)
    vvar = variants(vb, vr)

    row = lax.broadcasted_iota(I32, (BLOCK, 2 * BLOCK), 0)
    col = lax.broadcasted_iota(I32, (BLOCK, 2 * BLOCK), 1)
    d = row + BLOCK - col
    mask = (d >= 0) & (d < WINDOW) & ((i > 0) | (col >= BLOCK))

    lgs, sinks = [], []
    for p in range(A_HEADS // 2):
        qp = (segnorm(q_ref[0, :, p * LANES:(p + 1) * LANES], qg_ref[...]) * (A_HEAD_DIM ** -0.5)).astype(BF16)
        for a in range(2):
            h = 2 * p + a
            lgs.append(jnp.where(mask, _mm_nt(qp, kvar[(p // 2, a)]) + bias_ref[h], NEG_INF))
            sinks.append(jnp.broadcast_to(sink_ref[h:h + 1, 0:1], (BLOCK, 1)))
    lg = jnp.concatenate(lgs, axis=0)
    sink = jnp.concatenate(sinks, axis=0)
    m = jnp.maximum(jnp.max(lg, axis=-1, keepdims=True), sink)
    e = jnp.exp(lg - m)
    den = jnp.sum(e, axis=-1, keepdims=True) + jnp.exp(sink - m)
    pr = (e / den).astype(BF16)
    for p in range(A_HEADS // 2):
        sl = slice(p * LANES, (p + 1) * LANES)
        acc = jnp.zeros((BLOCK, LANES), F32)
        for a in range(2):
            h = 2 * p + a
            acc = acc + _mm(pr[h * BLOCK:(h + 1) * BLOCK], vvar[(p // 2, a)])
        o_ref[0, :, sl] = (acc * sga_ref[0, :, sl].astype(F32)).astype(BF16)


def _attn0(q, k, v, sga, bias0, sinks, qg2, kg2):
    b, l, _ = q.shape
    nb = l // BLOCK

    def cur(n):
        return pl.BlockSpec((1, BLOCK, n), lambda bb, i: (bb, i, 0))

    def prev(n):
        return pl.BlockSpec((1, BLOCK, n), lambda bb, i: (bb, jnp.maximum(i - 1, 0), 0))

    return pl.pallas_call(
        _attn0_kernel,
        grid=(b, nb),
        in_specs=[cur(512), cur(128), prev(128), cur(128), prev(128), cur(512),
                  _const_spec(bias0.shape), _const_spec(sinks.shape), _const_spec(qg2.shape), _const_spec(kg2.shape)],
        out_specs=cur(512),
        out_shape=jax.ShapeDtypeStruct((b, l, 512), BF16),
        compiler_params=_params("arbitrary", "arbitrary"),
        name="attn0",
    )(q, k, k, v, v, sga, bias0, sinks, qg2, kg2)


def _ssm_kernel(u_ref, sgb_ref, bmat_ref, cre_ref, cim_ref, sc_ref, d_ref, gw_ref, gb_ref, o_ref, xre_ref, xim_ref):
    t = u_ref.shape[1]
    nq = bmat_ref.shape[0]
    half = bmat_ref.shape[2] // 2

    @pl.when(pl.program_id(1) == 0)
    def _():
        xre_ref[0:SUBLANES, :] = jnp.zeros((SUBLANES, xre_ref.shape[1]), F32)
        xim_ref[0:SUBLANES, :] = jnp.zeros((SUBLANES, xim_ref.shape[1]), F32)

    u = u_ref[0]
    ub = u.astype(BF16)
    for q in range(nq):
        bu = _mm(ub[:, q * LANES:(q + 1) * LANES], bmat_ref[q])
        xre_ref[SUBLANES:, q * half:(q + 1) * half] = bu[:, :half]
        xim_ref[SUBLANES:, q * half:(q + 1) * half] = bu[:, half:]

    def scan(r, _):
        base = pl.multiple_of(SUBLANES + r * SUBLANES, SUBLANES)
        xr = xre_ref[pl.ds(base, SUBLANES), :]
        xi = xim_ref[pl.ds(base, SUBLANES), :]
        for s, k in enumerate((1, 2, 4)):
            ar = sc_ref[2 * s]
            ai = sc_ref[2 * s + 1]
            sr = pltpu.roll(xr, k, axis=0)
            si = pltpu.roll(xi, k, axis=0)
            xr, xi = xr + ar * sr - ai * si, xi + ar * si + ai * sr
        cr = xre_ref[pl.ds(base - 1, 1), :]
        ci = xim_ref[pl.ds(base - 1, 1), :]
        pr = sc_ref[6]
        pi = sc_ref[7]
        xre_ref[pl.ds(base, SUBLANES), :] = xr + pr * cr - pi * ci
        xim_ref[pl.ds(base, SUBLANES), :] = xi + pr * ci + pi * cr
        return 0

    lax.fori_loop(0, t // SUBLANES, scan, 0, unroll=2)
    xre_ref[0:SUBLANES, :] = xre_ref[t:t + SUBLANES, :]
    xim_ref[0:SUBLANES, :] = xim_ref[t:t + SUBLANES, :]

    ys = []
    for q in range(nq):
        xr = xre_ref[SUBLANES:, q * half:(q + 1) * half].astype(BF16)
        xi = xim_ref[SUBLANES:, q * half:(q + 1) * half].astype(BF16)
        ys.append(_mm(xr, cre_ref[q]) + _mm(xi, cim_ref[q]))
    y = jnp.concatenate(ys, axis=1) + d_ref[...] * u
    y = jax.nn.gelu(y).astype(BF16)
    hh = _mm(y, gw_ref[...]) + gb_ref[...]
    w = hh.shape[1] // 2
    o_ref[0] = (hh[:, :w] * jax.nn.sigmoid(hh[:, w:]) * sgb_ref[0].astype(F32)).astype(BF16)


def _ssm(u, sgb, bmat, cre, cim, sc, dskip, gw, gb):
    b, l, w = u.shape
    t = ROW_TILE
    ns = sc.shape[-1]

    def row(n):
        return pl.BlockSpec((1, t, n), lambda bb, i: (bb, i, 0))

    return pl.pallas_call(
        _ssm_kernel,
        grid=(b, l // t),
        in_specs=[row(w), row(w), _const_spec(bmat.shape), _const_spec(cre.shape), _const_spec(cim.shape),
                  _const_spec(sc.shape), _const_spec(dskip.shape), _const_spec(gw.shape), _const_spec(gb.shape)],
        out_specs=row(w),
        out_shape=jax.ShapeDtypeStruct((b, l, w), BF16),
        scratch_shapes=[pltpu.VMEM((SUBLANES + t, ns), F32), pltpu.VMEM((SUBLANES + t, ns), F32)],
        compiler_params=_params("arbitrary", "arbitrary"),
        name="ssm",
    )(u, sgb, bmat, cre, cim, sc, dskip, gw, gb)


def _s5_prep(log_dt, a_re, a_im, b_re, b_im, c_re, c_im):
    g, p = a_re.shape
    h = b_re.shape[-1]
    gl = LANES // h
    nq = g // gl
    dt = jnp.exp(log_dt)[:, None]
    mag = jnp.exp(a_re * dt)
    ang = a_im * dt
    ab_re = mag * jnp.cos(ang)
    ab_im = mag * jnp.sin(ang)
    den = a_re * a_re + a_im * a_im
    n_re = ab_re - 1.0
    n_im = ab_im
    f_re = (n_re * a_re + n_im * a_im) / den
    f_im = (n_im * a_re - n_re * a_im) / den
    bb_re = f_re[..., None] * b_re - f_im[..., None] * b_im
    bb_im = f_re[..., None] * b_im + f_im[..., None] * b_re
    eye = jnp.eye(gl, dtype=F32)

    def bdiag_in(m):
        m = m.reshape(nq, gl, p, h)
        return jnp.einsum('qgph,gk->qghkp', m, eye).reshape(nq, gl * h, gl * p)

    def bdiag_out(m):
        m = m.reshape(nq, gl, h, p)
        return jnp.einsum('qghp,gk->qgpkh', m, eye).reshape(nq, gl * p, gl * h)

    bmat = jnp.concatenate([bdiag_in(bb_re), bdiag_in(bb_im)], axis=2).astype(BF16)
    cre = bdiag_out(c_re).astype(BF16)
    cim = bdiag_out(-c_im).astype(BF16)

    pw = [(ab_re.reshape(-1), ab_im.reshape(-1))]
    for _ in range(SUBLANES - 1):
        pr, pi = pw[-1]
        pw.append((pr * pw[0][0] - pi * pw[0][1], pr * pw[0][1] + pi * pw[0][0]))
    rows = jnp.arange(SUBLANES)[:, None]
    sc = []
    for k in (1, 2, 4):
        sc.append(jnp.where(rows >= k, pw[k - 1][0][None, :], 0.0))
        sc.append(jnp.where(rows >= k, pw[k - 1][1][None, :], 0.0))
    sc.append(jnp.stack([pw[r][0] for r in range(SUBLANES)]))
    sc.append(jnp.stack([pw[r][1] for r in range(SUBLANES)]))
    return bmat, cre, cim, jnp.stack(sc).astype(F32)


def _mid_kernel(x_ref, a_ref, s_ref, wo_ref, g_ref, wq_ref, wk_ref, wvt_ref, wg_ref, wqi_ref, wki_ref, wwt_ref,
                qg_ref, kg_ref, h_ref, q_ref, k_ref, vt_ref, sg_ref, qi_ref, ki_ref, wt_ref):
    aw = a_ref.shape[2]
    h = x_ref[0] + _mm(a_ref[0], wo_ref[0:aw, :]) + _mm(s_ref[0], wo_ref[aw:, :])
    h_ref[0] = h
    hn = _rms(h, g_ref[...]).astype(BF16)
    qf = _mm(hn, wq_ref[...])
    for hd in range(C_HEADS):
        sl = slice(hd * C_HEAD_DIM, (hd + 1) * C_HEAD_DIM)
        q_ref[0, :, sl] = (_rms(qf[:, sl], qg_ref[...]) * (C_HEAD_DIM ** -0.5)).astype(BF16)
    kf = _mm(hn, wk_ref[...])
    for hd in range(C_KV_HEADS):
        sl = slice(hd * C_HEAD_DIM, (hd + 1) * C_HEAD_DIM)
        k_ref[0, :, sl] = _rms(kf[:, sl], kg_ref[...]).astype(BF16)
    vt_ref[0] = _mm_nt(wvt_ref[...], hn).astype(BF16)
    sg_ref[0] = _silu(_mm(hn, wg_ref[...])).astype(BF16)
    qi_ref[0] = _mm(hn, wqi_ref[...]).astype(BF16)
    ki_ref[0] = _mm(hn, wki_ref[...]).astype(BF16)
    wt_ref[0] = _mm_nt(wwt_ref[...], hn) * ((IDX_HEADS ** -0.5) * (IDX_DIM ** -0.5))


def _mid(x, att0, ssm0, wo, g, wq, wk, wvt, wg, wqi, wki2, wwt, qg, kg):
    b, l, d = x.shape
    t = ROW_TILE

    def row(n):
        return pl.BlockSpec((1, t, n), lambda bb, i: (bb, i, 0))

    def col(n):
        return pl.BlockSpec((1, n, t), lambda bb, i: (bb, 0, i))

    weights = [wo, g, wq, wk, wvt, wg, wqi, wki2, wwt, qg, kg]
    cw = C_HEADS * C_HEAD_DIM
    ckv = C_KV_HEADS * C_HEAD_DIM
    return pl.pallas_call(
        _mid_kernel,
        grid=(b, l // t),
        in_specs=[row(d), row(att0.shape[2]), row(ssm0.shape[2])] + [_const_spec(w.shape) for w in weights],
        out_specs=[row(d), row(cw), row(ckv), col(ckv), row(cw), row(IDX_HEADS * IDX_DIM), row(2 * LANES),
                   col(IDX_HEADS)],
        out_shape=[jax.ShapeDtypeStruct((b, l, d), F32), jax.ShapeDtypeStruct((b, l, cw), BF16),
                   jax.ShapeDtypeStruct((b, l, ckv), BF16), jax.ShapeDtypeStruct((b, ckv, l), BF16),
                   jax.ShapeDtypeStruct((b, l, cw), BF16), jax.ShapeDtypeStruct((b, l, IDX_HEADS * IDX_DIM), BF16),
                   jax.ShapeDtypeStruct((b, l, 2 * LANES), BF16), jax.ShapeDtypeStruct((b, IDX_HEADS, l), F32)],
        compiler_params=_params("arbitrary", "arbitrary"),
        name="mid",
    )(x, att0, ssm0, *weights)


def _dsa_kernel(q_ref, qi_ref, wt_ref, sg_ref, k_ref, vt_ref, ki_ref, tab_ref, o_ref,
                sc_ref, best_ref, x_ref, acc_ref, *, seq_len, topk):
    i = pl.program_id(1)
    ck = KEY_CHUNK
    per = ck // BLOCK
    nch = (i + per) // per
    t_row = i * BLOCK + lax.broadcasted_iota(I32, (1, LANES), 1)
    kiota = lax.broadcasted_iota(I32, (ck, LANES), 0)

    def chunk_off(c):
        return pl.multiple_of(c * ck, ck)

    qi = qi_ref[0]
    qi_stack = [jnp.concatenate([qi[:, (2 * s) * LANES:(2 * s + 1) * LANES],
                                 qi[:, (2 * s + 1) * LANES:(2 * s + 2) * LANES]], axis=0) for s in range(2)]
    wt = wt_ref[0]

    def score_chunk(c, masked):
        off = chunk_off(c)
        sc = jnp.zeros((ck, LANES), F32)
        for a in range(2):
            kk = ki_ref[0, pl.ds(off, ck), a * LANES:(a + 1) * LANES]
            for s in range(2):
                r = _mm_nt(kk, qi_stack[s])
                for j in range(2):
                    hd = 2 * (2 * s + j) + a
                    sc = sc + jnp.maximum(r[:, j * LANES:(j + 1) * LANES], 0.0) * wt[hd:hd + 1, :]
        if masked:
            sc = jnp.where(off + kiota <= t_row, sc, NEG_INF)
        sc_ref[pl.ds(off, ck), :] = sc
        return _fold(sc, jnp.maximum)

    def score_body(c, mx):
        return jnp.maximum(mx, score_chunk(c, False))

    smax = lax.fori_loop(0, nch - 1, score_body, jnp.full((SUBLANES, LANES), NEG_INF, F32))
    smax = jnp.max(jnp.maximum(smax, score_chunk(nch - 1, True)), axis=0, keepdims=True)

    def count(*preds):
        rows = COUNT_ROWS
        sub = ck // rows

        def body(c, accs):
            out = []
            for u in range(sub):
                off = pl.multiple_of(c * ck + u * rows, rows)
                s = sc_ref[pl.ds(off, rows), :]
                for n, pred in enumerate(preds):
                    ind = pred(s, off).astype(I32)
                    out.append(accs[u * len(preds) + n]
                               + jnp.sum(ind.reshape(rows // SUBLANES, SUBLANES, LANES), axis=0))
            return tuple(out)

        accs = lax.fori_loop(0, nch, body, tuple(jnp.zeros((SUBLANES, LANES), I32) for _ in range(sub * len(preds))))
        res = [jnp.sum(sum(accs[n::len(preds)]), axis=0, keepdims=True) for n in range(len(preds))]
        return res[0] if len(preds) == 1 else res

    def key_value(k):
        return pltpu.bitcast(jnp.where(k < 0, INT_MIN - k, k), F32)

    def count_ge(k):
        thr = key_value(k)
        return count(lambda s, off: s >= thr)

    def full(v):
        return jnp.full((1, LANES), v, I32)

    def bisect(_, st):
        lo, hi, c_lo, c_hi = st
        mid = (lo >> 1) + (hi >> 1) + (lo & hi & 1)
        c = count_ge(mid)
        ge = c >= topk
        return jnp.where(ge, mid, lo), jnp.where(ge, hi, mid), jnp.where(ge, c, c_lo), jnp.where(ge, c_hi, c)

    searching = (i + 1) * BLOCK > topk

    def float_key(x):
        bits = pltpu.bitcast(x, I32)
        return jnp.where(bits < 0, INT_MIN - bits, bits)

    def search():
        k_lo = float_key(smax * 0.125)
        c = count_ge(k_lo)
        ok = (smax > 0.0) & (c >= topk)
        trips = jnp.where(jnp.min(jnp.where(ok, 1, 0)) > 0, 25, 32)
        st = (jnp.where(ok, k_lo, KEY_NEG_INF), float_key(smax) + 1, jnp.where(ok, c, nch * ck), full(0))
        out = lax.fori_loop(0, trips, bisect, st)
        return out[0], out[2], out[3]

    def exchange(v, a, b):
        v[a], v[b] = jnp.maximum(v[a], v[b]), jnp.minimum(v[a], v[b])

    def sort_desc(v):
        n, k = len(v), 2
        while k <= n:
            j = k // 2
            while j >= 1:
                for a in range(n):
                    b = a ^ j
                    if b > a:
                        exchange(v, *((a, b) if (a & k) == 0 else (b, a)))
                j //= 2
            k *= 2

    def merge_top(best, blk):
        n = len(best)
        v = [jnp.maximum(best[r], blk[n - 1 - r]) for r in range(n)]
        j = n // 2
        while j >= 1:
            for a in range(n):
                if a ^ j > a:
                    exchange(v, a, a ^ j)
            j //= 2
        return v

    crow = STREAMS * CAND * SUBLANES

    def cand_body(c, _):
        blk_all = sc_ref[pl.ds(pl.multiple_of(c * crow, crow), crow), :]
        for st in range(STREAMS):
            blk = [blk_all[(STREAMS * r + st) * SUBLANES:(STREAMS * r + st + 1) * SUBLANES] for r in range(CAND)]
            sort_desc(blk)
            base = st * CAND * SUBLANES
            best = [best_ref[base + r * SUBLANES:base + (r + 1) * SUBLANES, :] for r in range(CAND)]
            for r, x in enumerate(merge_top(best, blk)):
                best_ref[base + r * SUBLANES:base + (r + 1) * SUBLANES, :] = x
        return 0

    def cand_search():
        best_ref[...] = jnp.full(best_ref.shape, -jnp.inf, F32)
        lax.fori_loop(0, nch * (ck // crow), cand_body, 0)

        def count_cand(k):
            thr = key_value(k)
            parts = [jnp.sum((best_ref[r:r + COUNT_ROWS, :] >= thr).astype(I32)
                             .reshape(COUNT_ROWS // SUBLANES, SUBLANES, LANES), axis=0)
                     for r in range(0, crow, COUNT_ROWS)]
            return jnp.sum(sum(parts), axis=0, keepdims=True)

        def step(_, st):
            lo, hi = st
            mid = (lo >> 1) + (hi >> 1) + (lo & hi & 1)
            ge = count_cand(mid) >= topk
            return jnp.where(ge, mid, lo), jnp.where(ge, hi, mid)

        k_lo = float_key(smax * 0.125)
        ok = (smax > 0.0) & (count_cand(k_lo) >= topk)
        trips = jnp.where(jnp.min(jnp.where(ok, 1, 0)) > 0, 25, 32)
        vk, _ = lax.fori_loop(0, trips, step, (jnp.where(ok, k_lo, KEY_NEG_INF), float_key(smax) + 1))
        thr = key_value(vk)
        above_cand = jnp.sum((best_ref[...] > thr).astype(I32), axis=0, keepdims=True)
        c_ge, c_gt = count(lambda s, off: s >= thr, lambda s, off: s > thr)
        complete = jnp.min(jnp.where(c_gt == above_cand, 1, 0)) > 0
        return lax.cond(complete, lambda: (vk, c_ge, c_gt), search)

    vkey, c_lo, c_hi = lax.cond(searching, cand_search, lambda: (full(KEY_NEG_INF), full(topk), full(0)))
    vthr = key_value(vkey)
    need = topk - c_hi
    ties = c_lo - c_hi

    def tie_search():
        nxt = vkey + 1
        nxt = jnp.where((nxt > 0) & (nxt < KEY_MIN_NORMAL), KEY_MIN_NORMAL, nxt)
        step = key_value(nxt) - vthr

        def split(_, st):
            fl, fh = st
            fm = 0.5 * (fl + fh)
            thr = vthr + fm * step
            ge = count(lambda s, off: s >= thr) >= topk
            return jnp.where(ge, fm, fl), jnp.where(ge, fh, fm)

        fl, _ = lax.fori_loop(0, 26, split, (jnp.zeros((1, LANES), F32), jnp.ones((1, LANES), F32)))
        thr = vthr + fl * step
        want = topk - count(lambda s, off: s > thr)

        def body(_, st):
            lj, hj = st
            mid = (lj + hj) >> 1
            c = count(lambda s, off: (s == thr) & (off + kiota[:COUNT_ROWS] <= mid))
            ok = c >= want
            return jnp.where(ok, lj, mid), jnp.where(ok, mid, hj)

        _, hj = lax.fori_loop(0, 14, body, (full(-1), full(0) + (nch * ck - 1)))
        return thr, hj

    any_tie = searching & (jnp.max(ties - need) > 0)
    vthr, jmax = lax.cond(any_tie, tie_search, lambda: (vthr, full(seq_len)))

    def selection_mask(off):
        s = sc_ref[pl.ds(off, ck), :]
        s_idx = off + kiota
        sel = ((s > vthr) | ((s == vthr) & (s_idx <= jmax))) & (s_idx <= t_row)
        madd = jnp.where(sel, 0.0, NEG_INF)
        sc_ref[pl.ds(off, ck), :] = madd
        return madd

    q = q_ref[0]
    n_far = jnp.maximum((i - NEAR_BLOCKS + 1) // per, 0)
    hpg = C_HEADS // C_KV_HEADS

    for g in range(C_KV_HEADS):
        q_stack = [jnp.concatenate([q[:, (hpg * g + 2 * jj) * LANES:(hpg * g + 2 * jj + 1) * LANES],
                                    q[:, (hpg * g + 2 * jj + 1) * LANES:(hpg * g + 2 * jj + 2) * LANES]], axis=0)
                   for jj in range(hpg // 2)]

        def stage_body(near, g=g, q_stack=q_stack):
            def body(c, mx):
                off = chunk_off(c)
                madd = selection_mask(off) if g == 0 else sc_ref[pl.ds(off, ck), :]
                kc = k_ref[0, pl.ds(off, ck), g * LANES:(g + 1) * LANES]
                tidx = [jnp.clip(i - (c * per + r), 0, NEAR_BLOCKS) for r in range(per)]
                out = []
                for jj in range(hpg // 2):
                    lg = _mm_nt(kc, q_stack[jj])
                    for a in range(2):
                        hl = 2 * jj + a
                        x = lg[:, a * LANES:(a + 1) * LANES] + madd
                        if near:
                            x = x + jnp.concatenate([tab_ref[hpg * g + hl, tidx[r]] for r in range(per)], axis=0)
                        x_ref[hl, pl.ds(off, ck), :] = x
                        out.append(jnp.maximum(mx[hl], _fold(x, jnp.maximum)))
                return tuple(out)

            return body

        mx = tuple(jnp.full((SUBLANES, LANES), NEG_INF, F32) for _ in range(hpg))
        mx = lax.fori_loop(0, n_far, stage_body(False), mx)
        mx = lax.fori_loop(n_far, nch, stage_body(True), mx)
        m = [jnp.max(v, axis=0, keepdims=True) for v in mx]
        acc_ref[...] = jnp.zeros(acc_ref.shape, F32)

        def att_body(c, ls, g=g, m=m):
            off = chunk_off(c)
            vt = vt_ref[0, g * LANES:(g + 1) * LANES, pl.ds(off, ck)]
            out = []
            for jj in range(hpg // 2):
                ps = []
                for a in range(2):
                    hl = 2 * jj + a
                    p = jnp.exp(x_ref[hl, pl.ds(off, ck), :] - m[hl])
                    out.append(ls[hl] + _fold(p, jnp.add))
                    ps.append(p.astype(BF16))
                acc_ref[jj] += _mm(vt, jnp.concatenate(ps, axis=1))
            return tuple(out)

        ls = lax.fori_loop(0, nch, att_body, tuple(jnp.zeros((SUBLANES, LANES), F32) for _ in range(hpg)))
        for hl in range(hpg):
            sl = slice((hpg * g + hl) * LANES, (hpg * g + hl + 1) * LANES)
            ot = acc_ref[hl // 2, :, (hl % 2) * LANES:(hl % 2 + 1) * LANES] / jnp.sum(ls[hl], axis=0, keepdims=True)
            o_ref[0, :, sl] = (ot.T * sg_ref[0, :, sl].astype(F32)).astype(BF16)


def _dsa(q, qi, wt, sg, k, vt, ki2, tab):
    b, l, cw = q.shape
    nb = l // BLOCK
    topk = min(TOPK_MAX, l // 4)

    def blk(n):
        return pl.BlockSpec((1, BLOCK, n), lambda bb, i: (bb, i, 0))

    def whole(s1, s2):
        return pl.BlockSpec((1, s1, s2), lambda bb, i: (bb, 0, 0), pipeline_mode=pl.Buffered(1))

    hpg = C_HEADS // C_KV_HEADS
    return pl.pallas_call(
        functools.partial(_dsa_kernel, seq_len=l, topk=topk),
        grid=(b, nb),
        in_specs=[blk(cw), blk(qi.shape[2]), pl.BlockSpec((1, IDX_HEADS, BLOCK), lambda bb, i: (bb, 0, i)), blk(cw),
                  whole(l, k.shape[2]), whole(vt.shape[1], l), whole(l, ki2.shape[2]),
                  pl.BlockSpec(tab.shape, lambda bb, i: (0, 0, 0, 0), pipeline_mode=pl.Buffered(1))],
        out_specs=blk(cw),
        out_shape=jax.ShapeDtypeStruct((b, l, cw), BF16),
        scratch_shapes=[pltpu.VMEM((l, LANES), F32), pltpu.VMEM((STREAMS * CAND * SUBLANES, LANES), F32),
                        pltpu.VMEM((hpg, l, LANES), F32),
                        pltpu.VMEM((hpg // 2, C_HEAD_DIM, 2 * LANES), F32)],
        compiler_params=_params("arbitrary", "arbitrary"),
        name="dsa",
    )(q, qi, wt, sg, k, vt, ki2, tab)


def _out_kernel(h_ref, a_ref, w_ref, o_ref):
    o_ref[...] = h_ref[...] + _mm(a_ref[...], w_ref[...])


def _outproj(h2, a2, w):
    rows, d = h2.shape
    t = ROW_TILE
    return pl.pallas_call(
        _out_kernel,
        grid=(rows // t,),
        in_specs=[pl.BlockSpec((t, d), lambda i: (i, 0)), pl.BlockSpec((t, a2.shape[1]), lambda i: (i, 0)),
                  _const_spec(w.shape)],
        out_specs=pl.BlockSpec((t, d), lambda i: (i, 0)),
        out_shape=jax.ShapeDtypeStruct((rows, d), F32),
        compiler_params=_params("arbitrary"),
        name="outproj1",
    )(h2, a2, w)


def _bias_tables(rel_bias, seq_len):
    del seq_len
    nv = (NEAR_BLOCKS + 1) * BLOCK
    vec = rel_bias[_t5_bucket(jnp.arange(nv, dtype=I32))].astype(F32).T

    def window(lo, n):
        pad = max(0, -lo)
        body = vec[:, max(lo, 0):lo + n]
        return jnp.concatenate([jnp.broadcast_to(vec[:, :1], (vec.shape[0], pad)), body], axis=1)

    def toeplitz(g, rows, cols):
        w = rows + cols
        g2 = jnp.concatenate([g[:, rows - 1:rows - 1 + cols], g[:, :1], g[:, :rows - 1]], axis=1)
        flat = jnp.tile(g2, (1, rows))[:, :rows * (w - 1)]
        return flat.reshape(-1, rows, w - 1)[:, :, :cols]

    bias0 = jnp.transpose(toeplitz(window(BLOCK - (2 * BLOCK - 1), 3 * BLOCK - 1), 2 * BLOCK, BLOCK), (0, 2, 1))
    tiles = [toeplitz(window(dl * BLOCK - (BLOCK - 1), 2 * BLOCK - 1), BLOCK, BLOCK) for dl in range(NEAR_BLOCKS)]
    tab = jnp.stack(tiles, axis=1) - rel_bias[NUM_BUCKETS - 1].astype(F32)[:, None, None, None]
    tab = jnp.concatenate([tab, jnp.zeros((tab.shape[0], 1, BLOCK, BLOCK), F32)], axis=1)
    return bias0, tab


def kernel(x, rel_bias, norm_g, ev_w_in, ev_w_out, ev_q_norm_g, ev_k_norm_g, ev_sinks, ev_ssm_log_dt, ev_ssm_a_re,
           ev_ssm_a_im, ev_ssm_b_re, ev_ssm_b_im, ev_ssm_c_re, ev_ssm_c_im, ev_ssm_d, ev_glu_w, ev_glu_b, od_w_in,
           od_w_out, od_q_norm_g, od_k_norm_g):
    b, l, d = x.shape
    assert l % KEY_CHUNK == 0 and l % ROW_TILE == 0
    assert (NEAR_BLOCKS - 1) * BLOCK + 1 >= 16 * 64 ** (15 / 16) + 1
    bias0, tab = _bias_tables(rel_bias, l)

    q0, k0, v0, sga, u, sgb = _proj0(x.reshape(b * l, d), norm_g[0][None, :], ev_w_in[0].astype(BF16))
    shp = lambda a: a.reshape(b, l, a.shape[-1])
    qg2 = jnp.tile(ev_q_norm_g[0], 2)[None, :]
    kg2 = jnp.tile(ev_k_norm_g[0], 2)[None, :]
    sinks = jnp.broadcast_to(ev_sinks[0][:, None], (A_HEADS, LANES)).astype(F32)
    att0 = _attn0(shp(q0), shp(k0), shp(v0), shp(sga), bias0, sinks, qg2, kg2)
    bmat, cre, cim, sc = _s5_prep(ev_ssm_log_dt[0], ev_ssm_a_re[0], ev_ssm_a_im[0], ev_ssm_b_re[0], ev_ssm_b_im[0],
                                  ev_ssm_c_re[0], ev_ssm_c_im[0])
    ssm0 = _ssm(shp(u), shp(sgb), bmat, cre, cim, sc, ev_ssm_d[0].reshape(1, -1), ev_glu_w[0].astype(BF16),
                ev_glu_b[0][None, :])

    w1 = od_w_in[0]
    cw = C_HEADS * C_HEAD_DIM
    ckv = C_KV_HEADS * C_HEAD_DIM
    o = np.cumsum([0, cw, ckv, ckv, cw, IDX_HEADS * IDX_DIM, IDX_DIM, IDX_HEADS])
    wq, wk, wv, wg, wqi, wki, ww = (w1[:, o[n]:o[n + 1]] for n in range(7))
    zki = jnp.zeros((d, LANES - IDX_DIM), w1.dtype)
    wki2 = jnp.concatenate([wki, zki, zki, wki], axis=1)
    bf = lambda a: a.astype(BF16)
    h1, q1, k1, vt1, sg1, qi1, ki2, wt1 = _mid(
        x, att0, ssm0, bf(ev_w_out[0]), norm_g[1][None, :], bf(wq), bf(wk), bf(wv.T), bf(wg), bf(wqi), bf(wki2),
        bf(ww.T), od_q_norm_g[0][None, :], od_k_norm_g[0][None, :])
    att1 = _dsa(q1, qi1, wt1, sg1, k1, vt1, ki2, tab)
    out = _outproj(h1.reshape(b * l, d), att1.reshape(b * l, cw), bf(od_w_out[0]))
    return out.reshape(b, l, d)
```

```python
import functools
import math

import jax
import jax.numpy as jnp
import numpy as np
from jax import lax
from jax.experimental import pallas as pl
from jax.experimental.pallas import tpu as pltpu

F32 = jnp.float32
BF16 = jnp.bfloat16
I32 = jnp.int32

LANES = 128
SUBLANES = 8
VMEM_LIMIT = 56 * 1024 * 1024

BLOCK = 128
WINDOW = 128
A_HEADS = 8
A_HEAD_DIM = 64
A_KV_HEADS = 2
A_WIDTH = A_HEADS * A_HEAD_DIM
SSM_GROUP = 16
SSM_STATE = 64
C_HEADS = 8
C_HEAD_DIM = 128
C_KV_HEADS = 2
IDX_HEADS = 8
IDX_DIM = 64
TOPK_MAX = 256
NUM_BUCKETS = 32
REL_MAX_DIST = 1024
EPS = 1e-6
NEG_INF = -1e30
INT_MIN = -(2 ** 31)
KEY_MIN_NORMAL = 0x00800000
KEY_POS_INF = 0x7F800000
KEY_NEG_INF = INT_MIN + 0x00800000

ROW_TILE = 256
KEY_CHUNK = 1024
NEAR_BLOCKS = 8
FOLD_CHAINS = 8
COUNT_ROWS = 512
CAND = 64
STREAMS = 1
NT_DIMS = (((1,), (1,)), ((), ()))


def _t5_bucket(dist):
    n = jnp.maximum(dist, 0)
    max_exact = NUM_BUCKETS // 2
    nf = jnp.maximum(n, 1).astype(F32)
    large = max_exact + (jnp.log(nf / max_exact) / math.log(REL_MAX_DIST / max_exact)
                         * (NUM_BUCKETS - max_exact)).astype(I32)
    large = jnp.minimum(large, NUM_BUCKETS - 1)
    return jnp.where(n < max_exact, n, large)


def _silu(x):
    return x * jax.nn.sigmoid(x)


def _rms(x, g):
    ms = jnp.mean(x * x, axis=-1, keepdims=True)
    return x * lax.rsqrt(ms + EPS) * g


def _mm(a, b):
    return jnp.dot(a, b, preferred_element_type=F32)


def _mm_nt(a, b):
    return lax.dot_general(a, b, NT_DIMS, preferred_element_type=F32)


def _fold(x, op):
    n = x.shape[0] // SUBLANES
    chains = min(FOLD_CHAINS, n)
    accs = [x[r * SUBLANES:(r + 1) * SUBLANES] for r in range(chains)]
    for r in range(chains, n):
        accs[r % chains] = op(accs[r % chains], x[r * SUBLANES:(r + 1) * SUBLANES])
    while len(accs) > 1:
        accs = [op(a, b) for a, b in zip(accs[::2], accs[1::2])] + accs[len(accs) & ~1:]
    return accs[0]


def _params(*sem):
    return pltpu.CompilerParams(dimension_semantics=sem, vmem_limit_bytes=VMEM_LIMIT)


def _const_spec(shape):
    zeros = (0,) * len(shape)
    return pl.BlockSpec(shape, lambda *_: zeros)


def _proj0_kernel(x_ref, g_ref, w_ref, q_ref, k_ref, v_ref, sga_ref, u_ref, sgb_ref):
    hn = _rms(x_ref[...], g_ref[...]).astype(BF16)

    def mm(lo, hi):
        return _mm(hn, w_ref[:, lo:hi])

    q_ref[...] = mm(0, 512)
    k_ref[...] = mm(512, 640)
    v_ref[...] = mm(640, 768)
    sga_ref[...] = _silu(mm(768, 1280)).astype(BF16)
    u_ref[...] = mm(1280, 1792)
    sgb_ref[...] = _silu(mm(1792, 2304)).astype(BF16)


def _proj0(x2, g, w):
    rows, d = x2.shape
    t = ROW_TILE

    def row(n):
        return pl.BlockSpec((t, n), lambda i: (i, 0))

    return pl.pallas_call(
        _proj0_kernel,
        grid=(rows // t,),
        in_specs=[row(d), _const_spec((1, d)), _const_spec(w.shape)],
        out_specs=[row(512), row(128), row(128), row(512), row(512), row(512)],
        out_shape=[jax.ShapeDtypeStruct((rows, 512), F32), jax.ShapeDtypeStruct((rows, 128), F32),
                   jax.ShapeDtypeStruct((rows, 128), F32), jax.ShapeDtypeStruct((rows, 512), BF16),
                   jax.ShapeDtypeStruct((rows, 512), F32), jax.ShapeDtypeStruct((rows, 512), BF16)],
        compiler_params=_params("arbitrary"),
        name="proj0",
    )(x2, g, w)


def _attn0_kernel(q_ref, kc_ref, kp_ref, vc_ref, vp_ref, sga_ref, bias_ref, sink_ref, qg_ref, kg_ref, o_ref):
    i = pl.program_id(1)
    lane = lax.broadcasted_iota(I32, (1, LANES), 1)
    lo = lane < A_HEAD_DIM

    def segnorm(x, g2):
        sq = x * x
        s_lo = jnp.sum(jnp.where(lo, sq, 0.0), axis=-1, keepdims=True)
        s_hi = jnp.sum(jnp.where(lo, 0.0, sq), axis=-1, keepdims=True)
        inv = jnp.where(lo, lax.rsqrt(s_lo / A_HEAD_DIM + EPS), lax.rsqrt(s_hi / A_HEAD_DIM + EPS))
        return x * inv * g2

    kn = segnorm(jnp.concatenate([kp_ref[0], kc_ref[0]], axis=0), kg_ref[...])
    vb = jnp.concatenate([vp_ref[0], vc_ref[0]], axis=0)
    kr = pltpu.roll(kn, A_HEAD_DIM, axis=1)
    vr = pltpu.roll(vb, A_HEAD_DIM, axis=1)

    def variants(x, xr):
        return {(0, 0): jnp.where(lo, x, 0.0).astype(BF16), (0, 1): jnp.where(lo, 0.0, xr).astype(BF16),
                (1, 0): jnp.where(lo, xr, 0.0).astype(BF16), (1, 1): jnp.where(lo, 0.0, x).astype(BF16)}

    kvar = variants(kn, kr)
    vvar = variants(vb, vr)

    row = lax.broadcasted_iota(I32, (BLOCK, 2 * BLOCK), 0)
    col = lax.broadcasted_iota(I32, (BLOCK, 2 * BLOCK), 1)
    d = row + BLOCK - col
    mask = (d >= 0) & (d < WINDOW) & ((i > 0) | (col >= BLOCK))

    lgs, sinks = [], []
    for p in range(A_HEADS // 2):
        qp = (segnorm(q_ref[0, :, p * LANES:(p + 1) * LANES], qg_ref[...]) * (A_HEAD_DIM ** -0.5)).astype(BF16)
        for a in range(2):
            h = 2 * p + a
            lgs.append(jnp.where(mask, _mm_nt(qp, kvar[(p // 2, a)]) + bias_ref[h], NEG_INF))
            sinks.append(jnp.broadcast_to(sink_ref[h:h + 1, 0:1], (BLOCK, 1)))
    lg = jnp.concatenate(lgs, axis=0)
    sink = jnp.concatenate(sinks, axis=0)
    m = jnp.maximum(jnp.max(lg, axis=-1, keepdims=True), sink)
    e = jnp.exp(lg - m)
    den = jnp.sum(e, axis=-1, keepdims=True) + jnp.exp(sink - m)
    pr = (e / den).astype(BF16)
    for p in range(A_HEADS // 2):
        sl = slice(p * LANES, (p + 1) * LANES)
        acc = jnp.zeros((BLOCK, LANES), F32)
        for a in range(2):
            h = 2 * p + a
            acc = acc + _mm(pr[h * BLOCK:(h + 1) * BLOCK], vvar[(p // 2, a)])
        o_ref[0, :, sl] = (acc * sga_ref[0, :, sl].astype(F32)).astype(BF16)


def _attn0(q, k, v, sga, bias0, sinks, qg2, kg2):
    b, l, _ = q.shape
    nb = l // BLOCK

    def cur(n):
        return pl.BlockSpec((1, BLOCK, n), lambda bb, i: (bb, i, 0))

    def prev(n):
        return pl.BlockSpec((1, BLOCK, n), lambda bb, i: (bb, jnp.maximum(i - 1, 0), 0))

    return pl.pallas_call(
        _attn0_kernel,
        grid=(b, nb),
        in_specs=[cur(512), cur(128), prev(128), cur(128), prev(128), cur(512),
                  _const_spec(bias0.shape), _const_spec(sinks.shape), _const_spec(qg2.shape), _const_spec(kg2.shape)],
        out_specs=cur(512),
        out_shape=jax.ShapeDtypeStruct((b, l, 512), BF16),
        compiler_params=_params("arbitrary", "arbitrary"),
        name="attn0",
    )(q, k, k, v, v, sga, bias0, sinks, qg2, kg2)


def _ssm_kernel(u_ref, sgb_ref, bmat_ref, cre_ref, cim_ref, sc_ref, d_ref, gw_ref, gb_ref, o_ref, xre_ref, xim_ref):
    t = u_ref.shape[1]
    nq = bmat_ref.shape[0]
    half = bmat_ref.shape[2] // 2

    @pl.when(pl.program_id(1) == 0)
    def _():
        xre_ref[0:SUBLANES, :] = jnp.zeros((SUBLANES, xre_ref.shape[1]), F32)
        xim_ref[0:SUBLANES, :] = jnp.zeros((SUBLANES, xim_ref.shape[1]), F32)

    u = u_ref[0]
    ub = u.astype(BF16)
    for q in range(nq):
        bu = _mm(ub[:, q * LANES:(q + 1) * LANES], bmat_ref[q])
        xre_ref[SUBLANES:, q * half:(q + 1) * half] = bu[:, :half]
        xim_ref[SUBLANES:, q * half:(q + 1) * half] = bu[:, half:]

    def scan(r, _):
        base = pl.multiple_of(SUBLANES + r * SUBLANES, SUBLANES)
        xr = xre_ref[pl.ds(base, SUBLANES), :]
        xi = xim_ref[pl.ds(base, SUBLANES), :]
        for s, k in enumerate((1, 2, 4)):
            ar = sc_ref[2 * s]
            ai = sc_ref[2 * s + 1]
            sr = pltpu.roll(xr, k, axis=0)
            si = pltpu.roll(xi, k, axis=0)
            xr, xi = xr + ar * sr - ai * si, xi + ar * si + ai * sr
        cr = xre_ref[pl.ds(base - 1, 1), :]
        ci = xim_ref[pl.ds(base - 1, 1), :]
        pr = sc_ref[6]
        pi = sc_ref[7]
        xre_ref[pl.ds(base, SUBLANES), :] = xr + pr * cr - pi * ci
        xim_ref[pl.ds(base, SUBLANES), :] = xi + pr * ci + pi * cr
        return 0

    lax.fori_loop(0, t // SUBLANES, scan, 0, unroll=2)
    xre_ref[0:SUBLANES, :] = xre_ref[t:t + SUBLANES, :]
    xim_ref[0:SUBLANES, :] = xim_ref[t:t + SUBLANES, :]

    ys = []
    for q in range(nq):
        xr = xre_ref[SUBLANES:, q * half:(q + 1) * half].astype(BF16)
        xi = xim_ref[SUBLANES:, q * half:(q + 1) * half].astype(BF16)
        ys.append(_mm(xr, cre_ref[q]) + _mm(xi, cim_ref[q]))
    y = jnp.concatenate(ys, axis=1) + d_ref[...] * u
    y = jax.nn.gelu(y).astype(BF16)
    hh = _mm(y, gw_ref[...]) + gb_ref[...]
    w = hh.shape[1] // 2
    o_ref[0] = (hh[:, :w] * jax.nn.sigmoid(hh[:, w:]) * sgb_ref[0].astype(F32)).astype(BF16)


def _ssm(u, sgb, bmat, cre, cim, sc, dskip, gw, gb):
    b, l, w = u.shape
    t = ROW_TILE
    ns = sc.shape[-1]

    def row(n):
        return pl.BlockSpec((1, t, n), lambda bb, i: (bb, i, 0))

    return pl.pallas_call(
        _ssm_kernel,
        grid=(b, l // t),
        in_specs=[row(w), row(w), _const_spec(bmat.shape), _const_spec(cre.shape), _const_spec(cim.shape),
                  _const_spec(sc.shape), _const_spec(dskip.shape), _const_spec(gw.shape), _const_spec(gb.shape)],
        out_specs=row(w),
        out_shape=jax.ShapeDtypeStruct((b, l, w), BF16),
        scratch_shapes=[pltpu.VMEM((SUBLANES + t, ns), F32), pltpu.VMEM((SUBLANES + t, ns), F32)],
        compiler_params=_params("arbitrary", "arbitrary"),
        name="ssm",
    )(u, sgb, bmat, cre, cim, sc, dskip, gw, gb)


def _s5_prep(log_dt, a_re, a_im, b_re, b_im, c_re, c_im):
    g, p = a_re.shape
    h = b_re.shape[-1]
    gl = LANES // h
    nq = g // gl
    dt = jnp.exp(log_dt)[:, None]
    mag = jnp.exp(a_re * dt)
    ang = a_im * dt
    ab_re = mag * jnp.cos(ang)
    ab_im = mag * jnp.sin(ang)
    den = a_re * a_re + a_im * a_im
    n_re = ab_re - 1.0
    n_im = ab_im
    f_re = (n_re * a_re + n_im * a_im) / den
    f_im = (n_im * a_re - n_re * a_im) / den
    bb_re = f_re[..., None] * b_re - f_im[..., None] * b_im
    bb_im = f_re[..., None] * b_im + f_im[..., None] * b_re
    eye = jnp.eye(gl, dtype=F32)

    def bdiag_in(m):
        m = m.reshape(nq, gl, p, h)
        return jnp.einsum('qgph,gk->qghkp', m, eye).reshape(nq, gl * h, gl * p)

    def bdiag_out(m):
        m = m.reshape(nq, gl, h, p)
        return jnp.einsum('qghp,gk->qgpkh', m, eye).reshape(nq, gl * p, gl * h)

    bmat = jnp.concatenate([bdiag_in(bb_re), bdiag_in(bb_im)], axis=2).astype(BF16)
    cre = bdiag_out(c_re).astype(BF16)
    cim = bdiag_out(-c_im).astype(BF16)

    pw = [(ab_re.reshape(-1), ab_im.reshape(-1))]
    for _ in range(SUBLANES - 1):
        pr, pi = pw[-1]
        pw.append((pr * pw[0][0] - pi * pw[0][1], pr * pw[0][1] + pi * pw[0][0]))
    rows = jnp.arange(SUBLANES)[:, None]
    sc = []
    for k in (1, 2, 4):
        sc.append(jnp.where(rows >= k, pw[k - 1][0][None, :], 0.0))
        sc.append(jnp.where(rows >= k, pw[k - 1][1][None, :], 0.0))
    sc.append(jnp.stack([pw[r][0] for r in range(SUBLANES)]))
    sc.append(jnp.stack([pw[r][1] for r in range(SUBLANES)]))
    return bmat, cre, cim, jnp.stack(sc).astype(F32)


def _mid_kernel(x_ref, a_ref, s_ref, wo_ref, g_ref, wq_ref, wk_ref, wvt_ref, wg_ref, wqi_ref, wki_ref, wwt_ref,
                qg_ref, kg_ref, h_ref, q_ref, k_ref, vt_ref, sg_ref, qi_ref, ki_ref, wt_ref):
    aw = a_ref.shape[2]
    h = x_ref[0] + _mm(a_ref[0], wo_ref[0:aw, :]) + _mm(s_ref[0], wo_ref[aw:, :])
    h_ref[0] = h
    hn = _rms(h, g_ref[...]).astype(BF16)
    qf = _mm(hn, wq_ref[...])
    for hd in range(C_HEADS):
        sl = slice(hd * C_HEAD_DIM, (hd + 1) * C_HEAD_DIM)
        q_ref[0, :, sl] = (_rms(qf[:, sl], qg_ref[...]) * (C_HEAD_DIM ** -0.5)).astype(BF16)
    kf = _mm(hn, wk_ref[...])
    for hd in range(C_KV_HEADS):
        sl = slice(hd * C_HEAD_DIM, (hd + 1) * C_HEAD_DIM)
        k_ref[0, :, sl] = _rms(kf[:, sl], kg_ref[...]).astype(BF16)
    vt_ref[0] = _mm_nt(wvt_ref[...], hn).astype(BF16)
    sg_ref[0] = _silu(_mm(hn, wg_ref[...])).astype(BF16)
    qi_ref[0] = _mm(hn, wqi_ref[...]).astype(BF16)
    ki_ref[0] = _mm(hn, wki_ref[...]).astype(BF16)
    wt_ref[0] = _mm_nt(wwt_ref[...], hn) * ((IDX_HEADS ** -0.5) * (IDX_DIM ** -0.5))


def _mid(x, att0, ssm0, wo, g, wq, wk, wvt, wg, wqi, wki2, wwt, qg, kg):
    b, l, d = x.shape
    t = ROW_TILE

    def row(n):
        return pl.BlockSpec((1, t, n), lambda bb, i: (bb, i, 0))

    def col(n):
        return pl.BlockSpec((1, n, t), lambda bb, i: (bb, 0, i))

    weights = [wo, g, wq, wk, wvt, wg, wqi, wki2, wwt, qg, kg]
    cw = C_HEADS * C_HEAD_DIM
    ckv = C_KV_HEADS * C_HEAD_DIM
    return pl.pallas_call(
        _mid_kernel,
        grid=(b, l // t),
        in_specs=[row(d), row(att0.shape[2]), row(ssm0.shape[2])] + [_const_spec(w.shape) for w in weights],
        out_specs=[row(d), row(cw), row(ckv), col(ckv), row(cw), row(IDX_HEADS * IDX_DIM), row(2 * LANES),
                   col(IDX_HEADS)],
        out_shape=[jax.ShapeDtypeStruct((b, l, d), F32), jax.ShapeDtypeStruct((b, l, cw), BF16),
                   jax.ShapeDtypeStruct((b, l, ckv), BF16), jax.ShapeDtypeStruct((b, ckv, l), BF16),
                   jax.ShapeDtypeStruct((b, l, cw), BF16), jax.ShapeDtypeStruct((b, l, IDX_HEADS * IDX_DIM), BF16),
                   jax.ShapeDtypeStruct((b, l, 2 * LANES), BF16), jax.ShapeDtypeStruct((b, IDX_HEADS, l), F32)],
        compiler_params=_params("arbitrary", "arbitrary"),
        name="mid",
    )(x, att0, ssm0, *weights)


def _dsa_kernel(q_ref, qi_ref, wt_ref, sg_ref, k_ref, vt_ref, ki_ref, tab_ref, o_ref,
                sc_ref, best_ref, x_ref, acc_ref, *, seq_len, topk):
    i = pl.program_id(1)
    ck = KEY_CHUNK
    per = ck // BLOCK
    nch = (i + per) // per
    t_row = i * BLOCK + lax.broadcasted_iota(I32, (1, LANES), 1)
    kiota = lax.broadcasted_iota(I32, (ck, LANES), 0)

    def chunk_off(c):
        return pl.multiple_of(c * ck, ck)

    qi = qi_ref[0]
    qi_stack = [jnp.concatenate([qi[:, (2 * s) * LANES:(2 * s + 1) * LANES],
                                 qi[:, (2 * s + 1) * LANES:(2 * s + 2) * LANES]], axis=0) for s in range(2)]
    wt = wt_ref[0]

    def score_chunk(c, masked):
        off = chunk_off(c)
        sc = jnp.zeros((ck, LANES), F32)
        for a in range(2):
            kk = ki_ref[0, pl.ds(off, ck), a * LANES:(a + 1) * LANES]
            for s in range(2):
                r = _mm_nt(kk, qi_stack[s])
                for j in range(2):
                    hd = 2 * (2 * s + j) + a
                    sc = sc + jnp.maximum(r[:, j * LANES:(j + 1) * LANES], 0.0) * wt[hd:hd + 1, :]
        if masked:
            sc = jnp.where(off + kiota <= t_row, sc, NEG_INF)
        sc_ref[pl.ds(off, ck), :] = sc
        return _fold(sc, jnp.maximum)

    def score_body(c, mx):
        return jnp.maximum(mx, score_chunk(c, False))

    smax = lax.fori_loop(0, nch - 1, score_body, jnp.full((SUBLANES, LANES), NEG_INF, F32))
    smax = jnp.max(jnp.maximum(smax, score_chunk(nch - 1, True)), axis=0, keepdims=True)

    def count(*preds):
        rows = COUNT_ROWS
        sub = ck // rows

        def body(c, accs):
            out = []
            for u in range(sub):
                off = pl.multiple_of(c * ck + u * rows, rows)
                s = sc_ref[pl.ds(off, rows), :]
                for n, pred in enumerate(preds):
                    ind = pred(s, off).astype(I32)
                    out.append(accs[u * len(preds) + n]
                               + jnp.sum(ind.reshape(rows // SUBLANES, SUBLANES, LANES), axis=0))
            return tuple(out)

        accs = lax.fori_loop(0, nch, body, tuple(jnp.zeros((SUBLANES, LANES), I32) for _ in range(sub * len(preds))))
        res = [jnp.sum(sum(accs[n::len(preds)]), axis=0, keepdims=True) for n in range(len(preds))]
        return res[0] if len(preds) == 1 else res

    def key_value(k):
        return pltpu.bitcast(jnp.where(k < 0, INT_MIN - k, k), F32)

    def count_ge(k):
        thr = key_value(k)
        return count(lambda s, off: s >= thr)

    def full(v):
        return jnp.full((1, LANES), v, I32)

    def bisect(_, st):
        lo, hi, c_lo, c_hi = st
        mid = (lo >> 1) + (hi >> 1) + (lo & hi & 1)
        c = count_ge(mid)
        ge = c >= topk
        return jnp.where(ge, mid, lo), jnp.where(ge, hi, mid), jnp.where(ge, c, c_lo), jnp.where(ge, c_hi, c)

    searching = (i + 1) * BLOCK > topk

    def float_key(x):
        bits = pltpu.bitcast(x, I32)
        return jnp.where(bits < 0, INT_MIN - bits, bits)

    def search():
        k_lo = float_key(smax * 0.125)
        c = count_ge(k_lo)
        ok = (smax > 0.0) & (c >= topk)
        trips = jnp.where(jnp.min(jnp.where(ok, 1, 0)) > 0, 25, 32)
        st = (jnp.where(ok, k_lo, KEY_NEG_INF), float_key(smax) + 1, jnp.where(ok, c, nch * ck), full(0))
        out = lax.fori_loop(0, trips, bisect, st)
        return out[0], out[2], out[3]

    def exchange(v, a, b):
        v[a], v[b] = jnp.maximum(v[a], v[b]), jnp.minimum(v[a], v[b])

    def sort_desc(v):
        n, k = len(v), 2
        while k <= n:
            j = k // 2
            while j >= 1:
                for a in range(n):
                    b = a ^ j
                    if b > a:
                        exchange(v, *((a, b) if (a & k) == 0 else (b, a)))
                j //= 2
            k *= 2

    def merge_top(best, blk):
        n = len(best)
        v = [jnp.maximum(best[r], blk[n - 1 - r]) for r in range(n)]
        j = n // 2
        while j >= 1:
            for a in range(n):
                if a ^ j > a:
                    exchange(v, a, a ^ j)
            j //= 2
        return v

    crow = STREAMS * CAND * SUBLANES

    def cand_body(c, _):
        blk_all = sc_ref[pl.ds(pl.multiple_of(c * crow, crow), crow), :]
        for st in range(STREAMS):
            blk = [blk_all[(STREAMS * r + st) * SUBLANES:(STREAMS * r + st + 1) * SUBLANES] for r in range(CAND)]
            sort_desc(blk)
            base = st * CAND * SUBLANES
            best = [best_ref[base + r * SUBLANES:base + (r + 1) * SUBLANES, :] for r in range(CAND)]
            for r, x in enumerate(merge_top(best, blk)):
                best_ref[base + r * SUBLANES:base + (r + 1) * SUBLANES, :] = x
        return 0

    def cand_search():
        best_ref[...] = jnp.full(best_ref.shape, -jnp.inf, F32)
        lax.fori_loop(0, nch * (ck // crow), cand_body, 0)

        def count_cand(k):
            thr = key_value(k)
            parts = [jnp.sum((best_ref[r:r + COUNT_ROWS, :] >= thr).astype(I32)
                             .reshape(COUNT_ROWS // SUBLANES, SUBLANES, LANES), axis=0)
                     for r in range(0, crow, COUNT_ROWS)]
            return jnp.sum(sum(parts), axis=0, keepdims=True)

        def step(_, st):
            lo, hi = st
            mid = (lo >> 1) + (hi >> 1) + (lo & hi & 1)
            ge = count_cand(mid) >= topk
            return jnp.where(ge, mid, lo), jnp.where(ge, hi, mid)

        k_lo = float_key(smax * 0.125)
        ok = (smax > 0.0) & (count_cand(k_lo) >= topk)
        trips = jnp.where(jnp.min(jnp.where(ok, 1, 0)) > 0, 25, 32)
        vk, _ = lax.fori_loop(0, trips, step, (jnp.where(ok, k_lo, KEY_NEG_INF), float_key(smax) + 1))
        thr = key_value(vk)
        above_cand = jnp.sum((best_ref[...] > thr).astype(I32), axis=0, keepdims=True)
        c_ge, c_gt = count(lambda s, off: s >= thr, lambda s, off: s > thr)
        complete = jnp.min(jnp.where(c_gt == above_cand, 1, 0)) > 0
        return lax.cond(complete, lambda: (vk, c_ge, c_gt), search)

    vkey, c_lo, c_hi = lax.cond(searching, cand_search, lambda: (full(KEY_NEG_INF), full(topk), full(0)))
    vthr = key_value(vkey)
    need = topk - c_hi
    ties = c_lo - c_hi

    def tie_search():
        nxt = vkey + 1
        nxt = jnp.where((nxt > 0) & (nxt < KEY_MIN_NORMAL), KEY_MIN_NORMAL, nxt)
        step = key_value(nxt) - vthr

        def split(_, st):
            fl, fh = st
            fm = 0.5 * (fl + fh)
            thr = vthr + fm * step
            ge = count(lambda s, off: s >= thr) >= topk
            return jnp.where(ge, fm, fl), jnp.where(ge, fh, fm)

        fl, _ = lax.fori_loop(0, 26, split, (jnp.zeros((1, LANES), F32), jnp.ones((1, LANES), F32)))
        thr = vthr + fl * step
        want = topk - count(lambda s, off: s > thr)

        def body(_, st):
            lj, hj = st
            mid = (lj + hj) >> 1
            c = count(lambda s, off: (s == thr) & (off + kiota[:COUNT_ROWS] <= mid))
            ok = c >= want
            return jnp.where(ok, lj, mid), jnp.where(ok, mid, hj)

        _, hj = lax.fori_loop(0, 14, body, (full(-1), full(0) + (nch * ck - 1)))
        return thr, hj

    any_tie = searching & (jnp.max(ties - need) > 0)
    vthr, jmax = lax.cond(any_tie, tie_search, lambda: (vthr, full(seq_len)))

    def selection_mask(off):
        s = sc_ref[pl.ds(off, ck), :]
        s_idx = off + kiota
        sel = ((s > vthr) | ((s == vthr) & (s_idx <= jmax))) & (s_idx <= t_row)
        madd = jnp.where(sel, 0.0, NEG_INF)
        sc_ref[pl.ds(off, ck), :] = madd
        return madd

    q = q_ref[0]
    n_far = jnp.maximum((i - NEAR_BLOCKS + 1) // per, 0)
    hpg = C_HEADS // C_KV_HEADS

    for g in range(C_KV_HEADS):
        q_stack = [jnp.concatenate([q[:, (hpg * g + 2 * jj) * LANES:(hpg * g + 2 * jj + 1) * LANES],
                                    q[:, (hpg * g + 2 * jj + 1) * LANES:(hpg * g + 2 * jj + 2) * LANES]], axis=0)
                   for jj in range(hpg // 2)]

        def stage_body(near, g=g, q_stack=q_stack):
            def body(c, mx):
                off = chunk_off(c)
                madd = selection_mask(off) if g == 0 else sc_ref[pl.ds(off, ck), :]
                kc = k_ref[0, pl.ds(off, ck), g * LANES:(g + 1) * LANES]
                tidx = [jnp.clip(i - (c * per + r), 0, NEAR_BLOCKS) for r in range(per)]
                out = []
                for jj in range(hpg // 2):
                    lg = _mm_nt(kc, q_stack[jj])
                    for a in range(2):
                        hl = 2 * jj + a
                        x = lg[:, a * LANES:(a + 1) * LANES] + madd
                        if near:
                            x = x + jnp.concatenate([tab_ref[hpg * g + hl, tidx[r]] for r in range(per)], axis=0)
                        x_ref[hl, pl.ds(off, ck), :] = x
                        out.append(jnp.maximum(mx[hl], _fold(x, jnp.maximum)))
                return tuple(out)

            return body

        mx = tuple(jnp.full((SUBLANES, LANES), NEG_INF, F32) for _ in range(hpg))
        mx = lax.fori_loop(0, n_far, stage_body(False), mx)
        mx = lax.fori_loop(n_far, nch, stage_body(True), mx)
        m = [jnp.max(v, axis=0, keepdims=True) for v in mx]
        acc_ref[...] = jnp.zeros(acc_ref.shape, F32)

        def att_body(c, ls, g=g, m=m):
            off = chunk_off(c)
            vt = vt_ref[0, g * LANES:(g + 1) * LANES, pl.ds(off, ck)]
            out = []
            for jj in range(hpg // 2):
                ps = []
                for a in range(2):
                    hl = 2 * jj + a
                    p = jnp.exp(x_ref[hl, pl.ds(off, ck), :] - m[hl])
                    out.append(ls[hl] + _fold(p, jnp.add))
                    ps.append(p.astype(BF16))
                acc_ref[jj] += _mm(vt, jnp.concatenate(ps, axis=1))
            return tuple(out)

        ls = lax.fori_loop(0, nch, att_body, tuple(jnp.zeros((SUBLANES, LANES), F32) for _ in range(hpg)))
        for hl in range(hpg):
            sl = slice((hpg * g + hl) * LANES, (hpg * g + hl + 1) * LANES)
            ot = acc_ref[hl // 2, :, (hl % 2) * LANES:(hl % 2 + 1) * LANES] / jnp.sum(ls[hl], axis=0, keepdims=True)
            o_ref[0, :, sl] = (ot.T * sg_ref[0, :, sl].astype(F32)).astype(BF16)


def _dsa(q, qi, wt, sg, k, vt, ki2, tab):
    b, l, cw = q.shape
    nb = l // BLOCK
    topk = min(TOPK_MAX, l // 4)

    def blk(n):
        return pl.BlockSpec((1, BLOCK, n), lambda bb, i: (bb, i, 0))

    def whole(s1, s2):
        return pl.BlockSpec((1, s1, s2), lambda bb, i: (bb, 0, 0), pipeline_mode=pl.Buffered(1))

    hpg = C_HEADS // C_KV_HEADS
    return pl.pallas_call(
        functools.partial(_dsa_kernel, seq_len=l, topk=topk),
        grid=(b, nb),
        in_specs=[blk(cw), blk(qi.shape[2]), pl.BlockSpec((1, IDX_HEADS, BLOCK), lambda bb, i: (bb, 0, i)), blk(cw),
                  whole(l, k.shape[2]), whole(vt.shape[1], l), whole(l, ki2.shape[2]),
                  pl.BlockSpec(tab.shape, lambda bb, i: (0, 0, 0, 0), pipeline_mode=pl.Buffered(1))],
        out_specs=blk(cw),
        out_shape=jax.ShapeDtypeStruct((b, l, cw), BF16),
        scratch_shapes=[pltpu.VMEM((l, LANES), F32), pltpu.VMEM((STREAMS * CAND * SUBLANES, LANES), F32),
                        pltpu.VMEM((hpg, l, LANES), F32),
                        pltpu.VMEM((hpg // 2, C_HEAD_DIM, 2 * LANES), F32)],
        compiler_params=_params("arbitrary", "arbitrary"),
        name="dsa",
    )(q, qi, wt, sg, k, vt, ki2, tab)


def _out_kernel(h_ref, a_ref, w_ref, o_ref):
    o_ref[...] = h_ref[...] + _mm(a_ref[...], w_ref[...])


def _outproj(h2, a2, w):
    rows, d = h2.shape
    t = ROW_TILE
    return pl.pallas_call(
        _out_kernel,
        grid=(rows // t,),
        in_specs=[pl.BlockSpec((t, d), lambda i: (i, 0)), pl.BlockSpec((t, a2.shape[1]), lambda i: (i, 0)),
                  _const_spec(w.shape)],
        out_specs=pl.BlockSpec((t, d), lambda i: (i, 0)),
        out_shape=jax.ShapeDtypeStruct((rows, d), F32),
        compiler_params=_params("arbitrary"),
        name="outproj1",
    )(h2, a2, w)


def _bias_tables(rel_bias, seq_len):
    del seq_len
    nv = (NEAR_BLOCKS + 1) * BLOCK
    vec = rel_bias[_t5_bucket(jnp.arange(nv, dtype=I32))].astype(F32).T

    def window(lo, n):
        pad = max(0, -lo)
        body = vec[:, max(lo, 0):lo + n]
        return jnp.concatenate([jnp.broadcast_to(vec[:, :1], (vec.shape[0], pad)), body], axis=1)

    def toeplitz(g, rows, cols):
        w = rows + cols
        g2 = jnp.concatenate([g[:, rows - 1:rows - 1 + cols], g[:, :1], g[:, :rows - 1]], axis=1)
        flat = jnp.tile(g2, (1, rows))[:, :rows * (w - 1)]
        return flat.reshape(-1, rows, w - 1)[:, :, :cols]

    bias0 = jnp.transpose(toeplitz(window(BLOCK - (2 * BLOCK - 1), 3 * BLOCK - 1), 2 * BLOCK, BLOCK), (0, 2, 1))
    tiles = [toeplitz(window(dl * BLOCK - (BLOCK - 1), 2 * BLOCK - 1), BLOCK, BLOCK) for dl in range(NEAR_BLOCKS)]
    tab = jnp.stack(tiles, axis=1) - rel_bias[NUM_BUCKETS - 1].astype(F32)[:, None, None, None]
    tab = jnp.concatenate([tab, jnp.zeros((tab.shape[0], 1, BLOCK, BLOCK), F32)], axis=1)
    return bias0, tab


def kernel(x, rel_bias, norm_g, ev_w_in, ev_w_out, ev_q_norm_g, ev_k_norm_g, ev_sinks, ev_ssm_log_dt, ev_ssm_a_re,
           ev_ssm_a_im, ev_ssm_b_re, ev_ssm_b_im, ev_ssm_c_re, ev_ssm_c_im, ev_ssm_d, ev_glu_w, ev_glu_b, od_w_in,
           od_w_out, od_q_norm_g, od_k_norm_g):
    b, l, d = x.shape
    assert l % KEY_CHUNK == 0 and l % ROW_TILE == 0
    assert (NEAR_BLOCKS - 1) * BLOCK + 1 >= 16 * 64 ** (15 / 16) + 1
    bias0, tab = _bias_tables(rel_bias, l)

    q0, k0, v0, sga, u, sgb = _proj0(x.reshape(b * l, d), norm_g[0][None, :], ev_w_in[0].astype(BF16))
    shp = lambda a: a.reshape(b, l, a.shape[-1])
    qg2 = jnp.tile(ev_q_norm_g[0], 2)[None, :]
    kg2 = jnp.tile(ev_k_norm_g[0], 2)[None, :]
    sinks = jnp.broadcast_to(ev_sinks[0][:, None], (A_HEADS, LANES)).astype(F32)
    att0 = _attn0(shp(q0), shp(k0), shp(v0), shp(sga), bias0, sinks, qg2, kg2)
    bmat, cre, cim, sc = _s5_prep(ev_ssm_log_dt[0], ev_ssm_a_re[0], ev_ssm_a_im[0], ev_ssm_b_re[0], ev_ssm_b_im[0],
                                  ev_ssm_c_re[0], ev_ssm_c_im[0])
    ssm0 = _ssm(shp(u), shp(sgb), bmat, cre, cim, sc, ev_ssm_d[0].reshape(1, -1), ev_glu_w[0].astype(BF16),
                ev_glu_b[0][None, :])

    w1 = od_w_in[0]
    cw = C_HEADS * C_HEAD_DIM
    ckv = C_KV_HEADS * C_HEAD_DIM
    o = np.cumsum([0, cw, ckv, ckv, cw, IDX_HEADS * IDX_DIM, IDX_DIM, IDX_HEADS])
    wq, wk, wv, wg, wqi, wki, ww = (w1[:, o[n]:o[n + 1]] for n in range(7))
    zki = jnp.zeros((d, LANES - IDX_DIM), w1.dtype)
    wki2 = jnp.concatenate([wki, zki, zki, wki], axis=1)
    bf = lambda a: a.astype(BF16)
    h1, q1, k1, vt1, sg1, qi1, ki2, wt1 = _mid(
        x, att0, ssm0, bf(ev_w_out[0]), norm_g[1][None, :], bf(wq), bf(wk), bf(wv.T), bf(wg), bf(wqi), bf(wki2),
        bf(ww.T), od_q_norm_g[0][None, :], od_k_norm_g[0][None, :])
    att1 = _dsa(q1, qi1, wt1, sg1, k1, vt1, ki2, tab)
    out = _outproj(h1.reshape(b * l, d), att1.reshape(b * l, cw), bf(od_w_out[0]))
    return out.reshape(b, l, d)
```

```python
import functools
import math

import jax
import jax.numpy as jnp
import numpy as np
from jax import lax
from jax.experimental import pallas as pl
from jax.experimental.pallas import tpu as pltpu

F32 = jnp.float32
BF16 = jnp.bfloat16
I32 = jnp.int32

LANES = 128
SUBLANES = 8
VMEM_LIMIT = 56 * 1024 * 1024

BLOCK = 128
WINDOW = 128
A_HEADS = 8
A_HEAD_DIM = 64
A_KV_HEADS = 2
A_WIDTH = A_HEADS * A_HEAD_DIM
SSM_GROUP = 16
SSM_STATE = 64
C_HEADS = 8
C_HEAD_DIM = 128
C_KV_HEADS = 2
IDX_HEADS = 8
IDX_DIM = 64
TOPK_MAX = 256
NUM_BUCKETS = 32
REL_MAX_DIST = 1024
EPS = 1e-6
NEG_INF = -1e30
INT_MIN = -(2 ** 31)
KEY_MIN_NORMAL = 0x00800000
KEY_POS_INF = 0x7F800000
KEY_NEG_INF = INT_MIN + 0x00800000

ROW_TILE = 256
KEY_CHUNK = 1024
NEAR_BLOCKS = 8
FOLD_CHAINS = 8
COUNT_ROWS = 512
SUM_FLOOR = 1e-30
CAND = 32
STREAMS = 2
NT_DIMS = (((1,), (1,)), ((), ()))


def _t5_bucket(dist):
    n = jnp.maximum(dist, 0)
    max_exact = NUM_BUCKETS // 2
    nf = jnp.maximum(n, 1).astype(F32)
    large = max_exact + (jnp.log(nf / max_exact) / math.log(REL_MAX_DIST / max_exact)
                         * (NUM_BUCKETS - max_exact)).astype(I32)
    large = jnp.minimum(large, NUM_BUCKETS - 1)
    return jnp.where(n < max_exact, n, large)


def _silu(x):
    return x * jax.nn.sigmoid(x)


def _rms(x, g):
    ms = jnp.mean(x * x, axis=-1, keepdims=True)
    return x * lax.rsqrt(ms + EPS) * g


def _mm(a, b):
    return jnp.dot(a, b, preferred_element_type=F32)


def _mm_nt(a, b):
    return lax.dot_general(a, b, NT_DIMS, preferred_element_type=F32)


def _fold(x, op):
    n = x.shape[0] // SUBLANES
    chains = min(FOLD_CHAINS, n)
    accs = [x[r * SUBLANES:(r + 1) * SUBLANES] for r in range(chains)]
    for r in range(chains, n):
        accs[r % chains] = op(accs[r % chains], x[r * SUBLANES:(r + 1) * SUBLANES])
    while len(accs) > 1:
        accs = [op(a, b) for a, b in zip(accs[::2], accs[1::2])] + accs[len(accs) & ~1:]
    return accs[0]


def _params(*sem):
    return pltpu.CompilerParams(dimension_semantics=sem, vmem_limit_bytes=VMEM_LIMIT)


def _const_spec(shape):
    zeros = (0,) * len(shape)
    return pl.BlockSpec(shape, lambda *_: zeros)


def _proj0_kernel(x_ref, g_ref, w_ref, q_ref, k_ref, v_ref, sga_ref, u_ref, sgb_ref):
    hn = _rms(x_ref[...], g_ref[...]).astype(BF16)

    def mm(lo, hi):
        return _mm(hn, w_ref[:, lo:hi])

    q_ref[...] = mm(0, 512)
    k_ref[...] = mm(512, 640)
    v_ref[...] = mm(640, 768)
    sga_ref[...] = _silu(mm(768, 1280)).astype(BF16)
    u_ref[...] = mm(1280, 1792)
    sgb_ref[...] = _silu(mm(1792, 2304)).astype(BF16)


def _proj0(x2, g, w):
    rows, d = x2.shape
    t = ROW_TILE

    def row(n):
        return pl.BlockSpec((t, n), lambda i: (i, 0))

    return pl.pallas_call(
        _proj0_kernel,
        grid=(rows // t,),
        in_specs=[row(d), _const_spec((1, d)), _const_spec(w.shape)],
        out_specs=[row(512), row(128), row(128), row(512), row(512), row(512)],
        out_shape=[jax.ShapeDtypeStruct((rows, 512), F32), jax.ShapeDtypeStruct((rows, 128), F32),
                   jax.ShapeDtypeStruct((rows, 128), F32), jax.ShapeDtypeStruct((rows, 512), BF16),
                   jax.ShapeDtypeStruct((rows, 512), F32), jax.ShapeDtypeStruct((rows, 512), BF16)],
        compiler_params=_params("arbitrary"),
        name="proj0",
    )(x2, g, w)


def _attn0_kernel(q_ref, kc_ref, kp_ref, vc_ref, vp_ref, sga_ref, bias_ref, sink_ref, qg_ref, kg_ref, o_ref):
    i = pl.program_id(1)
    lane = lax.broadcasted_iota(I32, (1, LANES), 1)
    lo = lane < A_HEAD_DIM

    def segnorm(x, g2):
        sq = x * x
        s_lo = jnp.sum(jnp.where(lo, sq, 0.0), axis=-1, keepdims=True)
        s_hi = jnp.sum(jnp.where(lo, 0.0, sq), axis=-1, keepdims=True)
        inv = jnp.where(lo, lax.rsqrt(s_lo / A_HEAD_DIM + EPS), lax.rsqrt(s_hi / A_HEAD_DIM + EPS))
        return x * inv * g2

    kn = segnorm(jnp.concatenate([kp_ref[0], kc_ref[0]], axis=0), kg_ref[...])
    vb = jnp.concatenate([vp_ref[0], vc_ref[0]], axis=0)
    kr = pltpu.roll(kn, A_HEAD_DIM, axis=1)
    vr = pltpu.roll(vb, A_HEAD_DIM, axis=1)

    def variants(x, xr):
        return {(0, 0): jnp.where(lo, x, 0.0).astype(BF16), (0, 1): jnp.where(lo, 0.0, xr).astype(BF16),
                (1, 0): jnp.where(lo, xr, 0.0).astype(BF16), (1, 1): jnp.where(lo, 0.0, x).astype(BF16)}

    kvar = variants(kn, kr)
    vvar = variants(vb, vr)

    row = lax.broadcasted_iota(I32, (BLOCK, 2 * BLOCK), 0)
    col = lax.broadcasted_iota(I32, (BLOCK, 2 * BLOCK), 1)
    d = row + BLOCK - col
    mask = (d >= 0) & (d < WINDOW) & ((i > 0) | (col >= BLOCK))

    lgs, sinks = [], []
    for p in range(A_HEADS // 2):
        qp = (segnorm(q_ref[0, :, p * LANES:(p + 1) * LANES], qg_ref[...]) * (A_HEAD_DIM ** -0.5)).astype(BF16)
        for a in range(2):
            h = 2 * p + a
            lgs.append(jnp.where(mask, _mm_nt(qp, kvar[(p // 2, a)]) + bias_ref[h], NEG_INF))
            sinks.append(jnp.broadcast_to(sink_ref[h:h + 1, 0:1], (BLOCK, 1)))
    lg = jnp.concatenate(lgs, axis=0)
    sink = jnp.concatenate(sinks, axis=0)
    m = jnp.maximum(jnp.max(lg, axis=-1, keepdims=True), sink)
    e = jnp.exp(lg - m)
    den = jnp.sum(e, axis=-1, keepdims=True) + jnp.exp(sink - m)
    pr = (e / den).astype(BF16)
    for p in range(A_HEADS // 2):
        sl = slice(p * LANES, (p + 1) * LANES)
        acc = jnp.zeros((BLOCK, LANES), F32)
        for a in range(2):
            h = 2 * p + a
            acc = acc + _mm(pr[h * BLOCK:(h + 1) * BLOCK], vvar[(p // 2, a)])
        o_ref[0, :, sl] = (acc * sga_ref[0, :, sl].astype(F32)).astype(BF16)


def _attn0(q, k, v, sga, bias0, sinks, qg2, kg2):
    b, l, _ = q.shape
    nb = l // BLOCK

    def cur(n):
        return pl.BlockSpec((1, BLOCK, n), lambda bb, i: (bb, i, 0))

    def prev(n):
        return pl.BlockSpec((1, BLOCK, n), lambda bb, i: (bb, jnp.maximum(i - 1, 0), 0))

    return pl.pallas_call(
        _attn0_kernel,
        grid=(b, nb),
        in_specs=[cur(512), cur(128), prev(128), cur(128), prev(128), cur(512),
                  _const_spec(bias0.shape), _const_spec(sinks.shape), _const_spec(qg2.shape), _const_spec(kg2.shape)],
        out_specs=cur(512),
        out_shape=jax.ShapeDtypeStruct((b, l, 512), BF16),
        compiler_params=_params("arbitrary", "arbitrary"),
        name="attn0",
    )(q, k, k, v, v, sga, bias0, sinks, qg2, kg2)


def _ssm_kernel(u_ref, sgb_ref, bmat_ref, cre_ref, cim_ref, sc_ref, d_ref, gw_ref, gb_ref, o_ref, xre_ref, xim_ref):
    t = u_ref.shape[1]
    nq = bmat_ref.shape[0]
    half = bmat_ref.shape[2] // 2

    @pl.when(pl.program_id(1) == 0)
    def _():
        xre_ref[0:SUBLANES, :] = jnp.zeros((SUBLANES, xre_ref.shape[1]), F32)
        xim_ref[0:SUBLANES, :] = jnp.zeros((SUBLANES, xim_ref.shape[1]), F32)

    u = u_ref[0]
    ub = u.astype(BF16)
    for q in range(nq):
        bu = _mm(ub[:, q * LANES:(q + 1) * LANES], bmat_ref[q])
        xre_ref[SUBLANES:, q * half:(q + 1) * half] = bu[:, :half]
        xim_ref[SUBLANES:, q * half:(q + 1) * half] = bu[:, half:]

    def scan(r, _):
        base = pl.multiple_of(SUBLANES + r * SUBLANES, SUBLANES)
        xr = xre_ref[pl.ds(base, SUBLANES), :]
        xi = xim_ref[pl.ds(base, SUBLANES), :]
        for s, k in enumerate((1, 2, 4)):
            ar = sc_ref[2 * s]
            ai = sc_ref[2 * s + 1]
            sr = pltpu.roll(xr, k, axis=0)
            si = pltpu.roll(xi, k, axis=0)
            xr, xi = xr + ar * sr - ai * si, xi + ar * si + ai * sr
        cr = xre_ref[pl.ds(base - 1, 1), :]
        ci = xim_ref[pl.ds(base - 1, 1), :]
        pr = sc_ref[6]
        pi = sc_ref[7]
        xre_ref[pl.ds(base, SUBLANES), :] = xr + pr * cr - pi * ci
        xim_ref[pl.ds(base, SUBLANES), :] = xi + pr * ci + pi * cr
        return 0

    lax.fori_loop(0, t // SUBLANES, scan, 0, unroll=2)
    xre_ref[0:SUBLANES, :] = xre_ref[t:t + SUBLANES, :]
    xim_ref[0:SUBLANES, :] = xim_ref[t:t + SUBLANES, :]

    ys = []
    for q in range(nq):
        xr = xre_ref[SUBLANES:, q * half:(q + 1) * half].astype(BF16)
        xi = xim_ref[SUBLANES:, q * half:(q + 1) * half].astype(BF16)
        ys.append(_mm(xr, cre_ref[q]) + _mm(xi, cim_ref[q]))
    y = jnp.concatenate(ys, axis=1) + d_ref[...] * u
    y = jax.nn.gelu(y).astype(BF16)
    hh = _mm(y, gw_ref[...]) + gb_ref[...]
    w = hh.shape[1] // 2
    o_ref[0] = (hh[:, :w] * jax.nn.sigmoid(hh[:, w:]) * sgb_ref[0].astype(F32)).astype(BF16)


def _ssm(u, sgb, bmat, cre, cim, sc, dskip, gw, gb):
    b, l, w = u.shape
    t = ROW_TILE
    ns = sc.shape[-1]

    def row(n):
        return pl.BlockSpec((1, t, n), lambda bb, i: (bb, i, 0))

    return pl.pallas_call(
        _ssm_kernel,
        grid=(b, l // t),
        in_specs=[row(w), row(w), _const_spec(bmat.shape), _const_spec(cre.shape), _const_spec(cim.shape),
                  _const_spec(sc.shape), _const_spec(dskip.shape), _const_spec(gw.shape), _const_spec(gb.shape)],
        out_specs=row(w),
        out_shape=jax.ShapeDtypeStruct((b, l, w), BF16),
        scratch_shapes=[pltpu.VMEM((SUBLANES + t, ns), F32), pltpu.VMEM((SUBLANES + t, ns), F32)],
        compiler_params=_params("arbitrary", "arbitrary"),
        name="ssm",
    )(u, sgb, bmat, cre, cim, sc, dskip, gw, gb)


def _s5_prep(log_dt, a_re, a_im, b_re, b_im, c_re, c_im):
    g, p = a_re.shape
    h = b_re.shape[-1]
    gl = LANES // h
    nq = g // gl
    dt = jnp.exp(log_dt)[:, None]
    mag = jnp.exp(a_re * dt)
    ang = a_im * dt
    ab_re = mag * jnp.cos(ang)
    ab_im = mag * jnp.sin(ang)
    den = a_re * a_re + a_im * a_im
    n_re = ab_re - 1.0
    n_im = ab_im
    f_re = (n_re * a_re + n_im * a_im) / den
    f_im = (n_im * a_re - n_re * a_im) / den
    bb_re = f_re[..., None] * b_re - f_im[..., None] * b_im
    bb_im = f_re[..., None] * b_im + f_im[..., None] * b_re
    eye = jnp.eye(gl, dtype=F32)

    def bdiag_in(m):
        m = m.reshape(nq, gl, p, h)
        return jnp.einsum('qgph,gk->qghkp', m, eye).reshape(nq, gl * h, gl * p)

    def bdiag_out(m):
        m = m.reshape(nq, gl, h, p)
        return jnp.einsum('qghp,gk->qgpkh', m, eye).reshape(nq, gl * p, gl * h)

    bmat = jnp.concatenate([bdiag_in(bb_re), bdiag_in(bb_im)], axis=2).astype(BF16)
    cre = bdiag_out(c_re).astype(BF16)
    cim = bdiag_out(-c_im).astype(BF16)

    pw = [(ab_re.reshape(-1), ab_im.reshape(-1))]
    for _ in range(SUBLANES - 1):
        pr, pi = pw[-1]
        pw.append((pr * pw[0][0] - pi * pw[0][1], pr * pw[0][1] + pi * pw[0][0]))
    rows = jnp.arange(SUBLANES)[:, None]
    sc = []
    for k in (1, 2, 4):
        sc.append(jnp.where(rows >= k, pw[k - 1][0][None, :], 0.0))
        sc.append(jnp.where(rows >= k, pw[k - 1][1][None, :], 0.0))
    sc.append(jnp.stack([pw[r][0] for r in range(SUBLANES)]))
    sc.append(jnp.stack([pw[r][1] for r in range(SUBLANES)]))
    return bmat, cre, cim, jnp.stack(sc).astype(F32)


def _mid_kernel(x_ref, a_ref, s_ref, wo_ref, g_ref, wq_ref, wk_ref, wvt_ref, wg_ref, wqi_ref, wki_ref, wwt_ref,
                qg_ref, kg_ref, h_ref, q_ref, k_ref, vt_ref, sg_ref, qi_ref, ki_ref, wt_ref):
    aw = a_ref.shape[2]
    h = x_ref[0] + _mm(a_ref[0], wo_ref[0:aw, :]) + _mm(s_ref[0], wo_ref[aw:, :])
    h_ref[0] = h
    hn = _rms(h, g_ref[...]).astype(BF16)
    qf = _mm(hn, wq_ref[...])
    for hd in range(C_HEADS):
        sl = slice(hd * C_HEAD_DIM, (hd + 1) * C_HEAD_DIM)
        q_ref[0, :, sl] = (_rms(qf[:, sl], qg_ref[...]) * (C_HEAD_DIM ** -0.5)).astype(BF16)
    kf = _mm(hn, wk_ref[...])
    for hd in range(C_KV_HEADS):
        sl = slice(hd * C_HEAD_DIM, (hd + 1) * C_HEAD_DIM)
        k_ref[0, :, sl] = _rms(kf[:, sl], kg_ref[...]).astype(BF16)
    vt_ref[0] = _mm_nt(wvt_ref[...], hn).astype(BF16)
    sg_ref[0] = _silu(_mm(hn, wg_ref[...])).astype(BF16)
    qi_ref[0] = _mm(hn, wqi_ref[...]).astype(BF16)
    ki_ref[0] = _mm(hn, wki_ref[...]).astype(BF16)
    wt_ref[0] = _mm_nt(wwt_ref[...], hn) * ((IDX_HEADS ** -0.5) * (IDX_DIM ** -0.5))


def _mid(x, att0, ssm0, wo, g, wq, wk, wvt, wg, wqi, wki2, wwt, qg, kg):
    b, l, d = x.shape
    t = ROW_TILE

    def row(n):
        return pl.BlockSpec((1, t, n), lambda bb, i: (bb, i, 0))

    def col(n):
        return pl.BlockSpec((1, n, t), lambda bb, i: (bb, 0, i))

    weights = [wo, g, wq, wk, wvt, wg, wqi, wki2, wwt, qg, kg]
    cw = C_HEADS * C_HEAD_DIM
    ckv = C_KV_HEADS * C_HEAD_DIM
    return pl.pallas_call(
        _mid_kernel,
        grid=(b, l // t),
        in_specs=[row(d), row(att0.shape[2]), row(ssm0.shape[2])] + [_const_spec(w.shape) for w in weights],
        out_specs=[row(d), row(cw), row(ckv), col(ckv), row(cw), row(IDX_HEADS * IDX_DIM), row(2 * LANES),
                   col(IDX_HEADS)],
        out_shape=[jax.ShapeDtypeStruct((b, l, d), F32), jax.ShapeDtypeStruct((b, l, cw), BF16),
                   jax.ShapeDtypeStruct((b, l, ckv), BF16), jax.ShapeDtypeStruct((b, ckv, l), BF16),
                   jax.ShapeDtypeStruct((b, l, cw), BF16), jax.ShapeDtypeStruct((b, l, IDX_HEADS * IDX_DIM), BF16),
                   jax.ShapeDtypeStruct((b, l, 2 * LANES), BF16), jax.ShapeDtypeStruct((b, IDX_HEADS, l), F32)],
        compiler_params=_params("arbitrary", "arbitrary"),
        name="mid",
    )(x, att0, ssm0, *weights)


def _dsa_kernel(q_ref, qi_ref, wt_ref, sg_ref, k_ref, vt_ref, ki_ref, tab_ref, lb_ref, o_ref,
                sc_ref, best_ref, x_ref, acc_ref, *, seq_len, topk):
    i = pl.program_id(1)
    ck = KEY_CHUNK
    per = ck // BLOCK
    nch = (i + per) // per
    t_row = i * BLOCK + lax.broadcasted_iota(I32, (1, LANES), 1)
    kiota = lax.broadcasted_iota(I32, (ck, LANES), 0)

    def chunk_off(c):
        return pl.multiple_of(c * ck, ck)

    qi = qi_ref[0]
    qi_stack = [jnp.concatenate([qi[:, (2 * s) * LANES:(2 * s + 1) * LANES],
                                 qi[:, (2 * s + 1) * LANES:(2 * s + 2) * LANES]], axis=0) for s in range(2)]
    wt = wt_ref[0]

    def score_chunk(c, masked):
        off = chunk_off(c)
        sc = jnp.zeros((ck, LANES), F32)
        for a in range(2):
            kk = ki_ref[0, pl.ds(off, ck), a * LANES:(a + 1) * LANES]
            for s in range(2):
                r = _mm_nt(kk, qi_stack[s])
                for j in range(2):
                    hd = 2 * (2 * s + j) + a
                    sc = sc + jnp.maximum(r[:, j * LANES:(j + 1) * LANES], 0.0) * wt[hd:hd + 1, :]
        if masked:
            sc = jnp.where(off + kiota <= t_row, sc, NEG_INF)
        sc_ref[pl.ds(off, ck), :] = sc
        return _fold(sc, jnp.maximum)

    def score_body(c, mx):
        return jnp.maximum(mx, score_chunk(c, False))

    smax = lax.fori_loop(0, nch - 1, score_body, jnp.full((SUBLANES, LANES), NEG_INF, F32))
    smax = jnp.max(jnp.maximum(smax, score_chunk(nch - 1, True)), axis=0, keepdims=True)

    def count(*preds):
        rows = COUNT_ROWS
        sub = ck // rows

        def body(c, accs):
            out = []
            for u in range(sub):
                off = pl.multiple_of(c * ck + u * rows, rows)
                s = sc_ref[pl.ds(off, rows), :]
                for n, pred in enumerate(preds):
                    ind = pred(s, off).astype(I32)
                    out.append(accs[u * len(preds) + n]
                               + jnp.sum(ind.reshape(rows // SUBLANES, SUBLANES, LANES), axis=0))
            return tuple(out)

        accs = lax.fori_loop(0, nch, body, tuple(jnp.zeros((SUBLANES, LANES), I32) for _ in range(sub * len(preds))))
        res = [jnp.sum(sum(accs[n::len(preds)]), axis=0, keepdims=True) for n in range(len(preds))]
        return res[0] if len(preds) == 1 else res

    def key_value(k):
        return pltpu.bitcast(jnp.where(k < 0, INT_MIN - k, k), F32)

    def count_ge(k):
        thr = key_value(k)
        return count(lambda s, off: s >= thr)

    def full(v):
        return jnp.full((1, LANES), v, I32)

    def bisect(_, st):
        lo, hi, c_lo, c_hi = st
        mid = (lo >> 1) + (hi >> 1) + (lo & hi & 1)
        c = count_ge(mid)
        ge = c >= topk
        return jnp.where(ge, mid, lo), jnp.where(ge, hi, mid), jnp.where(ge, c, c_lo), jnp.where(ge, c_hi, c)

    searching = (i + 1) * BLOCK > topk

    def float_key(x):
        bits = pltpu.bitcast(x, I32)
        return jnp.where(bits < 0, INT_MIN - bits, bits)

    def search():
        k_lo = float_key(smax * 0.125)
        c = count_ge(k_lo)
        ok = (smax > 0.0) & (c >= topk)
        trips = jnp.where(jnp.min(jnp.where(ok, 1, 0)) > 0, 25, 32)
        st = (jnp.where(ok, k_lo, KEY_NEG_INF), float_key(smax) + 1, jnp.where(ok, c, nch * ck), full(0))
        out = lax.fori_loop(0, trips, bisect, st)
        return out[0], out[2], out[3]

    def exchange(v, a, b):
        v[a], v[b] = jnp.maximum(v[a], v[b]), jnp.minimum(v[a], v[b])

    def sort_desc(v):
        n, k = len(v), 2
        while k <= n:
            j = k // 2
            while j >= 1:
                for a in range(n):
                    b = a ^ j
                    if b > a:
                        exchange(v, *((a, b) if (a & k) == 0 else (b, a)))
                j //= 2
            k *= 2

    def merge_top(best, blk):
        n = len(best)
        v = [jnp.maximum(best[r], blk[n - 1 - r]) for r in range(n)]
        j = n // 2
        while j >= 1:
            for a in range(n):
                if a ^ j > a:
                    exchange(v, a, a ^ j)
            j //= 2
        return v

    crow = STREAMS * CAND * SUBLANES

    def cand_body(c, _):
        blk_all = sc_ref[pl.ds(pl.multiple_of(c * crow, crow), crow), :]
        for st in range(STREAMS):
            blk = [blk_all[(STREAMS * r + st) * SUBLANES:(STREAMS * r + st + 1) * SUBLANES] for r in range(CAND)]
            sort_desc(blk)
            base = st * CAND * SUBLANES
            best = [best_ref[base + r * SUBLANES:base + (r + 1) * SUBLANES, :] for r in range(CAND)]
            for r, x in enumerate(merge_top(best, blk)):
                best_ref[base + r * SUBLANES:base + (r + 1) * SUBLANES, :] = x
        return 0

    def cand_search():
        best_ref[...] = jnp.full(best_ref.shape, -jnp.inf, F32)
        lax.fori_loop(0, nch * (ck // crow), cand_body, 0)

        def count_cand(k):
            thr = key_value(k)
            parts = [jnp.sum((best_ref[r:r + COUNT_ROWS, :] >= thr).astype(I32)
                             .reshape(COUNT_ROWS // SUBLANES, SUBLANES, LANES), axis=0)
                     for r in range(0, crow, COUNT_ROWS)]
            return jnp.sum(sum(parts), axis=0, keepdims=True)

        def step(_, st):
            lo, hi = st
            mid = (lo >> 1) + (hi >> 1) + (lo & hi & 1)
            ge = count_cand(mid) >= topk
            return jnp.where(ge, mid, lo), jnp.where(ge, hi, mid)

        k_lo = float_key(smax * 0.125)
        ok = (smax > 0.0) & (count_cand(k_lo) >= topk)
        trips = jnp.where(jnp.min(jnp.where(ok, 1, 0)) > 0, 25, 32)
        vk, _ = lax.fori_loop(0, trips, step, (jnp.where(ok, k_lo, KEY_NEG_INF), float_key(smax) + 1))
        thr = key_value(vk)
        above_cand = jnp.sum((best_ref[...] > thr).astype(I32), axis=0, keepdims=True)
        c_ge, c_gt = count(lambda s, off: s >= thr, lambda s, off: s > thr)
        complete = jnp.min(jnp.where(c_gt == above_cand, 1, 0)) > 0
        return lax.cond(complete, lambda: (vk, c_ge, c_gt), search)

    vkey, c_lo, c_hi = lax.cond(searching, cand_search, lambda: (full(KEY_NEG_INF), full(topk), full(0)))
    vthr = key_value(vkey)
    need = topk - c_hi
    ties = c_lo - c_hi

    def tie_search():
        nxt = vkey + 1
        nxt = jnp.where((nxt > 0) & (nxt < KEY_MIN_NORMAL), KEY_MIN_NORMAL, nxt)
        step = key_value(nxt) - vthr

        def split(_, st):
            fl, fh = st
            fm = 0.5 * (fl + fh)
            thr = vthr + fm * step
            ge = count(lambda s, off: s >= thr) >= topk
            return jnp.where(ge, fm, fl), jnp.where(ge, fh, fm)

        fl, _ = lax.fori_loop(0, 26, split, (jnp.zeros((1, LANES), F32), jnp.ones((1, LANES), F32)))
        thr = vthr + fl * step
        want = topk - count(lambda s, off: s > thr)

        def body(_, st):
            lj, hj = st
            mid = (lj + hj) >> 1
            c = count(lambda s, off: (s == thr) & (off + kiota[:COUNT_ROWS] <= mid))
            ok = c >= want
            return jnp.where(ok, lj, mid), jnp.where(ok, mid, hj)

        _, hj = lax.fori_loop(0, 14, body, (full(-1), full(0) + (nch * ck - 1)))
        return thr, hj

    any_tie = searching & (jnp.max(ties - need) > 0)
    vthr, jmax = lax.cond(any_tie, tie_search, lambda: (vthr, full(seq_len)))

    def selection_mask(off):
        s = sc_ref[pl.ds(off, ck), :]
        s_idx = off + kiota
        sel = ((s > vthr) | ((s == vthr) & (s_idx <= jmax))) & (s_idx <= t_row)
        madd = jnp.where(sel, 0.0, NEG_INF)
        sc_ref[pl.ds(off, ck), :] = madd
        return madd

    q = q_ref[0]
    n_far = jnp.maximum((i - NEAR_BLOCKS + 1) // per, 0)
    hpg = C_HEADS // C_KV_HEADS
    npair = C_HEADS // 2
    q_pairs = [jnp.concatenate([q[:, (2 * j) * LANES:(2 * j + 1) * LANES],
                                q[:, (2 * j + 1) * LANES:(2 * j + 2) * LANES]], axis=0) for j in range(npair)]

    def bias_rows(hd, c):
        return jnp.concatenate([tab_ref[hd, jnp.clip(i - (c * per + r), 0, NEAR_BLOCKS)] for r in range(per)],
                               axis=0)

    def emit(hd, num, den):
        sl = slice(hd * LANES, (hd + 1) * LANES)
        o_ref[0, :, sl] = ((num / den).T * sg_ref[0, :, sl].astype(F32)).astype(BF16)

    def exact_attention():
        for g in range(C_KV_HEADS):
            def stage_body(near, g=g):
                def body(c, mx):
                    off = chunk_off(c)
                    madd = sc_ref[pl.ds(off, ck), :]
                    kc = k_ref[0, pl.ds(off, ck), g * LANES:(g + 1) * LANES]
                    out = []
                    for jj in range(hpg // 2):
                        lg = _mm_nt(kc, q_pairs[g * (hpg // 2) + jj])
                        for a in range(2):
                            hl = 2 * jj + a
                            x = lg[:, a * LANES:(a + 1) * LANES] + madd
                            if near:
                                x = x + bias_rows(hpg * g + hl, c)
                            x_ref[hl, pl.ds(off, ck), :] = x
                            out.append(jnp.maximum(mx[hl], _fold(x, jnp.maximum)))
                    return tuple(out)

                return body

            mx = tuple(jnp.full((SUBLANES, LANES), NEG_INF, F32) for _ in range(hpg))
            mx = lax.fori_loop(0, n_far, stage_body(False), mx)
            mx = lax.fori_loop(n_far, nch, stage_body(True), mx)
            m = [jnp.max(v, axis=0, keepdims=True) for v in mx]
            acc_ref[...] = jnp.zeros(acc_ref.shape, F32)

            def att_body(c, ls, g=g, m=m):
                off = chunk_off(c)
                vt = vt_ref[0, g * LANES:(g + 1) * LANES, pl.ds(off, ck)]
                out = []
                for jj in range(hpg // 2):
                    ps = []
                    for a in range(2):
                        hl = 2 * jj + a
                        p = jnp.exp(x_ref[hl, pl.ds(off, ck), :] - m[hl])
                        out.append(ls[hl] + _fold(p, jnp.add))
                        ps.append(p.astype(BF16))
                    acc_ref[jj] += _mm(vt, jnp.concatenate(ps, axis=1))
                return tuple(out)

            ls = lax.fori_loop(0, nch, att_body, tuple(jnp.zeros((SUBLANES, LANES), F32) for _ in range(hpg)))
            for hl in range(hpg):
                emit(hpg * g + hl, acc_ref[hl // 2, :, (hl % 2) * LANES:(hl % 2 + 1) * LANES],
                     jnp.sum(ls[hl], axis=0, keepdims=True))

    lb = lb_ref[...]
    acc_ref[...] = jnp.zeros(acc_ref.shape, F32)

    def stage(c, g, near):
        off = chunk_off(c)
        mb = (selection_mask(off) if g == 0 else sc_ref[pl.ds(off, ck), :]) - lb
        kc = k_ref[0, pl.ds(off, ck), g * LANES:(g + 1) * LANES]
        for jj in range(hpg // 2):
            lg = _mm_nt(kc, q_pairs[g * (hpg // 2) + jj])
            for a in range(2):
                hl = 2 * jj + a
                x = lg[:, a * LANES:(a + 1) * LANES] + mb
                if near:
                    x = x + bias_rows(hpg * g + hl, c)
                x_ref[hl, pl.ds(off, ck), :] = x

    def consume(c, g, ls):
        off = chunk_off(c)
        vt = vt_ref[0, g * LANES:(g + 1) * LANES, pl.ds(off, ck)]
        out = []
        for jj in range(hpg // 2):
            ps = []
            for a in range(2):
                hl = 2 * jj + a
                p = jnp.exp(x_ref[hl, pl.ds(off, ck), :])
                out.append(ls[hl] + _fold(p, jnp.add))
                ps.append(p.astype(BF16))
            acc_ref[g * (hpg // 2) + jj] += _mm(vt, jnp.concatenate(ps, axis=1))
        return tuple(out)

    dens = []
    for g in range(C_KV_HEADS):
        def step(near, g=g):
            def body(c, ls):
                out = consume(c, g, ls)
                stage(c + 1, g, near)
                return out

            return body

        stage(0, g, True)
        ls = tuple(jnp.zeros((SUBLANES, LANES), F32) for _ in range(hpg))
        split = jnp.maximum(n_far - 1, 0)
        ls = lax.fori_loop(0, split, step(False), ls)
        ls = lax.fori_loop(split, nch - 1, step(True), ls)
        ls = consume(nch - 1, g, ls)
        dens += [jnp.sum(v, axis=0, keepdims=True) for v in ls]
    in_range = jnp.min(functools.reduce(jnp.minimum, dens)) > SUM_FLOOR

    @pl.when(in_range)
    def _():
        for hd in range(C_HEADS):
            emit(hd, acc_ref[hd // 2, :, (hd % 2) * LANES:(hd % 2 + 1) * LANES], dens[hd])

    @pl.when(jnp.logical_not(in_range))
    def _():
        exact_attention()


def _dsa(q, qi, wt, sg, k, vt, ki2, tab, lb):
    b, l, cw = q.shape
    nb = l // BLOCK
    topk = min(TOPK_MAX, l // 4)

    def blk(n):
        return pl.BlockSpec((1, BLOCK, n), lambda bb, i: (bb, i, 0))

    def whole(s1, s2):
        return pl.BlockSpec((1, s1, s2), lambda bb, i: (bb, 0, 0), pipeline_mode=pl.Buffered(1))

    hpg = C_HEADS // C_KV_HEADS
    return pl.pallas_call(
        functools.partial(_dsa_kernel, seq_len=l, topk=topk),
        grid=(b, nb),
        in_specs=[blk(cw), blk(qi.shape[2]), pl.BlockSpec((1, IDX_HEADS, BLOCK), lambda bb, i: (bb, 0, i)), blk(cw),
                  whole(l, k.shape[2]), whole(vt.shape[1], l), whole(l, ki2.shape[2]),
                  pl.BlockSpec(tab.shape, lambda bb, i: (0, 0, 0, 0), pipeline_mode=pl.Buffered(1)),
                  _const_spec(lb.shape)],
        out_specs=blk(cw),
        out_shape=jax.ShapeDtypeStruct((b, l, cw), BF16),
        scratch_shapes=[pltpu.VMEM((l, LANES), F32), pltpu.VMEM((STREAMS * CAND * SUBLANES, LANES), F32),
                        pltpu.VMEM((hpg, l, LANES), F32),
                        pltpu.VMEM((C_HEADS // 2, C_HEAD_DIM, 2 * LANES), F32)],
        compiler_params=_params("arbitrary", "arbitrary"),
        name="dsa",
    )(q, qi, wt, sg, k, vt, ki2, tab, lb)


def _out_kernel(h_ref, a_ref, w_ref, o_ref):
    o_ref[...] = h_ref[...] + _mm(a_ref[...], w_ref[...])


def _outproj(h2, a2, w):
    rows, d = h2.shape
    t = ROW_TILE
    return pl.pallas_call(
        _out_kernel,
        grid=(rows // t,),
        in_specs=[pl.BlockSpec((t, d), lambda i: (i, 0)), pl.BlockSpec((t, a2.shape[1]), lambda i: (i, 0)),
                  _const_spec(w.shape)],
        out_specs=pl.BlockSpec((t, d), lambda i: (i, 0)),
        out_shape=jax.ShapeDtypeStruct((rows, d), F32),
        compiler_params=_params("arbitrary"),
        name="outproj1",
    )(h2, a2, w)


def _bias_tables(rel_bias, seq_len):
    del seq_len
    nv = (NEAR_BLOCKS + 1) * BLOCK
    vec = rel_bias[_t5_bucket(jnp.arange(nv, dtype=I32))].astype(F32).T

    def window(lo, n):
        pad = max(0, -lo)
        body = vec[:, max(lo, 0):lo + n]
        return jnp.concatenate([jnp.broadcast_to(vec[:, :1], (vec.shape[0], pad)), body], axis=1)

    def toeplitz(g, rows, cols):
        w = rows + cols
        g2 = jnp.concatenate([g[:, rows - 1:rows - 1 + cols], g[:, :1], g[:, :rows - 1]], axis=1)
        flat = jnp.tile(g2, (1, rows))[:, :rows * (w - 1)]
        return flat.reshape(-1, rows, w - 1)[:, :, :cols]

    bias0 = jnp.transpose(toeplitz(window(BLOCK - (2 * BLOCK - 1), 3 * BLOCK - 1), 2 * BLOCK, BLOCK), (0, 2, 1))
    tiles = [toeplitz(window(dl * BLOCK - (BLOCK - 1), 2 * BLOCK - 1), BLOCK, BLOCK) for dl in range(NEAR_BLOCKS)]
    tab = jnp.stack(tiles, axis=1) - rel_bias[NUM_BUCKETS - 1].astype(F32)[:, None, None, None]
    tab = jnp.concatenate([tab, jnp.zeros((tab.shape[0], 1, BLOCK, BLOCK), F32)], axis=1)
    return bias0, tab


def kernel(x, rel_bias, norm_g, ev_w_in, ev_w_out, ev_q_norm_g, ev_k_norm_g, ev_sinks, ev_ssm_log_dt, ev_ssm_a_re,
           ev_ssm_a_im, ev_ssm_b_re, ev_ssm_b_im, ev_ssm_c_re, ev_ssm_c_im, ev_ssm_d, ev_glu_w, ev_glu_b, od_w_in,
           od_w_out, od_q_norm_g, od_k_norm_g):
    b, l, d = x.shape
    assert l % KEY_CHUNK == 0 and l % ROW_TILE == 0
    assert (NEAR_BLOCKS - 1) * BLOCK + 1 >= 16 * 64 ** (15 / 16) + 1
    bias0, tab = _bias_tables(rel_bias, l)

    q0, k0, v0, sga, u, sgb = _proj0(x.reshape(b * l, d), norm_g[0][None, :], ev_w_in[0].astype(BF16))
    shp = lambda a: a.reshape(b, l, a.shape[-1])
    qg2 = jnp.tile(ev_q_norm_g[0], 2)[None, :]
    kg2 = jnp.tile(ev_k_norm_g[0], 2)[None, :]
    sinks = jnp.broadcast_to(ev_sinks[0][:, None], (A_HEADS, LANES)).astype(F32)
    att0 = _attn0(shp(q0), shp(k0), shp(v0), shp(sga), bias0, sinks, qg2, kg2)
    bmat, cre, cim, sc = _s5_prep(ev_ssm_log_dt[0], ev_ssm_a_re[0], ev_ssm_a_im[0], ev_ssm_b_re[0], ev_ssm_b_im[0],
                                  ev_ssm_c_re[0], ev_ssm_c_im[0])
    ssm0 = _ssm(shp(u), shp(sgb), bmat, cre, cim, sc, ev_ssm_d[0].reshape(1, -1), ev_glu_w[0].astype(BF16),
                ev_glu_b[0][None, :])

    w1 = od_w_in[0]
    cw = C_HEADS * C_HEAD_DIM
    ckv = C_KV_HEADS * C_HEAD_DIM
    o = np.cumsum([0, cw, ckv, ckv, cw, IDX_HEADS * IDX_DIM, IDX_DIM, IDX_HEADS])
    wq, wk, wv, wg, wqi, wki, ww = (w1[:, o[n]:o[n + 1]] for n in range(7))
    zki = jnp.zeros((d, LANES - IDX_DIM), w1.dtype)
    wki2 = jnp.concatenate([wki, zki, zki, wki], axis=1)
    bf = lambda a: a.astype(BF16)
    h1, q1, k1, vt1, sg1, qi1, ki2, wt1 = _mid(
        x, att0, ssm0, bf(ev_w_out[0]), norm_g[1][None, :], bf(wq), bf(wk), bf(wv.T), bf(wg), bf(wqi), bf(wki2),
        bf(ww.T), od_q_norm_g[0][None, :], od_k_norm_g[0][None, :])
    lb = (1.02 * C_HEAD_DIM ** 0.5 * jnp.max(jnp.abs(od_q_norm_g[0])) * jnp.max(jnp.abs(od_k_norm_g[0]))
          + jnp.max(tab))
    att1 = _dsa(q1, qi1, wt1, sg1, k1, vt1, ki2, tab, jnp.full((1, LANES), lb, F32))
    out = _outproj(h1.reshape(b * l, d), att1.reshape(b * l, cw), bf(od_w_out[0]))
    return out.reshape(b, l, d)
```

```python
import functools
import math

import jax
import jax.numpy as jnp
import numpy as np
from jax import lax
from jax.experimental import pallas as pl
from jax.experimental.pallas import tpu as pltpu

F32 = jnp.float32
BF16 = jnp.bfloat16
I32 = jnp.int32

LANES = 128
SUBLANES = 8
VMEM_LIMIT = 56 * 1024 * 1024

BLOCK = 128
WINDOW = 128
A_HEADS = 8
A_HEAD_DIM = 64
A_KV_HEADS = 2
A_WIDTH = A_HEADS * A_HEAD_DIM
SSM_GROUP = 16
SSM_STATE = 64
C_HEADS = 8
C_HEAD_DIM = 128
C_KV_HEADS = 2
IDX_HEADS = 8
IDX_DIM = 64
TOPK_MAX = 256
NUM_BUCKETS = 32
REL_MAX_DIST = 1024
EPS = 1e-6
NEG_INF = -1e30
INT_MIN = -(2 ** 31)
KEY_MIN_NORMAL = 0x00800000
KEY_POS_INF = 0x7F800000
KEY_NEG_INF = INT_MIN + 0x00800000

ROW_TILE = 512
KEY_CHUNK = 1024
NEAR_BLOCKS = 8
FOLD_CHAINS = 8
COUNT_ROWS = 512
SUM_FLOOR = 1e-30
CAND = 32
STREAMS = 2
NT_DIMS = (((1,), (1,)), ((), ()))


def _t5_bucket(dist):
    n = jnp.maximum(dist, 0)
    max_exact = NUM_BUCKETS // 2
    nf = jnp.maximum(n, 1).astype(F32)
    large = max_exact + (jnp.log(nf / max_exact) / math.log(REL_MAX_DIST / max_exact)
                         * (NUM_BUCKETS - max_exact)).astype(I32)
    large = jnp.minimum(large, NUM_BUCKETS - 1)
    return jnp.where(n < max_exact, n, large)


def _silu(x):
    return x * jax.nn.sigmoid(x)


def _rms(x, g):
    ms = jnp.mean(x * x, axis=-1, keepdims=True)
    return x * lax.rsqrt(ms + EPS) * g


def _mm(a, b):
    return jnp.dot(a, b, preferred_element_type=F32)


def _mm_nt(a, b):
    return lax.dot_general(a, b, NT_DIMS, preferred_element_type=F32)


def _fold(x, op):
    n = x.shape[0] // SUBLANES
    chains = min(FOLD_CHAINS, n)
    accs = [x[r * SUBLANES:(r + 1) * SUBLANES] for r in range(chains)]
    for r in range(chains, n):
        accs[r % chains] = op(accs[r % chains], x[r * SUBLANES:(r + 1) * SUBLANES])
    while len(accs) > 1:
        accs = [op(a, b) for a, b in zip(accs[::2], accs[1::2])] + accs[len(accs) & ~1:]
    return accs[0]


def _params(*sem):
    return pltpu.CompilerParams(dimension_semantics=sem, vmem_limit_bytes=VMEM_LIMIT)


def _const_spec(shape):
    zeros = (0,) * len(shape)
    return pl.BlockSpec(shape, lambda *_: zeros)


def _proj0_kernel(x_ref, g_ref, w_ref, q_ref, k_ref, v_ref, sga_ref, u_ref, sgb_ref):
    hn = _rms(x_ref[...], g_ref[...]).astype(BF16)

    def mm(lo, hi):
        return _mm(hn, w_ref[:, lo:hi])

    q_ref[...] = mm(0, 512)
    k_ref[...] = mm(512, 640)
    v_ref[...] = mm(640, 768)
    sga_ref[...] = _silu(mm(768, 1280)).astype(BF16)
    u_ref[...] = mm(1280, 1792)
    sgb_ref[...] = _silu(mm(1792, 2304)).astype(BF16)


def _proj0(x2, g, w):
    rows, d = x2.shape
    t = ROW_TILE

    def row(n):
        return pl.BlockSpec((t, n), lambda i: (i, 0))

    return pl.pallas_call(
        _proj0_kernel,
        grid=(rows // t,),
        in_specs=[row(d), _const_spec((1, d)), _const_spec(w.shape)],
        out_specs=[row(512), row(128), row(128), row(512), row(512), row(512)],
        out_shape=[jax.ShapeDtypeStruct((rows, 512), F32), jax.ShapeDtypeStruct((rows, 128), F32),
                   jax.ShapeDtypeStruct((rows, 128), F32), jax.ShapeDtypeStruct((rows, 512), BF16),
                   jax.ShapeDtypeStruct((rows, 512), F32), jax.ShapeDtypeStruct((rows, 512), BF16)],
        compiler_params=_params("arbitrary"),
        name="proj0",
    )(x2, g, w)


def _attn0_kernel(q_ref, kc_ref, kp_ref, vc_ref, vp_ref, sga_ref, bias_ref, sink_ref, qg_ref, kg_ref, o_ref):
    i = pl.program_id(1)
    lane = lax.broadcasted_iota(I32, (1, LANES), 1)
    lo = lane < A_HEAD_DIM

    def segnorm(x, g2):
        sq = x * x
        s_lo = jnp.sum(jnp.where(lo, sq, 0.0), axis=-1, keepdims=True)
        s_hi = jnp.sum(jnp.where(lo, 0.0, sq), axis=-1, keepdims=True)
        inv = jnp.where(lo, lax.rsqrt(s_lo / A_HEAD_DIM + EPS), lax.rsqrt(s_hi / A_HEAD_DIM + EPS))
        return x * inv * g2

    kn = segnorm(jnp.concatenate([kp_ref[0], kc_ref[0]], axis=0), kg_ref[...])
    vb = jnp.concatenate([vp_ref[0], vc_ref[0]], axis=0)
    kr = pltpu.roll(kn, A_HEAD_DIM, axis=1)
    vr = pltpu.roll(vb, A_HEAD_DIM, axis=1)

    def variants(x, xr):
        return {(0, 0): jnp.where(lo, x, 0.0).astype(BF16), (0, 1): jnp.where(lo, 0.0, xr).astype(BF16),
                (1, 0): jnp.where(lo, xr, 0.0).astype(BF16), (1, 1): jnp.where(lo, 0.0, x).astype(BF16)}

    kvar = variants(kn, kr)
    vvar = variants(vb, vr)

    row = lax.broadcasted_iota(I32, (BLOCK, 2 * BLOCK), 0)
    col = lax.broadcasted_iota(I32, (BLOCK, 2 * BLOCK), 1)
    d = row + BLOCK - col
    mask = (d >= 0) & (d < WINDOW) & ((i > 0) | (col >= BLOCK))

    lgs, sinks = [], []
    for p in range(A_HEADS // 2):
        qp = (segnorm(q_ref[0, :, p * LANES:(p + 1) * LANES], qg_ref[...]) * (A_HEAD_DIM ** -0.5)).astype(BF16)
        for a in range(2):
            h = 2 * p + a
            lgs.append(jnp.where(mask, _mm_nt(qp, kvar[(p // 2, a)]) + bias_ref[h], NEG_INF))
            sinks.append(jnp.broadcast_to(sink_ref[h:h + 1, 0:1], (BLOCK, 1)))
    lg = jnp.concatenate(lgs, axis=0)
    sink = jnp.concatenate(sinks, axis=0)
    m = jnp.maximum(jnp.max(lg, axis=-1, keepdims=True), sink)
    e = jnp.exp(lg - m)
    den = jnp.sum(e, axis=-1, keepdims=True) + jnp.exp(sink - m)
    pr = (e / den).astype(BF16)
    for p in range(A_HEADS // 2):
        sl = slice(p * LANES, (p + 1) * LANES)
        acc = jnp.zeros((BLOCK, LANES), F32)
        for a in range(2):
            h = 2 * p + a
            acc = acc + _mm(pr[h * BLOCK:(h + 1) * BLOCK], vvar[(p // 2, a)])
        o_ref[0, :, sl] = (acc * sga_ref[0, :, sl].astype(F32)).astype(BF16)


def _attn0(q, k, v, sga, bias0, sinks, qg2, kg2):
    b, l, _ = q.shape
    nb = l // BLOCK

    def cur(n):
        return pl.BlockSpec((1, BLOCK, n), lambda bb, i: (bb, i, 0))

    def prev(n):
        return pl.BlockSpec((1, BLOCK, n), lambda bb, i: (bb, jnp.maximum(i - 1, 0), 0))

    return pl.pallas_call(
        _attn0_kernel,
        grid=(b, nb),
        in_specs=[cur(512), cur(128), prev(128), cur(128), prev(128), cur(512),
                  _const_spec(bias0.shape), _const_spec(sinks.shape), _const_spec(qg2.shape), _const_spec(kg2.shape)],
        out_specs=cur(512),
        out_shape=jax.ShapeDtypeStruct((b, l, 512), BF16),
        compiler_params=_params("arbitrary", "arbitrary"),
        name="attn0",
    )(q, k, k, v, v, sga, bias0, sinks, qg2, kg2)


def _ssm_kernel(u_ref, sgb_ref, bmat_ref, cre_ref, cim_ref, sc_ref, d_ref, gw_ref, gb_ref, o_ref, xre_ref, xim_ref):
    t = u_ref.shape[1]
    nq = bmat_ref.shape[0]
    half = bmat_ref.shape[2] // 2

    @pl.when(pl.program_id(1) == 0)
    def _():
        xre_ref[0:SUBLANES, :] = jnp.zeros((SUBLANES, xre_ref.shape[1]), F32)
        xim_ref[0:SUBLANES, :] = jnp.zeros((SUBLANES, xim_ref.shape[1]), F32)

    u = u_ref[0]
    ub = u.astype(BF16)
    for q in range(nq):
        bu = _mm(ub[:, q * LANES:(q + 1) * LANES], bmat_ref[q])
        xre_ref[SUBLANES:, q * half:(q + 1) * half] = bu[:, :half]
        xim_ref[SUBLANES:, q * half:(q + 1) * half] = bu[:, half:]

    def scan(r, _):
        base = pl.multiple_of(SUBLANES + r * SUBLANES, SUBLANES)
        xr = xre_ref[pl.ds(base, SUBLANES), :]
        xi = xim_ref[pl.ds(base, SUBLANES), :]
        for s, k in enumerate((1, 2, 4)):
            ar = sc_ref[2 * s]
            ai = sc_ref[2 * s + 1]
            sr = pltpu.roll(xr, k, axis=0)
            si = pltpu.roll(xi, k, axis=0)
            xr, xi = xr + ar * sr - ai * si, xi + ar * si + ai * sr
        cr = xre_ref[pl.ds(base - 1, 1), :]
        ci = xim_ref[pl.ds(base - 1, 1), :]
        pr = sc_ref[6]
        pi = sc_ref[7]
        xre_ref[pl.ds(base, SUBLANES), :] = xr + pr * cr - pi * ci
        xim_ref[pl.ds(base, SUBLANES), :] = xi + pr * ci + pi * cr
        return 0

    lax.fori_loop(0, t // SUBLANES, scan, 0, unroll=2)
    xre_ref[0:SUBLANES, :] = xre_ref[t:t + SUBLANES, :]
    xim_ref[0:SUBLANES, :] = xim_ref[t:t + SUBLANES, :]

    ys = []
    for q in range(nq):
        xr = xre_ref[SUBLANES:, q * half:(q + 1) * half].astype(BF16)
        xi = xim_ref[SUBLANES:, q * half:(q + 1) * half].astype(BF16)
        ys.append(_mm(xr, cre_ref[q]) + _mm(xi, cim_ref[q]))
    y = jnp.concatenate(ys, axis=1) + d_ref[...] * u
    y = jax.nn.gelu(y).astype(BF16)
    hh = _mm(y, gw_ref[...]) + gb_ref[...]
    w = hh.shape[1] // 2
    o_ref[0] = (hh[:, :w] * jax.nn.sigmoid(hh[:, w:]) * sgb_ref[0].astype(F32)).astype(BF16)


def _ssm(u, sgb, bmat, cre, cim, sc, dskip, gw, gb):
    b, l, w = u.shape
    t = ROW_TILE
    ns = sc.shape[-1]

    def row(n):
        return pl.BlockSpec((1, t, n), lambda bb, i: (bb, i, 0))

    return pl.pallas_call(
        _ssm_kernel,
        grid=(b, l // t),
        in_specs=[row(w), row(w), _const_spec(bmat.shape), _const_spec(cre.shape), _const_spec(cim.shape),
                  _const_spec(sc.shape), _const_spec(dskip.shape), _const_spec(gw.shape), _const_spec(gb.shape)],
        out_specs=row(w),
        out_shape=jax.ShapeDtypeStruct((b, l, w), BF16),
        scratch_shapes=[pltpu.VMEM((SUBLANES + t, ns), F32), pltpu.VMEM((SUBLANES + t, ns), F32)],
        compiler_params=_params("arbitrary", "arbitrary"),
        name="ssm",
    )(u, sgb, bmat, cre, cim, sc, dskip, gw, gb)


def _s5_prep(log_dt, a_re, a_im, b_re, b_im, c_re, c_im):
    g, p = a_re.shape
    h = b_re.shape[-1]
    gl = LANES // h
    nq = g // gl
    dt = jnp.exp(log_dt)[:, None]
    mag = jnp.exp(a_re * dt)
    ang = a_im * dt
    ab_re = mag * jnp.cos(ang)
    ab_im = mag * jnp.sin(ang)
    den = a_re * a_re + a_im * a_im
    n_re = ab_re - 1.0
    n_im = ab_im
    f_re = (n_re * a_re + n_im * a_im) / den
    f_im = (n_im * a_re - n_re * a_im) / den
    bb_re = f_re[..., None] * b_re - f_im[..., None] * b_im
    bb_im = f_re[..., None] * b_im + f_im[..., None] * b_re
    eye = jnp.eye(gl, dtype=F32)

    def bdiag_in(m):
        m = m.reshape(nq, gl, p, h)
        return jnp.einsum('qgph,gk->qghkp', m, eye).reshape(nq, gl * h, gl * p)

    def bdiag_out(m):
        m = m.reshape(nq, gl, h, p)
        return jnp.einsum('qghp,gk->qgpkh', m, eye).reshape(nq, gl * p, gl * h)

    bmat = jnp.concatenate([bdiag_in(bb_re), bdiag_in(bb_im)], axis=2).astype(BF16)
    cre = bdiag_out(c_re).astype(BF16)
    cim = bdiag_out(-c_im).astype(BF16)

    pw = [(ab_re.reshape(-1), ab_im.reshape(-1))]
    for _ in range(SUBLANES - 1):
        pr, pi = pw[-1]
        pw.append((pr * pw[0][0] - pi * pw[0][1], pr * pw[0][1] + pi * pw[0][0]))
    rows = jnp.arange(SUBLANES)[:, None]
    sc = []
    for k in (1, 2, 4):
        sc.append(jnp.where(rows >= k, pw[k - 1][0][None, :], 0.0))
        sc.append(jnp.where(rows >= k, pw[k - 1][1][None, :], 0.0))
    sc.append(jnp.stack([pw[r][0] for r in range(SUBLANES)]))
    sc.append(jnp.stack([pw[r][1] for r in range(SUBLANES)]))
    return bmat, cre, cim, jnp.stack(sc).astype(F32)


def _mid_kernel(x_ref, a_ref, s_ref, wo_ref, g_ref, wq_ref, wk_ref, wvt_ref, wg_ref, wqi_ref, wki_ref, wwt_ref,
                qg_ref, kg_ref, h_ref, q_ref, k_ref, vt_ref, sg_ref, qi_ref, ki_ref, wt_ref):
    aw = a_ref.shape[2]
    h = x_ref[0] + _mm(a_ref[0], wo_ref[0:aw, :]) + _mm(s_ref[0], wo_ref[aw:, :])
    h_ref[0] = h
    hn = _rms(h, g_ref[...]).astype(BF16)
    qf = _mm(hn, wq_ref[...])
    for hd in range(C_HEADS):
        sl = slice(hd * C_HEAD_DIM, (hd + 1) * C_HEAD_DIM)
        q_ref[0, :, sl] = (_rms(qf[:, sl], qg_ref[...]) * (C_HEAD_DIM ** -0.5)).astype(BF16)
    kf = _mm(hn, wk_ref[...])
    for hd in range(C_KV_HEADS):
        sl = slice(hd * C_HEAD_DIM, (hd + 1) * C_HEAD_DIM)
        k_ref[0, :, sl] = _rms(kf[:, sl], kg_ref[...]).astype(BF16)
    vt_ref[0] = _mm_nt(wvt_ref[...], hn).astype(BF16)
    sg_ref[0] = _silu(_mm(hn, wg_ref[...])).astype(BF16)
    qi_ref[0] = _mm(hn, wqi_ref[...]).astype(BF16)
    ki_ref[0] = _mm(hn, wki_ref[...]).astype(BF16)
    wt_ref[0] = _mm_nt(wwt_ref[...], hn) * ((IDX_HEADS ** -0.5) * (IDX_DIM ** -0.5))


def _mid(x, att0, ssm0, wo, g, wq, wk, wvt, wg, wqi, wki2, wwt, qg, kg):
    b, l, d = x.shape
    t = ROW_TILE

    def row(n):
        return pl.BlockSpec((1, t, n), lambda bb, i: (bb, i, 0))

    def col(n):
        return pl.BlockSpec((1, n, t), lambda bb, i: (bb, 0, i))

    weights = [wo, g, wq, wk, wvt, wg, wqi, wki2, wwt, qg, kg]
    cw = C_HEADS * C_HEAD_DIM
    ckv = C_KV_HEADS * C_HEAD_DIM
    return pl.pallas_call(
        _mid_kernel,
        grid=(b, l // t),
        in_specs=[row(d), row(att0.shape[2]), row(ssm0.shape[2])] + [_const_spec(w.shape) for w in weights],
        out_specs=[row(d), row(cw), row(ckv), col(ckv), row(cw), row(IDX_HEADS * IDX_DIM), row(2 * LANES),
                   col(IDX_HEADS)],
        out_shape=[jax.ShapeDtypeStruct((b, l, d), F32), jax.ShapeDtypeStruct((b, l, cw), BF16),
                   jax.ShapeDtypeStruct((b, l, ckv), BF16), jax.ShapeDtypeStruct((b, ckv, l), BF16),
                   jax.ShapeDtypeStruct((b, l, cw), BF16), jax.ShapeDtypeStruct((b, l, IDX_HEADS * IDX_DIM), BF16),
                   jax.ShapeDtypeStruct((b, l, 2 * LANES), BF16), jax.ShapeDtypeStruct((b, IDX_HEADS, l), F32)],
        compiler_params=_params("arbitrary", "arbitrary"),
        name="mid",
    )(x, att0, ssm0, *weights)


def _dsa_kernel(q_ref, qi_ref, wt_ref, sg_ref, k_ref, vt_ref, ki_ref, tab_ref, lb_ref, o_ref,
                sc_ref, best_ref, x_ref, acc_ref, *, seq_len, topk):
    i = pl.program_id(1)
    ck = KEY_CHUNK
    per = ck // BLOCK
    nch = (i + per) // per
    t_row = i * BLOCK + lax.broadcasted_iota(I32, (1, LANES), 1)
    kiota = lax.broadcasted_iota(I32, (ck, LANES), 0)

    def chunk_off(c):
        return pl.multiple_of(c * ck, ck)

    qi = qi_ref[0]
    qi_stack = [jnp.concatenate([qi[:, (2 * s) * LANES:(2 * s + 1) * LANES],
                                 qi[:, (2 * s + 1) * LANES:(2 * s + 2) * LANES]], axis=0) for s in range(2)]
    wt = wt_ref[0]

    def score_chunk(c, masked):
        off = chunk_off(c)
        sc = jnp.zeros((ck, LANES), F32)
        for a in range(2):
            kk = ki_ref[0, pl.ds(off, ck), a * LANES:(a + 1) * LANES]
            for s in range(2):
                r = _mm_nt(kk, qi_stack[s])
                for j in range(2):
                    hd = 2 * (2 * s + j) + a
                    sc = sc + jnp.maximum(r[:, j * LANES:(j + 1) * LANES], 0.0) * wt[hd:hd + 1, :]
        if masked:
            sc = jnp.where(off + kiota <= t_row, sc, NEG_INF)
        sc_ref[pl.ds(off, ck), :] = sc
        return _fold(sc, jnp.maximum)

    def score_body(c, mx):
        return jnp.maximum(mx, score_chunk(c, False))

    smax = lax.fori_loop(0, nch - 1, score_body, jnp.full((SUBLANES, LANES), NEG_INF, F32))
    smax = jnp.max(jnp.maximum(smax, score_chunk(nch - 1, True)), axis=0, keepdims=True)

    def count(*preds):
        rows = COUNT_ROWS
        sub = ck // rows

        def body(c, accs):
            out = []
            for u in range(sub):
                off = pl.multiple_of(c * ck + u * rows, rows)
                s = sc_ref[pl.ds(off, rows), :]
                for n, pred in enumerate(preds):
                    ind = pred(s, off).astype(I32)
                    out.append(accs[u * len(preds) + n]
                               + jnp.sum(ind.reshape(rows // SUBLANES, SUBLANES, LANES), axis=0))
            return tuple(out)

        accs = lax.fori_loop(0, nch, body, tuple(jnp.zeros((SUBLANES, LANES), I32) for _ in range(sub * len(preds))))
        res = [jnp.sum(sum(accs[n::len(preds)]), axis=0, keepdims=True) for n in range(len(preds))]
        return res[0] if len(preds) == 1 else res

    def key_value(k):
        return pltpu.bitcast(jnp.where(k < 0, INT_MIN - k, k), F32)

    def count_ge(k):
        thr = key_value(k)
        return count(lambda s, off: s >= thr)

    def full(v):
        return jnp.full((1, LANES), v, I32)

    def bisect(_, st):
        lo, hi, c_lo, c_hi = st
        mid = (lo >> 1) + (hi >> 1) + (lo & hi & 1)
        c = count_ge(mid)
        ge = c >= topk
        return jnp.where(ge, mid, lo), jnp.where(ge, hi, mid), jnp.where(ge, c, c_lo), jnp.where(ge, c_hi, c)

    searching = (i + 1) * BLOCK > topk

    def float_key(x):
        bits = pltpu.bitcast(x, I32)
        return jnp.where(bits < 0, INT_MIN - bits, bits)

    def search():
        k_lo = float_key(smax * 0.125)
        c = count_ge(k_lo)
        ok = (smax > 0.0) & (c >= topk)
        trips = jnp.where(jnp.min(jnp.where(ok, 1, 0)) > 0, 25, 32)
        st = (jnp.where(ok, k_lo, KEY_NEG_INF), float_key(smax) + 1, jnp.where(ok, c, nch * ck), full(0))
        out = lax.fori_loop(0, trips, bisect, st)
        return out[0], out[2], out[3]

    def exchange(v, a, b):
        v[a], v[b] = jnp.maximum(v[a], v[b]), jnp.minimum(v[a], v[b])

    def sort_desc(v):
        n, k = len(v), 2
        while k <= n:
            j = k // 2
            while j >= 1:
                for a in range(n):
                    b = a ^ j
                    if b > a:
                        exchange(v, *((a, b) if (a & k) == 0 else (b, a)))
                j //= 2
            k *= 2

    def merge_top(best, blk):
        n = len(best)
        v = [jnp.maximum(best[r], blk[n - 1 - r]) for r in range(n)]
        j = n // 2
        while j >= 1:
            for a in range(n):
                if a ^ j > a:
                    exchange(v, a, a ^ j)
            j //= 2
        return v

    crow = STREAMS * CAND * SUBLANES

    def cand_body(c, _):
        blk_all = sc_ref[pl.ds(pl.multiple_of(c * crow, crow), crow), :]
        for st in range(STREAMS):
            blk = [blk_all[(STREAMS * r + st) * SUBLANES:(STREAMS * r + st + 1) * SUBLANES] for r in range(CAND)]
            sort_desc(blk)
            base = st * CAND * SUBLANES
            best = [best_ref[base + r * SUBLANES:base + (r + 1) * SUBLANES, :] for r in range(CAND)]
            for r, x in enumerate(merge_top(best, blk)):
                best_ref[base + r * SUBLANES:base + (r + 1) * SUBLANES, :] = x
        return 0

    def cand_search():
        best_ref[...] = jnp.full(best_ref.shape, -jnp.inf, F32)
        lax.fori_loop(0, nch * (ck // crow), cand_body, 0)

        def count_cand(k):
            thr = key_value(k)
            parts = [jnp.sum((best_ref[r:r + COUNT_ROWS, :] >= thr).astype(I32)
                             .reshape(COUNT_ROWS // SUBLANES, SUBLANES, LANES), axis=0)
                     for r in range(0, crow, COUNT_ROWS)]
            return jnp.sum(sum(parts), axis=0, keepdims=True)

        def step(_, st):
            lo, hi = st
            mid = (lo >> 1) + (hi >> 1) + (lo & hi & 1)
            ge = count_cand(mid) >= topk
            return jnp.where(ge, mid, lo), jnp.where(ge, hi, mid)

        k_lo = float_key(smax * 0.125)
        ok = (smax > 0.0) & (count_cand(k_lo) >= topk)
        trips = jnp.where(jnp.min(jnp.where(ok, 1, 0)) > 0, 25, 32)
        vk, _ = lax.fori_loop(0, trips, step, (jnp.where(ok, k_lo, KEY_NEG_INF), float_key(smax) + 1))
        thr = key_value(vk)
        above_cand = jnp.sum((best_ref[...] > thr).astype(I32), axis=0, keepdims=True)
        c_ge, c_gt = count(lambda s, off: s >= thr, lambda s, off: s > thr)
        complete = jnp.min(jnp.where(c_gt == above_cand, 1, 0)) > 0
        return lax.cond(complete, lambda: (vk, c_ge, c_gt), search)

    vkey, c_lo, c_hi = lax.cond(searching, cand_search, lambda: (full(KEY_NEG_INF), full(topk), full(0)))
    vthr = key_value(vkey)
    need = topk - c_hi
    ties = c_lo - c_hi

    def tie_search():
        nxt = vkey + 1
        nxt = jnp.where((nxt > 0) & (nxt < KEY_MIN_NORMAL), KEY_MIN_NORMAL, nxt)
        step = key_value(nxt) - vthr

        def split(_, st):
            fl, fh = st
            fm = 0.5 * (fl + fh)
            thr = vthr + fm * step
            ge = count(lambda s, off: s >= thr) >= topk
            return jnp.where(ge, fm, fl), jnp.where(ge, fh, fm)

        fl, _ = lax.fori_loop(0, 26, split, (jnp.zeros((1, LANES), F32), jnp.ones((1, LANES), F32)))
        thr = vthr + fl * step
        want = topk - count(lambda s, off: s > thr)

        def body(_, st):
            lj, hj = st
            mid = (lj + hj) >> 1
            c = count(lambda s, off: (s == thr) & (off + kiota[:COUNT_ROWS] <= mid))
            ok = c >= want
            return jnp.where(ok, lj, mid), jnp.where(ok, mid, hj)

        _, hj = lax.fori_loop(0, 14, body, (full(-1), full(0) + (nch * ck - 1)))
        return thr, hj

    any_tie = searching & (jnp.max(ties - need) > 0)
    vthr, jmax = lax.cond(any_tie, tie_search, lambda: (vthr, full(seq_len)))

    def selection_mask(off):
        s = sc_ref[pl.ds(off, ck), :]
        s_idx = off + kiota
        sel = ((s > vthr) | ((s == vthr) & (s_idx <= jmax))) & (s_idx <= t_row)
        madd = jnp.where(sel, 0.0, NEG_INF)
        sc_ref[pl.ds(off, ck), :] = madd
        return madd

    q = q_ref[0]
    n_far = jnp.maximum((i - NEAR_BLOCKS + 1) // per, 0)
    hpg = C_HEADS // C_KV_HEADS
    npair = C_HEADS // 2
    q_pairs = [jnp.concatenate([q[:, (2 * j) * LANES:(2 * j + 1) * LANES],
                                q[:, (2 * j + 1) * LANES:(2 * j + 2) * LANES]], axis=0) for j in range(npair)]

    def bias_rows(hd, c):
        return jnp.concatenate([tab_ref[hd, jnp.clip(i - (c * per + r), 0, NEAR_BLOCKS)] for r in range(per)],
                               axis=0)

    def emit(hd, num, den):
        sl = slice(hd * LANES, (hd + 1) * LANES)
        o_ref[0, :, sl] = ((num / den).T * sg_ref[0, :, sl].astype(F32)).astype(BF16)

    def exact_attention():
        for g in range(C_KV_HEADS):
            def stage_body(near, g=g):
                def body(c, mx):
                    off = chunk_off(c)
                    madd = sc_ref[pl.ds(off, ck), :]
                    kc = k_ref[0, pl.ds(off, ck), g * LANES:(g + 1) * LANES]
                    out = []
                    for jj in range(hpg // 2):
                        lg = _mm_nt(kc, q_pairs[g * (hpg // 2) + jj])
                        for a in range(2):
                            hl = 2 * jj + a
                            x = lg[:, a * LANES:(a + 1) * LANES] + madd
                            if near:
                                x = x + bias_rows(hpg * g + hl, c)
                            x_ref[hl, pl.ds(off, ck), :] = x
                            out.append(jnp.maximum(mx[hl], _fold(x, jnp.maximum)))
                    return tuple(out)

                return body

            mx = tuple(jnp.full((SUBLANES, LANES), NEG_INF, F32) for _ in range(hpg))
            mx = lax.fori_loop(0, n_far, stage_body(False), mx)
            mx = lax.fori_loop(n_far, nch, stage_body(True), mx)
            m = [jnp.max(v, axis=0, keepdims=True) for v in mx]
            acc_ref[...] = jnp.zeros(acc_ref.shape, F32)

            def att_body(c, ls, g=g, m=m):
                off = chunk_off(c)
                vt = vt_ref[0, g * LANES:(g + 1) * LANES, pl.ds(off, ck)]
                out = []
                for jj in range(hpg // 2):
                    ps = []
                    for a in range(2):
                        hl = 2 * jj + a
                        p = jnp.exp(x_ref[hl, pl.ds(off, ck), :] - m[hl])
                        out.append(ls[hl] + _fold(p, jnp.add))
                        ps.append(p.astype(BF16))
                    acc_ref[jj] += _mm(vt, jnp.concatenate(ps, axis=1))
                return tuple(out)

            ls = lax.fori_loop(0, nch, att_body, tuple(jnp.zeros((SUBLANES, LANES), F32) for _ in range(hpg)))
            for hl in range(hpg):
                emit(hpg * g + hl, acc_ref[hl // 2, :, (hl % 2) * LANES:(hl % 2 + 1) * LANES],
                     jnp.sum(ls[hl], axis=0, keepdims=True))

    lb = lb_ref[...]
    acc_ref[...] = jnp.zeros(acc_ref.shape, F32)

    def stage(c, g, near):
        off = chunk_off(c)
        mb = (selection_mask(off) if g == 0 else sc_ref[pl.ds(off, ck), :]) - lb
        kc = k_ref[0, pl.ds(off, ck), g * LANES:(g + 1) * LANES]
        for jj in range(hpg // 2):
            lg = _mm_nt(kc, q_pairs[g * (hpg // 2) + jj])
            for a in range(2):
                hl = 2 * jj + a
                x = lg[:, a * LANES:(a + 1) * LANES] + mb
                if near:
                    x = x + bias_rows(hpg * g + hl, c)
                x_ref[hl, pl.ds(off, ck), :] = x

    def consume(c, g, ls):
        off = chunk_off(c)
        vt = vt_ref[0, g * LANES:(g + 1) * LANES, pl.ds(off, ck)]
        out = []
        for jj in range(hpg // 2):
            ps = []
            for a in range(2):
                hl = 2 * jj + a
                p = jnp.exp(x_ref[hl, pl.ds(off, ck), :])
                out.append(ls[hl] + _fold(p, jnp.add))
                ps.append(p.astype(BF16))
            acc_ref[g * (hpg // 2) + jj] += _mm(vt, jnp.concatenate(ps, axis=1))
        return tuple(out)

    dens = []
    for g in range(C_KV_HEADS):
        def step(near, g=g):
            def body(c, ls):
                out = consume(c, g, ls)
                stage(c + 1, g, near)
                return out

            return body

        stage(0, g, True)
        ls = tuple(jnp.zeros((SUBLANES, LANES), F32) for _ in range(hpg))
        split = jnp.maximum(n_far - 1, 0)
        ls = lax.fori_loop(0, split, step(False), ls)
        ls = lax.fori_loop(split, nch - 1, step(True), ls)
        ls = consume(nch - 1, g, ls)
        dens += [jnp.sum(v, axis=0, keepdims=True) for v in ls]
    in_range = jnp.min(functools.reduce(jnp.minimum, dens)) > SUM_FLOOR

    @pl.when(in_range)
    def _():
        for hd in range(C_HEADS):
            emit(hd, acc_ref[hd // 2, :, (hd % 2) * LANES:(hd % 2 + 1) * LANES], dens[hd])

    @pl.when(jnp.logical_not(in_range))
    def _():
        exact_attention()


def _dsa(q, qi, wt, sg, k, vt, ki2, tab, lb):
    b, l, cw = q.shape
    nb = l // BLOCK
    topk = min(TOPK_MAX, l // 4)

    def blk(n):
        return pl.BlockSpec((1, BLOCK, n), lambda bb, i: (bb, i, 0))

    def whole(s1, s2):
        return pl.BlockSpec((1, s1, s2), lambda bb, i: (bb, 0, 0), pipeline_mode=pl.Buffered(1))

    hpg = C_HEADS // C_KV_HEADS
    return pl.pallas_call(
        functools.partial(_dsa_kernel, seq_len=l, topk=topk),
        grid=(b, nb),
        in_specs=[blk(cw), blk(qi.shape[2]), pl.BlockSpec((1, IDX_HEADS, BLOCK), lambda bb, i: (bb, 0, i)), blk(cw),
                  whole(l, k.shape[2]), whole(vt.shape[1], l), whole(l, ki2.shape[2]),
                  pl.BlockSpec(tab.shape, lambda bb, i: (0, 0, 0, 0), pipeline_mode=pl.Buffered(1)),
                  _const_spec(lb.shape)],
        out_specs=blk(cw),
        out_shape=jax.ShapeDtypeStruct((b, l, cw), BF16),
        scratch_shapes=[pltpu.VMEM((l, LANES), F32), pltpu.VMEM((STREAMS * CAND * SUBLANES, LANES), F32),
                        pltpu.VMEM((hpg, l, LANES), F32),
                        pltpu.VMEM((C_HEADS // 2, C_HEAD_DIM, 2 * LANES), F32)],
        compiler_params=_params("arbitrary", "arbitrary"),
        name="dsa",
    )(q, qi, wt, sg, k, vt, ki2, tab, lb)


def _out_kernel(h_ref, a_ref, w_ref, o_ref):
    o_ref[...] = h_ref[...] + _mm(a_ref[...], w_ref[...])


def _outproj(h2, a2, w):
    rows, d = h2.shape
    t = ROW_TILE
    return pl.pallas_call(
        _out_kernel,
        grid=(rows // t,),
        in_specs=[pl.BlockSpec((t, d), lambda i: (i, 0)), pl.BlockSpec((t, a2.shape[1]), lambda i: (i, 0)),
                  _const_spec(w.shape)],
        out_specs=pl.BlockSpec((t, d), lambda i: (i, 0)),
        out_shape=jax.ShapeDtypeStruct((rows, d), F32),
        compiler_params=_params("arbitrary"),
        name="outproj1",
    )(h2, a2, w)


def _bias_tables(rel_bias, seq_len):
    del seq_len
    nv = (NEAR_BLOCKS + 1) * BLOCK
    vec = rel_bias[_t5_bucket(jnp.arange(nv, dtype=I32))].astype(F32).T

    def window(lo, n):
        pad = max(0, -lo)
        body = vec[:, max(lo, 0):lo + n]
        return jnp.concatenate([jnp.broadcast_to(vec[:, :1], (vec.shape[0], pad)), body], axis=1)

    def toeplitz(g, rows, cols):
        w = rows + cols
        g2 = jnp.concatenate([g[:, rows - 1:rows - 1 + cols], g[:, :1], g[:, :rows - 1]], axis=1)
        flat = jnp.tile(g2, (1, rows))[:, :rows * (w - 1)]
        return flat.reshape(-1, rows, w - 1)[:, :, :cols]

    bias0 = jnp.transpose(toeplitz(window(BLOCK - (2 * BLOCK - 1), 3 * BLOCK - 1), 2 * BLOCK, BLOCK), (0, 2, 1))
    tiles = [toeplitz(window(dl * BLOCK - (BLOCK - 1), 2 * BLOCK - 1), BLOCK, BLOCK) for dl in range(NEAR_BLOCKS)]
    tab = jnp.stack(tiles, axis=1) - rel_bias[NUM_BUCKETS - 1].astype(F32)[:, None, None, None]
    tab = jnp.concatenate([tab, jnp.zeros((tab.shape[0], 1, BLOCK, BLOCK), F32)], axis=1)
    return bias0, tab


def kernel(x, rel_bias, norm_g, ev_w_in, ev_w_out, ev_q_norm_g, ev_k_norm_g, ev_sinks, ev_ssm_log_dt, ev_ssm_a_re,
           ev_ssm_a_im, ev_ssm_b_re, ev_ssm_b_im, ev_ssm_c_re, ev_ssm_c_im, ev_ssm_d, ev_glu_w, ev_glu_b, od_w_in,
           od_w_out, od_q_norm_g, od_k_norm_g):
    b, l, d = x.shape
    assert l % KEY_CHUNK == 0 and l % ROW_TILE == 0
    assert (NEAR_BLOCKS - 1) * BLOCK + 1 >= 16 * 64 ** (15 / 16) + 1
    bias0, tab = _bias_tables(rel_bias, l)

    q0, k0, v0, sga, u, sgb = _proj0(x.reshape(b * l, d), norm_g[0][None, :], ev_w_in[0].astype(BF16))
    shp = lambda a: a.reshape(b, l, a.shape[-1])
    qg2 = jnp.tile(ev_q_norm_g[0], 2)[None, :]
    kg2 = jnp.tile(ev_k_norm_g[0], 2)[None, :]
    sinks = jnp.broadcast_to(ev_sinks[0][:, None], (A_HEADS, LANES)).astype(F32)
    att0 = _attn0(shp(q0), shp(k0), shp(v0), shp(sga), bias0, sinks, qg2, kg2)
    bmat, cre, cim, sc = _s5_prep(ev_ssm_log_dt[0], ev_ssm_a_re[0], ev_ssm_a_im[0], ev_ssm_b_re[0], ev_ssm_b_im[0],
                                  ev_ssm_c_re[0], ev_ssm_c_im[0])
    ssm0 = _ssm(shp(u), shp(sgb), bmat, cre, cim, sc, ev_ssm_d[0].reshape(1, -1), ev_glu_w[0].astype(BF16),
                ev_glu_b[0][None, :])

    w1 = od_w_in[0]
    cw = C_HEADS * C_HEAD_DIM
    ckv = C_KV_HEADS * C_HEAD_DIM
    o = np.cumsum([0, cw, ckv, ckv, cw, IDX_HEADS * IDX_DIM, IDX_DIM, IDX_HEADS])
    wq, wk, wv, wg, wqi, wki, ww = (w1[:, o[n]:o[n + 1]] for n in range(7))
    zki = jnp.zeros((d, LANES - IDX_DIM), w1.dtype)
    wki2 = jnp.concatenate([wki, zki, zki, wki], axis=1)
    bf = lambda a: a.astype(BF16)
    h1, q1, k1, vt1, sg1, qi1, ki2, wt1 = _mid(
        x, att0, ssm0, bf(ev_w_out[0]), norm_g[1][None, :], bf(wq), bf(wk), bf(wv.T), bf(wg), bf(wqi), bf(wki2),
        bf(ww.T), od_q_norm_g[0][None, :], od_k_norm_g[0][None, :])
    lb = (1.02 * C_HEAD_DIM ** 0.5 * jnp.max(jnp.abs(od_q_norm_g[0])) * jnp.max(jnp.abs(od_k_norm_g[0]))
          + jnp.max(tab))
    att1 = _dsa(q1, qi1, wt1, sg1, k1, vt1, ki2, tab, jnp.full((1, LANES), lb, F32))
    out = _outproj(h1.reshape(b * l, d), att1.reshape(b * l, cw), bf(od_w_out[0]))
    return out.reshape(b, l, d)
```

```python
import functools
import math

import jax
import jax.numpy as jnp
import numpy as np
from jax import lax
from jax.experimental import pallas as pl
from jax.experimental.pallas import tpu as pltpu

F32 = jnp.float32
BF16 = jnp.bfloat16
I32 = jnp.int32

LANES = 128
SUBLANES = 8
VMEM_LIMIT = 56 * 1024 * 1024

BLOCK = 128
WINDOW = 128
A_HEADS = 8
A_HEAD_DIM = 64
A_KV_HEADS = 2
A_WIDTH = A_HEADS * A_HEAD_DIM
SSM_GROUP = 16
SSM_STATE = 64
C_HEADS = 8
C_HEAD_DIM = 128
C_KV_HEADS = 2
IDX_HEADS = 8
IDX_DIM = 64
TOPK_MAX = 256
NUM_BUCKETS = 32
REL_MAX_DIST = 1024
EPS = 1e-6
NEG_INF = -1e30
INT_MIN = -(2 ** 31)
KEY_MIN_NORMAL = 0x00800000
KEY_POS_INF = 0x7F800000
KEY_NEG_INF = INT_MIN + 0x00800000

ROW_TILE = 1024
KEY_CHUNK = 1024
NEAR_BLOCKS = 8
FOLD_CHAINS = 8
COUNT_ROWS = 512
SUM_FLOOR = 1e-30
CAND = 32
STREAMS = 2
NT_DIMS = (((1,), (1,)), ((), ()))


def _t5_bucket(dist):
    n = jnp.maximum(dist, 0)
    max_exact = NUM_BUCKETS // 2
    nf = jnp.maximum(n, 1).astype(F32)
    large = max_exact + (jnp.log(nf / max_exact) / math.log(REL_MAX_DIST / max_exact)
                         * (NUM_BUCKETS - max_exact)).astype(I32)
    large = jnp.minimum(large, NUM_BUCKETS - 1)
    return jnp.where(n < max_exact, n, large)


def _silu(x):
    return x * jax.nn.sigmoid(x)


def _rms(x, g):
    ms = jnp.mean(x * x, axis=-1, keepdims=True)
    return x * lax.rsqrt(ms + EPS) * g


def _mm(a, b):
    return jnp.dot(a, b, preferred_element_type=F32)


def _mm_nt(a, b):
    return lax.dot_general(a, b, NT_DIMS, preferred_element_type=F32)


def _fold(x, op):
    n = x.shape[0] // SUBLANES
    chains = min(FOLD_CHAINS, n)
    accs = [x[r * SUBLANES:(r + 1) * SUBLANES] for r in range(chains)]
    for r in range(chains, n):
        accs[r % chains] = op(accs[r % chains], x[r * SUBLANES:(r + 1) * SUBLANES])
    while len(accs) > 1:
        accs = [op(a, b) for a, b in zip(accs[::2], accs[1::2])] + accs[len(accs) & ~1:]
    return accs[0]


def _params(*sem):
    return pltpu.CompilerParams(dimension_semantics=sem, vmem_limit_bytes=VMEM_LIMIT)


def _const_spec(shape):
    zeros = (0,) * len(shape)
    return pl.BlockSpec(shape, lambda *_: zeros)


def _proj0_kernel(x_ref, g_ref, w_ref, q_ref, k_ref, v_ref, sga_ref, u_ref, sgb_ref):
    hn = _rms(x_ref[...], g_ref[...]).astype(BF16)

    def mm(lo, hi):
        return _mm(hn, w_ref[:, lo:hi])

    q_ref[...] = mm(0, 512)
    k_ref[...] = mm(512, 640)
    v_ref[...] = mm(640, 768)
    sga_ref[...] = _silu(mm(768, 1280)).astype(BF16)
    u_ref[...] = mm(1280, 1792)
    sgb_ref[...] = _silu(mm(1792, 2304)).astype(BF16)


def _proj0(x2, g, w):
    rows, d = x2.shape
    t = ROW_TILE

    def row(n):
        return pl.BlockSpec((t, n), lambda i: (i, 0))

    return pl.pallas_call(
        _proj0_kernel,
        grid=(rows // t,),
        in_specs=[row(d), _const_spec((1, d)), _const_spec(w.shape)],
        out_specs=[row(512), row(128), row(128), row(512), row(512), row(512)],
        out_shape=[jax.ShapeDtypeStruct((rows, 512), F32), jax.ShapeDtypeStruct((rows, 128), F32),
                   jax.ShapeDtypeStruct((rows, 128), F32), jax.ShapeDtypeStruct((rows, 512), BF16),
                   jax.ShapeDtypeStruct((rows, 512), F32), jax.ShapeDtypeStruct((rows, 512), BF16)],
        compiler_params=_params("arbitrary"),
        name="proj0",
    )(x2, g, w)


def _attn0_kernel(q_ref, kc_ref, kp_ref, vc_ref, vp_ref, sga_ref, bias_ref, sink_ref, qg_ref, kg_ref, o_ref):
    i = pl.program_id(1)
    lane = lax.broadcasted_iota(I32, (1, LANES), 1)
    lo = lane < A_HEAD_DIM

    def segnorm(x, g2):
        sq = x * x
        s_lo = jnp.sum(jnp.where(lo, sq, 0.0), axis=-1, keepdims=True)
        s_hi = jnp.sum(jnp.where(lo, 0.0, sq), axis=-1, keepdims=True)
        inv = jnp.where(lo, lax.rsqrt(s_lo / A_HEAD_DIM + EPS), lax.rsqrt(s_hi / A_HEAD_DIM + EPS))
        return x * inv * g2

    kn = segnorm(jnp.concatenate([kp_ref[0], kc_ref[0]], axis=0), kg_ref[...])
    vb = jnp.concatenate([vp_ref[0], vc_ref[0]], axis=0)
    kr = pltpu.roll(kn, A_HEAD_DIM, axis=1)
    vr = pltpu.roll(vb, A_HEAD_DIM, axis=1)

    def variants(x, xr):
        return {(0, 0): jnp.where(lo, x, 0.0).astype(BF16), (0, 1): jnp.where(lo, 0.0, xr).astype(BF16),
                (1, 0): jnp.where(lo, xr, 0.0).astype(BF16), (1, 1): jnp.where(lo, 0.0, x).astype(BF16)}

    kvar = variants(kn, kr)
    vvar = variants(vb, vr)

    row = lax.broadcasted_iota(I32, (BLOCK, 2 * BLOCK), 0)
    col = lax.broadcasted_iota(I32, (BLOCK, 2 * BLOCK), 1)
    d = row + BLOCK - col
    mask = (d >= 0) & (d < WINDOW) & ((i > 0) | (col >= BLOCK))

    lgs, sinks = [], []
    for p in range(A_HEADS // 2):
        qp = (segnorm(q_ref[0, :, p * LANES:(p + 1) * LANES], qg_ref[...]) * (A_HEAD_DIM ** -0.5)).astype(BF16)
        for a in range(2):
            h = 2 * p + a
            lgs.append(jnp.where(mask, _mm_nt(qp, kvar[(p // 2, a)]) + bias_ref[h], NEG_INF))
            sinks.append(jnp.broadcast_to(sink_ref[h:h + 1, 0:1], (BLOCK, 1)))
    lg = jnp.concatenate(lgs, axis=0)
    sink = jnp.concatenate(sinks, axis=0)
    m = jnp.maximum(jnp.max(lg, axis=-1, keepdims=True), sink)
    e = jnp.exp(lg - m)
    den = jnp.sum(e, axis=-1, keepdims=True) + jnp.exp(sink - m)
    pr = (e / den).astype(BF16)
    for p in range(A_HEADS // 2):
        sl = slice(p * LANES, (p + 1) * LANES)
        acc = jnp.zeros((BLOCK, LANES), F32)
        for a in range(2):
            h = 2 * p + a
            acc = acc + _mm(pr[h * BLOCK:(h + 1) * BLOCK], vvar[(p // 2, a)])
        o_ref[0, :, sl] = (acc * sga_ref[0, :, sl].astype(F32)).astype(BF16)


def _attn0(q, k, v, sga, bias0, sinks, qg2, kg2):
    b, l, _ = q.shape
    nb = l // BLOCK

    def cur(n):
        return pl.BlockSpec((1, BLOCK, n), lambda bb, i: (bb, i, 0))

    def prev(n):
        return pl.BlockSpec((1, BLOCK, n), lambda bb, i: (bb, jnp.maximum(i - 1, 0), 0))

    return pl.pallas_call(
        _attn0_kernel,
        grid=(b, nb),
        in_specs=[cur(512), cur(128), prev(128), cur(128), prev(128), cur(512),
                  _const_spec(bias0.shape), _const_spec(sinks.shape), _const_spec(qg2.shape), _const_spec(kg2.shape)],
        out_specs=cur(512),
        out_shape=jax.ShapeDtypeStruct((b, l, 512), BF16),
        compiler_params=_params("arbitrary", "arbitrary"),
        name="attn0",
    )(q, k, k, v, v, sga, bias0, sinks, qg2, kg2)


def _ssm_kernel(u_ref, sgb_ref, bmat_ref, cre_ref, cim_ref, sc_ref, d_ref, gw_ref, gb_ref, o_ref, xre_ref, xim_ref):
    t = u_ref.shape[1]
    nq = bmat_ref.shape[0]
    half = bmat_ref.shape[2] // 2

    @pl.when(pl.program_id(1) == 0)
    def _():
        xre_ref[0:SUBLANES, :] = jnp.zeros((SUBLANES, xre_ref.shape[1]), F32)
        xim_ref[0:SUBLANES, :] = jnp.zeros((SUBLANES, xim_ref.shape[1]), F32)

    u = u_ref[0]
    ub = u.astype(BF16)
    for q in range(nq):
        bu = _mm(ub[:, q * LANES:(q + 1) * LANES], bmat_ref[q])
        xre_ref[SUBLANES:, q * half:(q + 1) * half] = bu[:, :half]
        xim_ref[SUBLANES:, q * half:(q + 1) * half] = bu[:, half:]

    def scan(r, _):
        base = pl.multiple_of(SUBLANES + r * SUBLANES, SUBLANES)
        xr = xre_ref[pl.ds(base, SUBLANES), :]
        xi = xim_ref[pl.ds(base, SUBLANES), :]
        for s, k in enumerate((1, 2, 4)):
            ar = sc_ref[2 * s]
            ai = sc_ref[2 * s + 1]
            sr = pltpu.roll(xr, k, axis=0)
            si = pltpu.roll(xi, k, axis=0)
            xr, xi = xr + ar * sr - ai * si, xi + ar * si + ai * sr
        cr = xre_ref[pl.ds(base - 1, 1), :]
        ci = xim_ref[pl.ds(base - 1, 1), :]
        pr = sc_ref[6]
        pi = sc_ref[7]
        xre_ref[pl.ds(base, SUBLANES), :] = xr + pr * cr - pi * ci
        xim_ref[pl.ds(base, SUBLANES), :] = xi + pr * ci + pi * cr
        return 0

    lax.fori_loop(0, t // SUBLANES, scan, 0, unroll=2)
    xre_ref[0:SUBLANES, :] = xre_ref[t:t + SUBLANES, :]
    xim_ref[0:SUBLANES, :] = xim_ref[t:t + SUBLANES, :]

    ys = []
    for q in range(nq):
        xr = xre_ref[SUBLANES:, q * half:(q + 1) * half].astype(BF16)
        xi = xim_ref[SUBLANES:, q * half:(q + 1) * half].astype(BF16)
        ys.append(_mm(xr, cre_ref[q]) + _mm(xi, cim_ref[q]))
    y = jnp.concatenate(ys, axis=1) + d_ref[...] * u
    y = jax.nn.gelu(y).astype(BF16)
    hh = _mm(y, gw_ref[...]) + gb_ref[...]
    w = hh.shape[1] // 2
    o_ref[0] = (hh[:, :w] * jax.nn.sigmoid(hh[:, w:]) * sgb_ref[0].astype(F32)).astype(BF16)


def _ssm(u, sgb, bmat, cre, cim, sc, dskip, gw, gb):
    b, l, w = u.shape
    t = ROW_TILE
    ns = sc.shape[-1]

    def row(n):
        return pl.BlockSpec((1, t, n), lambda bb, i: (bb, i, 0))

    return pl.pallas_call(
        _ssm_kernel,
        grid=(b, l // t),
        in_specs=[row(w), row(w), _const_spec(bmat.shape), _const_spec(cre.shape), _const_spec(cim.shape),
                  _const_spec(sc.shape), _const_spec(dskip.shape), _const_spec(gw.shape), _const_spec(gb.shape)],
        out_specs=row(w),
        out_shape=jax.ShapeDtypeStruct((b, l, w), BF16),
        scratch_shapes=[pltpu.VMEM((SUBLANES + t, ns), F32), pltpu.VMEM((SUBLANES + t, ns), F32)],
        compiler_params=_params("arbitrary", "arbitrary"),
        name="ssm",
    )(u, sgb, bmat, cre, cim, sc, dskip, gw, gb)


def _s5_prep(log_dt, a_re, a_im, b_re, b_im, c_re, c_im):
    g, p = a_re.shape
    h = b_re.shape[-1]
    gl = LANES // h
    nq = g // gl
    dt = jnp.exp(log_dt)[:, None]
    mag = jnp.exp(a_re * dt)
    ang = a_im * dt
    ab_re = mag * jnp.cos(ang)
    ab_im = mag * jnp.sin(ang)
    den = a_re * a_re + a_im * a_im
    n_re = ab_re - 1.0
    n_im = ab_im
    f_re = (n_re * a_re + n_im * a_im) / den
    f_im = (n_im * a_re - n_re * a_im) / den
    bb_re = f_re[..., None] * b_re - f_im[..., None] * b_im
    bb_im = f_re[..., None] * b_im + f_im[..., None] * b_re
    eye = jnp.eye(gl, dtype=F32)

    def bdiag_in(m):
        m = m.reshape(nq, gl, p, h)
        return jnp.einsum('qgph,gk->qghkp', m, eye).reshape(nq, gl * h, gl * p)

    def bdiag_out(m):
        m = m.reshape(nq, gl, h, p)
        return jnp.einsum('qghp,gk->qgpkh', m, eye).reshape(nq, gl * p, gl * h)

    bmat = jnp.concatenate([bdiag_in(bb_re), bdiag_in(bb_im)], axis=2).astype(BF16)
    cre = bdiag_out(c_re).astype(BF16)
    cim = bdiag_out(-c_im).astype(BF16)

    pw = [(ab_re.reshape(-1), ab_im.reshape(-1))]
    for _ in range(SUBLANES - 1):
        pr, pi = pw[-1]
        pw.append((pr * pw[0][0] - pi * pw[0][1], pr * pw[0][1] + pi * pw[0][0]))
    rows = jnp.arange(SUBLANES)[:, None]
    sc = []
    for k in (1, 2, 4):
        sc.append(jnp.where(rows >= k, pw[k - 1][0][None, :], 0.0))
        sc.append(jnp.where(rows >= k, pw[k - 1][1][None, :], 0.0))
    sc.append(jnp.stack([pw[r][0] for r in range(SUBLANES)]))
    sc.append(jnp.stack([pw[r][1] for r in range(SUBLANES)]))
    return bmat, cre, cim, jnp.stack(sc).astype(F32)


def _mid_kernel(x_ref, a_ref, s_ref, wo_ref, g_ref, wq_ref, wk_ref, wvt_ref, wg_ref, wqi_ref, wki_ref, wwt_ref,
                qg_ref, kg_ref, h_ref, q_ref, k_ref, vt_ref, sg_ref, qi_ref, ki_ref, wt_ref):
    aw = a_ref.shape[2]
    h = x_ref[0] + _mm(a_ref[0], wo_ref[0:aw, :]) + _mm(s_ref[0], wo_ref[aw:, :])
    h_ref[0] = h
    hn = _rms(h, g_ref[...]).astype(BF16)
    qf = _mm(hn, wq_ref[...])
    for hd in range(C_HEADS):
        sl = slice(hd * C_HEAD_DIM, (hd + 1) * C_HEAD_DIM)
        q_ref[0, :, sl] = (_rms(qf[:, sl], qg_ref[...]) * (C_HEAD_DIM ** -0.5)).astype(BF16)
    kf = _mm(hn, wk_ref[...])
    for hd in range(C_KV_HEADS):
        sl = slice(hd * C_HEAD_DIM, (hd + 1) * C_HEAD_DIM)
        k_ref[0, :, sl] = _rms(kf[:, sl], kg_ref[...]).astype(BF16)
    vt_ref[0] = _mm_nt(wvt_ref[...], hn).astype(BF16)
    sg_ref[0] = _silu(_mm(hn, wg_ref[...])).astype(BF16)
    qi_ref[0] = _mm(hn, wqi_ref[...]).astype(BF16)
    ki_ref[0] = _mm(hn, wki_ref[...]).astype(BF16)
    wt_ref[0] = _mm_nt(wwt_ref[...], hn) * ((IDX_HEADS ** -0.5) * (IDX_DIM ** -0.5))


def _mid(x, att0, ssm0, wo, g, wq, wk, wvt, wg, wqi, wki2, wwt, qg, kg):
    b, l, d = x.shape
    t = ROW_TILE

    def row(n):
        return pl.BlockSpec((1, t, n), lambda bb, i: (bb, i, 0))

    def col(n):
        return pl.BlockSpec((1, n, t), lambda bb, i: (bb, 0, i))

    weights = [wo, g, wq, wk, wvt, wg, wqi, wki2, wwt, qg, kg]
    cw = C_HEADS * C_HEAD_DIM
    ckv = C_KV_HEADS * C_HEAD_DIM
    return pl.pallas_call(
        _mid_kernel,
        grid=(b, l // t),
        in_specs=[row(d), row(att0.shape[2]), row(ssm0.shape[2])] + [_const_spec(w.shape) for w in weights],
        out_specs=[row(d), row(cw), row(ckv), col(ckv), row(cw), row(IDX_HEADS * IDX_DIM), row(2 * LANES),
                   col(IDX_HEADS)],
        out_shape=[jax.ShapeDtypeStruct((b, l, d), F32), jax.ShapeDtypeStruct((b, l, cw), BF16),
                   jax.ShapeDtypeStruct((b, l, ckv), BF16), jax.ShapeDtypeStruct((b, ckv, l), BF16),
                   jax.ShapeDtypeStruct((b, l, cw), BF16), jax.ShapeDtypeStruct((b, l, IDX_HEADS * IDX_DIM), BF16),
                   jax.ShapeDtypeStruct((b, l, 2 * LANES), BF16), jax.ShapeDtypeStruct((b, IDX_HEADS, l), F32)],
        compiler_params=_params("arbitrary", "arbitrary"),
        name="mid",
    )(x, att0, ssm0, *weights)


def _dsa_kernel(q_ref, qi_ref, wt_ref, sg_ref, k_ref, vt_ref, ki_ref, tab_ref, lb_ref, o_ref,
                sc_ref, best_ref, x_ref, acc_ref, *, seq_len, topk):
    i = pl.program_id(1)
    ck = KEY_CHUNK
    per = ck // BLOCK
    nch = (i + per) // per
    t_row = i * BLOCK + lax.broadcasted_iota(I32, (1, LANES), 1)
    kiota = lax.broadcasted_iota(I32, (ck, LANES), 0)

    def chunk_off(c):
        return pl.multiple_of(c * ck, ck)

    qi = qi_ref[0]
    qi_stack = [jnp.concatenate([qi[:, (2 * s) * LANES:(2 * s + 1) * LANES],
                                 qi[:, (2 * s + 1) * LANES:(2 * s + 2) * LANES]], axis=0) for s in range(2)]
    wt = wt_ref[0]

    def score_chunk(c, masked):
        off = chunk_off(c)
        sc = jnp.zeros((ck, LANES), F32)
        for a in range(2):
            kk = ki_ref[0, pl.ds(off, ck), a * LANES:(a + 1) * LANES]
            for s in range(2):
                r = _mm_nt(kk, qi_stack[s])
                for j in range(2):
                    hd = 2 * (2 * s + j) + a
                    sc = sc + jnp.maximum(r[:, j * LANES:(j + 1) * LANES], 0.0) * wt[hd:hd + 1, :]
        if masked:
            sc = jnp.where(off + kiota <= t_row, sc, NEG_INF)
        sc_ref[pl.ds(off, ck), :] = sc
        return _fold(sc, jnp.maximum)

    def score_body(c, mx):
        return jnp.maximum(mx, score_chunk(c, False))

    smax = lax.fori_loop(0, nch - 1, score_body, jnp.full((SUBLANES, LANES), NEG_INF, F32))
    smax = jnp.max(jnp.maximum(smax, score_chunk(nch - 1, True)), axis=0, keepdims=True)

    def count(*preds):
        rows = COUNT_ROWS
        sub = ck // rows

        def body(c, accs):
            out = []
            for u in range(sub):
                off = pl.multiple_of(c * ck + u * rows, rows)
                s = sc_ref[pl.ds(off, rows), :]
                for n, pred in enumerate(preds):
                    ind = pred(s, off).astype(I32)
                    out.append(accs[u * len(preds) + n]
                               + jnp.sum(ind.reshape(rows // SUBLANES, SUBLANES, LANES), axis=0))
            return tuple(out)

        accs = lax.fori_loop(0, nch, body, tuple(jnp.zeros((SUBLANES, LANES), I32) for _ in range(sub * len(preds))))
        res = [jnp.sum(sum(accs[n::len(preds)]), axis=0, keepdims=True) for n in range(len(preds))]
        return res[0] if len(preds) == 1 else res

    def key_value(k):
        return pltpu.bitcast(jnp.where(k < 0, INT_MIN - k, k), F32)

    def count_ge(k):
        thr = key_value(k)
        return count(lambda s, off: s >= thr)

    def full(v):
        return jnp.full((1, LANES), v, I32)

    def bisect(_, st):
        lo, hi, c_lo, c_hi = st
        mid = (lo >> 1) + (hi >> 1) + (lo & hi & 1)
        c = count_ge(mid)
        ge = c >= topk
        return jnp.where(ge, mid, lo), jnp.where(ge, hi, mid), jnp.where(ge, c, c_lo), jnp.where(ge, c_hi, c)

    searching = (i + 1) * BLOCK > topk

    def float_key(x):
        bits = pltpu.bitcast(x, I32)
        return jnp.where(bits < 0, INT_MIN - bits, bits)

    def search():
        k_lo = float_key(smax * 0.125)
        c = count_ge(k_lo)
        ok = (smax > 0.0) & (c >= topk)
        trips = jnp.where(jnp.min(jnp.where(ok, 1, 0)) > 0, 25, 32)
        st = (jnp.where(ok, k_lo, KEY_NEG_INF), float_key(smax) + 1, jnp.where(ok, c, nch * ck), full(0))
        out = lax.fori_loop(0, trips, bisect, st)
        return out[0], out[2], out[3]

    def exchange(v, a, b):
        v[a], v[b] = jnp.maximum(v[a], v[b]), jnp.minimum(v[a], v[b])

    def sort_desc(v):
        n, k = len(v), 2
        while k <= n:
            j = k // 2
            while j >= 1:
                for a in range(n):
                    b = a ^ j
                    if b > a:
                        exchange(v, *((a, b) if (a & k) == 0 else (b, a)))
                j //= 2
            k *= 2

    def merge_top(best, blk):
        n = len(best)
        v = [jnp.maximum(best[r], blk[n - 1 - r]) for r in range(n)]
        j = n // 2
        while j >= 1:
            for a in range(n):
                if a ^ j > a:
                    exchange(v, a, a ^ j)
            j //= 2
        return v

    crow = STREAMS * CAND * SUBLANES

    def cand_body(c, _):
        blk_all = sc_ref[pl.ds(pl.multiple_of(c * crow, crow), crow), :]
        for st in range(STREAMS):
            blk = [blk_all[(STREAMS * r + st) * SUBLANES:(STREAMS * r + st + 1) * SUBLANES] for r in range(CAND)]
            sort_desc(blk)
            base = st * CAND * SUBLANES
            best = [best_ref[base + r * SUBLANES:base + (r + 1) * SUBLANES, :] for r in range(CAND)]
            for r, x in enumerate(merge_top(best, blk)):
                best_ref[base + r * SUBLANES:base + (r + 1) * SUBLANES, :] = x
        return 0

    def cand_search():
        best_ref[...] = jnp.full(best_ref.shape, -jnp.inf, F32)
        lax.fori_loop(0, nch * (ck // crow), cand_body, 0)

        def count_cand(k):
            thr = key_value(k)
            parts = [jnp.sum((best_ref[r:r + COUNT_ROWS, :] >= thr).astype(I32)
                             .reshape(COUNT_ROWS // SUBLANES, SUBLANES, LANES), axis=0)
                     for r in range(0, crow, COUNT_ROWS)]
            return jnp.sum(sum(parts), axis=0, keepdims=True)

        def step(_, st):
            lo, hi = st
            mid = (lo >> 1) + (hi >> 1) + (lo & hi & 1)
            ge = count_cand(mid) >= topk
            return jnp.where(ge, mid, lo), jnp.where(ge, hi, mid)

        k_lo = float_key(smax * 0.125)
        ok = (smax > 0.0) & (count_cand(k_lo) >= topk)
        trips = jnp.where(jnp.min(jnp.where(ok, 1, 0)) > 0, 25, 32)
        vk, _ = lax.fori_loop(0, trips, step, (jnp.where(ok, k_lo, KEY_NEG_INF), float_key(smax) + 1))
        thr = key_value(vk)
        above_cand = jnp.sum((best_ref[...] > thr).astype(I32), axis=0, keepdims=True)
        c_ge, c_gt = count(lambda s, off: s >= thr, lambda s, off: s > thr)
        complete = jnp.min(jnp.where(c_gt == above_cand, 1, 0)) > 0
        return lax.cond(complete, lambda: (vk, c_ge, c_gt), search)

    vkey, c_lo, c_hi = lax.cond(searching, cand_search, lambda: (full(KEY_NEG_INF), full(topk), full(0)))
    vthr = key_value(vkey)
    need = topk - c_hi
    ties = c_lo - c_hi

    def tie_search():
        nxt = vkey + 1
        nxt = jnp.where((nxt > 0) & (nxt < KEY_MIN_NORMAL), KEY_MIN_NORMAL, nxt)
        step = key_value(nxt) - vthr

        def split(_, st):
            fl, fh = st
            fm = 0.5 * (fl + fh)
            thr = vthr + fm * step
            ge = count(lambda s, off: s >= thr) >= topk
            return jnp.where(ge, fm, fl), jnp.where(ge, fh, fm)

        fl, _ = lax.fori_loop(0, 26, split, (jnp.zeros((1, LANES), F32), jnp.ones((1, LANES), F32)))
        thr = vthr + fl * step
        want = topk - count(lambda s, off: s > thr)

        def body(_, st):
            lj, hj = st
            mid = (lj + hj) >> 1
            c = count(lambda s, off: (s == thr) & (off + kiota[:COUNT_ROWS] <= mid))
            ok = c >= want
            return jnp.where(ok, lj, mid), jnp.where(ok, mid, hj)

        _, hj = lax.fori_loop(0, 14, body, (full(-1), full(0) + (nch * ck - 1)))
        return thr, hj

    any_tie = searching & (jnp.max(ties - need) > 0)
    vthr, jmax = lax.cond(any_tie, tie_search, lambda: (vthr, full(seq_len)))

    def selection_mask(off):
        s = sc_ref[pl.ds(off, ck), :]
        s_idx = off + kiota
        sel = ((s > vthr) | ((s == vthr) & (s_idx <= jmax))) & (s_idx <= t_row)
        madd = jnp.where(sel, 0.0, NEG_INF)
        sc_ref[pl.ds(off, ck), :] = madd
        return madd

    q = q_ref[0]
    n_far = jnp.maximum((i - NEAR_BLOCKS + 1) // per, 0)
    hpg = C_HEADS // C_KV_HEADS
    npair = C_HEADS // 2
    q_pairs = [jnp.concatenate([q[:, (2 * j) * LANES:(2 * j + 1) * LANES],
                                q[:, (2 * j + 1) * LANES:(2 * j + 2) * LANES]], axis=0) for j in range(npair)]

    def bias_rows(hd, c):
        return jnp.concatenate([tab_ref[hd, jnp.clip(i - (c * per + r), 0, NEAR_BLOCKS)] for r in range(per)],
                               axis=0)

    def emit(hd, num, den):
        sl = slice(hd * LANES, (hd + 1) * LANES)
        o_ref[0, :, sl] = ((num / den).T * sg_ref[0, :, sl].astype(F32)).astype(BF16)

    def exact_attention():
        for g in range(C_KV_HEADS):
            def stage_body(near, g=g):
                def body(c, mx):
                    off = chunk_off(c)
                    madd = sc_ref[pl.ds(off, ck), :]
                    kc = k_ref[0, pl.ds(off, ck), g * LANES:(g + 1) * LANES]
                    out = []
                    for jj in range(hpg // 2):
                        lg = _mm_nt(kc, q_pairs[g * (hpg // 2) + jj])
                        for a in range(2):
                            hl = 2 * jj + a
                            x = lg[:, a * LANES:(a + 1) * LANES] + madd
                            if near:
                                x = x + bias_rows(hpg * g + hl, c)
                            x_ref[hl, pl.ds(off, ck), :] = x
                            out.append(jnp.maximum(mx[hl], _fold(x, jnp.maximum)))
                    return tuple(out)

                return body

            mx = tuple(jnp.full((SUBLANES, LANES), NEG_INF, F32) for _ in range(hpg))
            mx = lax.fori_loop(0, n_far, stage_body(False), mx)
            mx = lax.fori_loop(n_far, nch, stage_body(True), mx)
            m = [jnp.max(v, axis=0, keepdims=True) for v in mx]
            acc_ref[...] = jnp.zeros(acc_ref.shape, F32)

            def att_body(c, ls, g=g, m=m):
                off = chunk_off(c)
                vt = vt_ref[0, g * LANES:(g + 1) * LANES, pl.ds(off, ck)]
                out = []
                for jj in range(hpg // 2):
                    ps = []
                    for a in range(2):
                        hl = 2 * jj + a
                        p = jnp.exp(x_ref[hl, pl.ds(off, ck), :] - m[hl])
                        out.append(ls[hl] + _fold(p, jnp.add))
                        ps.append(p.astype(BF16))
                    acc_ref[jj] += _mm(vt, jnp.concatenate(ps, axis=1))
                return tuple(out)

            ls = lax.fori_loop(0, nch, att_body, tuple(jnp.zeros((SUBLANES, LANES), F32) for _ in range(hpg)))
            for hl in range(hpg):
                emit(hpg * g + hl, acc_ref[hl // 2, :, (hl % 2) * LANES:(hl % 2 + 1) * LANES],
                     jnp.sum(ls[hl], axis=0, keepdims=True))

    lb = lb_ref[...]
    acc_ref[...] = jnp.zeros(acc_ref.shape, F32)

    def stage(c, g, near):
        off = chunk_off(c)
        mb = (selection_mask(off) if g == 0 else sc_ref[pl.ds(off, ck), :]) - lb
        kc = k_ref[0, pl.ds(off, ck), g * LANES:(g + 1) * LANES]
        for jj in range(hpg // 2):
            lg = _mm_nt(kc, q_pairs[g * (hpg // 2) + jj])
            for a in range(2):
                hl = 2 * jj + a
                x = lg[:, a * LANES:(a + 1) * LANES] + mb
                if near:
                    x = x + bias_rows(hpg * g + hl, c)
                x_ref[hl, pl.ds(off, ck), :] = x

    def consume(c, g, ls):
        off = chunk_off(c)
        vt = vt_ref[0, g * LANES:(g + 1) * LANES, pl.ds(off, ck)]
        out = []
        for jj in range(hpg // 2):
            ps = []
            for a in range(2):
                hl = 2 * jj + a
                p = jnp.exp(x_ref[hl, pl.ds(off, ck), :])
                out.append(ls[hl] + _fold(p, jnp.add))
                ps.append(p.astype(BF16))
            acc_ref[g * (hpg // 2) + jj] += _mm(vt, jnp.concatenate(ps, axis=1))
        return tuple(out)

    dens = []
    for g in range(C_KV_HEADS):
        def step(near, g=g):
            def body(c, ls):
                out = consume(c, g, ls)
                stage(c + 1, g, near)
                return out

            return body

        stage(0, g, True)
        ls = tuple(jnp.zeros((SUBLANES, LANES), F32) for _ in range(hpg))
        split = jnp.maximum(n_far - 1, 0)
        ls = lax.fori_loop(0, split, step(False), ls)
        ls = lax.fori_loop(split, nch - 1, step(True), ls)
        ls = consume(nch - 1, g, ls)
        dens += [jnp.sum(v, axis=0, keepdims=True) for v in ls]
    in_range = jnp.min(functools.reduce(jnp.minimum, dens)) > SUM_FLOOR

    @pl.when(in_range)
    def _():
        for hd in range(C_HEADS):
            emit(hd, acc_ref[hd // 2, :, (hd % 2) * LANES:(hd % 2 + 1) * LANES], dens[hd])

    @pl.when(jnp.logical_not(in_range))
    def _():
        exact_attention()


def _dsa(q, qi, wt, sg, k, vt, ki2, tab, lb):
    b, l, cw = q.shape
    nb = l // BLOCK
    topk = min(TOPK_MAX, l // 4)

    def blk(n):
        return pl.BlockSpec((1, BLOCK, n), lambda bb, i: (bb, i, 0))

    def whole(s1, s2):
        return pl.BlockSpec((1, s1, s2), lambda bb, i: (bb, 0, 0), pipeline_mode=pl.Buffered(1))

    hpg = C_HEADS // C_KV_HEADS
    return pl.pallas_call(
        functools.partial(_dsa_kernel, seq_len=l, topk=topk),
        grid=(b, nb),
        in_specs=[blk(cw), blk(qi.shape[2]), pl.BlockSpec((1, IDX_HEADS, BLOCK), lambda bb, i: (bb, 0, i)), blk(cw),
                  whole(l, k.shape[2]), whole(vt.shape[1], l), whole(l, ki2.shape[2]),
                  pl.BlockSpec(tab.shape, lambda bb, i: (0, 0, 0, 0), pipeline_mode=pl.Buffered(1)),
                  _const_spec(lb.shape)],
        out_specs=blk(cw),
        out_shape=jax.ShapeDtypeStruct((b, l, cw), BF16),
        scratch_shapes=[pltpu.VMEM((l, LANES), F32), pltpu.VMEM((STREAMS * CAND * SUBLANES, LANES), F32),
                        pltpu.VMEM((hpg, l, LANES), F32),
                        pltpu.VMEM((C_HEADS // 2, C_HEAD_DIM, 2 * LANES), F32)],
        compiler_params=_params("arbitrary", "arbitrary"),
        name="dsa",
    )(q, qi, wt, sg, k, vt, ki2, tab, lb)


def _out_kernel(h_ref, a_ref, w_ref, o_ref):
    o_ref[...] = h_ref[...] + _mm(a_ref[...], w_ref[...])


def _outproj(h2, a2, w):
    rows, d = h2.shape
    t = ROW_TILE
    return pl.pallas_call(
        _out_kernel,
        grid=(rows // t,),
        in_specs=[pl.BlockSpec((t, d), lambda i: (i, 0)), pl.BlockSpec((t, a2.shape[1]), lambda i: (i, 0)),
                  _const_spec(w.shape)],
        out_specs=pl.BlockSpec((t, d), lambda i: (i, 0)),
        out_shape=jax.ShapeDtypeStruct((rows, d), F32),
        compiler_params=_params("arbitrary"),
        name="outproj1",
    )(h2, a2, w)


def _bias_tables(rel_bias, seq_len):
    del seq_len
    nv = (NEAR_BLOCKS + 1) * BLOCK
    vec = rel_bias[_t5_bucket(jnp.arange(nv, dtype=I32))].astype(F32).T

    def window(lo, n):
        pad = max(0, -lo)
        body = vec[:, max(lo, 0):lo + n]
        return jnp.concatenate([jnp.broadcast_to(vec[:, :1], (vec.shape[0], pad)), body], axis=1)

    def toeplitz(g, rows, cols):
        w = rows + cols
        g2 = jnp.concatenate([g[:, rows - 1:rows - 1 + cols], g[:, :1], g[:, :rows - 1]], axis=1)
        flat = jnp.tile(g2, (1, rows))[:, :rows * (w - 1)]
        return flat.reshape(-1, rows, w - 1)[:, :, :cols]

    bias0 = jnp.transpose(toeplitz(window(BLOCK - (2 * BLOCK - 1), 3 * BLOCK - 1), 2 * BLOCK, BLOCK), (0, 2, 1))
    tiles = [toeplitz(window(dl * BLOCK - (BLOCK - 1), 2 * BLOCK - 1), BLOCK, BLOCK) for dl in range(NEAR_BLOCKS)]
    tab = jnp.stack(tiles, axis=1) - rel_bias[NUM_BUCKETS - 1].astype(F32)[:, None, None, None]
    tab = jnp.concatenate([tab, jnp.zeros((tab.shape[0], 1, BLOCK, BLOCK), F32)], axis=1)
    return bias0, tab


def kernel(x, rel_bias, norm_g, ev_w_in, ev_w_out, ev_q_norm_g, ev_k_norm_g, ev_sinks, ev_ssm_log_dt, ev_ssm_a_re,
           ev_ssm_a_im, ev_ssm_b_re, ev_ssm_b_im, ev_ssm_c_re, ev_ssm_c_im, ev_ssm_d, ev_glu_w, ev_glu_b, od_w_in,
           od_w_out, od_q_norm_g, od_k_norm_g):
    b, l, d = x.shape
    assert l % KEY_CHUNK == 0 and l % ROW_TILE == 0
    assert (NEAR_BLOCKS - 1) * BLOCK + 1 >= 16 * 64 ** (15 / 16) + 1
    bias0, tab = _bias_tables(rel_bias, l)

    q0, k0, v0, sga, u, sgb = _proj0(x.reshape(b * l, d), norm_g[0][None, :], ev_w_in[0].astype(BF16))
    shp = lambda a: a.reshape(b, l, a.shape[-1])
    qg2 = jnp.tile(ev_q_norm_g[0], 2)[None, :]
    kg2 = jnp.tile(ev_k_norm_g[0], 2)[None, :]
    sinks = jnp.broadcast_to(ev_sinks[0][:, None], (A_HEADS, LANES)).astype(F32)
    att0 = _attn0(shp(q0), shp(k0), shp(v0), shp(sga), bias0, sinks, qg2, kg2)
    bmat, cre, cim, sc = _s5_prep(ev_ssm_log_dt[0], ev_ssm_a_re[0], ev_ssm_a_im[0], ev_ssm_b_re[0], ev_ssm_b_im[0],
                                  ev_ssm_c_re[0], ev_ssm_c_im[0])
    ssm0 = _ssm(shp(u), shp(sgb), bmat, cre, cim, sc, ev_ssm_d[0].reshape(1, -1), ev_glu_w[0].astype(BF16),
                ev_glu_b[0][None, :])

    w1 = od_w_in[0]
    cw = C_HEADS * C_HEAD_DIM
    ckv = C_KV_HEADS * C_HEAD_DIM
    o = np.cumsum([0, cw, ckv, ckv, cw, IDX_HEADS * IDX_DIM, IDX_DIM, IDX_HEADS])
    wq, wk, wv, wg, wqi, wki, ww = (w1[:, o[n]:o[n + 1]] for n in range(7))
    zki = jnp.zeros((d, LANES - IDX_DIM), w1.dtype)
    wki2 = jnp.concatenate([wki, zki, zki, wki], axis=1)
    bf = lambda a: a.astype(BF16)
    h1, q1, k1, vt1, sg1, qi1, ki2, wt1 = _mid(
        x, att0, ssm0, bf(ev_w_out[0]), norm_g[1][None, :], bf(wq), bf(wk), bf(wv.T), bf(wg), bf(wqi), bf(wki2),
        bf(ww.T), od_q_norm_g[0][None, :], od_k_norm_g[0][None, :])
    lb = (1.02 * C_HEAD_DIM ** 0.5 * jnp.max(jnp.abs(od_q_norm_g[0])) * jnp.max(jnp.abs(od_k_norm_g[0]))
          + jnp.max(tab))
    att1 = _dsa(q1, qi1, wt1, sg1, k1, vt1, ki2, tab, jnp.full((1, LANES), lb, F32))
    out = _outproj(h1.reshape(b * l, d), att1.reshape(b * l, cw), bf(od_w_out[0]))
    return out.reshape(b, l, d)
```

```python
import functools
import math

import jax
import jax.numpy as jnp
import numpy as np
from jax import lax
from jax.experimental import pallas as pl
from jax.experimental.pallas import tpu as pltpu

F32 = jnp.float32
BF16 = jnp.bfloat16
I32 = jnp.int32

LANES = 128
SUBLANES = 8
VMEM_LIMIT = 56 * 1024 * 1024

BLOCK = 128
WINDOW = 128
A_HEADS = 8
A_HEAD_DIM = 64
A_KV_HEADS = 2
A_WIDTH = A_HEADS * A_HEAD_DIM
SSM_GROUP = 16
SSM_STATE = 64
C_HEADS = 8
C_HEAD_DIM = 128
C_KV_HEADS = 2
IDX_HEADS = 8
IDX_DIM = 64
TOPK_MAX = 256
NUM_BUCKETS = 32
REL_MAX_DIST = 1024
EPS = 1e-6
NEG_INF = -1e30
INT_MIN = -(2 ** 31)
KEY_MIN_NORMAL = 0x00800000
KEY_POS_INF = 0x7F800000
KEY_NEG_INF = INT_MIN + 0x00800000

ROW_TILE = 1024
KEY_CHUNK = 1024
NEAR_BLOCKS = 8
FOLD_CHAINS = 8
COUNT_ROWS = 512
SUM_FLOOR = 1e-30
CAND = 32
STREAMS = 2
NT_DIMS = (((1,), (1,)), ((), ()))


def _t5_bucket(dist):
    n = jnp.maximum(dist, 0)
    max_exact = NUM_BUCKETS // 2
    nf = jnp.maximum(n, 1).astype(F32)
    large = max_exact + (jnp.log(nf / max_exact) / math.log(REL_MAX_DIST / max_exact)
                         * (NUM_BUCKETS - max_exact)).astype(I32)
    large = jnp.minimum(large, NUM_BUCKETS - 1)
    return jnp.where(n < max_exact, n, large)


def _silu(x):
    return x * jax.nn.sigmoid(x)


def _rms(x, g):
    ms = jnp.mean(x * x, axis=-1, keepdims=True)
    return x * lax.rsqrt(ms + EPS) * g


def _mm(a, b):
    return jnp.dot(a, b, preferred_element_type=F32)


def _mm_nt(a, b):
    return lax.dot_general(a, b, NT_DIMS, preferred_element_type=F32)


def _fold(x, op):
    n = x.shape[0] // SUBLANES
    chains = min(FOLD_CHAINS, n)
    accs = [x[r * SUBLANES:(r + 1) * SUBLANES] for r in range(chains)]
    for r in range(chains, n):
        accs[r % chains] = op(accs[r % chains], x[r * SUBLANES:(r + 1) * SUBLANES])
    while len(accs) > 1:
        accs = [op(a, b) for a, b in zip(accs[::2], accs[1::2])] + accs[len(accs) & ~1:]
    return accs[0]


def _params(*sem):
    return pltpu.CompilerParams(dimension_semantics=sem, vmem_limit_bytes=VMEM_LIMIT)


def _const_spec(shape):
    zeros = (0,) * len(shape)
    return pl.BlockSpec(shape, lambda *_: zeros)


def _proj0_kernel(x_ref, g_ref, w_ref, qg_ref, kg_ref, q_ref, k_ref, v_ref, sga_ref, u_ref, sgb_ref):
    hn = _rms(x_ref[...], g_ref[...]).astype(BF16)
    lo = lax.broadcasted_iota(I32, (1, LANES), 1) < A_HEAD_DIM

    def mm(n):
        return _mm(hn, w_ref[:, n * A_WIDTH:(n + 1) * A_WIDTH])

    def segnorm(x, g2):
        sq = x * x
        s_lo = jnp.sum(jnp.where(lo, sq, 0.0), axis=-1, keepdims=True)
        s_hi = jnp.sum(jnp.where(lo, 0.0, sq), axis=-1, keepdims=True)
        inv = jnp.where(lo, lax.rsqrt(s_lo / A_HEAD_DIM + EPS), lax.rsqrt(s_hi / A_HEAD_DIM + EPS))
        return x * inv * g2

    q, k = mm(0), mm(1)
    for p in range(A_WIDTH // LANES):
        sl = slice(p * LANES, (p + 1) * LANES)
        q_ref[:, sl] = (segnorm(q[:, sl], qg_ref[...]) * (A_HEAD_DIM ** -0.5)).astype(BF16)
        k_ref[:, sl] = segnorm(k[:, sl], kg_ref[...]).astype(BF16)
    v_ref[...] = mm(2).astype(BF16)
    sga_ref[...] = _silu(mm(3)).astype(BF16)
    u_ref[...] = mm(4)
    sgb_ref[...] = _silu(mm(5)).astype(BF16)


def _proj0(x2, g, w, qg2, kg2):
    rows, d = x2.shape
    t = ROW_TILE

    def row(n):
        return pl.BlockSpec((t, n), lambda i: (i, 0))

    n = A_WIDTH
    return pl.pallas_call(
        _proj0_kernel,
        grid=(rows // t,),
        in_specs=[row(d), _const_spec((1, d)), _const_spec(w.shape), _const_spec(qg2.shape), _const_spec(kg2.shape)],
        out_specs=[row(n)] * 6,
        out_shape=[jax.ShapeDtypeStruct((rows, n), BF16), jax.ShapeDtypeStruct((rows, n), BF16),
                   jax.ShapeDtypeStruct((rows, n), BF16), jax.ShapeDtypeStruct((rows, n), BF16),
                   jax.ShapeDtypeStruct((rows, n), F32), jax.ShapeDtypeStruct((rows, n), BF16)],
        compiler_params=_params("arbitrary"),
        name="proj0",
    )(x2, g, w, qg2, kg2)


def _attn0_kernel(q_ref, kc_ref, kp_ref, vc_ref, vp_ref, sga_ref, bias_ref, sink_ref, ones_ref, o_ref):
    i = pl.program_id(1)
    kb = jnp.concatenate([kp_ref[0], kc_ref[0]], axis=0)
    vb = jnp.concatenate([vp_ref[0], vc_ref[0]], axis=0)

    def variant(x, g, a):
        n = 2 * g + a
        return x[:, n * LANES:(n + 1) * LANES]

    row = lax.broadcasted_iota(I32, (BLOCK, 2 * BLOCK), 0)
    col = lax.broadcasted_iota(I32, (BLOCK, 2 * BLOCK), 1)
    d = row + BLOCK - col
    mask = (d >= 0) & (d < WINDOW) & ((i > 0) | (col >= BLOCK))

    lgs, sinks = [], []
    for p in range(A_HEADS // 2):
        qp = q_ref[0, :, p * LANES:(p + 1) * LANES]
        for a in range(2):
            h = 2 * p + a
            lgs.append(jnp.where(mask, _mm_nt(qp, variant(kb, p // 2, a)) + bias_ref[h], NEG_INF))
            sinks.append(jnp.broadcast_to(sink_ref[h:h + 1, 0:1], (BLOCK, 1)))
    lg = jnp.concatenate(lgs, axis=0)
    sink = jnp.concatenate(sinks, axis=0)
    m = jnp.maximum(jnp.max(lg, axis=-1, keepdims=True), sink)
    e = jnp.exp(lg - m).astype(BF16)
    inv = 1.0 / (_mm(e, ones_ref[...]) + jnp.exp(sink - m))
    for p in range(A_HEADS // 2):
        sl = slice(p * LANES, (p + 1) * LANES)
        acc = jnp.zeros((BLOCK, LANES), F32)
        for a in range(2):
            h = 2 * p + a
            hs = slice(h * BLOCK, (h + 1) * BLOCK)
            acc = acc + _mm(e[hs], variant(vb, p // 2, a)) * inv[hs]
        o_ref[0, :, sl] = (acc * sga_ref[0, :, sl].astype(F32)).astype(BF16)


def _attn0(q, k, v, sga, bias0, sinks):
    b, l, _ = q.shape
    nb = l // BLOCK
    ones = jnp.ones((2 * BLOCK, LANES), BF16)

    def cur(n):
        return pl.BlockSpec((1, BLOCK, n), lambda bb, i: (bb, i, 0))

    def prev(n):
        return pl.BlockSpec((1, BLOCK, n), lambda bb, i: (bb, jnp.maximum(i - 1, 0), 0))

    return pl.pallas_call(
        _attn0_kernel,
        grid=(b, nb),
        in_specs=[cur(512), cur(512), prev(512), cur(512), prev(512), cur(512),
                  _const_spec(bias0.shape), _const_spec(sinks.shape), _const_spec(ones.shape)],
        out_specs=cur(512),
        out_shape=jax.ShapeDtypeStruct((b, l, 512), BF16),
        compiler_params=_params("arbitrary", "arbitrary"),
        name="attn0",
    )(q, k, k, v, v, sga, bias0, sinks, ones)


def _ssm_kernel(u_ref, sgb_ref, bmat_ref, cre_ref, cim_ref, sc_ref, d_ref, gw_ref, gb_ref, o_ref, xre_ref, xim_ref):
    t = u_ref.shape[1]
    nq = bmat_ref.shape[0]
    half = bmat_ref.shape[2] // 2

    @pl.when(pl.program_id(1) == 0)
    def _():
        xre_ref[0:SUBLANES, :] = jnp.zeros((SUBLANES, xre_ref.shape[1]), F32)
        xim_ref[0:SUBLANES, :] = jnp.zeros((SUBLANES, xim_ref.shape[1]), F32)

    u = u_ref[0]
    ub = u.astype(BF16)
    for q in range(nq):
        bu = _mm(ub[:, q * LANES:(q + 1) * LANES], bmat_ref[q])
        xre_ref[SUBLANES:, q * half:(q + 1) * half] = bu[:, :half]
        xim_ref[SUBLANES:, q * half:(q + 1) * half] = bu[:, half:]

    def scan(r, _):
        base = pl.multiple_of(SUBLANES + r * SUBLANES, SUBLANES)
        xr = xre_ref[pl.ds(base, SUBLANES), :]
        xi = xim_ref[pl.ds(base, SUBLANES), :]
        for s, k in enumerate((1, 2, 4)):
            ar = sc_ref[2 * s]
            ai = sc_ref[2 * s + 1]
            sr = pltpu.roll(xr, k, axis=0)
            si = pltpu.roll(xi, k, axis=0)
            xr, xi = xr + ar * sr - ai * si, xi + ar * si + ai * sr
        cr = xre_ref[pl.ds(base - 1, 1), :]
        ci = xim_ref[pl.ds(base - 1, 1), :]
        pr = sc_ref[6]
        pi = sc_ref[7]
        xre_ref[pl.ds(base, SUBLANES), :] = xr + pr * cr - pi * ci
        xim_ref[pl.ds(base, SUBLANES), :] = xi + pr * ci + pi * cr
        return 0

    lax.fori_loop(0, t // SUBLANES, scan, 0, unroll=2)
    xre_ref[0:SUBLANES, :] = xre_ref[t:t + SUBLANES, :]
    xim_ref[0:SUBLANES, :] = xim_ref[t:t + SUBLANES, :]

    ys = []
    for q in range(nq):
        xr = xre_ref[SUBLANES:, q * half:(q + 1) * half].astype(BF16)
        xi = xim_ref[SUBLANES:, q * half:(q + 1) * half].astype(BF16)
        ys.append(_mm(xr, cre_ref[q]) + _mm(xi, cim_ref[q]))
    y = jnp.concatenate(ys, axis=1) + d_ref[...] * u
    y = jax.nn.gelu(y).astype(BF16)
    hh = _mm(y, gw_ref[...]) + gb_ref[...]
    w = hh.shape[1] // 2
    o_ref[0] = (hh[:, :w] * jax.nn.sigmoid(hh[:, w:]) * sgb_ref[0].astype(F32)).astype(BF16)


def _ssm(u, sgb, bmat, cre, cim, sc, dskip, gw, gb):
    b, l, w = u.shape
    t = ROW_TILE
    ns = sc.shape[-1]

    def row(n):
        return pl.BlockSpec((1, t, n), lambda bb, i: (bb, i, 0))

    return pl.pallas_call(
        _ssm_kernel,
        grid=(b, l // t),
        in_specs=[row(w), row(w), _const_spec(bmat.shape), _const_spec(cre.shape), _const_spec(cim.shape),
                  _const_spec(sc.shape), _const_spec(dskip.shape), _const_spec(gw.shape), _const_spec(gb.shape)],
        out_specs=row(w),
        out_shape=jax.ShapeDtypeStruct((b, l, w), BF16),
        scratch_shapes=[pltpu.VMEM((SUBLANES + t, ns), F32), pltpu.VMEM((SUBLANES + t, ns), F32)],
        compiler_params=_params("arbitrary", "arbitrary"),
        name="ssm",
    )(u, sgb, bmat, cre, cim, sc, dskip, gw, gb)


def _s5_prep(log_dt, a_re, a_im, b_re, b_im, c_re, c_im):
    g, p = a_re.shape
    h = b_re.shape[-1]
    gl = LANES // h
    nq = g // gl
    dt = jnp.exp(log_dt)[:, None]
    mag = jnp.exp(a_re * dt)
    ang = a_im * dt
    ab_re = mag * jnp.cos(ang)
    ab_im = mag * jnp.sin(ang)
    den = a_re * a_re + a_im * a_im
    n_re = ab_re - 1.0
    n_im = ab_im
    f_re = (n_re * a_re + n_im * a_im) / den
    f_im = (n_im * a_re - n_re * a_im) / den
    bb_re = f_re[..., None] * b_re - f_im[..., None] * b_im
    bb_im = f_re[..., None] * b_im + f_im[..., None] * b_re
    eye = jnp.eye(gl, dtype=F32)

    def bdiag_in(m):
        m = m.reshape(nq, gl, p, h)
        return jnp.einsum('qgph,gk->qghkp', m, eye).reshape(nq, gl * h, gl * p)

    def bdiag_out(m):
        m = m.reshape(nq, gl, h, p)
        return jnp.einsum('qghp,gk->qgpkh', m, eye).reshape(nq, gl * p, gl * h)

    bmat = jnp.concatenate([bdiag_in(bb_re), bdiag_in(bb_im)], axis=2).astype(BF16)
    cre = bdiag_out(c_re).astype(BF16)
    cim = bdiag_out(-c_im).astype(BF16)

    pw = [(ab_re.reshape(-1), ab_im.reshape(-1))]
    for _ in range(SUBLANES - 1):
        pr, pi = pw[-1]
        pw.append((pr * pw[0][0] - pi * pw[0][1], pr * pw[0][1] + pi * pw[0][0]))
    rows = jnp.arange(SUBLANES)[:, None]
    sc = []
    for k in (1, 2, 4):
        sc.append(jnp.where(rows >= k, pw[k - 1][0][None, :], 0.0))
        sc.append(jnp.where(rows >= k, pw[k - 1][1][None, :], 0.0))
    sc.append(jnp.stack([pw[r][0] for r in range(SUBLANES)]))
    sc.append(jnp.stack([pw[r][1] for r in range(SUBLANES)]))
    return bmat, cre, cim, jnp.stack(sc).astype(F32)


def _mid_kernel(x_ref, a_ref, s_ref, wo_ref, g_ref, wq_ref, wk_ref, wvt_ref, wg_ref, wqi_ref, wki_ref, wwt_ref,
                qg_ref, kg_ref, h_ref, q_ref, k_ref, vt_ref, sg_ref, qi_ref, ki_ref, wt_ref):
    aw = a_ref.shape[2]
    h = x_ref[0] + _mm(a_ref[0], wo_ref[0:aw, :]) + _mm(s_ref[0], wo_ref[aw:, :])
    h_ref[0] = h
    hn = _rms(h, g_ref[...]).astype(BF16)
    qf = _mm(hn, wq_ref[...])
    for hd in range(C_HEADS):
        sl = slice(hd * C_HEAD_DIM, (hd + 1) * C_HEAD_DIM)
        q_ref[0, :, sl] = (_rms(qf[:, sl], qg_ref[...]) * (C_HEAD_DIM ** -0.5)).astype(BF16)
    kf = _mm(hn, wk_ref[...])
    for hd in range(C_KV_HEADS):
        sl = slice(hd * C_HEAD_DIM, (hd + 1) * C_HEAD_DIM)
        k_ref[0, :, sl] = _rms(kf[:, sl], kg_ref[...]).astype(BF16)
    vt_ref[0] = _mm_nt(wvt_ref[...], hn).astype(BF16)
    sg_ref[0] = _silu(_mm(hn, wg_ref[...])).astype(BF16)
    qi_ref[0] = _mm(hn, wqi_ref[...]).astype(BF16)
    ki_ref[0] = _mm(hn, wki_ref[...]).astype(BF16)
    wt_ref[0] = _mm_nt(wwt_ref[...], hn) * ((IDX_HEADS ** -0.5) * (IDX_DIM ** -0.5))


def _mid(x, att0, ssm0, wo, g, wq, wk, wvt, wg, wqi, wki2, wwt, qg, kg):
    b, l, d = x.shape
    t = ROW_TILE

    def row(n):
        return pl.BlockSpec((1, t, n), lambda bb, i: (bb, i, 0))

    def col(n):
        return pl.BlockSpec((1, n, t), lambda bb, i: (bb, 0, i))

    weights = [wo, g, wq, wk, wvt, wg, wqi, wki2, wwt, qg, kg]
    cw = C_HEADS * C_HEAD_DIM
    ckv = C_KV_HEADS * C_HEAD_DIM
    return pl.pallas_call(
        _mid_kernel,
        grid=(b, l // t),
        in_specs=[row(d), row(att0.shape[2]), row(ssm0.shape[2])] + [_const_spec(w.shape) for w in weights],
        out_specs=[row(d), row(cw), row(ckv), col(ckv), row(cw), row(IDX_HEADS * IDX_DIM), row(2 * LANES),
                   col(IDX_HEADS)],
        out_shape=[jax.ShapeDtypeStruct((b, l, d), F32), jax.ShapeDtypeStruct((b, l, cw), BF16),
                   jax.ShapeDtypeStruct((b, l, ckv), BF16), jax.ShapeDtypeStruct((b, ckv, l), BF16),
                   jax.ShapeDtypeStruct((b, l, cw), BF16), jax.ShapeDtypeStruct((b, l, IDX_HEADS * IDX_DIM), BF16),
                   jax.ShapeDtypeStruct((b, l, 2 * LANES), BF16), jax.ShapeDtypeStruct((b, IDX_HEADS, l), F32)],
        compiler_params=_params("arbitrary", "arbitrary"),
        name="mid",
    )(x, att0, ssm0, *weights)


def _dsa_kernel(q_ref, qi_ref, wt_ref, sg_ref, k_ref, vt_ref, ki_ref, tab_ref, lb_ref, o_ref,
                sc_ref, best_ref, x_ref, acc_ref, *, seq_len, topk):
    i = pl.program_id(1)
    ck = KEY_CHUNK
    per = ck // BLOCK
    nch = (i + per) // per
    t_row = i * BLOCK + lax.broadcasted_iota(I32, (1, LANES), 1)
    kiota = lax.broadcasted_iota(I32, (ck, LANES), 0)

    def chunk_off(c):
        return pl.multiple_of(c * ck, ck)

    qi = qi_ref[0]
    qi_stack = [jnp.concatenate([qi[:, (2 * s) * LANES:(2 * s + 1) * LANES],
                                 qi[:, (2 * s + 1) * LANES:(2 * s + 2) * LANES]], axis=0) for s in range(2)]
    wt = wt_ref[0]

    def score_chunk(c, masked):
        off = chunk_off(c)
        sc = jnp.zeros((ck, LANES), F32)
        for a in range(2):
            kk = ki_ref[0, pl.ds(off, ck), a * LANES:(a + 1) * LANES]
            for s in range(2):
                r = _mm_nt(kk, qi_stack[s])
                for j in range(2):
                    hd = 2 * (2 * s + j) + a
                    sc = sc + jnp.maximum(r[:, j * LANES:(j + 1) * LANES], 0.0) * wt[hd:hd + 1, :]
        if masked:
            sc = jnp.where(off + kiota <= t_row, sc, NEG_INF)
        sc_ref[pl.ds(off, ck), :] = sc
        return _fold(sc, jnp.maximum)

    def score_body(c, mx):
        return jnp.maximum(mx, score_chunk(c, False))

    smax = lax.fori_loop(0, nch - 1, score_body, jnp.full((SUBLANES, LANES), NEG_INF, F32))
    smax = jnp.max(jnp.maximum(smax, score_chunk(nch - 1, True)), axis=0, keepdims=True)

    def count(*preds):
        rows = COUNT_ROWS
        sub = ck // rows

        def body(c, accs):
            out = []
            for u in range(sub):
                off = pl.multiple_of(c * ck + u * rows, rows)
                s = sc_ref[pl.ds(off, rows), :]
                for n, pred in enumerate(preds):
                    ind = pred(s, off).astype(I32)
                    out.append(accs[u * len(preds) + n]
                               + jnp.sum(ind.reshape(rows // SUBLANES, SUBLANES, LANES), axis=0))
            return tuple(out)

        accs = lax.fori_loop(0, nch, body, tuple(jnp.zeros((SUBLANES, LANES), I32) for _ in range(sub * len(preds))))
        res = [jnp.sum(sum(accs[n::len(preds)]), axis=0, keepdims=True) for n in range(len(preds))]
        return res[0] if len(preds) == 1 else res

    def key_value(k):
        return pltpu.bitcast(jnp.where(k < 0, INT_MIN - k, k), F32)

    def count_ge(k):
        thr = key_value(k)
        return count(lambda s, off: s >= thr)

    def full(v):
        return jnp.full((1, LANES), v, I32)

    def bisect(_, st):
        lo, hi, c_lo, c_hi = st
        mid = (lo >> 1) + (hi >> 1) + (lo & hi & 1)
        c = count_ge(mid)
        ge = c >= topk
        return jnp.where(ge, mid, lo), jnp.where(ge, hi, mid), jnp.where(ge, c, c_lo), jnp.where(ge, c_hi, c)

    searching = (i + 1) * BLOCK > topk

    def float_key(x):
        bits = pltpu.bitcast(x, I32)
        return jnp.where(bits < 0, INT_MIN - bits, bits)

    def search():
        k_lo = float_key(smax * 0.125)
        c = count_ge(k_lo)
        ok = (smax > 0.0) & (c >= topk)
        trips = jnp.where(jnp.min(jnp.where(ok, 1, 0)) > 0, 25, 32)
        st = (jnp.where(ok, k_lo, KEY_NEG_INF), float_key(smax) + 1, jnp.where(ok, c, nch * ck), full(0))
        out = lax.fori_loop(0, trips, bisect, st)
        return out[0], out[2], out[3]

    def exchange(v, a, b):
        v[a], v[b] = jnp.maximum(v[a], v[b]), jnp.minimum(v[a], v[b])

    def sort_desc(v):
        n, k = len(v), 2
        while k <= n:
            j = k // 2
            while j >= 1:
                for a in range(n):
                    b = a ^ j
                    if b > a:
                        exchange(v, *((a, b) if (a & k) == 0 else (b, a)))
                j //= 2
            k *= 2

    def merge_top(best, blk):
        n = len(best)
        v = [jnp.maximum(best[r], blk[n - 1 - r]) for r in range(n)]
        j = n // 2
        while j >= 1:
            for a in range(n):
                if a ^ j > a:
                    exchange(v, a, a ^ j)
            j //= 2
        return v

    crow = STREAMS * CAND * SUBLANES

    def cand_body(c, _):
        blk_all = sc_ref[pl.ds(pl.multiple_of(c * crow, crow), crow), :]
        for st in range(STREAMS):
            blk = [blk_all[(STREAMS * r + st) * SUBLANES:(STREAMS * r + st + 1) * SUBLANES] for r in range(CAND)]
            sort_desc(blk)
            base = st * CAND * SUBLANES
            best = [best_ref[base + r * SUBLANES:base + (r + 1) * SUBLANES, :] for r in range(CAND)]
            for r, x in enumerate(merge_top(best, blk)):
                best_ref[base + r * SUBLANES:base + (r + 1) * SUBLANES, :] = x
        return 0

    def cand_search():
        best_ref[...] = jnp.full(best_ref.shape, -jnp.inf, F32)
        lax.fori_loop(0, nch * (ck // crow), cand_body, 0)

        def count_cand(k):
            thr = key_value(k)
            parts = [jnp.sum((best_ref[r:r + COUNT_ROWS, :] >= thr).astype(I32)
                             .reshape(COUNT_ROWS // SUBLANES, SUBLANES, LANES), axis=0)
                     for r in range(0, crow, COUNT_ROWS)]
            return jnp.sum(sum(parts), axis=0, keepdims=True)

        def step(_, st):
            lo, hi = st
            mid = (lo >> 1) + (hi >> 1) + (lo & hi & 1)
            ge = count_cand(mid) >= topk
            return jnp.where(ge, mid, lo), jnp.where(ge, hi, mid)

        k_lo = float_key(smax * 0.125)
        ok = (smax > 0.0) & (count_cand(k_lo) >= topk)
        trips = jnp.where(jnp.min(jnp.where(ok, 1, 0)) > 0, 25, 32)
        vk, _ = lax.fori_loop(0, trips, step, (jnp.where(ok, k_lo, KEY_NEG_INF), float_key(smax) + 1))
        thr = key_value(vk)
        above_cand = jnp.sum((best_ref[...] > thr).astype(I32), axis=0, keepdims=True)
        c_ge, c_gt = count(lambda s, off: s >= thr, lambda s, off: s > thr)
        complete = jnp.min(jnp.where(c_gt == above_cand, 1, 0)) > 0
        return lax.cond(complete, lambda: (vk, c_ge, c_gt), search)

    vkey, c_lo, c_hi = lax.cond(searching, cand_search, lambda: (full(KEY_NEG_INF), full(topk), full(0)))
    vthr = key_value(vkey)
    need = topk - c_hi
    ties = c_lo - c_hi

    def tie_search():
        nxt = vkey + 1
        nxt = jnp.where((nxt > 0) & (nxt < KEY_MIN_NORMAL), KEY_MIN_NORMAL, nxt)
        step = key_value(nxt) - vthr

        def split(_, st):
            fl, fh = st
            fm = 0.5 * (fl + fh)
            thr = vthr + fm * step
            ge = count(lambda s, off: s >= thr) >= topk
            return jnp.where(ge, fm, fl), jnp.where(ge, fh, fm)

        fl, _ = lax.fori_loop(0, 26, split, (jnp.zeros((1, LANES), F32), jnp.ones((1, LANES), F32)))
        thr = vthr + fl * step
        want = topk - count(lambda s, off: s > thr)

        def body(_, st):
            lj, hj = st
            mid = (lj + hj) >> 1
            c = count(lambda s, off: (s == thr) & (off + kiota[:COUNT_ROWS] <= mid))
            ok = c >= want
            return jnp.where(ok, lj, mid), jnp.where(ok, mid, hj)

        _, hj = lax.fori_loop(0, 14, body, (full(-1), full(0) + (nch * ck - 1)))
        return thr, hj

    any_tie = searching & (jnp.max(ties - need) > 0)
    vthr, jmax = lax.cond(any_tie, tie_search, lambda: (vthr, full(seq_len)))

    def selection_mask(off):
        s = sc_ref[pl.ds(off, ck), :]
        s_idx = off + kiota
        sel = ((s > vthr) | ((s == vthr) & (s_idx <= jmax))) & (s_idx <= t_row)
        madd = jnp.where(sel, 0.0, NEG_INF)
        sc_ref[pl.ds(off, ck), :] = madd
        return madd

    q = q_ref[0]
    n_far = jnp.maximum((i - NEAR_BLOCKS + 1) // per, 0)
    hpg = C_HEADS // C_KV_HEADS
    npair = C_HEADS // 2
    q_pairs = [jnp.concatenate([q[:, (2 * j) * LANES:(2 * j + 1) * LANES],
                                q[:, (2 * j + 1) * LANES:(2 * j + 2) * LANES]], axis=0) for j in range(npair)]

    def bias_rows(hd, c):
        return jnp.concatenate([tab_ref[hd, jnp.clip(i - (c * per + r), 0, NEAR_BLOCKS)] for r in range(per)],
                               axis=0)

    def emit(hd, num, den):
        sl = slice(hd * LANES, (hd + 1) * LANES)
        o_ref[0, :, sl] = ((num / den).T * sg_ref[0, :, sl].astype(F32)).astype(BF16)

    def exact_attention():
        for g in range(C_KV_HEADS):
            def stage_body(near, g=g):
                def body(c, mx):
                    off = chunk_off(c)
                    madd = sc_ref[pl.ds(off, ck), :]
                    kc = k_ref[0, pl.ds(off, ck), g * LANES:(g + 1) * LANES]
                    out = []
                    for jj in range(hpg // 2):
                        lg = _mm_nt(kc, q_pairs[g * (hpg // 2) + jj])
                        for a in range(2):
                            hl = 2 * jj + a
                            x = lg[:, a * LANES:(a + 1) * LANES] + madd
                            if near:
                                x = x + bias_rows(hpg * g + hl, c)
                            x_ref[hl, pl.ds(off, ck), :] = x
                            out.append(jnp.maximum(mx[hl], _fold(x, jnp.maximum)))
                    return tuple(out)

                return body

            mx = tuple(jnp.full((SUBLANES, LANES), NEG_INF, F32) for _ in range(hpg))
            mx = lax.fori_loop(0, n_far, stage_body(False), mx)
            mx = lax.fori_loop(n_far, nch, stage_body(True), mx)
            m = [jnp.max(v, axis=0, keepdims=True) for v in mx]
            acc_ref[...] = jnp.zeros(acc_ref.shape, F32)

            def att_body(c, ls, g=g, m=m):
                off = chunk_off(c)
                vt = vt_ref[0, g * LANES:(g + 1) * LANES, pl.ds(off, ck)]
                out = []
                for jj in range(hpg // 2):
                    ps = []
                    for a in range(2):
                        hl = 2 * jj + a
                        p = jnp.exp(x_ref[hl, pl.ds(off, ck), :] - m[hl])
                        out.append(ls[hl] + _fold(p, jnp.add))
                        ps.append(p.astype(BF16))
                    acc_ref[jj] += _mm(vt, jnp.concatenate(ps, axis=1))
                return tuple(out)

            ls = lax.fori_loop(0, nch, att_body, tuple(jnp.zeros((SUBLANES, LANES), F32) for _ in range(hpg)))
            for hl in range(hpg):
                emit(hpg * g + hl, acc_ref[hl // 2, :, (hl % 2) * LANES:(hl % 2 + 1) * LANES],
                     jnp.sum(ls[hl], axis=0, keepdims=True))

    lb = lb_ref[...]
    acc_ref[...] = jnp.zeros(acc_ref.shape, F32)

    def stage(c, g, near):
        off = chunk_off(c)
        mb = (selection_mask(off) if g == 0 else sc_ref[pl.ds(off, ck), :]) - lb
        kc = k_ref[0, pl.ds(off, ck), g * LANES:(g + 1) * LANES]
        for jj in range(hpg // 2):
            lg = _mm_nt(kc, q_pairs[g * (hpg // 2) + jj])
            for a in range(2):
                hl = 2 * jj + a
                x = lg[:, a * LANES:(a + 1) * LANES] + mb
                if near:
                    x = x + bias_rows(hpg * g + hl, c)
                x_ref[hl, pl.ds(off, ck), :] = x

    def consume(c, g, ls):
        off = chunk_off(c)
        vt = vt_ref[0, g * LANES:(g + 1) * LANES, pl.ds(off, ck)]
        out = []
        for jj in range(hpg // 2):
            ps = []
            for a in range(2):
                hl = 2 * jj + a
                p = jnp.exp(x_ref[hl, pl.ds(off, ck), :])
                out.append(ls[hl] + _fold(p, jnp.add))
                ps.append(p.astype(BF16))
            acc_ref[g * (hpg // 2) + jj] += _mm(vt, jnp.concatenate(ps, axis=1))
        return tuple(out)

    dens = []
    for g in range(C_KV_HEADS):
        def step(near, g=g):
            def body(c, ls):
                out = consume(c, g, ls)
                stage(c + 1, g, near)
                return out

            return body

        stage(0, g, True)
        ls = tuple(jnp.zeros((SUBLANES, LANES), F32) for _ in range(hpg))
        split = jnp.maximum(n_far - 1, 0)
        ls = lax.fori_loop(0, split, step(False), ls)
        ls = lax.fori_loop(split, nch - 1, step(True), ls)
        ls = consume(nch - 1, g, ls)
        dens += [jnp.sum(v, axis=0, keepdims=True) for v in ls]
    in_range = jnp.min(functools.reduce(jnp.minimum, dens)) > SUM_FLOOR

    @pl.when(in_range)
    def _():
        for hd in range(C_HEADS):
            emit(hd, acc_ref[hd // 2, :, (hd % 2) * LANES:(hd % 2 + 1) * LANES], dens[hd])

    @pl.when(jnp.logical_not(in_range))
    def _():
        exact_attention()


def _dsa(q, qi, wt, sg, k, vt, ki2, tab, lb):
    b, l, cw = q.shape
    nb = l // BLOCK
    topk = min(TOPK_MAX, l // 4)

    def blk(n):
        return pl.BlockSpec((1, BLOCK, n), lambda bb, i: (bb, i, 0))

    def whole(s1, s2):
        return pl.BlockSpec((1, s1, s2), lambda bb, i: (bb, 0, 0), pipeline_mode=pl.Buffered(1))

    hpg = C_HEADS // C_KV_HEADS
    return pl.pallas_call(
        functools.partial(_dsa_kernel, seq_len=l, topk=topk),
        grid=(b, nb),
        in_specs=[blk(cw), blk(qi.shape[2]), pl.BlockSpec((1, IDX_HEADS, BLOCK), lambda bb, i: (bb, 0, i)), blk(cw),
                  whole(l, k.shape[2]), whole(vt.shape[1], l), whole(l, ki2.shape[2]),
                  pl.BlockSpec(tab.shape, lambda bb, i: (0, 0, 0, 0), pipeline_mode=pl.Buffered(1)),
                  _const_spec(lb.shape)],
        out_specs=blk(cw),
        out_shape=jax.ShapeDtypeStruct((b, l, cw), BF16),
        scratch_shapes=[pltpu.VMEM((l, LANES), F32), pltpu.VMEM((STREAMS * CAND * SUBLANES, LANES), F32),
                        pltpu.VMEM((hpg, l, LANES), F32),
                        pltpu.VMEM((C_HEADS // 2, C_HEAD_DIM, 2 * LANES), F32)],
        compiler_params=_params("arbitrary", "arbitrary"),
        name="dsa",
    )(q, qi, wt, sg, k, vt, ki2, tab, lb)


def _out_kernel(h_ref, a_ref, w_ref, o_ref):
    o_ref[...] = h_ref[...] + _mm(a_ref[...], w_ref[...])


def _outproj(h2, a2, w):
    rows, d = h2.shape
    t = ROW_TILE
    return pl.pallas_call(
        _out_kernel,
        grid=(rows // t,),
        in_specs=[pl.BlockSpec((t, d), lambda i: (i, 0)), pl.BlockSpec((t, a2.shape[1]), lambda i: (i, 0)),
                  _const_spec(w.shape)],
        out_specs=pl.BlockSpec((t, d), lambda i: (i, 0)),
        out_shape=jax.ShapeDtypeStruct((rows, d), F32),
        compiler_params=_params("arbitrary"),
        name="outproj1",
    )(h2, a2, w)


def _bias_tables(rel_bias, seq_len):
    del seq_len
    nv = (NEAR_BLOCKS + 1) * BLOCK
    vec = rel_bias[_t5_bucket(jnp.arange(nv, dtype=I32))].astype(F32).T

    def window(lo, n):
        pad = max(0, -lo)
        body = vec[:, max(lo, 0):lo + n]
        return jnp.concatenate([jnp.broadcast_to(vec[:, :1], (vec.shape[0], pad)), body], axis=1)

    def toeplitz(g, rows, cols):
        w = rows + cols
        g2 = jnp.concatenate([g[:, rows - 1:rows - 1 + cols], g[:, :1], g[:, :rows - 1]], axis=1)
        flat = jnp.tile(g2, (1, rows))[:, :rows * (w - 1)]
        return flat.reshape(-1, rows, w - 1)[:, :, :cols]

    bias0 = jnp.transpose(toeplitz(window(BLOCK - (2 * BLOCK - 1), 3 * BLOCK - 1), 2 * BLOCK, BLOCK), (0, 2, 1))
    tiles = [toeplitz(window(dl * BLOCK - (BLOCK - 1), 2 * BLOCK - 1), BLOCK, BLOCK) for dl in range(NEAR_BLOCKS)]
    tab = jnp.stack(tiles, axis=1) - rel_bias[NUM_BUCKETS - 1].astype(F32)[:, None, None, None]
    tab = jnp.concatenate([tab, jnp.zeros((tab.shape[0], 1, BLOCK, BLOCK), F32)], axis=1)
    return bias0, tab


def kernel(x, rel_bias, norm_g, ev_w_in, ev_w_out, ev_q_norm_g, ev_k_norm_g, ev_sinks, ev_ssm_log_dt, ev_ssm_a_re,
           ev_ssm_a_im, ev_ssm_b_re, ev_ssm_b_im, ev_ssm_c_re, ev_ssm_c_im, ev_ssm_d, ev_glu_w, ev_glu_b, od_w_in,
           od_w_out, od_q_norm_g, od_k_norm_g):
    b, l, d = x.shape
    assert l % KEY_CHUNK == 0 and l % ROW_TILE == 0
    assert (NEAR_BLOCKS - 1) * BLOCK + 1 >= 16 * 64 ** (15 / 16) + 1
    bias0, tab = _bias_tables(rel_bias, l)

    w0 = ev_w_in[0]
    hd = A_HEAD_DIM
    o0 = np.cumsum([0, A_WIDTH, A_KV_HEADS * hd, A_KV_HEADS * hd, A_WIDTH, A_WIDTH, A_WIDTH])
    wq0, wk0, wv0, wga, wu, wgb = (w0[:, o0[n]:o0[n + 1]] for n in range(6))
    z = jnp.zeros((d, hd), w0.dtype)

    def variants(w):
        return jnp.concatenate([c for g in range(A_KV_HEADS) for c in (w[:, g * hd:(g + 1) * hd], z, z,
                                                                       w[:, g * hd:(g + 1) * hd])], axis=1)

    w0x = jnp.concatenate([wq0, variants(wk0), variants(wv0), wga, wu, wgb], axis=1).astype(BF16)
    qg2 = jnp.tile(ev_q_norm_g[0], 2)[None, :]
    kg2 = jnp.tile(ev_k_norm_g[0], 2)[None, :]
    q0, k0, v0, sga, u, sgb = _proj0(x.reshape(b * l, d), norm_g[0][None, :], w0x, qg2, kg2)
    shp = lambda a: a.reshape(b, l, a.shape[-1])
    sinks = jnp.broadcast_to(ev_sinks[0][:, None], (A_HEADS, LANES)).astype(F32)
    att0 = _attn0(shp(q0), shp(k0), shp(v0), shp(sga), bias0, sinks)
    bmat, cre, cim, sc = _s5_prep(ev_ssm_log_dt[0], ev_ssm_a_re[0], ev_ssm_a_im[0], ev_ssm_b_re[0], ev_ssm_b_im[0],
                                  ev_ssm_c_re[0], ev_ssm_c_im[0])
    ssm0 = _ssm(shp(u), shp(sgb), bmat, cre, cim, sc, ev_ssm_d[0].reshape(1, -1), ev_glu_w[0].astype(BF16),
                ev_glu_b[0][None, :])

    w1 = od_w_in[0]
    cw = C_HEADS * C_HEAD_DIM
    ckv = C_KV_HEADS * C_HEAD_DIM
    o = np.cumsum([0, cw, ckv, ckv, cw, IDX_HEADS * IDX_DIM, IDX_DIM, IDX_HEADS])
    wq, wk, wv, wg, wqi, wki, ww = (w1[:, o[n]:o[n + 1]] for n in range(7))
    zki = jnp.zeros((d, LANES - IDX_DIM), w1.dtype)
    wki2 = jnp.concatenate([wki, zki, zki, wki], axis=1)
    bf = lambda a: a.astype(BF16)
    h1, q1, k1, vt1, sg1, qi1, ki2, wt1 = _mid(
        x, att0, ssm0, bf(ev_w_out[0]), norm_g[1][None, :], bf(wq), bf(wk), bf(wv.T), bf(wg), bf(wqi), bf(wki2),
        bf(ww.T), od_q_norm_g[0][None, :], od_k_norm_g[0][None, :])
    lb = (1.02 * C_HEAD_DIM ** 0.5 * jnp.max(jnp.abs(od_q_norm_g[0])) * jnp.max(jnp.abs(od_k_norm_g[0]))
          + jnp.max(tab))
    att1 = _dsa(q1, qi1, wt1, sg1, k1, vt1, ki2, tab, jnp.full((1, LANES), lb, F32))
    out = _outproj(h1.reshape(b * l, d), att1.reshape(b * l, cw), bf(od_w_out[0]))
    return out.reshape(b, l, d)
```

```python
import functools
import math

import jax
import jax.numpy as jnp
import numpy as np
from jax import lax
from jax.experimental import pallas as pl
from jax.experimental.pallas import tpu as pltpu

F32 = jnp.float32
BF16 = jnp.bfloat16
I32 = jnp.int32

LANES = 128
SUBLANES = 8
VMEM_LIMIT = 56 * 1024 * 1024

BLOCK = 128
WINDOW = 128
A_HEADS = 8
A_HEAD_DIM = 64
A_KV_HEADS = 2
A_WIDTH = A_HEADS * A_HEAD_DIM
SSM_GROUP = 16
SSM_STATE = 64
C_HEADS = 8
C_HEAD_DIM = 128
C_KV_HEADS = 2
IDX_HEADS = 8
IDX_DIM = 64
TOPK_MAX = 256
NUM_BUCKETS = 32
REL_MAX_DIST = 1024
EPS = 1e-6
NEG_INF = -1e30
INT_MIN = -(2 ** 31)
KEY_MIN_NORMAL = 0x00800000
KEY_POS_INF = 0x7F800000
KEY_NEG_INF = INT_MIN + 0x00800000

ROW_TILE = 1024
KEY_CHUNK = 1024
NEAR_BLOCKS = 8
FOLD_CHAINS = 8
COUNT_ROWS = 512
SUM_FLOOR = 1e-30
CAND = 32
STREAMS = 2
NT_DIMS = (((1,), (1,)), ((), ()))


def _t5_bucket(dist):
    n = jnp.maximum(dist, 0)
    max_exact = NUM_BUCKETS // 2
    nf = jnp.maximum(n, 1).astype(F32)
    large = max_exact + (jnp.log(nf / max_exact) / math.log(REL_MAX_DIST / max_exact)
                         * (NUM_BUCKETS - max_exact)).astype(I32)
    large = jnp.minimum(large, NUM_BUCKETS - 1)
    return jnp.where(n < max_exact, n, large)


def _silu(x):
    return x * jax.nn.sigmoid(x)


def _rms(x, g):
    ms = jnp.mean(x * x, axis=-1, keepdims=True)
    return x * lax.rsqrt(ms + EPS) * g


def _mm(a, b):
    return jnp.dot(a, b, preferred_element_type=F32)


def _mm_nt(a, b):
    return lax.dot_general(a, b, NT_DIMS, preferred_element_type=F32)


def _fold(x, op):
    n = x.shape[0] // SUBLANES
    chains = min(FOLD_CHAINS, n)
    accs = [x[r * SUBLANES:(r + 1) * SUBLANES] for r in range(chains)]
    for r in range(chains, n):
        accs[r % chains] = op(accs[r % chains], x[r * SUBLANES:(r + 1) * SUBLANES])
    while len(accs) > 1:
        accs = [op(a, b) for a, b in zip(accs[::2], accs[1::2])] + accs[len(accs) & ~1:]
    return accs[0]


def _params(*sem):
    return pltpu.CompilerParams(dimension_semantics=sem, vmem_limit_bytes=VMEM_LIMIT)


def _const_spec(shape):
    zeros = (0,) * len(shape)
    return pl.BlockSpec(shape, lambda *_: zeros)


def _proj0_kernel(x_ref, g_ref, w_ref, qg_ref, kg_ref, q_ref, k_ref, v_ref, sga_ref, u_ref, sgb_ref):
    hn = _rms(x_ref[...], g_ref[...]).astype(BF16)
    lo = lax.broadcasted_iota(I32, (1, LANES), 1) < A_HEAD_DIM

    def mm(n):
        return _mm(hn, w_ref[:, n * A_WIDTH:(n + 1) * A_WIDTH])

    def segnorm(x, g2):
        sq = x * x
        s_lo = jnp.sum(jnp.where(lo, sq, 0.0), axis=-1, keepdims=True)
        s_hi = jnp.sum(jnp.where(lo, 0.0, sq), axis=-1, keepdims=True)
        inv = jnp.where(lo, lax.rsqrt(s_lo / A_HEAD_DIM + EPS), lax.rsqrt(s_hi / A_HEAD_DIM + EPS))
        return x * inv * g2

    q, k = mm(0), mm(1)
    for p in range(A_WIDTH // LANES):
        sl = slice(p * LANES, (p + 1) * LANES)
        q_ref[:, sl] = (segnorm(q[:, sl], qg_ref[...]) * (A_HEAD_DIM ** -0.5)).astype(BF16)
        k_ref[:, sl] = segnorm(k[:, sl], kg_ref[...]).astype(BF16)
    v_ref[...] = mm(2).astype(BF16)
    sga_ref[...] = _silu(mm(3)).astype(BF16)
    u_ref[...] = mm(4)
    sgb_ref[...] = _silu(mm(5)).astype(BF16)


def _proj0(x2, g, w, qg2, kg2):
    rows, d = x2.shape
    t = ROW_TILE

    def row(n):
        return pl.BlockSpec((t, n), lambda i: (i, 0))

    n = A_WIDTH
    return pl.pallas_call(
        _proj0_kernel,
        grid=(rows // t,),
        in_specs=[row(d), _const_spec((1, d)), _const_spec(w.shape), _const_spec(qg2.shape), _const_spec(kg2.shape)],
        out_specs=[row(n)] * 6,
        out_shape=[jax.ShapeDtypeStruct((rows, n), BF16), jax.ShapeDtypeStruct((rows, n), BF16),
                   jax.ShapeDtypeStruct((rows, n), BF16), jax.ShapeDtypeStruct((rows, n), BF16),
                   jax.ShapeDtypeStruct((rows, n), F32), jax.ShapeDtypeStruct((rows, n), BF16)],
        compiler_params=_params("arbitrary"),
        name="proj0",
    )(x2, g, w, qg2, kg2)


def _attn0_kernel(q_ref, kc_ref, kp_ref, vc_ref, vp_ref, sga_ref, bias_ref, sink_ref, ones_ref, o_ref):
    i = pl.program_id(1)
    kb = jnp.concatenate([kp_ref[0], kc_ref[0]], axis=0)
    vb = jnp.concatenate([vp_ref[0], vc_ref[0]], axis=0)

    def variant(x, g, a):
        n = 2 * g + a
        return x[:, n * LANES:(n + 1) * LANES]

    row = lax.broadcasted_iota(I32, (BLOCK, 2 * BLOCK), 0)
    col = lax.broadcasted_iota(I32, (BLOCK, 2 * BLOCK), 1)
    d = row + BLOCK - col
    mask = (d >= 0) & (d < WINDOW) & ((i > 0) | (col >= BLOCK))

    lgs, sinks = [], []
    for p in range(A_HEADS // 2):
        qp = q_ref[0, :, p * LANES:(p + 1) * LANES]
        for a in range(2):
            h = 2 * p + a
            lgs.append(jnp.where(mask, _mm_nt(qp, variant(kb, p // 2, a)) + bias_ref[h], NEG_INF))
            sinks.append(jnp.broadcast_to(sink_ref[h:h + 1, 0:1], (BLOCK, 1)))
    lg = jnp.concatenate(lgs, axis=0)
    sink = jnp.concatenate(sinks, axis=0)
    m = jnp.maximum(jnp.max(lg, axis=-1, keepdims=True), sink)
    e = jnp.exp(lg - m).astype(BF16)
    inv = 1.0 / (_mm(e, ones_ref[...]) + jnp.exp(sink - m))
    for p in range(A_HEADS // 2):
        sl = slice(p * LANES, (p + 1) * LANES)
        acc = jnp.zeros((BLOCK, LANES), F32)
        for a in range(2):
            h = 2 * p + a
            hs = slice(h * BLOCK, (h + 1) * BLOCK)
            acc = acc + _mm(e[hs], variant(vb, p // 2, a)) * inv[hs]
        o_ref[0, :, sl] = (acc * sga_ref[0, :, sl].astype(F32)).astype(BF16)


def _attn0(q, k, v, sga, bias0, sinks):
    b, l, _ = q.shape
    nb = l // BLOCK
    ones = jnp.ones((2 * BLOCK, LANES), BF16)

    def cur(n):
        return pl.BlockSpec((1, BLOCK, n), lambda bb, i: (bb, i, 0))

    def prev(n):
        return pl.BlockSpec((1, BLOCK, n), lambda bb, i: (bb, jnp.maximum(i - 1, 0), 0))

    return pl.pallas_call(
        _attn0_kernel,
        grid=(b, nb),
        in_specs=[cur(512), cur(512), prev(512), cur(512), prev(512), cur(512),
                  _const_spec(bias0.shape), _const_spec(sinks.shape), _const_spec(ones.shape)],
        out_specs=cur(512),
        out_shape=jax.ShapeDtypeStruct((b, l, 512), BF16),
        compiler_params=_params("arbitrary", "arbitrary"),
        name="attn0",
    )(q, k, k, v, v, sga, bias0, sinks, ones)


def _ssm_kernel(u_ref, sgb_ref, bmat_ref, cre_ref, cim_ref, sc_ref, d_ref, gw_ref, gb_ref, o_ref, xre_ref, xim_ref):
    t = u_ref.shape[1]
    nq = bmat_ref.shape[0]
    half = bmat_ref.shape[2] // 2

    @pl.when(pl.program_id(1) == 0)
    def _():
        xre_ref[0:SUBLANES, :] = jnp.zeros((SUBLANES, xre_ref.shape[1]), F32)
        xim_ref[0:SUBLANES, :] = jnp.zeros((SUBLANES, xim_ref.shape[1]), F32)

    u = u_ref[0]
    ub = u.astype(BF16)
    for q in range(nq):
        bu = _mm(ub[:, q * LANES:(q + 1) * LANES], bmat_ref[q])
        xre_ref[SUBLANES:, q * half:(q + 1) * half] = bu[:, :half]
        xim_ref[SUBLANES:, q * half:(q + 1) * half] = bu[:, half:]

    def scan(r, _):
        base = pl.multiple_of(SUBLANES + r * SUBLANES, SUBLANES)
        xr = xre_ref[pl.ds(base, SUBLANES), :]
        xi = xim_ref[pl.ds(base, SUBLANES), :]
        for s, k in enumerate((1, 2, 4)):
            ar = sc_ref[2 * s]
            ai = sc_ref[2 * s + 1]
            sr = pltpu.roll(xr, k, axis=0)
            si = pltpu.roll(xi, k, axis=0)
            xr, xi = xr + ar * sr - ai * si, xi + ar * si + ai * sr
        cr = xre_ref[pl.ds(base - 1, 1), :]
        ci = xim_ref[pl.ds(base - 1, 1), :]
        pr = sc_ref[6]
        pi = sc_ref[7]
        xre_ref[pl.ds(base, SUBLANES), :] = xr + pr * cr - pi * ci
        xim_ref[pl.ds(base, SUBLANES), :] = xi + pr * ci + pi * cr
        return 0

    lax.fori_loop(0, t // SUBLANES, scan, 0, unroll=2)
    xre_ref[0:SUBLANES, :] = xre_ref[t:t + SUBLANES, :]
    xim_ref[0:SUBLANES, :] = xim_ref[t:t + SUBLANES, :]

    ys = []
    for q in range(nq):
        xr = xre_ref[SUBLANES:, q * half:(q + 1) * half].astype(BF16)
        xi = xim_ref[SUBLANES:, q * half:(q + 1) * half].astype(BF16)
        ys.append(_mm(xr, cre_ref[q]) + _mm(xi, cim_ref[q]))
    y = jnp.concatenate(ys, axis=1) + d_ref[...] * u
    y = jax.nn.gelu(y).astype(BF16)
    hh = _mm(y, gw_ref[...]) + gb_ref[...]
    w = hh.shape[1] // 2
    o_ref[0] = (hh[:, :w] * jax.nn.sigmoid(hh[:, w:]) * sgb_ref[0].astype(F32)).astype(BF16)


def _ssm(u, sgb, bmat, cre, cim, sc, dskip, gw, gb):
    b, l, w = u.shape
    t = ROW_TILE
    ns = sc.shape[-1]

    def row(n):
        return pl.BlockSpec((1, t, n), lambda bb, i: (bb, i, 0))

    return pl.pallas_call(
        _ssm_kernel,
        grid=(b, l // t),
        in_specs=[row(w), row(w), _const_spec(bmat.shape), _const_spec(cre.shape), _const_spec(cim.shape),
                  _const_spec(sc.shape), _const_spec(dskip.shape), _const_spec(gw.shape), _const_spec(gb.shape)],
        out_specs=row(w),
        out_shape=jax.ShapeDtypeStruct((b, l, w), BF16),
        scratch_shapes=[pltpu.VMEM((SUBLANES + t, ns), F32), pltpu.VMEM((SUBLANES + t, ns), F32)],
        compiler_params=_params("arbitrary", "arbitrary"),
        name="ssm",
    )(u, sgb, bmat, cre, cim, sc, dskip, gw, gb)


def _s5_prep(log_dt, a_re, a_im, b_re, b_im, c_re, c_im):
    g, p = a_re.shape
    h = b_re.shape[-1]
    gl = LANES // h
    nq = g // gl
    dt = jnp.exp(log_dt)[:, None]
    mag = jnp.exp(a_re * dt)
    ang = a_im * dt
    ab_re = mag * jnp.cos(ang)
    ab_im = mag * jnp.sin(ang)
    den = a_re * a_re + a_im * a_im
    n_re = ab_re - 1.0
    n_im = ab_im
    f_re = (n_re * a_re + n_im * a_im) / den
    f_im = (n_im * a_re - n_re * a_im) / den
    bb_re = f_re[..., None] * b_re - f_im[..., None] * b_im
    bb_im = f_re[..., None] * b_im + f_im[..., None] * b_re
    eye = jnp.eye(gl, dtype=F32)

    def bdiag_in(m):
        m = m.reshape(nq, gl, p, h)
        return jnp.einsum('qgph,gk->qghkp', m, eye).reshape(nq, gl * h, gl * p)

    def bdiag_out(m):
        m = m.reshape(nq, gl, h, p)
        return jnp.einsum('qghp,gk->qgpkh', m, eye).reshape(nq, gl * p, gl * h)

    bmat = jnp.concatenate([bdiag_in(bb_re), bdiag_in(bb_im)], axis=2).astype(BF16)
    cre = bdiag_out(c_re).astype(BF16)
    cim = bdiag_out(-c_im).astype(BF16)

    pw = [(ab_re.reshape(-1), ab_im.reshape(-1))]
    for _ in range(SUBLANES - 1):
        pr, pi = pw[-1]
        pw.append((pr * pw[0][0] - pi * pw[0][1], pr * pw[0][1] + pi * pw[0][0]))
    rows = jnp.arange(SUBLANES)[:, None]
    sc = []
    for k in (1, 2, 4):
        sc.append(jnp.where(rows >= k, pw[k - 1][0][None, :], 0.0))
        sc.append(jnp.where(rows >= k, pw[k - 1][1][None, :], 0.0))
    sc.append(jnp.stack([pw[r][0] for r in range(SUBLANES)]))
    sc.append(jnp.stack([pw[r][1] for r in range(SUBLANES)]))
    return bmat, cre, cim, jnp.stack(sc).astype(F32)


def _mid_kernel(x_ref, a_ref, s_ref, wo_ref, g_ref, wq_ref, wk_ref, wvt_ref, wg_ref, wqi_ref, wki_ref, wwt_ref,
                qg_ref, kg_ref, h_ref, q_ref, k_ref, vt_ref, sg_ref, qi_ref, ki_ref, wt_ref):
    aw = a_ref.shape[2]
    h = x_ref[0] + _mm(a_ref[0], wo_ref[0:aw, :]) + _mm(s_ref[0], wo_ref[aw:, :])
    h_ref[0] = h
    hn = _rms(h, g_ref[...]).astype(BF16)
    qf = _mm(hn, wq_ref[...])
    for hd in range(C_HEADS):
        sl = slice(hd * C_HEAD_DIM, (hd + 1) * C_HEAD_DIM)
        q_ref[0, :, sl] = (_rms(qf[:, sl], qg_ref[...]) * (C_HEAD_DIM ** -0.5)).astype(BF16)
    kf = _mm(hn, wk_ref[...])
    for hd in range(C_KV_HEADS):
        sl = slice(hd * C_HEAD_DIM, (hd + 1) * C_HEAD_DIM)
        k_ref[0, :, sl] = _rms(kf[:, sl], kg_ref[...]).astype(BF16)
    vt_ref[0] = _mm_nt(wvt_ref[...], hn).astype(BF16)
    sg_ref[0] = _silu(_mm(hn, wg_ref[...])).astype(BF16)
    qi_ref[0] = _mm(hn, wqi_ref[...]).astype(BF16)
    ki_ref[0] = _mm(hn, wki_ref[...]).astype(BF16)
    wt_ref[0] = _mm_nt(wwt_ref[...], hn) * ((IDX_HEADS ** -0.5) * (IDX_DIM ** -0.5))


def _mid(x, att0, ssm0, wo, g, wq, wk, wvt, wg, wqi, wki2, wwt, qg, kg):
    b, l, d = x.shape
    t = ROW_TILE

    def row(n):
        return pl.BlockSpec((1, t, n), lambda bb, i: (bb, i, 0))

    def col(n):
        return pl.BlockSpec((1, n, t), lambda bb, i: (bb, 0, i))

    weights = [wo, g, wq, wk, wvt, wg, wqi, wki2, wwt, qg, kg]
    cw = C_HEADS * C_HEAD_DIM
    ckv = C_KV_HEADS * C_HEAD_DIM
    return pl.pallas_call(
        _mid_kernel,
        grid=(b, l // t),
        in_specs=[row(d), row(att0.shape[2]), row(ssm0.shape[2])] + [_const_spec(w.shape) for w in weights],
        out_specs=[row(d), row(cw), row(ckv), col(ckv), row(cw), row(IDX_HEADS * IDX_DIM), row(2 * LANES),
                   col(IDX_HEADS)],
        out_shape=[jax.ShapeDtypeStruct((b, l, d), F32), jax.ShapeDtypeStruct((b, l, cw), BF16),
                   jax.ShapeDtypeStruct((b, l, ckv), BF16), jax.ShapeDtypeStruct((b, ckv, l), BF16),
                   jax.ShapeDtypeStruct((b, l, cw), BF16), jax.ShapeDtypeStruct((b, l, IDX_HEADS * IDX_DIM), BF16),
                   jax.ShapeDtypeStruct((b, l, 2 * LANES), BF16), jax.ShapeDtypeStruct((b, IDX_HEADS, l), F32)],
        compiler_params=_params("arbitrary", "arbitrary"),
        name="mid",
    )(x, att0, ssm0, *weights)


def _dsa_kernel(q_ref, qi_ref, wt_ref, sg_ref, k_ref, vt_ref, ki_ref, tab_ref, lb_ref, o_ref,
                sc_ref, best_ref, x_ref, acc_ref, *, seq_len, topk):
    i = pl.program_id(1)
    ck = KEY_CHUNK
    per = ck // BLOCK
    nch = (i + per) // per
    t_row = i * BLOCK + lax.broadcasted_iota(I32, (1, LANES), 1)
    kiota = lax.broadcasted_iota(I32, (ck, LANES), 0)

    def chunk_off(c):
        return pl.multiple_of(c * ck, ck)

    qi = qi_ref[0]
    qi_stack = [jnp.concatenate([qi[:, (2 * s) * LANES:(2 * s + 1) * LANES],
                                 qi[:, (2 * s + 1) * LANES:(2 * s + 2) * LANES]], axis=0) for s in range(2)]
    wt = wt_ref[0]

    def score_chunk(c, masked):
        off = chunk_off(c)
        sc = jnp.zeros((ck, LANES), F32)
        for a in range(2):
            kk = ki_ref[0, pl.ds(off, ck), a * LANES:(a + 1) * LANES]
            for s in range(2):
                r = _mm_nt(kk, qi_stack[s])
                for j in range(2):
                    hd = 2 * (2 * s + j) + a
                    sc = sc + jnp.maximum(r[:, j * LANES:(j + 1) * LANES], 0.0) * wt[hd:hd + 1, :]
        if masked:
            sc = jnp.where(off + kiota <= t_row, sc, NEG_INF)
        sc_ref[pl.ds(off, ck), :] = sc
        return _fold(sc, jnp.maximum)

    def score_body(c, mx):
        return jnp.maximum(mx, score_chunk(c, False))

    smax = lax.fori_loop(0, nch - 1, score_body, jnp.full((SUBLANES, LANES), NEG_INF, F32))
    smax = jnp.max(jnp.maximum(smax, score_chunk(nch - 1, True)), axis=0, keepdims=True)

    def count(*preds):
        rows = COUNT_ROWS
        sub = ck // rows

        def body(c, accs):
            out = []
            for u in range(sub):
                off = pl.multiple_of(c * ck + u * rows, rows)
                s = sc_ref[pl.ds(off, rows), :]
                for n, pred in enumerate(preds):
                    ind = pred(s, off).astype(I32)
                    out.append(accs[u * len(preds) + n]
                               + jnp.sum(ind.reshape(rows // SUBLANES, SUBLANES, LANES), axis=0))
            return tuple(out)

        accs = lax.fori_loop(0, nch, body, tuple(jnp.zeros((SUBLANES, LANES), I32) for _ in range(sub * len(preds))))
        res = [jnp.sum(sum(accs[n::len(preds)]), axis=0, keepdims=True) for n in range(len(preds))]
        return res[0] if len(preds) == 1 else res

    def key_value(k):
        return pltpu.bitcast(jnp.where(k < 0, INT_MIN - k, k), F32)

    def count_ge(k):
        thr = key_value(k)
        return count(lambda s, off: s >= thr)

    def full(v):
        return jnp.full((1, LANES), v, I32)

    def bisect(_, st):
        lo, hi, c_lo, c_hi = st
        mid = (lo >> 1) + (hi >> 1) + (lo & hi & 1)
        c = count_ge(mid)
        ge = c >= topk
        return jnp.where(ge, mid, lo), jnp.where(ge, hi, mid), jnp.where(ge, c, c_lo), jnp.where(ge, c_hi, c)

    searching = (i + 1) * BLOCK > topk

    def float_key(x):
        bits = pltpu.bitcast(x, I32)
        return jnp.where(bits < 0, INT_MIN - bits, bits)

    def search():
        k_lo = float_key(smax * 0.125)
        c = count_ge(k_lo)
        ok = (smax > 0.0) & (c >= topk)
        trips = jnp.where(jnp.min(jnp.where(ok, 1, 0)) > 0, 25, 32)
        st = (jnp.where(ok, k_lo, KEY_NEG_INF), float_key(smax) + 1, jnp.where(ok, c, nch * ck), full(0))
        out = lax.fori_loop(0, trips, bisect, st)
        return out[0], out[2], out[3]

    def exchange(v, a, b):
        v[a], v[b] = jnp.maximum(v[a], v[b]), jnp.minimum(v[a], v[b])

    def sort_desc(v):
        n, k = len(v), 2
        while k <= n:
            j = k // 2
            while j >= 1:
                for a in range(n):
                    b = a ^ j
                    if b > a:
                        exchange(v, *((a, b) if (a & k) == 0 else (b, a)))
                j //= 2
            k *= 2

    def merge_top(best, blk):
        n = len(best)
        v = [jnp.maximum(best[r], blk[n - 1 - r]) for r in range(n)]
        j = n // 2
        while j >= 1:
            for a in range(n):
                if a ^ j > a:
                    exchange(v, a, a ^ j)
            j //= 2
        return v

    crow = STREAMS * CAND * SUBLANES

    def cand_body(c, _):
        blk_all = sc_ref[pl.ds(pl.multiple_of(c * crow, crow), crow), :]
        for st in range(STREAMS):
            blk = [blk_all[(STREAMS * r + st) * SUBLANES:(STREAMS * r + st + 1) * SUBLANES] for r in range(CAND)]
            sort_desc(blk)
            base = st * CAND * SUBLANES
            best = [best_ref[base + r * SUBLANES:base + (r + 1) * SUBLANES, :] for r in range(CAND)]
            for r, x in enumerate(merge_top(best, blk)):
                best_ref[base + r * SUBLANES:base + (r + 1) * SUBLANES, :] = x
        return 0

    def cand_search():
        best_ref[...] = jnp.full(best_ref.shape, -jnp.inf, F32)
        lax.fori_loop(0, (i + crow // BLOCK) // (crow // BLOCK), cand_body, 0)

        def count_cand(k):
            thr = key_value(k)
            parts = [jnp.sum((best_ref[r:r + COUNT_ROWS, :] >= thr).astype(I32)
                             .reshape(COUNT_ROWS // SUBLANES, SUBLANES, LANES), axis=0)
                     for r in range(0, crow, COUNT_ROWS)]
            return jnp.sum(sum(parts), axis=0, keepdims=True)

        def step(_, st):
            lo, hi = st
            mid = (lo >> 1) + (hi >> 1) + (lo & hi & 1)
            ge = count_cand(mid) >= topk
            return jnp.where(ge, mid, lo), jnp.where(ge, hi, mid)

        k_lo = float_key(smax * 0.125)
        ok = (smax > 0.0) & (count_cand(k_lo) >= topk)
        trips = jnp.where(jnp.min(jnp.where(ok, 1, 0)) > 0, 25, 32)
        vk, _ = lax.fori_loop(0, trips, step, (jnp.where(ok, k_lo, KEY_NEG_INF), float_key(smax) + 1))
        thr = key_value(vk)
        above_cand = jnp.sum((best_ref[...] > thr).astype(I32), axis=0, keepdims=True)
        c_ge, c_gt = count(lambda s, off: s >= thr, lambda s, off: s > thr)
        complete = jnp.min(jnp.where(c_gt == above_cand, 1, 0)) > 0
        return lax.cond(complete, lambda: (vk, c_ge, c_gt), search)

    vkey, c_lo, c_hi = lax.cond(searching, cand_search, lambda: (full(KEY_NEG_INF), full(topk), full(0)))
    vthr = key_value(vkey)
    need = topk - c_hi
    ties = c_lo - c_hi

    def tie_search():
        nxt = vkey + 1
        nxt = jnp.where((nxt > 0) & (nxt < KEY_MIN_NORMAL), KEY_MIN_NORMAL, nxt)
        step = key_value(nxt) - vthr

        def split(_, st):
            fl, fh = st
            fm = 0.5 * (fl + fh)
            thr = vthr + fm * step
            ge = count(lambda s, off: s >= thr) >= topk
            return jnp.where(ge, fm, fl), jnp.where(ge, fh, fm)

        fl, _ = lax.fori_loop(0, 26, split, (jnp.zeros((1, LANES), F32), jnp.ones((1, LANES), F32)))
        thr = vthr + fl * step
        want = topk - count(lambda s, off: s > thr)

        def body(_, st):
            lj, hj = st
            mid = (lj + hj) >> 1
            c = count(lambda s, off: (s == thr) & (off + kiota[:COUNT_ROWS] <= mid))
            ok = c >= want
            return jnp.where(ok, lj, mid), jnp.where(ok, mid, hj)

        _, hj = lax.fori_loop(0, 14, body, (full(-1), full(0) + (nch * ck - 1)))
        return thr, hj

    any_tie = searching & (jnp.max(ties - need) > 0)
    vthr, jmax = lax.cond(any_tie, tie_search, lambda: (vthr, full(seq_len)))

    def selection_mask(off):
        s = sc_ref[pl.ds(off, ck), :]
        s_idx = off + kiota
        sel = ((s > vthr) | ((s == vthr) & (s_idx <= jmax))) & (s_idx <= t_row)
        madd = jnp.where(sel, 0.0, NEG_INF)
        sc_ref[pl.ds(off, ck), :] = madd
        return madd

    q = q_ref[0]
    n_far = jnp.maximum((i - NEAR_BLOCKS + 1) // per, 0)
    hpg = C_HEADS // C_KV_HEADS
    npair = C_HEADS // 2
    q_pairs = [jnp.concatenate([q[:, (2 * j) * LANES:(2 * j + 1) * LANES],
                                q[:, (2 * j + 1) * LANES:(2 * j + 2) * LANES]], axis=0) for j in range(npair)]

    def bias_rows(hd, c):
        return jnp.concatenate([tab_ref[hd, jnp.clip(i - (c * per + r), 0, NEAR_BLOCKS)] for r in range(per)],
                               axis=0)

    def emit(hd, num, den):
        sl = slice(hd * LANES, (hd + 1) * LANES)
        o_ref[0, :, sl] = ((num / den).T * sg_ref[0, :, sl].astype(F32)).astype(BF16)

    def exact_attention():
        for g in range(C_KV_HEADS):
            def stage_body(near, g=g):
                def body(c, mx):
                    off = chunk_off(c)
                    madd = sc_ref[pl.ds(off, ck), :]
                    kc = k_ref[0, pl.ds(off, ck), g * LANES:(g + 1) * LANES]
                    out = []
                    for jj in range(hpg // 2):
                        lg = _mm_nt(kc, q_pairs[g * (hpg // 2) + jj])
                        for a in range(2):
                            hl = 2 * jj + a
                            x = lg[:, a * LANES:(a + 1) * LANES] + madd
                            if near:
                                x = x + bias_rows(hpg * g + hl, c)
                            x_ref[hl, pl.ds(off, ck), :] = x
                            out.append(jnp.maximum(mx[hl], _fold(x, jnp.maximum)))
                    return tuple(out)

                return body

            mx = tuple(jnp.full((SUBLANES, LANES), NEG_INF, F32) for _ in range(hpg))
            mx = lax.fori_loop(0, n_far, stage_body(False), mx)
            mx = lax.fori_loop(n_far, nch, stage_body(True), mx)
            m = [jnp.max(v, axis=0, keepdims=True) for v in mx]
            acc_ref[...] = jnp.zeros(acc_ref.shape, F32)

            def att_body(c, ls, g=g, m=m):
                off = chunk_off(c)
                vt = vt_ref[0, g * LANES:(g + 1) * LANES, pl.ds(off, ck)]
                out = []
                for jj in range(hpg // 2):
                    ps = []
                    for a in range(2):
                        hl = 2 * jj + a
                        p = jnp.exp(x_ref[hl, pl.ds(off, ck), :] - m[hl])
                        out.append(ls[hl] + _fold(p, jnp.add))
                        ps.append(p.astype(BF16))
                    acc_ref[jj] += _mm(vt, jnp.concatenate(ps, axis=1))
                return tuple(out)

            ls = lax.fori_loop(0, nch, att_body, tuple(jnp.zeros((SUBLANES, LANES), F32) for _ in range(hpg)))
            for hl in range(hpg):
                emit(hpg * g + hl, acc_ref[hl // 2, :, (hl % 2) * LANES:(hl % 2 + 1) * LANES],
                     jnp.sum(ls[hl], axis=0, keepdims=True))

    lb = lb_ref[...]
    acc_ref[...] = jnp.zeros(acc_ref.shape, F32)

    def stage(c, g, near):
        off = chunk_off(c)
        mb = (selection_mask(off) if g == 0 else sc_ref[pl.ds(off, ck), :]) - lb
        kc = k_ref[0, pl.ds(off, ck), g * LANES:(g + 1) * LANES]
        for jj in range(hpg // 2):
            lg = _mm_nt(kc, q_pairs[g * (hpg // 2) + jj])
            for a in range(2):
                hl = 2 * jj + a
                x = lg[:, a * LANES:(a + 1) * LANES] + mb
                if near:
                    x = x + bias_rows(hpg * g + hl, c)
                x_ref[hl, pl.ds(off, ck), :] = x

    def consume(c, g, ls):
        off = chunk_off(c)
        vt = vt_ref[0, g * LANES:(g + 1) * LANES, pl.ds(off, ck)]
        out = []
        for jj in range(hpg // 2):
            ps = []
            for a in range(2):
                hl = 2 * jj + a
                p = jnp.exp(x_ref[hl, pl.ds(off, ck), :])
                out.append(ls[hl] + _fold(p, jnp.add))
                ps.append(p.astype(BF16))
            acc_ref[g * (hpg // 2) + jj] += _mm(vt, jnp.concatenate(ps, axis=1))
        return tuple(out)

    dens = []
    for g in range(C_KV_HEADS):
        def step(near, g=g):
            def body(c, ls):
                out = consume(c, g, ls)
                stage(c + 1, g, near)
                return out

            return body

        stage(0, g, True)
        ls = tuple(jnp.zeros((SUBLANES, LANES), F32) for _ in range(hpg))
        split = jnp.maximum(n_far - 1, 0)
        ls = lax.fori_loop(0, split, step(False), ls)
        ls = lax.fori_loop(split, nch - 1, step(True), ls)
        ls = consume(nch - 1, g, ls)
        dens += [jnp.sum(v, axis=0, keepdims=True) for v in ls]
    in_range = jnp.min(functools.reduce(jnp.minimum, dens)) > SUM_FLOOR

    @pl.when(in_range)
    def _():
        for hd in range(C_HEADS):
            emit(hd, acc_ref[hd // 2, :, (hd % 2) * LANES:(hd % 2 + 1) * LANES], dens[hd])

    @pl.when(jnp.logical_not(in_range))
    def _():
        exact_attention()


def _dsa(q, qi, wt, sg, k, vt, ki2, tab, lb):
    b, l, cw = q.shape
    nb = l // BLOCK
    topk = min(TOPK_MAX, l // 4)

    def blk(n):
        return pl.BlockSpec((1, BLOCK, n), lambda bb, i: (bb, i, 0))

    def whole(s1, s2):
        return pl.BlockSpec((1, s1, s2), lambda bb, i: (bb, 0, 0), pipeline_mode=pl.Buffered(1))

    hpg = C_HEADS // C_KV_HEADS
    return pl.pallas_call(
        functools.partial(_dsa_kernel, seq_len=l, topk=topk),
        grid=(b, nb),
        in_specs=[blk(cw), blk(qi.shape[2]), pl.BlockSpec((1, IDX_HEADS, BLOCK), lambda bb, i: (bb, 0, i)), blk(cw),
                  whole(l, k.shape[2]), whole(vt.shape[1], l), whole(l, ki2.shape[2]),
                  pl.BlockSpec(tab.shape, lambda bb, i: (0, 0, 0, 0), pipeline_mode=pl.Buffered(1)),
                  _const_spec(lb.shape)],
        out_specs=blk(cw),
        out_shape=jax.ShapeDtypeStruct((b, l, cw), BF16),
        scratch_shapes=[pltpu.VMEM((l, LANES), F32), pltpu.VMEM((STREAMS * CAND * SUBLANES, LANES), F32),
                        pltpu.VMEM((hpg, l, LANES), F32),
                        pltpu.VMEM((C_HEADS // 2, C_HEAD_DIM, 2 * LANES), F32)],
        compiler_params=_params("arbitrary", "arbitrary"),
        name="dsa",
    )(q, qi, wt, sg, k, vt, ki2, tab, lb)


def _out_kernel(h_ref, a_ref, w_ref, o_ref):
    o_ref[...] = h_ref[...] + _mm(a_ref[...], w_ref[...])


def _outproj(h2, a2, w):
    rows, d = h2.shape
    t = ROW_TILE
    return pl.pallas_call(
        _out_kernel,
        grid=(rows // t,),
        in_specs=[pl.BlockSpec((t, d), lambda i: (i, 0)), pl.BlockSpec((t, a2.shape[1]), lambda i: (i, 0)),
                  _const_spec(w.shape)],
        out_specs=pl.BlockSpec((t, d), lambda i: (i, 0)),
        out_shape=jax.ShapeDtypeStruct((rows, d), F32),
        compiler_params=_params("arbitrary"),
        name="outproj1",
    )(h2, a2, w)


def _bias_tables(rel_bias, seq_len):
    del seq_len
    nv = (NEAR_BLOCKS + 1) * BLOCK
    vec = rel_bias[_t5_bucket(jnp.arange(nv, dtype=I32))].astype(F32).T

    nh = vec.shape[0]
    vecp = jnp.concatenate([jnp.broadcast_to(vec[:, :1], (nh, BLOCK - 1)), vec], axis=1)

    def toeplitz(g, rows, cols):
        w = rows + cols
        g2 = jnp.concatenate([g[..., rows - 1:rows - 1 + cols], g[..., :1], g[..., :rows - 1]], axis=-1)
        flat = jnp.tile(g2, (1,) * (g.ndim - 1) + (rows,))[..., :rows * (w - 1)]
        return flat.reshape(g.shape[:-1] + (rows, w - 1))[..., :cols]

    bias0 = jnp.transpose(toeplitz(vecp[:, :3 * BLOCK - 1], 2 * BLOCK, BLOCK), (0, 2, 1))
    r = vecp[:, :(NEAR_BLOCKS + 1) * BLOCK].reshape(nh, NEAR_BLOCKS + 1, BLOCK)
    wins = jnp.concatenate([r[:, :-1], r[:, 1:, :BLOCK - 1]], axis=2)
    tab = toeplitz(wins, BLOCK, BLOCK) - rel_bias[NUM_BUCKETS - 1].astype(F32)[:, None, None, None]
    tab = jnp.concatenate([tab, jnp.zeros((tab.shape[0], 1, BLOCK, BLOCK), F32)], axis=1)
    return bias0, tab


def kernel(x, rel_bias, norm_g, ev_w_in, ev_w_out, ev_q_norm_g, ev_k_norm_g, ev_sinks, ev_ssm_log_dt, ev_ssm_a_re,
           ev_ssm_a_im, ev_ssm_b_re, ev_ssm_b_im, ev_ssm_c_re, ev_ssm_c_im, ev_ssm_d, ev_glu_w, ev_glu_b, od_w_in,
           od_w_out, od_q_norm_g, od_k_norm_g):
    b, l, d = x.shape
    assert l % KEY_CHUNK == 0 and l % ROW_TILE == 0
    assert (NEAR_BLOCKS - 1) * BLOCK + 1 >= 16 * 64 ** (15 / 16) + 1
    bias0, tab = _bias_tables(rel_bias, l)

    w0 = ev_w_in[0]
    hd = A_HEAD_DIM
    o0 = np.cumsum([0, A_WIDTH, A_KV_HEADS * hd, A_KV_HEADS * hd, A_WIDTH, A_WIDTH, A_WIDTH])
    wq0, wk0, wv0, wga, wu, wgb = (w0[:, o0[n]:o0[n + 1]] for n in range(6))
    z = jnp.zeros((d, hd), w0.dtype)

    def variants(w):
        return jnp.concatenate([c for g in range(A_KV_HEADS) for c in (w[:, g * hd:(g + 1) * hd], z, z,
                                                                       w[:, g * hd:(g + 1) * hd])], axis=1)

    w0x = jnp.concatenate([wq0, variants(wk0), variants(wv0), wga, wu, wgb], axis=1).astype(BF16)
    qg2 = jnp.tile(ev_q_norm_g[0], 2)[None, :]
    kg2 = jnp.tile(ev_k_norm_g[0], 2)[None, :]
    q0, k0, v0, sga, u, sgb = _proj0(x.reshape(b * l, d), norm_g[0][None, :], w0x, qg2, kg2)
    shp = lambda a: a.reshape(b, l, a.shape[-1])
    sinks = jnp.broadcast_to(ev_sinks[0][:, None], (A_HEADS, LANES)).astype(F32)
    att0 = _attn0(shp(q0), shp(k0), shp(v0), shp(sga), bias0, sinks)
    bmat, cre, cim, sc = _s5_prep(ev_ssm_log_dt[0], ev_ssm_a_re[0], ev_ssm_a_im[0], ev_ssm_b_re[0], ev_ssm_b_im[0],
                                  ev_ssm_c_re[0], ev_ssm_c_im[0])
    ssm0 = _ssm(shp(u), shp(sgb), bmat, cre, cim, sc, ev_ssm_d[0].reshape(1, -1), ev_glu_w[0].astype(BF16),
                ev_glu_b[0][None, :])

    w1 = od_w_in[0]
    cw = C_HEADS * C_HEAD_DIM
    ckv = C_KV_HEADS * C_HEAD_DIM
    o = np.cumsum([0, cw, ckv, ckv, cw, IDX_HEADS * IDX_DIM, IDX_DIM, IDX_HEADS])
    wq, wk, wv, wg, wqi, wki, ww = (w1[:, o[n]:o[n + 1]] for n in range(7))
    zki = jnp.zeros((d, LANES - IDX_DIM), w1.dtype)
    wki2 = jnp.concatenate([wki, zki, zki, wki], axis=1)
    bf = lambda a: a.astype(BF16)
    h1, q1, k1, vt1, sg1, qi1, ki2, wt1 = _mid(
        x, att0, ssm0, bf(ev_w_out[0]), norm_g[1][None, :], bf(wq), bf(wk), bf(wv.T), bf(wg), bf(wqi), bf(wki2),
        bf(ww.T), od_q_norm_g[0][None, :], od_k_norm_g[0][None, :])
    lb = (1.02 * C_HEAD_DIM ** 0.5 * jnp.max(jnp.abs(od_q_norm_g[0])) * jnp.max(jnp.abs(od_k_norm_g[0]))
          + jnp.max(tab))
    att1 = _dsa(q1, qi1, wt1, sg1, k1, vt1, ki2, tab, jnp.full((1, LANES), lb, F32))
    out = _outproj(h1.reshape(b * l, d), att1.reshape(b * l, cw), bf(od_w_out[0]))
    return out.reshape(b, l, d)
```

```python
import functools
import math

import jax
import jax.numpy as jnp
import numpy as np
from jax import lax
from jax.experimental import pallas as pl
from jax.experimental.pallas import tpu as pltpu

F32 = jnp.float32
BF16 = jnp.bfloat16
I32 = jnp.int32

LANES = 128
SUBLANES = 8
VMEM_LIMIT = 56 * 1024 * 1024

BLOCK = 128
WINDOW = 128
A_HEADS = 8
A_HEAD_DIM = 64
A_KV_HEADS = 2
A_WIDTH = A_HEADS * A_HEAD_DIM
SSM_GROUP = 16
SSM_STATE = 64
C_HEADS = 8
C_HEAD_DIM = 128
C_KV_HEADS = 2
IDX_HEADS = 8
IDX_DIM = 64
TOPK_MAX = 256
NUM_BUCKETS = 32
REL_MAX_DIST = 1024
EPS = 1e-6
NEG_INF = -1e30
INT_MIN = -(2 ** 31)
KEY_MIN_NORMAL = 0x00800000
KEY_POS_INF = 0x7F800000
KEY_NEG_INF = INT_MIN + 0x007FFFFF
MAGNITUDE_BITS = 0x7FFFFFFF

ROW_TILE = 1024
KEY_CHUNK = 1024
NEAR_BLOCKS = 8
FOLD_CHAINS = 8
COUNT_ROWS = 512
SUM_FLOOR = 1e-30
CAND = 32
STREAMS = 2
NT_DIMS = (((1,), (1,)), ((), ()))


def _t5_bucket(dist):
    n = jnp.maximum(dist, 0)
    max_exact = NUM_BUCKETS // 2
    nf = jnp.maximum(n, 1).astype(F32)
    large = max_exact + (jnp.log(nf / max_exact) / math.log(REL_MAX_DIST / max_exact)
                         * (NUM_BUCKETS - max_exact)).astype(I32)
    large = jnp.minimum(large, NUM_BUCKETS - 1)
    return jnp.where(n < max_exact, n, large)


def _silu(x):
    return x * jax.nn.sigmoid(x)


def _rms(x, g):
    ms = jnp.mean(x * x, axis=-1, keepdims=True)
    return x * lax.rsqrt(ms + EPS) * g


def _mm(a, b):
    return jnp.dot(a, b, preferred_element_type=F32)


def _mm_nt(a, b):
    return lax.dot_general(a, b, NT_DIMS, preferred_element_type=F32)


def _fold(x, op):
    n = x.shape[0] // SUBLANES
    chains = min(FOLD_CHAINS, n)
    accs = [x[r * SUBLANES:(r + 1) * SUBLANES] for r in range(chains)]
    for r in range(chains, n):
        accs[r % chains] = op(accs[r % chains], x[r * SUBLANES:(r + 1) * SUBLANES])
    while len(accs) > 1:
        accs = [op(a, b) for a, b in zip(accs[::2], accs[1::2])] + accs[len(accs) & ~1:]
    return accs[0]


def _params(*sem):
    return pltpu.CompilerParams(dimension_semantics=sem, vmem_limit_bytes=VMEM_LIMIT)


def _const_spec(shape):
    zeros = (0,) * len(shape)
    return pl.BlockSpec(shape, lambda *_: zeros)


def _proj0_kernel(x_ref, g_ref, w_ref, qg_ref, kg_ref, q_ref, k_ref, v_ref, sga_ref, u_ref, sgb_ref):
    hn = _rms(x_ref[...], g_ref[...]).astype(BF16)
    lo = lax.broadcasted_iota(I32, (1, LANES), 1) < A_HEAD_DIM

    def mm(n):
        return _mm(hn, w_ref[:, n * A_WIDTH:(n + 1) * A_WIDTH])

    def segnorm(x, g2):
        sq = x * x
        s_lo = jnp.sum(jnp.where(lo, sq, 0.0), axis=-1, keepdims=True)
        s_hi = jnp.sum(jnp.where(lo, 0.0, sq), axis=-1, keepdims=True)
        inv = jnp.where(lo, lax.rsqrt(s_lo / A_HEAD_DIM + EPS), lax.rsqrt(s_hi / A_HEAD_DIM + EPS))
        return x * inv * g2

    q, k = mm(0), mm(1)
    for p in range(A_WIDTH // LANES):
        sl = slice(p * LANES, (p + 1) * LANES)
        q_ref[:, sl] = (segnorm(q[:, sl], qg_ref[...]) * (A_HEAD_DIM ** -0.5)).astype(BF16)
        k_ref[:, sl] = segnorm(k[:, sl], kg_ref[...]).astype(BF16)
    v_ref[...] = mm(2).astype(BF16)
    sga_ref[...] = _silu(mm(3)).astype(BF16)
    u_ref[...] = mm(4)
    sgb_ref[...] = _silu(mm(5)).astype(BF16)


def _proj0(x2, g, w, qg2, kg2):
    rows, d = x2.shape
    t = ROW_TILE

    def row(n):
        return pl.BlockSpec((t, n), lambda i: (i, 0))

    n = A_WIDTH
    return pl.pallas_call(
        _proj0_kernel,
        grid=(rows // t,),
        in_specs=[row(d), _const_spec((1, d)), _const_spec(w.shape), _const_spec(qg2.shape), _const_spec(kg2.shape)],
        out_specs=[row(n)] * 6,
        out_shape=[jax.ShapeDtypeStruct((rows, n), BF16), jax.ShapeDtypeStruct((rows, n), BF16),
                   jax.ShapeDtypeStruct((rows, n), BF16), jax.ShapeDtypeStruct((rows, n), BF16),
                   jax.ShapeDtypeStruct((rows, n), F32), jax.ShapeDtypeStruct((rows, n), BF16)],
        compiler_params=_params("arbitrary"),
        name="proj0",
    )(x2, g, w, qg2, kg2)


def _attn0_kernel(q_ref, kc_ref, kp_ref, vc_ref, vp_ref, sga_ref, bias_ref, sink_ref, ones_ref, o_ref):
    i = pl.program_id(1)
    kb = jnp.concatenate([kp_ref[0], kc_ref[0]], axis=0)
    vb = jnp.concatenate([vp_ref[0], vc_ref[0]], axis=0)

    def variant(x, g, a):
        n = 2 * g + a
        return x[:, n * LANES:(n + 1) * LANES]

    row = lax.broadcasted_iota(I32, (BLOCK, 2 * BLOCK), 0)
    col = lax.broadcasted_iota(I32, (BLOCK, 2 * BLOCK), 1)
    d = row + BLOCK - col
    mask = (d >= 0) & (d < WINDOW) & ((i > 0) | (col >= BLOCK))

    lgs, sinks = [], []
    for p in range(A_HEADS // 2):
        qp = q_ref[0, :, p * LANES:(p + 1) * LANES]
        for a in range(2):
            h = 2 * p + a
            lgs.append(jnp.where(mask, _mm_nt(qp, variant(kb, p // 2, a)) + bias_ref[h], NEG_INF))
            sinks.append(jnp.broadcast_to(sink_ref[h:h + 1, 0:1], (BLOCK, 1)))
    lg = jnp.concatenate(lgs, axis=0)
    sink = jnp.concatenate(sinks, axis=0)
    m = jnp.maximum(jnp.max(lg, axis=-1, keepdims=True), sink)
    e = jnp.exp(lg - m).astype(BF16)
    inv = 1.0 / (_mm(e, ones_ref[...]) + jnp.exp(sink - m))
    for p in range(A_HEADS // 2):
        sl = slice(p * LANES, (p + 1) * LANES)
        acc = jnp.zeros((BLOCK, LANES), F32)
        for a in range(2):
            h = 2 * p + a
            hs = slice(h * BLOCK, (h + 1) * BLOCK)
            acc = acc + _mm(e[hs], variant(vb, p // 2, a)) * inv[hs]
        o_ref[0, :, sl] = (acc * sga_ref[0, :, sl].astype(F32)).astype(BF16)


def _attn0(q, k, v, sga, bias0, sinks):
    b, l, _ = q.shape
    nb = l // BLOCK
    ones = jnp.ones((2 * BLOCK, LANES), BF16)

    def cur(n):
        return pl.BlockSpec((1, BLOCK, n), lambda bb, i: (bb, i, 0))

    def prev(n):
        return pl.BlockSpec((1, BLOCK, n), lambda bb, i: (bb, jnp.maximum(i - 1, 0), 0))

    return pl.pallas_call(
        _attn0_kernel,
        grid=(b, nb),
        in_specs=[cur(512), cur(512), prev(512), cur(512), prev(512), cur(512),
                  _const_spec(bias0.shape), _const_spec(sinks.shape), _const_spec(ones.shape)],
        out_specs=cur(512),
        out_shape=jax.ShapeDtypeStruct((b, l, 512), BF16),
        compiler_params=_params("arbitrary", "arbitrary"),
        name="attn0",
    )(q, k, k, v, v, sga, bias0, sinks, ones)


def _ssm_kernel(u_ref, sgb_ref, bmat_ref, cre_ref, cim_ref, sc_ref, d_ref, gw_ref, gb_ref, o_ref, xre_ref, xim_ref):
    t = u_ref.shape[1]
    nq = bmat_ref.shape[0]
    half = bmat_ref.shape[2] // 2

    @pl.when(pl.program_id(1) == 0)
    def _():
        xre_ref[0:SUBLANES, :] = jnp.zeros((SUBLANES, xre_ref.shape[1]), F32)
        xim_ref[0:SUBLANES, :] = jnp.zeros((SUBLANES, xim_ref.shape[1]), F32)

    u = u_ref[0]
    ub = u.astype(BF16)
    for q in range(nq):
        bu = _mm(ub[:, q * LANES:(q + 1) * LANES], bmat_ref[q])
        xre_ref[SUBLANES:, q * half:(q + 1) * half] = bu[:, :half]
        xim_ref[SUBLANES:, q * half:(q + 1) * half] = bu[:, half:]

    def scan(r, _):
        base = pl.multiple_of(SUBLANES + r * SUBLANES, SUBLANES)
        xr = xre_ref[pl.ds(base, SUBLANES), :]
        xi = xim_ref[pl.ds(base, SUBLANES), :]
        for s, k in enumerate((1, 2, 4)):
            ar = sc_ref[2 * s]
            ai = sc_ref[2 * s + 1]
            sr = pltpu.roll(xr, k, axis=0)
            si = pltpu.roll(xi, k, axis=0)
            xr, xi = xr + ar * sr - ai * si, xi + ar * si + ai * sr
        cr = xre_ref[pl.ds(base - 1, 1), :]
        ci = xim_ref[pl.ds(base - 1, 1), :]
        pr = sc_ref[6]
        pi = sc_ref[7]
        xre_ref[pl.ds(base, SUBLANES), :] = xr + pr * cr - pi * ci
        xim_ref[pl.ds(base, SUBLANES), :] = xi + pr * ci + pi * cr
        return 0

    lax.fori_loop(0, t // SUBLANES, scan, 0, unroll=2)
    xre_ref[0:SUBLANES, :] = xre_ref[t:t + SUBLANES, :]
    xim_ref[0:SUBLANES, :] = xim_ref[t:t + SUBLANES, :]

    ys = []
    for q in range(nq):
        xr = xre_ref[SUBLANES:, q * half:(q + 1) * half].astype(BF16)
        xi = xim_ref[SUBLANES:, q * half:(q + 1) * half].astype(BF16)
        ys.append(_mm(xr, cre_ref[q]) + _mm(xi, cim_ref[q]))
    y = jnp.concatenate(ys, axis=1) + d_ref[...] * u
    y = jax.nn.gelu(y).astype(BF16)
    hh = _mm(y, gw_ref[...]) + gb_ref[...]
    w = hh.shape[1] // 2
    o_ref[0] = (hh[:, :w] * jax.nn.sigmoid(hh[:, w:]) * sgb_ref[0].astype(F32)).astype(BF16)


def _ssm(u, sgb, bmat, cre, cim, sc, dskip, gw, gb):
    b, l, w = u.shape
    t = ROW_TILE
    ns = sc.shape[-1]

    def row(n):
        return pl.BlockSpec((1, t, n), lambda bb, i: (bb, i, 0))

    return pl.pallas_call(
        _ssm_kernel,
        grid=(b, l // t),
        in_specs=[row(w), row(w), _const_spec(bmat.shape), _const_spec(cre.shape), _const_spec(cim.shape),
                  _const_spec(sc.shape), _const_spec(dskip.shape), _const_spec(gw.shape), _const_spec(gb.shape)],
        out_specs=row(w),
        out_shape=jax.ShapeDtypeStruct((b, l, w), BF16),
        scratch_shapes=[pltpu.VMEM((SUBLANES + t, ns), F32), pltpu.VMEM((SUBLANES + t, ns), F32)],
        compiler_params=_params("arbitrary", "arbitrary"),
        name="ssm",
    )(u, sgb, bmat, cre, cim, sc, dskip, gw, gb)


def _s5_prep(log_dt, a_re, a_im, b_re, b_im, c_re, c_im):
    g, p = a_re.shape
    h = b_re.shape[-1]
    gl = LANES // h
    nq = g // gl
    dt = jnp.exp(log_dt)[:, None]
    mag = jnp.exp(a_re * dt)
    ang = a_im * dt
    ab_re = mag * jnp.cos(ang)
    ab_im = mag * jnp.sin(ang)
    den = a_re * a_re + a_im * a_im
    n_re = ab_re - 1.0
    n_im = ab_im
    f_re = (n_re * a_re + n_im * a_im) / den
    f_im = (n_im * a_re - n_re * a_im) / den
    bb_re = f_re[..., None] * b_re - f_im[..., None] * b_im
    bb_im = f_re[..., None] * b_im + f_im[..., None] * b_re
    eye = jnp.eye(gl, dtype=F32)

    def bdiag_in(m):
        m = m.reshape(nq, gl, p, h)
        return jnp.einsum('qgph,gk->qghkp', m, eye).reshape(nq, gl * h, gl * p)

    def bdiag_out(m):
        m = m.reshape(nq, gl, h, p)
        return jnp.einsum('qghp,gk->qgpkh', m, eye).reshape(nq, gl * p, gl * h)

    bmat = jnp.concatenate([bdiag_in(bb_re), bdiag_in(bb_im)], axis=2).astype(BF16)
    cre = bdiag_out(c_re).astype(BF16)
    cim = bdiag_out(-c_im).astype(BF16)

    pw = [(ab_re.reshape(-1), ab_im.reshape(-1))]
    for _ in range(SUBLANES - 1):
        pr, pi = pw[-1]
        pw.append((pr * pw[0][0] - pi * pw[0][1], pr * pw[0][1] + pi * pw[0][0]))
    rows = jnp.arange(SUBLANES)[:, None]
    sc = []
    for k in (1, 2, 4):
        sc.append(jnp.where(rows >= k, pw[k - 1][0][None, :], 0.0))
        sc.append(jnp.where(rows >= k, pw[k - 1][1][None, :], 0.0))
    sc.append(jnp.stack([pw[r][0] for r in range(SUBLANES)]))
    sc.append(jnp.stack([pw[r][1] for r in range(SUBLANES)]))
    return bmat, cre, cim, jnp.stack(sc).astype(F32)


def _mid_kernel(x_ref, a_ref, s_ref, wo_ref, g_ref, wq_ref, wk_ref, wvt_ref, wg_ref, wqi_ref, wki_ref, wwt_ref,
                qg_ref, kg_ref, h_ref, q_ref, k_ref, vt_ref, sg_ref, qi_ref, ki_ref, wt_ref):
    aw = a_ref.shape[2]
    h = x_ref[0] + _mm(a_ref[0], wo_ref[0:aw, :]) + _mm(s_ref[0], wo_ref[aw:, :])
    h_ref[0] = h
    hn = _rms(h, g_ref[...]).astype(BF16)
    qf = _mm(hn, wq_ref[...])
    for hd in range(C_HEADS):
        sl = slice(hd * C_HEAD_DIM, (hd + 1) * C_HEAD_DIM)
        q_ref[0, :, sl] = (_rms(qf[:, sl], qg_ref[...]) * (C_HEAD_DIM ** -0.5)).astype(BF16)
    kf = _mm(hn, wk_ref[...])
    for hd in range(C_KV_HEADS):
        sl = slice(hd * C_HEAD_DIM, (hd + 1) * C_HEAD_DIM)
        k_ref[0, :, sl] = _rms(kf[:, sl], kg_ref[...]).astype(BF16)
    vt_ref[0] = _mm_nt(wvt_ref[...], hn).astype(BF16)
    sg_ref[0] = _silu(_mm(hn, wg_ref[...])).astype(BF16)
    qi_ref[0] = _mm(hn, wqi_ref[...]).astype(BF16)
    ki_ref[0] = _mm(hn, wki_ref[...]).astype(BF16)
    wt_ref[0] = _mm_nt(wwt_ref[...], hn) * ((IDX_HEADS ** -0.5) * (IDX_DIM ** -0.5))


def _mid(x, att0, ssm0, wo, g, wq, wk, wvt, wg, wqi, wki2, wwt, qg, kg):
    b, l, d = x.shape
    t = ROW_TILE

    def row(n):
        return pl.BlockSpec((1, t, n), lambda bb, i: (bb, i, 0))

    def col(n):
        return pl.BlockSpec((1, n, t), lambda bb, i: (bb, 0, i))

    weights = [wo, g, wq, wk, wvt, wg, wqi, wki2, wwt, qg, kg]
    cw = C_HEADS * C_HEAD_DIM
    ckv = C_KV_HEADS * C_HEAD_DIM
    return pl.pallas_call(
        _mid_kernel,
        grid=(b, l // t),
        in_specs=[row(d), row(att0.shape[2]), row(ssm0.shape[2])] + [_const_spec(w.shape) for w in weights],
        out_specs=[row(d), row(cw), row(ckv), col(ckv), row(cw), row(IDX_HEADS * IDX_DIM), row(2 * LANES),
                   col(IDX_HEADS)],
        out_shape=[jax.ShapeDtypeStruct((b, l, d), F32), jax.ShapeDtypeStruct((b, l, cw), BF16),
                   jax.ShapeDtypeStruct((b, l, ckv), BF16), jax.ShapeDtypeStruct((b, ckv, l), BF16),
                   jax.ShapeDtypeStruct((b, l, cw), BF16), jax.ShapeDtypeStruct((b, l, IDX_HEADS * IDX_DIM), BF16),
                   jax.ShapeDtypeStruct((b, l, 2 * LANES), BF16), jax.ShapeDtypeStruct((b, IDX_HEADS, l), F32)],
        compiler_params=_params("arbitrary", "arbitrary"),
        name="mid",
    )(x, att0, ssm0, *weights)


def _dsa_kernel(q_ref, qi_ref, wt_ref, sg_ref, k_ref, vt_ref, ki_ref, tab_ref, lb_ref, o_ref,
                sc_ref, best_ref, x_ref, acc_ref, *, seq_len, topk):
    i = pl.program_id(1)
    ck = KEY_CHUNK
    per = ck // BLOCK
    nch = (i + per) // per
    t_row = i * BLOCK + lax.broadcasted_iota(I32, (1, LANES), 1)
    kiota = lax.broadcasted_iota(I32, (ck, LANES), 0)

    def chunk_off(c):
        return pl.multiple_of(c * ck, ck)

    qi = qi_ref[0]
    qi_stack = [jnp.concatenate([qi[:, (2 * s) * LANES:(2 * s + 1) * LANES],
                                 qi[:, (2 * s + 1) * LANES:(2 * s + 2) * LANES]], axis=0) for s in range(2)]
    wt = wt_ref[0]

    def score_chunk(c, masked):
        off = chunk_off(c)
        sc = jnp.zeros((ck, LANES), F32)
        for a in range(2):
            kk = ki_ref[0, pl.ds(off, ck), a * LANES:(a + 1) * LANES]
            for s in range(2):
                r = _mm_nt(kk, qi_stack[s])
                for j in range(2):
                    hd = 2 * (2 * s + j) + a
                    sc = sc + jnp.maximum(r[:, j * LANES:(j + 1) * LANES], 0.0) * wt[hd:hd + 1, :]
        if masked:
            sc = jnp.where(off + kiota <= t_row, sc, NEG_INF)
        sc_ref[pl.ds(off, ck), :] = sc
        return _fold(sc, jnp.maximum)

    def score_body(c, mx):
        return jnp.maximum(mx, score_chunk(c, False))

    smax = lax.fori_loop(0, nch - 1, score_body, jnp.full((SUBLANES, LANES), NEG_INF, F32))
    smax = jnp.max(jnp.maximum(smax, score_chunk(nch - 1, True)), axis=0, keepdims=True)

    def count(*preds):
        rows = COUNT_ROWS
        sub = ck // rows

        def body(c, accs):
            out = []
            for u in range(sub):
                off = pl.multiple_of(c * ck + u * rows, rows)
                s = sc_ref[pl.ds(off, rows), :]
                for n, pred in enumerate(preds):
                    ind = pred(s, off).astype(I32)
                    out.append(accs[u * len(preds) + n]
                               + jnp.sum(ind.reshape(rows // SUBLANES, SUBLANES, LANES), axis=0))
            return tuple(out)

        accs = lax.fori_loop(0, nch, body, tuple(jnp.zeros((SUBLANES, LANES), I32) for _ in range(sub * len(preds))))
        res = [jnp.sum(sum(accs[n::len(preds)]), axis=0, keepdims=True) for n in range(len(preds))]
        return res[0] if len(preds) == 1 else res

    def key_value(k):
        return pltpu.bitcast(k ^ ((k >> 31) & MAGNITUDE_BITS), F32)

    def count_ge(k):
        thr = key_value(k)
        return count(lambda s, off: s >= thr)

    def full(v):
        return jnp.full((1, LANES), v, I32)

    def bisect(_, st):
        lo, hi, c_lo, c_hi = st
        mid = (lo >> 1) + (hi >> 1) + (lo & hi & 1)
        c = count_ge(mid)
        ge = c >= topk
        return jnp.where(ge, mid, lo), jnp.where(ge, hi, mid), jnp.where(ge, c, c_lo), jnp.where(ge, c_hi, c)

    searching = (i + 1) * BLOCK > topk

    def float_key(x):
        bits = pltpu.bitcast(x, I32)
        return bits ^ ((bits >> 31) & MAGNITUDE_BITS)

    def search():
        k_lo = float_key(smax * 0.125)
        c = count_ge(k_lo)
        ok = (smax > 0.0) & (c >= topk)
        trips = jnp.where(jnp.min(jnp.where(ok, 1, 0)) > 0, 25, 32)
        st = (jnp.where(ok, k_lo, KEY_NEG_INF), float_key(smax) + 1, jnp.where(ok, c, nch * ck), full(0))
        out = lax.fori_loop(0, trips, bisect, st)
        return out[0], out[2], out[3]

    def exchange(v, a, b):
        v[a], v[b] = jnp.maximum(v[a], v[b]), jnp.minimum(v[a], v[b])

    def sort_desc(v):
        n, k = len(v), 2
        while k <= n:
            j = k // 2
            while j >= 1:
                for a in range(n):
                    b = a ^ j
                    if b > a:
                        exchange(v, *((a, b) if (a & k) == 0 else (b, a)))
                j //= 2
            k *= 2

    def merge_top(best, blk):
        n = len(best)
        v = [jnp.maximum(best[r], blk[n - 1 - r]) for r in range(n)]
        j = n // 2
        while j >= 1:
            for a in range(n):
                if a ^ j > a:
                    exchange(v, a, a ^ j)
            j //= 2
        return v

    crow = STREAMS * CAND * SUBLANES

    def cand_body(c, _):
        blk_all = sc_ref[pl.ds(pl.multiple_of(c * crow, crow), crow), :]
        for st in range(STREAMS):
            blk = [blk_all[(STREAMS * r + st) * SUBLANES:(STREAMS * r + st + 1) * SUBLANES] for r in range(CAND)]
            sort_desc(blk)
            base = st * CAND * SUBLANES
            best = [best_ref[base + r * SUBLANES:base + (r + 1) * SUBLANES, :] for r in range(CAND)]
            for r, x in enumerate(merge_top(best, blk)):
                best_ref[base + r * SUBLANES:base + (r + 1) * SUBLANES, :] = x
        return 0

    def cand_search():
        best_ref[...] = jnp.full(best_ref.shape, -jnp.inf, F32)
        lax.fori_loop(0, (i + crow // BLOCK) // (crow // BLOCK), cand_body, 0)

        def all_sublanes(x):
            for shift in (4, 2, 1):
                x = x + pltpu.roll(x, shift, axis=0)
            return x

        def count_cand(k):
            thr = key_value(k)[None]
            parts = [jnp.sum((best_ref[r:r + COUNT_ROWS, :].reshape(COUNT_ROWS // SUBLANES, SUBLANES, LANES)
                              >= thr).astype(I32), axis=0) for r in range(0, crow, COUNT_ROWS)]
            return all_sublanes(sum(parts))

        def step(_, st):
            lo, hi = st
            mid = (lo >> 1) + (hi >> 1) + (lo & hi & 1)
            take = ~((count_cand(mid) - topk) >> 31)
            return (mid & take) | (lo & ~take), (hi & take) | (mid & ~take)

        smax8 = jnp.broadcast_to(smax, (SUBLANES, LANES))
        k_lo = float_key(smax8 * 0.125)
        ok = (smax8 > 0.0) & (count_cand(k_lo) >= topk)
        trips = jnp.where(jnp.min(jnp.where(ok, 1, 0)) > 0, 25, 32)
        vk8, _ = lax.fori_loop(0, trips, step, (jnp.where(ok, k_lo, KEY_NEG_INF), float_key(smax8) + 1))
        vk = vk8[0:1]
        thr = key_value(vk)
        above_cand = jnp.sum((best_ref[...] > thr).astype(I32), axis=0, keepdims=True)
        c_ge, c_gt = count(lambda s, off: s >= thr, lambda s, off: s > thr)
        complete = jnp.min(jnp.where(c_gt == above_cand, 1, 0)) > 0
        return lax.cond(complete, lambda: (vk, c_ge, c_gt), search)

    vkey, c_lo, c_hi = lax.cond(searching, cand_search, lambda: (full(KEY_NEG_INF), full(topk), full(0)))
    vthr = key_value(vkey)
    need = topk - c_hi
    ties = c_lo - c_hi

    def tie_search():
        def split_step():
            nxt = vkey + 1
            nxt = jnp.where((nxt > 0) & (nxt < KEY_MIN_NORMAL), KEY_MIN_NORMAL, nxt)
            step = key_value(nxt) - vthr

            def split(_, st):
                fl, fh = st
                fm = 0.5 * (fl + fh)
                t = vthr + fm * step
                ge = count(lambda s, off: s >= t) >= topk
                return jnp.where(ge, fm, fl), jnp.where(ge, fh, fm)

            fl, _ = lax.fori_loop(0, 26, split, (jnp.zeros((1, LANES), F32), jnp.ones((1, LANES), F32)))
            t = vthr + fl * step
            return t, topk - count(lambda s, off: s > t)

        inside = jnp.max(ties - count(lambda s, off: s == vthr)) > 0
        thr, want = lax.cond(inside, split_step, lambda: (vthr, need))

        def body(_, st):
            lj, hj = st
            mid = (lj + hj) >> 1
            c = count(lambda s, off: (s == thr) & (off + kiota[:COUNT_ROWS] <= mid))
            ok = c >= want
            return jnp.where(ok, lj, mid), jnp.where(ok, mid, hj)

        _, hj = lax.fori_loop(0, 14, body, (full(-1), full(0) + (nch * ck - 1)))
        return thr, hj

    any_tie = searching & (jnp.max(ties - need) > 0)
    vthr, jmax = lax.cond(any_tie, tie_search, lambda: (vthr, full(seq_len)))

    def selection_mask(off):
        s = sc_ref[pl.ds(off, ck), :]
        s_idx = off + kiota
        sel = ((s > vthr) | ((s == vthr) & (s_idx <= jmax))) & (s_idx <= t_row)
        madd = jnp.where(sel, 0.0, NEG_INF)
        sc_ref[pl.ds(off, ck), :] = madd
        return madd

    q = q_ref[0]
    n_far = jnp.maximum((i - NEAR_BLOCKS + 1) // per, 0)
    hpg = C_HEADS // C_KV_HEADS
    npair = C_HEADS // 2
    q_pairs = [jnp.concatenate([q[:, (2 * j) * LANES:(2 * j + 1) * LANES],
                                q[:, (2 * j + 1) * LANES:(2 * j + 2) * LANES]], axis=0) for j in range(npair)]

    def bias_rows(hd, c):
        return jnp.concatenate([tab_ref[hd, jnp.clip(i - (c * per + r), 0, NEAR_BLOCKS)] for r in range(per)],
                               axis=0)

    def emit(hd, num, den):
        sl = slice(hd * LANES, (hd + 1) * LANES)
        o_ref[0, :, sl] = ((num / den).T * sg_ref[0, :, sl].astype(F32)).astype(BF16)

    def exact_attention():
        for g in range(C_KV_HEADS):
            def stage_body(near, g=g):
                def body(c, mx):
                    off = chunk_off(c)
                    madd = sc_ref[pl.ds(off, ck), :]
                    kc = k_ref[0, pl.ds(off, ck), g * LANES:(g + 1) * LANES]
                    out = []
                    for jj in range(hpg // 2):
                        lg = _mm_nt(kc, q_pairs[g * (hpg // 2) + jj])
                        for a in range(2):
                            hl = 2 * jj + a
                            x = lg[:, a * LANES:(a + 1) * LANES] + madd
                            if near:
                                x = x + bias_rows(hpg * g + hl, c)
                            x_ref[hl, pl.ds(off, ck), :] = x
                            out.append(jnp.maximum(mx[hl], _fold(x, jnp.maximum)))
                    return tuple(out)

                return body

            mx = tuple(jnp.full((SUBLANES, LANES), NEG_INF, F32) for _ in range(hpg))
            mx = lax.fori_loop(0, n_far, stage_body(False), mx)
            mx = lax.fori_loop(n_far, nch, stage_body(True), mx)
            m = [jnp.max(v, axis=0, keepdims=True) for v in mx]
            acc_ref[...] = jnp.zeros(acc_ref.shape, F32)

            def att_body(c, ls, g=g, m=m):
                off = chunk_off(c)
                vt = vt_ref[0, g * LANES:(g + 1) * LANES, pl.ds(off, ck)]
                out = []
                for jj in range(hpg // 2):
                    ps = []
                    for a in range(2):
                        hl = 2 * jj + a
                        p = jnp.exp(x_ref[hl, pl.ds(off, ck), :] - m[hl])
                        out.append(ls[hl] + _fold(p, jnp.add))
                        ps.append(p.astype(BF16))
                    acc_ref[jj] += _mm(vt, jnp.concatenate(ps, axis=1))
                return tuple(out)

            ls = lax.fori_loop(0, nch, att_body, tuple(jnp.zeros((SUBLANES, LANES), F32) for _ in range(hpg)))
            for hl in range(hpg):
                emit(hpg * g + hl, acc_ref[hl // 2, :, (hl % 2) * LANES:(hl % 2 + 1) * LANES],
                     jnp.sum(ls[hl], axis=0, keepdims=True))

    lb = lb_ref[...]
    acc_ref[...] = jnp.zeros(acc_ref.shape, F32)

    def stage(c, g, near):
        off = chunk_off(c)
        mb = (selection_mask(off) if g == 0 else sc_ref[pl.ds(off, ck), :]) - lb
        kc = k_ref[0, pl.ds(off, ck), g * LANES:(g + 1) * LANES]
        for jj in range(hpg // 2):
            lg = _mm_nt(kc, q_pairs[g * (hpg // 2) + jj])
            for a in range(2):
                hl = 2 * jj + a
                x = lg[:, a * LANES:(a + 1) * LANES] + mb
                if near:
                    x = x + bias_rows(hpg * g + hl, c)
                x_ref[hl, pl.ds(off, ck), :] = x

    def consume(c, g, ls):
        off = chunk_off(c)
        vt = vt_ref[0, g * LANES:(g + 1) * LANES, pl.ds(off, ck)]
        out = []
        for jj in range(hpg // 2):
            ps = []
            for a in range(2):
                hl = 2 * jj + a
                p = jnp.exp(x_ref[hl, pl.ds(off, ck), :])
                out.append(ls[hl] + _fold(p, jnp.add))
                ps.append(p.astype(BF16))
            acc_ref[g * (hpg // 2) + jj] += _mm(vt, jnp.concatenate(ps, axis=1))
        return tuple(out)

    dens = []
    for g in range(C_KV_HEADS):
        def step(near, g=g):
            def body(c, ls):
                out = consume(c, g, ls)
                stage(c + 1, g, near)
                return out

            return body

        stage(0, g, True)
        ls = tuple(jnp.zeros((SUBLANES, LANES), F32) for _ in range(hpg))
        split = jnp.maximum(n_far - 1, 0)
        ls = lax.fori_loop(0, split, step(False), ls)
        ls = lax.fori_loop(split, nch - 1, step(True), ls)
        ls = consume(nch - 1, g, ls)
        dens += [jnp.sum(v, axis=0, keepdims=True) for v in ls]
    in_range = jnp.min(functools.reduce(jnp.minimum, dens)) > SUM_FLOOR

    @pl.when(in_range)
    def _():
        for hd in range(C_HEADS):
            emit(hd, acc_ref[hd // 2, :, (hd % 2) * LANES:(hd % 2 + 1) * LANES], dens[hd])

    @pl.when(jnp.logical_not(in_range))
    def _():
        exact_attention()


def _dsa(q, qi, wt, sg, k, vt, ki2, tab, lb):
    b, l, cw = q.shape
    nb = l // BLOCK
    topk = min(TOPK_MAX, l // 4)

    def blk(n):
        return pl.BlockSpec((1, BLOCK, n), lambda bb, i: (bb, i, 0))

    def whole(s1, s2):
        return pl.BlockSpec((1, s1, s2), lambda bb, i: (bb, 0, 0), pipeline_mode=pl.Buffered(1))

    hpg = C_HEADS // C_KV_HEADS
    return pl.pallas_call(
        functools.partial(_dsa_kernel, seq_len=l, topk=topk),
        grid=(b, nb),
        in_specs=[blk(cw), blk(qi.shape[2]), pl.BlockSpec((1, IDX_HEADS, BLOCK), lambda bb, i: (bb, 0, i)), blk(cw),
                  whole(l, k.shape[2]), whole(vt.shape[1], l), whole(l, ki2.shape[2]),
                  pl.BlockSpec(tab.shape, lambda bb, i: (0, 0, 0, 0), pipeline_mode=pl.Buffered(1)),
                  _const_spec(lb.shape)],
        out_specs=blk(cw),
        out_shape=jax.ShapeDtypeStruct((b, l, cw), BF16),
        scratch_shapes=[pltpu.VMEM((l, LANES), F32), pltpu.VMEM((STREAMS * CAND * SUBLANES, LANES), F32),
                        pltpu.VMEM((hpg, l, LANES), F32),
                        pltpu.VMEM((C_HEADS // 2, C_HEAD_DIM, 2 * LANES), F32)],
        compiler_params=_params("arbitrary", "arbitrary"),
        name="dsa",
    )(q, qi, wt, sg, k, vt, ki2, tab, lb)


def _out_kernel(h_ref, a_ref, w_ref, o_ref):
    o_ref[...] = h_ref[...] + _mm(a_ref[...], w_ref[...])


def _outproj(h2, a2, w):
    rows, d = h2.shape
    t = ROW_TILE
    return pl.pallas_call(
        _out_kernel,
        grid=(rows // t,),
        in_specs=[pl.BlockSpec((t, d), lambda i: (i, 0)), pl.BlockSpec((t, a2.shape[1]), lambda i: (i, 0)),
                  _const_spec(w.shape)],
        out_specs=pl.BlockSpec((t, d), lambda i: (i, 0)),
        out_shape=jax.ShapeDtypeStruct((rows, d), F32),
        compiler_params=_params("arbitrary"),
        name="outproj1",
    )(h2, a2, w)


def _bias_tables(rel_bias, seq_len):
    del seq_len
    nv = (NEAR_BLOCKS + 1) * BLOCK
    vec = rel_bias[_t5_bucket(jnp.arange(nv, dtype=I32))].astype(F32).T

    nh = vec.shape[0]
    vecp = jnp.concatenate([jnp.broadcast_to(vec[:, :1], (nh, BLOCK - 1)), vec], axis=1)

    def toeplitz(g, rows, cols):
        w = rows + cols
        g2 = jnp.concatenate([g[..., rows - 1:rows - 1 + cols], g[..., :1], g[..., :rows - 1]], axis=-1)
        flat = jnp.tile(g2, (1,) * (g.ndim - 1) + (rows,))[..., :rows * (w - 1)]
        return flat.reshape(g.shape[:-1] + (rows, w - 1))[..., :cols]

    bias0 = jnp.transpose(toeplitz(vecp[:, :3 * BLOCK - 1], 2 * BLOCK, BLOCK), (0, 2, 1))
    r = vecp[:, :(NEAR_BLOCKS + 1) * BLOCK].reshape(nh, NEAR_BLOCKS + 1, BLOCK)
    wins = jnp.concatenate([r[:, :-1], r[:, 1:, :BLOCK - 1]], axis=2)
    tab = toeplitz(wins, BLOCK, BLOCK) - rel_bias[NUM_BUCKETS - 1].astype(F32)[:, None, None, None]
    tab = jnp.concatenate([tab, jnp.zeros((tab.shape[0], 1, BLOCK, BLOCK), F32)], axis=1)
    return bias0, tab


def kernel(x, rel_bias, norm_g, ev_w_in, ev_w_out, ev_q_norm_g, ev_k_norm_g, ev_sinks, ev_ssm_log_dt, ev_ssm_a_re,
           ev_ssm_a_im, ev_ssm_b_re, ev_ssm_b_im, ev_ssm_c_re, ev_ssm_c_im, ev_ssm_d, ev_glu_w, ev_glu_b, od_w_in,
           od_w_out, od_q_norm_g, od_k_norm_g):
    b, l, d = x.shape
    assert l % KEY_CHUNK == 0 and l % ROW_TILE == 0
    assert (NEAR_BLOCKS - 1) * BLOCK + 1 >= 16 * 64 ** (15 / 16) + 1
    bias0, tab = _bias_tables(rel_bias, l)

    w0 = ev_w_in[0]
    hd = A_HEAD_DIM
    o0 = np.cumsum([0, A_WIDTH, A_KV_HEADS * hd, A_KV_HEADS * hd, A_WIDTH, A_WIDTH, A_WIDTH])
    wq0, wk0, wv0, wga, wu, wgb = (w0[:, o0[n]:o0[n + 1]] for n in range(6))
    z = jnp.zeros((d, hd), w0.dtype)

    def variants(w):
        return jnp.concatenate([c for g in range(A_KV_HEADS) for c in (w[:, g * hd:(g + 1) * hd], z, z,
                                                                       w[:, g * hd:(g + 1) * hd])], axis=1)

    w0x = jnp.concatenate([wq0, variants(wk0), variants(wv0), wga, wu, wgb], axis=1).astype(BF16)
    qg2 = jnp.tile(ev_q_norm_g[0], 2)[None, :]
    kg2 = jnp.tile(ev_k_norm_g[0], 2)[None, :]
    q0, k0, v0, sga, u, sgb = _proj0(x.reshape(b * l, d), norm_g[0][None, :], w0x, qg2, kg2)
    shp = lambda a: a.reshape(b, l, a.shape[-1])
    sinks = jnp.broadcast_to(ev_sinks[0][:, None], (A_HEADS, LANES)).astype(F32)
    att0 = _attn0(shp(q0), shp(k0), shp(v0), shp(sga), bias0, sinks)
    bmat, cre, cim, sc = _s5_prep(ev_ssm_log_dt[0], ev_ssm_a_re[0], ev_ssm_a_im[0], ev_ssm_b_re[0], ev_ssm_b_im[0],
                                  ev_ssm_c_re[0], ev_ssm_c_im[0])
    ssm0 = _ssm(shp(u), shp(sgb), bmat, cre, cim, sc, ev_ssm_d[0].reshape(1, -1), ev_glu_w[0].astype(BF16),
                ev_glu_b[0][None, :])

    w1 = od_w_in[0]
    cw = C_HEADS * C_HEAD_DIM
    ckv = C_KV_HEADS * C_HEAD_DIM
    o = np.cumsum([0, cw, ckv, ckv, cw, IDX_HEADS * IDX_DIM, IDX_DIM, IDX_HEADS])
    wq, wk, wv, wg, wqi, wki, ww = (w1[:, o[n]:o[n + 1]] for n in range(7))
    zki = jnp.zeros((d, LANES - IDX_DIM), w1.dtype)
    wki2 = jnp.concatenate([wki, zki, zki, wki], axis=1)
    bf = lambda a: a.astype(BF16)
    h1, q1, k1, vt1, sg1, qi1, ki2, wt1 = _mid(
        x, att0, ssm0, bf(ev_w_out[0]), norm_g[1][None, :], bf(wq), bf(wk), bf(wv.T), bf(wg), bf(wqi), bf(wki2),
        bf(ww.T), od_q_norm_g[0][None, :], od_k_norm_g[0][None, :])
    lb = (1.02 * C_HEAD_DIM ** 0.5 * jnp.max(jnp.abs(od_q_norm_g[0])) * jnp.max(jnp.abs(od_k_norm_g[0]))
          + jnp.max(tab))
    att1 = _dsa(q1, qi1, wt1, sg1, k1, vt1, ki2, tab, jnp.full((1, LANES), lb, F32))
    out = _outproj(h1.reshape(b * l, d), att1.reshape(b * l, cw), bf(od_w_out[0]))
    return out.reshape(b, l, d)
```

```python
import functools
import math

import jax
import jax.numpy as jnp
import numpy as np
from jax import lax
from jax.experimental import pallas as pl
from jax.experimental.pallas import tpu as pltpu

F32 = jnp.float32
BF16 = jnp.bfloat16
I32 = jnp.int32

LANES = 128
SUBLANES = 8
VMEM_LIMIT = 56 * 1024 * 1024

BLOCK = 128
WINDOW = 128
A_HEADS = 8
A_HEAD_DIM = 64
A_KV_HEADS = 2
A_WIDTH = A_HEADS * A_HEAD_DIM
SSM_GROUP = 16
SSM_STATE = 64
C_HEADS = 8
C_HEAD_DIM = 128
C_KV_HEADS = 2
IDX_HEADS = 8
IDX_DIM = 64
TOPK_MAX = 256
NUM_BUCKETS = 32
REL_MAX_DIST = 1024
EPS = 1e-6
NEG_INF = -1e30
INT_MIN = -(2 ** 31)
KEY_MIN_NORMAL = 0x00800000
KEY_POS_INF = 0x7F800000
KEY_NEG_INF = INT_MIN + 0x007FFFFF
MAGNITUDE_BITS = 0x7FFFFFFF

ROW_TILE = 1024
KEY_CHUNK = 1024
NEAR_BLOCKS = 8
FOLD_CHAINS = 8
COUNT_ROWS = 512
SUM_FLOOR = 1e-30
CAND = 32
STREAMS = 2
NT_DIMS = (((1,), (1,)), ((), ()))


def _t5_bucket(dist):
    n = jnp.maximum(dist, 0)
    max_exact = NUM_BUCKETS // 2
    nf = jnp.maximum(n, 1).astype(F32)
    large = max_exact + (jnp.log(nf / max_exact) / math.log(REL_MAX_DIST / max_exact)
                         * (NUM_BUCKETS - max_exact)).astype(I32)
    large = jnp.minimum(large, NUM_BUCKETS - 1)
    return jnp.where(n < max_exact, n, large)


def _silu(x):
    return x * jax.nn.sigmoid(x)


def _rms(x, g):
    ms = jnp.mean(x * x, axis=-1, keepdims=True)
    return x * lax.rsqrt(ms + EPS) * g


def _mm(a, b):
    return jnp.dot(a, b, preferred_element_type=F32)


def _mm_nt(a, b):
    return lax.dot_general(a, b, NT_DIMS, preferred_element_type=F32)


def _fold(x, op):
    n = x.shape[0] // SUBLANES
    chains = min(FOLD_CHAINS, n)
    accs = [x[r * SUBLANES:(r + 1) * SUBLANES] for r in range(chains)]
    for r in range(chains, n):
        accs[r % chains] = op(accs[r % chains], x[r * SUBLANES:(r + 1) * SUBLANES])
    while len(accs) > 1:
        accs = [op(a, b) for a, b in zip(accs[::2], accs[1::2])] + accs[len(accs) & ~1:]
    return accs[0]


def _params(*sem):
    return pltpu.CompilerParams(dimension_semantics=sem, vmem_limit_bytes=VMEM_LIMIT)


def _const_spec(shape):
    zeros = (0,) * len(shape)
    return pl.BlockSpec(shape, lambda *_: zeros)


def _proj0_kernel(x_ref, g_ref, w_ref, qg_ref, kg_ref, q_ref, k_ref, v_ref, sga_ref, u_ref, sgb_ref):
    hn = _rms(x_ref[...], g_ref[...]).astype(BF16)
    lo = lax.broadcasted_iota(I32, (1, LANES), 1) < A_HEAD_DIM

    def mm(n):
        return _mm(hn, w_ref[:, n * A_WIDTH:(n + 1) * A_WIDTH])

    def segnorm(x, g2):
        sq = x * x
        s_lo = jnp.sum(jnp.where(lo, sq, 0.0), axis=-1, keepdims=True)
        s_hi = jnp.sum(jnp.where(lo, 0.0, sq), axis=-1, keepdims=True)
        inv = jnp.where(lo, lax.rsqrt(s_lo / A_HEAD_DIM + EPS), lax.rsqrt(s_hi / A_HEAD_DIM + EPS))
        return x * inv * g2

    q, k = mm(0), mm(1)
    for p in range(A_WIDTH // LANES):
        sl = slice(p * LANES, (p + 1) * LANES)
        q_ref[:, sl] = (segnorm(q[:, sl], qg_ref[...]) * (A_HEAD_DIM ** -0.5)).astype(BF16)
        k_ref[:, sl] = segnorm(k[:, sl], kg_ref[...]).astype(BF16)
    v_ref[...] = mm(2).astype(BF16)
    sga_ref[...] = _silu(mm(3)).astype(BF16)
    u_ref[...] = mm(4)
    sgb_ref[...] = _silu(mm(5)).astype(BF16)


def _proj0(x2, g, w, qg2, kg2):
    rows, d = x2.shape
    t = ROW_TILE

    def row(n):
        return pl.BlockSpec((t, n), lambda i: (i, 0))

    n = A_WIDTH
    return pl.pallas_call(
        _proj0_kernel,
        grid=(rows // t,),
        in_specs=[row(d), _const_spec((1, d)), _const_spec(w.shape), _const_spec(qg2.shape), _const_spec(kg2.shape)],
        out_specs=[row(n)] * 6,
        out_shape=[jax.ShapeDtypeStruct((rows, n), BF16), jax.ShapeDtypeStruct((rows, n), BF16),
                   jax.ShapeDtypeStruct((rows, n), BF16), jax.ShapeDtypeStruct((rows, n), BF16),
                   jax.ShapeDtypeStruct((rows, n), F32), jax.ShapeDtypeStruct((rows, n), BF16)],
        compiler_params=_params("arbitrary"),
        name="proj0",
    )(x2, g, w, qg2, kg2)


def _attn0_kernel(q_ref, kc_ref, kp_ref, vc_ref, vp_ref, sga_ref, bias_ref, sink_ref, ones_ref, o_ref):
    i = pl.program_id(1)
    kb = jnp.concatenate([kp_ref[0], kc_ref[0]], axis=0)
    vb = jnp.concatenate([vp_ref[0], vc_ref[0]], axis=0)

    def variant(x, g, a):
        n = 2 * g + a
        return x[:, n * LANES:(n + 1) * LANES]

    row = lax.broadcasted_iota(I32, (BLOCK, 2 * BLOCK), 0)
    col = lax.broadcasted_iota(I32, (BLOCK, 2 * BLOCK), 1)
    d = row + BLOCK - col
    mask = (d >= 0) & (d < WINDOW) & ((i > 0) | (col >= BLOCK))

    lgs, sinks = [], []
    for p in range(A_HEADS // 2):
        qp = q_ref[0, :, p * LANES:(p + 1) * LANES]
        for a in range(2):
            h = 2 * p + a
            lgs.append(jnp.where(mask, _mm_nt(qp, variant(kb, p // 2, a)) + bias_ref[h], NEG_INF))
            sinks.append(jnp.broadcast_to(sink_ref[h:h + 1, 0:1], (BLOCK, 1)))
    lg = jnp.concatenate(lgs, axis=0)
    sink = jnp.concatenate(sinks, axis=0)
    m = jnp.maximum(jnp.max(lg, axis=-1, keepdims=True), sink)
    e = jnp.exp(lg - m).astype(BF16)
    inv = 1.0 / (_mm(e, ones_ref[...]) + jnp.exp(sink - m))
    for p in range(A_HEADS // 2):
        sl = slice(p * LANES, (p + 1) * LANES)
        acc = jnp.zeros((BLOCK, LANES), F32)
        for a in range(2):
            h = 2 * p + a
            hs = slice(h * BLOCK, (h + 1) * BLOCK)
            acc = acc + _mm(e[hs], variant(vb, p // 2, a)) * inv[hs]
        o_ref[0, :, sl] = (acc * sga_ref[0, :, sl].astype(F32)).astype(BF16)


def _attn0(q, k, v, sga, bias0, sinks):
    b, l, _ = q.shape
    nb = l // BLOCK
    ones = jnp.ones((2 * BLOCK, LANES), BF16)

    def cur(n):
        return pl.BlockSpec((1, BLOCK, n), lambda bb, i: (bb, i, 0))

    def prev(n):
        return pl.BlockSpec((1, BLOCK, n), lambda bb, i: (bb, jnp.maximum(i - 1, 0), 0))

    return pl.pallas_call(
        _attn0_kernel,
        grid=(b, nb),
        in_specs=[cur(512), cur(512), prev(512), cur(512), prev(512), cur(512),
                  _const_spec(bias0.shape), _const_spec(sinks.shape), _const_spec(ones.shape)],
        out_specs=cur(512),
        out_shape=jax.ShapeDtypeStruct((b, l, 512), BF16),
        compiler_params=_params("arbitrary", "arbitrary"),
        name="attn0",
    )(q, k, k, v, v, sga, bias0, sinks, ones)


def _ssm_kernel(u_ref, sgb_ref, bmat_ref, cre_ref, cim_ref, sc_ref, d_ref, gw_ref, gb_ref, o_ref, xre_ref, xim_ref):
    t = u_ref.shape[1]
    nq = bmat_ref.shape[0]
    half = bmat_ref.shape[2] // 2

    @pl.when(pl.program_id(1) == 0)
    def _():
        xre_ref[0:SUBLANES, :] = jnp.zeros((SUBLANES, xre_ref.shape[1]), F32)
        xim_ref[0:SUBLANES, :] = jnp.zeros((SUBLANES, xim_ref.shape[1]), F32)

    u = u_ref[0]
    ub = u.astype(BF16)
    for q in range(nq):
        bu = _mm(ub[:, q * LANES:(q + 1) * LANES], bmat_ref[q])
        xre_ref[SUBLANES:, q * half:(q + 1) * half] = bu[:, :half]
        xim_ref[SUBLANES:, q * half:(q + 1) * half] = bu[:, half:]

    def scan(r, _):
        base = pl.multiple_of(SUBLANES + r * SUBLANES, SUBLANES)
        xr = xre_ref[pl.ds(base, SUBLANES), :]
        xi = xim_ref[pl.ds(base, SUBLANES), :]
        for s, k in enumerate((1, 2, 4)):
            ar = sc_ref[2 * s]
            ai = sc_ref[2 * s + 1]
            sr = pltpu.roll(xr, k, axis=0)
            si = pltpu.roll(xi, k, axis=0)
            xr, xi = xr + ar * sr - ai * si, xi + ar * si + ai * sr
        cr = xre_ref[pl.ds(base - 1, 1), :]
        ci = xim_ref[pl.ds(base - 1, 1), :]
        pr = sc_ref[6]
        pi = sc_ref[7]
        xre_ref[pl.ds(base, SUBLANES), :] = xr + pr * cr - pi * ci
        xim_ref[pl.ds(base, SUBLANES), :] = xi + pr * ci + pi * cr
        return 0

    lax.fori_loop(0, t // SUBLANES, scan, 0, unroll=2)
    xre_ref[0:SUBLANES, :] = xre_ref[t:t + SUBLANES, :]
    xim_ref[0:SUBLANES, :] = xim_ref[t:t + SUBLANES, :]

    ys = []
    for q in range(nq):
        xr = xre_ref[SUBLANES:, q * half:(q + 1) * half].astype(BF16)
        xi = xim_ref[SUBLANES:, q * half:(q + 1) * half].astype(BF16)
        ys.append(_mm(xr, cre_ref[q]) + _mm(xi, cim_ref[q]))
    y = jnp.concatenate(ys, axis=1) + d_ref[...] * u
    y = jax.nn.gelu(y).astype(BF16)
    hh = _mm(y, gw_ref[...]) + gb_ref[...]
    w = hh.shape[1] // 2
    o_ref[0] = (hh[:, :w] * jax.nn.sigmoid(hh[:, w:]) * sgb_ref[0].astype(F32)).astype(BF16)


def _ssm(u, sgb, bmat, cre, cim, sc, dskip, gw, gb):
    b, l, w = u.shape
    t = ROW_TILE
    ns = sc.shape[-1]

    def row(n):
        return pl.BlockSpec((1, t, n), lambda bb, i: (bb, i, 0))

    return pl.pallas_call(
        _ssm_kernel,
        grid=(b, l // t),
        in_specs=[row(w), row(w), _const_spec(bmat.shape), _const_spec(cre.shape), _const_spec(cim.shape),
                  _const_spec(sc.shape), _const_spec(dskip.shape), _const_spec(gw.shape), _const_spec(gb.shape)],
        out_specs=row(w),
        out_shape=jax.ShapeDtypeStruct((b, l, w), BF16),
        scratch_shapes=[pltpu.VMEM((SUBLANES + t, ns), F32), pltpu.VMEM((SUBLANES + t, ns), F32)],
        compiler_params=_params("arbitrary", "arbitrary"),
        name="ssm",
    )(u, sgb, bmat, cre, cim, sc, dskip, gw, gb)


def _s5_prep(log_dt, a_re, a_im, b_re, b_im, c_re, c_im):
    g, p = a_re.shape
    h = b_re.shape[-1]
    gl = LANES // h
    nq = g // gl
    dt = jnp.exp(log_dt)[:, None]
    mag = jnp.exp(a_re * dt)
    ang = a_im * dt
    ab_re = mag * jnp.cos(ang)
    ab_im = mag * jnp.sin(ang)
    den = a_re * a_re + a_im * a_im
    n_re = ab_re - 1.0
    n_im = ab_im
    f_re = (n_re * a_re + n_im * a_im) / den
    f_im = (n_im * a_re - n_re * a_im) / den
    bb_re = f_re[..., None] * b_re - f_im[..., None] * b_im
    bb_im = f_re[..., None] * b_im + f_im[..., None] * b_re
    eye = jnp.eye(gl, dtype=F32)

    def bdiag_in(m):
        m = m.reshape(nq, gl, p, h)
        return jnp.einsum('qgph,gk->qghkp', m, eye).reshape(nq, gl * h, gl * p)

    def bdiag_out(m):
        m = m.reshape(nq, gl, h, p)
        return jnp.einsum('qghp,gk->qgpkh', m, eye).reshape(nq, gl * p, gl * h)

    bmat = jnp.concatenate([bdiag_in(bb_re), bdiag_in(bb_im)], axis=2).astype(BF16)
    cre = bdiag_out(c_re).astype(BF16)
    cim = bdiag_out(-c_im).astype(BF16)

    pw = [(ab_re.reshape(-1), ab_im.reshape(-1))]
    for _ in range(SUBLANES - 1):
        pr, pi = pw[-1]
        pw.append((pr * pw[0][0] - pi * pw[0][1], pr * pw[0][1] + pi * pw[0][0]))
    rows = jnp.arange(SUBLANES)[:, None]
    sc = []
    for k in (1, 2, 4):
        sc.append(jnp.where(rows >= k, pw[k - 1][0][None, :], 0.0))
        sc.append(jnp.where(rows >= k, pw[k - 1][1][None, :], 0.0))
    sc.append(jnp.stack([pw[r][0] for r in range(SUBLANES)]))
    sc.append(jnp.stack([pw[r][1] for r in range(SUBLANES)]))
    return bmat, cre, cim, jnp.stack(sc).astype(F32)


def _mid_kernel(x_ref, a_ref, s_ref, wo_ref, g_ref, wq_ref, wk_ref, wvt_ref, wg_ref, wqi_ref, wki_ref, wwt_ref,
                qg_ref, kg_ref, h_ref, q_ref, k_ref, vt_ref, sg_ref, qi_ref, ki_ref, wt_ref):
    aw = a_ref.shape[2]
    h = x_ref[0] + _mm(a_ref[0], wo_ref[0:aw, :]) + _mm(s_ref[0], wo_ref[aw:, :])
    h_ref[0] = h
    hn = _rms(h, g_ref[...]).astype(BF16)
    qf = _mm(hn, wq_ref[...])
    for hd in range(C_HEADS):
        sl = slice(hd * C_HEAD_DIM, (hd + 1) * C_HEAD_DIM)
        q_ref[0, :, sl] = (_rms(qf[:, sl], qg_ref[...]) * (C_HEAD_DIM ** -0.5)).astype(BF16)
    kf = _mm(hn, wk_ref[...])
    for hd in range(C_KV_HEADS):
        sl = slice(hd * C_HEAD_DIM, (hd + 1) * C_HEAD_DIM)
        k_ref[0, :, sl] = _rms(kf[:, sl], kg_ref[...]).astype(BF16)
    vt_ref[0] = _mm_nt(wvt_ref[...], hn).astype(BF16)
    sg_ref[0] = _silu(_mm(hn, wg_ref[...])).astype(BF16)
    qi_ref[0] = _mm(hn, wqi_ref[...]).astype(BF16)
    ki_ref[0] = _mm(hn, wki_ref[...]).astype(BF16)
    wt_ref[0] = _mm_nt(wwt_ref[...], hn) * ((IDX_HEADS ** -0.5) * (IDX_DIM ** -0.5))


def _mid(x, att0, ssm0, wo, g, wq, wk, wvt, wg, wqi, wki2, wwt, qg, kg):
    b, l, d = x.shape
    t = ROW_TILE

    def row(n):
        return pl.BlockSpec((1, t, n), lambda bb, i: (bb, i, 0))

    def col(n):
        return pl.BlockSpec((1, n, t), lambda bb, i: (bb, 0, i))

    weights = [wo, g, wq, wk, wvt, wg, wqi, wki2, wwt, qg, kg]
    cw = C_HEADS * C_HEAD_DIM
    ckv = C_KV_HEADS * C_HEAD_DIM
    return pl.pallas_call(
        _mid_kernel,
        grid=(b, l // t),
        in_specs=[row(d), row(att0.shape[2]), row(ssm0.shape[2])] + [_const_spec(w.shape) for w in weights],
        out_specs=[row(d), row(cw), row(ckv), col(ckv), row(cw), row(IDX_HEADS * IDX_DIM), row(2 * LANES),
                   col(IDX_HEADS)],
        out_shape=[jax.ShapeDtypeStruct((b, l, d), F32), jax.ShapeDtypeStruct((b, l, cw), BF16),
                   jax.ShapeDtypeStruct((b, l, ckv), BF16), jax.ShapeDtypeStruct((b, ckv, l), BF16),
                   jax.ShapeDtypeStruct((b, l, cw), BF16), jax.ShapeDtypeStruct((b, l, IDX_HEADS * IDX_DIM), BF16),
                   jax.ShapeDtypeStruct((b, l, 2 * LANES), BF16), jax.ShapeDtypeStruct((b, IDX_HEADS, l), F32)],
        compiler_params=_params("arbitrary", "arbitrary"),
        name="mid",
    )(x, att0, ssm0, *weights)


def _dsa_kernel(q_ref, qi_ref, wt_ref, sg_ref, k_ref, vt_ref, ki_ref, tab_ref, lb_ref, o_ref,
                sc_ref, best_ref, x_ref, acc_ref, *, seq_len, topk):
    i = pl.program_id(1)
    ck = KEY_CHUNK
    per = ck // BLOCK
    nch = (i + per) // per
    t_row = i * BLOCK + lax.broadcasted_iota(I32, (1, LANES), 1)
    kiota = lax.broadcasted_iota(I32, (ck, LANES), 0)

    def chunk_off(c):
        return pl.multiple_of(c * ck, ck)

    qi = qi_ref[0]
    qi_stack = [jnp.concatenate([qi[:, (2 * s) * LANES:(2 * s + 1) * LANES],
                                 qi[:, (2 * s + 1) * LANES:(2 * s + 2) * LANES]], axis=0) for s in range(2)]
    wt = wt_ref[0]

    def score_chunk(c, masked):
        off = chunk_off(c)
        sc = jnp.zeros((ck, LANES), F32)
        for a in range(2):
            kk = ki_ref[0, pl.ds(off, ck), a * LANES:(a + 1) * LANES]
            for s in range(2):
                r = _mm_nt(kk, qi_stack[s])
                for j in range(2):
                    hd = 2 * (2 * s + j) + a
                    sc = sc + jnp.maximum(r[:, j * LANES:(j + 1) * LANES], 0.0) * wt[hd:hd + 1, :]
        if masked:
            sc = jnp.where(off + kiota <= t_row, sc, NEG_INF)
        sc_ref[pl.ds(off, ck), :] = sc
        return _fold(sc, jnp.maximum)

    def score_body(c, mx):
        return jnp.maximum(mx, score_chunk(c, False))

    smax = lax.fori_loop(0, nch - 1, score_body, jnp.full((SUBLANES, LANES), NEG_INF, F32))
    smax = jnp.max(jnp.maximum(smax, score_chunk(nch - 1, True)), axis=0, keepdims=True)

    def count(*preds):
        rows = COUNT_ROWS
        sub = ck // rows

        def body(c, accs):
            out = []
            for u in range(sub):
                off = pl.multiple_of(c * ck + u * rows, rows)
                s = sc_ref[pl.ds(off, rows), :]
                for n, pred in enumerate(preds):
                    ind = pred(s, off).astype(I32)
                    out.append(accs[u * len(preds) + n]
                               + jnp.sum(ind.reshape(rows // SUBLANES, SUBLANES, LANES), axis=0))
            return tuple(out)

        accs = lax.fori_loop(0, nch, body, tuple(jnp.zeros((SUBLANES, LANES), I32) for _ in range(sub * len(preds))))
        res = [jnp.sum(sum(accs[n::len(preds)]), axis=0, keepdims=True) for n in range(len(preds))]
        return res[0] if len(preds) == 1 else res

    def key_value(k):
        return pltpu.bitcast(k ^ ((k >> 31) & MAGNITUDE_BITS), F32)

    def count_ge(k):
        thr = key_value(k)
        return count(lambda s, off: s >= thr)

    def full(v):
        return jnp.full((1, LANES), v, I32)

    def bisect(_, st):
        lo, hi, c_lo, c_hi = st
        mid = (lo >> 1) + (hi >> 1) + (lo & hi & 1)
        c = count_ge(mid)
        ge = c >= topk
        return jnp.where(ge, mid, lo), jnp.where(ge, hi, mid), jnp.where(ge, c, c_lo), jnp.where(ge, c_hi, c)

    searching = (i + 1) * BLOCK > topk

    def float_key(x):
        bits = pltpu.bitcast(x, I32)
        return bits ^ ((bits >> 31) & MAGNITUDE_BITS)

    def search():
        k_lo = float_key(smax * 0.125)
        c = count_ge(k_lo)
        ok = (smax > 0.0) & (c >= topk)
        trips = jnp.where(jnp.min(jnp.where(ok, 1, 0)) > 0, 25, 32)
        st = (jnp.where(ok, k_lo, KEY_NEG_INF), float_key(smax) + 1, jnp.where(ok, c, nch * ck), full(0))
        out = lax.fori_loop(0, trips, bisect, st)
        return out[0], out[2], out[3]

    def exchange(v, a, b):
        v[a], v[b] = jnp.maximum(v[a], v[b]), jnp.minimum(v[a], v[b])

    def sort_desc(v):
        n, k = len(v), 2
        while k <= n:
            j = k // 2
            while j >= 1:
                for a in range(n):
                    b = a ^ j
                    if b > a:
                        exchange(v, *((a, b) if (a & k) == 0 else (b, a)))
                j //= 2
            k *= 2

    def merge_top(best, blk):
        n = len(best)
        v = [jnp.maximum(best[r], blk[n - 1 - r]) for r in range(n)]
        j = n // 2
        while j >= 1:
            for a in range(n):
                if a ^ j > a:
                    exchange(v, a, a ^ j)
            j //= 2
        return v

    crow = STREAMS * CAND * SUBLANES

    def cand_body(c, _):
        blk_all = sc_ref[pl.ds(pl.multiple_of(c * crow, crow), crow), :]
        for st in range(STREAMS):
            blk = [blk_all[(STREAMS * r + st) * SUBLANES:(STREAMS * r + st + 1) * SUBLANES] for r in range(CAND)]
            sort_desc(blk)
            base = st * CAND * SUBLANES
            best = [best_ref[base + r * SUBLANES:base + (r + 1) * SUBLANES, :] for r in range(CAND)]
            for r, x in enumerate(merge_top(best, blk)):
                best_ref[base + r * SUBLANES:base + (r + 1) * SUBLANES, :] = x
        return 0

    def cand_search():
        best_ref[...] = jnp.full(best_ref.shape, -jnp.inf, F32)
        lax.fori_loop(0, (i + crow // BLOCK) // (crow // BLOCK), cand_body, 0)

        def all_sublanes(x):
            for shift in (4, 2, 1):
                x = x + pltpu.roll(x, shift, axis=0)
            return x

        def count_cand(k):
            thr = key_value(k)[None]
            parts = [jnp.sum((best_ref[r:r + COUNT_ROWS, :].reshape(COUNT_ROWS // SUBLANES, SUBLANES, LANES)
                              >= thr).astype(I32), axis=0) for r in range(0, crow, COUNT_ROWS)]
            return all_sublanes(sum(parts))

        def step(_, st):
            lo, hi = st
            mid = (lo >> 1) + (hi >> 1) + (lo & hi & 1)
            take = ~((count_cand(mid) - topk) >> 31)
            return (mid & take) | (lo & ~take), (hi & take) | (mid & ~take)

        smax8 = jnp.broadcast_to(smax, (SUBLANES, LANES))
        k_lo = float_key(smax8 * 0.125)
        ok = (smax8 > 0.0) & (count_cand(k_lo) >= topk)
        trips = jnp.where(jnp.min(jnp.where(ok, 1, 0)) > 0, 25, 32)
        vk8, _ = lax.fori_loop(0, trips, step, (jnp.where(ok, k_lo, KEY_NEG_INF), float_key(smax8) + 1))
        vk = vk8[0:1]
        thr = key_value(vk)
        above_cand = jnp.sum((best_ref[...] > thr).astype(I32), axis=0, keepdims=True)
        c_ge, c_gt = count(lambda s, off: s >= thr, lambda s, off: s > thr)
        complete = jnp.min(jnp.where(c_gt == above_cand, 1, 0)) > 0
        return lax.cond(complete, lambda: (vk, c_ge, c_gt), search)

    vkey, c_lo, c_hi = lax.cond(searching, cand_search, lambda: (full(KEY_NEG_INF), full(topk), full(0)))
    vthr = key_value(vkey)
    need = topk - c_hi
    ties = c_lo - c_hi

    def tie_search():
        def split_step():
            nxt = vkey + 1
            nxt = jnp.where((nxt > 0) & (nxt < KEY_MIN_NORMAL), KEY_MIN_NORMAL, nxt)
            step = key_value(nxt) - vthr

            def split(_, st):
                fl, fh = st
                fm = 0.5 * (fl + fh)
                t = vthr + fm * step
                ge = count(lambda s, off: s >= t) >= topk
                return jnp.where(ge, fm, fl), jnp.where(ge, fh, fm)

            fl, _ = lax.fori_loop(0, 26, split, (jnp.zeros((1, LANES), F32), jnp.ones((1, LANES), F32)))
            t = vthr + fl * step
            return t, topk - count(lambda s, off: s > t)

        inside = jnp.max(ties - count(lambda s, off: s == vthr)) > 0
        thr, want = lax.cond(inside, split_step, lambda: (vthr, need))

        rr = lax.broadcasted_iota(I32, (BLOCK, BLOCK), 0)
        cc = lax.broadcasted_iota(I32, (BLOCK, BLOCK), 1)
        tril = jnp.where(cc <= rr, 1.0, 0.0).astype(BF16)
        want_f = want.astype(F32)

        def body(c, before):
            off = chunk_off(c)
            blocks = [sc_ref[pl.ds(pl.multiple_of(off + r * BLOCK, BLOCK), BLOCK), :] for r in range(per)]
            hits = [s == thr for s in blocks]
            ranks = [_mm(tril, jnp.where(h, 1.0, 0.0).astype(BF16)) for h in hits]
            for r in range(per):
                rank = ranks[r] + before
                sc_ref[pl.ds(pl.multiple_of(off + r * BLOCK, BLOCK), BLOCK), :] = jnp.where(
                    hits[r] & (rank > want_f), NEG_INF, blocks[r])
                before = rank[BLOCK - 1:BLOCK, :]
            return before

        lax.fori_loop(0, nch, body, jnp.zeros((1, LANES), F32))
        return thr

    any_tie = searching & (jnp.max(ties - need) > 0)
    vthr = lax.cond(any_tie, tie_search, lambda: vthr)

    def selection_mask(off):
        s = sc_ref[pl.ds(off, ck), :]
        s_idx = off + kiota
        sel = (s >= vthr) & (s_idx <= t_row)
        madd = jnp.where(sel, 0.0, NEG_INF)
        sc_ref[pl.ds(off, ck), :] = madd
        return madd

    q = q_ref[0]
    n_far = jnp.maximum((i - NEAR_BLOCKS + 1) // per, 0)
    hpg = C_HEADS // C_KV_HEADS
    npair = C_HEADS // 2
    q_pairs = [jnp.concatenate([q[:, (2 * j) * LANES:(2 * j + 1) * LANES],
                                q[:, (2 * j + 1) * LANES:(2 * j + 2) * LANES]], axis=0) for j in range(npair)]

    def bias_rows(hd, c):
        return jnp.concatenate([tab_ref[hd, jnp.clip(i - (c * per + r), 0, NEAR_BLOCKS)] for r in range(per)],
                               axis=0)

    def emit(hd, num, den):
        sl = slice(hd * LANES, (hd + 1) * LANES)
        o_ref[0, :, sl] = ((num / den).T * sg_ref[0, :, sl].astype(F32)).astype(BF16)

    def exact_attention():
        for g in range(C_KV_HEADS):
            def stage_body(near, g=g):
                def body(c, mx):
                    off = chunk_off(c)
                    madd = sc_ref[pl.ds(off, ck), :]
                    kc = k_ref[0, pl.ds(off, ck), g * LANES:(g + 1) * LANES]
                    out = []
                    for jj in range(hpg // 2):
                        lg = _mm_nt(kc, q_pairs[g * (hpg // 2) + jj])
                        for a in range(2):
                            hl = 2 * jj + a
                            x = lg[:, a * LANES:(a + 1) * LANES] + madd
                            if near:
                                x = x + bias_rows(hpg * g + hl, c)
                            x_ref[hl, pl.ds(off, ck), :] = x
                            out.append(jnp.maximum(mx[hl], _fold(x, jnp.maximum)))
                    return tuple(out)

                return body

            mx = tuple(jnp.full((SUBLANES, LANES), NEG_INF, F32) for _ in range(hpg))
            mx = lax.fori_loop(0, n_far, stage_body(False), mx)
            mx = lax.fori_loop(n_far, nch, stage_body(True), mx)
            m = [jnp.max(v, axis=0, keepdims=True) for v in mx]
            acc_ref[...] = jnp.zeros(acc_ref.shape, F32)

            def att_body(c, ls, g=g, m=m):
                off = chunk_off(c)
                vt = vt_ref[0, g * LANES:(g + 1) * LANES, pl.ds(off, ck)]
                out = []
                for jj in range(hpg // 2):
                    ps = []
                    for a in range(2):
                        hl = 2 * jj + a
                        p = jnp.exp(x_ref[hl, pl.ds(off, ck), :] - m[hl])
                        out.append(ls[hl] + _fold(p, jnp.add))
                        ps.append(p.astype(BF16))
                    acc_ref[jj] += _mm(vt, jnp.concatenate(ps, axis=1))
                return tuple(out)

            ls = lax.fori_loop(0, nch, att_body, tuple(jnp.zeros((SUBLANES, LANES), F32) for _ in range(hpg)))
            for hl in range(hpg):
                emit(hpg * g + hl, acc_ref[hl // 2, :, (hl % 2) * LANES:(hl % 2 + 1) * LANES],
                     jnp.sum(ls[hl], axis=0, keepdims=True))

    lb = lb_ref[...]
    acc_ref[...] = jnp.zeros(acc_ref.shape, F32)

    def stage(c, g, near):
        off = chunk_off(c)
        mb = (selection_mask(off) if g == 0 else sc_ref[pl.ds(off, ck), :]) - lb
        kc = k_ref[0, pl.ds(off, ck), g * LANES:(g + 1) * LANES]
        for jj in range(hpg // 2):
            lg = _mm_nt(kc, q_pairs[g * (hpg // 2) + jj])
            for a in range(2):
                hl = 2 * jj + a
                x = lg[:, a * LANES:(a + 1) * LANES] + mb
                if near:
                    x = x + bias_rows(hpg * g + hl, c)
                x_ref[hl, pl.ds(off, ck), :] = x

    def consume(c, g, ls):
        off = chunk_off(c)
        vt = vt_ref[0, g * LANES:(g + 1) * LANES, pl.ds(off, ck)]
        out = []
        for jj in range(hpg // 2):
            ps = []
            for a in range(2):
                hl = 2 * jj + a
                p = jnp.exp(x_ref[hl, pl.ds(off, ck), :])
                out.append(ls[hl] + _fold(p, jnp.add))
                ps.append(p.astype(BF16))
            acc_ref[g * (hpg // 2) + jj] += _mm(vt, jnp.concatenate(ps, axis=1))
        return tuple(out)

    dens = []
    for g in range(C_KV_HEADS):
        def step(near, g=g):
            def body(c, ls):
                out = consume(c, g, ls)
                stage(c + 1, g, near)
                return out

            return body

        stage(0, g, True)
        ls = tuple(jnp.zeros((SUBLANES, LANES), F32) for _ in range(hpg))
        split = jnp.maximum(n_far - 1, 0)
        ls = lax.fori_loop(0, split, step(False), ls)
        ls = lax.fori_loop(split, nch - 1, step(True), ls)
        ls = consume(nch - 1, g, ls)
        dens += [jnp.sum(v, axis=0, keepdims=True) for v in ls]
    in_range = jnp.min(functools.reduce(jnp.minimum, dens)) > SUM_FLOOR

    @pl.when(in_range)
    def _():
        for hd in range(C_HEADS):
            emit(hd, acc_ref[hd // 2, :, (hd % 2) * LANES:(hd % 2 + 1) * LANES], dens[hd])

    @pl.when(jnp.logical_not(in_range))
    def _():
        exact_attention()


def _dsa(q, qi, wt, sg, k, vt, ki2, tab, lb):
    b, l, cw = q.shape
    nb = l // BLOCK
    topk = min(TOPK_MAX, l // 4)

    def blk(n):
        return pl.BlockSpec((1, BLOCK, n), lambda bb, i: (bb, i, 0))

    def whole(s1, s2):
        return pl.BlockSpec((1, s1, s2), lambda bb, i: (bb, 0, 0), pipeline_mode=pl.Buffered(1))

    hpg = C_HEADS // C_KV_HEADS
    return pl.pallas_call(
        functools.partial(_dsa_kernel, seq_len=l, topk=topk),
        grid=(b, nb),
        in_specs=[blk(cw), blk(qi.shape[2]), pl.BlockSpec((1, IDX_HEADS, BLOCK), lambda bb, i: (bb, 0, i)), blk(cw),
                  whole(l, k.shape[2]), whole(vt.shape[1], l), whole(l, ki2.shape[2]),
                  pl.BlockSpec(tab.shape, lambda bb, i: (0, 0, 0, 0), pipeline_mode=pl.Buffered(1)),
                  _const_spec(lb.shape)],
        out_specs=blk(cw),
        out_shape=jax.ShapeDtypeStruct((b, l, cw), BF16),
        scratch_shapes=[pltpu.VMEM((l, LANES), F32), pltpu.VMEM((STREAMS * CAND * SUBLANES, LANES), F32),
                        pltpu.VMEM((hpg, l, LANES), F32),
                        pltpu.VMEM((C_HEADS // 2, C_HEAD_DIM, 2 * LANES), F32)],
        compiler_params=_params("arbitrary", "arbitrary"),
        name="dsa",
    )(q, qi, wt, sg, k, vt, ki2, tab, lb)


def _out_kernel(h_ref, a_ref, w_ref, o_ref):
    o_ref[...] = h_ref[...] + _mm(a_ref[...], w_ref[...])


def _outproj(h2, a2, w):
    rows, d = h2.shape
    t = ROW_TILE
    return pl.pallas_call(
        _out_kernel,
        grid=(rows // t,),
        in_specs=[pl.BlockSpec((t, d), lambda i: (i, 0)), pl.BlockSpec((t, a2.shape[1]), lambda i: (i, 0)),
                  _const_spec(w.shape)],
        out_specs=pl.BlockSpec((t, d), lambda i: (i, 0)),
        out_shape=jax.ShapeDtypeStruct((rows, d), F32),
        compiler_params=_params("arbitrary"),
        name="outproj1",
    )(h2, a2, w)


def _bias_tables(rel_bias, seq_len):
    del seq_len
    nv = (NEAR_BLOCKS + 1) * BLOCK
    vec = rel_bias[_t5_bucket(jnp.arange(nv, dtype=I32))].astype(F32).T

    nh = vec.shape[0]
    vecp = jnp.concatenate([jnp.broadcast_to(vec[:, :1], (nh, BLOCK - 1)), vec], axis=1)

    def toeplitz(g, rows, cols):
        w = rows + cols
        g2 = jnp.concatenate([g[..., rows - 1:rows - 1 + cols], g[..., :1], g[..., :rows - 1]], axis=-1)
        flat = jnp.tile(g2, (1,) * (g.ndim - 1) + (rows,))[..., :rows * (w - 1)]
        return flat.reshape(g.shape[:-1] + (rows, w - 1))[..., :cols]

    bias0 = jnp.transpose(toeplitz(vecp[:, :3 * BLOCK - 1], 2 * BLOCK, BLOCK), (0, 2, 1))
    r = vecp[:, :(NEAR_BLOCKS + 1) * BLOCK].reshape(nh, NEAR_BLOCKS + 1, BLOCK)
    wins = jnp.concatenate([r[:, :-1], r[:, 1:, :BLOCK - 1]], axis=2)
    tab = toeplitz(wins, BLOCK, BLOCK) - rel_bias[NUM_BUCKETS - 1].astype(F32)[:, None, None, None]
    tab = jnp.concatenate([tab, jnp.zeros((tab.shape[0], 1, BLOCK, BLOCK), F32)], axis=1)
    return bias0, tab


def kernel(x, rel_bias, norm_g, ev_w_in, ev_w_out, ev_q_norm_g, ev_k_norm_g, ev_sinks, ev_ssm_log_dt, ev_ssm_a_re,
           ev_ssm_a_im, ev_ssm_b_re, ev_ssm_b_im, ev_ssm_c_re, ev_ssm_c_im, ev_ssm_d, ev_glu_w, ev_glu_b, od_w_in,
           od_w_out, od_q_norm_g, od_k_norm_g):
    b, l, d = x.shape
    assert l % KEY_CHUNK == 0 and l % ROW_TILE == 0
    assert (NEAR_BLOCKS - 1) * BLOCK + 1 >= 16 * 64 ** (15 / 16) + 1
    bias0, tab = _bias_tables(rel_bias, l)

    w0 = ev_w_in[0]
    hd = A_HEAD_DIM
    o0 = np.cumsum([0, A_WIDTH, A_KV_HEADS * hd, A_KV_HEADS * hd, A_WIDTH, A_WIDTH, A_WIDTH])
    wq0, wk0, wv0, wga, wu, wgb = (w0[:, o0[n]:o0[n + 1]] for n in range(6))
    z = jnp.zeros((d, hd), w0.dtype)

    def variants(w):
        return jnp.concatenate([c for g in range(A_KV_HEADS) for c in (w[:, g * hd:(g + 1) * hd], z, z,
                                                                       w[:, g * hd:(g + 1) * hd])], axis=1)

    w0x = jnp.concatenate([wq0, variants(wk0), variants(wv0), wga, wu, wgb], axis=1).astype(BF16)
    qg2 = jnp.tile(ev_q_norm_g[0], 2)[None, :]
    kg2 = jnp.tile(ev_k_norm_g[0], 2)[None, :]
    q0, k0, v0, sga, u, sgb = _proj0(x.reshape(b * l, d), norm_g[0][None, :], w0x, qg2, kg2)
    shp = lambda a: a.reshape(b, l, a.shape[-1])
    sinks = jnp.broadcast_to(ev_sinks[0][:, None], (A_HEADS, LANES)).astype(F32)
    att0 = _attn0(shp(q0), shp(k0), shp(v0), shp(sga), bias0, sinks)
    bmat, cre, cim, sc = _s5_prep(ev_ssm_log_dt[0], ev_ssm_a_re[0], ev_ssm_a_im[0], ev_ssm_b_re[0], ev_ssm_b_im[0],
                                  ev_ssm_c_re[0], ev_ssm_c_im[0])
    ssm0 = _ssm(shp(u), shp(sgb), bmat, cre, cim, sc, ev_ssm_d[0].reshape(1, -1), ev_glu_w[0].astype(BF16),
                ev_glu_b[0][None, :])

    w1 = od_w_in[0]
    cw = C_HEADS * C_HEAD_DIM
    ckv = C_KV_HEADS * C_HEAD_DIM
    o = np.cumsum([0, cw, ckv, ckv, cw, IDX_HEADS * IDX_DIM, IDX_DIM, IDX_HEADS])
    wq, wk, wv, wg, wqi, wki, ww = (w1[:, o[n]:o[n + 1]] for n in range(7))
    zki = jnp.zeros((d, LANES - IDX_DIM), w1.dtype)
    wki2 = jnp.concatenate([wki, zki, zki, wki], axis=1)
    bf = lambda a: a.astype(BF16)
    h1, q1, k1, vt1, sg1, qi1, ki2, wt1 = _mid(
        x, att0, ssm0, bf(ev_w_out[0]), norm_g[1][None, :], bf(wq), bf(wk), bf(wv.T), bf(wg), bf(wqi), bf(wki2),
        bf(ww.T), od_q_norm_g[0][None, :], od_k_norm_g[0][None, :])
    lb = (1.02 * C_HEAD_DIM ** 0.5 * jnp.max(jnp.abs(od_q_norm_g[0])) * jnp.max(jnp.abs(od_k_norm_g[0]))
          + jnp.max(tab))
    att1 = _dsa(q1, qi1, wt1, sg1, k1, vt1, ki2, tab, jnp.full((1, LANES), lb, F32))
    out = _outproj(h1.reshape(b * l, d), att1.reshape(b * l, cw), bf(od_w_out[0]))
    return out.reshape(b, l, d)
```

```python
import functools
import math

import jax
import jax.numpy as jnp
import numpy as np
from jax import lax
from jax.experimental import pallas as pl
from jax.experimental.pallas import tpu as pltpu

F32 = jnp.float32
BF16 = jnp.bfloat16
I32 = jnp.int32

LANES = 128
SUBLANES = 8
VMEM_LIMIT = 56 * 1024 * 1024

BLOCK = 128
WINDOW = 128
A_HEADS = 8
A_HEAD_DIM = 64
A_KV_HEADS = 2
A_WIDTH = A_HEADS * A_HEAD_DIM
SSM_GROUP = 16
SSM_STATE = 64
C_HEADS = 8
C_HEAD_DIM = 128
C_KV_HEADS = 2
IDX_HEADS = 8
IDX_DIM = 64
TOPK_MAX = 256
NUM_BUCKETS = 32
REL_MAX_DIST = 1024
EPS = 1e-6
NEG_INF = -1e30
INT_MIN = -(2 ** 31)
KEY_MIN_NORMAL = 0x00800000
KEY_POS_INF = 0x7F800000
KEY_NEG_INF = INT_MIN + 0x007FFFFF
MAGNITUDE_BITS = 0x7FFFFFFF

ROW_TILE = 1024
KEY_CHUNK = 1024
NEAR_BLOCKS = 8
FOLD_CHAINS = 8
COUNT_ROWS = 512
SUM_FLOOR = 1e-30
CAND = 32
STREAMS = 2
NT_DIMS = (((1,), (1,)), ((), ()))


def _t5_bucket(dist):
    n = jnp.maximum(dist, 0)
    max_exact = NUM_BUCKETS // 2
    nf = jnp.maximum(n, 1).astype(F32)
    large = max_exact + (jnp.log(nf / max_exact) / math.log(REL_MAX_DIST / max_exact)
                         * (NUM_BUCKETS - max_exact)).astype(I32)
    large = jnp.minimum(large, NUM_BUCKETS - 1)
    return jnp.where(n < max_exact, n, large)


def _silu(x):
    return x * jax.nn.sigmoid(x)


def _rms(x, g):
    ms = jnp.mean(x * x, axis=-1, keepdims=True)
    return x * lax.rsqrt(ms + EPS) * g


def _mm(a, b):
    return jnp.dot(a, b, preferred_element_type=F32)


def _mm_nt(a, b):
    return lax.dot_general(a, b, NT_DIMS, preferred_element_type=F32)


def _fold(x, op):
    n = x.shape[0] // SUBLANES
    chains = min(FOLD_CHAINS, n)
    accs = [x[r * SUBLANES:(r + 1) * SUBLANES] for r in range(chains)]
    for r in range(chains, n):
        accs[r % chains] = op(accs[r % chains], x[r * SUBLANES:(r + 1) * SUBLANES])
    while len(accs) > 1:
        accs = [op(a, b) for a, b in zip(accs[::2], accs[1::2])] + accs[len(accs) & ~1:]
    return accs[0]


def _params(*sem):
    return pltpu.CompilerParams(dimension_semantics=sem, vmem_limit_bytes=VMEM_LIMIT)


def _const_spec(shape):
    zeros = (0,) * len(shape)
    return pl.BlockSpec(shape, lambda *_: zeros)


def _proj0_kernel(x_ref, g_ref, w_ref, qg_ref, kg_ref, q_ref, k_ref, v_ref, sga_ref, u_ref, sgb_ref):
    hn = _rms(x_ref[...], g_ref[...]).astype(BF16)
    lo = lax.broadcasted_iota(I32, (1, LANES), 1) < A_HEAD_DIM

    def mm(n):
        return _mm(hn, w_ref[:, n * A_WIDTH:(n + 1) * A_WIDTH])

    def segnorm(x, g2):
        sq = x * x
        s_lo = jnp.sum(jnp.where(lo, sq, 0.0), axis=-1, keepdims=True)
        s_hi = jnp.sum(jnp.where(lo, 0.0, sq), axis=-1, keepdims=True)
        inv = jnp.where(lo, lax.rsqrt(s_lo / A_HEAD_DIM + EPS), lax.rsqrt(s_hi / A_HEAD_DIM + EPS))
        return x * inv * g2

    q, k = mm(0), mm(1)
    for p in range(A_WIDTH // LANES):
        sl = slice(p * LANES, (p + 1) * LANES)
        q_ref[:, sl] = (segnorm(q[:, sl], qg_ref[...]) * (A_HEAD_DIM ** -0.5)).astype(BF16)
        k_ref[:, sl] = segnorm(k[:, sl], kg_ref[...]).astype(BF16)
    v_ref[...] = mm(2).astype(BF16)
    sga_ref[...] = _silu(mm(3)).astype(BF16)
    u_ref[...] = mm(4)
    sgb_ref[...] = _silu(mm(5)).astype(BF16)


def _proj0(x2, g, w, qg2, kg2):
    rows, d = x2.shape
    t = ROW_TILE

    def row(n):
        return pl.BlockSpec((t, n), lambda i: (i, 0))

    n = A_WIDTH
    return pl.pallas_call(
        _proj0_kernel,
        grid=(rows // t,),
        in_specs=[row(d), _const_spec((1, d)), _const_spec(w.shape), _const_spec(qg2.shape), _const_spec(kg2.shape)],
        out_specs=[row(n)] * 6,
        out_shape=[jax.ShapeDtypeStruct((rows, n), BF16), jax.ShapeDtypeStruct((rows, n), BF16),
                   jax.ShapeDtypeStruct((rows, n), BF16), jax.ShapeDtypeStruct((rows, n), BF16),
                   jax.ShapeDtypeStruct((rows, n), F32), jax.ShapeDtypeStruct((rows, n), BF16)],
        compiler_params=_params("arbitrary"),
        name="proj0",
    )(x2, g, w, qg2, kg2)


def _attn0_kernel(q_ref, kc_ref, kp_ref, vc_ref, vp_ref, sga_ref, bias_ref, sink_ref, ones_ref, o_ref):
    i = pl.program_id(1)
    kb = jnp.concatenate([kp_ref[0], kc_ref[0]], axis=0)
    vb = jnp.concatenate([vp_ref[0], vc_ref[0]], axis=0)

    def variant(x, g, a):
        n = 2 * g + a
        return x[:, n * LANES:(n + 1) * LANES]

    row = lax.broadcasted_iota(I32, (BLOCK, 2 * BLOCK), 0)
    col = lax.broadcasted_iota(I32, (BLOCK, 2 * BLOCK), 1)
    d = row + BLOCK - col
    mask = (d >= 0) & (d < WINDOW) & ((i > 0) | (col >= BLOCK))

    lgs, sinks = [], []
    for p in range(A_HEADS // 2):
        qp = q_ref[0, :, p * LANES:(p + 1) * LANES]
        for a in range(2):
            h = 2 * p + a
            lgs.append(jnp.where(mask, _mm_nt(qp, variant(kb, p // 2, a)) + bias_ref[h], NEG_INF))
            sinks.append(jnp.broadcast_to(sink_ref[h:h + 1, 0:1], (BLOCK, 1)))
    lg = jnp.concatenate(lgs, axis=0)
    sink = jnp.concatenate(sinks, axis=0)
    m = jnp.maximum(jnp.max(lg, axis=-1, keepdims=True), sink)
    e = jnp.exp(lg - m).astype(BF16)
    inv = 1.0 / (_mm(e, ones_ref[...]) + jnp.exp(sink - m))
    for p in range(A_HEADS // 2):
        sl = slice(p * LANES, (p + 1) * LANES)
        acc = jnp.zeros((BLOCK, LANES), F32)
        for a in range(2):
            h = 2 * p + a
            hs = slice(h * BLOCK, (h + 1) * BLOCK)
            acc = acc + _mm(e[hs], variant(vb, p // 2, a)) * inv[hs]
        o_ref[0, :, sl] = (acc * sga_ref[0, :, sl].astype(F32)).astype(BF16)


def _attn0(q, k, v, sga, bias0, sinks):
    b, l, _ = q.shape
    nb = l // BLOCK
    ones = jnp.ones((2 * BLOCK, LANES), BF16)

    def cur(n):
        return pl.BlockSpec((1, BLOCK, n), lambda bb, i: (bb, i, 0))

    def prev(n):
        return pl.BlockSpec((1, BLOCK, n), lambda bb, i: (bb, jnp.maximum(i - 1, 0), 0))

    return pl.pallas_call(
        _attn0_kernel,
        grid=(b, nb),
        in_specs=[cur(512), cur(512), prev(512), cur(512), prev(512), cur(512),
                  _const_spec(bias0.shape), _const_spec(sinks.shape), _const_spec(ones.shape)],
        out_specs=cur(512),
        out_shape=jax.ShapeDtypeStruct((b, l, 512), BF16),
        compiler_params=_params("arbitrary", "arbitrary"),
        name="attn0",
    )(q, k, k, v, v, sga, bias0, sinks, ones)


def _ssm_kernel(u_ref, sgb_ref, bmat_ref, cre_ref, cim_ref, sc_ref, d_ref, gw_ref, gb_ref, o_ref, xre_ref, xim_ref):
    t = u_ref.shape[1]
    nq = bmat_ref.shape[0]
    half = bmat_ref.shape[2] // 2

    @pl.when(pl.program_id(1) == 0)
    def _():
        xre_ref[0:SUBLANES, :] = jnp.zeros((SUBLANES, xre_ref.shape[1]), F32)
        xim_ref[0:SUBLANES, :] = jnp.zeros((SUBLANES, xim_ref.shape[1]), F32)

    u = u_ref[0]
    ub = u.astype(BF16)
    for q in range(nq):
        bu = _mm(ub[:, q * LANES:(q + 1) * LANES], bmat_ref[q])
        xre_ref[SUBLANES:, q * half:(q + 1) * half] = bu[:, :half]
        xim_ref[SUBLANES:, q * half:(q + 1) * half] = bu[:, half:]

    def scan(r, _):
        base = pl.multiple_of(SUBLANES + r * SUBLANES, SUBLANES)
        xr = xre_ref[pl.ds(base, SUBLANES), :]
        xi = xim_ref[pl.ds(base, SUBLANES), :]
        for s, k in enumerate((1, 2, 4)):
            ar = sc_ref[2 * s]
            ai = sc_ref[2 * s + 1]
            sr = pltpu.roll(xr, k, axis=0)
            si = pltpu.roll(xi, k, axis=0)
            xr, xi = xr + ar * sr - ai * si, xi + ar * si + ai * sr
        cr = xre_ref[pl.ds(base - 1, 1), :]
        ci = xim_ref[pl.ds(base - 1, 1), :]
        pr = sc_ref[6]
        pi = sc_ref[7]
        xre_ref[pl.ds(base, SUBLANES), :] = xr + pr * cr - pi * ci
        xim_ref[pl.ds(base, SUBLANES), :] = xi + pr * ci + pi * cr
        return 0

    lax.fori_loop(0, t // SUBLANES, scan, 0, unroll=2)
    xre_ref[0:SUBLANES, :] = xre_ref[t:t + SUBLANES, :]
    xim_ref[0:SUBLANES, :] = xim_ref[t:t + SUBLANES, :]

    ys = []
    for q in range(nq):
        xr = xre_ref[SUBLANES:, q * half:(q + 1) * half].astype(BF16)
        xi = xim_ref[SUBLANES:, q * half:(q + 1) * half].astype(BF16)
        ys.append(_mm(xr, cre_ref[q]) + _mm(xi, cim_ref[q]))
    y = jnp.concatenate(ys, axis=1) + d_ref[...] * u
    y = jax.nn.gelu(y).astype(BF16)
    hh = _mm(y, gw_ref[...]) + gb_ref[...]
    w = hh.shape[1] // 2
    o_ref[0] = (hh[:, :w] * jax.nn.sigmoid(hh[:, w:]) * sgb_ref[0].astype(F32)).astype(BF16)


def _ssm(u, sgb, bmat, cre, cim, sc, dskip, gw, gb):
    b, l, w = u.shape
    t = ROW_TILE
    ns = sc.shape[-1]

    def row(n):
        return pl.BlockSpec((1, t, n), lambda bb, i: (bb, i, 0))

    return pl.pallas_call(
        _ssm_kernel,
        grid=(b, l // t),
        in_specs=[row(w), row(w), _const_spec(bmat.shape), _const_spec(cre.shape), _const_spec(cim.shape),
                  _const_spec(sc.shape), _const_spec(dskip.shape), _const_spec(gw.shape), _const_spec(gb.shape)],
        out_specs=row(w),
        out_shape=jax.ShapeDtypeStruct((b, l, w), BF16),
        scratch_shapes=[pltpu.VMEM((SUBLANES + t, ns), F32), pltpu.VMEM((SUBLANES + t, ns), F32)],
        compiler_params=_params("arbitrary", "arbitrary"),
        name="ssm",
    )(u, sgb, bmat, cre, cim, sc, dskip, gw, gb)


def _s5_prep(log_dt, a_re, a_im, b_re, b_im, c_re, c_im):
    g, p = a_re.shape
    h = b_re.shape[-1]
    gl = LANES // h
    nq = g // gl
    dt = jnp.exp(log_dt)[:, None]
    mag = jnp.exp(a_re * dt)
    ang = a_im * dt
    ab_re = mag * jnp.cos(ang)
    ab_im = mag * jnp.sin(ang)
    den = a_re * a_re + a_im * a_im
    n_re = ab_re - 1.0
    n_im = ab_im
    f_re = (n_re * a_re + n_im * a_im) / den
    f_im = (n_im * a_re - n_re * a_im) / den
    bb_re = f_re[..., None] * b_re - f_im[..., None] * b_im
    bb_im = f_re[..., None] * b_im + f_im[..., None] * b_re
    eye = jnp.eye(gl, dtype=F32)

    def bdiag_in(m):
        m = m.reshape(nq, gl, p, h)
        return jnp.einsum('qgph,gk->qghkp', m, eye).reshape(nq, gl * h, gl * p)

    def bdiag_out(m):
        m = m.reshape(nq, gl, h, p)
        return jnp.einsum('qghp,gk->qgpkh', m, eye).reshape(nq, gl * p, gl * h)

    bmat = jnp.concatenate([bdiag_in(bb_re), bdiag_in(bb_im)], axis=2).astype(BF16)
    cre = bdiag_out(c_re).astype(BF16)
    cim = bdiag_out(-c_im).astype(BF16)

    pw = [(ab_re.reshape(-1), ab_im.reshape(-1))]
    for _ in range(SUBLANES - 1):
        pr, pi = pw[-1]
        pw.append((pr * pw[0][0] - pi * pw[0][1], pr * pw[0][1] + pi * pw[0][0]))
    rows = jnp.arange(SUBLANES)[:, None]
    sc = []
    for k in (1, 2, 4):
        sc.append(jnp.where(rows >= k, pw[k - 1][0][None, :], 0.0))
        sc.append(jnp.where(rows >= k, pw[k - 1][1][None, :], 0.0))
    sc.append(jnp.stack([pw[r][0] for r in range(SUBLANES)]))
    sc.append(jnp.stack([pw[r][1] for r in range(SUBLANES)]))
    return bmat, cre, cim, jnp.stack(sc).astype(F32)


def _mid_kernel(x_ref, a_ref, s_ref, wo_ref, g_ref, wq_ref, wk_ref, wvt_ref, wg_ref, wqi_ref, wki_ref, wwt_ref,
                qg_ref, kg_ref, h_ref, q_ref, k_ref, vt_ref, sg_ref, qi_ref, ki_ref, wt_ref):
    aw = a_ref.shape[2]
    h = x_ref[0] + _mm(a_ref[0], wo_ref[0:aw, :]) + _mm(s_ref[0], wo_ref[aw:, :])
    h_ref[0] = h
    hn = _rms(h, g_ref[...]).astype(BF16)
    qf = _mm(hn, wq_ref[...])
    for hd in range(C_HEADS):
        sl = slice(hd * C_HEAD_DIM, (hd + 1) * C_HEAD_DIM)
        q_ref[0, :, sl] = (_rms(qf[:, sl], qg_ref[...]) * (C_HEAD_DIM ** -0.5)).astype(BF16)
    kf = _mm(hn, wk_ref[...])
    for hd in range(C_KV_HEADS):
        sl = slice(hd * C_HEAD_DIM, (hd + 1) * C_HEAD_DIM)
        k_ref[0, :, sl] = _rms(kf[:, sl], kg_ref[...]).astype(BF16)
    vt_ref[0] = _mm_nt(wvt_ref[...], hn).astype(BF16)
    sg_ref[0] = _silu(_mm(hn, wg_ref[...])).astype(BF16)
    qi_ref[0] = _mm(hn, wqi_ref[...]).astype(BF16)
    ki_ref[0] = _mm(hn, wki_ref[...]).astype(BF16)
    wt_ref[0] = _mm_nt(wwt_ref[...], hn) * ((IDX_HEADS ** -0.5) * (IDX_DIM ** -0.5))


def _mid(x, att0, ssm0, wo, g, wq, wk, wvt, wg, wqi, wki2, wwt, qg, kg):
    b, l, d = x.shape
    t = ROW_TILE

    def row(n):
        return pl.BlockSpec((1, t, n), lambda bb, i: (bb, i, 0))

    def col(n):
        return pl.BlockSpec((1, n, t), lambda bb, i: (bb, 0, i))

    weights = [wo, g, wq, wk, wvt, wg, wqi, wki2, wwt, qg, kg]
    cw = C_HEADS * C_HEAD_DIM
    ckv = C_KV_HEADS * C_HEAD_DIM
    return pl.pallas_call(
        _mid_kernel,
        grid=(b, l // t),
        in_specs=[row(d), row(att0.shape[2]), row(ssm0.shape[2])] + [_const_spec(w.shape) for w in weights],
        out_specs=[row(d), row(cw), row(ckv), col(ckv), row(cw), row(IDX_HEADS * IDX_DIM), row(2 * LANES),
                   col(IDX_HEADS)],
        out_shape=[jax.ShapeDtypeStruct((b, l, d), F32), jax.ShapeDtypeStruct((b, l, cw), BF16),
                   jax.ShapeDtypeStruct((b, l, ckv), BF16), jax.ShapeDtypeStruct((b, ckv, l), BF16),
                   jax.ShapeDtypeStruct((b, l, cw), BF16), jax.ShapeDtypeStruct((b, l, IDX_HEADS * IDX_DIM), BF16),
                   jax.ShapeDtypeStruct((b, l, 2 * LANES), BF16), jax.ShapeDtypeStruct((b, IDX_HEADS, l), F32)],
        compiler_params=_params("arbitrary", "arbitrary"),
        name="mid",
    )(x, att0, ssm0, *weights)


def _dsa_kernel(q_ref, qi_ref, wt_ref, sg_ref, k_ref, vt_ref, ki_ref, tab_ref, lb_ref, o_ref,
                sc_ref, best_ref, x_ref, acc_ref, *, seq_len, topk):
    i = pl.program_id(1)
    ck = KEY_CHUNK
    per = ck // BLOCK
    nch = (i + per) // per
    t_row = i * BLOCK + lax.broadcasted_iota(I32, (1, LANES), 1)
    kiota = lax.broadcasted_iota(I32, (ck, LANES), 0)

    def chunk_off(c):
        return pl.multiple_of(c * ck, ck)

    qi = qi_ref[0]
    qi_stack = [jnp.concatenate([qi[:, (2 * s) * LANES:(2 * s + 1) * LANES],
                                 qi[:, (2 * s + 1) * LANES:(2 * s + 2) * LANES]], axis=0) for s in range(2)]
    wt = wt_ref[0]

    def score_chunk(c, masked):
        off = chunk_off(c)
        sc = jnp.zeros((ck, LANES), F32)
        for a in range(2):
            kk = ki_ref[0, pl.ds(off, ck), a * LANES:(a + 1) * LANES]
            for s in range(2):
                r = _mm_nt(kk, qi_stack[s])
                for j in range(2):
                    hd = 2 * (2 * s + j) + a
                    sc = sc + jnp.maximum(r[:, j * LANES:(j + 1) * LANES], 0.0) * wt[hd:hd + 1, :]
        if masked:
            sc = jnp.where(off + kiota <= t_row, sc, NEG_INF)
        sc_ref[pl.ds(off, ck), :] = sc
        return _fold(sc, jnp.maximum)

    def exchange(v, a, b):
        v[a], v[b] = jnp.maximum(v[a], v[b]), jnp.minimum(v[a], v[b])

    def sort_desc(v):
        n, k = len(v), 2
        while k <= n:
            j = k // 2
            while j >= 1:
                for a in range(n):
                    b = a ^ j
                    if b > a:
                        exchange(v, *((a, b) if (a & k) == 0 else (b, a)))
                j //= 2
            k *= 2

    def merge_top(best, blk):
        n = len(best)
        v = [jnp.maximum(best[r], blk[n - 1 - r]) for r in range(n)]
        j = n // 2
        while j >= 1:
            for a in range(n):
                if a ^ j > a:
                    exchange(v, a, a ^ j)
            j //= 2
        return v

    crow = STREAMS * CAND * SUBLANES

    def cand_step(row0):
        blk_all = sc_ref[pl.ds(pl.multiple_of(row0, crow), crow), :]
        for st in range(STREAMS):
            blk = [blk_all[(STREAMS * r + st) * SUBLANES:(STREAMS * r + st + 1) * SUBLANES] for r in range(CAND)]
            sort_desc(blk)
            base = st * CAND * SUBLANES
            best = [best_ref[base + r * SUBLANES:base + (r + 1) * SUBLANES, :] for r in range(CAND)]
            for r, x in enumerate(merge_top(best, blk)):
                best_ref[base + r * SUBLANES:base + (r + 1) * SUBLANES, :] = x

    def score_body(c, mx):
        return jnp.maximum(jnp.maximum(mx, score_chunk(2 * c, False)), score_chunk(2 * c + 1, False))

    below = nch - 1
    smax = lax.fori_loop(0, below // 2, score_body, jnp.full((SUBLANES, LANES), NEG_INF, F32))
    smax = lax.cond(below % 2 == 1, lambda: jnp.maximum(smax, score_chunk(below - 1, False)), lambda: smax)
    smax = jnp.max(jnp.maximum(smax, score_chunk(nch - 1, True)), axis=0, keepdims=True)

    def count(*preds):
        rows = COUNT_ROWS
        sub = ck // rows

        def body(c, accs):
            out = []
            for u in range(sub):
                off = pl.multiple_of(c * ck + u * rows, rows)
                s = sc_ref[pl.ds(off, rows), :]
                for n, pred in enumerate(preds):
                    ind = pred(s, off).astype(I32)
                    out.append(accs[u * len(preds) + n]
                               + jnp.sum(ind.reshape(rows // SUBLANES, SUBLANES, LANES), axis=0))
            return tuple(out)

        accs = lax.fori_loop(0, nch, body, tuple(jnp.zeros((SUBLANES, LANES), I32) for _ in range(sub * len(preds))))
        res = [jnp.sum(sum(accs[n::len(preds)]), axis=0, keepdims=True) for n in range(len(preds))]
        return res[0] if len(preds) == 1 else res

    def key_value(k):
        return pltpu.bitcast(k ^ ((k >> 31) & MAGNITUDE_BITS), F32)

    def count_ge(k):
        thr = key_value(k)
        return count(lambda s, off: s >= thr)

    def full(v):
        return jnp.full((1, LANES), v, I32)

    def bisect(_, st):
        lo, hi, c_lo, c_hi = st
        mid = (lo >> 1) + (hi >> 1) + (lo & hi & 1)
        c = count_ge(mid)
        ge = c >= topk
        return jnp.where(ge, mid, lo), jnp.where(ge, hi, mid), jnp.where(ge, c, c_lo), jnp.where(ge, c_hi, c)

    searching = (i + 1) * BLOCK > topk

    def float_key(x):
        bits = pltpu.bitcast(x, I32)
        return bits ^ ((bits >> 31) & MAGNITUDE_BITS)

    def search():
        k_lo = float_key(smax * 0.125)
        c = count_ge(k_lo)
        ok = (smax > 0.0) & (c >= topk)
        trips = jnp.where(jnp.min(jnp.where(ok, 1, 0)) > 0, 25, 32)
        st = (jnp.where(ok, k_lo, KEY_NEG_INF), float_key(smax) + 1, jnp.where(ok, c, nch * ck), full(0))
        out = lax.fori_loop(0, trips, bisect, st)
        return out[0], out[2], out[3]

    def cand_search():
        best_ref[...] = jnp.full(best_ref.shape, -jnp.inf, F32)

        def cand_body(c, _):
            cand_step(c * crow)
            return 0

        lax.fori_loop(0, (i + crow // BLOCK) // (crow // BLOCK), cand_body, 0)

        def all_sublanes(x):
            for shift in (4, 2, 1):
                x = x + pltpu.roll(x, shift, axis=0)
            return x

        def count_cand(k):
            thr = key_value(k)[None]
            parts = [jnp.sum((best_ref[r:r + COUNT_ROWS, :].reshape(COUNT_ROWS // SUBLANES, SUBLANES, LANES)
                              >= thr).astype(I32), axis=0) for r in range(0, crow, COUNT_ROWS)]
            return all_sublanes(sum(parts))

        def step(_, st):
            lo, hi = st
            mid = (lo >> 1) + (hi >> 1) + (lo & hi & 1)
            take = ~((count_cand(mid) - topk) >> 31)
            return (mid & take) | (lo & ~take), (hi & take) | (mid & ~take)

        smax8 = jnp.broadcast_to(smax, (SUBLANES, LANES))
        k_lo = float_key(smax8 * 0.125)
        ok = (smax8 > 0.0) & (count_cand(k_lo) >= topk)
        trips = jnp.where(jnp.min(jnp.where(ok, 1, 0)) > 0, 25, 32)
        vk8, _ = lax.fori_loop(0, trips, step, (jnp.where(ok, k_lo, KEY_NEG_INF), float_key(smax8) + 1))
        vk = vk8[0:1]
        thr = key_value(vk)
        above_cand = jnp.sum((best_ref[...] > thr).astype(I32), axis=0, keepdims=True)
        c_ge, c_gt = count(lambda s, off: s >= thr, lambda s, off: s > thr)
        complete = jnp.min(jnp.where(c_gt == above_cand, 1, 0)) > 0
        return lax.cond(complete, lambda: (vk, c_ge, c_gt), search)

    vkey, c_lo, c_hi = lax.cond(searching, cand_search, lambda: (full(KEY_NEG_INF), full(topk), full(0)))
    vthr = key_value(vkey)
    need = topk - c_hi
    ties = c_lo - c_hi

    def tie_search():
        def split_step():
            nxt = vkey + 1
            nxt = jnp.where((nxt > 0) & (nxt < KEY_MIN_NORMAL), KEY_MIN_NORMAL, nxt)
            step = key_value(nxt) - vthr

            def split(_, st):
                fl, fh = st
                fm = 0.5 * (fl + fh)
                t = vthr + fm * step
                ge = count(lambda s, off: s >= t) >= topk
                return jnp.where(ge, fm, fl), jnp.where(ge, fh, fm)

            fl, _ = lax.fori_loop(0, 26, split, (jnp.zeros((1, LANES), F32), jnp.ones((1, LANES), F32)))
            t = vthr + fl * step
            return t, topk - count(lambda s, off: s > t)

        inside = jnp.max(ties - count(lambda s, off: s == vthr)) > 0
        thr, want = lax.cond(inside, split_step, lambda: (vthr, need))

        rr = lax.broadcasted_iota(I32, (BLOCK, BLOCK), 0)
        cc = lax.broadcasted_iota(I32, (BLOCK, BLOCK), 1)
        tril = jnp.where(cc <= rr, 1.0, 0.0).astype(BF16)
        want_f = want.astype(F32)

        def body(c, before):
            off = chunk_off(c)
            blocks = [sc_ref[pl.ds(pl.multiple_of(off + r * BLOCK, BLOCK), BLOCK), :] for r in range(per)]
            hits = [s == thr for s in blocks]
            ranks = [_mm(tril, jnp.where(h, 1.0, 0.0).astype(BF16)) for h in hits]
            for r in range(per):
                rank = ranks[r] + before
                sc_ref[pl.ds(pl.multiple_of(off + r * BLOCK, BLOCK), BLOCK), :] = jnp.where(
                    hits[r] & (rank > want_f), NEG_INF, blocks[r])
                before = rank[BLOCK - 1:BLOCK, :]
            return before

        lax.fori_loop(0, nch, body, jnp.zeros((1, LANES), F32))
        return thr

    any_tie = searching & (jnp.max(ties - need) > 0)
    vthr = lax.cond(any_tie, tie_search, lambda: vthr)

    def selection_mask(off):
        s = sc_ref[pl.ds(off, ck), :]
        s_idx = off + kiota
        sel = (s >= vthr) & (s_idx <= t_row)
        madd = jnp.where(sel, 0.0, NEG_INF)
        sc_ref[pl.ds(off, ck), :] = madd
        return madd

    q = q_ref[0]
    n_far = jnp.maximum((i - NEAR_BLOCKS + 1) // per, 0)
    hpg = C_HEADS // C_KV_HEADS
    npair = C_HEADS // 2
    q_pairs = [jnp.concatenate([q[:, (2 * j) * LANES:(2 * j + 1) * LANES],
                                q[:, (2 * j + 1) * LANES:(2 * j + 2) * LANES]], axis=0) for j in range(npair)]

    def bias_rows(hd, c):
        return jnp.concatenate([tab_ref[hd, jnp.clip(i - (c * per + r), 0, NEAR_BLOCKS)] for r in range(per)],
                               axis=0)

    def emit(hd, num, den):
        sl = slice(hd * LANES, (hd + 1) * LANES)
        o_ref[0, :, sl] = ((num / den).T * sg_ref[0, :, sl].astype(F32)).astype(BF16)

    def exact_attention():
        for g in range(C_KV_HEADS):
            def stage_body(near, g=g):
                def body(c, mx):
                    off = chunk_off(c)
                    madd = sc_ref[pl.ds(off, ck), :]
                    kc = k_ref[0, pl.ds(off, ck), g * LANES:(g + 1) * LANES]
                    out = []
                    for jj in range(hpg // 2):
                        lg = _mm_nt(kc, q_pairs[g * (hpg // 2) + jj])
                        for a in range(2):
                            hl = 2 * jj + a
                            x = lg[:, a * LANES:(a + 1) * LANES] + madd
                            if near:
                                x = x + bias_rows(hpg * g + hl, c)
                            x_ref[hl, pl.ds(off, ck), :] = x
                            out.append(jnp.maximum(mx[hl], _fold(x, jnp.maximum)))
                    return tuple(out)

                return body

            mx = tuple(jnp.full((SUBLANES, LANES), NEG_INF, F32) for _ in range(hpg))
            mx = lax.fori_loop(0, n_far, stage_body(False), mx)
            mx = lax.fori_loop(n_far, nch, stage_body(True), mx)
            m = [jnp.max(v, axis=0, keepdims=True) for v in mx]
            acc_ref[...] = jnp.zeros(acc_ref.shape, F32)

            def att_body(c, ls, g=g, m=m):
                off = chunk_off(c)
                vt = vt_ref[0, g * LANES:(g + 1) * LANES, pl.ds(off, ck)]
                out = []
                for jj in range(hpg // 2):
                    ps = []
                    for a in range(2):
                        hl = 2 * jj + a
                        p = jnp.exp(x_ref[hl, pl.ds(off, ck), :] - m[hl])
                        out.append(ls[hl] + _fold(p, jnp.add))
                        ps.append(p.astype(BF16))
                    acc_ref[jj] += _mm(vt, jnp.concatenate(ps, axis=1))
                return tuple(out)

            ls = lax.fori_loop(0, nch, att_body, tuple(jnp.zeros((SUBLANES, LANES), F32) for _ in range(hpg)))
            for hl in range(hpg):
                emit(hpg * g + hl, acc_ref[hl // 2, :, (hl % 2) * LANES:(hl % 2 + 1) * LANES],
                     jnp.sum(ls[hl], axis=0, keepdims=True))

    lb = lb_ref[...]
    acc_ref[...] = jnp.zeros(acc_ref.shape, F32)

    def stage(c, g, near):
        off = chunk_off(c)
        mb = (selection_mask(off) if g == 0 else sc_ref[pl.ds(off, ck), :]) - lb
        kc = k_ref[0, pl.ds(off, ck), g * LANES:(g + 1) * LANES]
        for jj in range(hpg // 2):
            lg = _mm_nt(kc, q_pairs[g * (hpg // 2) + jj])
            for a in range(2):
                hl = 2 * jj + a
                x = lg[:, a * LANES:(a + 1) * LANES] + mb
                if near:
                    x = x + bias_rows(hpg * g + hl, c)
                x_ref[hl, pl.ds(off, ck), :] = x

    def consume(c, g, ls):
        off = chunk_off(c)
        vt = vt_ref[0, g * LANES:(g + 1) * LANES, pl.ds(off, ck)]
        out = []
        for jj in range(hpg // 2):
            ps = []
            for a in range(2):
                hl = 2 * jj + a
                p = jnp.exp(x_ref[hl, pl.ds(off, ck), :])
                out.append(ls[hl] + _fold(p, jnp.add))
                ps.append(p.astype(BF16))
            acc_ref[g * (hpg // 2) + jj] += _mm(vt, jnp.concatenate(ps, axis=1))
        return tuple(out)

    dens = []
    for g in range(C_KV_HEADS):
        def step(near, g=g):
            def body(c, ls):
                out = consume(c, g, ls)
                stage(c + 1, g, near)
                return out

            return body

        stage(0, g, True)
        ls = tuple(jnp.zeros((SUBLANES, LANES), F32) for _ in range(hpg))
        split = jnp.maximum(n_far - 1, 0)
        ls = lax.fori_loop(0, split, step(False), ls)
        ls = lax.fori_loop(split, nch - 1, step(True), ls)
        ls = consume(nch - 1, g, ls)
        dens += [jnp.sum(v, axis=0, keepdims=True) for v in ls]
    in_range = jnp.min(functools.reduce(jnp.minimum, dens)) > SUM_FLOOR

    @pl.when(in_range)
    def _():
        for hd in range(C_HEADS):
            emit(hd, acc_ref[hd // 2, :, (hd % 2) * LANES:(hd % 2 + 1) * LANES], dens[hd])

    @pl.when(jnp.logical_not(in_range))
    def _():
        exact_attention()


def _dsa(q, qi, wt, sg, k, vt, ki2, tab, lb):
    b, l, cw = q.shape
    nb = l // BLOCK
    topk = min(TOPK_MAX, l // 4)

    def blk(n):
        return pl.BlockSpec((1, BLOCK, n), lambda bb, i: (bb, i, 0))

    def whole(s1, s2):
        return pl.BlockSpec((1, s1, s2), lambda bb, i: (bb, 0, 0), pipeline_mode=pl.Buffered(1))

    hpg = C_HEADS // C_KV_HEADS
    return pl.pallas_call(
        functools.partial(_dsa_kernel, seq_len=l, topk=topk),
        grid=(b, nb),
        in_specs=[blk(cw), blk(qi.shape[2]), pl.BlockSpec((1, IDX_HEADS, BLOCK), lambda bb, i: (bb, 0, i)), blk(cw),
                  whole(l, k.shape[2]), whole(vt.shape[1], l), whole(l, ki2.shape[2]),
                  pl.BlockSpec(tab.shape, lambda bb, i: (0, 0, 0, 0), pipeline_mode=pl.Buffered(1)),
                  _const_spec(lb.shape)],
        out_specs=blk(cw),
        out_shape=jax.ShapeDtypeStruct((b, l, cw), BF16),
        scratch_shapes=[pltpu.VMEM((l, LANES), F32), pltpu.VMEM((STREAMS * CAND * SUBLANES, LANES), F32),
                        pltpu.VMEM((hpg, l, LANES), F32),
                        pltpu.VMEM((C_HEADS // 2, C_HEAD_DIM, 2 * LANES), F32)],
        compiler_params=_params("arbitrary", "arbitrary"),
        name="dsa",
    )(q, qi, wt, sg, k, vt, ki2, tab, lb)


def _out_kernel(h_ref, a_ref, w_ref, o_ref):
    o_ref[...] = h_ref[...] + _mm(a_ref[...], w_ref[...])


def _outproj(h2, a2, w):
    rows, d = h2.shape
    t = ROW_TILE
    return pl.pallas_call(
        _out_kernel,
        grid=(rows // t,),
        in_specs=[pl.BlockSpec((t, d), lambda i: (i, 0)), pl.BlockSpec((t, a2.shape[1]), lambda i: (i, 0)),
                  _const_spec(w.shape)],
        out_specs=pl.BlockSpec((t, d), lambda i: (i, 0)),
        out_shape=jax.ShapeDtypeStruct((rows, d), F32),
        compiler_params=_params("arbitrary"),
        name="outproj1",
    )(h2, a2, w)


def _bias_tables(rel_bias, seq_len):
    del seq_len
    nv = (NEAR_BLOCKS + 1) * BLOCK
    vec = rel_bias[_t5_bucket(jnp.arange(nv, dtype=I32))].astype(F32).T

    nh = vec.shape[0]
    vecp = jnp.concatenate([jnp.broadcast_to(vec[:, :1], (nh, BLOCK - 1)), vec], axis=1)

    def toeplitz(g, rows, cols):
        w = rows + cols
        g2 = jnp.concatenate([g[..., rows - 1:rows - 1 + cols], g[..., :1], g[..., :rows - 1]], axis=-1)
        flat = jnp.tile(g2, (1,) * (g.ndim - 1) + (rows,))[..., :rows * (w - 1)]
        return flat.reshape(g.shape[:-1] + (rows, w - 1))[..., :cols]

    bias0 = jnp.transpose(toeplitz(vecp[:, :3 * BLOCK - 1], 2 * BLOCK, BLOCK), (0, 2, 1))
    r = vecp[:, :(NEAR_BLOCKS + 1) * BLOCK].reshape(nh, NEAR_BLOCKS + 1, BLOCK)
    wins = jnp.concatenate([r[:, :-1], r[:, 1:, :BLOCK - 1]], axis=2)
    tab = toeplitz(wins, BLOCK, BLOCK) - rel_bias[NUM_BUCKETS - 1].astype(F32)[:, None, None, None]
    tab = jnp.concatenate([tab, jnp.zeros((tab.shape[0], 1, BLOCK, BLOCK), F32)], axis=1)
    return bias0, tab


def kernel(x, rel_bias, norm_g, ev_w_in, ev_w_out, ev_q_norm_g, ev_k_norm_g, ev_sinks, ev_ssm_log_dt, ev_ssm_a_re,
           ev_ssm_a_im, ev_ssm_b_re, ev_ssm_b_im, ev_ssm_c_re, ev_ssm_c_im, ev_ssm_d, ev_glu_w, ev_glu_b, od_w_in,
           od_w_out, od_q_norm_g, od_k_norm_g):
    b, l, d = x.shape
    assert l % KEY_CHUNK == 0 and l % ROW_TILE == 0
    assert (NEAR_BLOCKS - 1) * BLOCK + 1 >= 16 * 64 ** (15 / 16) + 1
    bias0, tab = _bias_tables(rel_bias, l)

    w0 = ev_w_in[0]
    hd = A_HEAD_DIM
    o0 = np.cumsum([0, A_WIDTH, A_KV_HEADS * hd, A_KV_HEADS * hd, A_WIDTH, A_WIDTH, A_WIDTH])
    wq0, wk0, wv0, wga, wu, wgb = (w0[:, o0[n]:o0[n + 1]] for n in range(6))
    z = jnp.zeros((d, hd), w0.dtype)

    def variants(w):
        return jnp.concatenate([c for g in range(A_KV_HEADS) for c in (w[:, g * hd:(g + 1) * hd], z, z,
                                                                       w[:, g * hd:(g + 1) * hd])], axis=1)

    w0x = jnp.concatenate([wq0, variants(wk0), variants(wv0), wga, wu, wgb], axis=1).astype(BF16)
    qg2 = jnp.tile(ev_q_norm_g[0], 2)[None, :]
    kg2 = jnp.tile(ev_k_norm_g[0], 2)[None, :]
    q0, k0, v0, sga, u, sgb = _proj0(x.reshape(b * l, d), norm_g[0][None, :], w0x, qg2, kg2)
    shp = lambda a: a.reshape(b, l, a.shape[-1])
    sinks = jnp.broadcast_to(ev_sinks[0][:, None], (A_HEADS, LANES)).astype(F32)
    att0 = _attn0(shp(q0), shp(k0), shp(v0), shp(sga), bias0, sinks)
    bmat, cre, cim, sc = _s5_prep(ev_ssm_log_dt[0], ev_ssm_a_re[0], ev_ssm_a_im[0], ev_ssm_b_re[0], ev_ssm_b_im[0],
                                  ev_ssm_c_re[0], ev_ssm_c_im[0])
    ssm0 = _ssm(shp(u), shp(sgb), bmat, cre, cim, sc, ev_ssm_d[0].reshape(1, -1), ev_glu_w[0].astype(BF16),
                ev_glu_b[0][None, :])

    w1 = od_w_in[0]
    cw = C_HEADS * C_HEAD_DIM
    ckv = C_KV_HEADS * C_HEAD_DIM
    o = np.cumsum([0, cw, ckv, ckv, cw, IDX_HEADS * IDX_DIM, IDX_DIM, IDX_HEADS])
    wq, wk, wv, wg, wqi, wki, ww = (w1[:, o[n]:o[n + 1]] for n in range(7))
    zki = jnp.zeros((d, LANES - IDX_DIM), w1.dtype)
    wki2 = jnp.concatenate([wki, zki, zki, wki], axis=1)
    bf = lambda a: a.astype(BF16)
    h1, q1, k1, vt1, sg1, qi1, ki2, wt1 = _mid(
        x, att0, ssm0, bf(ev_w_out[0]), norm_g[1][None, :], bf(wq), bf(wk), bf(wv.T), bf(wg), bf(wqi), bf(wki2),
        bf(ww.T), od_q_norm_g[0][None, :], od_k_norm_g[0][None, :])
    lb = (1.02 * C_HEAD_DIM ** 0.5 * jnp.max(jnp.abs(od_q_norm_g[0])) * jnp.max(jnp.abs(od_k_norm_g[0]))
          + jnp.max(tab))
    att1 = _dsa(q1, qi1, wt1, sg1, k1, vt1, ki2, tab, jnp.full((1, LANES), lb, F32))
    out = _outproj(h1.reshape(b * l, d), att1.reshape(b * l, cw), bf(od_w_out[0]))
    return out.reshape(b, l, d)
```

```python
import functools
import math

import jax
import jax.numpy as jnp
import numpy as np
from jax import lax
from jax.experimental import pallas as pl
from jax.experimental.pallas import tpu as pltpu

F32 = jnp.float32
BF16 = jnp.bfloat16
I32 = jnp.int32

LANES = 128
SUBLANES = 8
VMEM_LIMIT = 56 * 1024 * 1024

BLOCK = 128
WINDOW = 128
A_HEADS = 8
A_HEAD_DIM = 64
A_KV_HEADS = 2
A_WIDTH = A_HEADS * A_HEAD_DIM
SSM_GROUP = 16
SSM_STATE = 64
C_HEADS = 8
C_HEAD_DIM = 128
C_KV_HEADS = 2
IDX_HEADS = 8
IDX_DIM = 64
TOPK_MAX = 256
NUM_BUCKETS = 32
REL_MAX_DIST = 1024
EPS = 1e-6
NEG_INF = -1e30
INT_MIN = -(2 ** 31)
KEY_MIN_NORMAL = 0x00800000
KEY_POS_INF = 0x7F800000
KEY_NEG_INF = INT_MIN + 0x007FFFFF
MAGNITUDE_BITS = 0x7FFFFFFF

ROW_TILE = 1024
KEY_CHUNK = 1024
NEAR_BLOCKS = 8
FOLD_CHAINS = 8
COUNT_ROWS = 512
SUM_FLOOR = 1e-30
CAND = 32
STREAMS = 2
NT_DIMS = (((1,), (1,)), ((), ()))


def _t5_bucket(dist):
    n = jnp.maximum(dist, 0)
    max_exact = NUM_BUCKETS // 2
    nf = jnp.maximum(n, 1).astype(F32)
    large = max_exact + (jnp.log(nf / max_exact) / math.log(REL_MAX_DIST / max_exact)
                         * (NUM_BUCKETS - max_exact)).astype(I32)
    large = jnp.minimum(large, NUM_BUCKETS - 1)
    return jnp.where(n < max_exact, n, large)


def _silu(x):
    return x * jax.nn.sigmoid(x)


def _rms(x, g):
    ms = jnp.mean(x * x, axis=-1, keepdims=True)
    return x * lax.rsqrt(ms + EPS) * g


def _mm(a, b):
    return jnp.dot(a, b, preferred_element_type=F32)


def _mm_nt(a, b):
    return lax.dot_general(a, b, NT_DIMS, preferred_element_type=F32)


def _fold(x, op):
    n = x.shape[0] // SUBLANES
    chains = min(FOLD_CHAINS, n)
    accs = [x[r * SUBLANES:(r + 1) * SUBLANES] for r in range(chains)]
    for r in range(chains, n):
        accs[r % chains] = op(accs[r % chains], x[r * SUBLANES:(r + 1) * SUBLANES])
    while len(accs) > 1:
        accs = [op(a, b) for a, b in zip(accs[::2], accs[1::2])] + accs[len(accs) & ~1:]
    return accs[0]


def _params(*sem):
    return pltpu.CompilerParams(dimension_semantics=sem, vmem_limit_bytes=VMEM_LIMIT)


def _const_spec(shape):
    zeros = (0,) * len(shape)
    return pl.BlockSpec(shape, lambda *_: zeros)


def _proj0_kernel(x_ref, g_ref, w_ref, qg_ref, kg_ref, q_ref, k_ref, v_ref, sga_ref, u_ref, sgb_ref):
    hn = _rms(x_ref[...], g_ref[...]).astype(BF16)
    lo = lax.broadcasted_iota(I32, (1, LANES), 1) < A_HEAD_DIM

    def mm(n):
        return _mm(hn, w_ref[:, n * A_WIDTH:(n + 1) * A_WIDTH])

    def segnorm(x, g2):
        sq = x * x
        s_lo = jnp.sum(jnp.where(lo, sq, 0.0), axis=-1, keepdims=True)
        s_hi = jnp.sum(jnp.where(lo, 0.0, sq), axis=-1, keepdims=True)
        inv = jnp.where(lo, lax.rsqrt(s_lo / A_HEAD_DIM + EPS), lax.rsqrt(s_hi / A_HEAD_DIM + EPS))
        return x * inv * g2

    q, k = mm(0), mm(1)
    for p in range(A_WIDTH // LANES):
        sl = slice(p * LANES, (p + 1) * LANES)
        q_ref[:, sl] = (segnorm(q[:, sl], qg_ref[...]) * (A_HEAD_DIM ** -0.5)).astype(BF16)
        k_ref[:, sl] = segnorm(k[:, sl], kg_ref[...]).astype(BF16)
    v_ref[...] = mm(2).astype(BF16)
    sga_ref[...] = _silu(mm(3)).astype(BF16)
    u_ref[...] = mm(4)
    sgb_ref[...] = _silu(mm(5)).astype(BF16)


def _proj0(x2, g, w, qg2, kg2):
    rows, d = x2.shape
    t = ROW_TILE

    def row(n):
        return pl.BlockSpec((t, n), lambda i: (i, 0))

    n = A_WIDTH
    return pl.pallas_call(
        _proj0_kernel,
        grid=(rows // t,),
        in_specs=[row(d), _const_spec((1, d)), _const_spec(w.shape), _const_spec(qg2.shape), _const_spec(kg2.shape)],
        out_specs=[row(n)] * 6,
        out_shape=[jax.ShapeDtypeStruct((rows, n), BF16), jax.ShapeDtypeStruct((rows, n), BF16),
                   jax.ShapeDtypeStruct((rows, n), BF16), jax.ShapeDtypeStruct((rows, n), BF16),
                   jax.ShapeDtypeStruct((rows, n), F32), jax.ShapeDtypeStruct((rows, n), BF16)],
        compiler_params=_params("arbitrary"),
        name="proj0",
    )(x2, g, w, qg2, kg2)


def _attn0_kernel(q_ref, kc_ref, kp_ref, vc_ref, vp_ref, sga_ref, bias_ref, sink_ref, ones_ref, o_ref):
    i = pl.program_id(1)
    kb = jnp.concatenate([kp_ref[0], kc_ref[0]], axis=0)
    vb = jnp.concatenate([vp_ref[0], vc_ref[0]], axis=0)

    def variant(x, g, a):
        n = 2 * g + a
        return x[:, n * LANES:(n + 1) * LANES]

    row = lax.broadcasted_iota(I32, (BLOCK, 2 * BLOCK), 0)
    col = lax.broadcasted_iota(I32, (BLOCK, 2 * BLOCK), 1)
    d = row + BLOCK - col
    mask = (d >= 0) & (d < WINDOW) & ((i > 0) | (col >= BLOCK))

    lgs, sinks = [], []
    for p in range(A_HEADS // 2):
        qp = q_ref[0, :, p * LANES:(p + 1) * LANES]
        for a in range(2):
            h = 2 * p + a
            lgs.append(jnp.where(mask, _mm_nt(qp, variant(kb, p // 2, a)) + bias_ref[h], NEG_INF))
            sinks.append(jnp.broadcast_to(sink_ref[h:h + 1, 0:1], (BLOCK, 1)))
    lg = jnp.concatenate(lgs, axis=0)
    sink = jnp.concatenate(sinks, axis=0)
    m = jnp.maximum(jnp.max(lg, axis=-1, keepdims=True), sink)
    e = jnp.exp(lg - m).astype(BF16)
    inv = 1.0 / (_mm(e, ones_ref[...]) + jnp.exp(sink - m))
    for p in range(A_HEADS // 2):
        sl = slice(p * LANES, (p + 1) * LANES)
        acc = jnp.zeros((BLOCK, LANES), F32)
        for a in range(2):
            h = 2 * p + a
            hs = slice(h * BLOCK, (h + 1) * BLOCK)
            acc = acc + _mm(e[hs], variant(vb, p // 2, a)) * inv[hs]
        o_ref[0, :, sl] = (acc * sga_ref[0, :, sl].astype(F32)).astype(BF16)


def _attn0(q, k, v, sga, bias0, sinks):
    b, l, _ = q.shape
    nb = l // BLOCK
    ones = jnp.ones((2 * BLOCK, LANES), BF16)

    def cur(n):
        return pl.BlockSpec((1, BLOCK, n), lambda bb, i: (bb, i, 0))

    def prev(n):
        return pl.BlockSpec((1, BLOCK, n), lambda bb, i: (bb, jnp.maximum(i - 1, 0), 0))

    return pl.pallas_call(
        _attn0_kernel,
        grid=(b, nb),
        in_specs=[cur(512), cur(512), prev(512), cur(512), prev(512), cur(512),
                  _const_spec(bias0.shape), _const_spec(sinks.shape), _const_spec(ones.shape)],
        out_specs=cur(512),
        out_shape=jax.ShapeDtypeStruct((b, l, 512), BF16),
        compiler_params=_params("arbitrary", "arbitrary"),
        name="attn0",
    )(q, k, k, v, v, sga, bias0, sinks, ones)


def _ssm_kernel(u_ref, sgb_ref, bmat_ref, cre_ref, cim_ref, sc_ref, d_ref, gw_ref, gb_ref, o_ref, xre_ref, xim_ref):
    t = u_ref.shape[1]
    nq = bmat_ref.shape[0]
    half = bmat_ref.shape[2] // 2

    @pl.when(pl.program_id(1) == 0)
    def _():
        xre_ref[0:SUBLANES, :] = jnp.zeros((SUBLANES, xre_ref.shape[1]), F32)
        xim_ref[0:SUBLANES, :] = jnp.zeros((SUBLANES, xim_ref.shape[1]), F32)

    u = u_ref[0]
    ub = u.astype(BF16)
    for q in range(nq):
        bu = _mm(ub[:, q * LANES:(q + 1) * LANES], bmat_ref[q])
        xre_ref[SUBLANES:, q * half:(q + 1) * half] = bu[:, :half]
        xim_ref[SUBLANES:, q * half:(q + 1) * half] = bu[:, half:]

    def scan(r, _):
        base = pl.multiple_of(SUBLANES + r * SUBLANES, SUBLANES)
        xr = xre_ref[pl.ds(base, SUBLANES), :]
        xi = xim_ref[pl.ds(base, SUBLANES), :]
        for s, k in enumerate((1, 2, 4)):
            ar = sc_ref[2 * s]
            ai = sc_ref[2 * s + 1]
            sr = pltpu.roll(xr, k, axis=0)
            si = pltpu.roll(xi, k, axis=0)
            xr, xi = xr + ar * sr - ai * si, xi + ar * si + ai * sr
        cr = xre_ref[pl.ds(base - 1, 1), :]
        ci = xim_ref[pl.ds(base - 1, 1), :]
        pr = sc_ref[6]
        pi = sc_ref[7]
        xre_ref[pl.ds(base, SUBLANES), :] = xr + pr * cr - pi * ci
        xim_ref[pl.ds(base, SUBLANES), :] = xi + pr * ci + pi * cr
        return 0

    lax.fori_loop(0, t // SUBLANES, scan, 0, unroll=2)
    xre_ref[0:SUBLANES, :] = xre_ref[t:t + SUBLANES, :]
    xim_ref[0:SUBLANES, :] = xim_ref[t:t + SUBLANES, :]

    ys = []
    for q in range(nq):
        xr = xre_ref[SUBLANES:, q * half:(q + 1) * half].astype(BF16)
        xi = xim_ref[SUBLANES:, q * half:(q + 1) * half].astype(BF16)
        ys.append(_mm(xr, cre_ref[q]) + _mm(xi, cim_ref[q]))
    y = jnp.concatenate(ys, axis=1) + d_ref[...] * u
    y = jax.nn.gelu(y).astype(BF16)
    hh = _mm(y, gw_ref[...]) + gb_ref[...]
    w = hh.shape[1] // 2
    o_ref[0] = (hh[:, :w] * jax.nn.sigmoid(hh[:, w:]) * sgb_ref[0].astype(F32)).astype(BF16)


def _ssm(u, sgb, bmat, cre, cim, sc, dskip, gw, gb):
    b, l, w = u.shape
    t = ROW_TILE
    ns = sc.shape[-1]

    def row(n):
        return pl.BlockSpec((1, t, n), lambda bb, i: (bb, i, 0))

    return pl.pallas_call(
        _ssm_kernel,
        grid=(b, l // t),
        in_specs=[row(w), row(w), _const_spec(bmat.shape), _const_spec(cre.shape), _const_spec(cim.shape),
                  _const_spec(sc.shape), _const_spec(dskip.shape), _const_spec(gw.shape), _const_spec(gb.shape)],
        out_specs=row(w),
        out_shape=jax.ShapeDtypeStruct((b, l, w), BF16),
        scratch_shapes=[pltpu.VMEM((SUBLANES + t, ns), F32), pltpu.VMEM((SUBLANES + t, ns), F32)],
        compiler_params=_params("arbitrary", "arbitrary"),
        name="ssm",
    )(u, sgb, bmat, cre, cim, sc, dskip, gw, gb)


def _s5_prep(log_dt, a_re, a_im, b_re, b_im, c_re, c_im):
    g, p = a_re.shape
    h = b_re.shape[-1]
    gl = LANES // h
    nq = g // gl
    dt = jnp.exp(log_dt)[:, None]
    mag = jnp.exp(a_re * dt)
    ang = a_im * dt
    ab_re = mag * jnp.cos(ang)
    ab_im = mag * jnp.sin(ang)
    den = a_re * a_re + a_im * a_im
    n_re = ab_re - 1.0
    n_im = ab_im
    f_re = (n_re * a_re + n_im * a_im) / den
    f_im = (n_im * a_re - n_re * a_im) / den
    bb_re = f_re[..., None] * b_re - f_im[..., None] * b_im
    bb_im = f_re[..., None] * b_im + f_im[..., None] * b_re
    eye = jnp.eye(gl, dtype=F32)

    def bdiag_in(m):
        m = m.reshape(nq, gl, p, h)
        return jnp.einsum('qgph,gk->qghkp', m, eye).reshape(nq, gl * h, gl * p)

    def bdiag_out(m):
        m = m.reshape(nq, gl, h, p)
        return jnp.einsum('qghp,gk->qgpkh', m, eye).reshape(nq, gl * p, gl * h)

    bmat = jnp.concatenate([bdiag_in(bb_re), bdiag_in(bb_im)], axis=2).astype(BF16)
    cre = bdiag_out(c_re).astype(BF16)
    cim = bdiag_out(-c_im).astype(BF16)

    pw = [(ab_re.reshape(-1), ab_im.reshape(-1))]
    for _ in range(SUBLANES - 1):
        pr, pi = pw[-1]
        pw.append((pr * pw[0][0] - pi * pw[0][1], pr * pw[0][1] + pi * pw[0][0]))
    rows = jnp.arange(SUBLANES)[:, None]
    sc = []
    for k in (1, 2, 4):
        sc.append(jnp.where(rows >= k, pw[k - 1][0][None, :], 0.0))
        sc.append(jnp.where(rows >= k, pw[k - 1][1][None, :], 0.0))
    sc.append(jnp.stack([pw[r][0] for r in range(SUBLANES)]))
    sc.append(jnp.stack([pw[r][1] for r in range(SUBLANES)]))
    return bmat, cre, cim, jnp.stack(sc).astype(F32)


def _mid_kernel(x_ref, a_ref, s_ref, wo_ref, g_ref, wq_ref, wk_ref, wvt_ref, wg_ref, wqi_ref, wki_ref, wwt_ref,
                qg_ref, kg_ref, h_ref, q_ref, k_ref, vt_ref, sg_ref, qi_ref, ki_ref, wt_ref):
    aw = a_ref.shape[2]
    h = x_ref[0] + _mm(a_ref[0], wo_ref[0:aw, :]) + _mm(s_ref[0], wo_ref[aw:, :])
    h_ref[0] = h
    hn = _rms(h, g_ref[...]).astype(BF16)
    qf = _mm(hn, wq_ref[...])
    for hd in range(C_HEADS):
        sl = slice(hd * C_HEAD_DIM, (hd + 1) * C_HEAD_DIM)
        q_ref[0, :, sl] = (_rms(qf[:, sl], qg_ref[...]) * (C_HEAD_DIM ** -0.5)).astype(BF16)
    kf = _mm(hn, wk_ref[...])
    for hd in range(C_KV_HEADS):
        sl = slice(hd * C_HEAD_DIM, (hd + 1) * C_HEAD_DIM)
        k_ref[0, :, sl] = _rms(kf[:, sl], kg_ref[...]).astype(BF16)
    vt_ref[0] = _mm_nt(wvt_ref[...], hn).astype(BF16)
    sg_ref[0] = _silu(_mm(hn, wg_ref[...])).astype(BF16)
    qi_ref[0] = _mm(hn, wqi_ref[...]).astype(BF16)
    ki_ref[0] = _mm(hn, wki_ref[...]).astype(BF16)
    wt_ref[0] = _mm_nt(wwt_ref[...], hn) * ((IDX_HEADS ** -0.5) * (IDX_DIM ** -0.5))


def _mid(x, att0, ssm0, wo, g, wq, wk, wvt, wg, wqi, wki2, wwt, qg, kg):
    b, l, d = x.shape
    t = ROW_TILE

    def row(n):
        return pl.BlockSpec((1, t, n), lambda bb, i: (bb, i, 0))

    def col(n):
        return pl.BlockSpec((1, n, t), lambda bb, i: (bb, 0, i))

    weights = [wo, g, wq, wk, wvt, wg, wqi, wki2, wwt, qg, kg]
    cw = C_HEADS * C_HEAD_DIM
    ckv = C_KV_HEADS * C_HEAD_DIM
    return pl.pallas_call(
        _mid_kernel,
        grid=(b, l // t),
        in_specs=[row(d), row(att0.shape[2]), row(ssm0.shape[2])] + [_const_spec(w.shape) for w in weights],
        out_specs=[row(d), row(cw), row(ckv), col(ckv), row(cw), row(IDX_HEADS * IDX_DIM), row(2 * LANES),
                   col(IDX_HEADS)],
        out_shape=[jax.ShapeDtypeStruct((b, l, d), F32), jax.ShapeDtypeStruct((b, l, cw), BF16),
                   jax.ShapeDtypeStruct((b, l, ckv), BF16), jax.ShapeDtypeStruct((b, ckv, l), BF16),
                   jax.ShapeDtypeStruct((b, l, cw), BF16), jax.ShapeDtypeStruct((b, l, IDX_HEADS * IDX_DIM), BF16),
                   jax.ShapeDtypeStruct((b, l, 2 * LANES), BF16), jax.ShapeDtypeStruct((b, IDX_HEADS, l), F32)],
        compiler_params=_params("arbitrary", "arbitrary"),
        name="mid",
    )(x, att0, ssm0, *weights)


def _dsa_kernel(q_ref, qi_ref, wt_ref, sg_ref, k_ref, vt_ref, ki_ref, tab_ref, lb_ref, o_ref,
                sc_ref, best_ref, x_ref, acc_ref, *, topk):
    i = pl.program_id(1)
    ck = KEY_CHUNK
    per = ck // BLOCK
    nch = (i + per) // per
    t_row = i * BLOCK + lax.broadcasted_iota(I32, (1, LANES), 1)
    kiota = lax.broadcasted_iota(I32, (ck, LANES), 0)

    def chunk_off(c):
        return pl.multiple_of(c * ck, ck)

    qi = qi_ref[0]
    qi_stack = [jnp.concatenate([qi[:, (2 * s) * LANES:(2 * s + 1) * LANES],
                                 qi[:, (2 * s + 1) * LANES:(2 * s + 2) * LANES]], axis=0) for s in range(2)]
    wt = wt_ref[0]

    def score_chunk(c, masked):
        off = chunk_off(c)
        sc = jnp.zeros((ck, LANES), F32)
        for a in range(2):
            kk = ki_ref[0, pl.ds(off, ck), a * LANES:(a + 1) * LANES]
            for s in range(2):
                r = _mm_nt(kk, qi_stack[s])
                for j in range(2):
                    hd = 2 * (2 * s + j) + a
                    sc = sc + jnp.maximum(r[:, j * LANES:(j + 1) * LANES], 0.0) * wt[hd:hd + 1, :]
        if masked:
            sc = jnp.where(off + kiota <= t_row, sc, NEG_INF)
        sc_ref[pl.ds(off, ck), :] = sc
        return _fold(sc, jnp.maximum)

    def exchange(v, a, b):
        v[a], v[b] = jnp.maximum(v[a], v[b]), jnp.minimum(v[a], v[b])

    def sort_desc(v):
        n, k = len(v), 2
        while k <= n:
            j = k // 2
            while j >= 1:
                for a in range(n):
                    b = a ^ j
                    if b > a:
                        exchange(v, *((a, b) if (a & k) == 0 else (b, a)))
                j //= 2
            k *= 2

    def merge_top(best, blk):
        n = len(best)
        v = [jnp.maximum(best[r], blk[n - 1 - r]) for r in range(n)]
        j = n // 2
        while j >= 1:
            for a in range(n):
                if a ^ j > a:
                    exchange(v, a, a ^ j)
            j //= 2
        return v

    crow = STREAMS * CAND * SUBLANES

    def cand_step(row0):
        blk_all = sc_ref[pl.ds(pl.multiple_of(row0, crow), crow), :]
        for st in range(STREAMS):
            blk = [blk_all[(STREAMS * r + st) * SUBLANES:(STREAMS * r + st + 1) * SUBLANES] for r in range(CAND)]
            sort_desc(blk)
            base = st * CAND * SUBLANES
            best = [best_ref[base + r * SUBLANES:base + (r + 1) * SUBLANES, :] for r in range(CAND)]
            for r, x in enumerate(merge_top(best, blk)):
                best_ref[base + r * SUBLANES:base + (r + 1) * SUBLANES, :] = x

    def score_body(c, mx):
        return jnp.maximum(jnp.maximum(mx, score_chunk(2 * c, False)), score_chunk(2 * c + 1, False))

    below = nch - 1
    smax = lax.fori_loop(0, below // 2, score_body, jnp.full((SUBLANES, LANES), NEG_INF, F32))
    smax = lax.cond(below % 2 == 1, lambda: jnp.maximum(smax, score_chunk(below - 1, False)), lambda: smax)
    smax = jnp.max(jnp.maximum(smax, score_chunk(nch - 1, True)), axis=0, keepdims=True)

    def count(*preds):
        rows = COUNT_ROWS
        sub = ck // rows

        def body(c, accs):
            out = []
            for u in range(sub):
                off = pl.multiple_of(c * ck + u * rows, rows)
                s = sc_ref[pl.ds(off, rows), :]
                for n, pred in enumerate(preds):
                    ind = pred(s, off).astype(I32)
                    out.append(accs[u * len(preds) + n]
                               + jnp.sum(ind.reshape(rows // SUBLANES, SUBLANES, LANES), axis=0))
            return tuple(out)

        accs = lax.fori_loop(0, nch, body, tuple(jnp.zeros((SUBLANES, LANES), I32) for _ in range(sub * len(preds))))
        res = [jnp.sum(sum(accs[n::len(preds)]), axis=0, keepdims=True) for n in range(len(preds))]
        return res[0] if len(preds) == 1 else res

    def key_value(k):
        return pltpu.bitcast(k ^ ((k >> 31) & MAGNITUDE_BITS), F32)

    def count_ge(k):
        thr = key_value(k)
        return count(lambda s, off: s >= thr)

    def full(v):
        return jnp.full((1, LANES), v, I32)

    def bisect(_, st):
        lo, hi, c_lo, c_hi = st
        mid = (lo >> 1) + (hi >> 1) + (lo & hi & 1)
        c = count_ge(mid)
        ge = c >= topk
        return jnp.where(ge, mid, lo), jnp.where(ge, hi, mid), jnp.where(ge, c, c_lo), jnp.where(ge, c_hi, c)

    searching = (i + 1) * BLOCK > topk

    def float_key(x):
        bits = pltpu.bitcast(x, I32)
        return bits ^ ((bits >> 31) & MAGNITUDE_BITS)

    def search():
        k_lo = float_key(smax * 0.125)
        c = count_ge(k_lo)
        ok = (smax > 0.0) & (c >= topk)
        trips = jnp.where(jnp.min(jnp.where(ok, 1, 0)) > 0, 25, 32)
        st = (jnp.where(ok, k_lo, KEY_NEG_INF), float_key(smax) + 1, jnp.where(ok, c, nch * ck), full(0))
        out = lax.fori_loop(0, trips, bisect, st)
        return out[0], out[2], out[3]

    def cand_search():
        best_ref[...] = jnp.full(best_ref.shape, -jnp.inf, F32)

        def cand_body(c, _):
            cand_step(c * crow)
            return 0

        lax.fori_loop(0, (i + crow // BLOCK) // (crow // BLOCK), cand_body, 0)

        def all_sublanes(x):
            for shift in (4, 2, 1):
                x = x + pltpu.roll(x, shift, axis=0)
            return x

        def count_cand(k):
            thr = key_value(k)[None]
            parts = [jnp.sum((best_ref[r:r + COUNT_ROWS, :].reshape(COUNT_ROWS // SUBLANES, SUBLANES, LANES)
                              >= thr).astype(I32), axis=0) for r in range(0, crow, COUNT_ROWS)]
            return all_sublanes(sum(parts))

        def step(_, st):
            lo, hi = st
            mid = (lo >> 1) + (hi >> 1) + (lo & hi & 1)
            take = ~((count_cand(mid) - topk) >> 31)
            return (mid & take) | (lo & ~take), (hi & take) | (mid & ~take)

        smax8 = jnp.broadcast_to(smax, (SUBLANES, LANES))
        k_lo = float_key(smax8 * 0.125)
        ok = (smax8 > 0.0) & (count_cand(k_lo) >= topk)
        trips = jnp.where(jnp.min(jnp.where(ok, 1, 0)) > 0, 25, 32)
        vk8, _ = lax.fori_loop(0, trips, step, (jnp.where(ok, k_lo, KEY_NEG_INF), float_key(smax8) + 1))
        vk = vk8[0:1]
        thr = key_value(vk)
        above_cand = jnp.sum((best_ref[...] > thr).astype(I32), axis=0, keepdims=True)
        c_ge, c_gt = count(lambda s, off: s >= thr, lambda s, off: s > thr)
        complete = jnp.min(jnp.where(c_gt == above_cand, 1, 0)) > 0
        return lax.cond(complete, lambda: (vk, c_ge, c_gt), search)

    vkey, c_lo, c_hi = lax.cond(searching, cand_search, lambda: (full(KEY_NEG_INF), full(topk), full(0)))
    vthr = key_value(vkey)
    need = topk - c_hi
    ties = c_lo - c_hi

    def tie_search():
        def split_step():
            nxt = vkey + 1
            nxt = jnp.where((nxt > 0) & (nxt < KEY_MIN_NORMAL), KEY_MIN_NORMAL, nxt)
            step = key_value(nxt) - vthr

            def split(_, st):
                fl, fh = st
                fm = 0.5 * (fl + fh)
                t = vthr + fm * step
                ge = count(lambda s, off: s >= t) >= topk
                return jnp.where(ge, fm, fl), jnp.where(ge, fh, fm)

            fl, _ = lax.fori_loop(0, 26, split, (jnp.zeros((1, LANES), F32), jnp.ones((1, LANES), F32)))
            t = vthr + fl * step
            return t, topk - count(lambda s, off: s > t)

        inside = jnp.max(ties - count(lambda s, off: s == vthr)) > 0
        thr, want = lax.cond(inside, split_step, lambda: (vthr, need))

        rr = lax.broadcasted_iota(I32, (BLOCK, BLOCK), 0)
        cc = lax.broadcasted_iota(I32, (BLOCK, BLOCK), 1)
        tril = jnp.where(cc <= rr, 1.0, 0.0).astype(BF16)
        want_f = want.astype(F32)

        def body(c, before):
            off = chunk_off(c)
            blocks = [sc_ref[pl.ds(pl.multiple_of(off + r * BLOCK, BLOCK), BLOCK), :] for r in range(per)]
            hits = [s == thr for s in blocks]
            ranks = [_mm(tril, jnp.where(h, 1.0, 0.0).astype(BF16)) for h in hits]
            for r in range(per):
                rank = ranks[r] + before
                sc_ref[pl.ds(pl.multiple_of(off + r * BLOCK, BLOCK), BLOCK), :] = jnp.where(
                    hits[r] & (rank > want_f), NEG_INF, blocks[r])
                before = rank[BLOCK - 1:BLOCK, :]
            return before

        lax.fori_loop(0, nch, body, jnp.zeros((1, LANES), F32))
        return thr

    any_tie = searching & (jnp.max(ties - need) > 0)
    vthr = lax.cond(any_tie, tie_search, lambda: vthr)

    def selection_mask(off):
        s = sc_ref[pl.ds(off, ck), :]
        s_idx = off + kiota
        sel = (s >= vthr) & (s_idx <= t_row)
        madd = jnp.where(sel, 0.0, NEG_INF)
        sc_ref[pl.ds(off, ck), :] = madd
        return madd

    q = q_ref[0]
    n_far = jnp.maximum((i - NEAR_BLOCKS + 1) // per, 0)
    hpg = C_HEADS // C_KV_HEADS
    npair = C_HEADS // 2
    q_pairs = [jnp.concatenate([q[:, (2 * j) * LANES:(2 * j + 1) * LANES],
                                q[:, (2 * j + 1) * LANES:(2 * j + 2) * LANES]], axis=0) for j in range(npair)]

    def bias_rows(hd, c):
        return jnp.concatenate([tab_ref[hd, jnp.clip(i - (c * per + r), 0, NEAR_BLOCKS)] for r in range(per)],
                               axis=0)

    def emit(hd, num, den):
        sl = slice(hd * LANES, (hd + 1) * LANES)
        o_ref[0, :, sl] = ((num / den).T * sg_ref[0, :, sl].astype(F32)).astype(BF16)

    def exact_attention():
        for g in range(C_KV_HEADS):
            def stage_body(near, g=g):
                def body(c, mx):
                    off = chunk_off(c)
                    madd = sc_ref[pl.ds(off, ck), :]
                    kc = k_ref[0, pl.ds(off, ck), g * LANES:(g + 1) * LANES]
                    out = []
                    for jj in range(hpg // 2):
                        lg = _mm_nt(kc, q_pairs[g * (hpg // 2) + jj])
                        for a in range(2):
                            hl = 2 * jj + a
                            x = lg[:, a * LANES:(a + 1) * LANES] + madd
                            if near:
                                x = x + bias_rows(hpg * g + hl, c)
                            x_ref[hl, pl.ds(off, ck), :] = x
                            out.append(jnp.maximum(mx[hl], _fold(x, jnp.maximum)))
                    return tuple(out)

                return body

            mx = tuple(jnp.full((SUBLANES, LANES), NEG_INF, F32) for _ in range(hpg))
            mx = lax.fori_loop(0, n_far, stage_body(False), mx)
            mx = lax.fori_loop(n_far, nch, stage_body(True), mx)
            m = [jnp.max(v, axis=0, keepdims=True) for v in mx]
            acc_ref[...] = jnp.zeros(acc_ref.shape, F32)

            def att_body(c, ls, g=g, m=m):
                off = chunk_off(c)
                vt = vt_ref[0, g * LANES:(g + 1) * LANES, pl.ds(off, ck)]
                out = []
                for jj in range(hpg // 2):
                    ps = []
                    for a in range(2):
                        hl = 2 * jj + a
                        p = jnp.exp(x_ref[hl, pl.ds(off, ck), :] - m[hl])
                        out.append(ls[hl] + _fold(p, jnp.add))
                        ps.append(p.astype(BF16))
                    acc_ref[jj] += _mm(vt, jnp.concatenate(ps, axis=1))
                return tuple(out)

            ls = lax.fori_loop(0, nch, att_body, tuple(jnp.zeros((SUBLANES, LANES), F32) for _ in range(hpg)))
            for hl in range(hpg):
                emit(hpg * g + hl, acc_ref[hl // 2, :, (hl % 2) * LANES:(hl % 2 + 1) * LANES],
                     jnp.sum(ls[hl], axis=0, keepdims=True))

    lb = lb_ref[...]
    acc_ref[...] = jnp.zeros(acc_ref.shape, F32)

    def stage(c, g, near):
        off = chunk_off(c)
        mb = (selection_mask(off) if g == 0 else sc_ref[pl.ds(off, ck), :]) - lb
        kc = k_ref[0, pl.ds(off, ck), g * LANES:(g + 1) * LANES]
        for jj in range(hpg // 2):
            lg = _mm_nt(kc, q_pairs[g * (hpg // 2) + jj])
            for a in range(2):
                hl = 2 * jj + a
                x = lg[:, a * LANES:(a + 1) * LANES] + mb
                if near:
                    x = x + bias_rows(hpg * g + hl, c)
                x_ref[hl, pl.ds(off, ck), :] = x

    def consume(c, g, ls):
        off = chunk_off(c)
        vt = vt_ref[0, g * LANES:(g + 1) * LANES, pl.ds(off, ck)]
        out = []
        for jj in range(hpg // 2):
            ps = []
            for a in range(2):
                hl = 2 * jj + a
                p = jnp.exp(x_ref[hl, pl.ds(off, ck), :])
                out.append(ls[hl] + _fold(p, jnp.add))
                ps.append(p.astype(BF16))
            acc_ref[g * (hpg // 2) + jj] += _mm(vt, jnp.concatenate(ps, axis=1))
        return tuple(out)

    dens = []
    for g in range(C_KV_HEADS):
        def step(near, g=g):
            def body(c, ls):
                out = consume(c, g, ls)
                stage(c + 1, g, near)
                return out

            return body

        stage(0, g, True)
        ls = tuple(jnp.zeros((SUBLANES, LANES), F32) for _ in range(hpg))
        split = jnp.maximum(n_far - 1, 0)
        ls = lax.fori_loop(0, split, step(False), ls)
        ls = lax.fori_loop(split, nch - 1, step(True), ls)
        ls = consume(nch - 1, g, ls)
        dens += [jnp.sum(v, axis=0, keepdims=True) for v in ls]
    in_range = jnp.min(functools.reduce(jnp.minimum, dens)) > SUM_FLOOR

    @pl.when(in_range)
    def _():
        for hd in range(C_HEADS):
            emit(hd, acc_ref[hd // 2, :, (hd % 2) * LANES:(hd % 2 + 1) * LANES], dens[hd])

    @pl.when(jnp.logical_not(in_range))
    def _():
        exact_attention()


def _dsa(q, qi, wt, sg, k, vt, ki2, tab, lb):
    b, l, cw = q.shape
    nb = l // BLOCK
    topk = min(TOPK_MAX, l // 4)

    def blk(n):
        return pl.BlockSpec((1, BLOCK, n), lambda bb, i: (bb, i, 0))

    def whole(s1, s2):
        return pl.BlockSpec((1, s1, s2), lambda bb, i: (bb, 0, 0), pipeline_mode=pl.Buffered(1))

    hpg = C_HEADS // C_KV_HEADS
    return pl.pallas_call(
        functools.partial(_dsa_kernel, topk=topk),
        grid=(b, nb),
        in_specs=[blk(cw), blk(qi.shape[2]), pl.BlockSpec((1, IDX_HEADS, BLOCK), lambda bb, i: (bb, 0, i)), blk(cw),
                  whole(l, k.shape[2]), whole(vt.shape[1], l), whole(l, ki2.shape[2]),
                  pl.BlockSpec(tab.shape, lambda bb, i: (0, 0, 0, 0), pipeline_mode=pl.Buffered(1)),
                  _const_spec(lb.shape)],
        out_specs=blk(cw),
        out_shape=jax.ShapeDtypeStruct((b, l, cw), BF16),
        scratch_shapes=[pltpu.VMEM((l, LANES), F32), pltpu.VMEM((STREAMS * CAND * SUBLANES, LANES), F32),
                        pltpu.VMEM((hpg, l, LANES), F32),
                        pltpu.VMEM((C_HEADS // 2, C_HEAD_DIM, 2 * LANES), F32)],
        compiler_params=_params("arbitrary", "arbitrary"),
        name="dsa",
    )(q, qi, wt, sg, k, vt, ki2, tab, lb)


def _out_kernel(h_ref, a_ref, w_ref, o_ref):
    o_ref[...] = h_ref[...] + _mm(a_ref[...], w_ref[...])


def _outproj(h2, a2, w):
    rows, d = h2.shape
    t = ROW_TILE
    return pl.pallas_call(
        _out_kernel,
        grid=(rows // t,),
        in_specs=[pl.BlockSpec((t, d), lambda i: (i, 0)), pl.BlockSpec((t, a2.shape[1]), lambda i: (i, 0)),
                  _const_spec(w.shape)],
        out_specs=pl.BlockSpec((t, d), lambda i: (i, 0)),
        out_shape=jax.ShapeDtypeStruct((rows, d), F32),
        compiler_params=_params("arbitrary"),
        name="outproj1",
    )(h2, a2, w)


def _bias_tables(rel_bias):
    nv = (NEAR_BLOCKS + 1) * BLOCK
    vec = rel_bias[_t5_bucket(jnp.arange(nv, dtype=I32))].astype(F32).T

    nh = vec.shape[0]
    vecp = jnp.concatenate([jnp.broadcast_to(vec[:, :1], (nh, BLOCK - 1)), vec], axis=1)

    def toeplitz(g, rows, cols):
        w = rows + cols
        g2 = jnp.concatenate([g[..., rows - 1:rows - 1 + cols], g[..., :1], g[..., :rows - 1]], axis=-1)
        flat = jnp.tile(g2, (1,) * (g.ndim - 1) + (rows,))[..., :rows * (w - 1)]
        return flat.reshape(g.shape[:-1] + (rows, w - 1))[..., :cols]

    bias0 = jnp.transpose(toeplitz(vecp[:, :3 * BLOCK - 1], 2 * BLOCK, BLOCK), (0, 2, 1))
    r = vecp[:, :(NEAR_BLOCKS + 1) * BLOCK].reshape(nh, NEAR_BLOCKS + 1, BLOCK)
    wins = jnp.concatenate([r[:, :-1], r[:, 1:, :BLOCK - 1]], axis=2)
    tab = toeplitz(wins, BLOCK, BLOCK) - rel_bias[NUM_BUCKETS - 1].astype(F32)[:, None, None, None]
    tab = jnp.concatenate([tab, jnp.zeros((tab.shape[0], 1, BLOCK, BLOCK), F32)], axis=1)
    return bias0, tab


def kernel(x, rel_bias, norm_g, ev_w_in, ev_w_out, ev_q_norm_g, ev_k_norm_g, ev_sinks, ev_ssm_log_dt, ev_ssm_a_re,
           ev_ssm_a_im, ev_ssm_b_re, ev_ssm_b_im, ev_ssm_c_re, ev_ssm_c_im, ev_ssm_d, ev_glu_w, ev_glu_b, od_w_in,
           od_w_out, od_q_norm_g, od_k_norm_g):
    b, l, d = x.shape
    assert l % KEY_CHUNK == 0 and l % ROW_TILE == 0
    assert (NEAR_BLOCKS - 1) * BLOCK + 1 >= 16 * 64 ** (15 / 16) + 1
    bias0, tab = _bias_tables(rel_bias)

    w0 = ev_w_in[0]
    hd = A_HEAD_DIM
    o0 = np.cumsum([0, A_WIDTH, A_KV_HEADS * hd, A_KV_HEADS * hd, A_WIDTH, A_WIDTH, A_WIDTH])
    wq0, wk0, wv0, wga, wu, wgb = (w0[:, o0[n]:o0[n + 1]] for n in range(6))
    z = jnp.zeros((d, hd), w0.dtype)

    def variants(w):
        return jnp.concatenate([c for g in range(A_KV_HEADS) for c in (w[:, g * hd:(g + 1) * hd], z, z,
                                                                       w[:, g * hd:(g + 1) * hd])], axis=1)

    w0x = jnp.concatenate([wq0, variants(wk0), variants(wv0), wga, wu, wgb], axis=1).astype(BF16)
    qg2 = jnp.tile(ev_q_norm_g[0], 2)[None, :]
    kg2 = jnp.tile(ev_k_norm_g[0], 2)[None, :]
    q0, k0, v0, sga, u, sgb = _proj0(x.reshape(b * l, d), norm_g[0][None, :], w0x, qg2, kg2)
    shp = lambda a: a.reshape(b, l, a.shape[-1])
    sinks = jnp.broadcast_to(ev_sinks[0][:, None], (A_HEADS, LANES)).astype(F32)
    att0 = _attn0(shp(q0), shp(k0), shp(v0), shp(sga), bias0, sinks)
    bmat, cre, cim, sc = _s5_prep(ev_ssm_log_dt[0], ev_ssm_a_re[0], ev_ssm_a_im[0], ev_ssm_b_re[0], ev_ssm_b_im[0],
                                  ev_ssm_c_re[0], ev_ssm_c_im[0])
    ssm0 = _ssm(shp(u), shp(sgb), bmat, cre, cim, sc, ev_ssm_d[0].reshape(1, -1), ev_glu_w[0].astype(BF16),
                ev_glu_b[0][None, :])

    w1 = od_w_in[0]
    cw = C_HEADS * C_HEAD_DIM
    ckv = C_KV_HEADS * C_HEAD_DIM
    o = np.cumsum([0, cw, ckv, ckv, cw, IDX_HEADS * IDX_DIM, IDX_DIM, IDX_HEADS])
    wq, wk, wv, wg, wqi, wki, ww = (w1[:, o[n]:o[n + 1]] for n in range(7))
    zki = jnp.zeros((d, LANES - IDX_DIM), w1.dtype)
    wki2 = jnp.concatenate([wki, zki, zki, wki], axis=1)
    bf = lambda a: a.astype(BF16)
    h1, q1, k1, vt1, sg1, qi1, ki2, wt1 = _mid(
        x, att0, ssm0, bf(ev_w_out[0]), norm_g[1][None, :], bf(wq), bf(wk), bf(wv.T), bf(wg), bf(wqi), bf(wki2),
        bf(ww.T), od_q_norm_g[0][None, :], od_k_norm_g[0][None, :])
    lb = (1.02 * C_HEAD_DIM ** 0.5 * jnp.max(jnp.abs(od_q_norm_g[0])) * jnp.max(jnp.abs(od_k_norm_g[0]))
          + jnp.max(tab))
    att1 = _dsa(q1, qi1, wt1, sg1, k1, vt1, ki2, tab, jnp.full((1, LANES), lb, F32))
    out = _outproj(h1.reshape(b * l, d), att1.reshape(b * l, cw), bf(od_w_out[0]))
    return out.reshape(b, l, d)
```

```python
import functools
import math

import jax
import jax.numpy as jnp
import numpy as np
from jax import lax
from jax.experimental import pallas as pl
from jax.experimental.pallas import tpu as pltpu

F32 = jnp.float32
BF16 = jnp.bfloat16
I32 = jnp.int32

LANES = 128
SUBLANES = 8
VMEM_LIMIT = 56 * 1024 * 1024

BLOCK = 128
WINDOW = 128
A_HEADS = 8
A_HEAD_DIM = 64
A_KV_HEADS = 2
A_WIDTH = A_HEADS * A_HEAD_DIM
SSM_GROUP = 16
SSM_STATE = 64
C_HEADS = 8
C_HEAD_DIM = 128
C_KV_HEADS = 2
IDX_HEADS = 8
IDX_DIM = 64
TOPK_MAX = 256
NUM_BUCKETS = 32
REL_MAX_DIST = 1024
EPS = 1e-6
NEG_INF = -1e30
INT_MIN = -(2 ** 31)
KEY_MIN_NORMAL = 0x00800000
KEY_POS_INF = 0x7F800000
KEY_NEG_INF = INT_MIN + 0x007FFFFF
MAGNITUDE_BITS = 0x7FFFFFFF
MANTISSA_BITS = 23
BRACKET_BINADES = 3
BRACKET_STEPS = math.ceil(math.log2(BRACKET_BINADES * 2 ** MANTISSA_BITS + 1))
FULL_STEPS = math.ceil(math.log2(2 * KEY_POS_INF + 2))
SPLIT_STEPS = 26
BF16_SLACK = 1.02

ROW_TILE = 1024
KEY_CHUNK = 1024
NEAR_BLOCKS = 8
FOLD_CHAINS = 8
COUNT_ROWS = 512
SUM_FLOOR = 1e-30
CAND = 32
STREAMS = 2
NT_DIMS = (((1,), (1,)), ((), ()))


def _t5_bucket(dist):
    n = jnp.maximum(dist, 0)
    max_exact = NUM_BUCKETS // 2
    nf = jnp.maximum(n, 1).astype(F32)
    large = max_exact + (jnp.log(nf / max_exact) / math.log(REL_MAX_DIST / max_exact)
                         * (NUM_BUCKETS - max_exact)).astype(I32)
    large = jnp.minimum(large, NUM_BUCKETS - 1)
    return jnp.where(n < max_exact, n, large)


def _silu(x):
    return x * jax.nn.sigmoid(x)


def _rms(x, g):
    ms = jnp.mean(x * x, axis=-1, keepdims=True)
    return x * lax.rsqrt(ms + EPS) * g


def _mm(a, b):
    return jnp.dot(a, b, preferred_element_type=F32)


def _mm_nt(a, b):
    return lax.dot_general(a, b, NT_DIMS, preferred_element_type=F32)


def _fold(x, op):
    n = x.shape[0] // SUBLANES
    chains = min(FOLD_CHAINS, n)
    accs = [x[r * SUBLANES:(r + 1) * SUBLANES] for r in range(chains)]
    for r in range(chains, n):
        accs[r % chains] = op(accs[r % chains], x[r * SUBLANES:(r + 1) * SUBLANES])
    while len(accs) > 1:
        accs = [op(a, b) for a, b in zip(accs[::2], accs[1::2])] + accs[len(accs) & ~1:]
    return accs[0]


def _params(*sem):
    return pltpu.CompilerParams(dimension_semantics=sem, vmem_limit_bytes=VMEM_LIMIT)


def _const_spec(shape):
    zeros = (0,) * len(shape)
    return pl.BlockSpec(shape, lambda *_: zeros)


def _proj0_kernel(x_ref, g_ref, w_ref, qg_ref, kg_ref, q_ref, k_ref, v_ref, sga_ref, u_ref, sgb_ref):
    hn = _rms(x_ref[...], g_ref[...]).astype(BF16)
    lo = lax.broadcasted_iota(I32, (1, LANES), 1) < A_HEAD_DIM

    def mm(n):
        return _mm(hn, w_ref[:, n * A_WIDTH:(n + 1) * A_WIDTH])

    def segnorm(x, g2):
        sq = x * x
        s_lo = jnp.sum(jnp.where(lo, sq, 0.0), axis=-1, keepdims=True)
        s_hi = jnp.sum(jnp.where(lo, 0.0, sq), axis=-1, keepdims=True)
        inv = jnp.where(lo, lax.rsqrt(s_lo / A_HEAD_DIM + EPS), lax.rsqrt(s_hi / A_HEAD_DIM + EPS))
        return x * inv * g2

    q, k = mm(0), mm(1)
    for p in range(A_WIDTH // LANES):
        sl = slice(p * LANES, (p + 1) * LANES)
        q_ref[:, sl] = (segnorm(q[:, sl], qg_ref[...]) * (A_HEAD_DIM ** -0.5)).astype(BF16)
        k_ref[:, sl] = segnorm(k[:, sl], kg_ref[...]).astype(BF16)
    v_ref[...] = mm(2).astype(BF16)
    sga_ref[...] = _silu(mm(3)).astype(BF16)
    u_ref[...] = mm(4)
    sgb_ref[...] = _silu(mm(5)).astype(BF16)


def _proj0(x2, g, w, qg2, kg2):
    rows, d = x2.shape
    t = ROW_TILE

    def row(n):
        return pl.BlockSpec((t, n), lambda i: (i, 0))

    n = A_WIDTH
    return pl.pallas_call(
        _proj0_kernel,
        grid=(rows // t,),
        in_specs=[row(d), _const_spec((1, d)), _const_spec(w.shape), _const_spec(qg2.shape), _const_spec(kg2.shape)],
        out_specs=[row(n)] * 6,
        out_shape=[jax.ShapeDtypeStruct((rows, n), BF16), jax.ShapeDtypeStruct((rows, n), BF16),
                   jax.ShapeDtypeStruct((rows, n), BF16), jax.ShapeDtypeStruct((rows, n), BF16),
                   jax.ShapeDtypeStruct((rows, n), F32), jax.ShapeDtypeStruct((rows, n), BF16)],
        compiler_params=_params("arbitrary"),
        name="proj0",
    )(x2, g, w, qg2, kg2)


def _attn0_kernel(q_ref, kc_ref, kp_ref, vc_ref, vp_ref, sga_ref, bias_ref, sink_ref, ones_ref, o_ref):
    i = pl.program_id(1)
    kb = jnp.concatenate([kp_ref[0], kc_ref[0]], axis=0)
    vb = jnp.concatenate([vp_ref[0], vc_ref[0]], axis=0)

    def variant(x, g, a):
        n = 2 * g + a
        return x[:, n * LANES:(n + 1) * LANES]

    row = lax.broadcasted_iota(I32, (BLOCK, 2 * BLOCK), 0)
    col = lax.broadcasted_iota(I32, (BLOCK, 2 * BLOCK), 1)
    d = row + BLOCK - col
    mask = (d >= 0) & (d < WINDOW) & ((i > 0) | (col >= BLOCK))

    lgs, sinks = [], []
    for p in range(A_HEADS // 2):
        qp = q_ref[0, :, p * LANES:(p + 1) * LANES]
        for a in range(2):
            h = 2 * p + a
            lgs.append(jnp.where(mask, _mm_nt(qp, variant(kb, p // 2, a)) + bias_ref[h], NEG_INF))
            sinks.append(jnp.broadcast_to(sink_ref[h:h + 1, 0:1], (BLOCK, 1)))
    lg = jnp.concatenate(lgs, axis=0)
    sink = jnp.concatenate(sinks, axis=0)
    m = jnp.maximum(jnp.max(lg, axis=-1, keepdims=True), sink)
    e = jnp.exp(lg - m).astype(BF16)
    inv = 1.0 / (_mm(e, ones_ref[...]) + jnp.exp(sink - m))
    for p in range(A_HEADS // 2):
        sl = slice(p * LANES, (p + 1) * LANES)
        acc = jnp.zeros((BLOCK, LANES), F32)
        for a in range(2):
            h = 2 * p + a
            hs = slice(h * BLOCK, (h + 1) * BLOCK)
            acc = acc + _mm(e[hs], variant(vb, p // 2, a)) * inv[hs]
        o_ref[0, :, sl] = (acc * sga_ref[0, :, sl].astype(F32)).astype(BF16)


def _attn0(q, k, v, sga, bias0, sinks):
    b, l, _ = q.shape
    nb = l // BLOCK
    ones = jnp.ones((2 * BLOCK, LANES), BF16)

    def cur(n):
        return pl.BlockSpec((1, BLOCK, n), lambda bb, i: (bb, i, 0))

    def prev(n):
        return pl.BlockSpec((1, BLOCK, n), lambda bb, i: (bb, jnp.maximum(i - 1, 0), 0))

    return pl.pallas_call(
        _attn0_kernel,
        grid=(b, nb),
        in_specs=[cur(512), cur(512), prev(512), cur(512), prev(512), cur(512),
                  _const_spec(bias0.shape), _const_spec(sinks.shape), _const_spec(ones.shape)],
        out_specs=cur(512),
        out_shape=jax.ShapeDtypeStruct((b, l, 512), BF16),
        compiler_params=_params("arbitrary", "arbitrary"),
        name="attn0",
    )(q, k, k, v, v, sga, bias0, sinks, ones)


def _ssm_kernel(u_ref, sgb_ref, bmat_ref, cre_ref, cim_ref, sc_ref, d_ref, gw_ref, gb_ref, o_ref, xre_ref, xim_ref):
    t = u_ref.shape[1]
    nq = bmat_ref.shape[0]
    half = bmat_ref.shape[2] // 2

    @pl.when(pl.program_id(1) == 0)
    def _():
        xre_ref[0:SUBLANES, :] = jnp.zeros((SUBLANES, xre_ref.shape[1]), F32)
        xim_ref[0:SUBLANES, :] = jnp.zeros((SUBLANES, xim_ref.shape[1]), F32)

    u = u_ref[0]
    ub = u.astype(BF16)
    for q in range(nq):
        bu = _mm(ub[:, q * LANES:(q + 1) * LANES], bmat_ref[q])
        xre_ref[SUBLANES:, q * half:(q + 1) * half] = bu[:, :half]
        xim_ref[SUBLANES:, q * half:(q + 1) * half] = bu[:, half:]

    def scan(r, _):
        base = pl.multiple_of(SUBLANES + r * SUBLANES, SUBLANES)
        xr = xre_ref[pl.ds(base, SUBLANES), :]
        xi = xim_ref[pl.ds(base, SUBLANES), :]
        for s, k in enumerate((1, 2, 4)):
            ar = sc_ref[2 * s]
            ai = sc_ref[2 * s + 1]
            sr = pltpu.roll(xr, k, axis=0)
            si = pltpu.roll(xi, k, axis=0)
            xr, xi = xr + ar * sr - ai * si, xi + ar * si + ai * sr
        cr = xre_ref[pl.ds(base - 1, 1), :]
        ci = xim_ref[pl.ds(base - 1, 1), :]
        pr = sc_ref[6]
        pi = sc_ref[7]
        xre_ref[pl.ds(base, SUBLANES), :] = xr + pr * cr - pi * ci
        xim_ref[pl.ds(base, SUBLANES), :] = xi + pr * ci + pi * cr
        return 0

    lax.fori_loop(0, t // SUBLANES, scan, 0, unroll=2)
    xre_ref[0:SUBLANES, :] = xre_ref[t:t + SUBLANES, :]
    xim_ref[0:SUBLANES, :] = xim_ref[t:t + SUBLANES, :]

    ys = []
    for q in range(nq):
        xr = xre_ref[SUBLANES:, q * half:(q + 1) * half].astype(BF16)
        xi = xim_ref[SUBLANES:, q * half:(q + 1) * half].astype(BF16)
        ys.append(_mm(xr, cre_ref[q]) + _mm(xi, cim_ref[q]))
    y = jnp.concatenate(ys, axis=1) + d_ref[...] * u
    y = jax.nn.gelu(y).astype(BF16)
    hh = _mm(y, gw_ref[...]) + gb_ref[...]
    w = hh.shape[1] // 2
    o_ref[0] = (hh[:, :w] * jax.nn.sigmoid(hh[:, w:]) * sgb_ref[0].astype(F32)).astype(BF16)


def _ssm(u, sgb, bmat, cre, cim, sc, dskip, gw, gb):
    b, l, w = u.shape
    t = ROW_TILE
    ns = sc.shape[-1]

    def row(n):
        return pl.BlockSpec((1, t, n), lambda bb, i: (bb, i, 0))

    return pl.pallas_call(
        _ssm_kernel,
        grid=(b, l // t),
        in_specs=[row(w), row(w), _const_spec(bmat.shape), _const_spec(cre.shape), _const_spec(cim.shape),
                  _const_spec(sc.shape), _const_spec(dskip.shape), _const_spec(gw.shape), _const_spec(gb.shape)],
        out_specs=row(w),
        out_shape=jax.ShapeDtypeStruct((b, l, w), BF16),
        scratch_shapes=[pltpu.VMEM((SUBLANES + t, ns), F32), pltpu.VMEM((SUBLANES + t, ns), F32)],
        compiler_params=_params("arbitrary", "arbitrary"),
        name="ssm",
    )(u, sgb, bmat, cre, cim, sc, dskip, gw, gb)


def _s5_prep(log_dt, a_re, a_im, b_re, b_im, c_re, c_im):
    g, p = a_re.shape
    h = b_re.shape[-1]
    gl = LANES // h
    nq = g // gl
    dt = jnp.exp(log_dt)[:, None]
    mag = jnp.exp(a_re * dt)
    ang = a_im * dt
    ab_re = mag * jnp.cos(ang)
    ab_im = mag * jnp.sin(ang)
    den = a_re * a_re + a_im * a_im
    n_re = ab_re - 1.0
    n_im = ab_im
    f_re = (n_re * a_re + n_im * a_im) / den
    f_im = (n_im * a_re - n_re * a_im) / den
    bb_re = f_re[..., None] * b_re - f_im[..., None] * b_im
    bb_im = f_re[..., None] * b_im + f_im[..., None] * b_re
    eye = jnp.eye(gl, dtype=F32)

    def bdiag_in(m):
        m = m.reshape(nq, gl, p, h)
        return jnp.einsum('qgph,gk->qghkp', m, eye).reshape(nq, gl * h, gl * p)

    def bdiag_out(m):
        m = m.reshape(nq, gl, h, p)
        return jnp.einsum('qghp,gk->qgpkh', m, eye).reshape(nq, gl * p, gl * h)

    bmat = jnp.concatenate([bdiag_in(bb_re), bdiag_in(bb_im)], axis=2).astype(BF16)
    cre = bdiag_out(c_re).astype(BF16)
    cim = bdiag_out(-c_im).astype(BF16)

    pw = [(ab_re.reshape(-1), ab_im.reshape(-1))]
    for _ in range(SUBLANES - 1):
        pr, pi = pw[-1]
        pw.append((pr * pw[0][0] - pi * pw[0][1], pr * pw[0][1] + pi * pw[0][0]))
    rows = jnp.arange(SUBLANES)[:, None]
    sc = []
    for k in (1, 2, 4):
        sc.append(jnp.where(rows >= k, pw[k - 1][0][None, :], 0.0))
        sc.append(jnp.where(rows >= k, pw[k - 1][1][None, :], 0.0))
    sc.append(jnp.stack([pw[r][0] for r in range(SUBLANES)]))
    sc.append(jnp.stack([pw[r][1] for r in range(SUBLANES)]))
    return bmat, cre, cim, jnp.stack(sc).astype(F32)


def _mid_kernel(x_ref, a_ref, s_ref, wo_ref, g_ref, wq_ref, wk_ref, wvt_ref, wg_ref, wqi_ref, wki_ref, wwt_ref,
                qg_ref, kg_ref, h_ref, q_ref, k_ref, vt_ref, sg_ref, qi_ref, ki_ref, wt_ref):
    aw = a_ref.shape[2]
    h = x_ref[0] + _mm(a_ref[0], wo_ref[0:aw, :]) + _mm(s_ref[0], wo_ref[aw:, :])
    h_ref[0] = h
    hn = _rms(h, g_ref[...]).astype(BF16)
    qf = _mm(hn, wq_ref[...])
    for hd in range(C_HEADS):
        sl = slice(hd * C_HEAD_DIM, (hd + 1) * C_HEAD_DIM)
        q_ref[0, :, sl] = (_rms(qf[:, sl], qg_ref[...]) * (C_HEAD_DIM ** -0.5)).astype(BF16)
    kf = _mm(hn, wk_ref[...])
    for hd in range(C_KV_HEADS):
        sl = slice(hd * C_HEAD_DIM, (hd + 1) * C_HEAD_DIM)
        k_ref[0, :, sl] = _rms(kf[:, sl], kg_ref[...]).astype(BF16)
    vt_ref[0] = _mm_nt(wvt_ref[...], hn).astype(BF16)
    sg_ref[0] = _silu(_mm(hn, wg_ref[...])).astype(BF16)
    qi_ref[0] = _mm(hn, wqi_ref[...]).astype(BF16)
    ki_ref[0] = _mm(hn, wki_ref[...]).astype(BF16)
    wt_ref[0] = _mm_nt(wwt_ref[...], hn) * ((IDX_HEADS ** -0.5) * (IDX_DIM ** -0.5))


def _mid(x, att0, ssm0, wo, g, wq, wk, wvt, wg, wqi, wki2, wwt, qg, kg):
    b, l, d = x.shape
    t = ROW_TILE

    def row(n):
        return pl.BlockSpec((1, t, n), lambda bb, i: (bb, i, 0))

    def col(n):
        return pl.BlockSpec((1, n, t), lambda bb, i: (bb, 0, i))

    weights = [wo, g, wq, wk, wvt, wg, wqi, wki2, wwt, qg, kg]
    cw = C_HEADS * C_HEAD_DIM
    ckv = C_KV_HEADS * C_HEAD_DIM
    return pl.pallas_call(
        _mid_kernel,
        grid=(b, l // t),
        in_specs=[row(d), row(att0.shape[2]), row(ssm0.shape[2])] + [_const_spec(w.shape) for w in weights],
        out_specs=[row(d), row(cw), row(ckv), col(ckv), row(cw), row(IDX_HEADS * IDX_DIM), row(2 * LANES),
                   col(IDX_HEADS)],
        out_shape=[jax.ShapeDtypeStruct((b, l, d), F32), jax.ShapeDtypeStruct((b, l, cw), BF16),
                   jax.ShapeDtypeStruct((b, l, ckv), BF16), jax.ShapeDtypeStruct((b, ckv, l), BF16),
                   jax.ShapeDtypeStruct((b, l, cw), BF16), jax.ShapeDtypeStruct((b, l, IDX_HEADS * IDX_DIM), BF16),
                   jax.ShapeDtypeStruct((b, l, 2 * LANES), BF16), jax.ShapeDtypeStruct((b, IDX_HEADS, l), F32)],
        compiler_params=_params("arbitrary", "arbitrary"),
        name="mid",
    )(x, att0, ssm0, *weights)


def _dsa_kernel(q_ref, qi_ref, wt_ref, sg_ref, k_ref, vt_ref, ki_ref, tab_ref, lb_ref, o_ref,
                sc_ref, best_ref, x_ref, acc_ref, *, topk):
    i = pl.program_id(1)
    ck = KEY_CHUNK
    per = ck // BLOCK
    nch = (i + per) // per
    t_row = i * BLOCK + lax.broadcasted_iota(I32, (1, LANES), 1)
    kiota = lax.broadcasted_iota(I32, (ck, LANES), 0)

    def chunk_off(c):
        return pl.multiple_of(c * ck, ck)

    qi = qi_ref[0]
    qi_stack = [jnp.concatenate([qi[:, (2 * s) * LANES:(2 * s + 1) * LANES],
                                 qi[:, (2 * s + 1) * LANES:(2 * s + 2) * LANES]], axis=0) for s in range(2)]
    wt = wt_ref[0]

    def score_chunk(c, masked):
        off = chunk_off(c)
        sc = jnp.zeros((ck, LANES), F32)
        for a in range(2):
            kk = ki_ref[0, pl.ds(off, ck), a * LANES:(a + 1) * LANES]
            for s in range(2):
                r = _mm_nt(kk, qi_stack[s])
                for j in range(2):
                    hd = 2 * (2 * s + j) + a
                    sc = sc + jnp.maximum(r[:, j * LANES:(j + 1) * LANES], 0.0) * wt[hd:hd + 1, :]
        if masked:
            sc = jnp.where(off + kiota <= t_row, sc, NEG_INF)
        sc_ref[pl.ds(off, ck), :] = sc
        return _fold(sc, jnp.maximum)

    def exchange(v, a, b):
        v[a], v[b] = jnp.maximum(v[a], v[b]), jnp.minimum(v[a], v[b])

    def sort_desc(v):
        n, k = len(v), 2
        while k <= n:
            j = k // 2
            while j >= 1:
                for a in range(n):
                    b = a ^ j
                    if b > a:
                        exchange(v, *((a, b) if (a & k) == 0 else (b, a)))
                j //= 2
            k *= 2

    def merge_top(best, blk):
        n = len(best)
        v = [jnp.maximum(best[r], blk[n - 1 - r]) for r in range(n)]
        j = n // 2
        while j >= 1:
            for a in range(n):
                if a ^ j > a:
                    exchange(v, a, a ^ j)
            j //= 2
        return v

    crow = STREAMS * CAND * SUBLANES

    def cand_step(row0):
        blk_all = sc_ref[pl.ds(pl.multiple_of(row0, crow), crow), :]
        for st in range(STREAMS):
            blk = [blk_all[(STREAMS * r + st) * SUBLANES:(STREAMS * r + st + 1) * SUBLANES] for r in range(CAND)]
            sort_desc(blk)
            base = st * CAND * SUBLANES
            best = [best_ref[base + r * SUBLANES:base + (r + 1) * SUBLANES, :] for r in range(CAND)]
            for r, x in enumerate(merge_top(best, blk)):
                best_ref[base + r * SUBLANES:base + (r + 1) * SUBLANES, :] = x

    def score_body(c, mx):
        return jnp.maximum(jnp.maximum(mx, score_chunk(2 * c, False)), score_chunk(2 * c + 1, False))

    below = nch - 1
    smax = lax.fori_loop(0, below // 2, score_body, jnp.full((SUBLANES, LANES), NEG_INF, F32))
    smax = lax.cond(below % 2 == 1, lambda: jnp.maximum(smax, score_chunk(below - 1, False)), lambda: smax)
    smax = jnp.max(jnp.maximum(smax, score_chunk(nch - 1, True)), axis=0, keepdims=True)

    def count(*preds):
        rows = COUNT_ROWS
        sub = ck // rows

        def body(c, accs):
            out = []
            for u in range(sub):
                off = pl.multiple_of(c * ck + u * rows, rows)
                s = sc_ref[pl.ds(off, rows), :]
                for n, pred in enumerate(preds):
                    ind = pred(s, off).astype(I32)
                    out.append(accs[u * len(preds) + n]
                               + jnp.sum(ind.reshape(rows // SUBLANES, SUBLANES, LANES), axis=0))
            return tuple(out)

        accs = lax.fori_loop(0, nch, body, tuple(jnp.zeros((SUBLANES, LANES), I32) for _ in range(sub * len(preds))))
        res = [jnp.sum(sum(accs[n::len(preds)]), axis=0, keepdims=True) for n in range(len(preds))]
        return res[0] if len(preds) == 1 else res

    def key_value(k):
        return pltpu.bitcast(k ^ ((k >> 31) & MAGNITUDE_BITS), F32)

    def count_ge(k):
        thr = key_value(k)
        return count(lambda s, off: s >= thr)

    def full(v):
        return jnp.full((1, LANES), v, I32)

    def bisect(_, st):
        lo, hi, c_lo, c_hi = st
        mid = (lo >> 1) + (hi >> 1) + (lo & hi & 1)
        c = count_ge(mid)
        ge = c >= topk
        return jnp.where(ge, mid, lo), jnp.where(ge, hi, mid), jnp.where(ge, c, c_lo), jnp.where(ge, c_hi, c)

    searching = (i + 1) * BLOCK > topk

    def float_key(x):
        bits = pltpu.bitcast(x, I32)
        return bits ^ ((bits >> 31) & MAGNITUDE_BITS)

    def search():
        k_lo = float_key(smax * 2.0 ** -BRACKET_BINADES)
        c = count_ge(k_lo)
        ok = (smax > 0.0) & (c >= topk)
        trips = jnp.where(jnp.min(jnp.where(ok, 1, 0)) > 0, BRACKET_STEPS, FULL_STEPS)
        st = (jnp.where(ok, k_lo, KEY_NEG_INF), float_key(smax) + 1, jnp.where(ok, c, nch * ck), full(0))
        out = lax.fori_loop(0, trips, bisect, st)
        return out[0], out[2], out[3]

    def cand_search():
        best_ref[...] = jnp.full(best_ref.shape, -jnp.inf, F32)

        def cand_body(c, _):
            cand_step(c * crow)
            return 0

        lax.fori_loop(0, (i + crow // BLOCK) // (crow // BLOCK), cand_body, 0)

        def all_sublanes(x):
            for shift in (4, 2, 1):
                x = x + pltpu.roll(x, shift, axis=0)
            return x

        def count_cand(k):
            thr = key_value(k)[None]
            parts = [jnp.sum((best_ref[r:r + COUNT_ROWS, :].reshape(COUNT_ROWS // SUBLANES, SUBLANES, LANES)
                              >= thr).astype(I32), axis=0) for r in range(0, crow, COUNT_ROWS)]
            return all_sublanes(sum(parts))

        def step(_, st):
            lo, hi = st
            mid = (lo >> 1) + (hi >> 1) + (lo & hi & 1)
            take = ~((count_cand(mid) - topk) >> 31)
            return (mid & take) | (lo & ~take), (hi & take) | (mid & ~take)

        smax8 = jnp.broadcast_to(smax, (SUBLANES, LANES))
        k_lo = float_key(smax8 * 2.0 ** -BRACKET_BINADES)
        ok = (smax8 > 0.0) & (count_cand(k_lo) >= topk)
        trips = jnp.where(jnp.min(jnp.where(ok, 1, 0)) > 0, BRACKET_STEPS, FULL_STEPS)
        vk8, _ = lax.fori_loop(0, trips, step, (jnp.where(ok, k_lo, KEY_NEG_INF), float_key(smax8) + 1))
        vk = vk8[0:1]
        thr = key_value(vk)
        above_cand = jnp.sum((best_ref[...] > thr).astype(I32), axis=0, keepdims=True)
        c_ge, c_gt = count(lambda s, off: s >= thr, lambda s, off: s > thr)
        complete = jnp.min(jnp.where(c_gt == above_cand, 1, 0)) > 0
        return lax.cond(complete, lambda: (vk, c_ge, c_gt), search)

    vkey, c_lo, c_hi = lax.cond(searching, cand_search, lambda: (full(KEY_NEG_INF), full(topk), full(0)))
    vthr = key_value(vkey)
    need = topk - c_hi
    ties = c_lo - c_hi

    def tie_search():
        def split_step():
            nxt = vkey + 1
            nxt = jnp.where((nxt > 0) & (nxt < KEY_MIN_NORMAL), KEY_MIN_NORMAL, nxt)
            step = key_value(nxt) - vthr

            def split(_, st):
                fl, fh = st
                fm = 0.5 * (fl + fh)
                t = vthr + fm * step
                ge = count(lambda s, off: s >= t) >= topk
                return jnp.where(ge, fm, fl), jnp.where(ge, fh, fm)

            fl, _ = lax.fori_loop(0, SPLIT_STEPS, split, (jnp.zeros((1, LANES), F32), jnp.ones((1, LANES), F32)))
            t = vthr + fl * step
            return t, topk - count(lambda s, off: s > t)

        inside = jnp.max(ties - count(lambda s, off: s == vthr)) > 0
        thr, want = lax.cond(inside, split_step, lambda: (vthr, need))

        rr = lax.broadcasted_iota(I32, (BLOCK, BLOCK), 0)
        cc = lax.broadcasted_iota(I32, (BLOCK, BLOCK), 1)
        tril = jnp.where(cc <= rr, 1.0, 0.0).astype(BF16)
        want_f = want.astype(F32)

        def body(c, before):
            off = chunk_off(c)
            blocks = [sc_ref[pl.ds(pl.multiple_of(off + r * BLOCK, BLOCK), BLOCK), :] for r in range(per)]
            hits = [s == thr for s in blocks]
            ranks = [_mm(tril, jnp.where(h, 1.0, 0.0).astype(BF16)) for h in hits]
            for r in range(per):
                rank = ranks[r] + before
                sc_ref[pl.ds(pl.multiple_of(off + r * BLOCK, BLOCK), BLOCK), :] = jnp.where(
                    hits[r] & (rank > want_f), NEG_INF, blocks[r])
                before = rank[BLOCK - 1:BLOCK, :]
            return before

        lax.fori_loop(0, nch, body, jnp.zeros((1, LANES), F32))
        return thr

    any_tie = searching & (jnp.max(ties - need) > 0)
    vthr = lax.cond(any_tie, tie_search, lambda: vthr)

    def selection_mask(off):
        s = sc_ref[pl.ds(off, ck), :]
        s_idx = off + kiota
        sel = (s >= vthr) & (s_idx <= t_row)
        madd = jnp.where(sel, 0.0, NEG_INF)
        sc_ref[pl.ds(off, ck), :] = madd
        return madd

    q = q_ref[0]
    n_far = jnp.maximum((i - NEAR_BLOCKS + 1) // per, 0)
    hpg = C_HEADS // C_KV_HEADS
    npair = C_HEADS // 2
    q_pairs = [jnp.concatenate([q[:, (2 * j) * LANES:(2 * j + 1) * LANES],
                                q[:, (2 * j + 1) * LANES:(2 * j + 2) * LANES]], axis=0) for j in range(npair)]

    def bias_rows(hd, c):
        return jnp.concatenate([tab_ref[hd, jnp.clip(i - (c * per + r), 0, NEAR_BLOCKS)] for r in range(per)],
                               axis=0)

    def emit(hd, num, den):
        sl = slice(hd * LANES, (hd + 1) * LANES)
        o_ref[0, :, sl] = ((num / den).T * sg_ref[0, :, sl].astype(F32)).astype(BF16)

    def exact_attention():
        for g in range(C_KV_HEADS):
            def stage_body(near, g=g):
                def body(c, mx):
                    off = chunk_off(c)
                    madd = sc_ref[pl.ds(off, ck), :]
                    kc = k_ref[0, pl.ds(off, ck), g * LANES:(g + 1) * LANES]
                    out = []
                    for jj in range(hpg // 2):
                        lg = _mm_nt(kc, q_pairs[g * (hpg // 2) + jj])
                        for a in range(2):
                            hl = 2 * jj + a
                            x = lg[:, a * LANES:(a + 1) * LANES] + madd
                            if near:
                                x = x + bias_rows(hpg * g + hl, c)
                            x_ref[hl, pl.ds(off, ck), :] = x
                            out.append(jnp.maximum(mx[hl], _fold(x, jnp.maximum)))
                    return tuple(out)

                return body

            mx = tuple(jnp.full((SUBLANES, LANES), NEG_INF, F32) for _ in range(hpg))
            mx = lax.fori_loop(0, n_far, stage_body(False), mx)
            mx = lax.fori_loop(n_far, nch, stage_body(True), mx)
            m = [jnp.max(v, axis=0, keepdims=True) for v in mx]
            acc_ref[...] = jnp.zeros(acc_ref.shape, F32)

            def att_body(c, ls, g=g, m=m):
                off = chunk_off(c)
                vt = vt_ref[0, g * LANES:(g + 1) * LANES, pl.ds(off, ck)]
                out = []
                for jj in range(hpg // 2):
                    ps = []
                    for a in range(2):
                        hl = 2 * jj + a
                        p = jnp.exp(x_ref[hl, pl.ds(off, ck), :] - m[hl])
                        out.append(ls[hl] + _fold(p, jnp.add))
                        ps.append(p.astype(BF16))
                    acc_ref[jj] += _mm(vt, jnp.concatenate(ps, axis=1))
                return tuple(out)

            ls = lax.fori_loop(0, nch, att_body, tuple(jnp.zeros((SUBLANES, LANES), F32) for _ in range(hpg)))
            for hl in range(hpg):
                emit(hpg * g + hl, acc_ref[hl // 2, :, (hl % 2) * LANES:(hl % 2 + 1) * LANES],
                     jnp.sum(ls[hl], axis=0, keepdims=True))

    lb = lb_ref[...]
    acc_ref[...] = jnp.zeros(acc_ref.shape, F32)

    def stage(c, g, near):
        off = chunk_off(c)
        mb = (selection_mask(off) if g == 0 else sc_ref[pl.ds(off, ck), :]) - lb
        kc = k_ref[0, pl.ds(off, ck), g * LANES:(g + 1) * LANES]
        for jj in range(hpg // 2):
            lg = _mm_nt(kc, q_pairs[g * (hpg // 2) + jj])
            for a in range(2):
                hl = 2 * jj + a
                x = lg[:, a * LANES:(a + 1) * LANES] + mb
                if near:
                    x = x + bias_rows(hpg * g + hl, c)
                x_ref[hl, pl.ds(off, ck), :] = x

    def consume(c, g, ls):
        off = chunk_off(c)
        vt = vt_ref[0, g * LANES:(g + 1) * LANES, pl.ds(off, ck)]
        out = []
        for jj in range(hpg // 2):
            ps = []
            for a in range(2):
                hl = 2 * jj + a
                p = jnp.exp(x_ref[hl, pl.ds(off, ck), :])
                out.append(ls[hl] + _fold(p, jnp.add))
                ps.append(p.astype(BF16))
            acc_ref[g * (hpg // 2) + jj] += _mm(vt, jnp.concatenate(ps, axis=1))
        return tuple(out)

    dens = []
    for g in range(C_KV_HEADS):
        def step(near, g=g):
            def body(c, ls):
                out = consume(c, g, ls)
                stage(c + 1, g, near)
                return out

            return body

        stage(0, g, True)
        ls = tuple(jnp.zeros((SUBLANES, LANES), F32) for _ in range(hpg))
        split = jnp.maximum(n_far - 1, 0)
        ls = lax.fori_loop(0, split, step(False), ls)
        ls = lax.fori_loop(split, nch - 1, step(True), ls)
        ls = consume(nch - 1, g, ls)
        dens += [jnp.sum(v, axis=0, keepdims=True) for v in ls]
    in_range = jnp.min(functools.reduce(jnp.minimum, dens)) > SUM_FLOOR

    @pl.when(in_range)
    def _():
        for hd in range(C_HEADS):
            emit(hd, acc_ref[hd // 2, :, (hd % 2) * LANES:(hd % 2 + 1) * LANES], dens[hd])

    @pl.when(jnp.logical_not(in_range))
    def _():
        exact_attention()


def _dsa(q, qi, wt, sg, k, vt, ki2, tab, lb):
    b, l, cw = q.shape
    nb = l // BLOCK
    topk = min(TOPK_MAX, l // 4)

    def blk(n):
        return pl.BlockSpec((1, BLOCK, n), lambda bb, i: (bb, i, 0))

    def whole(s1, s2):
        return pl.BlockSpec((1, s1, s2), lambda bb, i: (bb, 0, 0), pipeline_mode=pl.Buffered(1))

    hpg = C_HEADS // C_KV_HEADS
    return pl.pallas_call(
        functools.partial(_dsa_kernel, topk=topk),
        grid=(b, nb),
        in_specs=[blk(cw), blk(qi.shape[2]), pl.BlockSpec((1, IDX_HEADS, BLOCK), lambda bb, i: (bb, 0, i)), blk(cw),
                  whole(l, k.shape[2]), whole(vt.shape[1], l), whole(l, ki2.shape[2]),
                  pl.BlockSpec(tab.shape, lambda bb, i: (0, 0, 0, 0), pipeline_mode=pl.Buffered(1)),
                  _const_spec(lb.shape)],
        out_specs=blk(cw),
        out_shape=jax.ShapeDtypeStruct((b, l, cw), BF16),
        scratch_shapes=[pltpu.VMEM((l, LANES), F32), pltpu.VMEM((STREAMS * CAND * SUBLANES, LANES), F32),
                        pltpu.VMEM((hpg, l, LANES), F32),
                        pltpu.VMEM((C_HEADS // 2, C_HEAD_DIM, 2 * LANES), F32)],
        compiler_params=_params("arbitrary", "arbitrary"),
        name="dsa",
    )(q, qi, wt, sg, k, vt, ki2, tab, lb)


def _out_kernel(h_ref, a_ref, w_ref, o_ref):
    o_ref[...] = h_ref[...] + _mm(a_ref[...], w_ref[...])


def _outproj(h2, a2, w):
    rows, d = h2.shape
    t = ROW_TILE
    return pl.pallas_call(
        _out_kernel,
        grid=(rows // t,),
        in_specs=[pl.BlockSpec((t, d), lambda i: (i, 0)), pl.BlockSpec((t, a2.shape[1]), lambda i: (i, 0)),
                  _const_spec(w.shape)],
        out_specs=pl.BlockSpec((t, d), lambda i: (i, 0)),
        out_shape=jax.ShapeDtypeStruct((rows, d), F32),
        compiler_params=_params("arbitrary"),
        name="outproj1",
    )(h2, a2, w)


def _bias_tables(rel_bias):
    nv = (NEAR_BLOCKS + 1) * BLOCK
    vec = rel_bias[_t5_bucket(jnp.arange(nv, dtype=I32))].astype(F32).T

    nh = vec.shape[0]
    vecp = jnp.concatenate([jnp.broadcast_to(vec[:, :1], (nh, BLOCK - 1)), vec], axis=1)

    def toeplitz(g, rows, cols):
        w = rows + cols
        g2 = jnp.concatenate([g[..., rows - 1:rows - 1 + cols], g[..., :1], g[..., :rows - 1]], axis=-1)
        flat = jnp.tile(g2, (1,) * (g.ndim - 1) + (rows,))[..., :rows * (w - 1)]
        return flat.reshape(g.shape[:-1] + (rows, w - 1))[..., :cols]

    bias0 = jnp.transpose(toeplitz(vecp[:, :3 * BLOCK - 1], 2 * BLOCK, BLOCK), (0, 2, 1))
    r = vecp[:, :(NEAR_BLOCKS + 1) * BLOCK].reshape(nh, NEAR_BLOCKS + 1, BLOCK)
    wins = jnp.concatenate([r[:, :-1], r[:, 1:, :BLOCK - 1]], axis=2)
    tab = toeplitz(wins, BLOCK, BLOCK) - rel_bias[NUM_BUCKETS - 1].astype(F32)[:, None, None, None]
    tab = jnp.concatenate([tab, jnp.zeros((tab.shape[0], 1, BLOCK, BLOCK), F32)], axis=1)
    return bias0, tab


def kernel(x, rel_bias, norm_g, ev_w_in, ev_w_out, ev_q_norm_g, ev_k_norm_g, ev_sinks, ev_ssm_log_dt, ev_ssm_a_re,
           ev_ssm_a_im, ev_ssm_b_re, ev_ssm_b_im, ev_ssm_c_re, ev_ssm_c_im, ev_ssm_d, ev_glu_w, ev_glu_b, od_w_in,
           od_w_out, od_q_norm_g, od_k_norm_g):
    b, l, d = x.shape
    assert l % KEY_CHUNK == 0 and l % ROW_TILE == 0
    assert (NEAR_BLOCKS - 1) * BLOCK + 1 >= 16 * 64 ** (15 / 16) + 1
    bias0, tab = _bias_tables(rel_bias)

    w0 = ev_w_in[0]
    hd = A_HEAD_DIM
    o0 = np.cumsum([0, A_WIDTH, A_KV_HEADS * hd, A_KV_HEADS * hd, A_WIDTH, A_WIDTH, A_WIDTH])
    wq0, wk0, wv0, wga, wu, wgb = (w0[:, o0[n]:o0[n + 1]] for n in range(6))
    z = jnp.zeros((d, hd), w0.dtype)

    def variants(w):
        return jnp.concatenate([c for g in range(A_KV_HEADS) for c in (w[:, g * hd:(g + 1) * hd], z, z,
                                                                       w[:, g * hd:(g + 1) * hd])], axis=1)

    w0x = jnp.concatenate([wq0, variants(wk0), variants(wv0), wga, wu, wgb], axis=1).astype(BF16)
    qg2 = jnp.tile(ev_q_norm_g[0], 2)[None, :]
    kg2 = jnp.tile(ev_k_norm_g[0], 2)[None, :]
    q0, k0, v0, sga, u, sgb = _proj0(x.reshape(b * l, d), norm_g[0][None, :], w0x, qg2, kg2)
    shp = lambda a: a.reshape(b, l, a.shape[-1])
    sinks = jnp.broadcast_to(ev_sinks[0][:, None], (A_HEADS, LANES)).astype(F32)
    att0 = _attn0(shp(q0), shp(k0), shp(v0), shp(sga), bias0, sinks)
    bmat, cre, cim, sc = _s5_prep(ev_ssm_log_dt[0], ev_ssm_a_re[0], ev_ssm_a_im[0], ev_ssm_b_re[0], ev_ssm_b_im[0],
                                  ev_ssm_c_re[0], ev_ssm_c_im[0])
    ssm0 = _ssm(shp(u), shp(sgb), bmat, cre, cim, sc, ev_ssm_d[0].reshape(1, -1), ev_glu_w[0].astype(BF16),
                ev_glu_b[0][None, :])

    w1 = od_w_in[0]
    cw = C_HEADS * C_HEAD_DIM
    ckv = C_KV_HEADS * C_HEAD_DIM
    o = np.cumsum([0, cw, ckv, ckv, cw, IDX_HEADS * IDX_DIM, IDX_DIM, IDX_HEADS])
    wq, wk, wv, wg, wqi, wki, ww = (w1[:, o[n]:o[n + 1]] for n in range(7))
    zki = jnp.zeros((d, LANES - IDX_DIM), w1.dtype)
    wki2 = jnp.concatenate([wki, zki, zki, wki], axis=1)
    bf = lambda a: a.astype(BF16)
    h1, q1, k1, vt1, sg1, qi1, ki2, wt1 = _mid(
        x, att0, ssm0, bf(ev_w_out[0]), norm_g[1][None, :], bf(wq), bf(wk), bf(wv.T), bf(wg), bf(wqi), bf(wki2),
        bf(ww.T), od_q_norm_g[0][None, :], od_k_norm_g[0][None, :])
    lb = (BF16_SLACK * C_HEAD_DIM ** 0.5 * jnp.max(jnp.abs(od_q_norm_g[0])) * jnp.max(jnp.abs(od_k_norm_g[0]))
          + jnp.max(tab))
    att1 = _dsa(q1, qi1, wt1, sg1, k1, vt1, ki2, tab, jnp.full((1, LANES), lb, F32))
    out = _outproj(h1.reshape(b * l, d), att1.reshape(b * l, cw), bf(od_w_out[0]))
    return out.reshape(b, l, d)
```

```python
import functools
import math

import jax
import jax.numpy as jnp
import numpy as np
from jax import lax
from jax.experimental import pallas as pl
from jax.experimental.pallas import tpu as pltpu

F32 = jnp.float32
BF16 = jnp.bfloat16
I32 = jnp.int32

LANES = 128
SUBLANES = 8
VMEM_LIMIT = 56 * 1024 * 1024

BLOCK = 128
WINDOW = 128
A_HEADS = 8
A_HEAD_DIM = 64
A_KV_HEADS = 2
A_WIDTH = A_HEADS * A_HEAD_DIM
SSM_GROUP = 16
SSM_STATE = 64
C_HEADS = 8
C_HEAD_DIM = 128
C_KV_HEADS = 2
IDX_HEADS = 8
IDX_DIM = 64
TOPK_MAX = 256
NUM_BUCKETS = 32
REL_MAX_DIST = 1024
EPS = 1e-6
NEG_INF = -1e30
INT_MIN = -(2 ** 31)
KEY_MIN_NORMAL = 0x00800000
KEY_POS_INF = 0x7F800000
KEY_NEG_INF = INT_MIN + 0x007FFFFF
MAGNITUDE_BITS = 0x7FFFFFFF
MANTISSA_BITS = 23
BRACKET_BINADES = 3
BRACKET_STEPS = math.ceil(math.log2(BRACKET_BINADES * 2 ** MANTISSA_BITS + 1))
FULL_STEPS = math.ceil(math.log2(2 * KEY_POS_INF + 2))
SPLIT_STEPS = 26
BF16_SLACK = 1.02

ROW_TILE = 1024
KEY_CHUNK = 1024
NEAR_BLOCKS = 8
FOLD_CHAINS = 8
COUNT_ROWS = 512
SUM_FLOOR = 1e-30
CAND = 32
STREAMS = 2
NT_DIMS = (((1,), (1,)), ((), ()))


def _t5_bucket(dist):
    n = jnp.maximum(dist, 0)
    max_exact = NUM_BUCKETS // 2
    nf = jnp.maximum(n, 1).astype(F32)
    large = max_exact + (jnp.log(nf / max_exact) / math.log(REL_MAX_DIST / max_exact)
                         * (NUM_BUCKETS - max_exact)).astype(I32)
    large = jnp.minimum(large, NUM_BUCKETS - 1)
    return jnp.where(n < max_exact, n, large)


def _silu(x):
    return x * jax.nn.sigmoid(x)


def _rms(x, g):
    ms = jnp.mean(x * x, axis=-1, keepdims=True)
    return x * lax.rsqrt(ms + EPS) * g


def _mm(a, b):
    return jnp.dot(a, b, preferred_element_type=F32)


def _mm_nt(a, b):
    return lax.dot_general(a, b, NT_DIMS, preferred_element_type=F32)


def _fold(x, op):
    n = x.shape[0] // SUBLANES
    chains = min(FOLD_CHAINS, n)
    accs = [x[r * SUBLANES:(r + 1) * SUBLANES] for r in range(chains)]
    for r in range(chains, n):
        accs[r % chains] = op(accs[r % chains], x[r * SUBLANES:(r + 1) * SUBLANES])
    while len(accs) > 1:
        accs = [op(a, b) for a, b in zip(accs[::2], accs[1::2])] + accs[len(accs) & ~1:]
    return accs[0]


def _params(*sem):
    return pltpu.CompilerParams(dimension_semantics=sem, vmem_limit_bytes=VMEM_LIMIT)


def _const_spec(shape):
    zeros = (0,) * len(shape)
    return pl.BlockSpec(shape, lambda *_: zeros)


def _proj0_kernel(x_ref, g_ref, w_ref, qg_ref, kg_ref, q_ref, k_ref, v_ref, sga_ref, u_ref, sgb_ref):
    hn = _rms(x_ref[...], g_ref[...]).astype(BF16)
    lo = lax.broadcasted_iota(I32, (1, LANES), 1) < A_HEAD_DIM

    def mm(n):
        return _mm(hn, w_ref[:, n * A_WIDTH:(n + 1) * A_WIDTH])

    def segnorm(x, g2):
        sq = x * x
        s_lo = jnp.sum(jnp.where(lo, sq, 0.0), axis=-1, keepdims=True)
        s_hi = jnp.sum(jnp.where(lo, 0.0, sq), axis=-1, keepdims=True)
        inv = jnp.where(lo, lax.rsqrt(s_lo / A_HEAD_DIM + EPS), lax.rsqrt(s_hi / A_HEAD_DIM + EPS))
        return x * inv * g2

    q, k = mm(0), mm(1)
    for p in range(A_WIDTH // LANES):
        sl = slice(p * LANES, (p + 1) * LANES)
        q_ref[:, sl] = (segnorm(q[:, sl], qg_ref[...]) * (A_HEAD_DIM ** -0.5)).astype(BF16)
        k_ref[:, sl] = segnorm(k[:, sl], kg_ref[...]).astype(BF16)
    v_ref[...] = mm(2).astype(BF16)
    sga_ref[...] = _silu(mm(3)).astype(BF16)
    u_ref[...] = mm(4)
    sgb_ref[...] = _silu(mm(5)).astype(BF16)


def _proj0(x2, g, w, qg2, kg2):
    rows, d = x2.shape
    t = ROW_TILE

    def row(n):
        return pl.BlockSpec((t, n), lambda i: (i, 0))

    n = A_WIDTH
    return pl.pallas_call(
        _proj0_kernel,
        grid=(rows // t,),
        in_specs=[row(d), _const_spec((1, d)), _const_spec(w.shape), _const_spec(qg2.shape), _const_spec(kg2.shape)],
        out_specs=[row(n)] * 6,
        out_shape=[jax.ShapeDtypeStruct((rows, n), BF16), jax.ShapeDtypeStruct((rows, n), BF16),
                   jax.ShapeDtypeStruct((rows, n), BF16), jax.ShapeDtypeStruct((rows, n), BF16),
                   jax.ShapeDtypeStruct((rows, n), F32), jax.ShapeDtypeStruct((rows, n), BF16)],
        compiler_params=_params("arbitrary"),
        name="proj0",
    )(x2, g, w, qg2, kg2)


def _attn0_kernel(q_ref, kc_ref, kp_ref, vc_ref, vp_ref, sga_ref, bias_ref, sink_ref, ones_ref, o_ref):
    i = pl.program_id(1)
    kb = jnp.concatenate([kp_ref[0], kc_ref[0]], axis=0)
    vb = jnp.concatenate([vp_ref[0], vc_ref[0]], axis=0)

    def variant(x, g, a):
        n = 2 * g + a
        return x[:, n * LANES:(n + 1) * LANES]

    row = lax.broadcasted_iota(I32, (BLOCK, 2 * BLOCK), 0)
    col = lax.broadcasted_iota(I32, (BLOCK, 2 * BLOCK), 1)
    d = row + BLOCK - col
    mask = (d >= 0) & (d < WINDOW) & ((i > 0) | (col >= BLOCK))

    lgs, sinks = [], []
    for p in range(A_HEADS // 2):
        qp = q_ref[0, :, p * LANES:(p + 1) * LANES]
        for a in range(2):
            h = 2 * p + a
            lgs.append(jnp.where(mask, _mm_nt(qp, variant(kb, p // 2, a)) + bias_ref[h], NEG_INF))
            sinks.append(jnp.broadcast_to(sink_ref[h:h + 1, 0:1], (BLOCK, 1)))
    lg = jnp.concatenate(lgs, axis=0)
    sink = jnp.concatenate(sinks, axis=0)
    m = jnp.maximum(jnp.max(lg, axis=-1, keepdims=True), sink)
    e = jnp.exp(lg - m).astype(BF16)
    inv = 1.0 / (_mm(e, ones_ref[...]) + jnp.exp(sink - m))
    for p in range(A_HEADS // 2):
        sl = slice(p * LANES, (p + 1) * LANES)
        acc = jnp.zeros((BLOCK, LANES), F32)
        for a in range(2):
            h = 2 * p + a
            hs = slice(h * BLOCK, (h + 1) * BLOCK)
            acc = acc + _mm(e[hs], variant(vb, p // 2, a)) * inv[hs]
        o_ref[0, :, sl] = (acc * sga_ref[0, :, sl].astype(F32)).astype(BF16)


def _attn0(q, k, v, sga, bias0, sinks):
    b, l, _ = q.shape
    nb = l // BLOCK
    ones = jnp.ones((2 * BLOCK, LANES), BF16)

    def cur(n):
        return pl.BlockSpec((1, BLOCK, n), lambda bb, i: (bb, i, 0))

    def prev(n):
        return pl.BlockSpec((1, BLOCK, n), lambda bb, i: (bb, jnp.maximum(i - 1, 0), 0))

    return pl.pallas_call(
        _attn0_kernel,
        grid=(b, nb),
        in_specs=[cur(512), cur(512), prev(512), cur(512), prev(512), cur(512),
                  _const_spec(bias0.shape), _const_spec(sinks.shape), _const_spec(ones.shape)],
        out_specs=cur(512),
        out_shape=jax.ShapeDtypeStruct((b, l, 512), BF16),
        compiler_params=_params("arbitrary", "arbitrary"),
        name="attn0",
    )(q, k, k, v, v, sga, bias0, sinks, ones)


def _ssm_kernel(u_ref, sgb_ref, bmat_ref, cre_ref, cim_ref, sc_ref, d_ref, gw_ref, gb_ref, o_ref, xre_ref, xim_ref):
    t = u_ref.shape[1]
    nq = bmat_ref.shape[0]
    half = bmat_ref.shape[2] // 2

    @pl.when(pl.program_id(1) == 0)
    def _():
        xre_ref[0:SUBLANES, :] = jnp.zeros((SUBLANES, xre_ref.shape[1]), F32)
        xim_ref[0:SUBLANES, :] = jnp.zeros((SUBLANES, xim_ref.shape[1]), F32)

    u = u_ref[0]
    ub = u.astype(BF16)
    for q in range(nq):
        bu = _mm(ub[:, q * LANES:(q + 1) * LANES], bmat_ref[q])
        xre_ref[SUBLANES:, q * half:(q + 1) * half] = bu[:, :half]
        xim_ref[SUBLANES:, q * half:(q + 1) * half] = bu[:, half:]

    def scan(r, _):
        base = pl.multiple_of(SUBLANES + r * SUBLANES, SUBLANES)
        xr = xre_ref[pl.ds(base, SUBLANES), :]
        xi = xim_ref[pl.ds(base, SUBLANES), :]
        for s, k in enumerate((1, 2, 4)):
            ar = sc_ref[2 * s]
            ai = sc_ref[2 * s + 1]
            sr = pltpu.roll(xr, k, axis=0)
            si = pltpu.roll(xi, k, axis=0)
            xr, xi = xr + ar * sr - ai * si, xi + ar * si + ai * sr
        cr = xre_ref[pl.ds(base - 1, 1), :]
        ci = xim_ref[pl.ds(base - 1, 1), :]
        pr = sc_ref[6]
        pi = sc_ref[7]
        xre_ref[pl.ds(base, SUBLANES), :] = xr + pr * cr - pi * ci
        xim_ref[pl.ds(base, SUBLANES), :] = xi + pr * ci + pi * cr
        return 0

    lax.fori_loop(0, t // SUBLANES, scan, 0, unroll=2)
    xre_ref[0:SUBLANES, :] = xre_ref[t:t + SUBLANES, :]
    xim_ref[0:SUBLANES, :] = xim_ref[t:t + SUBLANES, :]

    ys = []
    for q in range(nq):
        xr = xre_ref[SUBLANES:, q * half:(q + 1) * half].astype(BF16)
        xi = xim_ref[SUBLANES:, q * half:(q + 1) * half].astype(BF16)
        ys.append(_mm(xr, cre_ref[q]) + _mm(xi, cim_ref[q]))
    y = jnp.concatenate(ys, axis=1) + d_ref[...] * u
    y = jax.nn.gelu(y).astype(BF16)
    hh = _mm(y, gw_ref[...]) + gb_ref[...]
    w = hh.shape[1] // 2
    o_ref[0] = (hh[:, :w] * jax.nn.sigmoid(hh[:, w:]) * sgb_ref[0].astype(F32)).astype(BF16)


def _ssm(u, sgb, bmat, cre, cim, sc, dskip, gw, gb):
    b, l, w = u.shape
    t = ROW_TILE
    ns = sc.shape[-1]

    def row(n):
        return pl.BlockSpec((1, t, n), lambda bb, i: (bb, i, 0))

    return pl.pallas_call(
        _ssm_kernel,
        grid=(b, l // t),
        in_specs=[row(w), row(w), _const_spec(bmat.shape), _const_spec(cre.shape), _const_spec(cim.shape),
                  _const_spec(sc.shape), _const_spec(dskip.shape), _const_spec(gw.shape), _const_spec(gb.shape)],
        out_specs=row(w),
        out_shape=jax.ShapeDtypeStruct((b, l, w), BF16),
        scratch_shapes=[pltpu.VMEM((SUBLANES + t, ns), F32), pltpu.VMEM((SUBLANES + t, ns), F32)],
        compiler_params=_params("arbitrary", "arbitrary"),
        name="ssm",
    )(u, sgb, bmat, cre, cim, sc, dskip, gw, gb)


def _s5_prep(log_dt, a_re, a_im, b_re, b_im, c_re, c_im):
    g, p = a_re.shape
    h = b_re.shape[-1]
    gl = LANES // h
    nq = g // gl
    dt = jnp.exp(log_dt)[:, None]
    mag = jnp.exp(a_re * dt)
    ang = a_im * dt
    ab_re = mag * jnp.cos(ang)
    ab_im = mag * jnp.sin(ang)
    den = a_re * a_re + a_im * a_im
    n_re = ab_re - 1.0
    n_im = ab_im
    f_re = (n_re * a_re + n_im * a_im) / den
    f_im = (n_im * a_re - n_re * a_im) / den
    bb_re = f_re[..., None] * b_re - f_im[..., None] * b_im
    bb_im = f_re[..., None] * b_im + f_im[..., None] * b_re
    eye = jnp.eye(gl, dtype=F32)

    def bdiag_in(m):
        m = m.reshape(nq, gl, p, h)
        return jnp.einsum('qgph,gk->qghkp', m, eye).reshape(nq, gl * h, gl * p)

    def bdiag_out(m):
        m = m.reshape(nq, gl, h, p)
        return jnp.einsum('qghp,gk->qgpkh', m, eye).reshape(nq, gl * p, gl * h)

    bmat = jnp.concatenate([bdiag_in(bb_re), bdiag_in(bb_im)], axis=2).astype(BF16)
    cre = bdiag_out(c_re).astype(BF16)
    cim = bdiag_out(-c_im).astype(BF16)

    pw = [(ab_re.reshape(-1), ab_im.reshape(-1))]
    for _ in range(SUBLANES - 1):
        pr, pi = pw[-1]
        pw.append((pr * pw[0][0] - pi * pw[0][1], pr * pw[0][1] + pi * pw[0][0]))
    rows = jnp.arange(SUBLANES)[:, None]
    sc = []
    for k in (1, 2, 4):
        sc.append(jnp.where(rows >= k, pw[k - 1][0][None, :], 0.0))
        sc.append(jnp.where(rows >= k, pw[k - 1][1][None, :], 0.0))
    sc.append(jnp.stack([pw[r][0] for r in range(SUBLANES)]))
    sc.append(jnp.stack([pw[r][1] for r in range(SUBLANES)]))
    return bmat, cre, cim, jnp.stack(sc).astype(F32)


def _mid_kernel(x_ref, a_ref, s_ref, wo_ref, g_ref, wq_ref, wk_ref, wvt_ref, wg_ref, wqi_ref, wki_ref, wwt_ref,
                qg_ref, kg_ref, h_ref, q_ref, k_ref, vt_ref, sg_ref, qi_ref, ki_ref, wt_ref):
    aw = a_ref.shape[2]
    h = x_ref[0] + _mm(a_ref[0], wo_ref[0:aw, :]) + _mm(s_ref[0], wo_ref[aw:, :])
    h_ref[0] = h
    hn = _rms(h, g_ref[...]).astype(BF16)
    qf = _mm(hn, wq_ref[...])
    for hd in range(C_HEADS):
        sl = slice(hd * C_HEAD_DIM, (hd + 1) * C_HEAD_DIM)
        q_ref[0, :, sl] = (_rms(qf[:, sl], qg_ref[...]) * (C_HEAD_DIM ** -0.5)).astype(BF16)
    kf = _mm(hn, wk_ref[...])
    for hd in range(C_KV_HEADS):
        sl = slice(hd * C_HEAD_DIM, (hd + 1) * C_HEAD_DIM)
        k_ref[0, :, sl] = _rms(kf[:, sl], kg_ref[...]).astype(BF16)
    vt_ref[0] = _mm_nt(wvt_ref[...], hn).astype(BF16)
    sg_ref[0] = _silu(_mm(hn, wg_ref[...])).astype(BF16)
    qi_ref[0] = _mm(hn, wqi_ref[...]).astype(BF16)
    ki_ref[0] = _mm(hn, wki_ref[...]).astype(BF16)
    wt_ref[0] = _mm_nt(wwt_ref[...], hn) * ((IDX_HEADS ** -0.5) * (IDX_DIM ** -0.5))


def _mid(x, att0, ssm0, wo, g, wq, wk, wvt, wg, wqi, wki2, wwt, qg, kg):
    b, l, d = x.shape
    t = ROW_TILE

    def row(n):
        return pl.BlockSpec((1, t, n), lambda bb, i: (bb, i, 0))

    def col(n):
        return pl.BlockSpec((1, n, t), lambda bb, i: (bb, 0, i))

    weights = [wo, g, wq, wk, wvt, wg, wqi, wki2, wwt, qg, kg]
    cw = C_HEADS * C_HEAD_DIM
    ckv = C_KV_HEADS * C_HEAD_DIM
    return pl.pallas_call(
        _mid_kernel,
        grid=(b, l // t),
        in_specs=[row(d), row(att0.shape[2]), row(ssm0.shape[2])] + [_const_spec(w.shape) for w in weights],
        out_specs=[row(d), row(cw), row(ckv), col(ckv), row(cw), row(IDX_HEADS * IDX_DIM), row(2 * LANES),
                   col(IDX_HEADS)],
        out_shape=[jax.ShapeDtypeStruct((b, l, d), F32), jax.ShapeDtypeStruct((b, l, cw), BF16),
                   jax.ShapeDtypeStruct((b, l, ckv), BF16), jax.ShapeDtypeStruct((b, ckv, l), BF16),
                   jax.ShapeDtypeStruct((b, l, cw), BF16), jax.ShapeDtypeStruct((b, l, IDX_HEADS * IDX_DIM), BF16),
                   jax.ShapeDtypeStruct((b, l, 2 * LANES), BF16), jax.ShapeDtypeStruct((b, IDX_HEADS, l), F32)],
        compiler_params=_params("arbitrary", "arbitrary"),
        name="mid",
    )(x, att0, ssm0, *weights)


def _dsa_kernel(q_ref, qi_ref, wt_ref, sg_ref, k_ref, vt_ref, ki_ref, tab_ref, lb_ref, o_ref,
                sc_ref, best_ref, x_ref, acc_ref, *, topk):
    i = pl.program_id(1)
    ck = KEY_CHUNK
    per = ck // BLOCK
    nch = (i + per) // per
    t_row = i * BLOCK + lax.broadcasted_iota(I32, (1, LANES), 1)
    kiota = lax.broadcasted_iota(I32, (ck, LANES), 0)

    def chunk_off(c):
        return pl.multiple_of(c * ck, ck)

    qi = qi_ref[0]
    qi_stack = [jnp.concatenate([qi[:, (2 * s) * LANES:(2 * s + 1) * LANES],
                                 qi[:, (2 * s + 1) * LANES:(2 * s + 2) * LANES]], axis=0) for s in range(2)]
    wt = wt_ref[0]

    def score_chunk(c, masked):
        off = chunk_off(c)
        sc = jnp.zeros((ck, LANES), F32)
        for a in range(2):
            kk = ki_ref[0, pl.ds(off, ck), a * LANES:(a + 1) * LANES]
            for s in range(2):
                r = _mm_nt(kk, qi_stack[s])
                for j in range(2):
                    hd = 2 * (2 * s + j) + a
                    sc = sc + jnp.maximum(r[:, j * LANES:(j + 1) * LANES], 0.0) * wt[hd:hd + 1, :]
        if masked:
            sc = jnp.where(off + kiota <= t_row, sc, NEG_INF)
        sc_ref[pl.ds(off, ck), :] = sc
        return _fold(sc, jnp.maximum)

    def exchange(v, a, b):
        v[a], v[b] = jnp.maximum(v[a], v[b]), jnp.minimum(v[a], v[b])

    def sort_desc(v):
        n, k = len(v), 2
        while k <= n:
            j = k // 2
            while j >= 1:
                for a in range(n):
                    b = a ^ j
                    if b > a:
                        exchange(v, *((a, b) if (a & k) == 0 else (b, a)))
                j //= 2
            k *= 2

    def merge_top(best, blk):
        n = len(best)
        v = [jnp.maximum(best[r], blk[n - 1 - r]) for r in range(n)]
        j = n // 2
        while j >= 1:
            for a in range(n):
                if a ^ j > a:
                    exchange(v, a, a ^ j)
            j //= 2
        return v

    crow = STREAMS * CAND * SUBLANES

    def cand_step(row0):
        blk_all = sc_ref[pl.ds(pl.multiple_of(row0, crow), crow), :]
        for st in range(STREAMS):
            blk = [blk_all[(STREAMS * r + st) * SUBLANES:(STREAMS * r + st + 1) * SUBLANES] for r in range(CAND)]
            sort_desc(blk)
            base = st * CAND * SUBLANES
            best = [best_ref[base + r * SUBLANES:base + (r + 1) * SUBLANES, :] for r in range(CAND)]
            for r, x in enumerate(merge_top(best, blk)):
                best_ref[base + r * SUBLANES:base + (r + 1) * SUBLANES, :] = x

    def score_body(c, mx):
        return jnp.maximum(jnp.maximum(mx, score_chunk(2 * c, False)), score_chunk(2 * c + 1, False))

    below = nch - 1
    smax = lax.fori_loop(0, below // 2, score_body, jnp.full((SUBLANES, LANES), NEG_INF, F32))
    smax = lax.cond(below % 2 == 1, lambda: jnp.maximum(smax, score_chunk(below - 1, False)), lambda: smax)
    smax = jnp.max(jnp.maximum(smax, score_chunk(nch - 1, True)), axis=0, keepdims=True)

    def count(*preds):
        rows = COUNT_ROWS
        sub = ck // rows

        def body(c, accs):
            out = []
            for u in range(sub):
                off = pl.multiple_of(c * ck + u * rows, rows)
                s = sc_ref[pl.ds(off, rows), :]
                for n, pred in enumerate(preds):
                    ind = pred(s, off).astype(I32)
                    out.append(accs[u * len(preds) + n]
                               + jnp.sum(ind.reshape(rows // SUBLANES, SUBLANES, LANES), axis=0))
            return tuple(out)

        accs = lax.fori_loop(0, nch, body, tuple(jnp.zeros((SUBLANES, LANES), I32) for _ in range(sub * len(preds))))
        res = [jnp.sum(sum(accs[n::len(preds)]), axis=0, keepdims=True) for n in range(len(preds))]
        return res[0] if len(preds) == 1 else res

    def key_value(k):
        return pltpu.bitcast(k ^ ((k >> 31) & MAGNITUDE_BITS), F32)

    def count_ge(k):
        thr = key_value(k)
        return count(lambda s, off: s >= thr)

    def full(v):
        return jnp.full((1, LANES), v, I32)

    def bisect(_, st):
        lo, hi, c_lo, c_hi = st
        mid = (lo >> 1) + (hi >> 1) + (lo & hi & 1)
        c = count_ge(mid)
        ge = c >= topk
        return jnp.where(ge, mid, lo), jnp.where(ge, hi, mid), jnp.where(ge, c, c_lo), jnp.where(ge, c_hi, c)

    searching = (i + 1) * BLOCK > topk

    def float_key(x):
        bits = pltpu.bitcast(x, I32)
        return bits ^ ((bits >> 31) & MAGNITUDE_BITS)

    def search():
        k_lo = float_key(smax * 2.0 ** -BRACKET_BINADES)
        c = count_ge(k_lo)
        ok = (smax > 0.0) & (c >= topk)
        trips = jnp.where(jnp.min(jnp.where(ok, 1, 0)) > 0, BRACKET_STEPS, FULL_STEPS)
        st = (jnp.where(ok, k_lo, KEY_NEG_INF), float_key(smax) + 1, jnp.where(ok, c, nch * ck), full(0))
        out = lax.fori_loop(0, trips, bisect, st)
        return out[0], out[2], out[3]

    def cand_search():
        best_ref[...] = jnp.full(best_ref.shape, -jnp.inf, F32)

        def cand_body(c, _):
            cand_step(c * crow)
            return 0

        lax.fori_loop(0, (i + crow // BLOCK) // (crow // BLOCK), cand_body, 0)

        def all_sublanes(x):
            for shift in (4, 2, 1):
                x = x + pltpu.roll(x, shift, axis=0)
            return x

        def count_cand(k):
            thr = key_value(k)[None]
            parts = [jnp.sum((best_ref[r:r + COUNT_ROWS, :].reshape(COUNT_ROWS // SUBLANES, SUBLANES, LANES)
                              >= thr).astype(I32), axis=0) for r in range(0, crow, COUNT_ROWS)]
            return all_sublanes(sum(parts))

        def step(_, st):
            lo, hi = st
            mid = (lo >> 1) + (hi >> 1) + (lo & hi & 1)
            take = ~((count_cand(mid) - topk) >> 31)
            return (mid & take) | (lo & ~take), (hi & take) | (mid & ~take)

        smax8 = jnp.broadcast_to(smax, (SUBLANES, LANES))
        k_lo = float_key(smax8 * 2.0 ** -BRACKET_BINADES)
        ok = (smax8 > 0.0) & (count_cand(k_lo) >= topk)
        trips = jnp.where(jnp.min(jnp.where(ok, 1, 0)) > 0, BRACKET_STEPS, FULL_STEPS)
        vk8, _ = lax.fori_loop(0, trips, step, (jnp.where(ok, k_lo, KEY_NEG_INF), float_key(smax8) + 1))
        vk = vk8[0:1]
        thr = key_value(vk)
        above_cand = jnp.sum((best_ref[...] > thr).astype(I32), axis=0, keepdims=True)
        c_ge, c_gt = count(lambda s, off: s >= thr, lambda s, off: s > thr)
        complete = jnp.min(jnp.where(c_gt == above_cand, 1, 0)) > 0
        return lax.cond(complete, lambda: (vk, c_ge, c_gt), search)

    vkey, c_lo, c_hi = lax.cond(searching, cand_search, lambda: (full(KEY_NEG_INF), full(topk), full(0)))
    vthr = key_value(vkey)
    need = topk - c_hi
    ties = c_lo - c_hi

    def tie_search():
        def split_step():
            nxt = vkey + 1
            nxt = jnp.where((nxt > 0) & (nxt < KEY_MIN_NORMAL), KEY_MIN_NORMAL, nxt)
            step = key_value(nxt) - vthr

            def split(_, st):
                fl, fh = st
                fm = 0.5 * (fl + fh)
                t = vthr + fm * step
                ge = count(lambda s, off: s >= t) >= topk
                return jnp.where(ge, fm, fl), jnp.where(ge, fh, fm)

            fl, _ = lax.fori_loop(0, SPLIT_STEPS, split, (jnp.zeros((1, LANES), F32), jnp.ones((1, LANES), F32)))
            t = vthr + fl * step
            return t, topk - count(lambda s, off: s > t)

        inside = jnp.max(ties - count(lambda s, off: s == vthr)) > 0
        thr, want = lax.cond(inside, split_step, lambda: (vthr, need))

        rr = lax.broadcasted_iota(I32, (BLOCK, BLOCK), 0)
        cc = lax.broadcasted_iota(I32, (BLOCK, BLOCK), 1)
        tril = jnp.where(cc <= rr, 1.0, 0.0).astype(BF16)
        want_f = want.astype(F32)

        def body(c, before):
            off = chunk_off(c)
            blocks = [sc_ref[pl.ds(pl.multiple_of(off + r * BLOCK, BLOCK), BLOCK), :] for r in range(per)]
            hits = [s == thr for s in blocks]
            ranks = [_mm(tril, jnp.where(h, 1.0, 0.0).astype(BF16)) for h in hits]
            for r in range(per):
                rank = ranks[r] + before
                sc_ref[pl.ds(pl.multiple_of(off + r * BLOCK, BLOCK), BLOCK), :] = jnp.where(
                    hits[r] & (rank > want_f), NEG_INF, blocks[r])
                before = rank[BLOCK - 1:BLOCK, :]
            return before

        lax.fori_loop(0, nch, body, jnp.zeros((1, LANES), F32))
        return thr

    any_tie = searching & (jnp.max(ties - need) > 0)
    vthr = lax.cond(any_tie, tie_search, lambda: vthr)

    def selection_mask(off):
        s = sc_ref[pl.ds(off, ck), :]
        s_idx = off + kiota
        sel = (s >= vthr) & (s_idx <= t_row)
        madd = jnp.where(sel, 0.0, NEG_INF)
        sc_ref[pl.ds(off, ck), :] = madd
        return madd

    q = q_ref[0]
    n_far = jnp.maximum((i - NEAR_BLOCKS + 1) // per, 0)
    hpg = C_HEADS // C_KV_HEADS
    npair = C_HEADS // 2
    q_pairs = [jnp.concatenate([q[:, (2 * j) * LANES:(2 * j + 1) * LANES],
                                q[:, (2 * j + 1) * LANES:(2 * j + 2) * LANES]], axis=0) for j in range(npair)]

    def bias_rows(hd, c):
        return jnp.concatenate([tab_ref[hd, jnp.clip(i - (c * per + r), 0, NEAR_BLOCKS)] for r in range(per)],
                               axis=0)

    def emit(hd, num, den):
        sl = slice(hd * LANES, (hd + 1) * LANES)
        o_ref[0, :, sl] = ((num / den).T * sg_ref[0, :, sl].astype(F32)).astype(BF16)

    def exact_attention():
        for g in range(C_KV_HEADS):
            def stage_body(near, g=g):
                def body(c, mx):
                    off = chunk_off(c)
                    madd = sc_ref[pl.ds(off, ck), :]
                    kc = k_ref[0, pl.ds(off, ck), g * LANES:(g + 1) * LANES]
                    out = []
                    for jj in range(hpg // 2):
                        lg = _mm_nt(kc, q_pairs[g * (hpg // 2) + jj])
                        for a in range(2):
                            hl = 2 * jj + a
                            x = lg[:, a * LANES:(a + 1) * LANES] + madd
                            if near:
                                x = x + bias_rows(hpg * g + hl, c)
                            x_ref[hl, pl.ds(off, ck), :] = x
                            out.append(jnp.maximum(mx[hl], _fold(x, jnp.maximum)))
                    return tuple(out)

                return body

            mx = tuple(jnp.full((SUBLANES, LANES), NEG_INF, F32) for _ in range(hpg))
            mx = lax.fori_loop(0, n_far, stage_body(False), mx)
            mx = lax.fori_loop(n_far, nch, stage_body(True), mx)
            m = [jnp.max(v, axis=0, keepdims=True) for v in mx]
            acc_ref[...] = jnp.zeros(acc_ref.shape, F32)

            def att_body(c, ls, g=g, m=m):
                off = chunk_off(c)
                vt = vt_ref[0, g * LANES:(g + 1) * LANES, pl.ds(off, ck)]
                out = []
                for jj in range(hpg // 2):
                    ps = []
                    for a in range(2):
                        hl = 2 * jj + a
                        p = jnp.exp(x_ref[hl, pl.ds(off, ck), :] - m[hl])
                        out.append(ls[hl] + _fold(p, jnp.add))
                        ps.append(p.astype(BF16))
                    acc_ref[jj] += _mm(vt, jnp.concatenate(ps, axis=1))
                return tuple(out)

            ls = lax.fori_loop(0, nch, att_body, tuple(jnp.zeros((SUBLANES, LANES), F32) for _ in range(hpg)))
            for hl in range(hpg):
                emit(hpg * g + hl, acc_ref[hl // 2, :, (hl % 2) * LANES:(hl % 2 + 1) * LANES],
                     jnp.sum(ls[hl], axis=0, keepdims=True))

    lb = lb_ref[...]
    acc_ref[...] = jnp.zeros(acc_ref.shape, F32)

    def ring(hd, c):
        base = (hd // hpg) * 2 * ck
        return hd % hpg, pl.ds(pl.multiple_of(base + (c % 2) * ck, ck), ck)

    def stage(c, near):
        off = chunk_off(c)
        mb = selection_mask(off) - lb
        for g in range(C_KV_HEADS):
            kc = k_ref[0, pl.ds(off, ck), g * LANES:(g + 1) * LANES]
            for jj in range(hpg // 2):
                lg = _mm_nt(kc, q_pairs[g * (hpg // 2) + jj])
                for a in range(2):
                    hd = hpg * g + 2 * jj + a
                    x = lg[:, a * LANES:(a + 1) * LANES] + mb
                    if near:
                        x = x + bias_rows(hd, c)
                    hl, rows = ring(hd, c)
                    x_ref[hl, rows, :] = x

    def consume(c, ls):
        off = chunk_off(c)
        out = []
        for g in range(C_KV_HEADS):
            vt = vt_ref[0, g * LANES:(g + 1) * LANES, pl.ds(off, ck)]
            for jj in range(hpg // 2):
                ps = []
                for a in range(2):
                    hd = hpg * g + 2 * jj + a
                    hl, rows = ring(hd, c)
                    p = jnp.exp(x_ref[hl, rows, :])
                    out.append(ls[hd] + _fold(p, jnp.add))
                    ps.append(p.astype(BF16))
                acc_ref[g * (hpg // 2) + jj] += _mm(vt, jnp.concatenate(ps, axis=1))
        return tuple(out)

    def step(near):
        def body(c, ls):
            out = consume(c, ls)
            stage(c + 1, near)
            return out

        return body

    stage(0, True)
    ls = tuple(jnp.zeros((SUBLANES, LANES), F32) for _ in range(C_HEADS))
    split = jnp.maximum(n_far - 1, 0)
    ls = lax.fori_loop(0, split, step(False), ls)
    ls = lax.fori_loop(split, nch - 1, step(True), ls)
    ls = consume(nch - 1, ls)
    dens = [jnp.sum(v, axis=0, keepdims=True) for v in ls]
    in_range = jnp.min(functools.reduce(jnp.minimum, dens)) > SUM_FLOOR

    @pl.when(in_range)
    def _():
        for hd in range(C_HEADS):
            emit(hd, acc_ref[hd // 2, :, (hd % 2) * LANES:(hd % 2 + 1) * LANES], dens[hd])

    @pl.when(jnp.logical_not(in_range))
    def _():
        exact_attention()


def _dsa(q, qi, wt, sg, k, vt, ki2, tab, lb):
    b, l, cw = q.shape
    nb = l // BLOCK
    topk = min(TOPK_MAX, l // 4)

    def blk(n):
        return pl.BlockSpec((1, BLOCK, n), lambda bb, i: (bb, i, 0))

    def whole(s1, s2):
        return pl.BlockSpec((1, s1, s2), lambda bb, i: (bb, 0, 0), pipeline_mode=pl.Buffered(1))

    hpg = C_HEADS // C_KV_HEADS
    return pl.pallas_call(
        functools.partial(_dsa_kernel, topk=topk),
        grid=(b, nb),
        in_specs=[blk(cw), blk(qi.shape[2]), pl.BlockSpec((1, IDX_HEADS, BLOCK), lambda bb, i: (bb, 0, i)), blk(cw),
                  whole(l, k.shape[2]), whole(vt.shape[1], l), whole(l, ki2.shape[2]),
                  pl.BlockSpec(tab.shape, lambda bb, i: (0, 0, 0, 0), pipeline_mode=pl.Buffered(1)),
                  _const_spec(lb.shape)],
        out_specs=blk(cw),
        out_shape=jax.ShapeDtypeStruct((b, l, cw), BF16),
        scratch_shapes=[pltpu.VMEM((l, LANES), F32), pltpu.VMEM((STREAMS * CAND * SUBLANES, LANES), F32),
                        pltpu.VMEM((hpg, max(l, 2 * C_KV_HEADS * KEY_CHUNK), LANES), F32),
                        pltpu.VMEM((C_HEADS // 2, C_HEAD_DIM, 2 * LANES), F32)],
        compiler_params=_params("arbitrary", "arbitrary"),
        name="dsa",
    )(q, qi, wt, sg, k, vt, ki2, tab, lb)


def _out_kernel(h_ref, a_ref, w_ref, o_ref):
    o_ref[...] = h_ref[...] + _mm(a_ref[...], w_ref[...])


def _outproj(h2, a2, w):
    rows, d = h2.shape
    t = ROW_TILE
    return pl.pallas_call(
        _out_kernel,
        grid=(rows // t,),
        in_specs=[pl.BlockSpec((t, d), lambda i: (i, 0)), pl.BlockSpec((t, a2.shape[1]), lambda i: (i, 0)),
                  _const_spec(w.shape)],
        out_specs=pl.BlockSpec((t, d), lambda i: (i, 0)),
        out_shape=jax.ShapeDtypeStruct((rows, d), F32),
        compiler_params=_params("arbitrary"),
        name="outproj1",
    )(h2, a2, w)


def _bias_tables(rel_bias):
    nv = (NEAR_BLOCKS + 1) * BLOCK
    vec = rel_bias[_t5_bucket(jnp.arange(nv, dtype=I32))].astype(F32).T

    nh = vec.shape[0]
    vecp = jnp.concatenate([jnp.broadcast_to(vec[:, :1], (nh, BLOCK - 1)), vec], axis=1)

    def toeplitz(g, rows, cols):
        w = rows + cols
        g2 = jnp.concatenate([g[..., rows - 1:rows - 1 + cols], g[..., :1], g[..., :rows - 1]], axis=-1)
        flat = jnp.tile(g2, (1,) * (g.ndim - 1) + (rows,))[..., :rows * (w - 1)]
        return flat.reshape(g.shape[:-1] + (rows, w - 1))[..., :cols]

    bias0 = jnp.transpose(toeplitz(vecp[:, :3 * BLOCK - 1], 2 * BLOCK, BLOCK), (0, 2, 1))
    r = vecp[:, :(NEAR_BLOCKS + 1) * BLOCK].reshape(nh, NEAR_BLOCKS + 1, BLOCK)
    wins = jnp.concatenate([r[:, :-1], r[:, 1:, :BLOCK - 1]], axis=2)
    tab = toeplitz(wins, BLOCK, BLOCK) - rel_bias[NUM_BUCKETS - 1].astype(F32)[:, None, None, None]
    tab = jnp.concatenate([tab, jnp.zeros((tab.shape[0], 1, BLOCK, BLOCK), F32)], axis=1)
    return bias0, tab


def kernel(x, rel_bias, norm_g, ev_w_in, ev_w_out, ev_q_norm_g, ev_k_norm_g, ev_sinks, ev_ssm_log_dt, ev_ssm_a_re,
           ev_ssm_a_im, ev_ssm_b_re, ev_ssm_b_im, ev_ssm_c_re, ev_ssm_c_im, ev_ssm_d, ev_glu_w, ev_glu_b, od_w_in,
           od_w_out, od_q_norm_g, od_k_norm_g):
    b, l, d = x.shape
    assert l % KEY_CHUNK == 0 and l % ROW_TILE == 0
    assert (NEAR_BLOCKS - 1) * BLOCK + 1 >= 16 * 64 ** (15 / 16) + 1
    bias0, tab = _bias_tables(rel_bias)

    w0 = ev_w_in[0]
    hd = A_HEAD_DIM
    o0 = np.cumsum([0, A_WIDTH, A_KV_HEADS * hd, A_KV_HEADS * hd, A_WIDTH, A_WIDTH, A_WIDTH])
    wq0, wk0, wv0, wga, wu, wgb = (w0[:, o0[n]:o0[n + 1]] for n in range(6))
    z = jnp.zeros((d, hd), w0.dtype)

    def variants(w):
        return jnp.concatenate([c for g in range(A_KV_HEADS) for c in (w[:, g * hd:(g + 1) * hd], z, z,
                                                                       w[:, g * hd:(g + 1) * hd])], axis=1)

    w0x = jnp.concatenate([wq0, variants(wk0), variants(wv0), wga, wu, wgb], axis=1).astype(BF16)
    qg2 = jnp.tile(ev_q_norm_g[0], 2)[None, :]
    kg2 = jnp.tile(ev_k_norm_g[0], 2)[None, :]
    q0, k0, v0, sga, u, sgb = _proj0(x.reshape(b * l, d), norm_g[0][None, :], w0x, qg2, kg2)
    shp = lambda a: a.reshape(b, l, a.shape[-1])
    sinks = jnp.broadcast_to(ev_sinks[0][:, None], (A_HEADS, LANES)).astype(F32)
    att0 = _attn0(shp(q0), shp(k0), shp(v0), shp(sga), bias0, sinks)
    bmat, cre, cim, sc = _s5_prep(ev_ssm_log_dt[0], ev_ssm_a_re[0], ev_ssm_a_im[0], ev_ssm_b_re[0], ev_ssm_b_im[0],
                                  ev_ssm_c_re[0], ev_ssm_c_im[0])
    ssm0 = _ssm(shp(u), shp(sgb), bmat, cre, cim, sc, ev_ssm_d[0].reshape(1, -1), ev_glu_w[0].astype(BF16),
                ev_glu_b[0][None, :])

    w1 = od_w_in[0]
    cw = C_HEADS * C_HEAD_DIM
    ckv = C_KV_HEADS * C_HEAD_DIM
    o = np.cumsum([0, cw, ckv, ckv, cw, IDX_HEADS * IDX_DIM, IDX_DIM, IDX_HEADS])
    wq, wk, wv, wg, wqi, wki, ww = (w1[:, o[n]:o[n + 1]] for n in range(7))
    zki = jnp.zeros((d, LANES - IDX_DIM), w1.dtype)
    wki2 = jnp.concatenate([wki, zki, zki, wki], axis=1)
    bf = lambda a: a.astype(BF16)
    h1, q1, k1, vt1, sg1, qi1, ki2, wt1 = _mid(
        x, att0, ssm0, bf(ev_w_out[0]), norm_g[1][None, :], bf(wq), bf(wk), bf(wv.T), bf(wg), bf(wqi), bf(wki2),
        bf(ww.T), od_q_norm_g[0][None, :], od_k_norm_g[0][None, :])
    lb = (BF16_SLACK * C_HEAD_DIM ** 0.5 * jnp.max(jnp.abs(od_q_norm_g[0])) * jnp.max(jnp.abs(od_k_norm_g[0]))
          + jnp.max(tab))
    att1 = _dsa(q1, qi1, wt1, sg1, k1, vt1, ki2, tab, jnp.full((1, LANES), lb, F32))
    out = _outproj(h1.reshape(b * l, d), att1.reshape(b * l, cw), bf(od_w_out[0]))
    return out.reshape(b, l, d)
```

```python
import functools
import math

import jax
import jax.numpy as jnp
import numpy as np
from jax import lax
from jax.experimental import pallas as pl
from jax.experimental.pallas import tpu as pltpu

F32 = jnp.float32
BF16 = jnp.bfloat16
I32 = jnp.int32

LANES = 128
SUBLANES = 8
VMEM_LIMIT = 56 * 1024 * 1024

BLOCK = 128
WINDOW = 128
A_HEADS = 8
A_HEAD_DIM = 64
A_KV_HEADS = 2
A_WIDTH = A_HEADS * A_HEAD_DIM
SSM_GROUP = 16
SSM_STATE = 64
C_HEADS = 8
C_HEAD_DIM = 128
C_KV_HEADS = 2
IDX_HEADS = 8
IDX_DIM = 64
TOPK_MAX = 256
NUM_BUCKETS = 32
REL_MAX_DIST = 1024
EPS = 1e-6
NEG_INF = -1e30
INT_MIN = -(2 ** 31)
KEY_MIN_NORMAL = 0x00800000
KEY_POS_INF = 0x7F800000
KEY_NEG_INF = INT_MIN + 0x007FFFFF
MAGNITUDE_BITS = 0x7FFFFFFF
MANTISSA_BITS = 23
BRACKET_BINADES = 3
BRACKET_STEPS = math.ceil(math.log2(BRACKET_BINADES * 2 ** MANTISSA_BITS + 1))
FULL_STEPS = math.ceil(math.log2(2 * KEY_POS_INF + 2))
SPLIT_STEPS = 26
BF16_SLACK = 1.02

ROW_TILE = 1024
KEY_CHUNK = 1024
NEAR_BLOCKS = 8
FOLD_CHAINS = 8
COUNT_ROWS = 512
SUM_FLOOR = 1e-30
CAND = 32
STREAMS = 2
NT_DIMS = (((1,), (1,)), ((), ()))


def _t5_bucket(dist):
    n = jnp.maximum(dist, 0)
    max_exact = NUM_BUCKETS // 2
    nf = jnp.maximum(n, 1).astype(F32)
    large = max_exact + (jnp.log(nf / max_exact) / math.log(REL_MAX_DIST / max_exact)
                         * (NUM_BUCKETS - max_exact)).astype(I32)
    large = jnp.minimum(large, NUM_BUCKETS - 1)
    return jnp.where(n < max_exact, n, large)


def _silu(x):
    return x * jax.nn.sigmoid(x)


def _rms(x, g):
    ms = jnp.mean(x * x, axis=-1, keepdims=True)
    return x * lax.rsqrt(ms + EPS) * g


def _mm(a, b):
    return jnp.dot(a, b, preferred_element_type=F32)


def _mm_nt(a, b):
    return lax.dot_general(a, b, NT_DIMS, preferred_element_type=F32)


def _fold(x, op):
    n = x.shape[0] // SUBLANES
    chains = min(FOLD_CHAINS, n)
    accs = [x[r * SUBLANES:(r + 1) * SUBLANES] for r in range(chains)]
    for r in range(chains, n):
        accs[r % chains] = op(accs[r % chains], x[r * SUBLANES:(r + 1) * SUBLANES])
    while len(accs) > 1:
        accs = [op(a, b) for a, b in zip(accs[::2], accs[1::2])] + accs[len(accs) & ~1:]
    return accs[0]


def _params(*sem):
    return pltpu.CompilerParams(dimension_semantics=sem, vmem_limit_bytes=VMEM_LIMIT)


def _const_spec(shape):
    zeros = (0,) * len(shape)
    return pl.BlockSpec(shape, lambda *_: zeros)


def _proj0_kernel(x_ref, g_ref, w_ref, qg_ref, kg_ref, q_ref, k_ref, v_ref, sga_ref, u_ref, sgb_ref):
    hn = _rms(x_ref[...], g_ref[...]).astype(BF16)
    lo = lax.broadcasted_iota(I32, (1, LANES), 1) < A_HEAD_DIM

    def mm(n):
        return _mm(hn, w_ref[:, n * A_WIDTH:(n + 1) * A_WIDTH])

    def segnorm(x, g2):
        sq = x * x
        s_lo = jnp.sum(jnp.where(lo, sq, 0.0), axis=-1, keepdims=True)
        s_hi = jnp.sum(jnp.where(lo, 0.0, sq), axis=-1, keepdims=True)
        inv = jnp.where(lo, lax.rsqrt(s_lo / A_HEAD_DIM + EPS), lax.rsqrt(s_hi / A_HEAD_DIM + EPS))
        return x * inv * g2

    q, k = mm(0), mm(1)
    for p in range(A_WIDTH // LANES):
        sl = slice(p * LANES, (p + 1) * LANES)
        q_ref[:, sl] = (segnorm(q[:, sl], qg_ref[...]) * (A_HEAD_DIM ** -0.5)).astype(BF16)
        k_ref[:, sl] = segnorm(k[:, sl], kg_ref[...]).astype(BF16)
    v_ref[...] = mm(2).astype(BF16)
    sga_ref[...] = _silu(mm(3)).astype(BF16)
    u_ref[...] = mm(4)
    sgb_ref[...] = _silu(mm(5)).astype(BF16)


def _proj0(x2, g, w, qg2, kg2):
    rows, d = x2.shape
    t = ROW_TILE

    def row(n):
        return pl.BlockSpec((t, n), lambda i: (i, 0))

    n = A_WIDTH
    return pl.pallas_call(
        _proj0_kernel,
        grid=(rows // t,),
        in_specs=[row(d), _const_spec((1, d)), _const_spec(w.shape), _const_spec(qg2.shape), _const_spec(kg2.shape)],
        out_specs=[row(n)] * 6,
        out_shape=[jax.ShapeDtypeStruct((rows, n), BF16), jax.ShapeDtypeStruct((rows, n), BF16),
                   jax.ShapeDtypeStruct((rows, n), BF16), jax.ShapeDtypeStruct((rows, n), BF16),
                   jax.ShapeDtypeStruct((rows, n), F32), jax.ShapeDtypeStruct((rows, n), BF16)],
        compiler_params=_params("arbitrary"),
        name="proj0",
    )(x2, g, w, qg2, kg2)


def _attn0_kernel(q_ref, kc_ref, kp_ref, vc_ref, vp_ref, sga_ref, bias_ref, sink_ref, ones_ref, o_ref):
    i = pl.program_id(1)
    kb = jnp.concatenate([kp_ref[0], kc_ref[0]], axis=0)
    vb = jnp.concatenate([vp_ref[0], vc_ref[0]], axis=0)

    def variant(x, g, a):
        n = 2 * g + a
        return x[:, n * LANES:(n + 1) * LANES]

    row = lax.broadcasted_iota(I32, (BLOCK, 2 * BLOCK), 0)
    col = lax.broadcasted_iota(I32, (BLOCK, 2 * BLOCK), 1)
    d = row + BLOCK - col
    mask = (d >= 0) & (d < WINDOW) & ((i > 0) | (col >= BLOCK))

    lgs, sinks = [], []
    for p in range(A_HEADS // 2):
        qp = q_ref[0, :, p * LANES:(p + 1) * LANES]
        for a in range(2):
            h = 2 * p + a
            lgs.append(jnp.where(mask, _mm_nt(qp, variant(kb, p // 2, a)) + bias_ref[h], NEG_INF))
            sinks.append(jnp.broadcast_to(sink_ref[h:h + 1, 0:1], (BLOCK, 1)))
    lg = jnp.concatenate(lgs, axis=0)
    sink = jnp.concatenate(sinks, axis=0)
    m = jnp.maximum(jnp.max(lg, axis=-1, keepdims=True), sink)
    e = jnp.exp(lg - m).astype(BF16)
    inv = 1.0 / (_mm(e, ones_ref[...]) + jnp.exp(sink - m))
    for p in range(A_HEADS // 2):
        sl = slice(p * LANES, (p + 1) * LANES)
        acc = jnp.zeros((BLOCK, LANES), F32)
        for a in range(2):
            h = 2 * p + a
            hs = slice(h * BLOCK, (h + 1) * BLOCK)
            acc = acc + _mm(e[hs], variant(vb, p // 2, a)) * inv[hs]
        o_ref[0, :, sl] = (acc * sga_ref[0, :, sl].astype(F32)).astype(BF16)


def _attn0(q, k, v, sga, bias0, sinks):
    b, l, _ = q.shape
    nb = l // BLOCK
    ones = jnp.ones((2 * BLOCK, LANES), BF16)

    def cur(n):
        return pl.BlockSpec((1, BLOCK, n), lambda bb, i: (bb, i, 0))

    def prev(n):
        return pl.BlockSpec((1, BLOCK, n), lambda bb, i: (bb, jnp.maximum(i - 1, 0), 0))

    return pl.pallas_call(
        _attn0_kernel,
        grid=(b, nb),
        in_specs=[cur(512), cur(512), prev(512), cur(512), prev(512), cur(512),
                  _const_spec(bias0.shape), _const_spec(sinks.shape), _const_spec(ones.shape)],
        out_specs=cur(512),
        out_shape=jax.ShapeDtypeStruct((b, l, 512), BF16),
        compiler_params=_params("arbitrary", "arbitrary"),
        name="attn0",
    )(q, k, k, v, v, sga, bias0, sinks, ones)


def _ssm_kernel(u_ref, sgb_ref, bmat_ref, cre_ref, cim_ref, sc_ref, d_ref, gw_ref, gb_ref, o_ref, xre_ref, xim_ref):
    t = u_ref.shape[1]
    nq = bmat_ref.shape[0]
    half = bmat_ref.shape[2] // 2

    @pl.when(pl.program_id(1) == 0)
    def _():
        xre_ref[0:SUBLANES, :] = jnp.zeros((SUBLANES, xre_ref.shape[1]), F32)
        xim_ref[0:SUBLANES, :] = jnp.zeros((SUBLANES, xim_ref.shape[1]), F32)

    u = u_ref[0]
    ub = u.astype(BF16)
    for q in range(nq):
        bu = _mm(ub[:, q * LANES:(q + 1) * LANES], bmat_ref[q])
        xre_ref[SUBLANES:, q * half:(q + 1) * half] = bu[:, :half]
        xim_ref[SUBLANES:, q * half:(q + 1) * half] = bu[:, half:]

    def scan(r, _):
        base = pl.multiple_of(SUBLANES + r * SUBLANES, SUBLANES)
        xr = xre_ref[pl.ds(base, SUBLANES), :]
        xi = xim_ref[pl.ds(base, SUBLANES), :]
        for s, k in enumerate((1, 2, 4)):
            ar = sc_ref[2 * s]
            ai = sc_ref[2 * s + 1]
            sr = pltpu.roll(xr, k, axis=0)
            si = pltpu.roll(xi, k, axis=0)
            xr, xi = xr + ar * sr - ai * si, xi + ar * si + ai * sr
        cr = xre_ref[pl.ds(base - 1, 1), :]
        ci = xim_ref[pl.ds(base - 1, 1), :]
        pr = sc_ref[6]
        pi = sc_ref[7]
        xre_ref[pl.ds(base, SUBLANES), :] = xr + pr * cr - pi * ci
        xim_ref[pl.ds(base, SUBLANES), :] = xi + pr * ci + pi * cr
        return 0

    lax.fori_loop(0, t // SUBLANES, scan, 0, unroll=2)
    xre_ref[0:SUBLANES, :] = xre_ref[t:t + SUBLANES, :]
    xim_ref[0:SUBLANES, :] = xim_ref[t:t + SUBLANES, :]

    ys = []
    for q in range(nq):
        xr = xre_ref[SUBLANES:, q * half:(q + 1) * half].astype(BF16)
        xi = xim_ref[SUBLANES:, q * half:(q + 1) * half].astype(BF16)
        ys.append(_mm(xr, cre_ref[q]) + _mm(xi, cim_ref[q]))
    y = jnp.concatenate(ys, axis=1) + d_ref[...] * u
    y = jax.nn.gelu(y).astype(BF16)
    hh = _mm(y, gw_ref[...]) + gb_ref[...]
    w = hh.shape[1] // 2
    o_ref[0] = (hh[:, :w] * jax.nn.sigmoid(hh[:, w:]) * sgb_ref[0].astype(F32)).astype(BF16)


def _ssm(u, sgb, bmat, cre, cim, sc, dskip, gw, gb):
    b, l, w = u.shape
    t = ROW_TILE
    ns = sc.shape[-1]

    def row(n):
        return pl.BlockSpec((1, t, n), lambda bb, i: (bb, i, 0))

    return pl.pallas_call(
        _ssm_kernel,
        grid=(b, l // t),
        in_specs=[row(w), row(w), _const_spec(bmat.shape), _const_spec(cre.shape), _const_spec(cim.shape),
                  _const_spec(sc.shape), _const_spec(dskip.shape), _const_spec(gw.shape), _const_spec(gb.shape)],
        out_specs=row(w),
        out_shape=jax.ShapeDtypeStruct((b, l, w), BF16),
        scratch_shapes=[pltpu.VMEM((SUBLANES + t, ns), F32), pltpu.VMEM((SUBLANES + t, ns), F32)],
        compiler_params=_params("arbitrary", "arbitrary"),
        name="ssm",
    )(u, sgb, bmat, cre, cim, sc, dskip, gw, gb)


def _s5_prep(log_dt, a_re, a_im, b_re, b_im, c_re, c_im):
    g, p = a_re.shape
    h = b_re.shape[-1]
    gl = LANES // h
    nq = g // gl
    dt = jnp.exp(log_dt)[:, None]
    mag = jnp.exp(a_re * dt)
    ang = a_im * dt
    ab_re = mag * jnp.cos(ang)
    ab_im = mag * jnp.sin(ang)
    den = a_re * a_re + a_im * a_im
    n_re = ab_re - 1.0
    n_im = ab_im
    f_re = (n_re * a_re + n_im * a_im) / den
    f_im = (n_im * a_re - n_re * a_im) / den
    bb_re = f_re[..., None] * b_re - f_im[..., None] * b_im
    bb_im = f_re[..., None] * b_im + f_im[..., None] * b_re
    eye = jnp.eye(gl, dtype=F32)

    def bdiag_in(m):
        m = m.reshape(nq, gl, p, h)
        return jnp.einsum('qgph,gk->qghkp', m, eye).reshape(nq, gl * h, gl * p)

    def bdiag_out(m):
        m = m.reshape(nq, gl, h, p)
        return jnp.einsum('qghp,gk->qgpkh', m, eye).reshape(nq, gl * p, gl * h)

    bmat = jnp.concatenate([bdiag_in(bb_re), bdiag_in(bb_im)], axis=2).astype(BF16)
    cre = bdiag_out(c_re).astype(BF16)
    cim = bdiag_out(-c_im).astype(BF16)

    pw = [(ab_re.reshape(-1), ab_im.reshape(-1))]
    for _ in range(SUBLANES - 1):
        pr, pi = pw[-1]
        pw.append((pr * pw[0][0] - pi * pw[0][1], pr * pw[0][1] + pi * pw[0][0]))
    rows = jnp.arange(SUBLANES)[:, None]
    sc = []
    for k in (1, 2, 4):
        sc.append(jnp.where(rows >= k, pw[k - 1][0][None, :], 0.0))
        sc.append(jnp.where(rows >= k, pw[k - 1][1][None, :], 0.0))
    sc.append(jnp.stack([pw[r][0] for r in range(SUBLANES)]))
    sc.append(jnp.stack([pw[r][1] for r in range(SUBLANES)]))
    return bmat, cre, cim, jnp.stack(sc).astype(F32)


def _mid_kernel(x_ref, a_ref, s_ref, wo_ref, g_ref, wq_ref, wk_ref, wvt_ref, wg_ref, wqi_ref, wki_ref, wwt_ref,
                qg_ref, kg_ref, h_ref, q_ref, k_ref, vt_ref, sg_ref, qi_ref, ki_ref, wt_ref):
    aw = a_ref.shape[2]
    h = x_ref[0] + _mm(a_ref[0], wo_ref[0:aw, :]) + _mm(s_ref[0], wo_ref[aw:, :])
    h_ref[0] = h
    hn = _rms(h, g_ref[...]).astype(BF16)
    qf = _mm(hn, wq_ref[...])
    for hd in range(C_HEADS):
        sl = slice(hd * C_HEAD_DIM, (hd + 1) * C_HEAD_DIM)
        q_ref[0, :, sl] = (_rms(qf[:, sl], qg_ref[...]) * (C_HEAD_DIM ** -0.5)).astype(BF16)
    kf = _mm(hn, wk_ref[...])
    for hd in range(C_KV_HEADS):
        sl = slice(hd * C_HEAD_DIM, (hd + 1) * C_HEAD_DIM)
        k_ref[0, :, sl] = _rms(kf[:, sl], kg_ref[...]).astype(BF16)
    vt_ref[0] = _mm_nt(wvt_ref[...], hn).astype(BF16)
    sg_ref[0] = _silu(_mm(hn, wg_ref[...])).astype(BF16)
    qi_ref[0] = _mm(hn, wqi_ref[...]).astype(BF16)
    ki_ref[0] = _mm(hn, wki_ref[...]).astype(BF16)
    wt_ref[0] = _mm_nt(wwt_ref[...], hn) * ((IDX_HEADS ** -0.5) * (IDX_DIM ** -0.5))


def _mid(x, att0, ssm0, wo, g, wq, wk, wvt, wg, wqi, wki2, wwt, qg, kg):
    b, l, d = x.shape
    t = ROW_TILE

    def row(n):
        return pl.BlockSpec((1, t, n), lambda bb, i: (bb, i, 0))

    def col(n):
        return pl.BlockSpec((1, n, t), lambda bb, i: (bb, 0, i))

    weights = [wo, g, wq, wk, wvt, wg, wqi, wki2, wwt, qg, kg]
    cw = C_HEADS * C_HEAD_DIM
    ckv = C_KV_HEADS * C_HEAD_DIM
    return pl.pallas_call(
        _mid_kernel,
        grid=(b, l // t),
        in_specs=[row(d), row(att0.shape[2]), row(ssm0.shape[2])] + [_const_spec(w.shape) for w in weights],
        out_specs=[row(d), row(cw), row(ckv), col(ckv), row(cw), row(IDX_HEADS * IDX_DIM), row(2 * LANES),
                   col(IDX_HEADS)],
        out_shape=[jax.ShapeDtypeStruct((b, l, d), F32), jax.ShapeDtypeStruct((b, l, cw), BF16),
                   jax.ShapeDtypeStruct((b, l, ckv), BF16), jax.ShapeDtypeStruct((b, ckv, l), BF16),
                   jax.ShapeDtypeStruct((b, l, cw), BF16), jax.ShapeDtypeStruct((b, l, IDX_HEADS * IDX_DIM), BF16),
                   jax.ShapeDtypeStruct((b, l, 2 * LANES), BF16), jax.ShapeDtypeStruct((b, IDX_HEADS, l), F32)],
        compiler_params=_params("arbitrary", "arbitrary"),
        name="mid",
    )(x, att0, ssm0, *weights)


def _dsa_kernel(q_ref, qi_ref, wt_ref, sg_ref, k_ref, vt_ref, ki_ref, tab_ref, lb_ref, o_ref,
                sc_ref, best_ref, x_ref, acc_ref, ring_a_ref, ring_b_ref, mb_ref, *, topk):
    i = pl.program_id(1)
    ck = KEY_CHUNK
    per = ck // BLOCK
    nch = (i + per) // per
    t_row = i * BLOCK + lax.broadcasted_iota(I32, (1, LANES), 1)
    kiota = lax.broadcasted_iota(I32, (ck, LANES), 0)

    def chunk_off(c):
        return pl.multiple_of(c * ck, ck)

    qi = qi_ref[0]
    qi_stack = [jnp.concatenate([qi[:, (2 * s) * LANES:(2 * s + 1) * LANES],
                                 qi[:, (2 * s + 1) * LANES:(2 * s + 2) * LANES]], axis=0) for s in range(2)]
    wt = wt_ref[0]

    def score_chunk(c, masked):
        off = chunk_off(c)
        sc = jnp.zeros((ck, LANES), F32)
        for a in range(2):
            kk = ki_ref[0, pl.ds(off, ck), a * LANES:(a + 1) * LANES]
            for s in range(2):
                r = _mm_nt(kk, qi_stack[s])
                for j in range(2):
                    hd = 2 * (2 * s + j) + a
                    sc = sc + jnp.maximum(r[:, j * LANES:(j + 1) * LANES], 0.0) * wt[hd:hd + 1, :]
        if masked:
            sc = jnp.where(off + kiota <= t_row, sc, NEG_INF)
        sc_ref[pl.ds(off, ck), :] = sc
        return _fold(sc, jnp.maximum)

    def exchange(v, a, b):
        v[a], v[b] = jnp.maximum(v[a], v[b]), jnp.minimum(v[a], v[b])

    def sort_desc(v):
        n, k = len(v), 2
        while k <= n:
            j = k // 2
            while j >= 1:
                for a in range(n):
                    b = a ^ j
                    if b > a:
                        exchange(v, *((a, b) if (a & k) == 0 else (b, a)))
                j //= 2
            k *= 2

    def merge_top(best, blk):
        n = len(best)
        v = [jnp.maximum(best[r], blk[n - 1 - r]) for r in range(n)]
        j = n // 2
        while j >= 1:
            for a in range(n):
                if a ^ j > a:
                    exchange(v, a, a ^ j)
            j //= 2
        return v

    crow = STREAMS * CAND * SUBLANES

    def cand_step(row0):
        blk_all = sc_ref[pl.ds(pl.multiple_of(row0, crow), crow), :]
        for st in range(STREAMS):
            blk = [blk_all[(STREAMS * r + st) * SUBLANES:(STREAMS * r + st + 1) * SUBLANES] for r in range(CAND)]
            sort_desc(blk)
            base = st * CAND * SUBLANES
            best = [best_ref[base + r * SUBLANES:base + (r + 1) * SUBLANES, :] for r in range(CAND)]
            for r, x in enumerate(merge_top(best, blk)):
                best_ref[base + r * SUBLANES:base + (r + 1) * SUBLANES, :] = x

    def score_body(c, mx):
        return jnp.maximum(jnp.maximum(mx, score_chunk(2 * c, False)), score_chunk(2 * c + 1, False))

    below = nch - 1
    smax = lax.fori_loop(0, below // 2, score_body, jnp.full((SUBLANES, LANES), NEG_INF, F32))
    smax = lax.cond(below % 2 == 1, lambda: jnp.maximum(smax, score_chunk(below - 1, False)), lambda: smax)
    smax = jnp.max(jnp.maximum(smax, score_chunk(nch - 1, True)), axis=0, keepdims=True)

    def count(*preds):
        rows = COUNT_ROWS
        sub = ck // rows

        def body(c, accs):
            out = []
            for u in range(sub):
                off = pl.multiple_of(c * ck + u * rows, rows)
                s = sc_ref[pl.ds(off, rows), :]
                for n, pred in enumerate(preds):
                    ind = pred(s, off).astype(I32)
                    out.append(accs[u * len(preds) + n]
                               + jnp.sum(ind.reshape(rows // SUBLANES, SUBLANES, LANES), axis=0))
            return tuple(out)

        accs = lax.fori_loop(0, nch, body, tuple(jnp.zeros((SUBLANES, LANES), I32) for _ in range(sub * len(preds))))
        res = [jnp.sum(sum(accs[n::len(preds)]), axis=0, keepdims=True) for n in range(len(preds))]
        return res[0] if len(preds) == 1 else res

    def key_value(k):
        return pltpu.bitcast(k ^ ((k >> 31) & MAGNITUDE_BITS), F32)

    def count_ge(k):
        thr = key_value(k)
        return count(lambda s, off: s >= thr)

    def full(v):
        return jnp.full((1, LANES), v, I32)

    def bisect(_, st):
        lo, hi, c_lo, c_hi = st
        mid = (lo >> 1) + (hi >> 1) + (lo & hi & 1)
        c = count_ge(mid)
        ge = c >= topk
        return jnp.where(ge, mid, lo), jnp.where(ge, hi, mid), jnp.where(ge, c, c_lo), jnp.where(ge, c_hi, c)

    searching = (i + 1) * BLOCK > topk

    def float_key(x):
        bits = pltpu.bitcast(x, I32)
        return bits ^ ((bits >> 31) & MAGNITUDE_BITS)

    def search():
        k_lo = float_key(smax * 2.0 ** -BRACKET_BINADES)
        c = count_ge(k_lo)
        ok = (smax > 0.0) & (c >= topk)
        trips = jnp.where(jnp.min(jnp.where(ok, 1, 0)) > 0, BRACKET_STEPS, FULL_STEPS)
        st = (jnp.where(ok, k_lo, KEY_NEG_INF), float_key(smax) + 1, jnp.where(ok, c, nch * ck), full(0))
        out = lax.fori_loop(0, trips, bisect, st)
        return out[0], out[2], out[3]

    def cand_search():
        best_ref[...] = jnp.full(best_ref.shape, -jnp.inf, F32)

        def cand_body(c, _):
            cand_step(c * crow)
            return 0

        lax.fori_loop(0, (i + crow // BLOCK) // (crow // BLOCK), cand_body, 0)

        def all_sublanes(x):
            for shift in (4, 2, 1):
                x = x + pltpu.roll(x, shift, axis=0)
            return x

        def count_cand(k):
            thr = key_value(k)[None]
            parts = [jnp.sum((best_ref[r:r + COUNT_ROWS, :].reshape(COUNT_ROWS // SUBLANES, SUBLANES, LANES)
                              >= thr).astype(I32), axis=0) for r in range(0, crow, COUNT_ROWS)]
            return all_sublanes(sum(parts))

        def step(_, st):
            lo, hi = st
            mid = (lo >> 1) + (hi >> 1) + (lo & hi & 1)
            take = ~((count_cand(mid) - topk) >> 31)
            return (mid & take) | (lo & ~take), (hi & take) | (mid & ~take)

        smax8 = jnp.broadcast_to(smax, (SUBLANES, LANES))
        k_lo = float_key(smax8 * 2.0 ** -BRACKET_BINADES)
        ok = (smax8 > 0.0) & (count_cand(k_lo) >= topk)
        trips = jnp.where(jnp.min(jnp.where(ok, 1, 0)) > 0, BRACKET_STEPS, FULL_STEPS)
        vk8, _ = lax.fori_loop(0, trips, step, (jnp.where(ok, k_lo, KEY_NEG_INF), float_key(smax8) + 1))
        vk = vk8[0:1]
        thr = key_value(vk)
        above_cand = jnp.sum((best_ref[...] > thr).astype(I32), axis=0, keepdims=True)
        c_ge, c_gt = count(lambda s, off: s >= thr, lambda s, off: s > thr)
        complete = jnp.min(jnp.where(c_gt == above_cand, 1, 0)) > 0
        return lax.cond(complete, lambda: (vk, c_ge, c_gt), search)

    vkey, c_lo, c_hi = lax.cond(searching, cand_search, lambda: (full(KEY_NEG_INF), full(topk), full(0)))
    vthr = key_value(vkey)
    need = topk - c_hi
    ties = c_lo - c_hi

    def tie_search():
        def split_step():
            nxt = vkey + 1
            nxt = jnp.where((nxt > 0) & (nxt < KEY_MIN_NORMAL), KEY_MIN_NORMAL, nxt)
            step = key_value(nxt) - vthr

            def split(_, st):
                fl, fh = st
                fm = 0.5 * (fl + fh)
                t = vthr + fm * step
                ge = count(lambda s, off: s >= t) >= topk
                return jnp.where(ge, fm, fl), jnp.where(ge, fh, fm)

            fl, _ = lax.fori_loop(0, SPLIT_STEPS, split, (jnp.zeros((1, LANES), F32), jnp.ones((1, LANES), F32)))
            t = vthr + fl * step
            return t, topk - count(lambda s, off: s > t)

        inside = jnp.max(ties - count(lambda s, off: s == vthr)) > 0
        thr, want = lax.cond(inside, split_step, lambda: (vthr, need))

        rr = lax.broadcasted_iota(I32, (BLOCK, BLOCK), 0)
        cc = lax.broadcasted_iota(I32, (BLOCK, BLOCK), 1)
        tril = jnp.where(cc <= rr, 1.0, 0.0).astype(BF16)
        want_f = want.astype(F32)

        def body(c, before):
            off = chunk_off(c)
            blocks = [sc_ref[pl.ds(pl.multiple_of(off + r * BLOCK, BLOCK), BLOCK), :] for r in range(per)]
            hits = [s == thr for s in blocks]
            ranks = [_mm(tril, jnp.where(h, 1.0, 0.0).astype(BF16)) for h in hits]
            for r in range(per):
                rank = ranks[r] + before
                sc_ref[pl.ds(pl.multiple_of(off + r * BLOCK, BLOCK), BLOCK), :] = jnp.where(
                    hits[r] & (rank > want_f), NEG_INF, blocks[r])
                before = rank[BLOCK - 1:BLOCK, :]
            return before

        lax.fori_loop(0, nch, body, jnp.zeros((1, LANES), F32))
        return thr

    any_tie = searching & (jnp.max(ties - need) > 0)
    vthr = lax.cond(any_tie, tie_search, lambda: vthr)

    def selection_mask(off):
        s = sc_ref[pl.ds(off, ck), :]
        s_idx = off + kiota
        sel = (s >= vthr) & (s_idx <= t_row)
        madd = jnp.where(sel, 0.0, NEG_INF)
        sc_ref[pl.ds(off, ck), :] = madd
        return madd

    q = q_ref[0]
    n_far = jnp.maximum((i - NEAR_BLOCKS + 1) // per, 0)
    hpg = C_HEADS // C_KV_HEADS
    npair = C_HEADS // 2
    q_pairs = [jnp.concatenate([q[:, (2 * j) * LANES:(2 * j + 1) * LANES],
                                q[:, (2 * j + 1) * LANES:(2 * j + 2) * LANES]], axis=0) for j in range(npair)]

    def bias_rows(hd, c):
        return jnp.concatenate([tab_ref[hd, jnp.clip(i - (c * per + r), 0, NEAR_BLOCKS)] for r in range(per)],
                               axis=0)

    def emit(hd, num, den):
        sl = slice(hd * LANES, (hd + 1) * LANES)
        o_ref[0, :, sl] = ((num / den).T * sg_ref[0, :, sl].astype(F32)).astype(BF16)

    def exact_attention():
        for g in range(C_KV_HEADS):
            def stage_body(near, g=g):
                def body(c, mx):
                    off = chunk_off(c)
                    madd = sc_ref[pl.ds(off, ck), :]
                    kc = k_ref[0, pl.ds(off, ck), g * LANES:(g + 1) * LANES]
                    out = []
                    for jj in range(hpg // 2):
                        lg = _mm_nt(kc, q_pairs[g * (hpg // 2) + jj])
                        for a in range(2):
                            hl = 2 * jj + a
                            x = lg[:, a * LANES:(a + 1) * LANES] + madd
                            if near:
                                x = x + bias_rows(hpg * g + hl, c)
                            x_ref[hl, pl.ds(off, ck), :] = x
                            out.append(jnp.maximum(mx[hl], _fold(x, jnp.maximum)))
                    return tuple(out)

                return body

            mx = tuple(jnp.full((SUBLANES, LANES), NEG_INF, F32) for _ in range(hpg))
            mx = lax.fori_loop(0, n_far, stage_body(False), mx)
            mx = lax.fori_loop(n_far, nch, stage_body(True), mx)
            m = [jnp.max(v, axis=0, keepdims=True) for v in mx]
            acc_ref[...] = jnp.zeros(acc_ref.shape, F32)

            def att_body(c, ls, g=g, m=m):
                off = chunk_off(c)
                vt = vt_ref[0, g * LANES:(g + 1) * LANES, pl.ds(off, ck)]
                out = []
                for jj in range(hpg // 2):
                    ps = []
                    for a in range(2):
                        hl = 2 * jj + a
                        p = jnp.exp(x_ref[hl, pl.ds(off, ck), :] - m[hl])
                        out.append(ls[hl] + _fold(p, jnp.add))
                        ps.append(p.astype(BF16))
                    acc_ref[jj] += _mm(vt, jnp.concatenate(ps, axis=1))
                return tuple(out)

            ls = lax.fori_loop(0, nch, att_body, tuple(jnp.zeros((SUBLANES, LANES), F32) for _ in range(hpg)))
            for hl in range(hpg):
                emit(hpg * g + hl, acc_ref[hl // 2, :, (hl % 2) * LANES:(hl % 2 + 1) * LANES],
                     jnp.sum(ls[hl], axis=0, keepdims=True))

    lb = lb_ref[...]
    acc_ref[...] = jnp.zeros(acc_ref.shape, F32)

    def stage_mask(c):
        mb_ref[...] = selection_mask(chunk_off(c)) - lb

    def stage_pair(c, j, ring_ref, near):
        g = j // (hpg // 2)
        kc = k_ref[0, pl.ds(chunk_off(c), ck), g * LANES:(g + 1) * LANES]
        lg = _mm_nt(kc, q_pairs[j])
        for a in range(2):
            hd = 2 * j + a
            x = lg[:, a * LANES:(a + 1) * LANES] + mb_ref[...]
            if near:
                x = x + bias_rows(hd, c)
            ring_ref[hd] = x

    def consume_pair(c, j, ring_ref, ls):
        g = j // (hpg // 2)
        vt = vt_ref[0, g * LANES:(g + 1) * LANES, pl.ds(chunk_off(c), ck)]
        ps = []
        for a in range(2):
            hd = 2 * j + a
            p = jnp.exp(ring_ref[hd])
            ls[hd] = ls[hd] + _fold(p, jnp.add)
            ps.append(p.astype(BF16))
        acc_ref[j] += _mm(vt, jnp.concatenate(ps, axis=1))

    def stage(c, ring_ref):
        stage_mask(c)
        for j in range(npair):
            stage_pair(c, j, ring_ref, True)

    def consume(c, ring_ref, ls, then_stage=None):
        ls = list(ls)
        if then_stage is not None:
            stage_mask(then_stage[0])
        for j in range(npair):
            consume_pair(c, j, ring_ref, ls)
            if then_stage is not None:
                stage_pair(then_stage[0], j, *then_stage[1:])
        return tuple(ls)

    def pair_step(near):
        def body(k, ls):
            c = 2 * k
            ls = consume(c, ring_a_ref, ls, (c + 1, ring_b_ref, near))
            return consume(c + 1, ring_b_ref, ls, (c + 2, ring_a_ref, near))

        return body

    stage(0, ring_a_ref)
    ls = tuple(jnp.zeros((SUBLANES, LANES), F32) for _ in range(C_HEADS))
    npairs = (nch - 1) // 2
    far_pairs = jnp.maximum((n_far - 1) // 2, 0)
    ls = lax.fori_loop(0, far_pairs, pair_step(False), ls)
    ls = lax.fori_loop(far_pairs, npairs, pair_step(True), ls)
    last = 2 * npairs

    def tail_two(ls):
        ls = consume(last, ring_a_ref, ls, (last + 1, ring_b_ref, True))
        return consume(last + 1, ring_b_ref, ls)

    ls = lax.cond(nch - 1 > last, tail_two, lambda ls: consume(last, ring_a_ref, ls), ls)
    dens = [jnp.sum(v, axis=0, keepdims=True) for v in ls]
    in_range = jnp.min(functools.reduce(jnp.minimum, dens)) > SUM_FLOOR

    @pl.when(in_range)
    def _():
        for hd in range(C_HEADS):
            emit(hd, acc_ref[hd // 2, :, (hd % 2) * LANES:(hd % 2 + 1) * LANES], dens[hd])

    @pl.when(jnp.logical_not(in_range))
    def _():
        exact_attention()


def _dsa(q, qi, wt, sg, k, vt, ki2, tab, lb):
    b, l, cw = q.shape
    nb = l // BLOCK
    topk = min(TOPK_MAX, l // 4)

    def blk(n):
        return pl.BlockSpec((1, BLOCK, n), lambda bb, i: (bb, i, 0))

    def whole(s1, s2):
        return pl.BlockSpec((1, s1, s2), lambda bb, i: (bb, 0, 0), pipeline_mode=pl.Buffered(1))

    hpg = C_HEADS // C_KV_HEADS
    return pl.pallas_call(
        functools.partial(_dsa_kernel, topk=topk),
        grid=(b, nb),
        in_specs=[blk(cw), blk(qi.shape[2]), pl.BlockSpec((1, IDX_HEADS, BLOCK), lambda bb, i: (bb, 0, i)), blk(cw),
                  whole(l, k.shape[2]), whole(vt.shape[1], l), whole(l, ki2.shape[2]),
                  pl.BlockSpec(tab.shape, lambda bb, i: (0, 0, 0, 0), pipeline_mode=pl.Buffered(1)),
                  _const_spec(lb.shape)],
        out_specs=blk(cw),
        out_shape=jax.ShapeDtypeStruct((b, l, cw), BF16),
        scratch_shapes=[pltpu.VMEM((l, LANES), F32), pltpu.VMEM((STREAMS * CAND * SUBLANES, LANES), F32),
                        pltpu.VMEM((hpg, l, LANES), F32),
                        pltpu.VMEM((C_HEADS // 2, C_HEAD_DIM, 2 * LANES), F32),
                        pltpu.VMEM((C_HEADS, KEY_CHUNK, LANES), F32), pltpu.VMEM((C_HEADS, KEY_CHUNK, LANES), F32),
                        pltpu.VMEM((KEY_CHUNK, LANES), F32)],
        compiler_params=_params("arbitrary", "arbitrary"),
        name="dsa",
    )(q, qi, wt, sg, k, vt, ki2, tab, lb)


def _out_kernel(h_ref, a_ref, w_ref, o_ref):
    o_ref[...] = h_ref[...] + _mm(a_ref[...], w_ref[...])


def _outproj(h2, a2, w):
    rows, d = h2.shape
    t = ROW_TILE
    return pl.pallas_call(
        _out_kernel,
        grid=(rows // t,),
        in_specs=[pl.BlockSpec((t, d), lambda i: (i, 0)), pl.BlockSpec((t, a2.shape[1]), lambda i: (i, 0)),
                  _const_spec(w.shape)],
        out_specs=pl.BlockSpec((t, d), lambda i: (i, 0)),
        out_shape=jax.ShapeDtypeStruct((rows, d), F32),
        compiler_params=_params("arbitrary"),
        name="outproj1",
    )(h2, a2, w)


def _bias_tables(rel_bias):
    nv = (NEAR_BLOCKS + 1) * BLOCK
    vec = rel_bias[_t5_bucket(jnp.arange(nv, dtype=I32))].astype(F32).T

    nh = vec.shape[0]
    vecp = jnp.concatenate([jnp.broadcast_to(vec[:, :1], (nh, BLOCK - 1)), vec], axis=1)

    def toeplitz(g, rows, cols):
        w = rows + cols
        g2 = jnp.concatenate([g[..., rows - 1:rows - 1 + cols], g[..., :1], g[..., :rows - 1]], axis=-1)
        flat = jnp.tile(g2, (1,) * (g.ndim - 1) + (rows,))[..., :rows * (w - 1)]
        return flat.reshape(g.shape[:-1] + (rows, w - 1))[..., :cols]

    bias0 = jnp.transpose(toeplitz(vecp[:, :3 * BLOCK - 1], 2 * BLOCK, BLOCK), (0, 2, 1))
    r = vecp[:, :(NEAR_BLOCKS + 1) * BLOCK].reshape(nh, NEAR_BLOCKS + 1, BLOCK)
    wins = jnp.concatenate([r[:, :-1], r[:, 1:, :BLOCK - 1]], axis=2)
    tab = toeplitz(wins, BLOCK, BLOCK) - rel_bias[NUM_BUCKETS - 1].astype(F32)[:, None, None, None]
    tab = jnp.concatenate([tab, jnp.zeros((tab.shape[0], 1, BLOCK, BLOCK), F32)], axis=1)
    return bias0, tab


def kernel(x, rel_bias, norm_g, ev_w_in, ev_w_out, ev_q_norm_g, ev_k_norm_g, ev_sinks, ev_ssm_log_dt, ev_ssm_a_re,
           ev_ssm_a_im, ev_ssm_b_re, ev_ssm_b_im, ev_ssm_c_re, ev_ssm_c_im, ev_ssm_d, ev_glu_w, ev_glu_b, od_w_in,
           od_w_out, od_q_norm_g, od_k_norm_g):
    b, l, d = x.shape
    assert l % KEY_CHUNK == 0 and l % ROW_TILE == 0
    assert (NEAR_BLOCKS - 1) * BLOCK + 1 >= 16 * 64 ** (15 / 16) + 1
    bias0, tab = _bias_tables(rel_bias)

    w0 = ev_w_in[0]
    hd = A_HEAD_DIM
    o0 = np.cumsum([0, A_WIDTH, A_KV_HEADS * hd, A_KV_HEADS * hd, A_WIDTH, A_WIDTH, A_WIDTH])
    wq0, wk0, wv0, wga, wu, wgb = (w0[:, o0[n]:o0[n + 1]] for n in range(6))
    z = jnp.zeros((d, hd), w0.dtype)

    def variants(w):
        return jnp.concatenate([c for g in range(A_KV_HEADS) for c in (w[:, g * hd:(g + 1) * hd], z, z,
                                                                       w[:, g * hd:(g + 1) * hd])], axis=1)

    w0x = jnp.concatenate([wq0, variants(wk0), variants(wv0), wga, wu, wgb], axis=1).astype(BF16)
    qg2 = jnp.tile(ev_q_norm_g[0], 2)[None, :]
    kg2 = jnp.tile(ev_k_norm_g[0], 2)[None, :]
    q0, k0, v0, sga, u, sgb = _proj0(x.reshape(b * l, d), norm_g[0][None, :], w0x, qg2, kg2)
    shp = lambda a: a.reshape(b, l, a.shape[-1])
    sinks = jnp.broadcast_to(ev_sinks[0][:, None], (A_HEADS, LANES)).astype(F32)
    att0 = _attn0(shp(q0), shp(k0), shp(v0), shp(sga), bias0, sinks)
    bmat, cre, cim, sc = _s5_prep(ev_ssm_log_dt[0], ev_ssm_a_re[0], ev_ssm_a_im[0], ev_ssm_b_re[0], ev_ssm_b_im[0],
                                  ev_ssm_c_re[0], ev_ssm_c_im[0])
    ssm0 = _ssm(shp(u), shp(sgb), bmat, cre, cim, sc, ev_ssm_d[0].reshape(1, -1), ev_glu_w[0].astype(BF16),
                ev_glu_b[0][None, :])

    w1 = od_w_in[0]
    cw = C_HEADS * C_HEAD_DIM
    ckv = C_KV_HEADS * C_HEAD_DIM
    o = np.cumsum([0, cw, ckv, ckv, cw, IDX_HEADS * IDX_DIM, IDX_DIM, IDX_HEADS])
    wq, wk, wv, wg, wqi, wki, ww = (w1[:, o[n]:o[n + 1]] for n in range(7))
    zki = jnp.zeros((d, LANES - IDX_DIM), w1.dtype)
    wki2 = jnp.concatenate([wki, zki, zki, wki], axis=1)
    bf = lambda a: a.astype(BF16)
    h1, q1, k1, vt1, sg1, qi1, ki2, wt1 = _mid(
        x, att0, ssm0, bf(ev_w_out[0]), norm_g[1][None, :], bf(wq), bf(wk), bf(wv.T), bf(wg), bf(wqi), bf(wki2),
        bf(ww.T), od_q_norm_g[0][None, :], od_k_norm_g[0][None, :])
    lb = (BF16_SLACK * C_HEAD_DIM ** 0.5 * jnp.max(jnp.abs(od_q_norm_g[0])) * jnp.max(jnp.abs(od_k_norm_g[0]))
          + jnp.max(tab))
    att1 = _dsa(q1, qi1, wt1, sg1, k1, vt1, ki2, tab, jnp.full((1, LANES), lb, F32))
    out = _outproj(h1.reshape(b * l, d), att1.reshape(b * l, cw), bf(od_w_out[0]))
    return out.reshape(b, l, d)
```

```python
import functools
import math

import jax
import jax.numpy as jnp
import numpy as np
from jax import lax
from jax.experimental import pallas as pl
from jax.experimental.pallas import tpu as pltpu

F32 = jnp.float32
BF16 = jnp.bfloat16
I32 = jnp.int32

LANES = 128
SUBLANES = 8
VMEM_LIMIT = 56 * 1024 * 1024

BLOCK = 128
WINDOW = 128
A_HEADS = 8
A_HEAD_DIM = 64
A_KV_HEADS = 2
A_WIDTH = A_HEADS * A_HEAD_DIM
SSM_GROUP = 16
SSM_STATE = 64
C_HEADS = 8
C_HEAD_DIM = 128
C_KV_HEADS = 2
IDX_HEADS = 8
IDX_DIM = 64
TOPK_MAX = 256
NUM_BUCKETS = 32
REL_MAX_DIST = 1024
EPS = 1e-6
NEG_INF = -1e30
INT_MIN = -(2 ** 31)
KEY_MIN_NORMAL = 0x00800000
KEY_POS_INF = 0x7F800000
KEY_NEG_INF = INT_MIN + 0x007FFFFF
MAGNITUDE_BITS = 0x7FFFFFFF
MANTISSA_BITS = 23
BRACKET_BINADES = 3
BRACKET_STEPS = math.ceil(math.log2(BRACKET_BINADES * 2 ** MANTISSA_BITS + 1))
FULL_STEPS = math.ceil(math.log2(2 * KEY_POS_INF + 2))
SPLIT_STEPS = 26
BF16_SLACK = 1.02

ROW_TILE = 1024
KEY_CHUNK = 1024
NEAR_BLOCKS = 8
FOLD_CHAINS = 8
COUNT_ROWS = 512
SUM_FLOOR = 1e-30
CAND = 32
STREAMS = 2
NT_DIMS = (((1,), (1,)), ((), ()))


def _t5_bucket(dist):
    n = jnp.maximum(dist, 0)
    max_exact = NUM_BUCKETS // 2
    nf = jnp.maximum(n, 1).astype(F32)
    large = max_exact + (jnp.log(nf / max_exact) / math.log(REL_MAX_DIST / max_exact)
                         * (NUM_BUCKETS - max_exact)).astype(I32)
    large = jnp.minimum(large, NUM_BUCKETS - 1)
    return jnp.where(n < max_exact, n, large)


def _silu(x):
    return x * jax.nn.sigmoid(x)


def _rms(x, g):
    ms = jnp.mean(x * x, axis=-1, keepdims=True)
    return x * lax.rsqrt(ms + EPS) * g


def _mm(a, b):
    return jnp.dot(a, b, preferred_element_type=F32)


def _mm_nt(a, b):
    return lax.dot_general(a, b, NT_DIMS, preferred_element_type=F32)


def _fold(x, op):
    n = x.shape[0] // SUBLANES
    chains = min(FOLD_CHAINS, n)
    accs = [x[r * SUBLANES:(r + 1) * SUBLANES] for r in range(chains)]
    for r in range(chains, n):
        accs[r % chains] = op(accs[r % chains], x[r * SUBLANES:(r + 1) * SUBLANES])
    while len(accs) > 1:
        accs = [op(a, b) for a, b in zip(accs[::2], accs[1::2])] + accs[len(accs) & ~1:]
    return accs[0]


def _params(*sem):
    return pltpu.CompilerParams(dimension_semantics=sem, vmem_limit_bytes=VMEM_LIMIT)


def _const_spec(shape):
    zeros = (0,) * len(shape)
    return pl.BlockSpec(shape, lambda *_: zeros)


def _proj0_kernel(x_ref, g_ref, w_ref, qg_ref, kg_ref, q_ref, k_ref, v_ref, sga_ref, u_ref, sgb_ref):
    hn = _rms(x_ref[...], g_ref[...]).astype(BF16)
    lo = lax.broadcasted_iota(I32, (1, LANES), 1) < A_HEAD_DIM

    def mm(n):
        return _mm(hn, w_ref[:, n * A_WIDTH:(n + 1) * A_WIDTH])

    def segnorm(x, g2):
        sq = x * x
        s_lo = jnp.sum(jnp.where(lo, sq, 0.0), axis=-1, keepdims=True)
        s_hi = jnp.sum(jnp.where(lo, 0.0, sq), axis=-1, keepdims=True)
        inv = jnp.where(lo, lax.rsqrt(s_lo / A_HEAD_DIM + EPS), lax.rsqrt(s_hi / A_HEAD_DIM + EPS))
        return x * inv * g2

    q, k = mm(0), mm(1)
    for p in range(A_WIDTH // LANES):
        sl = slice(p * LANES, (p + 1) * LANES)
        q_ref[:, sl] = (segnorm(q[:, sl], qg_ref[...]) * (A_HEAD_DIM ** -0.5)).astype(BF16)
        k_ref[:, sl] = segnorm(k[:, sl], kg_ref[...]).astype(BF16)
    v_ref[...] = mm(2).astype(BF16)
    sga_ref[...] = _silu(mm(3)).astype(BF16)
    u_ref[...] = mm(4)
    sgb_ref[...] = _silu(mm(5)).astype(BF16)


def _proj0(x2, g, w, qg2, kg2):
    rows, d = x2.shape
    t = ROW_TILE

    def row(n):
        return pl.BlockSpec((t, n), lambda i: (i, 0))

    n = A_WIDTH
    return pl.pallas_call(
        _proj0_kernel,
        grid=(rows // t,),
        in_specs=[row(d), _const_spec((1, d)), _const_spec(w.shape), _const_spec(qg2.shape), _const_spec(kg2.shape)],
        out_specs=[row(n)] * 6,
        out_shape=[jax.ShapeDtypeStruct((rows, n), BF16), jax.ShapeDtypeStruct((rows, n), BF16),
                   jax.ShapeDtypeStruct((rows, n), BF16), jax.ShapeDtypeStruct((rows, n), BF16),
                   jax.ShapeDtypeStruct((rows, n), F32), jax.ShapeDtypeStruct((rows, n), BF16)],
        compiler_params=_params("arbitrary"),
        name="proj0",
    )(x2, g, w, qg2, kg2)


def _attn0_kernel(q_ref, kc_ref, kp_ref, vc_ref, vp_ref, sga_ref, bias_ref, sink_ref, ones_ref, o_ref):
    i = pl.program_id(1)
    kb = jnp.concatenate([kp_ref[0], kc_ref[0]], axis=0)
    vb = jnp.concatenate([vp_ref[0], vc_ref[0]], axis=0)

    def variant(x, g, a):
        n = 2 * g + a
        return x[:, n * LANES:(n + 1) * LANES]

    row = lax.broadcasted_iota(I32, (BLOCK, 2 * BLOCK), 0)
    col = lax.broadcasted_iota(I32, (BLOCK, 2 * BLOCK), 1)
    d = row + BLOCK - col
    mask = (d >= 0) & (d < WINDOW) & ((i > 0) | (col >= BLOCK))

    lgs, sinks = [], []
    for p in range(A_HEADS // 2):
        qp = q_ref[0, :, p * LANES:(p + 1) * LANES]
        for a in range(2):
            h = 2 * p + a
            lgs.append(jnp.where(mask, _mm_nt(qp, variant(kb, p // 2, a)) + bias_ref[h], NEG_INF))
            sinks.append(jnp.broadcast_to(sink_ref[h:h + 1, 0:1], (BLOCK, 1)))
    lg = jnp.concatenate(lgs, axis=0)
    sink = jnp.concatenate(sinks, axis=0)
    m = jnp.maximum(jnp.max(lg, axis=-1, keepdims=True), sink)
    e = jnp.exp(lg - m).astype(BF16)
    inv = 1.0 / (_mm(e, ones_ref[...]) + jnp.exp(sink - m))
    for p in range(A_HEADS // 2):
        sl = slice(p * LANES, (p + 1) * LANES)
        acc = jnp.zeros((BLOCK, LANES), F32)
        for a in range(2):
            h = 2 * p + a
            hs = slice(h * BLOCK, (h + 1) * BLOCK)
            acc = acc + _mm(e[hs], variant(vb, p // 2, a)) * inv[hs]
        o_ref[0, :, sl] = (acc * sga_ref[0, :, sl].astype(F32)).astype(BF16)


def _attn0(q, k, v, sga, bias0, sinks):
    b, l, _ = q.shape
    nb = l // BLOCK
    ones = jnp.ones((2 * BLOCK, LANES), BF16)

    def cur(n):
        return pl.BlockSpec((1, BLOCK, n), lambda bb, i: (bb, i, 0))

    def prev(n):
        return pl.BlockSpec((1, BLOCK, n), lambda bb, i: (bb, jnp.maximum(i - 1, 0), 0))

    return pl.pallas_call(
        _attn0_kernel,
        grid=(b, nb),
        in_specs=[cur(512), cur(512), prev(512), cur(512), prev(512), cur(512),
                  _const_spec(bias0.shape), _const_spec(sinks.shape), _const_spec(ones.shape)],
        out_specs=cur(512),
        out_shape=jax.ShapeDtypeStruct((b, l, 512), BF16),
        compiler_params=_params("arbitrary", "arbitrary"),
        name="attn0",
    )(q, k, k, v, v, sga, bias0, sinks, ones)


def _ssm_kernel(u_ref, sgb_ref, bmat_ref, cre_ref, cim_ref, sc_ref, d_ref, gw_ref, gb_ref, o_ref, xre_ref, xim_ref):
    t = u_ref.shape[1]
    nq = bmat_ref.shape[0]
    half = bmat_ref.shape[2] // 2

    @pl.when(pl.program_id(1) == 0)
    def _():
        xre_ref[0:SUBLANES, :] = jnp.zeros((SUBLANES, xre_ref.shape[1]), F32)
        xim_ref[0:SUBLANES, :] = jnp.zeros((SUBLANES, xim_ref.shape[1]), F32)

    u = u_ref[0]
    ub = u.astype(BF16)
    for q in range(nq):
        bu = _mm(ub[:, q * LANES:(q + 1) * LANES], bmat_ref[q])
        xre_ref[SUBLANES:, q * half:(q + 1) * half] = bu[:, :half]
        xim_ref[SUBLANES:, q * half:(q + 1) * half] = bu[:, half:]

    def scan(r, _):
        base = pl.multiple_of(SUBLANES + r * SUBLANES, SUBLANES)
        xr = xre_ref[pl.ds(base, SUBLANES), :]
        xi = xim_ref[pl.ds(base, SUBLANES), :]
        for s, k in enumerate((1, 2, 4)):
            ar = sc_ref[2 * s]
            ai = sc_ref[2 * s + 1]
            sr = pltpu.roll(xr, k, axis=0)
            si = pltpu.roll(xi, k, axis=0)
            xr, xi = xr + ar * sr - ai * si, xi + ar * si + ai * sr
        cr = xre_ref[pl.ds(base - 1, 1), :]
        ci = xim_ref[pl.ds(base - 1, 1), :]
        pr = sc_ref[6]
        pi = sc_ref[7]
        xre_ref[pl.ds(base, SUBLANES), :] = xr + pr * cr - pi * ci
        xim_ref[pl.ds(base, SUBLANES), :] = xi + pr * ci + pi * cr
        return 0

    lax.fori_loop(0, t // SUBLANES, scan, 0, unroll=2)
    xre_ref[0:SUBLANES, :] = xre_ref[t:t + SUBLANES, :]
    xim_ref[0:SUBLANES, :] = xim_ref[t:t + SUBLANES, :]

    ys = []
    for q in range(nq):
        xr = xre_ref[SUBLANES:, q * half:(q + 1) * half].astype(BF16)
        xi = xim_ref[SUBLANES:, q * half:(q + 1) * half].astype(BF16)
        ys.append(_mm(xr, cre_ref[q]) + _mm(xi, cim_ref[q]))
    y = jnp.concatenate(ys, axis=1) + d_ref[...] * u
    y = jax.nn.gelu(y).astype(BF16)
    hh = _mm(y, gw_ref[...]) + gb_ref[...]
    w = hh.shape[1] // 2
    o_ref[0] = (hh[:, :w] * jax.nn.sigmoid(hh[:, w:]) * sgb_ref[0].astype(F32)).astype(BF16)


def _ssm(u, sgb, bmat, cre, cim, sc, dskip, gw, gb):
    b, l, w = u.shape
    t = ROW_TILE
    ns = sc.shape[-1]

    def row(n):
        return pl.BlockSpec((1, t, n), lambda bb, i: (bb, i, 0))

    return pl.pallas_call(
        _ssm_kernel,
        grid=(b, l // t),
        in_specs=[row(w), row(w), _const_spec(bmat.shape), _const_spec(cre.shape), _const_spec(cim.shape),
                  _const_spec(sc.shape), _const_spec(dskip.shape), _const_spec(gw.shape), _const_spec(gb.shape)],
        out_specs=row(w),
        out_shape=jax.ShapeDtypeStruct((b, l, w), BF16),
        scratch_shapes=[pltpu.VMEM((SUBLANES + t, ns), F32), pltpu.VMEM((SUBLANES + t, ns), F32)],
        compiler_params=_params("arbitrary", "arbitrary"),
        name="ssm",
    )(u, sgb, bmat, cre, cim, sc, dskip, gw, gb)


def _s5_prep(log_dt, a_re, a_im, b_re, b_im, c_re, c_im):
    g, p = a_re.shape
    h = b_re.shape[-1]
    gl = LANES // h
    nq = g // gl
    dt = jnp.exp(log_dt)[:, None]
    mag = jnp.exp(a_re * dt)
    ang = a_im * dt
    ab_re = mag * jnp.cos(ang)
    ab_im = mag * jnp.sin(ang)
    den = a_re * a_re + a_im * a_im
    n_re = ab_re - 1.0
    n_im = ab_im
    f_re = (n_re * a_re + n_im * a_im) / den
    f_im = (n_im * a_re - n_re * a_im) / den
    bb_re = f_re[..., None] * b_re - f_im[..., None] * b_im
    bb_im = f_re[..., None] * b_im + f_im[..., None] * b_re
    eye = jnp.eye(gl, dtype=F32)

    def bdiag_in(m):
        m = m.reshape(nq, gl, p, h)
        return jnp.einsum('qgph,gk->qghkp', m, eye).reshape(nq, gl * h, gl * p)

    def bdiag_out(m):
        m = m.reshape(nq, gl, h, p)
        return jnp.einsum('qghp,gk->qgpkh', m, eye).reshape(nq, gl * p, gl * h)

    bmat = jnp.concatenate([bdiag_in(bb_re), bdiag_in(bb_im)], axis=2).astype(BF16)
    cre = bdiag_out(c_re).astype(BF16)
    cim = bdiag_out(-c_im).astype(BF16)

    pw = [(ab_re.reshape(-1), ab_im.reshape(-1))]
    for _ in range(SUBLANES - 1):
        pr, pi = pw[-1]
        pw.append((pr * pw[0][0] - pi * pw[0][1], pr * pw[0][1] + pi * pw[0][0]))
    rows = jnp.arange(SUBLANES)[:, None]
    sc = []
    for k in (1, 2, 4):
        sc.append(jnp.where(rows >= k, pw[k - 1][0][None, :], 0.0))
        sc.append(jnp.where(rows >= k, pw[k - 1][1][None, :], 0.0))
    sc.append(jnp.stack([pw[r][0] for r in range(SUBLANES)]))
    sc.append(jnp.stack([pw[r][1] for r in range(SUBLANES)]))
    return bmat, cre, cim, jnp.stack(sc).astype(F32)


def _mid_kernel(x_ref, a_ref, s_ref, wo_ref, g_ref, wq_ref, wk_ref, wvt_ref, wg_ref, wqi_ref, wki_ref, wwt_ref,
                qg_ref, kg_ref, h_ref, q_ref, k_ref, vt_ref, sg_ref, qi_ref, ki_ref, wt_ref):
    aw = a_ref.shape[2]
    h = x_ref[0] + _mm(a_ref[0], wo_ref[0:aw, :]) + _mm(s_ref[0], wo_ref[aw:, :])
    h_ref[0] = h
    hn = _rms(h, g_ref[...]).astype(BF16)
    qf = _mm(hn, wq_ref[...])
    for hd in range(C_HEADS):
        sl = slice(hd * C_HEAD_DIM, (hd + 1) * C_HEAD_DIM)
        q_ref[0, :, sl] = (_rms(qf[:, sl], qg_ref[...]) * (C_HEAD_DIM ** -0.5)).astype(BF16)
    kf = _mm(hn, wk_ref[...])
    for hd in range(C_KV_HEADS):
        sl = slice(hd * C_HEAD_DIM, (hd + 1) * C_HEAD_DIM)
        k_ref[0, :, sl] = _rms(kf[:, sl], kg_ref[...]).astype(BF16)
    vt_ref[0] = _mm_nt(wvt_ref[...], hn).astype(BF16)
    sg_ref[0] = _silu(_mm(hn, wg_ref[...])).astype(BF16)
    qi_ref[0] = _mm(hn, wqi_ref[...]).astype(BF16)
    ki_ref[0] = _mm(hn, wki_ref[...]).astype(BF16)
    wt_ref[0] = _mm_nt(wwt_ref[...], hn) * ((IDX_HEADS ** -0.5) * (IDX_DIM ** -0.5))


def _mid(x, att0, ssm0, wo, g, wq, wk, wvt, wg, wqi, wki2, wwt, qg, kg):
    b, l, d = x.shape
    t = ROW_TILE

    def row(n):
        return pl.BlockSpec((1, t, n), lambda bb, i: (bb, i, 0))

    def col(n):
        return pl.BlockSpec((1, n, t), lambda bb, i: (bb, 0, i))

    weights = [wo, g, wq, wk, wvt, wg, wqi, wki2, wwt, qg, kg]
    cw = C_HEADS * C_HEAD_DIM
    ckv = C_KV_HEADS * C_HEAD_DIM
    return pl.pallas_call(
        _mid_kernel,
        grid=(b, l // t),
        in_specs=[row(d), row(att0.shape[2]), row(ssm0.shape[2])] + [_const_spec(w.shape) for w in weights],
        out_specs=[row(d), row(cw), row(ckv), col(ckv), row(cw), row(IDX_HEADS * IDX_DIM), row(2 * LANES),
                   col(IDX_HEADS)],
        out_shape=[jax.ShapeDtypeStruct((b, l, d), F32), jax.ShapeDtypeStruct((b, l, cw), BF16),
                   jax.ShapeDtypeStruct((b, l, ckv), BF16), jax.ShapeDtypeStruct((b, ckv, l), BF16),
                   jax.ShapeDtypeStruct((b, l, cw), BF16), jax.ShapeDtypeStruct((b, l, IDX_HEADS * IDX_DIM), BF16),
                   jax.ShapeDtypeStruct((b, l, 2 * LANES), BF16), jax.ShapeDtypeStruct((b, IDX_HEADS, l), F32)],
        compiler_params=_params("arbitrary", "arbitrary"),
        name="mid",
    )(x, att0, ssm0, *weights)


def _dsa_kernel(q_ref, qi_ref, wt_ref, sg_ref, k_ref, vt_ref, ki_ref, tab_ref, lb_ref, o_ref,
                sc_ref, best_ref, x_ref, acc_ref, ring_a_ref, ring_b_ref, mb_ref, *, topk):
    i = pl.program_id(1)
    ck = KEY_CHUNK
    per = ck // BLOCK
    nch = (i + per) // per
    t_row = i * BLOCK + lax.broadcasted_iota(I32, (1, LANES), 1)
    kiota = lax.broadcasted_iota(I32, (ck, LANES), 0)

    def chunk_off(c):
        return pl.multiple_of(c * ck, ck)

    qi = qi_ref[0]
    qi_stack = [jnp.concatenate([qi[:, (2 * s) * LANES:(2 * s + 1) * LANES],
                                 qi[:, (2 * s + 1) * LANES:(2 * s + 2) * LANES]], axis=0) for s in range(2)]
    wt = wt_ref[0]

    def score_chunk(c, masked):
        off = chunk_off(c)
        sc = jnp.zeros((ck, LANES), F32)
        for a in range(2):
            kk = ki_ref[0, pl.ds(off, ck), a * LANES:(a + 1) * LANES]
            for s in range(2):
                r = _mm_nt(kk, qi_stack[s])
                for j in range(2):
                    hd = 2 * (2 * s + j) + a
                    sc = sc + jnp.maximum(r[:, j * LANES:(j + 1) * LANES], 0.0) * wt[hd:hd + 1, :]
        if masked:
            sc = jnp.where(off + kiota <= t_row, sc, NEG_INF)
        sc_ref[pl.ds(off, ck), :] = sc
        return _fold(sc, jnp.maximum)

    def exchange(v, a, b):
        v[a], v[b] = jnp.maximum(v[a], v[b]), jnp.minimum(v[a], v[b])

    def sort_desc(v):
        n, k = len(v), 2
        while k <= n:
            j = k // 2
            while j >= 1:
                for a in range(n):
                    b = a ^ j
                    if b > a:
                        exchange(v, *((a, b) if (a & k) == 0 else (b, a)))
                j //= 2
            k *= 2

    def merge_top(best, blk):
        n = len(best)
        v = [jnp.maximum(best[r], blk[n - 1 - r]) for r in range(n)]
        j = n // 2
        while j >= 1:
            for a in range(n):
                if a ^ j > a:
                    exchange(v, a, a ^ j)
            j //= 2
        return v

    crow = STREAMS * CAND * SUBLANES

    def cand_step(row0):
        blk_all = sc_ref[pl.ds(pl.multiple_of(row0, crow), crow), :]
        for st in range(STREAMS):
            blk = [blk_all[(STREAMS * r + st) * SUBLANES:(STREAMS * r + st + 1) * SUBLANES] for r in range(CAND)]
            sort_desc(blk)
            base = st * CAND * SUBLANES
            best = [best_ref[base + r * SUBLANES:base + (r + 1) * SUBLANES, :] for r in range(CAND)]
            for r, x in enumerate(merge_top(best, blk)):
                best_ref[base + r * SUBLANES:base + (r + 1) * SUBLANES, :] = x

    def score_body(c, mx):
        return jnp.maximum(jnp.maximum(mx, score_chunk(2 * c, False)), score_chunk(2 * c + 1, False))

    below = nch - 1
    smax = lax.fori_loop(0, below // 2, score_body, jnp.full((SUBLANES, LANES), NEG_INF, F32))
    smax = lax.cond(below % 2 == 1, lambda: jnp.maximum(smax, score_chunk(below - 1, False)), lambda: smax)
    smax = jnp.max(jnp.maximum(smax, score_chunk(nch - 1, True)), axis=0, keepdims=True)

    def count(*preds):
        rows = COUNT_ROWS
        sub = ck // rows

        def body(c, accs):
            out = []
            for u in range(sub):
                off = pl.multiple_of(c * ck + u * rows, rows)
                s = sc_ref[pl.ds(off, rows), :]
                for n, pred in enumerate(preds):
                    ind = pred(s, off).astype(I32)
                    out.append(accs[u * len(preds) + n]
                               + jnp.sum(ind.reshape(rows // SUBLANES, SUBLANES, LANES), axis=0))
            return tuple(out)

        accs = lax.fori_loop(0, nch, body, tuple(jnp.zeros((SUBLANES, LANES), I32) for _ in range(sub * len(preds))))
        res = [jnp.sum(sum(accs[n::len(preds)]), axis=0, keepdims=True) for n in range(len(preds))]
        return res[0] if len(preds) == 1 else res

    def key_value(k):
        return pltpu.bitcast(k ^ ((k >> 31) & MAGNITUDE_BITS), F32)

    def count_ge(k):
        thr = key_value(k)
        return count(lambda s, off: s >= thr)

    def full(v):
        return jnp.full((1, LANES), v, I32)

    def bisect(_, st):
        lo, hi, c_lo, c_hi = st
        mid = (lo >> 1) + (hi >> 1) + (lo & hi & 1)
        c = count_ge(mid)
        ge = c >= topk
        return jnp.where(ge, mid, lo), jnp.where(ge, hi, mid), jnp.where(ge, c, c_lo), jnp.where(ge, c_hi, c)

    searching = (i + 1) * BLOCK > topk

    def float_key(x):
        bits = pltpu.bitcast(x, I32)
        return bits ^ ((bits >> 31) & MAGNITUDE_BITS)

    def search():
        k_lo = float_key(smax * 2.0 ** -BRACKET_BINADES)
        c = count_ge(k_lo)
        ok = (smax > 0.0) & (c >= topk)
        trips = jnp.where(jnp.min(jnp.where(ok, 1, 0)) > 0, BRACKET_STEPS, FULL_STEPS)
        st = (jnp.where(ok, k_lo, KEY_NEG_INF), float_key(smax) + 1, jnp.where(ok, c, nch * ck), full(0))
        out = lax.fori_loop(0, trips, bisect, st)
        return out[0], out[2], out[3]

    def cand_search():
        best_ref[...] = jnp.full(best_ref.shape, -jnp.inf, F32)

        def cand_body(c, _):
            cand_step(c * crow)
            return 0

        lax.fori_loop(0, (i + crow // BLOCK) // (crow // BLOCK), cand_body, 0)

        def all_sublanes(x):
            for shift in (4, 2, 1):
                x = x + pltpu.roll(x, shift, axis=0)
            return x

        def count_cand(k):
            thr = key_value(k)[None]
            parts = [jnp.sum((best_ref[r:r + COUNT_ROWS, :].reshape(COUNT_ROWS // SUBLANES, SUBLANES, LANES)
                              >= thr).astype(I32), axis=0) for r in range(0, crow, COUNT_ROWS)]
            return all_sublanes(sum(parts))

        def step(_, st):
            lo, hi = st
            mid = (lo >> 1) + (hi >> 1) + (lo & hi & 1)
            take = ~((count_cand(mid) - topk) >> 31)
            return (mid & take) | (lo & ~take), (hi & take) | (mid & ~take)

        smax8 = jnp.broadcast_to(smax, (SUBLANES, LANES))
        k_lo = float_key(smax8 * 2.0 ** -BRACKET_BINADES)
        ok = (smax8 > 0.0) & (count_cand(k_lo) >= topk)
        trips = jnp.where(jnp.min(jnp.where(ok, 1, 0)) > 0, BRACKET_STEPS, FULL_STEPS)
        vk8, _ = lax.fori_loop(0, trips, step, (jnp.where(ok, k_lo, KEY_NEG_INF), float_key(smax8) + 1))
        vk = vk8[0:1]
        thr = key_value(vk)
        above_cand = jnp.sum((best_ref[...] > thr).astype(I32), axis=0, keepdims=True)
        c_ge, c_gt = count(lambda s, off: s >= thr, lambda s, off: s > thr)
        complete = jnp.min(jnp.where(c_gt == above_cand, 1, 0)) > 0
        return lax.cond(complete, lambda: (vk, c_ge, c_gt), search)

    vkey, c_lo, c_hi = lax.cond(searching, cand_search, lambda: (full(KEY_NEG_INF), full(topk), full(0)))
    vthr = key_value(vkey)
    need = topk - c_hi
    ties = c_lo - c_hi

    def tie_search():
        def split_step():
            nxt = vkey + 1
            nxt = jnp.where((nxt > 0) & (nxt < KEY_MIN_NORMAL), KEY_MIN_NORMAL, nxt)
            step = key_value(nxt) - vthr

            def split(_, st):
                fl, fh = st
                fm = 0.5 * (fl + fh)
                t = vthr + fm * step
                ge = count(lambda s, off: s >= t) >= topk
                return jnp.where(ge, fm, fl), jnp.where(ge, fh, fm)

            fl, _ = lax.fori_loop(0, SPLIT_STEPS, split, (jnp.zeros((1, LANES), F32), jnp.ones((1, LANES), F32)))
            t = vthr + fl * step
            return t, topk - count(lambda s, off: s > t)

        inside = jnp.max(ties - count(lambda s, off: s == vthr)) > 0
        thr, want = lax.cond(inside, split_step, lambda: (vthr, need))

        rr = lax.broadcasted_iota(I32, (BLOCK, BLOCK), 0)
        cc = lax.broadcasted_iota(I32, (BLOCK, BLOCK), 1)
        tril = jnp.where(cc <= rr, 1.0, 0.0).astype(BF16)
        want_f = want.astype(F32)

        def body(c, before):
            off = chunk_off(c)
            blocks = [sc_ref[pl.ds(pl.multiple_of(off + r * BLOCK, BLOCK), BLOCK), :] for r in range(per)]
            hits = [s == thr for s in blocks]
            ranks = [_mm(tril, jnp.where(h, 1.0, 0.0).astype(BF16)) for h in hits]
            for r in range(per):
                rank = ranks[r] + before
                sc_ref[pl.ds(pl.multiple_of(off + r * BLOCK, BLOCK), BLOCK), :] = jnp.where(
                    hits[r] & (rank > want_f), NEG_INF, blocks[r])
                before = rank[BLOCK - 1:BLOCK, :]
            return before

        lax.fori_loop(0, nch, body, jnp.zeros((1, LANES), F32))
        return thr

    any_tie = searching & (jnp.max(ties - need) > 0)
    vthr = lax.cond(any_tie, tie_search, lambda: vthr)

    def selection_mask(off):
        s = sc_ref[pl.ds(off, ck), :]
        s_idx = off + kiota
        sel = (s >= vthr) & (s_idx <= t_row)
        madd = jnp.where(sel, 0.0, NEG_INF)
        sc_ref[pl.ds(off, ck), :] = madd
        return madd

    q = q_ref[0]
    n_far = jnp.maximum((i - NEAR_BLOCKS + 1) // per, 0)
    hpg = C_HEADS // C_KV_HEADS
    npair = C_HEADS // 2
    q_pairs = [jnp.concatenate([q[:, (2 * j) * LANES:(2 * j + 1) * LANES],
                                q[:, (2 * j + 1) * LANES:(2 * j + 2) * LANES]], axis=0) for j in range(npair)]

    def bias_rows(hd, c):
        return jnp.concatenate([tab_ref[hd, jnp.clip(i - (c * per + r), 0, NEAR_BLOCKS)] for r in range(per)],
                               axis=0)

    def emit(hd, num, den):
        sl = slice(hd * LANES, (hd + 1) * LANES)
        o_ref[0, :, sl] = ((num / den).T * sg_ref[0, :, sl].astype(F32)).astype(BF16)

    def exact_attention():
        for g in range(C_KV_HEADS):
            def stage_body(near, g=g):
                def body(c, mx):
                    off = chunk_off(c)
                    madd = sc_ref[pl.ds(off, ck), :]
                    kc = k_ref[0, pl.ds(off, ck), g * LANES:(g + 1) * LANES]
                    out = []
                    for jj in range(hpg // 2):
                        lg = _mm_nt(kc, q_pairs[g * (hpg // 2) + jj])
                        for a in range(2):
                            hl = 2 * jj + a
                            x = lg[:, a * LANES:(a + 1) * LANES] + madd
                            if near:
                                x = x + bias_rows(hpg * g + hl, c)
                            x_ref[hl, pl.ds(off, ck), :] = x
                            out.append(jnp.maximum(mx[hl], _fold(x, jnp.maximum)))
                    return tuple(out)

                return body

            mx = tuple(jnp.full((SUBLANES, LANES), NEG_INF, F32) for _ in range(hpg))
            mx = lax.fori_loop(0, n_far, stage_body(False), mx)
            mx = lax.fori_loop(n_far, nch, stage_body(True), mx)
            m = [jnp.max(v, axis=0, keepdims=True) for v in mx]
            acc_ref[...] = jnp.zeros(acc_ref.shape, F32)

            def att_body(c, ls, g=g, m=m):
                off = chunk_off(c)
                vt = vt_ref[0, g * LANES:(g + 1) * LANES, pl.ds(off, ck)]
                out = []
                for jj in range(hpg // 2):
                    ps = []
                    for a in range(2):
                        hl = 2 * jj + a
                        p = jnp.exp(x_ref[hl, pl.ds(off, ck), :] - m[hl])
                        out.append(ls[hl] + _fold(p, jnp.add))
                        ps.append(p.astype(BF16))
                    acc_ref[jj] += _mm(vt, jnp.concatenate(ps, axis=1))
                return tuple(out)

            ls = lax.fori_loop(0, nch, att_body, tuple(jnp.zeros((SUBLANES, LANES), F32) for _ in range(hpg)))
            for hl in range(hpg):
                emit(hpg * g + hl, acc_ref[hl // 2, :, (hl % 2) * LANES:(hl % 2 + 1) * LANES],
                     jnp.sum(ls[hl], axis=0, keepdims=True))

    lb = lb_ref[...]
    acc_ref[...] = jnp.zeros(acc_ref.shape, F32)

    def stage_mask(c):
        mb_ref[...] = selection_mask(chunk_off(c)) - lb

    def stage_pair(c, j, ring_ref, near):
        g = j // (hpg // 2)
        kc = k_ref[0, pl.ds(chunk_off(c), ck), g * LANES:(g + 1) * LANES]
        lg = _mm_nt(kc, q_pairs[j])
        for a in range(2):
            hd = 2 * j + a
            x = lg[:, a * LANES:(a + 1) * LANES] + mb_ref[...]
            if near:
                x = x + bias_rows(hd, c)
            ring_ref[hd] = x

    def consume_pair(c, j, ring_ref, ls):
        g = j // (hpg // 2)
        vt = vt_ref[0, g * LANES:(g + 1) * LANES, pl.ds(chunk_off(c), ck)]
        ps = []
        for a in range(2):
            hd = 2 * j + a
            p = jnp.exp(ring_ref[hd])
            ls[hd] = ls[hd] + _fold(p, jnp.add)
            ps.append(p.astype(BF16))
        acc_ref[j] += _mm(vt, jnp.concatenate(ps, axis=1))

    def stage(c, ring_ref):
        stage_mask(c)
        for j in range(npair):
            stage_pair(c, j, ring_ref, True)

    def consume(c, ring_ref, ls, then_stage=None):
        ls = list(ls)
        if then_stage is not None:
            stage_mask(then_stage[0])
        for j in range(npair):
            if then_stage is not None:
                stage_pair(then_stage[0], j, *then_stage[1:])
            consume_pair(c, j, ring_ref, ls)
        return tuple(ls)

    def pair_step(near):
        def body(k, ls):
            c = 2 * k
            ls = consume(c, ring_a_ref, ls, (c + 1, ring_b_ref, near))
            return consume(c + 1, ring_b_ref, ls, (c + 2, ring_a_ref, near))

        return body

    stage(0, ring_a_ref)
    ls = tuple(jnp.zeros((SUBLANES, LANES), F32) for _ in range(C_HEADS))
    npairs = (nch - 1) // 2
    far_pairs = jnp.maximum((n_far - 1) // 2, 0)
    ls = lax.fori_loop(0, far_pairs, pair_step(False), ls)
    ls = lax.fori_loop(far_pairs, npairs, pair_step(True), ls)
    last = 2 * npairs

    def tail_two(ls):
        ls = consume(last, ring_a_ref, ls, (last + 1, ring_b_ref, True))
        return consume(last + 1, ring_b_ref, ls)

    ls = lax.cond(nch - 1 > last, tail_two, lambda ls: consume(last, ring_a_ref, ls), ls)
    dens = [jnp.sum(v, axis=0, keepdims=True) for v in ls]
    in_range = jnp.min(functools.reduce(jnp.minimum, dens)) > SUM_FLOOR

    @pl.when(in_range)
    def _():
        for hd in range(C_HEADS):
            emit(hd, acc_ref[hd // 2, :, (hd % 2) * LANES:(hd % 2 + 1) * LANES], dens[hd])

    @pl.when(jnp.logical_not(in_range))
    def _():
        exact_attention()


def _dsa(q, qi, wt, sg, k, vt, ki2, tab, lb):
    b, l, cw = q.shape
    nb = l // BLOCK
    topk = min(TOPK_MAX, l // 4)

    def blk(n):
        return pl.BlockSpec((1, BLOCK, n), lambda bb, i: (bb, i, 0))

    def whole(s1, s2):
        return pl.BlockSpec((1, s1, s2), lambda bb, i: (bb, 0, 0), pipeline_mode=pl.Buffered(1))

    hpg = C_HEADS // C_KV_HEADS
    return pl.pallas_call(
        functools.partial(_dsa_kernel, topk=topk),
        grid=(b, nb),
        in_specs=[blk(cw), blk(qi.shape[2]), pl.BlockSpec((1, IDX_HEADS, BLOCK), lambda bb, i: (bb, 0, i)), blk(cw),
                  whole(l, k.shape[2]), whole(vt.shape[1], l), whole(l, ki2.shape[2]),
                  pl.BlockSpec(tab.shape, lambda bb, i: (0, 0, 0, 0), pipeline_mode=pl.Buffered(1)),
                  _const_spec(lb.shape)],
        out_specs=blk(cw),
        out_shape=jax.ShapeDtypeStruct((b, l, cw), BF16),
        scratch_shapes=[pltpu.VMEM((l, LANES), F32), pltpu.VMEM((STREAMS * CAND * SUBLANES, LANES), F32),
                        pltpu.VMEM((hpg, l, LANES), F32),
                        pltpu.VMEM((C_HEADS // 2, C_HEAD_DIM, 2 * LANES), F32),
                        pltpu.VMEM((C_HEADS, KEY_CHUNK, LANES), F32), pltpu.VMEM((C_HEADS, KEY_CHUNK, LANES), F32),
                        pltpu.VMEM((KEY_CHUNK, LANES), F32)],
        compiler_params=_params("arbitrary", "arbitrary"),
        name="dsa",
    )(q, qi, wt, sg, k, vt, ki2, tab, lb)


def _out_kernel(h_ref, a_ref, w_ref, o_ref):
    o_ref[...] = h_ref[...] + _mm(a_ref[...], w_ref[...])


def _outproj(h2, a2, w):
    rows, d = h2.shape
    t = ROW_TILE
    return pl.pallas_call(
        _out_kernel,
        grid=(rows // t,),
        in_specs=[pl.BlockSpec((t, d), lambda i: (i, 0)), pl.BlockSpec((t, a2.shape[1]), lambda i: (i, 0)),
                  _const_spec(w.shape)],
        out_specs=pl.BlockSpec((t, d), lambda i: (i, 0)),
        out_shape=jax.ShapeDtypeStruct((rows, d), F32),
        compiler_params=_params("arbitrary"),
        name="outproj1",
    )(h2, a2, w)


def _bias_tables(rel_bias):
    nv = (NEAR_BLOCKS + 1) * BLOCK
    vec = rel_bias[_t5_bucket(jnp.arange(nv, dtype=I32))].astype(F32).T

    nh = vec.shape[0]
    vecp = jnp.concatenate([jnp.broadcast_to(vec[:, :1], (nh, BLOCK - 1)), vec], axis=1)

    def toeplitz(g, rows, cols):
        w = rows + cols
        g2 = jnp.concatenate([g[..., rows - 1:rows - 1 + cols], g[..., :1], g[..., :rows - 1]], axis=-1)
        flat = jnp.tile(g2, (1,) * (g.ndim - 1) + (rows,))[..., :rows * (w - 1)]
        return flat.reshape(g.shape[:-1] + (rows, w - 1))[..., :cols]

    bias0 = jnp.transpose(toeplitz(vecp[:, :3 * BLOCK - 1], 2 * BLOCK, BLOCK), (0, 2, 1))
    r = vecp[:, :(NEAR_BLOCKS + 1) * BLOCK].reshape(nh, NEAR_BLOCKS + 1, BLOCK)
    wins = jnp.concatenate([r[:, :-1], r[:, 1:, :BLOCK - 1]], axis=2)
    tab = toeplitz(wins, BLOCK, BLOCK) - rel_bias[NUM_BUCKETS - 1].astype(F32)[:, None, None, None]
    tab = jnp.concatenate([tab, jnp.zeros((tab.shape[0], 1, BLOCK, BLOCK), F32)], axis=1)
    return bias0, tab


def kernel(x, rel_bias, norm_g, ev_w_in, ev_w_out, ev_q_norm_g, ev_k_norm_g, ev_sinks, ev_ssm_log_dt, ev_ssm_a_re,
           ev_ssm_a_im, ev_ssm_b_re, ev_ssm_b_im, ev_ssm_c_re, ev_ssm_c_im, ev_ssm_d, ev_glu_w, ev_glu_b, od_w_in,
           od_w_out, od_q_norm_g, od_k_norm_g):
    b, l, d = x.shape
    assert l % KEY_CHUNK == 0 and l % ROW_TILE == 0
    assert (NEAR_BLOCKS - 1) * BLOCK + 1 >= 16 * 64 ** (15 / 16) + 1
    bias0, tab = _bias_tables(rel_bias)

    w0 = ev_w_in[0]
    hd = A_HEAD_DIM
    o0 = np.cumsum([0, A_WIDTH, A_KV_HEADS * hd, A_KV_HEADS * hd, A_WIDTH, A_WIDTH, A_WIDTH])
    wq0, wk0, wv0, wga, wu, wgb = (w0[:, o0[n]:o0[n + 1]] for n in range(6))
    z = jnp.zeros((d, hd), w0.dtype)

    def variants(w):
        return jnp.concatenate([c for g in range(A_KV_HEADS) for c in (w[:, g * hd:(g + 1) * hd], z, z,
                                                                       w[:, g * hd:(g + 1) * hd])], axis=1)

    w0x = jnp.concatenate([wq0, variants(wk0), variants(wv0), wga, wu, wgb], axis=1).astype(BF16)
    qg2 = jnp.tile(ev_q_norm_g[0], 2)[None, :]
    kg2 = jnp.tile(ev_k_norm_g[0], 2)[None, :]
    q0, k0, v0, sga, u, sgb = _proj0(x.reshape(b * l, d), norm_g[0][None, :], w0x, qg2, kg2)
    shp = lambda a: a.reshape(b, l, a.shape[-1])
    sinks = jnp.broadcast_to(ev_sinks[0][:, None], (A_HEADS, LANES)).astype(F32)
    att0 = _attn0(shp(q0), shp(k0), shp(v0), shp(sga), bias0, sinks)
    bmat, cre, cim, sc = _s5_prep(ev_ssm_log_dt[0], ev_ssm_a_re[0], ev_ssm_a_im[0], ev_ssm_b_re[0], ev_ssm_b_im[0],
                                  ev_ssm_c_re[0], ev_ssm_c_im[0])
    ssm0 = _ssm(shp(u), shp(sgb), bmat, cre, cim, sc, ev_ssm_d[0].reshape(1, -1), ev_glu_w[0].astype(BF16),
                ev_glu_b[0][None, :])

    w1 = od_w_in[0]
    cw = C_HEADS * C_HEAD_DIM
    ckv = C_KV_HEADS * C_HEAD_DIM
    o = np.cumsum([0, cw, ckv, ckv, cw, IDX_HEADS * IDX_DIM, IDX_DIM, IDX_HEADS])
    wq, wk, wv, wg, wqi, wki, ww = (w1[:, o[n]:o[n + 1]] for n in range(7))
    zki = jnp.zeros((d, LANES - IDX_DIM), w1.dtype)
    wki2 = jnp.concatenate([wki, zki, zki, wki], axis=1)
    bf = lambda a: a.astype(BF16)
    h1, q1, k1, vt1, sg1, qi1, ki2, wt1 = _mid(
        x, att0, ssm0, bf(ev_w_out[0]), norm_g[1][None, :], bf(wq), bf(wk), bf(wv.T), bf(wg), bf(wqi), bf(wki2),
        bf(ww.T), od_q_norm_g[0][None, :], od_k_norm_g[0][None, :])
    lb = (BF16_SLACK * C_HEAD_DIM ** 0.5 * jnp.max(jnp.abs(od_q_norm_g[0])) * jnp.max(jnp.abs(od_k_norm_g[0]))
          + jnp.max(tab))
    att1 = _dsa(q1, qi1, wt1, sg1, k1, vt1, ki2, tab, jnp.full((1, LANES), lb, F32))
    out = _outproj(h1.reshape(b * l, d), att1.reshape(b * l, cw), bf(od_w_out[0]))
    return out.reshape(b, l, d)
```

```python
import functools
import math

import jax
import jax.numpy as jnp
import numpy as np
from jax import lax
from jax.experimental import pallas as pl
from jax.experimental.pallas import tpu as pltpu

F32 = jnp.float32
BF16 = jnp.bfloat16
I32 = jnp.int32

LANES = 128
SUBLANES = 8
VMEM_LIMIT = 56 * 1024 * 1024

BLOCK = 128
WINDOW = 128
A_HEADS = 8
A_HEAD_DIM = 64
A_KV_HEADS = 2
A_WIDTH = A_HEADS * A_HEAD_DIM
SSM_GROUP = 16
SSM_STATE = 64
C_HEADS = 8
C_HEAD_DIM = 128
C_KV_HEADS = 2
IDX_HEADS = 8
IDX_DIM = 64
TOPK_MAX = 256
NUM_BUCKETS = 32
REL_MAX_DIST = 1024
EPS = 1e-6
NEG_INF = -1e30
INT_MIN = -(2 ** 31)
KEY_MIN_NORMAL = 0x00800000
KEY_POS_INF = 0x7F800000
KEY_NEG_INF = INT_MIN + 0x007FFFFF
MAGNITUDE_BITS = 0x7FFFFFFF
MANTISSA_BITS = 23
BRACKET_BINADES = 3
BRACKET_STEPS = math.ceil(math.log2(BRACKET_BINADES * 2 ** MANTISSA_BITS + 1))
FULL_STEPS = math.ceil(math.log2(2 * KEY_POS_INF + 2))
SPLIT_STEPS = 26
BF16_SLACK = 1.02

ROW_TILE = 1024
KEY_CHUNK = 1024
NEAR_BLOCKS = 8
FOLD_CHAINS = 8
COUNT_ROWS = 512
SUM_FLOOR = 1e-30
CAND = 32
STREAMS = 2
NT_DIMS = (((1,), (1,)), ((), ()))


def _t5_bucket(dist):
    n = jnp.maximum(dist, 0)
    max_exact = NUM_BUCKETS // 2
    nf = jnp.maximum(n, 1).astype(F32)
    large = max_exact + (jnp.log(nf / max_exact) / math.log(REL_MAX_DIST / max_exact)
                         * (NUM_BUCKETS - max_exact)).astype(I32)
    large = jnp.minimum(large, NUM_BUCKETS - 1)
    return jnp.where(n < max_exact, n, large)


def _silu(x):
    return x * jax.nn.sigmoid(x)


def _rms(x, g):
    ms = jnp.mean(x * x, axis=-1, keepdims=True)
    return x * lax.rsqrt(ms + EPS) * g


def _mm(a, b):
    return jnp.dot(a, b, preferred_element_type=F32)


def _mm_nt(a, b):
    return lax.dot_general(a, b, NT_DIMS, preferred_element_type=F32)


def _fold(x, op):
    n = x.shape[0] // SUBLANES
    chains = min(FOLD_CHAINS, n)
    accs = [x[r * SUBLANES:(r + 1) * SUBLANES] for r in range(chains)]
    for r in range(chains, n):
        accs[r % chains] = op(accs[r % chains], x[r * SUBLANES:(r + 1) * SUBLANES])
    while len(accs) > 1:
        accs = [op(a, b) for a, b in zip(accs[::2], accs[1::2])] + accs[len(accs) & ~1:]
    return accs[0]


def _params(*sem):
    return pltpu.CompilerParams(dimension_semantics=sem, vmem_limit_bytes=VMEM_LIMIT)


def _const_spec(shape):
    zeros = (0,) * len(shape)
    return pl.BlockSpec(shape, lambda *_: zeros)


def _proj0_kernel(x_ref, g_ref, w_ref, qg_ref, kg_ref, q_ref, k_ref, v_ref, sga_ref, u_ref, sgb_ref):
    hn = _rms(x_ref[...], g_ref[...]).astype(BF16)
    lo = lax.broadcasted_iota(I32, (1, LANES), 1) < A_HEAD_DIM

    def mm(n):
        return _mm(hn, w_ref[:, n * A_WIDTH:(n + 1) * A_WIDTH])

    def segnorm(x, g2):
        sq = x * x
        s_lo = jnp.sum(jnp.where(lo, sq, 0.0), axis=-1, keepdims=True)
        s_hi = jnp.sum(jnp.where(lo, 0.0, sq), axis=-1, keepdims=True)
        inv = jnp.where(lo, lax.rsqrt(s_lo / A_HEAD_DIM + EPS), lax.rsqrt(s_hi / A_HEAD_DIM + EPS))
        return x * inv * g2

    q, k = mm(0), mm(1)
    for p in range(A_WIDTH // LANES):
        sl = slice(p * LANES, (p + 1) * LANES)
        q_ref[:, sl] = (segnorm(q[:, sl], qg_ref[...]) * (A_HEAD_DIM ** -0.5)).astype(BF16)
        k_ref[:, sl] = segnorm(k[:, sl], kg_ref[...]).astype(BF16)
    v_ref[...] = mm(2).astype(BF16)
    sga_ref[...] = _silu(mm(3)).astype(BF16)
    u_ref[...] = mm(4)
    sgb_ref[...] = _silu(mm(5)).astype(BF16)


def _proj0(x2, g, w, qg2, kg2):
    rows, d = x2.shape
    t = ROW_TILE

    def row(n):
        return pl.BlockSpec((t, n), lambda i: (i, 0))

    n = A_WIDTH
    return pl.pallas_call(
        _proj0_kernel,
        grid=(rows // t,),
        in_specs=[row(d), _const_spec((1, d)), _const_spec(w.shape), _const_spec(qg2.shape), _const_spec(kg2.shape)],
        out_specs=[row(n)] * 6,
        out_shape=[jax.ShapeDtypeStruct((rows, n), BF16), jax.ShapeDtypeStruct((rows, n), BF16),
                   jax.ShapeDtypeStruct((rows, n), BF16), jax.ShapeDtypeStruct((rows, n), BF16),
                   jax.ShapeDtypeStruct((rows, n), F32), jax.ShapeDtypeStruct((rows, n), BF16)],
        compiler_params=_params("arbitrary"),
        name="proj0",
    )(x2, g, w, qg2, kg2)


def _attn0_kernel(q_ref, kc_ref, kp_ref, vc_ref, vp_ref, sga_ref, bias_ref, sink_ref, ones_ref, o_ref):
    i = pl.program_id(1)
    kb = jnp.concatenate([kp_ref[0], kc_ref[0]], axis=0)
    vb = jnp.concatenate([vp_ref[0], vc_ref[0]], axis=0)

    def variant(x, g, a):
        n = 2 * g + a
        return x[:, n * LANES:(n + 1) * LANES]

    row = lax.broadcasted_iota(I32, (BLOCK, 2 * BLOCK), 0)
    col = lax.broadcasted_iota(I32, (BLOCK, 2 * BLOCK), 1)
    d = row + BLOCK - col
    mask = (d >= 0) & (d < WINDOW) & ((i > 0) | (col >= BLOCK))

    lgs, sinks = [], []
    for p in range(A_HEADS // 2):
        qp = q_ref[0, :, p * LANES:(p + 1) * LANES]
        for a in range(2):
            h = 2 * p + a
            lgs.append(jnp.where(mask, _mm_nt(qp, variant(kb, p // 2, a)) + bias_ref[h], NEG_INF))
            sinks.append(jnp.broadcast_to(sink_ref[h:h + 1, 0:1], (BLOCK, 1)))
    lg = jnp.concatenate(lgs, axis=0)
    sink = jnp.concatenate(sinks, axis=0)
    m = jnp.maximum(jnp.max(lg, axis=-1, keepdims=True), sink)
    e = jnp.exp(lg - m).astype(BF16)
    inv = 1.0 / (_mm(e, ones_ref[...]) + jnp.exp(sink - m))
    for p in range(A_HEADS // 2):
        sl = slice(p * LANES, (p + 1) * LANES)
        acc = jnp.zeros((BLOCK, LANES), F32)
        for a in range(2):
            h = 2 * p + a
            hs = slice(h * BLOCK, (h + 1) * BLOCK)
            acc = acc + _mm(e[hs], variant(vb, p // 2, a)) * inv[hs]
        o_ref[0, :, sl] = (acc * sga_ref[0, :, sl].astype(F32)).astype(BF16)


def _attn0(q, k, v, sga, bias0, sinks):
    b, l, _ = q.shape
    nb = l // BLOCK
    ones = jnp.ones((2 * BLOCK, LANES), BF16)

    def cur(n):
        return pl.BlockSpec((1, BLOCK, n), lambda bb, i: (bb, i, 0))

    def prev(n):
        return pl.BlockSpec((1, BLOCK, n), lambda bb, i: (bb, jnp.maximum(i - 1, 0), 0))

    return pl.pallas_call(
        _attn0_kernel,
        grid=(b, nb),
        in_specs=[cur(512), cur(512), prev(512), cur(512), prev(512), cur(512),
                  _const_spec(bias0.shape), _const_spec(sinks.shape), _const_spec(ones.shape)],
        out_specs=cur(512),
        out_shape=jax.ShapeDtypeStruct((b, l, 512), BF16),
        compiler_params=_params("arbitrary", "arbitrary"),
        name="attn0",
    )(q, k, k, v, v, sga, bias0, sinks, ones)


def _ssm_kernel(u_ref, sgb_ref, bmat_ref, cre_ref, cim_ref, sc_ref, d_ref, gw_ref, gb_ref, o_ref, xre_ref, xim_ref):
    t = u_ref.shape[1]
    nq = bmat_ref.shape[0]
    half = bmat_ref.shape[2] // 2

    @pl.when(pl.program_id(1) == 0)
    def _():
        xre_ref[0:SUBLANES, :] = jnp.zeros((SUBLANES, xre_ref.shape[1]), F32)
        xim_ref[0:SUBLANES, :] = jnp.zeros((SUBLANES, xim_ref.shape[1]), F32)

    u = u_ref[0]
    ub = u.astype(BF16)
    for q in range(nq):
        bu = _mm(ub[:, q * LANES:(q + 1) * LANES], bmat_ref[q])
        xre_ref[SUBLANES:, q * half:(q + 1) * half] = bu[:, :half]
        xim_ref[SUBLANES:, q * half:(q + 1) * half] = bu[:, half:]

    def scan(r, _):
        base = pl.multiple_of(SUBLANES + r * SUBLANES, SUBLANES)
        xr = xre_ref[pl.ds(base, SUBLANES), :]
        xi = xim_ref[pl.ds(base, SUBLANES), :]
        for s, k in enumerate((1, 2, 4)):
            ar = sc_ref[2 * s]
            ai = sc_ref[2 * s + 1]
            sr = pltpu.roll(xr, k, axis=0)
            si = pltpu.roll(xi, k, axis=0)
            xr, xi = xr + ar * sr - ai * si, xi + ar * si + ai * sr
        cr = xre_ref[pl.ds(base - 1, 1), :]
        ci = xim_ref[pl.ds(base - 1, 1), :]
        pr = sc_ref[6]
        pi = sc_ref[7]
        xre_ref[pl.ds(base, SUBLANES), :] = xr + pr * cr - pi * ci
        xim_ref[pl.ds(base, SUBLANES), :] = xi + pr * ci + pi * cr
        return 0

    lax.fori_loop(0, t // SUBLANES, scan, 0, unroll=2)
    xre_ref[0:SUBLANES, :] = xre_ref[t:t + SUBLANES, :]
    xim_ref[0:SUBLANES, :] = xim_ref[t:t + SUBLANES, :]

    ys = []
    for q in range(nq):
        xr = xre_ref[SUBLANES:, q * half:(q + 1) * half].astype(BF16)
        xi = xim_ref[SUBLANES:, q * half:(q + 1) * half].astype(BF16)
        ys.append(_mm(xr, cre_ref[q]) + _mm(xi, cim_ref[q]))
    y = jnp.concatenate(ys, axis=1) + d_ref[...] * u
    y = jax.nn.gelu(y).astype(BF16)
    hh = _mm(y, gw_ref[...]) + gb_ref[...]
    w = hh.shape[1] // 2
    o_ref[0] = (hh[:, :w] * jax.nn.sigmoid(hh[:, w:]) * sgb_ref[0].astype(F32)).astype(BF16)


def _ssm(u, sgb, bmat, cre, cim, sc, dskip, gw, gb):
    b, l, w = u.shape
    t = ROW_TILE
    ns = sc.shape[-1]

    def row(n):
        return pl.BlockSpec((1, t, n), lambda bb, i: (bb, i, 0))

    return pl.pallas_call(
        _ssm_kernel,
        grid=(b, l // t),
        in_specs=[row(w), row(w), _const_spec(bmat.shape), _const_spec(cre.shape), _const_spec(cim.shape),
                  _const_spec(sc.shape), _const_spec(dskip.shape), _const_spec(gw.shape), _const_spec(gb.shape)],
        out_specs=row(w),
        out_shape=jax.ShapeDtypeStruct((b, l, w), BF16),
        scratch_shapes=[pltpu.VMEM((SUBLANES + t, ns), F32), pltpu.VMEM((SUBLANES + t, ns), F32)],
        compiler_params=_params("arbitrary", "arbitrary"),
        name="ssm",
    )(u, sgb, bmat, cre, cim, sc, dskip, gw, gb)


def _s5_prep(log_dt, a_re, a_im, b_re, b_im, c_re, c_im):
    g, p = a_re.shape
    h = b_re.shape[-1]
    gl = LANES // h
    nq = g // gl
    dt = jnp.exp(log_dt)[:, None]
    mag = jnp.exp(a_re * dt)
    ang = a_im * dt
    ab_re = mag * jnp.cos(ang)
    ab_im = mag * jnp.sin(ang)
    den = a_re * a_re + a_im * a_im
    n_re = ab_re - 1.0
    n_im = ab_im
    f_re = (n_re * a_re + n_im * a_im) / den
    f_im = (n_im * a_re - n_re * a_im) / den
    bb_re = f_re[..., None] * b_re - f_im[..., None] * b_im
    bb_im = f_re[..., None] * b_im + f_im[..., None] * b_re
    eye = jnp.eye(gl, dtype=F32)

    def bdiag_in(m):
        m = m.reshape(nq, gl, p, h)
        return jnp.einsum('qgph,gk->qghkp', m, eye).reshape(nq, gl * h, gl * p)

    def bdiag_out(m):
        m = m.reshape(nq, gl, h, p)
        return jnp.einsum('qghp,gk->qgpkh', m, eye).reshape(nq, gl * p, gl * h)

    bmat = jnp.concatenate([bdiag_in(bb_re), bdiag_in(bb_im)], axis=2).astype(BF16)
    cre = bdiag_out(c_re).astype(BF16)
    cim = bdiag_out(-c_im).astype(BF16)

    pw = [(ab_re.reshape(-1), ab_im.reshape(-1))]
    for _ in range(SUBLANES - 1):
        pr, pi = pw[-1]
        pw.append((pr * pw[0][0] - pi * pw[0][1], pr * pw[0][1] + pi * pw[0][0]))
    rows = jnp.arange(SUBLANES)[:, None]
    sc = []
    for k in (1, 2, 4):
        sc.append(jnp.where(rows >= k, pw[k - 1][0][None, :], 0.0))
        sc.append(jnp.where(rows >= k, pw[k - 1][1][None, :], 0.0))
    sc.append(jnp.stack([pw[r][0] for r in range(SUBLANES)]))
    sc.append(jnp.stack([pw[r][1] for r in range(SUBLANES)]))
    return bmat, cre, cim, jnp.stack(sc).astype(F32)


def _mid_kernel(x_ref, a_ref, s_ref, wo_ref, g_ref, wq_ref, wk_ref, wvt_ref, wg_ref, wqi_ref, wki_ref, wwt_ref,
                qg_ref, kg_ref, h_ref, q_ref, k_ref, vt_ref, sg_ref, qi_ref, ki_ref, wt_ref):
    aw = a_ref.shape[2]
    h = x_ref[0] + _mm(a_ref[0], wo_ref[0:aw, :]) + _mm(s_ref[0], wo_ref[aw:, :])
    h_ref[0] = h
    hn = _rms(h, g_ref[...]).astype(BF16)
    qf = _mm(hn, wq_ref[...])
    for hd in range(C_HEADS):
        sl = slice(hd * C_HEAD_DIM, (hd + 1) * C_HEAD_DIM)
        q_ref[0, :, sl] = (_rms(qf[:, sl], qg_ref[...]) * (C_HEAD_DIM ** -0.5)).astype(BF16)
    kf = _mm(hn, wk_ref[...])
    for hd in range(C_KV_HEADS):
        sl = slice(hd * C_HEAD_DIM, (hd + 1) * C_HEAD_DIM)
        k_ref[0, :, sl] = _rms(kf[:, sl], kg_ref[...]).astype(BF16)
    vt_ref[0] = _mm_nt(wvt_ref[...], hn).astype(BF16)
    sg_ref[0] = _silu(_mm(hn, wg_ref[...])).astype(BF16)
    qi_ref[0] = _mm(hn, wqi_ref[...]).astype(BF16)
    ki_ref[0] = _mm(hn, wki_ref[...]).astype(BF16)
    wt_ref[0] = _mm_nt(wwt_ref[...], hn) * ((IDX_HEADS ** -0.5) * (IDX_DIM ** -0.5))


def _mid(x, att0, ssm0, wo, g, wq, wk, wvt, wg, wqi, wki2, wwt, qg, kg):
    b, l, d = x.shape
    t = ROW_TILE

    def row(n):
        return pl.BlockSpec((1, t, n), lambda bb, i: (bb, i, 0))

    def col(n):
        return pl.BlockSpec((1, n, t), lambda bb, i: (bb, 0, i))

    weights = [wo, g, wq, wk, wvt, wg, wqi, wki2, wwt, qg, kg]
    cw = C_HEADS * C_HEAD_DIM
    ckv = C_KV_HEADS * C_HEAD_DIM
    return pl.pallas_call(
        _mid_kernel,
        grid=(b, l // t),
        in_specs=[row(d), row(att0.shape[2]), row(ssm0.shape[2])] + [_const_spec(w.shape) for w in weights],
        out_specs=[row(d), row(cw), row(ckv), col(ckv), row(cw), row(IDX_HEADS * IDX_DIM), row(2 * LANES),
                   col(IDX_HEADS)],
        out_shape=[jax.ShapeDtypeStruct((b, l, d), F32), jax.ShapeDtypeStruct((b, l, cw), BF16),
                   jax.ShapeDtypeStruct((b, l, ckv), BF16), jax.ShapeDtypeStruct((b, ckv, l), BF16),
                   jax.ShapeDtypeStruct((b, l, cw), BF16), jax.ShapeDtypeStruct((b, l, IDX_HEADS * IDX_DIM), BF16),
                   jax.ShapeDtypeStruct((b, l, 2 * LANES), BF16), jax.ShapeDtypeStruct((b, IDX_HEADS, l), F32)],
        compiler_params=_params("arbitrary", "arbitrary"),
        name="mid",
    )(x, att0, ssm0, *weights)


def _dsa_kernel(q_ref, qi_ref, wt_ref, sg_ref, k_ref, vt_ref, ki_ref, tab_ref, lb_ref, o_ref,
                sc_ref, best_ref, x_ref, acc_ref, ring_a_ref, ring_b_ref, mb_ref, *, topk):
    i = pl.program_id(1)
    ck = KEY_CHUNK
    per = ck // BLOCK
    nch = (i + per) // per
    t_row = i * BLOCK + lax.broadcasted_iota(I32, (1, LANES), 1)
    kiota = lax.broadcasted_iota(I32, (ck, LANES), 0)

    def chunk_off(c):
        return pl.multiple_of(c * ck, ck)

    qi = qi_ref[0]
    qi_stack = [jnp.concatenate([qi[:, (2 * s) * LANES:(2 * s + 1) * LANES],
                                 qi[:, (2 * s + 1) * LANES:(2 * s + 2) * LANES]], axis=0) for s in range(2)]
    wt = wt_ref[0]

    def score_chunk(c, masked):
        off = chunk_off(c)
        sc = jnp.zeros((ck, LANES), F32)
        for a in range(2):
            kk = ki_ref[0, pl.ds(off, ck), a * LANES:(a + 1) * LANES]
            for s in range(2):
                r = _mm_nt(kk, qi_stack[s])
                for j in range(2):
                    hd = 2 * (2 * s + j) + a
                    sc = sc + jnp.maximum(r[:, j * LANES:(j + 1) * LANES], 0.0) * wt[hd:hd + 1, :]
        if masked:
            sc = jnp.where(off + kiota <= t_row, sc, NEG_INF)
        sc_ref[pl.ds(off, ck), :] = sc
        return _fold(sc, jnp.maximum)

    def exchange(v, a, b):
        v[a], v[b] = jnp.maximum(v[a], v[b]), jnp.minimum(v[a], v[b])

    def sort_desc(v):
        n, k = len(v), 2
        while k <= n:
            j = k // 2
            while j >= 1:
                for a in range(n):
                    b = a ^ j
                    if b > a:
                        exchange(v, *((a, b) if (a & k) == 0 else (b, a)))
                j //= 2
            k *= 2

    def merge_top(best, blk):
        n = len(best)
        v = [jnp.maximum(best[r], blk[n - 1 - r]) for r in range(n)]
        j = n // 2
        while j >= 1:
            for a in range(n):
                if a ^ j > a:
                    exchange(v, a, a ^ j)
            j //= 2
        return v

    crow = STREAMS * CAND * SUBLANES

    def cand_step(row0):
        blk_all = sc_ref[pl.ds(pl.multiple_of(row0, crow), crow), :]
        for st in range(STREAMS):
            blk = [blk_all[(STREAMS * r + st) * SUBLANES:(STREAMS * r + st + 1) * SUBLANES] for r in range(CAND)]
            sort_desc(blk)
            base = st * CAND * SUBLANES
            best = [best_ref[base + r * SUBLANES:base + (r + 1) * SUBLANES, :] for r in range(CAND)]
            for r, x in enumerate(merge_top(best, blk)):
                best_ref[base + r * SUBLANES:base + (r + 1) * SUBLANES, :] = x

    def score_body(c, mx):
        return jnp.maximum(jnp.maximum(mx, score_chunk(2 * c, False)), score_chunk(2 * c + 1, False))

    below = nch - 1
    smax = lax.fori_loop(0, below // 2, score_body, jnp.full((SUBLANES, LANES), NEG_INF, F32))
    smax = lax.cond(below % 2 == 1, lambda: jnp.maximum(smax, score_chunk(below - 1, False)), lambda: smax)
    smax = jnp.max(jnp.maximum(smax, score_chunk(nch - 1, True)), axis=0, keepdims=True)

    def count(*preds):
        rows = COUNT_ROWS
        sub = ck // rows

        def body(c, accs):
            out = []
            for u in range(sub):
                off = pl.multiple_of(c * ck + u * rows, rows)
                s = sc_ref[pl.ds(off, rows), :]
                for n, pred in enumerate(preds):
                    ind = pred(s, off).astype(I32)
                    out.append(accs[u * len(preds) + n]
                               + jnp.sum(ind.reshape(rows // SUBLANES, SUBLANES, LANES), axis=0))
            return tuple(out)

        accs = lax.fori_loop(0, nch, body, tuple(jnp.zeros((SUBLANES, LANES), I32) for _ in range(sub * len(preds))))
        res = [jnp.sum(sum(accs[n::len(preds)]), axis=0, keepdims=True) for n in range(len(preds))]
        return res[0] if len(preds) == 1 else res

    def key_value(k):
        return pltpu.bitcast(k ^ ((k >> 31) & MAGNITUDE_BITS), F32)

    def count_ge(k):
        thr = key_value(k)
        return count(lambda s, off: s >= thr)

    def full(v):
        return jnp.full((1, LANES), v, I32)

    def bisect(_, st):
        lo, hi, c_lo, c_hi = st
        mid = (lo >> 1) + (hi >> 1) + (lo & hi & 1)
        c = count_ge(mid)
        ge = c >= topk
        return jnp.where(ge, mid, lo), jnp.where(ge, hi, mid), jnp.where(ge, c, c_lo), jnp.where(ge, c_hi, c)

    searching = (i + 1) * BLOCK > topk

    def float_key(x):
        bits = pltpu.bitcast(x, I32)
        return bits ^ ((bits >> 31) & MAGNITUDE_BITS)

    def search():
        k_lo = float_key(smax * 2.0 ** -BRACKET_BINADES)
        c = count_ge(k_lo)
        ok = (smax > 0.0) & (c >= topk)
        trips = jnp.where(jnp.min(jnp.where(ok, 1.0, 0.0)) > 0.0, BRACKET_STEPS, FULL_STEPS)
        st = (jnp.where(ok, k_lo, KEY_NEG_INF), float_key(smax) + 1, jnp.where(ok, c, nch * ck), full(0))
        out = lax.fori_loop(0, trips, bisect, st)
        return out[0], out[2], out[3]

    def cand_search():
        best_ref[...] = jnp.full(best_ref.shape, -jnp.inf, F32)

        def cand_body(c, _):
            cand_step(c * crow)
            return 0

        lax.fori_loop(0, (i + crow // BLOCK) // (crow // BLOCK), cand_body, 0)

        def all_sublanes(x):
            for shift in (4, 2, 1):
                x = x + pltpu.roll(x, shift, axis=0)
            return x

        def count_cand(k):
            thr = key_value(k)[None]
            parts = [jnp.sum((best_ref[r:r + COUNT_ROWS, :].reshape(COUNT_ROWS // SUBLANES, SUBLANES, LANES)
                              >= thr).astype(I32), axis=0) for r in range(0, crow, COUNT_ROWS)]
            return all_sublanes(sum(parts))

        def step(_, st):
            lo, hi = st
            mid = (lo >> 1) + (hi >> 1) + (lo & hi & 1)
            take = ~((count_cand(mid) - topk) >> 31)
            return (mid & take) | (lo & ~take), (hi & take) | (mid & ~take)

        smax8 = jnp.broadcast_to(smax, (SUBLANES, LANES))
        k_lo = float_key(smax8 * 2.0 ** -BRACKET_BINADES)
        ok = (smax8 > 0.0) & (count_cand(k_lo) >= topk)
        trips = jnp.where(jnp.min(jnp.where(ok, 1.0, 0.0)) > 0.0, BRACKET_STEPS, FULL_STEPS)
        vk8, _ = lax.fori_loop(0, trips, step, (jnp.where(ok, k_lo, KEY_NEG_INF), float_key(smax8) + 1))
        vk = vk8[0:1]
        thr = key_value(vk)
        above_cand = jnp.sum((best_ref[...] > thr).astype(I32), axis=0, keepdims=True)
        c_ge, c_gt = count(lambda s, off: s >= thr, lambda s, off: s > thr)
        complete = jnp.min(jnp.where(c_gt == above_cand, 1.0, 0.0)) > 0.0
        return lax.cond(complete, lambda: (vk, c_ge, c_gt), search)

    vkey, c_lo, c_hi = lax.cond(searching, cand_search, lambda: (full(KEY_NEG_INF), full(topk), full(0)))
    vthr = key_value(vkey)
    need = topk - c_hi
    ties = c_lo - c_hi

    def tie_search():
        def split_step():
            nxt = vkey + 1
            nxt = jnp.where((nxt > 0) & (nxt < KEY_MIN_NORMAL), KEY_MIN_NORMAL, nxt)
            step = key_value(nxt) - vthr

            def split(_, st):
                fl, fh = st
                fm = 0.5 * (fl + fh)
                t = vthr + fm * step
                ge = count(lambda s, off: s >= t) >= topk
                return jnp.where(ge, fm, fl), jnp.where(ge, fh, fm)

            fl, _ = lax.fori_loop(0, SPLIT_STEPS, split, (jnp.zeros((1, LANES), F32), jnp.ones((1, LANES), F32)))
            t = vthr + fl * step
            return t, topk - count(lambda s, off: s > t)

        inside = jnp.max(jnp.where(ties > count(lambda s, off: s == vthr), 1.0, 0.0)) > 0.0
        thr, want = lax.cond(inside, split_step, lambda: (vthr, need))

        rr = lax.broadcasted_iota(I32, (BLOCK, BLOCK), 0)
        cc = lax.broadcasted_iota(I32, (BLOCK, BLOCK), 1)
        tril = jnp.where(cc <= rr, 1.0, 0.0).astype(BF16)
        want_f = want.astype(F32)

        def body(c, before):
            off = chunk_off(c)
            blocks = [sc_ref[pl.ds(pl.multiple_of(off + r * BLOCK, BLOCK), BLOCK), :] for r in range(per)]
            hits = [s == thr for s in blocks]
            ranks = [_mm(tril, jnp.where(h, 1.0, 0.0).astype(BF16)) for h in hits]
            for r in range(per):
                rank = ranks[r] + before
                sc_ref[pl.ds(pl.multiple_of(off + r * BLOCK, BLOCK), BLOCK), :] = jnp.where(
                    hits[r] & (rank > want_f), NEG_INF, blocks[r])
                before = rank[BLOCK - 1:BLOCK, :]
            return before

        lax.fori_loop(0, nch, body, jnp.zeros((1, LANES), F32))
        return thr

    any_tie = searching & (jnp.max(jnp.where(ties > need, 1.0, 0.0)) > 0.0)
    vthr = lax.cond(any_tie, tie_search, lambda: vthr)

    def selection_mask(off):
        s = sc_ref[pl.ds(off, ck), :]
        s_idx = off + kiota
        sel = (s >= vthr) & (s_idx <= t_row)
        madd = jnp.where(sel, 0.0, NEG_INF)
        sc_ref[pl.ds(off, ck), :] = madd
        return madd

    q = q_ref[0]
    n_far = jnp.maximum((i - NEAR_BLOCKS + 1) // per, 0)
    hpg = C_HEADS // C_KV_HEADS
    npair = C_HEADS // 2
    q_pairs = [jnp.concatenate([q[:, (2 * j) * LANES:(2 * j + 1) * LANES],
                                q[:, (2 * j + 1) * LANES:(2 * j + 2) * LANES]], axis=0) for j in range(npair)]

    def bias_rows(hd, c):
        return jnp.concatenate([tab_ref[hd, jnp.clip(i - (c * per + r), 0, NEAR_BLOCKS)] for r in range(per)],
                               axis=0)

    def emit(hd, num, den):
        sl = slice(hd * LANES, (hd + 1) * LANES)
        o_ref[0, :, sl] = ((num / den).T * sg_ref[0, :, sl].astype(F32)).astype(BF16)

    def exact_attention():
        for g in range(C_KV_HEADS):
            def stage_body(near, g=g):
                def body(c, mx):
                    off = chunk_off(c)
                    madd = sc_ref[pl.ds(off, ck), :]
                    kc = k_ref[0, pl.ds(off, ck), g * LANES:(g + 1) * LANES]
                    out = []
                    for jj in range(hpg // 2):
                        lg = _mm_nt(kc, q_pairs[g * (hpg // 2) + jj])
                        for a in range(2):
                            hl = 2 * jj + a
                            x = lg[:, a * LANES:(a + 1) * LANES] + madd
                            if near:
                                x = x + bias_rows(hpg * g + hl, c)
                            x_ref[hl, pl.ds(off, ck), :] = x
                            out.append(jnp.maximum(mx[hl], _fold(x, jnp.maximum)))
                    return tuple(out)

                return body

            mx = tuple(jnp.full((SUBLANES, LANES), NEG_INF, F32) for _ in range(hpg))
            mx = lax.fori_loop(0, n_far, stage_body(False), mx)
            mx = lax.fori_loop(n_far, nch, stage_body(True), mx)
            m = [jnp.max(v, axis=0, keepdims=True) for v in mx]
            acc_ref[...] = jnp.zeros(acc_ref.shape, F32)

            def att_body(c, ls, g=g, m=m):
                off = chunk_off(c)
                vt = vt_ref[0, g * LANES:(g + 1) * LANES, pl.ds(off, ck)]
                out = []
                for jj in range(hpg // 2):
                    ps = []
                    for a in range(2):
                        hl = 2 * jj + a
                        p = jnp.exp(x_ref[hl, pl.ds(off, ck), :] - m[hl])
                        out.append(ls[hl] + _fold(p, jnp.add))
                        ps.append(p.astype(BF16))
                    acc_ref[jj] += _mm(vt, jnp.concatenate(ps, axis=1))
                return tuple(out)

            ls = lax.fori_loop(0, nch, att_body, tuple(jnp.zeros((SUBLANES, LANES), F32) for _ in range(hpg)))
            for hl in range(hpg):
                emit(hpg * g + hl, acc_ref[hl // 2, :, (hl % 2) * LANES:(hl % 2 + 1) * LANES],
                     jnp.sum(ls[hl], axis=0, keepdims=True))

    lb = lb_ref[...]
    acc_ref[...] = jnp.zeros(acc_ref.shape, F32)

    def stage_mask(c):
        mb_ref[...] = selection_mask(chunk_off(c)) - lb

    def stage_pair(c, j, ring_ref, near):
        g = j // (hpg // 2)
        kc = k_ref[0, pl.ds(chunk_off(c), ck), g * LANES:(g + 1) * LANES]
        lg = _mm_nt(kc, q_pairs[j])
        for a in range(2):
            hd = 2 * j + a
            x = lg[:, a * LANES:(a + 1) * LANES] + mb_ref[...]
            if near:
                x = x + bias_rows(hd, c)
            ring_ref[hd] = x

    def consume_pair(c, j, ring_ref, ls):
        g = j // (hpg // 2)
        vt = vt_ref[0, g * LANES:(g + 1) * LANES, pl.ds(chunk_off(c), ck)]
        ps = []
        for a in range(2):
            hd = 2 * j + a
            p = jnp.exp(ring_ref[hd])
            ls[hd] = ls[hd] + _fold(p, jnp.add)
            ps.append(p.astype(BF16))
        acc_ref[j] += _mm(vt, jnp.concatenate(ps, axis=1))

    def stage(c, ring_ref):
        stage_mask(c)
        for j in range(npair):
            stage_pair(c, j, ring_ref, True)

    def consume(c, ring_ref, ls, then_stage=None):
        ls = list(ls)
        if then_stage is not None:
            stage_mask(then_stage[0])
        for j in range(npair):
            if then_stage is not None:
                stage_pair(then_stage[0], j, *then_stage[1:])
            consume_pair(c, j, ring_ref, ls)
        return tuple(ls)

    def pair_step(near):
        def body(k, ls):
            c = 2 * k
            ls = consume(c, ring_a_ref, ls, (c + 1, ring_b_ref, near))
            return consume(c + 1, ring_b_ref, ls, (c + 2, ring_a_ref, near))

        return body

    stage(0, ring_a_ref)
    ls = tuple(jnp.zeros((SUBLANES, LANES), F32) for _ in range(C_HEADS))
    npairs = (nch - 1) // 2
    far_pairs = jnp.maximum((n_far - 1) // 2, 0)
    ls = lax.fori_loop(0, far_pairs, pair_step(False), ls)
    ls = lax.fori_loop(far_pairs, npairs, pair_step(True), ls)
    last = 2 * npairs

    def tail_two(ls):
        ls = consume(last, ring_a_ref, ls, (last + 1, ring_b_ref, True))
        return consume(last + 1, ring_b_ref, ls)

    ls = lax.cond(nch - 1 > last, tail_two, lambda ls: consume(last, ring_a_ref, ls), ls)
    dens = [jnp.sum(v, axis=0, keepdims=True) for v in ls]
    in_range = jnp.min(functools.reduce(jnp.minimum, dens)) > SUM_FLOOR

    @pl.when(in_range)
    def _():
        for hd in range(C_HEADS):
            emit(hd, acc_ref[hd // 2, :, (hd % 2) * LANES:(hd % 2 + 1) * LANES], dens[hd])

    @pl.when(jnp.logical_not(in_range))
    def _():
        exact_attention()


def _dsa(q, qi, wt, sg, k, vt, ki2, tab, lb):
    b, l, cw = q.shape
    nb = l // BLOCK
    topk = min(TOPK_MAX, l // 4)

    def blk(n):
        return pl.BlockSpec((1, BLOCK, n), lambda bb, i: (bb, i, 0))

    def whole(s1, s2):
        return pl.BlockSpec((1, s1, s2), lambda bb, i: (bb, 0, 0), pipeline_mode=pl.Buffered(1))

    hpg = C_HEADS // C_KV_HEADS
    return pl.pallas_call(
        functools.partial(_dsa_kernel, topk=topk),
        grid=(b, nb),
        in_specs=[blk(cw), blk(qi.shape[2]), pl.BlockSpec((1, IDX_HEADS, BLOCK), lambda bb, i: (bb, 0, i)), blk(cw),
                  whole(l, k.shape[2]), whole(vt.shape[1], l), whole(l, ki2.shape[2]),
                  pl.BlockSpec(tab.shape, lambda bb, i: (0, 0, 0, 0), pipeline_mode=pl.Buffered(1)),
                  _const_spec(lb.shape)],
        out_specs=blk(cw),
        out_shape=jax.ShapeDtypeStruct((b, l, cw), BF16),
        scratch_shapes=[pltpu.VMEM((l, LANES), F32), pltpu.VMEM((STREAMS * CAND * SUBLANES, LANES), F32),
                        pltpu.VMEM((hpg, l, LANES), F32),
                        pltpu.VMEM((C_HEADS // 2, C_HEAD_DIM, 2 * LANES), F32),
                        pltpu.VMEM((C_HEADS, KEY_CHUNK, LANES), F32), pltpu.VMEM((C_HEADS, KEY_CHUNK, LANES), F32),
                        pltpu.VMEM((KEY_CHUNK, LANES), F32)],
        compiler_params=_params("arbitrary", "arbitrary"),
        name="dsa",
    )(q, qi, wt, sg, k, vt, ki2, tab, lb)


def _out_kernel(h_ref, a_ref, w_ref, o_ref):
    o_ref[...] = h_ref[...] + _mm(a_ref[...], w_ref[...])


def _outproj(h2, a2, w):
    rows, d = h2.shape
    t = ROW_TILE
    return pl.pallas_call(
        _out_kernel,
        grid=(rows // t,),
        in_specs=[pl.BlockSpec((t, d), lambda i: (i, 0)), pl.BlockSpec((t, a2.shape[1]), lambda i: (i, 0)),
                  _const_spec(w.shape)],
        out_specs=pl.BlockSpec((t, d), lambda i: (i, 0)),
        out_shape=jax.ShapeDtypeStruct((rows, d), F32),
        compiler_params=_params("arbitrary"),
        name="outproj1",
    )(h2, a2, w)


def _bias_tables(rel_bias):
    nv = (NEAR_BLOCKS + 1) * BLOCK
    vec = rel_bias[_t5_bucket(jnp.arange(nv, dtype=I32))].astype(F32).T

    nh = vec.shape[0]
    vecp = jnp.concatenate([jnp.broadcast_to(vec[:, :1], (nh, BLOCK - 1)), vec], axis=1)

    def toeplitz(g, rows, cols):
        w = rows + cols
        g2 = jnp.concatenate([g[..., rows - 1:rows - 1 + cols], g[..., :1], g[..., :rows - 1]], axis=-1)
        flat = jnp.tile(g2, (1,) * (g.ndim - 1) + (rows,))[..., :rows * (w - 1)]
        return flat.reshape(g.shape[:-1] + (rows, w - 1))[..., :cols]

    bias0 = jnp.transpose(toeplitz(vecp[:, :3 * BLOCK - 1], 2 * BLOCK, BLOCK), (0, 2, 1))
    r = vecp[:, :(NEAR_BLOCKS + 1) * BLOCK].reshape(nh, NEAR_BLOCKS + 1, BLOCK)
    wins = jnp.concatenate([r[:, :-1], r[:, 1:, :BLOCK - 1]], axis=2)
    tab = toeplitz(wins, BLOCK, BLOCK) - rel_bias[NUM_BUCKETS - 1].astype(F32)[:, None, None, None]
    tab = jnp.concatenate([tab, jnp.zeros((tab.shape[0], 1, BLOCK, BLOCK), F32)], axis=1)
    return bias0, tab


def kernel(x, rel_bias, norm_g, ev_w_in, ev_w_out, ev_q_norm_g, ev_k_norm_g, ev_sinks, ev_ssm_log_dt, ev_ssm_a_re,
           ev_ssm_a_im, ev_ssm_b_re, ev_ssm_b_im, ev_ssm_c_re, ev_ssm_c_im, ev_ssm_d, ev_glu_w, ev_glu_b, od_w_in,
           od_w_out, od_q_norm_g, od_k_norm_g):
    b, l, d = x.shape
    assert l % KEY_CHUNK == 0 and l % ROW_TILE == 0
    assert (NEAR_BLOCKS - 1) * BLOCK + 1 >= 16 * 64 ** (15 / 16) + 1
    bias0, tab = _bias_tables(rel_bias)

    w0 = ev_w_in[0]
    hd = A_HEAD_DIM
    o0 = np.cumsum([0, A_WIDTH, A_KV_HEADS * hd, A_KV_HEADS * hd, A_WIDTH, A_WIDTH, A_WIDTH])
    wq0, wk0, wv0, wga, wu, wgb = (w0[:, o0[n]:o0[n + 1]] for n in range(6))
    z = jnp.zeros((d, hd), w0.dtype)

    def variants(w):
        return jnp.concatenate([c for g in range(A_KV_HEADS) for c in (w[:, g * hd:(g + 1) * hd], z, z,
                                                                       w[:, g * hd:(g + 1) * hd])], axis=1)

    w0x = jnp.concatenate([wq0, variants(wk0), variants(wv0), wga, wu, wgb], axis=1).astype(BF16)
    qg2 = jnp.tile(ev_q_norm_g[0], 2)[None, :]
    kg2 = jnp.tile(ev_k_norm_g[0], 2)[None, :]
    q0, k0, v0, sga, u, sgb = _proj0(x.reshape(b * l, d), norm_g[0][None, :], w0x, qg2, kg2)
    shp = lambda a: a.reshape(b, l, a.shape[-1])
    sinks = jnp.broadcast_to(ev_sinks[0][:, None], (A_HEADS, LANES)).astype(F32)
    att0 = _attn0(shp(q0), shp(k0), shp(v0), shp(sga), bias0, sinks)
    bmat, cre, cim, sc = _s5_prep(ev_ssm_log_dt[0], ev_ssm_a_re[0], ev_ssm_a_im[0], ev_ssm_b_re[0], ev_ssm_b_im[0],
                                  ev_ssm_c_re[0], ev_ssm_c_im[0])
    ssm0 = _ssm(shp(u), shp(sgb), bmat, cre, cim, sc, ev_ssm_d[0].reshape(1, -1), ev_glu_w[0].astype(BF16),
                ev_glu_b[0][None, :])

    w1 = od_w_in[0]
    cw = C_HEADS * C_HEAD_DIM
    ckv = C_KV_HEADS * C_HEAD_DIM
    o = np.cumsum([0, cw, ckv, ckv, cw, IDX_HEADS * IDX_DIM, IDX_DIM, IDX_HEADS])
    wq, wk, wv, wg, wqi, wki, ww = (w1[:, o[n]:o[n + 1]] for n in range(7))
    zki = jnp.zeros((d, LANES - IDX_DIM), w1.dtype)
    wki2 = jnp.concatenate([wki, zki, zki, wki], axis=1)
    bf = lambda a: a.astype(BF16)
    h1, q1, k1, vt1, sg1, qi1, ki2, wt1 = _mid(
        x, att0, ssm0, bf(ev_w_out[0]), norm_g[1][None, :], bf(wq), bf(wk), bf(wv.T), bf(wg), bf(wqi), bf(wki2),
        bf(ww.T), od_q_norm_g[0][None, :], od_k_norm_g[0][None, :])
    lb = (BF16_SLACK * C_HEAD_DIM ** 0.5 * jnp.max(jnp.abs(od_q_norm_g[0])) * jnp.max(jnp.abs(od_k_norm_g[0]))
          + jnp.max(tab))
    att1 = _dsa(q1, qi1, wt1, sg1, k1, vt1, ki2, tab, jnp.full((1, LANES), lb, F32))
    out = _outproj(h1.reshape(b * l, d), att1.reshape(b * l, cw), bf(od_w_out[0]))
    return out.reshape(b, l, d)
```

```python
import functools
import math

import jax
import jax.numpy as jnp
import numpy as np
from jax import lax
from jax.experimental import pallas as pl
from jax.experimental.pallas import tpu as pltpu

F32 = jnp.float32
BF16 = jnp.bfloat16
I32 = jnp.int32

LANES = 128
SUBLANES = 8
VMEM_LIMIT = 56 * 1024 * 1024

BLOCK = 128
WINDOW = 128
A_HEADS = 8
A_HEAD_DIM = 64
A_KV_HEADS = 2
A_WIDTH = A_HEADS * A_HEAD_DIM
SSM_GROUP = 16
SSM_STATE = 64
C_HEADS = 8
C_HEAD_DIM = 128
C_KV_HEADS = 2
IDX_HEADS = 8
IDX_DIM = 64
TOPK_MAX = 256
NUM_BUCKETS = 32
REL_MAX_DIST = 1024
EPS = 1e-6
NEG_INF = -1e30
INT_MIN = -(2 ** 31)
KEY_MIN_NORMAL = 0x00800000
KEY_POS_INF = 0x7F800000
KEY_NEG_INF = INT_MIN + 0x007FFFFF
MAGNITUDE_BITS = 0x7FFFFFFF
MANTISSA_BITS = 23
BRACKET_BINADES = 3
BRACKET_STEPS = math.ceil(math.log2(BRACKET_BINADES * 2 ** MANTISSA_BITS + 1))
FULL_STEPS = math.ceil(math.log2(2 * KEY_POS_INF + 2))
SPLIT_STEPS = 26
BF16_SLACK = 1.02

ROW_TILE = 1024
KEY_CHUNK = 1024
NEAR_BLOCKS = 8
FOLD_CHAINS = 8
COUNT_ROWS = 512
SUM_FLOOR = 1e-30
CAND = 32
STREAMS = 2
NT_DIMS = (((1,), (1,)), ((), ()))


def _t5_bucket(dist):
    n = jnp.maximum(dist, 0)
    max_exact = NUM_BUCKETS // 2
    nf = jnp.maximum(n, 1).astype(F32)
    large = max_exact + (jnp.log(nf / max_exact) / math.log(REL_MAX_DIST / max_exact)
                         * (NUM_BUCKETS - max_exact)).astype(I32)
    large = jnp.minimum(large, NUM_BUCKETS - 1)
    return jnp.where(n < max_exact, n, large)


def _silu(x):
    return x * jax.nn.sigmoid(x)


def _rms(x, g):
    ms = jnp.mean(x * x, axis=-1, keepdims=True)
    return x * lax.rsqrt(ms + EPS) * g


def _mm(a, b):
    return jnp.dot(a, b, preferred_element_type=F32)


def _mm_nt(a, b):
    return lax.dot_general(a, b, NT_DIMS, preferred_element_type=F32)


def _fold(x, op):
    n = x.shape[0] // SUBLANES
    chains = min(FOLD_CHAINS, n)
    accs = [x[r * SUBLANES:(r + 1) * SUBLANES] for r in range(chains)]
    for r in range(chains, n):
        accs[r % chains] = op(accs[r % chains], x[r * SUBLANES:(r + 1) * SUBLANES])
    while len(accs) > 1:
        accs = [op(a, b) for a, b in zip(accs[::2], accs[1::2])] + accs[len(accs) & ~1:]
    return accs[0]


def _params(*sem):
    return pltpu.CompilerParams(dimension_semantics=sem, vmem_limit_bytes=VMEM_LIMIT)


def _const_spec(shape):
    zeros = (0,) * len(shape)
    return pl.BlockSpec(shape, lambda *_: zeros)


def _proj0_kernel(x_ref, g_ref, w_ref, qg_ref, kg_ref, q_ref, k_ref, v_ref, sga_ref, u_ref, sgb_ref):
    hn = _rms(x_ref[...], g_ref[...]).astype(BF16)
    lo = lax.broadcasted_iota(I32, (1, LANES), 1) < A_HEAD_DIM

    def mm(n):
        return _mm(hn, w_ref[:, n * A_WIDTH:(n + 1) * A_WIDTH])

    def segnorm(x, g2):
        sq = x * x
        s_lo = jnp.sum(jnp.where(lo, sq, 0.0), axis=-1, keepdims=True)
        s_hi = jnp.sum(jnp.where(lo, 0.0, sq), axis=-1, keepdims=True)
        inv = jnp.where(lo, lax.rsqrt(s_lo / A_HEAD_DIM + EPS), lax.rsqrt(s_hi / A_HEAD_DIM + EPS))
        return x * inv * g2

    q, k = mm(0), mm(1)
    for p in range(A_WIDTH // LANES):
        sl = slice(p * LANES, (p + 1) * LANES)
        q_ref[:, sl] = (segnorm(q[:, sl], qg_ref[...]) * (A_HEAD_DIM ** -0.5)).astype(BF16)
        k_ref[:, sl] = segnorm(k[:, sl], kg_ref[...]).astype(BF16)
    v_ref[...] = mm(2).astype(BF16)
    sga_ref[...] = _silu(mm(3)).astype(BF16)
    u_ref[...] = mm(4)
    sgb_ref[...] = _silu(mm(5)).astype(BF16)


def _proj0(x2, g, w, qg2, kg2):
    rows, d = x2.shape
    t = ROW_TILE

    def row(n):
        return pl.BlockSpec((t, n), lambda i: (i, 0))

    n = A_WIDTH
    return pl.pallas_call(
        _proj0_kernel,
        grid=(rows // t,),
        in_specs=[row(d), _const_spec((1, d)), _const_spec(w.shape), _const_spec(qg2.shape), _const_spec(kg2.shape)],
        out_specs=[row(n)] * 6,
        out_shape=[jax.ShapeDtypeStruct((rows, n), BF16), jax.ShapeDtypeStruct((rows, n), BF16),
                   jax.ShapeDtypeStruct((rows, n), BF16), jax.ShapeDtypeStruct((rows, n), BF16),
                   jax.ShapeDtypeStruct((rows, n), F32), jax.ShapeDtypeStruct((rows, n), BF16)],
        compiler_params=_params("arbitrary"),
        name="proj0",
    )(x2, g, w, qg2, kg2)


def _attn0_kernel(q_ref, kc_ref, kp_ref, vc_ref, vp_ref, sga_ref, bias_ref, sink_ref, ones_ref, o_ref):
    i = pl.program_id(1)
    kb = jnp.concatenate([kp_ref[0], kc_ref[0]], axis=0)
    vb = jnp.concatenate([vp_ref[0], vc_ref[0]], axis=0)

    def variant(x, g, a):
        n = 2 * g + a
        return x[:, n * LANES:(n + 1) * LANES]

    row = lax.broadcasted_iota(I32, (BLOCK, 2 * BLOCK), 0)
    col = lax.broadcasted_iota(I32, (BLOCK, 2 * BLOCK), 1)
    d = row + BLOCK - col
    mask = (d >= 0) & (d < WINDOW) & ((i > 0) | (col >= BLOCK))

    lgs, sinks = [], []
    for p in range(A_HEADS // 2):
        qp = q_ref[0, :, p * LANES:(p + 1) * LANES]
        for a in range(2):
            h = 2 * p + a
            lgs.append(jnp.where(mask, _mm_nt(qp, variant(kb, p // 2, a)) + bias_ref[h], NEG_INF))
            sinks.append(jnp.broadcast_to(sink_ref[h:h + 1, 0:1], (BLOCK, 1)))
    lg = jnp.concatenate(lgs, axis=0)
    sink = jnp.concatenate(sinks, axis=0)
    m = jnp.maximum(jnp.max(lg, axis=-1, keepdims=True), sink)
    e = jnp.exp(lg - m).astype(BF16)
    inv = 1.0 / (_mm(e, ones_ref[...]) + jnp.exp(sink - m))
    for p in range(A_HEADS // 2):
        sl = slice(p * LANES, (p + 1) * LANES)
        acc = jnp.zeros((BLOCK, LANES), F32)
        for a in range(2):
            h = 2 * p + a
            hs = slice(h * BLOCK, (h + 1) * BLOCK)
            acc = acc + _mm(e[hs], variant(vb, p // 2, a)) * inv[hs]
        o_ref[0, :, sl] = (acc * sga_ref[0, :, sl].astype(F32)).astype(BF16)


def _attn0(q, k, v, sga, bias0, sinks):
    b, l, _ = q.shape
    nb = l // BLOCK
    ones = jnp.ones((2 * BLOCK, LANES), BF16)

    def cur(n):
        return pl.BlockSpec((1, BLOCK, n), lambda bb, i: (bb, i, 0))

    def prev(n):
        return pl.BlockSpec((1, BLOCK, n), lambda bb, i: (bb, jnp.maximum(i - 1, 0), 0))

    return pl.pallas_call(
        _attn0_kernel,
        grid=(b, nb),
        in_specs=[cur(512), cur(512), prev(512), cur(512), prev(512), cur(512),
                  _const_spec(bias0.shape), _const_spec(sinks.shape), _const_spec(ones.shape)],
        out_specs=cur(512),
        out_shape=jax.ShapeDtypeStruct((b, l, 512), BF16),
        compiler_params=_params("arbitrary", "arbitrary"),
        name="attn0",
    )(q, k, k, v, v, sga, bias0, sinks, ones)


def _ssm_kernel(u_ref, sgb_ref, bmat_ref, cre_ref, cim_ref, sc_ref, d_ref, gw_ref, gb_ref, o_ref, xre_ref, xim_ref):
    t = u_ref.shape[1]
    nq = bmat_ref.shape[0]
    half = bmat_ref.shape[2] // 2

    @pl.when(pl.program_id(1) == 0)
    def _():
        xre_ref[0:SUBLANES, :] = jnp.zeros((SUBLANES, xre_ref.shape[1]), F32)
        xim_ref[0:SUBLANES, :] = jnp.zeros((SUBLANES, xim_ref.shape[1]), F32)

    u = u_ref[0]
    ub = u.astype(BF16)
    for q in range(nq):
        bu = _mm(ub[:, q * LANES:(q + 1) * LANES], bmat_ref[q])
        xre_ref[SUBLANES:, q * half:(q + 1) * half] = bu[:, :half]
        xim_ref[SUBLANES:, q * half:(q + 1) * half] = bu[:, half:]

    def scan(r, _):
        base = pl.multiple_of(SUBLANES + r * SUBLANES, SUBLANES)
        xr = xre_ref[pl.ds(base, SUBLANES), :]
        xi = xim_ref[pl.ds(base, SUBLANES), :]
        for s, k in enumerate((1, 2, 4)):
            ar = sc_ref[2 * s]
            ai = sc_ref[2 * s + 1]
            sr = pltpu.roll(xr, k, axis=0)
            si = pltpu.roll(xi, k, axis=0)
            xr, xi = xr + ar * sr - ai * si, xi + ar * si + ai * sr
        cr = xre_ref[pl.ds(base - 1, 1), :]
        ci = xim_ref[pl.ds(base - 1, 1), :]
        pr = sc_ref[6]
        pi = sc_ref[7]
        xre_ref[pl.ds(base, SUBLANES), :] = xr + pr * cr - pi * ci
        xim_ref[pl.ds(base, SUBLANES), :] = xi + pr * ci + pi * cr
        return 0

    lax.fori_loop(0, t // SUBLANES, scan, 0, unroll=2)
    xre_ref[0:SUBLANES, :] = xre_ref[t:t + SUBLANES, :]
    xim_ref[0:SUBLANES, :] = xim_ref[t:t + SUBLANES, :]

    ys = []
    for q in range(nq):
        xr = xre_ref[SUBLANES:, q * half:(q + 1) * half].astype(BF16)
        xi = xim_ref[SUBLANES:, q * half:(q + 1) * half].astype(BF16)
        ys.append(_mm(xr, cre_ref[q]) + _mm(xi, cim_ref[q]))
    y = jnp.concatenate(ys, axis=1) + d_ref[...] * u
    y = jax.nn.gelu(y).astype(BF16)
    hh = _mm(y, gw_ref[...]) + gb_ref[...]
    w = hh.shape[1] // 2
    o_ref[0] = (hh[:, :w] * jax.nn.sigmoid(hh[:, w:]) * sgb_ref[0].astype(F32)).astype(BF16)


def _ssm(u, sgb, bmat, cre, cim, sc, dskip, gw, gb):
    b, l, w = u.shape
    t = ROW_TILE
    ns = sc.shape[-1]

    def row(n):
        return pl.BlockSpec((1, t, n), lambda bb, i: (bb, i, 0))

    return pl.pallas_call(
        _ssm_kernel,
        grid=(b, l // t),
        in_specs=[row(w), row(w), _const_spec(bmat.shape), _const_spec(cre.shape), _const_spec(cim.shape),
                  _const_spec(sc.shape), _const_spec(dskip.shape), _const_spec(gw.shape), _const_spec(gb.shape)],
        out_specs=row(w),
        out_shape=jax.ShapeDtypeStruct((b, l, w), BF16),
        scratch_shapes=[pltpu.VMEM((SUBLANES + t, ns), F32), pltpu.VMEM((SUBLANES + t, ns), F32)],
        compiler_params=_params("arbitrary", "arbitrary"),
        name="ssm",
    )(u, sgb, bmat, cre, cim, sc, dskip, gw, gb)


def _s5_prep(log_dt, a_re, a_im, b_re, b_im, c_re, c_im):
    g, p = a_re.shape
    h = b_re.shape[-1]
    gl = LANES // h
    nq = g // gl
    dt = jnp.exp(log_dt)[:, None]
    mag = jnp.exp(a_re * dt)
    ang = a_im * dt
    ab_re = mag * jnp.cos(ang)
    ab_im = mag * jnp.sin(ang)
    den = a_re * a_re + a_im * a_im
    n_re = ab_re - 1.0
    n_im = ab_im
    f_re = (n_re * a_re + n_im * a_im) / den
    f_im = (n_im * a_re - n_re * a_im) / den
    bb_re = f_re[..., None] * b_re - f_im[..., None] * b_im
    bb_im = f_re[..., None] * b_im + f_im[..., None] * b_re
    eye = jnp.eye(gl, dtype=F32)

    def bdiag_in(m):
        m = m.reshape(nq, gl, p, h)
        return jnp.einsum('qgph,gk->qghkp', m, eye).reshape(nq, gl * h, gl * p)

    def bdiag_out(m):
        m = m.reshape(nq, gl, h, p)
        return jnp.einsum('qghp,gk->qgpkh', m, eye).reshape(nq, gl * p, gl * h)

    bmat = jnp.concatenate([bdiag_in(bb_re), bdiag_in(bb_im)], axis=2).astype(BF16)
    cre = bdiag_out(c_re).astype(BF16)
    cim = bdiag_out(-c_im).astype(BF16)

    pw = [(ab_re.reshape(-1), ab_im.reshape(-1))]
    for _ in range(SUBLANES - 1):
        pr, pi = pw[-1]
        pw.append((pr * pw[0][0] - pi * pw[0][1], pr * pw[0][1] + pi * pw[0][0]))
    rows = jnp.arange(SUBLANES)[:, None]
    sc = []
    for k in (1, 2, 4):
        sc.append(jnp.where(rows >= k, pw[k - 1][0][None, :], 0.0))
        sc.append(jnp.where(rows >= k, pw[k - 1][1][None, :], 0.0))
    sc.append(jnp.stack([pw[r][0] for r in range(SUBLANES)]))
    sc.append(jnp.stack([pw[r][1] for r in range(SUBLANES)]))
    return bmat, cre, cim, jnp.stack(sc).astype(F32)


def _mid_kernel(x_ref, a_ref, s_ref, wo_ref, g_ref, wq_ref, wk_ref, wvt_ref, wg_ref, wqi_ref, wki_ref, wwt_ref,
                qg_ref, kg_ref, h_ref, q_ref, k_ref, vt_ref, sg_ref, qi_ref, ki_ref, wt_ref):
    aw = a_ref.shape[2]
    h = x_ref[0] + _mm(a_ref[0], wo_ref[0:aw, :]) + _mm(s_ref[0], wo_ref[aw:, :])
    h_ref[0] = h
    hn = _rms(h, g_ref[...]).astype(BF16)
    qf = _mm(hn, wq_ref[...])
    for hd in range(C_HEADS):
        sl = slice(hd * C_HEAD_DIM, (hd + 1) * C_HEAD_DIM)
        q_ref[0, :, sl] = (_rms(qf[:, sl], qg_ref[...]) * (C_HEAD_DIM ** -0.5)).astype(BF16)
    kf = _mm(hn, wk_ref[...])
    for hd in range(C_KV_HEADS):
        sl = slice(hd * C_HEAD_DIM, (hd + 1) * C_HEAD_DIM)
        k_ref[0, :, sl] = _rms(kf[:, sl], kg_ref[...]).astype(BF16)
    vt_ref[0] = _mm_nt(wvt_ref[...], hn).astype(BF16)
    sg_ref[0] = _silu(_mm(hn, wg_ref[...])).astype(BF16)
    qi_ref[0] = _mm(hn, wqi_ref[...]).astype(BF16)
    ki_ref[0] = _mm(hn, wki_ref[...]).astype(BF16)
    wt_ref[0] = _mm_nt(wwt_ref[...], hn) * ((IDX_HEADS ** -0.5) * (IDX_DIM ** -0.5))


def _mid(x, att0, ssm0, wo, g, wq, wk, wvt, wg, wqi, wki2, wwt, qg, kg):
    b, l, d = x.shape
    t = ROW_TILE

    def row(n):
        return pl.BlockSpec((1, t, n), lambda bb, i: (bb, i, 0))

    def col(n):
        return pl.BlockSpec((1, n, t), lambda bb, i: (bb, 0, i))

    weights = [wo, g, wq, wk, wvt, wg, wqi, wki2, wwt, qg, kg]
    cw = C_HEADS * C_HEAD_DIM
    ckv = C_KV_HEADS * C_HEAD_DIM
    return pl.pallas_call(
        _mid_kernel,
        grid=(b, l // t),
        in_specs=[row(d), row(att0.shape[2]), row(ssm0.shape[2])] + [_const_spec(w.shape) for w in weights],
        out_specs=[row(d), row(cw), row(ckv), col(ckv), row(cw), row(IDX_HEADS * IDX_DIM), row(2 * LANES),
                   col(IDX_HEADS)],
        out_shape=[jax.ShapeDtypeStruct((b, l, d), F32), jax.ShapeDtypeStruct((b, l, cw), BF16),
                   jax.ShapeDtypeStruct((b, l, ckv), BF16), jax.ShapeDtypeStruct((b, ckv, l), BF16),
                   jax.ShapeDtypeStruct((b, l, cw), BF16), jax.ShapeDtypeStruct((b, l, IDX_HEADS * IDX_DIM), BF16),
                   jax.ShapeDtypeStruct((b, l, 2 * LANES), BF16), jax.ShapeDtypeStruct((b, IDX_HEADS, l), F32)],
        compiler_params=_params("arbitrary", "arbitrary"),
        name="mid",
    )(x, att0, ssm0, *weights)


def _dsa_kernel(q_ref, qi_ref, wt_ref, sg_ref, k_ref, vt_ref, ki_ref, tab_ref, lb_ref, o_ref,
                sc_ref, best_ref, x_ref, acc_ref, ring_a_ref, ring_b_ref, mb_ref, *, topk):
    i = pl.program_id(1)
    ck = KEY_CHUNK
    per = ck // BLOCK
    nch = (i + per) // per
    t_row = i * BLOCK + lax.broadcasted_iota(I32, (1, LANES), 1)
    kiota = lax.broadcasted_iota(I32, (ck, LANES), 0)

    def chunk_off(c):
        return pl.multiple_of(c * ck, ck)

    qi = qi_ref[0]
    qi_stack = [jnp.concatenate([qi[:, (2 * s) * LANES:(2 * s + 1) * LANES],
                                 qi[:, (2 * s + 1) * LANES:(2 * s + 2) * LANES]], axis=0) for s in range(2)]
    wt = wt_ref[0]

    def score_chunk(c, masked):
        off = chunk_off(c)
        sc = jnp.zeros((ck, LANES), F32)
        for a in range(2):
            kk = ki_ref[0, pl.ds(off, ck), a * LANES:(a + 1) * LANES]
            for s in range(2):
                r = _mm_nt(kk, qi_stack[s])
                for j in range(2):
                    hd = 2 * (2 * s + j) + a
                    sc = sc + jnp.maximum(r[:, j * LANES:(j + 1) * LANES], 0.0) * wt[hd:hd + 1, :]
        if masked:
            sc = jnp.where(off + kiota <= t_row, sc, NEG_INF)
        sc_ref[pl.ds(off, ck), :] = sc
        return _fold(sc, jnp.maximum)

    def exchange(v, a, b):
        v[a], v[b] = jnp.maximum(v[a], v[b]), jnp.minimum(v[a], v[b])

    def sort_desc(v):
        n, k = len(v), 2
        while k <= n:
            j = k // 2
            while j >= 1:
                for a in range(n):
                    b = a ^ j
                    if b > a:
                        exchange(v, *((a, b) if (a & k) == 0 else (b, a)))
                j //= 2
            k *= 2

    def merge_top(best, blk):
        n = len(best)
        v = [jnp.maximum(best[r], blk[n - 1 - r]) for r in range(n)]
        j = n // 2
        while j >= 1:
            for a in range(n):
                if a ^ j > a:
                    exchange(v, a, a ^ j)
            j //= 2
        return v

    crow = STREAMS * CAND * SUBLANES

    def cand_step(row0):
        blk_all = sc_ref[pl.ds(pl.multiple_of(row0, crow), crow), :]
        for st in range(STREAMS):
            blk = [blk_all[(STREAMS * r + st) * SUBLANES:(STREAMS * r + st + 1) * SUBLANES] for r in range(CAND)]
            sort_desc(blk)
            base = st * CAND * SUBLANES
            best = [best_ref[base + r * SUBLANES:base + (r + 1) * SUBLANES, :] for r in range(CAND)]
            for r, x in enumerate(merge_top(best, blk)):
                best_ref[base + r * SUBLANES:base + (r + 1) * SUBLANES, :] = x

    def score_body(c, mx):
        return jnp.maximum(jnp.maximum(mx, score_chunk(2 * c, False)), score_chunk(2 * c + 1, False))

    below = nch - 1
    smax = lax.fori_loop(0, below // 2, score_body, jnp.full((SUBLANES, LANES), NEG_INF, F32))
    smax = lax.cond(below % 2 == 1, lambda: jnp.maximum(smax, score_chunk(below - 1, False)), lambda: smax)
    smax = jnp.max(jnp.maximum(smax, score_chunk(nch - 1, True)), axis=0, keepdims=True)

    def count(*preds):
        rows = COUNT_ROWS
        sub = ck // rows

        def body(c, accs):
            out = []
            for u in range(sub):
                off = pl.multiple_of(c * ck + u * rows, rows)
                s = sc_ref[pl.ds(off, rows), :]
                for n, pred in enumerate(preds):
                    ind = pred(s, off).astype(I32)
                    out.append(accs[u * len(preds) + n]
                               + jnp.sum(ind.reshape(rows // SUBLANES, SUBLANES, LANES), axis=0))
            return tuple(out)

        accs = lax.fori_loop(0, nch, body, tuple(jnp.zeros((SUBLANES, LANES), I32) for _ in range(sub * len(preds))))
        res = [jnp.sum(sum(accs[n::len(preds)]), axis=0, keepdims=True) for n in range(len(preds))]
        return res[0] if len(preds) == 1 else res

    def key_value(k):
        return pltpu.bitcast(k ^ ((k >> 31) & MAGNITUDE_BITS), F32)

    def count_ge(k):
        thr = key_value(k)
        return count(lambda s, off: s >= thr)

    def full(v):
        return jnp.full((1, LANES), v, I32)

    def bisect(_, st):
        lo, hi, c_lo, c_hi = st
        mid = (lo >> 1) + (hi >> 1) + (lo & hi & 1)
        c = count_ge(mid)
        ge = c >= topk
        return jnp.where(ge, mid, lo), jnp.where(ge, hi, mid), jnp.where(ge, c, c_lo), jnp.where(ge, c_hi, c)

    searching = (i + 1) * BLOCK > topk

    def float_key(x):
        bits = pltpu.bitcast(x, I32)
        return bits ^ ((bits >> 31) & MAGNITUDE_BITS)

    def search():
        k_lo = float_key(smax * 2.0 ** -BRACKET_BINADES)
        c = count_ge(k_lo)
        ok = (smax > 0.0) & (c >= topk)
        trips = jnp.where(jnp.min(jnp.where(ok, 1.0, 0.0)) > 0.0, BRACKET_STEPS, FULL_STEPS)
        st = (jnp.where(ok, k_lo, KEY_NEG_INF), float_key(smax) + 1, jnp.where(ok, c, nch * ck), full(0))
        out = lax.fori_loop(0, trips, bisect, st)
        return out[0], out[2], out[3]

    def cand_search():
        best_ref[...] = jnp.full(best_ref.shape, -jnp.inf, F32)

        def cand_body(c, _):
            cand_step(c * crow)
            return 0

        lax.fori_loop(0, (i + crow // BLOCK) // (crow // BLOCK), cand_body, 0)

        def all_sublanes(x):
            for shift in (4, 2, 1):
                x = x + pltpu.roll(x, shift, axis=0)
            return x

        def count_cand(k):
            thr = key_value(k)[None]
            parts = [jnp.sum((best_ref[r:r + COUNT_ROWS, :].reshape(COUNT_ROWS // SUBLANES, SUBLANES, LANES)
                              >= thr).astype(I32), axis=0) for r in range(0, crow, COUNT_ROWS)]
            return all_sublanes(sum(parts))

        def step(_, st):
            lo, hi = st
            mid = (lo >> 1) + (hi >> 1) + (lo & hi & 1)
            take = ~((count_cand(mid) - topk) >> 31)
            return (mid & take) | (lo & ~take), (hi & take) | (mid & ~take)

        smax8 = jnp.broadcast_to(smax, (SUBLANES, LANES))
        k_lo = float_key(smax8 * 2.0 ** -BRACKET_BINADES)
        ok = (smax8 > 0.0) & (count_cand(k_lo) >= topk)
        trips = jnp.where(jnp.min(jnp.where(ok, 1.0, 0.0)) > 0.0, BRACKET_STEPS, FULL_STEPS)
        vk8, _ = lax.fori_loop(0, trips, step, (jnp.where(ok, k_lo, KEY_NEG_INF), float_key(smax8) + 1))
        vk = vk8[0:1]
        thr = key_value(vk)
        above_cand = jnp.sum((best_ref[...] > thr).astype(I32), axis=0, keepdims=True)
        c_ge, c_gt = count(lambda s, off: s >= thr, lambda s, off: s > thr)
        complete = jnp.min(jnp.where(c_gt == above_cand, 1.0, 0.0)) > 0.0
        return lax.cond(complete, lambda: (vk, c_ge, c_gt), search)

    vkey, c_lo, c_hi = lax.cond(searching, cand_search, lambda: (full(KEY_NEG_INF), full(topk), full(0)))
    vthr = key_value(vkey)
    need = topk - c_hi
    ties = c_lo - c_hi

    def tie_search():
        def split_step():
            nxt = vkey + 1
            nxt = jnp.where((nxt > 0) & (nxt < KEY_MIN_NORMAL), KEY_MIN_NORMAL, nxt)
            step = key_value(nxt) - vthr

            def split(_, st):
                fl, fh = st
                fm = 0.5 * (fl + fh)
                t = vthr + fm * step
                ge = count(lambda s, off: s >= t) >= topk
                return jnp.where(ge, fm, fl), jnp.where(ge, fh, fm)

            fl, _ = lax.fori_loop(0, SPLIT_STEPS, split, (jnp.zeros((1, LANES), F32), jnp.ones((1, LANES), F32)))
            t = vthr + fl * step
            return t, topk - count(lambda s, off: s > t)

        inside = jnp.max(jnp.where(ties > count(lambda s, off: s == vthr), 1.0, 0.0)) > 0.0
        thr, want = lax.cond(inside, split_step, lambda: (vthr, need))

        rr = lax.broadcasted_iota(I32, (BLOCK, BLOCK), 0)
        cc = lax.broadcasted_iota(I32, (BLOCK, BLOCK), 1)
        tril = jnp.where(cc <= rr, 1.0, 0.0).astype(BF16)
        want_f = want.astype(F32)

        def body(c, before):
            off = chunk_off(c)
            blocks = [sc_ref[pl.ds(pl.multiple_of(off + r * BLOCK, BLOCK), BLOCK), :] for r in range(per)]
            hits = [s == thr for s in blocks]
            ranks = [_mm(tril, jnp.where(h, 1.0, 0.0).astype(BF16)) for h in hits]
            for r in range(per):
                rank = ranks[r] + before
                sc_ref[pl.ds(pl.multiple_of(off + r * BLOCK, BLOCK), BLOCK), :] = jnp.where(
                    hits[r] & (rank > want_f), NEG_INF, blocks[r])
                before = rank[BLOCK - 1:BLOCK, :]
            return before

        lax.fori_loop(0, nch, body, jnp.zeros((1, LANES), F32))
        return thr

    any_tie = searching & (jnp.max(jnp.where(ties > need, 1.0, 0.0)) > 0.0)
    vthr = lax.cond(any_tie, tie_search, lambda: vthr)

    def selection_mask(off):
        s = sc_ref[pl.ds(off, ck), :]
        s_idx = off + kiota
        sel = (s >= vthr) & (s_idx <= t_row)
        madd = jnp.where(sel, 0.0, NEG_INF)
        sc_ref[pl.ds(off, ck), :] = madd
        return madd

    q = q_ref[0]
    n_far = jnp.maximum((i - NEAR_BLOCKS + 1) // per, 0)
    hpg = C_HEADS // C_KV_HEADS
    npair = C_HEADS // 2
    q_pairs = [jnp.concatenate([q[:, (2 * j) * LANES:(2 * j + 1) * LANES],
                                q[:, (2 * j + 1) * LANES:(2 * j + 2) * LANES]], axis=0) for j in range(npair)]

    def bias_rows(hd, c):
        return jnp.concatenate([tab_ref[hd, jnp.clip(i - (c * per + r), 0, NEAR_BLOCKS)] for r in range(per)],
                               axis=0)

    def emit(hd, num, den):
        sl = slice(hd * LANES, (hd + 1) * LANES)
        o_ref[0, :, sl] = ((num / den).T * sg_ref[0, :, sl].astype(F32)).astype(BF16)

    def exact_attention():
        for g in range(C_KV_HEADS):
            def stage_body(near, g=g):
                def body(c, mx):
                    off = chunk_off(c)
                    madd = sc_ref[pl.ds(off, ck), :]
                    kc = k_ref[0, pl.ds(off, ck), g * LANES:(g + 1) * LANES]
                    out = []
                    for jj in range(hpg // 2):
                        lg = _mm_nt(kc, q_pairs[g * (hpg // 2) + jj])
                        for a in range(2):
                            hl = 2 * jj + a
                            x = lg[:, a * LANES:(a + 1) * LANES] + madd
                            if near:
                                x = x + bias_rows(hpg * g + hl, c)
                            x_ref[hl, pl.ds(off, ck), :] = x
                            out.append(jnp.maximum(mx[hl], _fold(x, jnp.maximum)))
                    return tuple(out)

                return body

            mx = tuple(jnp.full((SUBLANES, LANES), NEG_INF, F32) for _ in range(hpg))
            mx = lax.fori_loop(0, n_far, stage_body(False), mx)
            mx = lax.fori_loop(n_far, nch, stage_body(True), mx)
            m = [jnp.max(v, axis=0, keepdims=True) for v in mx]
            acc_ref[...] = jnp.zeros(acc_ref.shape, F32)

            def att_body(c, ls, g=g, m=m):
                off = chunk_off(c)
                vt = vt_ref[0, g * LANES:(g + 1) * LANES, pl.ds(off, ck)]
                out = []
                for jj in range(hpg // 2):
                    ps = []
                    for a in range(2):
                        hl = 2 * jj + a
                        p = jnp.exp(x_ref[hl, pl.ds(off, ck), :] - m[hl])
                        out.append(ls[hl] + _fold(p, jnp.add))
                        ps.append(p.astype(BF16))
                    acc_ref[jj] += _mm(vt, jnp.concatenate(ps, axis=1))
                return tuple(out)

            ls = lax.fori_loop(0, nch, att_body, tuple(jnp.zeros((SUBLANES, LANES), F32) for _ in range(hpg)))
            for hl in range(hpg):
                emit(hpg * g + hl, acc_ref[hl // 2, :, (hl % 2) * LANES:(hl % 2 + 1) * LANES],
                     jnp.sum(ls[hl], axis=0, keepdims=True))

    lb = lb_ref[...]
    acc_ref[...] = jnp.zeros(acc_ref.shape, F32)

    def stage_mask(c):
        mb_ref[...] = selection_mask(chunk_off(c)) - lb

    def stage_pair(c, j, ring_ref, near):
        g = j // (hpg // 2)
        kc = k_ref[0, pl.ds(chunk_off(c), ck), g * LANES:(g + 1) * LANES]
        lg = _mm_nt(kc, q_pairs[j])
        for a in range(2):
            hd = 2 * j + a
            x = lg[:, a * LANES:(a + 1) * LANES] + mb_ref[...]
            if near:
                x = x + bias_rows(hd, c)
            ring_ref[hd] = x

    def consume_pair(c, j, ring_ref, ls):
        g = j // (hpg // 2)
        vt = vt_ref[0, g * LANES:(g + 1) * LANES, pl.ds(chunk_off(c), ck)]
        ps = []
        for a in range(2):
            hd = 2 * j + a
            p = jnp.exp(ring_ref[hd])
            ls[hd] = ls[hd] + _fold(p, jnp.add)
            ps.append(p.astype(BF16))
        acc_ref[j] += _mm(vt, jnp.concatenate(ps, axis=1))

    def stage(c, ring_ref):
        stage_mask(c)
        for j in range(npair):
            stage_pair(c, j, ring_ref, True)

    def consume(c, ring_ref, ls, then_stage=None, final=False):
        ls = list(ls)
        if then_stage is not None:
            stage_mask(then_stage[0])
        for j in range(npair):
            if then_stage is not None:
                stage_pair(then_stage[0], j, *then_stage[1:])
            consume_pair(c, j, ring_ref, ls)
            if final:
                for a in range(2):
                    emit(2 * j + a, acc_ref[j, :, a * LANES:(a + 1) * LANES],
                         jnp.sum(ls[2 * j + a], axis=0, keepdims=True))
        return tuple(ls)

    def pair_step(near):
        def body(k, ls):
            c = 2 * k
            ls = consume(c, ring_a_ref, ls, (c + 1, ring_b_ref, near))
            return consume(c + 1, ring_b_ref, ls, (c + 2, ring_a_ref, near))

        return body

    stage(0, ring_a_ref)
    ls = tuple(jnp.zeros((SUBLANES, LANES), F32) for _ in range(C_HEADS))
    npairs = (nch - 1) // 2
    far_pairs = jnp.maximum((n_far - 1) // 2, 0)
    ls = lax.fori_loop(0, far_pairs, pair_step(False), ls)
    ls = lax.fori_loop(far_pairs, npairs, pair_step(True), ls)
    last = 2 * npairs

    def tail_two(ls):
        ls = consume(last, ring_a_ref, ls, (last + 1, ring_b_ref, True))
        return consume(last + 1, ring_b_ref, ls, final=True)

    ls = lax.cond(nch - 1 > last, tail_two, lambda ls: consume(last, ring_a_ref, ls, final=True), ls)
    dens = [jnp.sum(v, axis=0, keepdims=True) for v in ls]
    in_range = jnp.min(functools.reduce(jnp.minimum, dens)) > SUM_FLOOR

    @pl.when(jnp.logical_not(in_range))
    def _():
        exact_attention()


def _dsa(q, qi, wt, sg, k, vt, ki2, tab, lb):
    b, l, cw = q.shape
    nb = l // BLOCK
    topk = min(TOPK_MAX, l // 4)

    def blk(n):
        return pl.BlockSpec((1, BLOCK, n), lambda bb, i: (bb, i, 0))

    def whole(s1, s2):
        return pl.BlockSpec((1, s1, s2), lambda bb, i: (bb, 0, 0), pipeline_mode=pl.Buffered(1))

    hpg = C_HEADS // C_KV_HEADS
    return pl.pallas_call(
        functools.partial(_dsa_kernel, topk=topk),
        grid=(b, nb),
        in_specs=[blk(cw), blk(qi.shape[2]), pl.BlockSpec((1, IDX_HEADS, BLOCK), lambda bb, i: (bb, 0, i)), blk(cw),
                  whole(l, k.shape[2]), whole(vt.shape[1], l), whole(l, ki2.shape[2]),
                  pl.BlockSpec(tab.shape, lambda bb, i: (0, 0, 0, 0), pipeline_mode=pl.Buffered(1)),
                  _const_spec(lb.shape)],
        out_specs=blk(cw),
        out_shape=jax.ShapeDtypeStruct((b, l, cw), BF16),
        scratch_shapes=[pltpu.VMEM((l, LANES), F32), pltpu.VMEM((STREAMS * CAND * SUBLANES, LANES), F32),
                        pltpu.VMEM((hpg, l, LANES), F32),
                        pltpu.VMEM((C_HEADS // 2, C_HEAD_DIM, 2 * LANES), F32),
                        pltpu.VMEM((C_HEADS, KEY_CHUNK, LANES), F32), pltpu.VMEM((C_HEADS, KEY_CHUNK, LANES), F32),
                        pltpu.VMEM((KEY_CHUNK, LANES), F32)],
        compiler_params=_params("arbitrary", "arbitrary"),
        name="dsa",
    )(q, qi, wt, sg, k, vt, ki2, tab, lb)


def _out_kernel(h_ref, a_ref, w_ref, o_ref):
    o_ref[...] = h_ref[...] + _mm(a_ref[...], w_ref[...])


def _outproj(h2, a2, w):
    rows, d = h2.shape
    t = ROW_TILE
    return pl.pallas_call(
        _out_kernel,
        grid=(rows // t,),
        in_specs=[pl.BlockSpec((t, d), lambda i: (i, 0)), pl.BlockSpec((t, a2.shape[1]), lambda i: (i, 0)),
                  _const_spec(w.shape)],
        out_specs=pl.BlockSpec((t, d), lambda i: (i, 0)),
        out_shape=jax.ShapeDtypeStruct((rows, d), F32),
        compiler_params=_params("arbitrary"),
        name="outproj1",
    )(h2, a2, w)


def _bias_tables(rel_bias):
    nv = (NEAR_BLOCKS + 1) * BLOCK
    vec = rel_bias[_t5_bucket(jnp.arange(nv, dtype=I32))].astype(F32).T

    nh = vec.shape[0]
    vecp = jnp.concatenate([jnp.broadcast_to(vec[:, :1], (nh, BLOCK - 1)), vec], axis=1)

    def toeplitz(g, rows, cols):
        w = rows + cols
        g2 = jnp.concatenate([g[..., rows - 1:rows - 1 + cols], g[..., :1], g[..., :rows - 1]], axis=-1)
        flat = jnp.tile(g2, (1,) * (g.ndim - 1) + (rows,))[..., :rows * (w - 1)]
        return flat.reshape(g.shape[:-1] + (rows, w - 1))[..., :cols]

    bias0 = jnp.transpose(toeplitz(vecp[:, :3 * BLOCK - 1], 2 * BLOCK, BLOCK), (0, 2, 1))
    r = vecp[:, :(NEAR_BLOCKS + 1) * BLOCK].reshape(nh, NEAR_BLOCKS + 1, BLOCK)
    wins = jnp.concatenate([r[:, :-1], r[:, 1:, :BLOCK - 1]], axis=2)
    tab = toeplitz(wins, BLOCK, BLOCK) - rel_bias[NUM_BUCKETS - 1].astype(F32)[:, None, None, None]
    tab = jnp.concatenate([tab, jnp.zeros((tab.shape[0], 1, BLOCK, BLOCK), F32)], axis=1)
    return bias0, tab


def kernel(x, rel_bias, norm_g, ev_w_in, ev_w_out, ev_q_norm_g, ev_k_norm_g, ev_sinks, ev_ssm_log_dt, ev_ssm_a_re,
           ev_ssm_a_im, ev_ssm_b_re, ev_ssm_b_im, ev_ssm_c_re, ev_ssm_c_im, ev_ssm_d, ev_glu_w, ev_glu_b, od_w_in,
           od_w_out, od_q_norm_g, od_k_norm_g):
    b, l, d = x.shape
    assert l % KEY_CHUNK == 0 and l % ROW_TILE == 0
    assert (NEAR_BLOCKS - 1) * BLOCK + 1 >= 16 * 64 ** (15 / 16) + 1
    bias0, tab = _bias_tables(rel_bias)

    w0 = ev_w_in[0]
    hd = A_HEAD_DIM
    o0 = np.cumsum([0, A_WIDTH, A_KV_HEADS * hd, A_KV_HEADS * hd, A_WIDTH, A_WIDTH, A_WIDTH])
    wq0, wk0, wv0, wga, wu, wgb = (w0[:, o0[n]:o0[n + 1]] for n in range(6))
    z = jnp.zeros((d, hd), w0.dtype)

    def variants(w):
        return jnp.concatenate([c for g in range(A_KV_HEADS) for c in (w[:, g * hd:(g + 1) * hd], z, z,
                                                                       w[:, g * hd:(g + 1) * hd])], axis=1)

    w0x = jnp.concatenate([wq0, variants(wk0), variants(wv0), wga, wu, wgb], axis=1).astype(BF16)
    qg2 = jnp.tile(ev_q_norm_g[0], 2)[None, :]
    kg2 = jnp.tile(ev_k_norm_g[0], 2)[None, :]
    q0, k0, v0, sga, u, sgb = _proj0(x.reshape(b * l, d), norm_g[0][None, :], w0x, qg2, kg2)
    shp = lambda a: a.reshape(b, l, a.shape[-1])
    sinks = jnp.broadcast_to(ev_sinks[0][:, None], (A_HEADS, LANES)).astype(F32)
    att0 = _attn0(shp(q0), shp(k0), shp(v0), shp(sga), bias0, sinks)
    bmat, cre, cim, sc = _s5_prep(ev_ssm_log_dt[0], ev_ssm_a_re[0], ev_ssm_a_im[0], ev_ssm_b_re[0], ev_ssm_b_im[0],
                                  ev_ssm_c_re[0], ev_ssm_c_im[0])
    ssm0 = _ssm(shp(u), shp(sgb), bmat, cre, cim, sc, ev_ssm_d[0].reshape(1, -1), ev_glu_w[0].astype(BF16),
                ev_glu_b[0][None, :])

    w1 = od_w_in[0]
    cw = C_HEADS * C_HEAD_DIM
    ckv = C_KV_HEADS * C_HEAD_DIM
    o = np.cumsum([0, cw, ckv, ckv, cw, IDX_HEADS * IDX_DIM, IDX_DIM, IDX_HEADS])
    wq, wk, wv, wg, wqi, wki, ww = (w1[:, o[n]:o[n + 1]] for n in range(7))
    zki = jnp.zeros((d, LANES - IDX_DIM), w1.dtype)
    wki2 = jnp.concatenate([wki, zki, zki, wki], axis=1)
    bf = lambda a: a.astype(BF16)
    h1, q1, k1, vt1, sg1, qi1, ki2, wt1 = _mid(
        x, att0, ssm0, bf(ev_w_out[0]), norm_g[1][None, :], bf(wq), bf(wk), bf(wv.T), bf(wg), bf(wqi), bf(wki2),
        bf(ww.T), od_q_norm_g[0][None, :], od_k_norm_g[0][None, :])
    lb = (BF16_SLACK * C_HEAD_DIM ** 0.5 * jnp.max(jnp.abs(od_q_norm_g[0])) * jnp.max(jnp.abs(od_k_norm_g[0]))
          + jnp.max(tab))
    att1 = _dsa(q1, qi1, wt1, sg1, k1, vt1, ki2, tab, jnp.full((1, LANES), lb, F32))
    out = _outproj(h1.reshape(b * l, d), att1.reshape(b * l, cw), bf(od_w_out[0]))
    return out.reshape(b, l, d)
```
